```python
import math
import jax, jax.numpy as jnp
from jax import lax
import numpy as np

D_MODEL = 1024
BATCH = 1
SEQ = 16384
DEPTH = 1

PLE_DIM = 256
HEAD_DIM = 64
NSA_HEADS = 8
NSA_KV_GROUPS = 2
NSA_GROUP_SIZE = NSA_HEADS // NSA_KV_GROUPS
NSA_CMP_LEN = 32
NSA_CMP_STRIDE = 16
NSA_SLC_BLOCK = 64
NSA_SLC_TOPK = 16
NSA_LOCAL_BLOCKS = 2
NSA_WINDOW = 512
DSA_HEADS = 8
IDX_HEADS = 8
IDX_DIM = 32
DSA_TOPK_MAX = 256
Q_BLOCK = 128
REL_BUCKETS = 32
REL_MAX_EXACT = 16
REL_MAX_DIST = 128
N_EXPERTS = 32
TOP_K = 4
D_FF = 1024
SWIGLU_LIMIT = 7.0
SWIGLU_ALPHA = 1.702
MOE_ROWS = 256
EPS = 1e-6
NEG_INF = -1e30

NSA_W = NSA_HEADS * HEAD_DIM
DSA_W = DSA_HEADS * HEAD_DIM
IN_SPLITS = (NSA_W, 6 * NSA_KV_GROUPS * HEAD_DIM, 3 * NSA_HEADS, DSA_W, DSA_W, DSA_W,
             IDX_HEADS * IDX_DIM, IDX_DIM, IDX_HEADS, D_MODEL, D_MODEL)
D_IN = sum(IN_SPLITS)

kernel_name = "hybrid_nsa_dsa_moe_block"


def rmsnorm(x, g):
    xf = x.astype(jnp.float32)
    y = xf * lax.rsqrt(jnp.mean(xf * xf, axis=-1, keepdims=True) + EPS)
    return (y * g.astype(jnp.float32)).astype(x.dtype)


def rel_bucket(dist):
    n = jnp.maximum(dist, 0)
    nf = jnp.maximum(n, 1).astype(jnp.float32)
    large = REL_MAX_EXACT + (jnp.log(nf / REL_MAX_EXACT) / math.log(REL_MAX_DIST / REL_MAX_EXACT)
                             * (REL_BUCKETS - REL_MAX_EXACT)).astype(jnp.int32)
    large = jnp.minimum(large, REL_BUCKETS - 1)
    return jnp.where(n < REL_MAX_EXACT, n, large)


def masked_softmax(logits, mask):
    logits = jnp.where(mask, logits.astype(jnp.float32), NEG_INF)
    m = jnp.max(logits, axis=-1, keepdims=True)
    e = jnp.where(mask, jnp.exp(logits - m), 0.0)
    return e / jnp.maximum(jnp.sum(e, axis=-1, keepdims=True), 1e-30)


def split_cols(z, sizes):
    out, o = [], 0
    for s in sizes:
        out.append(z[..., o:o + s])
        o += s
    return out


def nsa_mixer(q, kv, gate_logits, pe_ck, w_ck1, w_ck2, pe_cv, w_cv1, w_cv2, g_q, g_k, rel_tab):
    B, S, _ = q.shape
    G, R, dh = NSA_KV_GROUPS, NSA_GROUP_SIZE, HEAD_DIM
    q = rmsnorm(q.reshape(B, S, G, R, dh), g_q) * (dh ** -0.5)
    kv = kv.reshape(B, S, 6, G, dh)
    k_cmp_raw, v_cmp_raw, k_slc, v_slc, k_win, v_win = [kv[:, :, j] for j in range(6)]

    n_cmp = (S - NSA_CMP_LEN) // NSA_CMP_STRIDE + 1
    cmp_start = jnp.arange(n_cmp) * NSA_CMP_STRIDE
    cmp_tok = cmp_start[:, None] + jnp.arange(NSA_CMP_LEN)[None, :]
    cmp_end = cmp_start + NSA_CMP_LEN - 1

    def compress(z, pe, w1, w2):
        zb = z[:, cmp_tok] + pe[None, None, :, None, :]
        hdn = jax.nn.silu(jnp.einsum('bclgd,lde->bcge', zb, w1))
        return jnp.einsum('bcge,ed->bcgd', hdn, w2)

    k_cmp = rmsnorm(compress(k_cmp_raw, pe_ck, w_ck1, w_ck2), g_k)
    v_cmp = compress(v_cmp_raw, pe_cv, w_cv1, w_cv2)

    n_slc = S // NSA_SLC_BLOCK
    slc_start = jnp.arange(n_slc) * NSA_SLC_BLOCK
    overlap = jnp.maximum(
        jnp.minimum(cmp_start[:, None] + NSA_CMP_LEN, slc_start[None, :] + NSA_SLC_BLOCK)
        - jnp.maximum(cmp_start[:, None], slc_start[None, :]), 0).astype(jnp.float32) / NSA_CMP_LEN
    n_sel = min(NSA_SLC_TOPK, n_slc)
    n_ar = jnp.arange(n_slc)

    k_slc = rmsnorm(k_slc, g_k).transpose(0, 2, 1, 3)
    v_slc = v_slc.transpose(0, 2, 1, 3)
    pad = ((0, 0), (0, 0), (NSA_WINDOW, 0), (0, 0))
    k_win = jnp.pad(rmsnorm(k_win, g_k).transpose(0, 2, 1, 3), pad)
    v_win = jnp.pad(v_win.transpose(0, 2, 1, 3), pad)
    gates = jax.nn.sigmoid(gate_logits.astype(jnp.float32)).astype(q.dtype).reshape(B, S, G, R, 3)

    tab = rel_tab.reshape(REL_BUCKETS, G, R).transpose(1, 2, 0)
    b_ix = jnp.arange(B)[:, None, None, None]
    g_ix = jnp.arange(G)[None, :, None, None]
    g5 = jnp.arange(G)[None, :, None, None, None]
    r5 = jnp.arange(R)[None, None, :, None, None]

    def block(blk):
        t0 = blk * Q_BLOCK
        t = t0 + jnp.arange(Q_BLOCK)
        qb = lax.dynamic_slice_in_dim(q, t0, Q_BLOCK, axis=1)
        gb = lax.dynamic_slice_in_dim(gates, t0, Q_BLOCK, axis=1)

        mask_c = cmp_end[None, :] <= t[:, None]
        lc = (jnp.einsum('btgrd,bcgd->bgrtc', qb, k_cmp).astype(jnp.float32)
              + tab[:, :, rel_bucket(t[:, None] - cmp_end[None, :])])
        pc = masked_softmax(lc, mask_c)
        o_c = jnp.einsum('bgrtc,bcgd->btgrd', pc.astype(v_cmp.dtype), v_cmp)

        imp = jnp.einsum('bgrtc,cn->bgtn', pc, overlap)
        cur = t // NSA_SLC_BLOCK
        valid = slc_start[None, :] <= t[:, None]
        forced = valid & ((n_ar[None, :] == 0) | (n_ar[None, :] > cur[:, None] - NSA_LOCAL_BLOCKS))
        score = jnp.where(forced, jnp.inf, jnp.where(valid, imp, -jnp.inf))
        _, sel = lax.top_k(score, n_sel)
        tok = (sel[..., None] * NSA_SLC_BLOCK + jnp.arange(NSA_SLC_BLOCK)).reshape(
            B, G, Q_BLOCK, n_sel * NSA_SLC_BLOCK)
        ks = k_slc[b_ix, g_ix, tok]
        vs = v_slc[b_ix, g_ix, tok]
        ls = (jnp.einsum('btgrd,bgtkd->bgrtk', qb, ks).astype(jnp.float32)
              + tab[g5, r5, rel_bucket(t[:, None] - tok)[:, :, None]])
        ps = masked_softmax(ls, (tok <= t[:, None])[:, :, None])
        o_s = jnp.einsum('bgrtk,bgtkd->btgrd', ps.astype(vs.dtype), vs)

        kw = lax.dynamic_slice_in_dim(k_win, t0, Q_BLOCK + NSA_WINDOW, axis=2)
        vw = lax.dynamic_slice_in_dim(v_win, t0, Q_BLOCK + NSA_WINDOW, axis=2)
        s = t0 - NSA_WINDOW + jnp.arange(Q_BLOCK + NSA_WINDOW)
        d = t[:, None] - s[None, :]
        mask_w = (d >= 0) & (d < NSA_WINDOW) & (s[None, :] >= 0)
        lw = jnp.einsum('btgrd,bgsd->bgrts', qb, kw).astype(jnp.float32) + tab[:, :, rel_bucket(d)]
        pw = masked_softmax(lw, mask_w)
        o_w = jnp.einsum('bgrts,bgsd->btgrd', pw.astype(vw.dtype), vw)

        o = gb[..., 0:1] * o_c + gb[..., 1:2] * o_s + gb[..., 2:3] * o_w
        return o.reshape(B, Q_BLOCK, NSA_W)

    out = lax.map(block, jnp.arange(S // Q_BLOCK))
    return out.transpose(1, 0, 2, 3).reshape(B, S, NSA_W)


def dsa_mixer(q, k, v, iq, ik, iw, g_q, g_k, rel_tab):
    B, S, _ = q.shape
    H, dh = DSA_HEADS, HEAD_DIM
    q = rmsnorm(q.reshape(B, S, H, dh), g_q) * (dh ** -0.5)
    k = rmsnorm(k.reshape(B, S, H, dh), g_k)
    v = v.reshape(B, S, H, dh)
    iq = iq.reshape(B, S, IDX_HEADS, IDX_DIM) * (IDX_DIM ** -0.5)
    iw = iw * (IDX_HEADS ** -0.5)
    n_keep = min(DSA_TOPK_MAX, S // 4)
    s_ar = jnp.arange(S)
    b_ix = jnp.arange(B)[:, None, None]

    def block(blk):
        t0 = blk * Q_BLOCK
        t = t0 + jnp.arange(Q_BLOCK)
        qi = lax.dynamic_slice_in_dim(iq, t0, Q_BLOCK, axis=1)
        wi = lax.dynamic_slice_in_dim(iw, t0, Q_BLOCK, axis=1)
        idx_score = jnp.einsum('bth,bths->bts', wi.astype(jnp.float32),
                               jax.nn.relu(jnp.einsum('bthd,bsd->bths', qi, ik).astype(jnp.float32)))
        idx_score = jnp.where(s_ar[None, :] <= t[:, None], idx_score, -jnp.inf)
        _, sel = lax.top_k(idx_score, n_keep)
        kg = k[b_ix, sel]
        vg = v[b_ix, sel]
        qb = lax.dynamic_slice_in_dim(q, t0, Q_BLOCK, axis=1)
        dist = t[:, None] - sel
        logits = (jnp.einsum('bthd,btkhd->bhtk', qb, kg).astype(jnp.float32)
                  + jnp.moveaxis(rel_tab[rel_bucket(dist)], -1, 1))
        pr = masked_softmax(logits, (dist >= 0)[:, None])
        o = jnp.einsum('bhtk,btkhd->bthd', pr.astype(vg.dtype), vg)
        return o.reshape(B, Q_BLOCK, DSA_W)

    out = lax.map(block, jnp.arange(S // Q_BLOCK))
    return out.transpose(1, 0, 2, 3).reshape(B, S, DSA_W)


def moe_ffn(h, w_r, b_r, w1, b1, w2, b2):
    B, S, D = h.shape
    N = B * S
    hf = h.reshape(N, D)
    logits = (hf @ w_r + b_r).astype(jnp.float32)
    top_val, top_idx = lax.top_k(logits, TOP_K)
    top_w = jax.nn.softmax(top_val, axis=-1)
    e_flat = top_idx.reshape(-1)
    tok_flat = jnp.repeat(jnp.arange(N, dtype=jnp.int32), TOP_K)
    w_flat = top_w.reshape(-1)
    order = jnp.argsort(e_flat)
    e_sorted, tok_sorted, w_sorted = e_flat[order], tok_flat[order], w_flat[order]
    counts = jnp.bincount(e_flat, length=N_EXPERTS)
    padded = (counts + MOE_ROWS - 1) // MOE_ROWS * MOE_ROWS
    pad_end = jnp.cumsum(padded)
    pad_start = pad_end - padded
    start = jnp.cumsum(counts) - counts
    dest = pad_start[e_sorted] + jnp.arange(N * TOP_K) - start[e_sorted]
    n_chunks = -(-(N * TOP_K) // MOE_ROWS) + N_EXPERTS
    P = n_chunks * MOE_ROWS
    row_tok = jnp.full((P,), N, jnp.int32).at[dest].set(tok_sorted)
    row_w = jnp.zeros((P,), jnp.float32).at[dest].set(w_sorted)
    chunk_e = jnp.minimum(jnp.searchsorted(pad_end, jnp.arange(n_chunks) * MOE_ROWS, side='right'),
                          N_EXPERTS - 1)
    h_pad = jnp.concatenate([hf, jnp.zeros((1, D), hf.dtype)], axis=0)

    def expert_rows(args):
        tok, e = args
        xe = h_pad[tok]
        gu = xe @ w1[e] + b1[e]
        gate = jnp.minimum(gu[:, :D_FF], SWIGLU_LIMIT)
        lin = jnp.clip(gu[:, D_FF:], -SWIGLU_LIMIT, SWIGLU_LIMIT)
        act = (lin + 1) * gate * jax.nn.sigmoid(SWIGLU_ALPHA * gate)
        return act @ w2[e] + b2[e]

    rows = lax.map(expert_rows, (row_tok.reshape(n_chunks, MOE_ROWS), chunk_e))
    rows = rows.reshape(P, D) * row_w[:, None].astype(rows.dtype)
    y = jax.ops.segment_sum(rows, row_tok, num_segments=N + 1)[:N]
    return y.reshape(B, S, D)


def setup_inputs(seed: int = 0) -> dict:
    key = jax.random.key(seed)
    ks = jax.random.split(key, 32)
    f32 = jnp.float32

    def nrm(k, shape, fan_in):
        return jax.random.normal(k, shape, f32) * (fan_in ** -0.5)

    def gain(k, shape):
        return 1.0 + 0.1 * jax.random.normal(k, shape, f32)

    L, dh = DEPTH, HEAD_DIM
    return {
        "x": jax.random.normal(ks[0], (BATCH, SEQ, D_MODEL), f32),
        "p": jax.random.normal(ks[1], (DEPTH, BATCH, SEQ, PLE_DIM), f32),
        "w_in": nrm(ks[2], (L, D_MODEL, D_IN), D_MODEL),
        "g_mix": gain(ks[3], (L, D_MODEL)),
        "g_q_nsa": gain(ks[4], (L, dh)),
        "g_k_nsa": gain(ks[5], (L, dh)),
        "g_q_dsa": gain(ks[6], (L, dh)),
        "g_k_dsa": gain(ks[7], (L, dh)),
        "pe_ck": 0.5 * jax.random.normal(ks[8], (L, NSA_CMP_LEN, dh), f32),
        "w_ck1": nrm(ks[9], (L, NSA_CMP_LEN, dh, dh), NSA_CMP_LEN * dh),
        "w_ck2": nrm(ks[10], (L, dh, dh), dh),
        "pe_cv": 0.5 * jax.random.normal(ks[11], (L, NSA_CMP_LEN, dh), f32),
        "w_cv1": nrm(ks[12], (L, NSA_CMP_LEN, dh, dh), NSA_CMP_LEN * dh),
        "w_cv2": nrm(ks[13], (L, dh, dh), dh),
        "w_branch_a": nrm(ks[14], (L, NSA_W, D_MODEL), NSA_W),
        "w_branch_b": nrm(ks[15], (L, DSA_W, D_MODEL), DSA_W),
        "w_out": nrm(ks[16], (L, D_MODEL, D_MODEL), D_MODEL),
        "rel_bias": 0.5 * jax.random.normal(ks[17], (REL_BUCKETS, NSA_HEADS + DSA_HEADS), f32),
        "g_ffn": gain(ks[18], (L, D_MODEL)),
        "w_router": nrm(ks[19], (L, D_MODEL, N_EXPERTS), D_MODEL),
        "b_router": 0.01 * jax.random.normal(ks[20], (L, N_EXPERTS), f32),
        "w_e1": nrm(ks[21], (L, N_EXPERTS, D_MODEL, 2 * D_FF), D_MODEL),
        "b_e1": 0.01 * jax.random.normal(ks[22], (L, N_EXPERTS, 2 * D_FF), f32),
        "w_e2": nrm(ks[23], (L, N_EXPERTS, D_FF, D_MODEL), D_FF),
        "b_e2": 0.01 * jax.random.normal(ks[24], (L, N_EXPERTS, D_MODEL), f32),
        "g_ple": gain(ks[25], (L, D_MODEL)),
        "w_ple": nrm(ks[26], (L, PLE_DIM, D_MODEL), PLE_DIM),
        "w_ple_gate": nrm(ks[27], (L, D_MODEL, D_MODEL), D_MODEL),
    }


def reference(x, p, w_in, g_mix, g_q_nsa, g_k_nsa, g_q_dsa, g_k_dsa, pe_ck, w_ck1, w_ck2,
              pe_cv, w_cv1, w_cv2, w_branch_a, w_branch_b, w_out, rel_bias, g_ffn, w_router,
              b_router, w_e1, b_e1, w_e2, b_e2, g_ple, w_ple, w_ple_gate):
    rel_nsa = rel_bias[:, :NSA_HEADS]
    rel_dsa = rel_bias[:, NSA_HEADS:]
    for i in range(DEPTH):
        h = rmsnorm(x, g_mix[i])
        z = h @ w_in[i]
        nq, nkv, ngate, dq, dk, dv, iq, ik, iw, ga, gb = split_cols(z, IN_SPLITS)
        o_a = nsa_mixer(nq, nkv, ngate, pe_ck[i], w_ck1[i], w_ck2[i], pe_cv[i], w_cv1[i], w_cv2[i],
                        g_q_nsa[i], g_k_nsa[i], rel_nsa)
        o_b = dsa_mixer(dq, dk, dv, iq, ik, iw, g_q_dsa[i], g_k_dsa[i], rel_dsa)
        merged = (jax.nn.sigmoid(ga) * (o_a @ w_branch_a[i])
                  + jax.nn.sigmoid(gb) * (o_b @ w_branch_b[i]))
        x = x + merged @ w_out[i]
        x = x + moe_ffn(rmsnorm(x, g_ffn[i]), w_router[i], b_router[i],
                        w_e1[i], b_e1[i], w_e2[i], b_e2[i])
        x = x + jax.nn.sigmoid(rmsnorm(x, g_ple[i]) @ w_ple_gate[i]) * (p[i] @ w_ple[i])
    return x
```

```python
import functools
import math

import numpy as np
import jax
import jax.numpy as jnp
from jax import lax
from jax.experimental import pallas as pl
from jax.experimental.pallas import tpu as pltpu

F32 = jnp.float32
BF16 = jnp.bfloat16
I32 = jnp.int32

D_MODEL = 1024
PLE_DIM = 256
HEAD_DIM = 64
NSA_HEADS = 8
NSA_KV_GROUPS = 2
NSA_GROUP_SIZE = NSA_HEADS // NSA_KV_GROUPS
NSA_CMP_LEN = 32
NSA_CMP_STRIDE = 16
NSA_SLC_BLOCK = 64
NSA_SLC_TOPK = 16
NSA_LOCAL_BLOCKS = 2
NSA_WINDOW = 512
DSA_HEADS = 8
IDX_HEADS = 8
IDX_DIM = 32
DSA_TOPK_MAX = 256
Q_BLOCK = 128
REL_BUCKETS = 32
REL_MAX_EXACT = 16
REL_MAX_DIST = 128
N_EXPERTS = 32
TOP_K = 4
D_FF = 1024
SWIGLU_LIMIT = 7.0
SWIGLU_ALPHA = 1.702
MOE_ROWS = 256
EPS = 1e-6

NSA_W = NSA_HEADS * HEAD_DIM
DSA_W = DSA_HEADS * HEAD_DIM

LANES = 128
KEY_TILE = 512
NEAR = 2 * Q_BLOCK
NSA_PADF = NSA_WINDOW
DSA_PADF = Q_BLOCK
VMEM_LIMIT = 56 * 1024 * 1024

OFF_NQ, OFF_CMP, OFF_KSLC, OFF_VSLC, OFF_KWIN, OFF_VWIN = 0, 512, 768, 896, 1024, 1152
OFF_DQ, OFF_DK, OFF_DV, OFF_IQ, OFF_GAB, OFF_SMALL = 1280, 1792, 2304, 2816, 3072, 5120
D_IN_PAD = 5248
SMALL_GATE, SMALL_IK, SMALL_IW = 0, 24, 56


def _rel_bucket_table(n):
    d = np.arange(n)
    nf = np.maximum(d, 1).astype(np.float64)
    large = REL_MAX_EXACT + (np.log(nf / REL_MAX_EXACT) / math.log(REL_MAX_DIST / REL_MAX_EXACT)
                             * (REL_BUCKETS - REL_MAX_EXACT)).astype(np.int64)
    large = np.minimum(large, REL_BUCKETS - 1)
    return np.where(d < REL_MAX_EXACT, d, large)


def _cparams(sem, vmem=VMEM_LIMIT):
    return pltpu.CompilerParams(dimension_semantics=sem, vmem_limit_bytes=vmem)


def _dot(a, b):
    return jnp.dot(a, b, preferred_element_type=F32)


def _dot_nt(a, b):
    return lax.dot_general(a, b, (((1,), (1,)), ((), ())), preferred_element_type=F32)


def _inproj_body(x_ref, gmix_ref, w_ref, gains_ref,
                 nq_ref, cmp_ref, kslc_ref, vslc_ref, kwin_ref, vwin_ref,
                 dq_ref, dk_ref, dv_ref, iq_ref, gab_ref, small_ref):
    x = x_ref[...]
    h = x * lax.rsqrt(jnp.mean(x * x, axis=-1, keepdims=True) + EPS) * gmix_ref[...]
    hb = h.astype(BF16)
    lo = lax.broadcasted_iota(I32, (1, LANES), 1) < HEAD_DIM

    def mm(off, n):
        return _dot(hb, w_ref[:, off:off + n])

    def headnorm(y, g):
        y2 = y * y
        s_lo = jnp.sum(jnp.where(lo, y2, 0.0), axis=-1, keepdims=True)
        s_hi = jnp.sum(jnp.where(lo, 0.0, y2), axis=-1, keepdims=True)
        r = jnp.where(lo, lax.rsqrt(s_lo * (1.0 / HEAD_DIM) + EPS), lax.rsqrt(s_hi * (1.0 / HEAD_DIM) + EPS))
        return y * r * g

    g_qn, g_kn, g_qd, g_kd = (gains_ref[k:k + 1, :] for k in range(4))
    for c in range(NSA_W // LANES):
        nq_ref[:, c * LANES:(c + 1) * LANES] = headnorm(mm(OFF_NQ + c * LANES, LANES), g_qn).astype(BF16)
    cmp_ref[...] = mm(OFF_CMP, 256).astype(BF16)
    kslc_ref[...] = headnorm(mm(OFF_KSLC, LANES), g_kn).astype(BF16)
    vslc_ref[...] = mm(OFF_VSLC, LANES).astype(BF16)
    kwin_ref[...] = headnorm(mm(OFF_KWIN, LANES), g_kn).astype(BF16)
    vwin_ref[...] = mm(OFF_VWIN, LANES).astype(BF16)
    for c in range(DSA_W // LANES):
        dq_ref[:, c * LANES:(c + 1) * LANES] = headnorm(mm(OFF_DQ + c * LANES, LANES), g_qd).astype(BF16)
        dk_ref[:, c * LANES:(c + 1) * LANES] = headnorm(mm(OFF_DK + c * LANES, LANES), g_kd).astype(BF16)
    dv_ref[...] = mm(OFF_DV, DSA_W).astype(BF16)
    iq_ref[...] = (mm(OFF_IQ, IDX_HEADS * IDX_DIM) * (IDX_DIM ** -0.5)).astype(BF16)
    gab_ref[...] = jax.nn.sigmoid(mm(OFF_GAB, 2 * D_MODEL))
    small_ref[...] = mm(OFF_SMALL, LANES)


def _inproj(x2, g_mix, w_perm, gains, tm=256):
    S = x2.shape[0]
    row = lambda n: pl.BlockSpec((tm, n), lambda i: (i, 0))
    full = lambda a: pl.BlockSpec(a.shape, lambda i: (0,) * a.ndim)
    widths = [(NSA_W, BF16), (256, BF16), (LANES, BF16), (LANES, BF16), (LANES, BF16), (LANES, BF16),
              (DSA_W, BF16), (DSA_W, BF16), (DSA_W, BF16), (IDX_HEADS * IDX_DIM, BF16),
              (2 * D_MODEL, F32), (LANES, F32)]
    return pl.pallas_call(
        _inproj_body,
        grid=(S // tm,),
        in_specs=[row(D_MODEL), full(g_mix), full(w_perm), full(gains)],
        out_specs=[row(n) for n, _ in widths],
        out_shape=[jax.ShapeDtypeStruct((S, n), dt) for n, dt in widths],
        compiler_params=_cparams(("arbitrary",)),
        name="inproj",
    )(x2, g_mix, w_perm, gains)


def _compress_body(xa_ref, xb_ref, w1a_ref, w1b_ref, w1f_ref, pe_ref, w2_ref, gk_ref, o_ref):
    is_k = pl.program_id(0) < NSA_KV_GROUPS
    pe_term = jnp.sum(pe_ref[...] * w1f_ref[...], axis=0, keepdims=True)
    pre = _dot(xa_ref[...], w1a_ref[...]) + _dot(xb_ref[...], w1b_ref[...]) + pe_term
    hdn = pre * jax.nn.sigmoid(pre)
    out = _dot(hdn.astype(BF16), w2_ref[...])
    normed = out * lax.rsqrt(jnp.mean(out * out, axis=-1, keepdims=True) + EPS) * gk_ref[...]
    o_ref[...] = jnp.where(is_k, normed, out).astype(BF16)


def _compress(xa, xb, w1a, w1b, w1f, pe, w2, gk):
    n4, ncp, kdim = xa.shape
    G = NSA_KV_GROUPS
    kv = lambda i: i // G
    return pl.pallas_call(
        _compress_body,
        grid=(n4,),
        in_specs=[pl.BlockSpec((None, ncp, kdim), lambda i: (i, 0, 0)),
                  pl.BlockSpec((None, ncp, kdim), lambda i: (i, 0, 0)),
                  pl.BlockSpec((None, kdim, HEAD_DIM), lambda i: (kv(i), 0, 0)),
                  pl.BlockSpec((None, kdim, HEAD_DIM), lambda i: (kv(i), 0, 0)),
                  pl.BlockSpec((None, 2 * kdim, HEAD_DIM), lambda i: (kv(i), 0, 0)),
                  pl.BlockSpec((None, 2 * kdim, 1), lambda i: (kv(i), 0, 0)),
                  pl.BlockSpec((None, HEAD_DIM, HEAD_DIM), lambda i: (kv(i), 0, 0)),
                  pl.BlockSpec((1, HEAD_DIM), lambda i: (0, 0))],
        out_specs=pl.BlockSpec((None, ncp, HEAD_DIM), lambda i: (i, 0, 0)),
        out_shape=jax.ShapeDtypeStruct((n4, ncp, HEAD_DIM), BF16),
        compiler_params=_cparams(("arbitrary",)),
        name="compress",
    )(xa, xb, w1a, w1b, w1f, pe, w2, gk)


def _cmp_body(q_ref, kc_ref, vc_ref, bt_ref, oc_ref, sel_ref, *, NS):
    i = pl.program_id(0)
    t0 = i * Q_BLOCK
    NCP = 4 * NS
    R = NSA_GROUP_SIZE
    lane = lax.broadcasted_iota(I32, (1, NCP), 1)
    plane = lane // NS
    nblk = lane - plane * NS
    cend = NSA_SLC_BLOCK * nblk + NSA_CMP_STRIDE * plane + (NSA_CMP_LEN - 1)
    trow = t0 + lax.broadcasted_iota(I32, (Q_BLOCK, 1), 0)
    mask = cend <= trow
    q = q_ref[...]
    kc = kc_ref[...]
    vc = vc_ref[...]
    psum = jnp.zeros((Q_BLOCK, NCP), F32)
    for r in range(R):
        s = _dot_nt(q[:, r * HEAD_DIM:(r + 1) * HEAD_DIM], kc)
        bt = bt_ref[r]
        for col in range(16):
            j, k = col // 4, col % 4
            n = 2 * i + 1 - k
            pos = jnp.where(n >= 0, j * NS + n, -1)
            s = s + jnp.where(lane == pos, bt[:, col:col + 1], 0.0)
        s = jnp.where(mask, s, -jnp.inf)
        m = jnp.maximum(jnp.max(s, axis=-1, keepdims=True), -1e30)
        e = jnp.exp(s - m)
        pc = e / jnp.maximum(jnp.sum(e, axis=-1, keepdims=True), 1e-30)
        oc_ref[:, r * HEAD_DIM:(r + 1) * HEAD_DIM] = _dot(pc.astype(BF16), vc)
        psum = psum + pc
    p0, p1, p2, p3 = (psum[:, j * NS:(j + 1) * NS] for j in range(4))
    n_ar = lax.broadcasted_iota(I32, (1, NS), 1)
    p3s = jnp.where(n_ar == 0, 0.0, pltpu.roll(p3, 1, axis=1))
    imp = p0 + p1 + p2 + 0.5 * p3 + 0.5 * p3s
    cur = trow // NSA_SLC_BLOCK
    valid = NSA_SLC_BLOCK * n_ar <= trow
    forced = valid & ((n_ar == 0) | (n_ar > cur - NSA_LOCAL_BLOCKS))
    score = jnp.where(forced, 1e30, jnp.where(valid, imp, -1.0))
    sel = jnp.zeros((Q_BLOCK, NS), jnp.bool_)
    for _ in range(min(NSA_SLC_TOPK, NS)):
        mx = jnp.max(score, axis=-1, keepdims=True)
        first = jnp.min(jnp.where(score == mx, n_ar, NS), axis=-1, keepdims=True)
        pick = n_ar == first
        sel = sel | pick
        score = jnp.where(pick, -2.0, score)
    selb = jnp.where(sel, 1.0, 0.0).astype(BF16)
    nw = sel_ref.shape[0]
    if NS < LANES:
        selb = jnp.concatenate([selb, jnp.zeros((Q_BLOCK, LANES - NS), BF16)], axis=1)
    for w in range(nw):
        sel_ref[w] = selb[:, w * LANES:(w + 1) * LANES]


def _cmp(nq, kvc, bt, S):
    NS = S // NSA_SLC_BLOCK
    NCP = 4 * NS
    NW = max(NS // LANES, 1)
    G, R = NSA_KV_GROUPS, NSA_GROUP_SIZE
    return pl.pallas_call(
        functools.partial(_cmp_body, NS=NS),
        grid=(S // Q_BLOCK, G),
        in_specs=[pl.BlockSpec((Q_BLOCK, R * HEAD_DIM), lambda i, g: (i, g)),
                  pl.BlockSpec((None, NCP, HEAD_DIM), lambda i, g: (g, 0, 0)),
                  pl.BlockSpec((None, NCP, HEAD_DIM), lambda i, g: (G + g, 0, 0)),
                  pl.BlockSpec((R, Q_BLOCK, LANES), lambda i, g: (g, 0, 0))],
        out_specs=[pl.BlockSpec((Q_BLOCK, R * HEAD_DIM), lambda i, g: (i, g)),
                   pl.BlockSpec((None, NW, Q_BLOCK, LANES), lambda i, g: (g, 0, i, 0))],
        out_shape=[jax.ShapeDtypeStruct((S, NSA_W), F32),
                   jax.ShapeDtypeStruct((G, NW, S, LANES), BF16)],
        compiler_params=_cparams(("arbitrary", "arbitrary")),
        name="cmp",
    )(nq, kvc, kvc, bt)


def _flash_init(m_ref, l_ref, acc_ref):
    m_ref[...] = jnp.full(m_ref.shape, -1e30, F32)
    l_ref[...] = jnp.zeros(l_ref.shape, F32)
    acc_ref[...] = jnp.zeros(acc_ref.shape, F32)


def _flash_update(s, mask, v, m_ref, l_ref, acc_ref):
    s = jnp.where(mask, s, -jnp.inf)
    m_old = m_ref[...]
    m_new = jnp.maximum(m_old, jnp.max(s, axis=-1, keepdims=True))
    alpha = jnp.exp(m_old - m_new)
    p = jnp.exp(s - m_new)
    l_ref[...] = alpha * l_ref[...] + jnp.sum(p, axis=-1, keepdims=True)
    acc_ref[...] = alpha * acc_ref[...] + _dot(p.astype(BF16), v)
    m_ref[...] = m_new


def _slcwin_body(q_ref, ks_ref, vs_ref, kw_ref, vw_ref, sel_ref, tb_ref, oc_ref, gl_ref, o_ref,
                 ms_ref, ls_ref, as_ref, mw_ref, lw_ref, aw_ref):
    i = pl.program_id(0)
    t0 = i * Q_BLOCK
    R = NSA_GROUP_SIZE
    T = Q_BLOCK
    q = q_ref[...]
    q4 = jnp.concatenate([q[:, r * HEAD_DIM:(r + 1) * HEAD_DIM] for r in range(R)], axis=0)
    irow = lax.broadcasted_iota(I32, (T, 1), 0)
    rep = lambda mk: jnp.concatenate([mk] * R, axis=0)
    tb4 = tb_ref[...].reshape(R * T, NEAR)

    def expand(w, base, ncols):
        rowi = lax.broadcasted_iota(I32, (LANES, ncols), 0)
        coli = lax.broadcasted_iota(I32, (LANES, ncols), 1)
        e = jnp.where(rowi == base + coli // NSA_SLC_BLOCK, 1.0, 0.0).astype(BF16)
        return _dot(sel_ref[w], e)

    _flash_init(ms_ref, ls_ref, as_ref)
    far_end = t0 - Q_BLOCK
    nfar = jnp.maximum(far_end + KEY_TILE - 1, 0) // KEY_TILE
    blocks_per_tile = KEY_TILE // NSA_SLC_BLOCK

    def far(kt, carry):
        r0 = pl.multiple_of(NSA_PADF + kt * KEY_TILE, LANES)
        k = ks_ref[pl.ds(r0, KEY_TILE), :]
        v = vs_ref[pl.ds(r0, KEY_TILE), :]
        s = _dot_nt(q4, k)
        b0 = kt * blocks_per_tile
        mexp = expand(b0 // LANES, b0 % LANES, KEY_TILE)
        colabs = kt * KEY_TILE + lax.broadcasted_iota(I32, (1, KEY_TILE), 1)
        mask = (mexp > 0.5) & (colabs < far_end)
        _flash_update(s, rep(mask), v, ms_ref, ls_ref, as_ref)
        return carry

    lax.fori_loop(0, nfar, far, 0)

    coln = lax.broadcasted_iota(I32, (1, NEAR), 1)
    near_ok = (coln - Q_BLOCK <= irow) & (coln + t0 - Q_BLOCK >= 0)
    rn = pl.multiple_of(t0 + NSA_PADF - Q_BLOCK, LANES)
    bp = jnp.maximum(2 * i - 2, 0)
    bd = 2 * i
    mexp = jnp.concatenate([expand(bp // LANES, bp % LANES, Q_BLOCK),
                            expand(bd // LANES, bd % LANES, Q_BLOCK)], axis=1)
    s = _dot_nt(q4, ks_ref[pl.ds(rn, NEAR), :]) + tb4
    _flash_update(s, rep((mexp > 0.5) & near_ok), vs_ref[pl.ds(rn, NEAR), :], ms_ref, ls_ref, as_ref)
    o_s = as_ref[...] / jnp.maximum(ls_ref[...], 1e-30)

    _flash_init(mw_ref, lw_ref, aw_ref)
    WF = NSA_WINDOW - Q_BLOCK
    rw = pl.multiple_of(t0 + NSA_PADF - NSA_WINDOW, LANES)
    colw = lax.broadcasted_iota(I32, (1, WF), 1)
    maskw = (colw > irow) & (colw + t0 - NSA_WINDOW >= 0)
    s = _dot_nt(q4, kw_ref[pl.ds(rw, WF), :])
    _flash_update(s, rep(maskw), vw_ref[pl.ds(rw, WF), :], mw_ref, lw_ref, aw_ref)
    s = _dot_nt(q4, kw_ref[pl.ds(rn, NEAR), :]) + tb4
    _flash_update(s, rep(near_ok), vw_ref[pl.ds(rn, NEAR), :], mw_ref, lw_ref, aw_ref)
    o_w = aw_ref[...] / jnp.maximum(lw_ref[...], 1e-30)

    gates = jax.nn.sigmoid(gl_ref[...])
    oc = oc_ref[...]
    for r in range(R):
        o = (gates[:, 3 * r:3 * r + 1] * oc[:, r * HEAD_DIM:(r + 1) * HEAD_DIM]
             + gates[:, 3 * r + 1:3 * r + 2] * o_s[r * T:(r + 1) * T]
             + gates[:, 3 * r + 2:3 * r + 3] * o_w[r * T:(r + 1) * T])
        o_ref[:, r * HEAD_DIM:(r + 1) * HEAD_DIM] = o.astype(BF16)


def _slcwin(nq, ks, vs, kw, vw, sel, tb, oc, gl, S):
    G, R = NSA_KV_GROUPS, NSA_GROUP_SIZE
    spad = ks.shape[1]
    NW = sel.shape[1]
    kvspec = pl.BlockSpec((None, spad, HEAD_DIM), lambda i, g: (g, 0, 0))
    qspec = pl.BlockSpec((Q_BLOCK, R * HEAD_DIM), lambda i, g: (i, g))
    st = lambda n: pltpu.VMEM((R * Q_BLOCK, n), F32)
    return pl.pallas_call(
        _slcwin_body,
        grid=(S // Q_BLOCK, G),
        in_specs=[qspec, kvspec, kvspec, kvspec, kvspec,
                  pl.BlockSpec((None, NW, Q_BLOCK, LANES), lambda i, g: (g, 0, i, 0)),
                  pl.BlockSpec((R, Q_BLOCK, NEAR), lambda i, g: (g, 0, 0)),
                  qspec,
                  pl.BlockSpec((None, Q_BLOCK, LANES), lambda i, g: (g, i, 0))],
        out_specs=qspec,
        out_shape=jax.ShapeDtypeStruct((S, NSA_W), BF16),
        scratch_shapes=[st(1), st(1), st(HEAD_DIM), st(1), st(1), st(HEAD_DIM)],
        compiler_params=_cparams(("arbitrary", "arbitrary")),
        name="slcwin",
    )(nq, ks, vs, kw, vw, sel, tb, oc, gl)


INT_MIN = -2 ** 31
NEG_INF_KEY = int(np.array(-np.inf, np.float32).view(np.int32)) ^ 0x7FFFFFFF


def _dsa_body(dq_ref, dk_ref, dv_ref, iq_ref, ikt_ref, iw_ref, tb_ref, o_ref,
              keys_ref, m_ref, l_ref, acc_ref, *, S, KEEP):
    i = pl.program_id(0)
    t0 = i * Q_BLOCK
    T = Q_BLOCK
    H = DSA_HEADS
    ntile = (t0 + Q_BLOCK + KEY_TILE - 1) // KEY_TILE
    irow = lax.broadcasted_iota(I32, (T, 1), 0)
    trow = t0 + irow
    col512 = lax.broadcasted_iota(I32, (1, KEY_TILE), 1)
    ktile = lambda kt: keys_ref[:, pl.ds(pl.multiple_of(Q_BLOCK + kt * KEY_TILE, LANES), KEY_TILE)]

    keys_ref[:, 0:Q_BLOCK] = jnp.full((T, Q_BLOCK), NEG_INF_KEY, I32)
    iqv = iq_ref[...]
    iwv = iw_ref[...] * (IDX_HEADS ** -0.5)
    qis = [iqv[:, h * IDX_DIM:(h + 1) * IDX_DIM] for h in range(IDX_HEADS)]
    wcol = [iwv[:, SMALL_IW + h:SMALL_IW + h + 1] for h in range(IDX_HEADS)]

    def score_tile(kt, carry):
        c0 = pl.multiple_of(kt * KEY_TILE, KEY_TILE)
        ik = ikt_ref[:, pl.ds(c0, KEY_TILE)]
        sc = jnp.zeros((T, KEY_TILE), F32)
        for h in range(IDX_HEADS):
            sc = sc + wcol[h] * jnp.maximum(_dot(qis[h], ik), 0.0)
        sc = jnp.where(sc == 0.0, 0.0, sc)
        sc = jnp.where(c0 + col512 <= trow, sc, -jnp.inf)
        bits = pltpu.bitcast(sc, I32)
        keys_ref[:, pl.ds(pl.multiple_of(Q_BLOCK + c0, LANES), KEY_TILE)] = jnp.where(
            bits < 0, bits ^ 0x7FFFFFFF, bits)
        return carry

    lax.fori_loop(0, ntile, score_tile, 0)

    def count(pred):
        def body(kt, acc):
            hit = jnp.where(pred(ktile(kt), kt * KEY_TILE + col512), 1, 0)
            return acc + hit[:, 0:128] + hit[:, 128:256] + hit[:, 256:384] + hit[:, 384:512]
        acc = lax.fori_loop(0, ntile, body, jnp.zeros((T, LANES), I32))
        return jnp.sum(acc, axis=-1, keepdims=True)

    c_pos = count(lambda kk, ab: kk >= 0)
    nonneg = c_pos >= KEEP
    prefix = jnp.where(nonneg, 0, INT_MIN)
    cnt = jnp.where(nonneg, c_pos, KEEP)

    def bit_step(b, carry):
        prefix, cnt = carry
        cand = prefix + lax.shift_left(jnp.int32(1), 30 - b)
        c = count(lambda kk, ab: kk >= cand)
        ok = c >= KEEP
        return jnp.where(ok, cand, prefix), jnp.where(ok, c, cnt)

    thr, cnt_ge = lax.fori_loop(0, 31, bit_step, (prefix, cnt))
    cnt_gt = count(lambda kk, ab: kk > thr)
    need = KEEP - cnt_gt
    tie = (cnt_ge > KEEP) & (thr > NEG_INF_KEY)

    def tie_cut():
        nbits = int(S).bit_length()

        def step(b, j0):
            cand = j0 + lax.shift_left(jnp.int32(1), nbits - 1 - b)
            c = count(lambda kk, ab: (kk == thr) & (ab < cand))
            return jnp.where(c < need, cand, j0)

        j0 = lax.fori_loop(0, nbits, step, jnp.zeros((T, 1), I32))
        return jnp.where(tie, j0 + 1, S)

    jcut = lax.cond(jnp.max(jnp.where(tie, 1, 0)) > 0, tie_cut, lambda: jnp.full((T, 1), S, I32))

    def chosen(kk, ab):
        return (kk > thr) | ((kk == thr) & (ab < jcut))

    _flash_init(m_ref, l_ref, acc_ref)
    lo = lax.broadcasted_iota(I32, (1, LANES), 1) < HEAD_DIM
    dq = dq_ref[...]
    qpairs = []
    for pr in range(H // 2):
        qp = dq[:, pr * LANES:(pr + 1) * LANES]
        zero = jnp.zeros_like(qp)
        qpairs.append(jnp.concatenate([jnp.where(lo, qp, zero), jnp.where(lo, zero, qp)], axis=0))

    def attend(r0, ncols, mask, bias):
        mask2 = jnp.concatenate([mask, mask], axis=0)
        for pr in range(H // 2):
            kp = dk_ref[pl.ds(r0, ncols), pr * LANES:(pr + 1) * LANES]
            vp = dv_ref[pl.ds(r0, ncols), pr * LANES:(pr + 1) * LANES]
            s = _dot_nt(qpairs[pr], kp)
            if bias is not None:
                s = s + bias[2 * pr:2 * pr + 2].reshape(2 * T, ncols)
            s = jnp.where(mask2, s, -jnp.inf)
            m_old = m_ref[pr]
            m_new = jnp.maximum(m_old, jnp.max(s, axis=-1, keepdims=True))
            alpha = jnp.exp(m_old - m_new)
            p = jnp.exp(s - m_new)
            l_ref[pr] = alpha * l_ref[pr] + jnp.sum(p, axis=-1, keepdims=True)
            m_ref[pr] = m_new
            pv = _dot(p.astype(BF16), vp)
            a_lane = jnp.where(lo, alpha[0:T], alpha[T:2 * T])
            acc_ref[pr] = a_lane * acc_ref[pr] + jnp.where(lo, pv[0:T], pv[T:2 * T])

    far_end = t0 - Q_BLOCK
    nfar = jnp.maximum(far_end + KEY_TILE - 1, 0) // KEY_TILE

    def far(kt, carry):
        ab = kt * KEY_TILE + col512
        mask = chosen(ktile(kt), ab) & (ab < far_end)
        attend(pl.multiple_of(DSA_PADF + kt * KEY_TILE, LANES), KEY_TILE, mask, None)
        return carry

    lax.fori_loop(0, nfar, far, 0)

    coln = lax.broadcasted_iota(I32, (1, NEAR), 1)
    abn = coln + t0 - Q_BLOCK
    kkn = keys_ref[:, pl.ds(pl.multiple_of(t0, LANES), NEAR)]
    maskn = chosen(kkn, abn) & (abn <= trow) & (abn >= 0)
    attend(pl.multiple_of(t0 + DSA_PADF - Q_BLOCK, LANES), NEAR, maskn, tb_ref[...])

    for pr in range(H // 2):
        l = l_ref[pr]
        l_lane = jnp.where(lo, l[0:T], l[T:2 * T])
        o_ref[:, pr * LANES:(pr + 1) * LANES] = (acc_ref[pr] / jnp.maximum(l_lane, 1e-30)).astype(BF16)


def _dsa(dq, dkp, dvp, iq, ikt, small, tb, S):
    KEEP = min(DSA_TOPK_MAX, S // 4)
    H = DSA_HEADS
    T = Q_BLOCK
    once = pl.Buffered(1)
    full = lambda a: pl.BlockSpec(a.shape, lambda i: (0,) * a.ndim, pipeline_mode=once)
    return pl.pallas_call(
        functools.partial(_dsa_body, S=S, KEEP=KEEP),
        grid=(S // T,),
        in_specs=[pl.BlockSpec((T, DSA_W), lambda i: (i, 0)),
                  full(dkp), full(dvp),
                  pl.BlockSpec((T, IDX_HEADS * IDX_DIM), lambda i: (i, 0)),
                  full(ikt),
                  pl.BlockSpec((T, LANES), lambda i: (i, 0)),
                  full(tb)],
        out_specs=pl.BlockSpec((T, DSA_W), lambda i: (i, 0)),
        out_shape=jax.ShapeDtypeStruct((S, DSA_W), BF16),
        scratch_shapes=[pltpu.VMEM((T, Q_BLOCK + S), I32),
                        pltpu.VMEM((H // 2, 2 * T, 1), F32),
                        pltpu.VMEM((H // 2, 2 * T, 1), F32),
                        pltpu.VMEM((H // 2, T, LANES), F32)],
        compiler_params=_cparams(("arbitrary",)),
        name="dsa",
    )(dq, dkp, dvp, iq, ikt, small, tb)


def _split_bf16(a):
    hi = a.astype(BF16)
    return hi, (a - hi.astype(F32)).astype(BF16)


def _merge_body(x_ref, oa_ref, ob_ref, gab_ref, wa_ref, wb_ref, wo_ref, gffn_ref, wr_ref, br_ref,
                x1_ref, h2_ref, ridx_ref, rw_ref):
    a = _dot(oa_ref[...], wa_ref[...])
    b = _dot(ob_ref[...], wb_ref[...])
    gab = gab_ref[...]
    merged = gab[:, :D_MODEL] * a + gab[:, D_MODEL:] * b
    x1 = x_ref[...] + _dot(merged.astype(BF16), wo_ref[...])
    x1_ref[...] = x1
    h2 = x1 * lax.rsqrt(jnp.mean(x1 * x1, axis=-1, keepdims=True) + EPS) * gffn_ref[...]
    h2_ref[...] = h2
    hh, hl = _split_bf16(h2)
    wh, wl = _split_bf16(wr_ref[...])
    logits = _dot(hh, wh) + _dot(hh, wl) + _dot(hl, wh) + br_ref[...]
    lane = lax.broadcasted_iota(I32, (1, LANES), 1)
    idx_out = jnp.zeros(logits.shape, I32)
    val_out = jnp.full(logits.shape, -jnp.inf, F32)
    for k in range(TOP_K):
        mx = jnp.max(logits, axis=-1, keepdims=True)
        first = jnp.min(jnp.where(logits == mx, lane, LANES), axis=-1, keepdims=True)
        idx_out = jnp.where(lane == k, first, idx_out)
        val_out = jnp.where(lane == k, mx, val_out)
        logits = jnp.where(lane == first, -jnp.inf, logits)
    e = jnp.exp(val_out - jnp.max(val_out, axis=-1, keepdims=True))
    ridx_ref[...] = idx_out
    rw_ref[...] = e / jnp.sum(e, axis=-1, keepdims=True)


def _merge(x2, oa, ob, gab, wa, wb, wo, gffn, wr, br, tm=256):
    S = x2.shape[0]
    row = lambda n: pl.BlockSpec((tm, n), lambda i: (i, 0))
    full = lambda a: pl.BlockSpec(a.shape, lambda i: (0,) * a.ndim)
    return pl.pallas_call(
        _merge_body,
        grid=(S // tm,),
        in_specs=[row(D_MODEL), row(NSA_W), row(DSA_W), row(2 * D_MODEL),
                  full(wa), full(wb), full(wo), full(gffn), full(wr), full(br)],
        out_specs=[row(D_MODEL), row(D_MODEL), row(LANES), row(LANES)],
        out_shape=[jax.ShapeDtypeStruct((S, D_MODEL), F32), jax.ShapeDtypeStruct((S, D_MODEL), F32),
                   jax.ShapeDtypeStruct((S, LANES), I32), jax.ShapeDtypeStruct((S, LANES), F32)],
        compiler_params=_cparams(("arbitrary",)),
        name="merge",
    )(x2, oa, ob, gab, wa, wb, wo, gffn, wr, br)


def _rank_body(ridx_ref, rank_ref, cnt_ref, carry_ref):
    @pl.when(pl.program_id(0) == 0)
    def _():
        carry_ref[...] = jnp.zeros(carry_ref.shape, F32)

    ridx = ridx_ref[...]
    tm = ridx.shape[0]
    lane = lax.broadcasted_iota(I32, (1, LANES), 1)
    hits = [lane == ridx[:, k:k + 1] for k in range(TOP_K)]
    member = jnp.zeros((tm, LANES), F32)
    for hk in hits:
        member = member + jnp.where(hk, 1.0, 0.0)
    ri = lax.broadcasted_iota(I32, (tm, tm), 0)
    ci = lax.broadcasted_iota(I32, (tm, tm), 1)
    lower = jnp.where(ci < ri, 1.0, 0.0).astype(BF16)
    before = _dot(lower, member.astype(BF16)) + carry_ref[0:1, :]
    out = jnp.zeros((tm, LANES), I32)
    for k, hk in enumerate(hits):
        rk = jnp.sum(jnp.where(hk, before, 0.0), axis=-1, keepdims=True)
        out = jnp.where(lane == k, rk.astype(I32), out)
    rank_ref[...] = out
    total = carry_ref[0:1, :] + jnp.sum(member, axis=0, keepdims=True)
    carry_ref[...] = jnp.broadcast_to(total, carry_ref.shape)
    cnt_ref[...] = jnp.broadcast_to(total, cnt_ref.shape).astype(I32)


def _rank(ridx, tm=512):
    S = ridx.shape[0]
    return pl.pallas_call(
        _rank_body,
        grid=(S // tm,),
        in_specs=[pl.BlockSpec((tm, LANES), lambda i: (i, 0))],
        out_specs=[pl.BlockSpec((tm, LANES), lambda i: (i, 0)), pl.BlockSpec((8, LANES), lambda i: (0, 0))],
        out_shape=[jax.ShapeDtypeStruct((S, LANES), I32), jax.ShapeDtypeStruct((8, LANES), I32)],
        scratch_shapes=[pltpu.VMEM((8, LANES), F32)],
        compiler_params=_cparams(("arbitrary",)),
        name="rank",
    )(ridx)


def _dispatch_body(dest_ref, h_ref, zero_ref, xe_ref, sem):
    del zero_ref
    i = pl.program_id(0)
    tm = h_ref.shape[0]

    def copy(r, k):
        d = dest_ref[(i * tm + r) * TOP_K + k]
        return pltpu.make_async_copy(h_ref.at[pl.ds(r, 1)], xe_ref.at[pl.ds(d, 1)], sem)

    def start(r, c):
        for k in range(TOP_K):
            copy(r, k).start()
        return c

    def wait(r, c):
        for k in range(TOP_K):
            copy(r, k).wait()
        return c

    lax.fori_loop(0, tm, start, 0)
    lax.fori_loop(0, tm, wait, 0)


def _dispatch(dest, h2, zeros_p, tm=128):
    S = h2.shape[0]
    return pl.pallas_call(
        _dispatch_body,
        grid_spec=pltpu.PrefetchScalarGridSpec(
            num_scalar_prefetch=1,
            grid=(S // tm,),
            in_specs=[pl.BlockSpec((tm, D_MODEL), lambda i, d: (i, 0)),
                      pl.BlockSpec(memory_space=pl.ANY)],
            out_specs=pl.BlockSpec(memory_space=pl.ANY),
            scratch_shapes=[pltpu.SemaphoreType.DMA(())]),
        out_shape=jax.ShapeDtypeStruct(zeros_p.shape, zeros_p.dtype),
        input_output_aliases={2: 0},
        compiler_params=_cparams(("arbitrary",)),
        name="dispatch",
    )(dest, h2, zeros_p)


def _experts_body(ce_ref, nu_ref, x_ref, w1_ref, b1_ref, w2_ref, b2_ref, y_ref):
    c = pl.program_id(0)

    @pl.when(c < nu_ref[0])
    def _():
        gu = _dot(x_ref[...].astype(BF16), w1_ref[...]) + b1_ref[...]
        gate = jnp.minimum(gu[:, :D_FF], SWIGLU_LIMIT)
        lin = jnp.clip(gu[:, D_FF:], -SWIGLU_LIMIT, SWIGLU_LIMIT)
        act = (lin + 1.0) * gate * jax.nn.sigmoid(SWIGLU_ALPHA * gate)
        y_ref[...] = _dot(act.astype(BF16), w2_ref[...]) + b2_ref[...]

    @pl.when(c >= nu_ref[0])
    def _():
        y_ref[...] = jnp.zeros(y_ref.shape, F32)


def _experts(chunk_e, n_used, xe, w1, b1, w2, b2):
    P = xe.shape[0]
    nch = P // MOE_ROWS
    return pl.pallas_call(
        _experts_body,
        grid_spec=pltpu.PrefetchScalarGridSpec(
            num_scalar_prefetch=2,
            grid=(nch,),
            in_specs=[pl.BlockSpec((MOE_ROWS, D_MODEL), lambda c, ce, nu: (c, 0)),
                      pl.BlockSpec((None, D_MODEL, 2 * D_FF), lambda c, ce, nu: (ce[c], 0, 0)),
                      pl.BlockSpec((None, 1, 2 * D_FF), lambda c, ce, nu: (ce[c], 0, 0)),
                      pl.BlockSpec((None, D_FF, D_MODEL), lambda c, ce, nu: (ce[c], 0, 0)),
                      pl.BlockSpec((None, 1, D_MODEL), lambda c, ce, nu: (ce[c], 0, 0))],
            out_specs=pl.BlockSpec((MOE_ROWS, D_MODEL), lambda c, ce, nu: (c, 0))),
        out_shape=jax.ShapeDtypeStruct((P, D_MODEL), F32),
        compiler_params=_cparams(("arbitrary",)),
        name="experts",
    )(chunk_e, n_used, xe, w1, b1, w2, b2)


def _final_body(dest_ref, x1_ref, rw_ref, p_ref, gple_ref, wg_ref, wp_ref, y_ref, o_ref, rows_ref, sem):
    i = pl.program_id(0)
    tm = x1_ref.shape[0]

    def copy(r, k):
        d = dest_ref[(i * tm + r) * TOP_K + k]
        return pltpu.make_async_copy(y_ref.at[pl.ds(d, 1)], rows_ref.at[k, pl.ds(r, 1)], sem)

    def start(r, c):
        for k in range(TOP_K):
            copy(r, k).start()
        return c

    def wait(r, c):
        for k in range(TOP_K):
            copy(r, k).wait()
        return c

    lax.fori_loop(0, tm, start, 0)
    lax.fori_loop(0, tm, wait, 0)
    rw = rw_ref[...]
    x2 = x1_ref[...]
    for k in range(TOP_K):
        x2 = x2 + rw[:, k:k + 1] * rows_ref[k]
    hn = x2 * lax.rsqrt(jnp.mean(x2 * x2, axis=-1, keepdims=True) + EPS) * gple_ref[...]
    gate = jax.nn.sigmoid(_dot(hn.astype(BF16), wg_ref[...]))
    o_ref[...] = x2 + gate * _dot(p_ref[...].astype(BF16), wp_ref[...])


def _final(dest, x1, rw, p2, gple, wg, wp, yexp, tm=128):
    S = x1.shape[0]
    row = lambda n: pl.BlockSpec((tm, n), lambda i, d: (i, 0))
    full = lambda a: pl.BlockSpec(a.shape, lambda i, d: (0,) * a.ndim)
    return pl.pallas_call(
        _final_body,
        grid_spec=pltpu.PrefetchScalarGridSpec(
            num_scalar_prefetch=1,
            grid=(S // tm,),
            in_specs=[row(D_MODEL), row(LANES), row(PLE_DIM), full(gple), full(wg), full(wp),
                      pl.BlockSpec(memory_space=pl.ANY)],
            out_specs=row(D_MODEL),
            scratch_shapes=[pltpu.VMEM((TOP_K, tm, D_MODEL), F32), pltpu.SemaphoreType.DMA(())]),
        out_shape=jax.ShapeDtypeStruct((S, D_MODEL), F32),
        compiler_params=_cparams(("arbitrary",)),
        name="final",
    )(dest, x1, rw, p2, gple, wg, wp, yexp)


def _near_bias(rel_cols):
    bucket = _rel_bucket_table(NEAR)
    f = (rel_cols[bucket] - rel_cols[REL_BUCKETS - 1][None, :]).T
    d = np.arange(Q_BLOCK)[:, None] - np.arange(NEAR)[None, :] + Q_BLOCK
    return jnp.where(d >= 0, f[:, np.clip(d, 0, NEAR - 1)], 0.0)


def _cmp_bias(rel_cols):
    bucket = _rel_bucket_table(NEAR)
    f = (rel_cols[bucket] - rel_cols[REL_BUCKETS - 1][None, :]).T
    col = np.arange(16)
    d = (np.arange(Q_BLOCK)[:, None] - NSA_CMP_STRIDE * (col // 4)[None, :] - (NSA_CMP_LEN - 1)
         + NSA_SLC_BLOCK * ((col % 4)[None, :] - 1))
    b = jnp.where((d >= 0) & (d < NEAR), f[:, np.clip(d, 0, NEAR - 1)], 0.0)
    return jnp.pad(b, ((0, 0), (0, 0), (0, LANES - 16)))


def kernel(x, p, w_in, g_mix, g_q_nsa, g_k_nsa, g_q_dsa, g_k_dsa, pe_ck, w_ck1, w_ck2, pe_cv, w_cv1, w_cv2,
           w_branch_a, w_branch_b, w_out, rel_bias, g_ffn, w_router, b_router, w_e1, b_e1, w_e2, b_e2,
           g_ple, w_ple, w_ple_gate):
    B, S, D = x.shape
    assert B == 1 and D == D_MODEL and S % KEY_TILE == 0 and w_in.shape[0] == 1
    G, R = NSA_KV_GROUPS, NSA_GROUP_SIZE
    x2 = x.reshape(S, D)

    perm = np.concatenate([np.arange(0, 1280), np.arange(1304, 3096), np.arange(3136, 5184),
                           np.arange(1280, 1304), np.arange(3096, 3136)])
    w_perm = jnp.pad(w_in[0][:, perm], ((0, 0), (0, D_IN_PAD - perm.size))).astype(BF16)
    two = lambda g: jnp.tile(g.reshape(1, HEAD_DIM), (1, 2))
    gains = jnp.concatenate([two(g_q_nsa[0]) * HEAD_DIM ** -0.5, two(g_k_nsa[0]),
                             two(g_q_dsa[0]) * HEAD_DIM ** -0.5, two(g_k_dsa[0])], axis=0)
    (nq, cmpraw, kslc, vslc, kwin, vwin, dq, dk, dv, iq, gab, small) = _inproj(
        x2, g_mix[0].reshape(1, D), w_perm, gains)

    NS = S // NSA_SLC_BLOCK
    nchunk = S // NSA_CMP_STRIDE
    xc = cmpraw.reshape(nchunk, NSA_CMP_STRIDE, 2 * G, HEAD_DIM).transpose(2, 0, 1, 3).reshape(
        2 * G, nchunk, NSA_CMP_STRIDE * HEAD_DIM)
    planes = lambda a: a.reshape(2 * G, NS, 4, -1).transpose(0, 2, 1, 3).reshape(2 * G, nchunk, -1)
    xa = planes(xc)
    xb = planes(jnp.roll(xc, -1, axis=1))
    half = NSA_CMP_STRIDE * HEAD_DIM
    w1 = jnp.stack([w_ck1[0], w_cv1[0]]).reshape(2, 2 * half, HEAD_DIM)
    pe = jnp.stack([pe_ck[0], pe_cv[0]]).reshape(2, 2 * half, 1)
    w2 = jnp.stack([w_ck2[0], w_cv2[0]]).astype(BF16)
    kvc = _compress(xa, xb, w1[:, :half].astype(BF16), w1[:, half:].astype(BF16), w1, pe, w2,
                    g_k_nsa[0].reshape(1, HEAD_DIM))

    rel_nsa = rel_bias[:, :NSA_HEADS]
    rel_dsa = rel_bias[:, NSA_HEADS:]
    oc, sel = _cmp(nq, kvc, _cmp_bias(rel_nsa), S)
    groups = lambda a: jnp.pad(a.reshape(S, G, HEAD_DIM).transpose(1, 0, 2),
                               ((0, 0), (NSA_PADF, KEY_TILE), (0, 0)))
    gl = small[:, SMALL_GATE:SMALL_GATE + 3 * NSA_HEADS].reshape(S, G, 3 * R).transpose(1, 0, 2)
    gl = jnp.pad(gl, ((0, 0), (0, 0), (0, LANES - 3 * R)))
    o_a = _slcwin(nq, groups(kslc), groups(vslc), groups(kwin), groups(vwin), sel,
                  _near_bias(rel_nsa), oc, gl, S)

    padk = lambda a: jnp.pad(a, ((DSA_PADF, KEY_TILE - DSA_PADF), (0, 0)))
    ikt = small[:, SMALL_IK:SMALL_IK + IDX_DIM].T.astype(BF16)
    o_b = _dsa(dq, padk(dk), padk(dv), iq, ikt, small, _near_bias(rel_dsa), S)

    wr = jnp.pad(w_router[0], ((0, 0), (0, LANES - N_EXPERTS)))
    br = jnp.pad(b_router[0].reshape(1, N_EXPERTS), ((0, 0), (0, LANES - N_EXPERTS)), constant_values=-1e30)
    x1, h2, ridx, rw = _merge(x2, o_a, o_b, gab, w_branch_a[0].astype(BF16), w_branch_b[0].astype(BF16),
                              w_out[0].astype(BF16), g_ffn[0].reshape(1, D), wr, br)

    rank, cnt = _rank(ridx)
    counts = cnt[0, :N_EXPERTS]
    padded = (counts + MOE_ROWS - 1) // MOE_ROWS * MOE_ROWS
    pad_end = jnp.cumsum(padded)
    pad_start = pad_end - padded
    eidx = ridx[:, :TOP_K]
    dest = (jnp.sum(jnp.where(eidx[..., None] == jnp.arange(N_EXPERTS), pad_start, 0), axis=-1)
            + rank[:, :TOP_K]).reshape(-1).astype(I32)
    n_chunks = -(-(S * TOP_K) // MOE_ROWS) + N_EXPERTS
    chunk_start = jnp.arange(n_chunks, dtype=I32) * MOE_ROWS
    chunk_e = jnp.minimum(jnp.sum(chunk_start[:, None] >= pad_end[None, :], axis=-1), N_EXPERTS - 1).astype(I32)
    n_used = (pad_end[-1] // MOE_ROWS).astype(I32).reshape(1)
    xe = _dispatch(dest, h2, jnp.zeros((n_chunks * MOE_ROWS, D), F32))
    yexp = _experts(chunk_e, n_used, xe, w_e1[0].astype(BF16), b_e1[0].reshape(N_EXPERTS, 1, 2 * D_FF),
                    w_e2[0].astype(BF16), b_e2[0].reshape(N_EXPERTS, 1, D))

    out = _final(dest, x1, rw, p[0].reshape(S, PLE_DIM), g_ple[0].reshape(1, D),
                 w_ple_gate[0].astype(BF16), w_ple[0].astype(BF16), yexp)
    return out.reshape(B, S, D)
```

```python
import functools
import math

import numpy as np
import jax
import jax.numpy as jnp
from jax import lax
from jax.experimental import pallas as pl
from jax.experimental.pallas import tpu as pltpu

F32 = jnp.float32
BF16 = jnp.bfloat16
I32 = jnp.int32

D_MODEL = 1024
PLE_DIM = 256
HEAD_DIM = 64
NSA_HEADS = 8
NSA_KV_GROUPS = 2
NSA_GROUP_SIZE = NSA_HEADS // NSA_KV_GROUPS
NSA_CMP_LEN = 32
NSA_CMP_STRIDE = 16
NSA_SLC_BLOCK = 64
NSA_SLC_TOPK = 16
NSA_LOCAL_BLOCKS = 2
NSA_WINDOW = 512
DSA_HEADS = 8
IDX_HEADS = 8
IDX_DIM = 32
DSA_TOPK_MAX = 256
Q_BLOCK = 128
REL_BUCKETS = 32
REL_MAX_EXACT = 16
REL_MAX_DIST = 128
N_EXPERTS = 32
TOP_K = 4
D_FF = 1024
SWIGLU_LIMIT = 7.0
SWIGLU_ALPHA = 1.702
MOE_ROWS = 256
EPS = 1e-6
LOG2E = math.log2(math.e)

NSA_W = NSA_HEADS * HEAD_DIM
DSA_W = DSA_HEADS * HEAD_DIM

LANES = 128
SUBLANES = 8
KEY_TILE = 512
NEAR = 2 * Q_BLOCK
NSA_PADF = NSA_WINDOW
DSA_PADF = Q_BLOCK
TAIL_PAD = 3 * KEY_TILE
VMEM_LIMIT = 56 * 1024 * 1024

OFF_NQ, OFF_CMP, OFF_KSLC, OFF_VSLC, OFF_KWIN, OFF_VWIN = 0, 512, 768, 896, 1024, 1152
OFF_DQ, OFF_DK, OFF_DV, OFF_IQ, OFF_GAB, OFF_SMALL = 1280, 1792, 2304, 2816, 3072, 5120
D_IN_PAD = 5248
SMALL_GATE, SMALL_IK, SMALL_IW = 0, 24, 56


def _rel_bucket_table(n):
    d = np.arange(n)
    nf = np.maximum(d, 1).astype(np.float64)
    large = REL_MAX_EXACT + (np.log(nf / REL_MAX_EXACT) / math.log(REL_MAX_DIST / REL_MAX_EXACT)
                             * (REL_BUCKETS - REL_MAX_EXACT)).astype(np.int64)
    large = np.minimum(large, REL_BUCKETS - 1)
    return np.where(d < REL_MAX_EXACT, d, large)


def _cparams(sem, vmem=VMEM_LIMIT):
    return pltpu.CompilerParams(dimension_semantics=sem, vmem_limit_bytes=vmem)


def _dot(a, b):
    return jnp.dot(a, b, preferred_element_type=F32)


def _dot_tn(a, b):
    return lax.dot_general(a, b, (((0,), (0,)), ((), ())), preferred_element_type=F32)


def _inproj_body(x_ref, gmix_ref, w_ref, gains_ref,
                 nq_ref, cmp_ref, kslc_ref, vslc_ref, kwin_ref, vwin_ref,
                 dq_ref, dk_ref, dv_ref, iq_ref, gab_ref, small_ref):
    x = x_ref[...]
    h = x * lax.rsqrt(jnp.mean(x * x, axis=-1, keepdims=True) + EPS) * gmix_ref[...]
    hb = h.astype(BF16)
    lo = lax.broadcasted_iota(I32, (1, LANES), 1) < HEAD_DIM

    def mm(off, n):
        return _dot(hb, w_ref[:, off:off + n])

    def headnorm(y, g):
        y2 = y * y
        s_lo = jnp.sum(jnp.where(lo, y2, 0.0), axis=-1, keepdims=True)
        s_hi = jnp.sum(jnp.where(lo, 0.0, y2), axis=-1, keepdims=True)
        r = jnp.where(lo, lax.rsqrt(s_lo * (1.0 / HEAD_DIM) + EPS), lax.rsqrt(s_hi * (1.0 / HEAD_DIM) + EPS))
        return y * r * g

    g_qn, g_kn, g_qd, g_kd = (gains_ref[k:k + 1, :] for k in range(4))
    for c in range(NSA_W // LANES):
        nq_ref[:, c * LANES:(c + 1) * LANES] = headnorm(mm(OFF_NQ + c * LANES, LANES), g_qn).astype(BF16)
    cmp_ref[...] = mm(OFF_CMP, 256).astype(BF16)
    kslc_ref[...] = headnorm(mm(OFF_KSLC, LANES), g_kn).astype(BF16)
    vslc_ref[...] = mm(OFF_VSLC, LANES).astype(BF16)
    kwin_ref[...] = headnorm(mm(OFF_KWIN, LANES), g_kn).astype(BF16)
    vwin_ref[...] = mm(OFF_VWIN, LANES).astype(BF16)
    for c in range(DSA_W // LANES):
        dq_ref[:, c * LANES:(c + 1) * LANES] = headnorm(mm(OFF_DQ + c * LANES, LANES), g_qd).astype(BF16)
        dk_ref[:, c * LANES:(c + 1) * LANES] = headnorm(mm(OFF_DK + c * LANES, LANES), g_kd).astype(BF16)
    dv_ref[...] = mm(OFF_DV, DSA_W).astype(BF16)
    iq_ref[...] = (mm(OFF_IQ, IDX_HEADS * IDX_DIM) * (IDX_DIM ** -0.5)).astype(BF16)
    gab_ref[...] = jax.nn.sigmoid(mm(OFF_GAB, 2 * D_MODEL))
    small_ref[...] = mm(OFF_SMALL, LANES)


def _inproj(x2, g_mix, w_perm, gains, tm=256):
    S = x2.shape[0]
    row = lambda n: pl.BlockSpec((tm, n), lambda i: (i, 0))
    full = lambda a: pl.BlockSpec(a.shape, lambda i: (0,) * a.ndim)
    widths = [(NSA_W, BF16), (256, BF16), (LANES, BF16), (LANES, BF16), (LANES, BF16), (LANES, BF16),
              (DSA_W, BF16), (DSA_W, BF16), (DSA_W, BF16), (IDX_HEADS * IDX_DIM, BF16),
              (2 * D_MODEL, F32), (LANES, F32)]
    return pl.pallas_call(
        _inproj_body,
        grid=(S // tm,),
        in_specs=[row(D_MODEL), full(g_mix), full(w_perm), full(gains)],
        out_specs=[row(n) for n, _ in widths],
        out_shape=[jax.ShapeDtypeStruct((S, n), dt) for n, dt in widths],
        compiler_params=_cparams(("arbitrary",)),
        name="inproj",
    )(x2, g_mix, w_perm, gains)


def _compress_body(xa_ref, xb_ref, w1a_ref, w1b_ref, w1f_ref, pe_ref, w2_ref, gk_ref, o_ref):
    is_k = pl.program_id(0) < NSA_KV_GROUPS
    pe_term = jnp.sum(pe_ref[...] * w1f_ref[...], axis=0, keepdims=True)
    pre = _dot(xa_ref[...], w1a_ref[...]) + _dot(xb_ref[...], w1b_ref[...]) + pe_term
    hdn = pre * jax.nn.sigmoid(pre)
    out = _dot(hdn.astype(BF16), w2_ref[...])
    normed = out * lax.rsqrt(jnp.mean(out * out, axis=-1, keepdims=True) + EPS) * gk_ref[...]
    o_ref[...] = jnp.where(is_k, normed, out).astype(BF16)


def _compress(xa, xb, w1a, w1b, w1f, pe, w2, gk):
    n4, ncp, kdim = xa.shape
    G = NSA_KV_GROUPS
    kv = lambda i: i // G
    return pl.pallas_call(
        _compress_body,
        grid=(n4,),
        in_specs=[pl.BlockSpec((None, ncp, kdim), lambda i: (i, 0, 0)),
                  pl.BlockSpec((None, ncp, kdim), lambda i: (i, 0, 0)),
                  pl.BlockSpec((None, kdim, HEAD_DIM), lambda i: (kv(i), 0, 0)),
                  pl.BlockSpec((None, kdim, HEAD_DIM), lambda i: (kv(i), 0, 0)),
                  pl.BlockSpec((None, 2 * kdim, HEAD_DIM), lambda i: (kv(i), 0, 0)),
                  pl.BlockSpec((None, 2 * kdim, 1), lambda i: (kv(i), 0, 0)),
                  pl.BlockSpec((None, HEAD_DIM, HEAD_DIM), lambda i: (kv(i), 0, 0)),
                  pl.BlockSpec((1, HEAD_DIM), lambda i: (0, 0))],
        out_specs=pl.BlockSpec((None, ncp, HEAD_DIM), lambda i: (i, 0, 0)),
        out_shape=jax.ShapeDtypeStruct((n4, ncp, HEAD_DIM), BF16),
        compiler_params=_cparams(("arbitrary",)),
        name="compress",
    )(xa, xb, w1a, w1b, w1f, pe, w2, gk)


def _cmp_body(qt_ref, kc_ref, vct_ref, bt_ref, oct_ref, selt_ref, s_ref, *, NS):
    i = pl.program_id(0)
    t0 = i * Q_BLOCK
    T = Q_BLOCK
    NCP = 4 * NS
    R = NSA_GROUP_SIZE
    row = lax.broadcasted_iota(I32, (NCP, 1), 0)
    plane = row // NS
    nblk = row - plane * NS
    cend = NSA_SLC_BLOCK * nblk + NSA_CMP_STRIDE * plane + (NSA_CMP_LEN - 1)
    tq = t0 + lax.broadcasted_iota(I32, (1, T), 1)
    mask = cend <= tq
    kc = kc_ref[...]
    vct = vct_ref[...]
    WIN = 2 * SUBLANES
    n_lo = 2 * i - 2
    wstart = jnp.clip(n_lo // SUBLANES * SUBLANES, 0, NS - WIN)
    wrow = lax.broadcasted_iota(I32, (WIN, 1), 0)
    psum = jnp.zeros((NCP, T), F32)
    for r in range(R):
        s_ref[r] = _dot(kc, qt_ref[r * HEAD_DIM:(r + 1) * HEAD_DIM, :])
    for r in range(R):
        bt = bt_ref[r]
        for j in range(4):
            add = jnp.zeros((WIN, T), F32)
            for k in range(4):
                n = 2 * i + 1 - k
                add = add + jnp.where((wrow == n - wstart) & (n >= 0), bt[4 * j + k:4 * j + k + 1, :], 0.0)
            sl = pl.ds(pl.multiple_of(j * NS + wstart, SUBLANES), WIN)
            s_ref[r, sl, :] = s_ref[r, sl, :] + add
        s = jnp.where(mask, s_ref[r], -jnp.inf)
        m = jnp.maximum(jnp.max(s, axis=0, keepdims=True), -1e30)
        e = jnp.exp2(s - m)
        pc = e / jnp.maximum(jnp.sum(e, axis=0, keepdims=True), 1e-30)
        oct_ref[r * HEAD_DIM:(r + 1) * HEAD_DIM, :] = _dot(vct, pc.astype(BF16))
        psum = psum + pc
    p0, p1, p2, p3 = (psum[j * NS:(j + 1) * NS, :] for j in range(4))
    n_ar = lax.broadcasted_iota(I32, (NS, 1), 0)
    p3s = jnp.where(n_ar == 0, 0.0, pltpu.roll(p3, 1, axis=0))
    imp = p0 + p1 + p2 + 0.5 * p3 + 0.5 * p3s
    cur = tq // NSA_SLC_BLOCK
    valid = NSA_SLC_BLOCK * n_ar <= tq
    forced = valid & ((n_ar == 0) | (n_ar > cur - NSA_LOCAL_BLOCKS))
    score = jnp.where(forced, 1e30, jnp.where(valid, imp, -1.0))
    sel = jnp.zeros((NS, T), jnp.bool_)
    for _ in range(min(NSA_SLC_TOPK, NS)):
        mx = jnp.max(score, axis=0, keepdims=True)
        first = jnp.min(jnp.where(score == mx, n_ar, NS), axis=0, keepdims=True)
        pick = n_ar == first
        sel = sel | pick
        score = jnp.where(pick, -2.0, score)
    selb = jnp.where(sel, 1.0, 0.0).astype(BF16)
    if NS < LANES:
        selb = jnp.concatenate([selb, jnp.zeros((LANES - NS, T), BF16)], axis=0)
    for w in range(selt_ref.shape[0]):
        selt_ref[w] = selb[w * LANES:(w + 1) * LANES, :]


def _cmp(nqt, kc, vct, bt, S):
    NS = S // NSA_SLC_BLOCK
    NCP = 4 * NS
    NW = max(NS // LANES, 1)
    G, R = NSA_KV_GROUPS, NSA_GROUP_SIZE
    return pl.pallas_call(
        functools.partial(_cmp_body, NS=NS),
        grid=(S // Q_BLOCK, G),
        in_specs=[pl.BlockSpec((R * HEAD_DIM, Q_BLOCK), lambda i, g: (g, i)),
                  pl.BlockSpec((None, NCP, HEAD_DIM), lambda i, g: (g, 0, 0)),
                  pl.BlockSpec((None, HEAD_DIM, NCP), lambda i, g: (g, 0, 0)),
                  pl.BlockSpec((R, 16, Q_BLOCK), lambda i, g: (g, 0, 0))],
        out_specs=[pl.BlockSpec((R * HEAD_DIM, Q_BLOCK), lambda i, g: (g, i)),
                   pl.BlockSpec((None, NW, LANES, Q_BLOCK), lambda i, g: (g, 0, 0, i))],
        out_shape=[jax.ShapeDtypeStruct((NSA_W, S), F32),
                   jax.ShapeDtypeStruct((G, NW, LANES, S), BF16)],
        scratch_shapes=[pltpu.VMEM((R, NCP, Q_BLOCK), F32)],
        compiler_params=_cparams(("arbitrary", "arbitrary")),
        name="cmp",
    )(nqt, kc, vct, bt)


def _flash_init(m_ref, l_ref, acc_ref):
    m_ref[...] = jnp.full(m_ref.shape, -1e30, F32)
    l_ref[...] = jnp.zeros(l_ref.shape, F32)
    acc_ref[...] = jnp.zeros(acc_ref.shape, F32)


def _flash_update(s, vt, m_ref, l_ref, acc_ref):
    m_old = m_ref[...]
    m_new = jnp.maximum(m_old, jnp.max(s, axis=0, keepdims=True))
    alpha = jnp.exp2(m_old - m_new)
    p = jnp.exp2(s - m_new)
    l_ref[...] = alpha * l_ref[...] + jnp.sum(p, axis=0, keepdims=True)
    acc_ref[...] = alpha * acc_ref[...] + _dot(vt, p.astype(BF16))
    m_ref[...] = m_new


def _slcwin_body(qt_ref, ks_ref, vst_ref, kw_ref, vwt_ref, selt_ref, tbt_ref, oct_ref, glt_ref, ot_ref,
                 ms_ref, ls_ref, as_ref, mw_ref, lw_ref, aw_ref, sbuf_ref):
    i = pl.program_id(0)
    t0 = i * Q_BLOCK
    R = NSA_GROUP_SIZE
    T = Q_BLOCK
    q4t = jnp.concatenate([qt_ref[r * HEAD_DIM:(r + 1) * HEAD_DIM, :] for r in range(R)], axis=1)
    tq = lax.broadcasted_iota(I32, (1, T), 1)
    rep = lambda mk: jnp.concatenate([mk] * R, axis=1)
    tb4 = jnp.concatenate([tbt_ref[r] for r in range(R)], axis=1)

    def expand(w, base, nkeys):
        keyi = lax.broadcasted_iota(I32, (nkeys, LANES), 0)
        blk = lax.broadcasted_iota(I32, (nkeys, LANES), 1)
        e = jnp.where(blk == base + keyi // NSA_SLC_BLOCK, 1.0, 0.0).astype(BF16)
        return _dot(e, selt_ref[w])

    _flash_init(ms_ref, ls_ref, as_ref)
    _flash_init(mw_ref, lw_ref, aw_ref)
    far_end = t0 - Q_BLOCK
    nfar = jnp.maximum(far_end + KEY_TILE - 1, 0) // KEY_TILE
    blocks_per_tile = KEY_TILE // NSA_SLC_BLOCK
    key512 = lax.broadcasted_iota(I32, (KEY_TILE, 1), 0)
    last_w = selt_ref.shape[0] - 1

    def score_far(kt, slot):
        r0 = pl.multiple_of(NSA_PADF + kt * KEY_TILE, LANES)
        s = _dot(ks_ref[pl.ds(r0, KEY_TILE), :], q4t)
        b0 = kt * blocks_per_tile
        mexp = expand(jnp.minimum(b0 // LANES, last_w), b0 % LANES, KEY_TILE)
        mask = (mexp > 0.5) & (kt * KEY_TILE + key512 < far_end)
        sbuf_ref[slot] = jnp.where(rep(mask), s, -jnp.inf)

    def update_far(kt, slot):
        r0 = pl.multiple_of(NSA_PADF + kt * KEY_TILE, LANES)
        _flash_update(sbuf_ref[slot], vst_ref[:, pl.ds(r0, KEY_TILE)], ms_ref, ls_ref, as_ref)

    score_far(0, 0)

    def far2(k2, carry):
        kt = 2 * k2
        score_far(kt + 1, 1)
        update_far(kt, 0)
        score_far(kt + 2, 0)
        update_far(kt + 1, 1)
        return carry

    lax.fori_loop(0, (nfar + 1) // 2, far2, 0)

    keyn = lax.broadcasted_iota(I32, (NEAR, 1), 0)
    near_ok = (keyn - Q_BLOCK <= tq) & (keyn + t0 - Q_BLOCK >= 0)
    rn = pl.multiple_of(t0 + NSA_PADF - Q_BLOCK, LANES)
    bp = jnp.maximum(2 * i - 2, 0)
    bd = 2 * i
    mexp = jnp.concatenate([expand(bp // LANES, bp % LANES, Q_BLOCK),
                            expand(bd // LANES, bd % LANES, Q_BLOCK)], axis=0)
    WF = NSA_WINDOW - Q_BLOCK
    rw = pl.multiple_of(t0 + NSA_PADF - NSA_WINDOW, LANES)
    keyw = lax.broadcasted_iota(I32, (WF, 1), 0)
    maskw = (keyw > tq) & (keyw + t0 - NSA_WINDOW >= 0)
    s_sn = jnp.where(rep((mexp > 0.5) & near_ok), _dot(ks_ref[pl.ds(rn, NEAR), :], q4t) + tb4, -jnp.inf)
    s_wf = jnp.where(rep(maskw), _dot(kw_ref[pl.ds(rw, WF), :], q4t), -jnp.inf)
    s_wn = jnp.where(rep(near_ok), _dot(kw_ref[pl.ds(rn, NEAR), :], q4t) + tb4, -jnp.inf)
    _flash_update(s_sn, vst_ref[:, pl.ds(rn, NEAR)], ms_ref, ls_ref, as_ref)
    _flash_update(s_wf, vwt_ref[:, pl.ds(rw, WF)], mw_ref, lw_ref, aw_ref)
    _flash_update(s_wn, vwt_ref[:, pl.ds(rn, NEAR)], mw_ref, lw_ref, aw_ref)
    o_s = as_ref[...] / jnp.maximum(ls_ref[...], 1e-30)
    o_w = aw_ref[...] / jnp.maximum(lw_ref[...], 1e-30)

    gates = jax.nn.sigmoid(glt_ref[...])
    for r in range(R):
        hs = slice(r * HEAD_DIM, (r + 1) * HEAD_DIM)
        qs = slice(r * T, (r + 1) * T)
        o = (gates[3 * r:3 * r + 1, :] * oct_ref[hs, :]
             + gates[3 * r + 1:3 * r + 2, :] * o_s[:, qs]
             + gates[3 * r + 2:3 * r + 3, :] * o_w[:, qs])
        ot_ref[hs, :] = o.astype(BF16)


def _slcwin(nqt, ks, vst, kw, vwt, selt, tbt, oct, glt, S):
    G, R = NSA_KV_GROUPS, NSA_GROUP_SIZE
    spad = ks.shape[1]
    NW = selt.shape[1]
    kspec = pl.BlockSpec((None, spad, HEAD_DIM), lambda i, g: (g, 0, 0))
    vspec = pl.BlockSpec((None, HEAD_DIM, spad), lambda i, g: (g, 0, 0))
    qspec = pl.BlockSpec((R * HEAD_DIM, Q_BLOCK), lambda i, g: (g, i))
    st = lambda n: pltpu.VMEM((n, R * Q_BLOCK), F32)
    return pl.pallas_call(
        _slcwin_body,
        grid=(S // Q_BLOCK, G),
        in_specs=[qspec, kspec, vspec, kspec, vspec,
                  pl.BlockSpec((None, NW, LANES, Q_BLOCK), lambda i, g: (g, 0, 0, i)),
                  pl.BlockSpec((R, NEAR, Q_BLOCK), lambda i, g: (g, 0, 0)),
                  qspec,
                  pl.BlockSpec((None, 16, Q_BLOCK), lambda i, g: (g, 0, i))],
        out_specs=qspec,
        out_shape=jax.ShapeDtypeStruct((NSA_W, S), BF16),
        scratch_shapes=[st(1), st(1), st(HEAD_DIM), st(1), st(1), st(HEAD_DIM),
                        pltpu.VMEM((2, KEY_TILE, R * Q_BLOCK), F32)],
        compiler_params=_cparams(("arbitrary", "arbitrary")),
        name="slcwin",
    )(nqt, ks, vst, kw, vwt, selt, tbt, oct, glt)


INT_MIN = -2 ** 31
NEG_INF_KEY = int(np.array(-np.inf, np.float32).view(np.int32)) ^ 0x7FFFFFFF


def _dsa_body(dqt_ref, dk_ref, dvt_ref, iqt_ref, ikt_ref, iwt_ref, tbt_ref, ot_ref,
              keys_ref, m_ref, l_ref, acc_ref, sbuf_ref, *, S, KEEP):
    i = pl.program_id(0)
    t0 = i * Q_BLOCK
    T = Q_BLOCK
    H = DSA_HEADS
    ntile = (t0 + Q_BLOCK + KEY_TILE - 1) // KEY_TILE
    tq = t0 + lax.broadcasted_iota(I32, (1, T), 1)
    key512 = lax.broadcasted_iota(I32, (KEY_TILE, 1), 0)
    ktile = lambda kt: keys_ref[pl.ds(pl.multiple_of(Q_BLOCK + kt * KEY_TILE, LANES), KEY_TILE), :]

    keys_ref[0:Q_BLOCK, :] = jnp.full((Q_BLOCK, T), NEG_INF_KEY, I32)
    iqt = iqt_ref[...]
    iwt = iwt_ref[...] * (IDX_HEADS ** -0.5)
    qis = [iqt[h * IDX_DIM:(h + 1) * IDX_DIM, :] for h in range(IDX_HEADS)]
    wrow = [iwt[h:h + 1, :] for h in range(IDX_HEADS)]

    def score_tile(kt, carry):
        c0 = pl.multiple_of(kt * KEY_TILE, KEY_TILE)
        ik = ikt_ref[:, pl.ds(c0, KEY_TILE)]
        sc = jnp.zeros((KEY_TILE, T), F32)
        for h in range(IDX_HEADS):
            sc = sc + wrow[h] * jnp.maximum(_dot_tn(ik, qis[h]), 0.0)
        sc = jnp.where(sc == 0.0, 0.0, sc)
        sc = jnp.where(c0 + key512 <= tq, sc, -jnp.inf)
        bits = pltpu.bitcast(sc, I32)
        keys_ref[pl.ds(pl.multiple_of(Q_BLOCK + c0, LANES), KEY_TILE), :] = jnp.where(
            bits < 0, bits ^ 0x7FFFFFFF, bits)
        return carry

    lax.fori_loop(0, ntile, score_tile, 0)

    def count(pred):
        def body(kt, acc):
            hit = jnp.where(pred(ktile(kt), kt * KEY_TILE + key512), 1, 0)
            return acc + jnp.sum(hit.reshape(KEY_TILE // SUBLANES, SUBLANES, T), axis=0)
        acc = lax.fori_loop(0, ntile, body, jnp.zeros((SUBLANES, T), I32))
        return jnp.sum(acc, axis=0, keepdims=True)

    def bit_cond(c):
        b, _, cnt = c
        return (b < 32) & (jnp.max(jnp.where(cnt != KEEP, 1, 0)) > 0)

    def bit_step(c):
        b, prefix, cnt = c
        cand = prefix + lax.shift_left(jnp.int32(1), 31 - b)
        n = count(lambda kk, ab: kk >= cand)
        ok = n >= KEEP
        return b + 1, jnp.where(ok, cand, prefix), jnp.where(ok, n, cnt)

    _, thr, cnt_ge = lax.while_loop(
        bit_cond, bit_step,
        (jnp.int32(0), jnp.full((1, T), INT_MIN, I32), jnp.full((1, T), KEY_TILE, I32) * ntile))
    tie = (cnt_ge > KEEP) & (thr > NEG_INF_KEY)

    def tie_cut():
        need = KEEP - count(lambda kk, ab: kk > thr)
        nbits = int(S).bit_length()

        def step(b, j0):
            cand = j0 + lax.shift_left(jnp.int32(1), nbits - 1 - b)
            c = count(lambda kk, ab: (kk == thr) & (ab < cand))
            return jnp.where(c < need, cand, j0)

        j0 = lax.fori_loop(0, nbits, step, jnp.zeros((1, T), I32))
        return jnp.where(tie, j0 + 1, S)

    jcut = lax.cond(jnp.max(jnp.where(tie, 1, 0)) > 0, tie_cut, lambda: jnp.full((1, T), S, I32))

    def chosen(kk, ab):
        return (kk > thr) | ((kk == thr) & (ab < jcut))

    _flash_init(m_ref, l_ref, acc_ref)
    lo = lax.broadcasted_iota(I32, (LANES, 1), 0) < HEAD_DIM
    qpairs = []
    for pr in range(H // 2):
        qp = dqt_ref[pr * LANES:(pr + 1) * LANES, :]
        zero = jnp.zeros_like(qp)
        qpairs.append(jnp.concatenate([jnp.where(lo, qp, zero), jnp.where(lo, zero, qp)], axis=1))

    def scores(r0, nkeys, mask, bias):
        mask2 = jnp.concatenate([mask, mask], axis=1)
        out = []
        for pr in range(H // 2):
            s = _dot(dk_ref[pl.ds(r0, nkeys), pr * LANES:(pr + 1) * LANES], qpairs[pr])
            if bias is not None:
                s = s + jnp.concatenate([bias[2 * pr], bias[2 * pr + 1]], axis=1)
            out.append(jnp.where(mask2, s, -jnp.inf))
        return out

    def update(pr, s, r0, nkeys):
        m_old = m_ref[pr]
        m_new = jnp.maximum(m_old, jnp.max(s, axis=0, keepdims=True))
        alpha = jnp.exp2(m_old - m_new)
        p = jnp.exp2(s - m_new)
        l_ref[pr] = alpha * l_ref[pr] + jnp.sum(p, axis=0, keepdims=True)
        m_ref[pr] = m_new
        vt = dvt_ref[pr * LANES:(pr + 1) * LANES, pl.ds(r0, nkeys)]
        acc_ref[pr] = alpha * acc_ref[pr] + _dot(vt, p.astype(BF16))

    far_end = t0 - Q_BLOCK
    nfar = jnp.maximum(far_end + KEY_TILE - 1, 0) // KEY_TILE
    far_row = lambda kt: pl.multiple_of(DSA_PADF + kt * KEY_TILE, LANES)

    def score_far(kt, slot):
        ab = kt * KEY_TILE + key512
        mask = chosen(ktile(jnp.minimum(kt, ntile - 1)), ab) & (ab < far_end)
        for pr, s in enumerate(scores(far_row(kt), KEY_TILE, mask, None)):
            sbuf_ref[slot, pr] = s

    def update_far(kt, slot):
        for pr in range(H // 2):
            update(pr, sbuf_ref[slot, pr], far_row(kt), KEY_TILE)

    score_far(0, 0)

    def far2(k2, carry):
        kt = 2 * k2
        score_far(kt + 1, 1)
        update_far(kt, 0)
        score_far(kt + 2, 0)
        update_far(kt + 1, 1)
        return carry

    lax.fori_loop(0, (nfar + 1) // 2, far2, 0)

    abn = lax.broadcasted_iota(I32, (NEAR, 1), 0) + t0 - Q_BLOCK
    kkn = keys_ref[pl.ds(pl.multiple_of(t0, LANES), NEAR), :]
    rn = pl.multiple_of(t0 + DSA_PADF - Q_BLOCK, LANES)
    for pr, s in enumerate(scores(rn, NEAR, chosen(kkn, abn) & (abn <= tq) & (abn >= 0), tbt_ref)):
        update(pr, s, rn, NEAR)

    for pr in range(H // 2):
        o = acc_ref[pr] / jnp.maximum(l_ref[pr], 1e-30)
        ot_ref[pr * LANES:(pr + 1) * LANES, :] = jnp.where(lo, o[:, 0:T], o[:, T:2 * T]).astype(BF16)


def _dsa(dqt, dkp, dvtp, iqt, ikt, iwt, tbt, S):
    KEEP = min(DSA_TOPK_MAX, S // 4)
    H = DSA_HEADS
    T = Q_BLOCK
    once = pl.Buffered(1)
    full = lambda a: pl.BlockSpec(a.shape, lambda i: (0,) * a.ndim, pipeline_mode=once)
    return pl.pallas_call(
        functools.partial(_dsa_body, S=S, KEEP=KEEP),
        grid=(S // T,),
        in_specs=[pl.BlockSpec((DSA_W, T), lambda i: (0, i)),
                  full(dkp), full(dvtp),
                  pl.BlockSpec((IDX_HEADS * IDX_DIM, T), lambda i: (0, i)),
                  full(ikt),
                  pl.BlockSpec((IDX_HEADS, T), lambda i: (0, i)),
                  full(tbt)],
        out_specs=pl.BlockSpec((DSA_W, T), lambda i: (0, i)),
        out_shape=jax.ShapeDtypeStruct((DSA_W, S), BF16),
        scratch_shapes=[pltpu.VMEM((Q_BLOCK + S, T), I32),
                        pltpu.VMEM((H // 2, 1, 2 * T), F32),
                        pltpu.VMEM((H // 2, 1, 2 * T), F32),
                        pltpu.VMEM((H // 2, LANES, 2 * T), F32),
                        pltpu.VMEM((2, H // 2, KEY_TILE, 2 * T), F32)],
        compiler_params=_cparams(("arbitrary",)),
        name="dsa",
    )(dqt, dkp, dvtp, iqt, ikt, iwt, tbt)


def _split_bf16(a):
    hi = a.astype(BF16)
    return hi, (a - hi.astype(F32)).astype(BF16)


def _merge_body(x_ref, oa_ref, ob_ref, gab_ref, wa_ref, wb_ref, wo_ref, gffn_ref, wr_ref, br_ref,
                x1_ref, h2_ref, ridx_ref, rw_ref):
    a = _dot(oa_ref[...], wa_ref[...])
    b = _dot(ob_ref[...], wb_ref[...])
    gab = gab_ref[...]
    merged = gab[:, :D_MODEL] * a + gab[:, D_MODEL:] * b
    x1 = x_ref[...] + _dot(merged.astype(BF16), wo_ref[...])
    x1_ref[...] = x1
    h2 = x1 * lax.rsqrt(jnp.mean(x1 * x1, axis=-1, keepdims=True) + EPS) * gffn_ref[...]
    h2_ref[...] = h2
    hh, hl = _split_bf16(h2)
    wh, wl = _split_bf16(wr_ref[...])
    logits = _dot(hh, wh) + _dot(hh, wl) + _dot(hl, wh) + br_ref[...]
    lane = lax.broadcasted_iota(I32, (1, LANES), 1)
    idx_out = jnp.zeros(logits.shape, I32)
    val_out = jnp.full(logits.shape, -jnp.inf, F32)
    for k in range(TOP_K):
        mx = jnp.max(logits, axis=-1, keepdims=True)
        first = jnp.min(jnp.where(logits == mx, lane, LANES), axis=-1, keepdims=True)
        idx_out = jnp.where(lane == k, first, idx_out)
        val_out = jnp.where(lane == k, mx, val_out)
        logits = jnp.where(lane == first, -jnp.inf, logits)
    e = jnp.exp(val_out - jnp.max(val_out, axis=-1, keepdims=True))
    ridx_ref[...] = idx_out
    rw_ref[...] = e / jnp.sum(e, axis=-1, keepdims=True)


def _merge(x2, oa, ob, gab, wa, wb, wo, gffn, wr, br, tm=256):
    S = x2.shape[0]
    row = lambda n: pl.BlockSpec((tm, n), lambda i: (i, 0))
    full = lambda a: pl.BlockSpec(a.shape, lambda i: (0,) * a.ndim)
    return pl.pallas_call(
        _merge_body,
        grid=(S // tm,),
        in_specs=[row(D_MODEL), row(NSA_W), row(DSA_W), row(2 * D_MODEL),
                  full(wa), full(wb), full(wo), full(gffn), full(wr), full(br)],
        out_specs=[row(D_MODEL), row(D_MODEL), row(LANES), row(LANES)],
        out_shape=[jax.ShapeDtypeStruct((S, D_MODEL), F32), jax.ShapeDtypeStruct((S, D_MODEL), F32),
                   jax.ShapeDtypeStruct((S, LANES), I32), jax.ShapeDtypeStruct((S, LANES), F32)],
        compiler_params=_cparams(("arbitrary",)),
        name="merge",
    )(x2, oa, ob, gab, wa, wb, wo, gffn, wr, br)


def _rank_body(ridx_ref, rank_ref, cnt_ref, carry_ref):
    @pl.when(pl.program_id(0) == 0)
    def _():
        carry_ref[...] = jnp.zeros(carry_ref.shape, F32)

    ridx = ridx_ref[...]
    tm = ridx.shape[0]
    lane = lax.broadcasted_iota(I32, (1, LANES), 1)
    hits = [lane == ridx[:, k:k + 1] for k in range(TOP_K)]
    member = jnp.zeros((tm, LANES), F32)
    for hk in hits:
        member = member + jnp.where(hk, 1.0, 0.0)
    ri = lax.broadcasted_iota(I32, (tm, tm), 0)
    ci = lax.broadcasted_iota(I32, (tm, tm), 1)
    lower = jnp.where(ci < ri, 1.0, 0.0).astype(BF16)
    before = _dot(lower, member.astype(BF16)) + carry_ref[0:1, :]
    out = jnp.zeros((tm, LANES), I32)
    for k, hk in enumerate(hits):
        rk = jnp.sum(jnp.where(hk, before, 0.0), axis=-1, keepdims=True)
        out = jnp.where(lane == k, rk.astype(I32), out)
    rank_ref[...] = out
    total = carry_ref[0:1, :] + jnp.sum(member, axis=0, keepdims=True)
    carry_ref[...] = jnp.broadcast_to(total, carry_ref.shape)
    cnt_ref[...] = jnp.broadcast_to(total, cnt_ref.shape).astype(I32)


def _rank(ridx, tm=512):
    S = ridx.shape[0]
    return pl.pallas_call(
        _rank_body,
        grid=(S // tm,),
        in_specs=[pl.BlockSpec((tm, LANES), lambda i: (i, 0))],
        out_specs=[pl.BlockSpec((tm, LANES), lambda i: (i, 0)), pl.BlockSpec((8, LANES), lambda i: (0, 0))],
        out_shape=[jax.ShapeDtypeStruct((S, LANES), I32), jax.ShapeDtypeStruct((8, LANES), I32)],
        scratch_shapes=[pltpu.VMEM((8, LANES), F32)],
        compiler_params=_cparams(("arbitrary",)),
        name="rank",
    )(ridx)


def _dispatch_body(dest_ref, h_ref, zero_ref, xe_ref, sem):
    del zero_ref
    i = pl.program_id(0)
    tm = h_ref.shape[0]

    def copy(r, k):
        d = dest_ref[(i * tm + r) * TOP_K + k]
        return pltpu.make_async_copy(h_ref.at[pl.ds(r, 1)], xe_ref.at[pl.ds(d, 1)], sem)

    def start(r, c):
        for k in range(TOP_K):
            copy(r, k).start()
        return c

    def wait(r, c):
        for k in range(TOP_K):
            copy(r, k).wait()
        return c

    lax.fori_loop(0, tm, start, 0)
    lax.fori_loop(0, tm, wait, 0)


def _dispatch(dest, h2, zeros_p, tm=128):
    S = h2.shape[0]
    return pl.pallas_call(
        _dispatch_body,
        grid_spec=pltpu.PrefetchScalarGridSpec(
            num_scalar_prefetch=1,
            grid=(S // tm,),
            in_specs=[pl.BlockSpec((tm, D_MODEL), lambda i, d: (i, 0)),
                      pl.BlockSpec(memory_space=pl.ANY)],
            out_specs=pl.BlockSpec(memory_space=pl.ANY),
            scratch_shapes=[pltpu.SemaphoreType.DMA(())]),
        out_shape=jax.ShapeDtypeStruct(zeros_p.shape, zeros_p.dtype),
        input_output_aliases={2: 0},
        compiler_params=_cparams(("arbitrary",)),
        name="dispatch",
    )(dest, h2, zeros_p)


def _experts_body(ce_ref, nu_ref, x_ref, w1_ref, b1_ref, w2_ref, b2_ref, y_ref):
    c = pl.program_id(0)

    @pl.when(c < nu_ref[0])
    def _():
        gu = _dot(x_ref[...].astype(BF16), w1_ref[...]) + b1_ref[...]
        gate = jnp.minimum(gu[:, :D_FF], SWIGLU_LIMIT)
        lin = jnp.clip(gu[:, D_FF:], -SWIGLU_LIMIT, SWIGLU_LIMIT)
        act = (lin + 1.0) * gate * jax.nn.sigmoid(SWIGLU_ALPHA * gate)
        y_ref[...] = _dot(act.astype(BF16), w2_ref[...]) + b2_ref[...]

    @pl.when(c >= nu_ref[0])
    def _():
        y_ref[...] = jnp.zeros(y_ref.shape, F32)


def _experts(chunk_e, n_used, xe, w1, b1, w2, b2):
    P = xe.shape[0]
    nch = P // MOE_ROWS
    return pl.pallas_call(
        _experts_body,
        grid_spec=pltpu.PrefetchScalarGridSpec(
            num_scalar_prefetch=2,
            grid=(nch,),
            in_specs=[pl.BlockSpec((MOE_ROWS, D_MODEL), lambda c, ce, nu: (c, 0)),
                      pl.BlockSpec((None, D_MODEL, 2 * D_FF), lambda c, ce, nu: (ce[c], 0, 0)),
                      pl.BlockSpec((None, 1, 2 * D_FF), lambda c, ce, nu: (ce[c], 0, 0)),
                      pl.BlockSpec((None, D_FF, D_MODEL), lambda c, ce, nu: (ce[c], 0, 0)),
                      pl.BlockSpec((None, 1, D_MODEL), lambda c, ce, nu: (ce[c], 0, 0))],
            out_specs=pl.BlockSpec((MOE_ROWS, D_MODEL), lambda c, ce, nu: (c, 0))),
        out_shape=jax.ShapeDtypeStruct((P, D_MODEL), F32),
        compiler_params=_cparams(("arbitrary",)),
        name="experts",
    )(chunk_e, n_used, xe, w1, b1, w2, b2)


def _final_body(dest_ref, x1_ref, rw_ref, p_ref, gple_ref, wg_ref, wp_ref, y_ref, o_ref, rows_ref, sem):
    i = pl.program_id(0)
    tm = x1_ref.shape[0]

    def copy(r, k):
        d = dest_ref[(i * tm + r) * TOP_K + k]
        return pltpu.make_async_copy(y_ref.at[pl.ds(d, 1)], rows_ref.at[k, pl.ds(r, 1)], sem)

    def start(r, c):
        for k in range(TOP_K):
            copy(r, k).start()
        return c

    def wait(r, c):
        for k in range(TOP_K):
            copy(r, k).wait()
        return c

    lax.fori_loop(0, tm, start, 0)
    lax.fori_loop(0, tm, wait, 0)
    rw = rw_ref[...]
    x2 = x1_ref[...]
    for k in range(TOP_K):
        x2 = x2 + rw[:, k:k + 1] * rows_ref[k]
    hn = x2 * lax.rsqrt(jnp.mean(x2 * x2, axis=-1, keepdims=True) + EPS) * gple_ref[...]
    gate = jax.nn.sigmoid(_dot(hn.astype(BF16), wg_ref[...]))
    o_ref[...] = x2 + gate * _dot(p_ref[...].astype(BF16), wp_ref[...])


def _final(dest, x1, rw, p2, gple, wg, wp, yexp, tm=128):
    S = x1.shape[0]
    row = lambda n: pl.BlockSpec((tm, n), lambda i, d: (i, 0))
    full = lambda a: pl.BlockSpec(a.shape, lambda i, d: (0,) * a.ndim)
    return pl.pallas_call(
        _final_body,
        grid_spec=pltpu.PrefetchScalarGridSpec(
            num_scalar_prefetch=1,
            grid=(S // tm,),
            in_specs=[row(D_MODEL), row(LANES), row(PLE_DIM), full(gple), full(wg), full(wp),
                      pl.BlockSpec(memory_space=pl.ANY)],
            out_specs=row(D_MODEL),
            scratch_shapes=[pltpu.VMEM((TOP_K, tm, D_MODEL), F32), pltpu.SemaphoreType.DMA(())]),
        out_shape=jax.ShapeDtypeStruct((S, D_MODEL), F32),
        compiler_params=_cparams(("arbitrary",)),
        name="final",
    )(dest, x1, rw, p2, gple, wg, wp, yexp)


def _near_bias_t(rel_cols):
    bucket = _rel_bucket_table(NEAR)
    f = (rel_cols[bucket] - rel_cols[REL_BUCKETS - 1][None, :]).T * LOG2E
    d = np.arange(Q_BLOCK)[None, :] - np.arange(NEAR)[:, None] + Q_BLOCK
    return jnp.where(d >= 0, f[:, np.clip(d, 0, NEAR - 1)], 0.0)


def _cmp_bias_t(rel_cols):
    bucket = _rel_bucket_table(NEAR)
    f = (rel_cols[bucket] - rel_cols[REL_BUCKETS - 1][None, :]).T * LOG2E
    rowi = np.arange(16)
    d = (np.arange(Q_BLOCK)[None, :] - NSA_CMP_STRIDE * (rowi // 4)[:, None] - (NSA_CMP_LEN - 1)
         + NSA_SLC_BLOCK * ((rowi % 4)[:, None] - 1))
    return jnp.where((d >= 0) & (d < NEAR), f[:, np.clip(d, 0, NEAR - 1)], 0.0)


def kernel(x, p, w_in, g_mix, g_q_nsa, g_k_nsa, g_q_dsa, g_k_dsa, pe_ck, w_ck1, w_ck2, pe_cv, w_cv1, w_cv2,
           w_branch_a, w_branch_b, w_out, rel_bias, g_ffn, w_router, b_router, w_e1, b_e1, w_e2, b_e2,
           g_ple, w_ple, w_ple_gate):
    B, S, D = x.shape
    assert B == 1 and D == D_MODEL and S % KEY_TILE == 0 and w_in.shape[0] == 1
    G, R = NSA_KV_GROUPS, NSA_GROUP_SIZE
    x2 = x.reshape(S, D)

    perm = np.concatenate([np.arange(0, 1280), np.arange(1304, 3096), np.arange(3136, 5184),
                           np.arange(1280, 1304), np.arange(3096, 3136)])
    w_perm = jnp.pad(w_in[0][:, perm], ((0, 0), (0, D_IN_PAD - perm.size))).astype(BF16)
    two = lambda g: jnp.tile(g.reshape(1, HEAD_DIM), (1, 2))
    qscale = HEAD_DIM ** -0.5 * LOG2E
    gains = jnp.concatenate([two(g_q_nsa[0]) * qscale, two(g_k_nsa[0]),
                             two(g_q_dsa[0]) * qscale, two(g_k_dsa[0])], axis=0)
    (nq, cmpraw, kslc, vslc, kwin, vwin, dq, dk, dv, iq, gab, small) = _inproj(
        x2, g_mix[0].reshape(1, D), w_perm, gains)

    NS = S // NSA_SLC_BLOCK
    nchunk = S // NSA_CMP_STRIDE
    xc = cmpraw.reshape(nchunk, NSA_CMP_STRIDE, 2 * G, HEAD_DIM).transpose(2, 0, 1, 3).reshape(
        2 * G, nchunk, NSA_CMP_STRIDE * HEAD_DIM)
    planes = lambda a: a.reshape(2 * G, NS, 4, -1).transpose(0, 2, 1, 3).reshape(2 * G, nchunk, -1)
    xa = planes(xc)
    xb = planes(jnp.roll(xc, -1, axis=1))
    half = NSA_CMP_STRIDE * HEAD_DIM
    w1 = jnp.stack([w_ck1[0], w_cv1[0]]).reshape(2, 2 * half, HEAD_DIM)
    pe = jnp.stack([pe_ck[0], pe_cv[0]]).reshape(2, 2 * half, 1)
    w2 = jnp.stack([w_ck2[0], w_cv2[0]]).astype(BF16)
    kvc = _compress(xa, xb, w1[:, :half].astype(BF16), w1[:, half:].astype(BF16), w1, pe, w2,
                    g_k_nsa[0].reshape(1, HEAD_DIM))

    rel_nsa = rel_bias[:, :NSA_HEADS]
    rel_dsa = rel_bias[:, NSA_HEADS:]
    nqt = nq.T
    oct, selt = _cmp(nqt, kvc[:G], kvc[G:].transpose(0, 2, 1), _cmp_bias_t(rel_nsa), S)
    kgroups = lambda a: jnp.pad(a.reshape(S, G, HEAD_DIM).transpose(1, 0, 2),
                                ((0, 0), (NSA_PADF, TAIL_PAD), (0, 0)))
    vgroups = lambda a: jnp.pad(a.reshape(S, G, HEAD_DIM).transpose(1, 2, 0),
                                ((0, 0), (0, 0), (NSA_PADF, TAIL_PAD)))
    glt = small[:, SMALL_GATE:SMALL_GATE + 3 * NSA_HEADS].reshape(S, G, 3 * R).transpose(1, 2, 0)
    glt = jnp.pad(glt, ((0, 0), (0, 16 - 3 * R), (0, 0)))
    o_at = _slcwin(nqt, kgroups(kslc), vgroups(vslc), kgroups(kwin), vgroups(vwin), selt,
                   _near_bias_t(rel_nsa), oct, glt, S)

    dkp = jnp.pad(dk, ((DSA_PADF, TAIL_PAD), (0, 0)))
    dvtp = jnp.pad(dv.T, ((0, 0), (DSA_PADF, TAIL_PAD)))
    ikt = small[:, SMALL_IK:SMALL_IK + IDX_DIM].T.astype(BF16)
    iwt = small[:, SMALL_IW:SMALL_IW + IDX_HEADS].T
    o_bt = _dsa(dq.T, dkp, dvtp, iq.T, ikt, iwt, _near_bias_t(rel_dsa), S)

    wr = jnp.pad(w_router[0], ((0, 0), (0, LANES - N_EXPERTS)))
    br = jnp.pad(b_router[0].reshape(1, N_EXPERTS), ((0, 0), (0, LANES - N_EXPERTS)), constant_values=-1e30)
    x1, h2, ridx, rw = _merge(x2, o_at.T, o_bt.T, gab, w_branch_a[0].astype(BF16), w_branch_b[0].astype(BF16),
                              w_out[0].astype(BF16), g_ffn[0].reshape(1, D), wr, br)

    rank, cnt = _rank(ridx)
    counts = cnt[0, :N_EXPERTS]
    padded = (counts + MOE_ROWS - 1) // MOE_ROWS * MOE_ROWS
    pad_end = jnp.cumsum(padded)
    pad_start = pad_end - padded
    eidx = ridx[:, :TOP_K]
    dest = (jnp.sum(jnp.where(eidx[..., None] == jnp.arange(N_EXPERTS), pad_start, 0), axis=-1)
            + rank[:, :TOP_K]).reshape(-1).astype(I32)
    n_chunks = -(-(S * TOP_K) // MOE_ROWS) + N_EXPERTS
    chunk_start = jnp.arange(n_chunks, dtype=I32) * MOE_ROWS
    chunk_e = jnp.minimum(jnp.sum(chunk_start[:, None] >= pad_end[None, :], axis=-1), N_EXPERTS - 1).astype(I32)
    n_used = (pad_end[-1] // MOE_ROWS).astype(I32).reshape(1)
    xe = _dispatch(dest, h2, jnp.zeros((n_chunks * MOE_ROWS, D), F32))
    yexp = _experts(chunk_e, n_used, xe, w_e1[0].astype(BF16), b_e1[0].reshape(N_EXPERTS, 1, 2 * D_FF),
                    w_e2[0].astype(BF16), b_e2[0].reshape(N_EXPERTS, 1, D))

    out = _final(dest, x1, rw, p[0].reshape(S, PLE_DIM), g_ple[0].reshape(1, D),
                 w_ple_gate[0].astype(BF16), w_ple[0].astype(BF16), yexp)
    return out.reshape(B, S, D)
```

```python
import functools
import math

import numpy as np
import jax
import jax.numpy as jnp
from jax import lax
from jax.experimental import pallas as pl
from jax.experimental.pallas import tpu as pltpu

F32 = jnp.float32
BF16 = jnp.bfloat16
I32 = jnp.int32

D_MODEL = 1024
PLE_DIM = 256
HEAD_DIM = 64
NSA_HEADS = 8
NSA_KV_GROUPS = 2
NSA_GROUP_SIZE = NSA_HEADS // NSA_KV_GROUPS
NSA_CMP_LEN = 32
NSA_CMP_STRIDE = 16
NSA_SLC_BLOCK = 64
NSA_SLC_TOPK = 16
NSA_LOCAL_BLOCKS = 2
NSA_WINDOW = 512
DSA_HEADS = 8
IDX_HEADS = 8
IDX_DIM = 32
DSA_TOPK_MAX = 256
Q_BLOCK = 128
REL_BUCKETS = 32
REL_MAX_EXACT = 16
REL_MAX_DIST = 128
N_EXPERTS = 32
TOP_K = 4
D_FF = 1024
SWIGLU_LIMIT = 7.0
SWIGLU_ALPHA = 1.702
MOE_ROWS = 256
EPS = 1e-6
LOG2E = math.log2(math.e)

NSA_W = NSA_HEADS * HEAD_DIM
DSA_W = DSA_HEADS * HEAD_DIM

LANES = 128
SUBLANES = 8
KEY_TILE = 512
NEAR = 2 * Q_BLOCK
NSA_PADF = NSA_WINDOW
DSA_PADF = Q_BLOCK
TAIL_PAD = 3 * KEY_TILE
VMEM_LIMIT = 56 * 1024 * 1024

OFF_NQ, OFF_CMP, OFF_KSLC, OFF_VSLC, OFF_KWIN, OFF_VWIN = 0, 512, 768, 896, 1024, 1152
OFF_DQ, OFF_DK, OFF_DV, OFF_IQ, OFF_GAB, OFF_SMALL = 1280, 1792, 2304, 2816, 3072, 5120
D_IN_PAD = 5248
SMALL_GATE, SMALL_IK, SMALL_IW = 0, 24, 56


def _rel_bucket_table(n):
    d = np.arange(n)
    nf = np.maximum(d, 1).astype(np.float64)
    large = REL_MAX_EXACT + (np.log(nf / REL_MAX_EXACT) / math.log(REL_MAX_DIST / REL_MAX_EXACT)
                             * (REL_BUCKETS - REL_MAX_EXACT)).astype(np.int64)
    large = np.minimum(large, REL_BUCKETS - 1)
    return np.where(d < REL_MAX_EXACT, d, large)


def _cparams(sem, vmem=VMEM_LIMIT):
    return pltpu.CompilerParams(dimension_semantics=sem, vmem_limit_bytes=vmem)


def _dot(a, b):
    return jnp.dot(a, b, preferred_element_type=F32)


def _dot_tn(a, b):
    return lax.dot_general(a, b, (((0,), (0,)), ((), ())), preferred_element_type=F32)


def _inproj_body(x_ref, gmix_ref, w_ref, gains_ref,
                 nq_ref, cmp_ref, kslc_ref, vslc_ref, kwin_ref, vwin_ref,
                 dq_ref, dk_ref, dv_ref, iq_ref, gab_ref, small_ref):
    x = x_ref[...]
    h = x * lax.rsqrt(jnp.mean(x * x, axis=-1, keepdims=True) + EPS) * gmix_ref[...]
    hb = h.astype(BF16)
    lo = lax.broadcasted_iota(I32, (1, LANES), 1) < HEAD_DIM

    def mm(off, n):
        return _dot(hb, w_ref[:, off:off + n])

    def headnorm(y, g):
        y2 = y * y
        s_lo = jnp.sum(jnp.where(lo, y2, 0.0), axis=-1, keepdims=True)
        s_hi = jnp.sum(jnp.where(lo, 0.0, y2), axis=-1, keepdims=True)
        r = jnp.where(lo, lax.rsqrt(s_lo * (1.0 / HEAD_DIM) + EPS), lax.rsqrt(s_hi * (1.0 / HEAD_DIM) + EPS))
        return y * r * g

    g_qn, g_kn, g_qd, g_kd = (gains_ref[k:k + 1, :] for k in range(4))
    for c in range(NSA_W // LANES):
        nq_ref[:, c * LANES:(c + 1) * LANES] = headnorm(mm(OFF_NQ + c * LANES, LANES), g_qn).astype(BF16)
    cmp_ref[...] = mm(OFF_CMP, 256).astype(BF16)
    kslc_ref[...] = headnorm(mm(OFF_KSLC, LANES), g_kn).astype(BF16)
    vslc_ref[...] = mm(OFF_VSLC, LANES).astype(BF16)
    kwin_ref[...] = headnorm(mm(OFF_KWIN, LANES), g_kn).astype(BF16)
    vwin_ref[...] = mm(OFF_VWIN, LANES).astype(BF16)
    for c in range(DSA_W // LANES):
        dq_ref[:, c * LANES:(c + 1) * LANES] = headnorm(mm(OFF_DQ + c * LANES, LANES), g_qd).astype(BF16)
        dk_ref[:, c * LANES:(c + 1) * LANES] = headnorm(mm(OFF_DK + c * LANES, LANES), g_kd).astype(BF16)
    dv_ref[...] = mm(OFF_DV, DSA_W).astype(BF16)
    iq_ref[...] = (mm(OFF_IQ, IDX_HEADS * IDX_DIM) * (IDX_DIM ** -0.5)).astype(BF16)
    gab_ref[...] = jax.nn.sigmoid(mm(OFF_GAB, 2 * D_MODEL))
    small_ref[...] = mm(OFF_SMALL, LANES)


def _inproj(x2, g_mix, w_perm, gains, tm=256):
    S = x2.shape[0]
    row = lambda n: pl.BlockSpec((tm, n), lambda i: (i, 0))
    full = lambda a: pl.BlockSpec(a.shape, lambda i: (0,) * a.ndim)
    widths = [(NSA_W, BF16), (256, BF16), (LANES, BF16), (LANES, BF16), (LANES, BF16), (LANES, BF16),
              (DSA_W, BF16), (DSA_W, BF16), (DSA_W, BF16), (IDX_HEADS * IDX_DIM, BF16),
              (2 * D_MODEL, F32), (LANES, F32)]
    return pl.pallas_call(
        _inproj_body,
        grid=(S // tm,),
        in_specs=[row(D_MODEL), full(g_mix), full(w_perm), full(gains)],
        out_specs=[row(n) for n, _ in widths],
        out_shape=[jax.ShapeDtypeStruct((S, n), dt) for n, dt in widths],
        compiler_params=_cparams(("arbitrary",)),
        name="inproj",
    )(x2, g_mix, w_perm, gains)


def _compress_body(xa_ref, xb_ref, w1a_ref, w1b_ref, w1f_ref, pe_ref, w2_ref, gk_ref, o_ref):
    is_k = pl.program_id(0) < NSA_KV_GROUPS
    pe_term = jnp.sum(pe_ref[...] * w1f_ref[...], axis=0, keepdims=True)
    pre = _dot(xa_ref[...], w1a_ref[...]) + _dot(xb_ref[...], w1b_ref[...]) + pe_term
    hdn = pre * jax.nn.sigmoid(pre)
    out = _dot(hdn.astype(BF16), w2_ref[...])
    normed = out * lax.rsqrt(jnp.mean(out * out, axis=-1, keepdims=True) + EPS) * gk_ref[...]
    o_ref[...] = jnp.where(is_k, normed, out).astype(BF16)


def _compress(xa, xb, w1a, w1b, w1f, pe, w2, gk):
    n4, ncp, kdim = xa.shape
    G = NSA_KV_GROUPS
    kv = lambda i: i // G
    return pl.pallas_call(
        _compress_body,
        grid=(n4,),
        in_specs=[pl.BlockSpec((None, ncp, kdim), lambda i: (i, 0, 0)),
                  pl.BlockSpec((None, ncp, kdim), lambda i: (i, 0, 0)),
                  pl.BlockSpec((None, kdim, HEAD_DIM), lambda i: (kv(i), 0, 0)),
                  pl.BlockSpec((None, kdim, HEAD_DIM), lambda i: (kv(i), 0, 0)),
                  pl.BlockSpec((None, 2 * kdim, HEAD_DIM), lambda i: (kv(i), 0, 0)),
                  pl.BlockSpec((None, 2 * kdim, 1), lambda i: (kv(i), 0, 0)),
                  pl.BlockSpec((None, HEAD_DIM, HEAD_DIM), lambda i: (kv(i), 0, 0)),
                  pl.BlockSpec((1, HEAD_DIM), lambda i: (0, 0))],
        out_specs=pl.BlockSpec((None, ncp, HEAD_DIM), lambda i: (i, 0, 0)),
        out_shape=jax.ShapeDtypeStruct((n4, ncp, HEAD_DIM), BF16),
        compiler_params=_cparams(("arbitrary",)),
        name="compress",
    )(xa, xb, w1a, w1b, w1f, pe, w2, gk)


def _cmp_body(qt_ref, kc_ref, vct_ref, bt_ref, oct_ref, selt_ref, s_ref, *, NS):
    i = pl.program_id(0)
    t0 = i * Q_BLOCK
    T = Q_BLOCK
    NCP = 4 * NS
    R = NSA_GROUP_SIZE
    row = lax.broadcasted_iota(I32, (NCP, 1), 0)
    plane = row // NS
    nblk = row - plane * NS
    cend = NSA_SLC_BLOCK * nblk + NSA_CMP_STRIDE * plane + (NSA_CMP_LEN - 1)
    tq = t0 + lax.broadcasted_iota(I32, (1, T), 1)
    mask = cend <= tq
    kc = kc_ref[...]
    vct = vct_ref[...]
    WIN = 2 * SUBLANES
    n_lo = 2 * i - 2
    wstart = jnp.clip(n_lo // SUBLANES * SUBLANES, 0, NS - WIN)
    wrow = lax.broadcasted_iota(I32, (WIN, 1), 0)
    psum = jnp.zeros((NCP, T), F32)
    for r in range(R):
        s_ref[r] = _dot(kc, qt_ref[r * HEAD_DIM:(r + 1) * HEAD_DIM, :])
    for r in range(R):
        bt = bt_ref[r]
        for j in range(4):
            add = jnp.zeros((WIN, T), F32)
            for k in range(4):
                n = 2 * i + 1 - k
                add = add + jnp.where((wrow == n - wstart) & (n >= 0), bt[4 * j + k:4 * j + k + 1, :], 0.0)
            sl = pl.ds(pl.multiple_of(j * NS + wstart, SUBLANES), WIN)
            s_ref[r, sl, :] = s_ref[r, sl, :] + add
        s = jnp.where(mask, s_ref[r], -jnp.inf)
        m = jnp.maximum(jnp.max(s, axis=0, keepdims=True), -1e30)
        e = jnp.exp2(s - m)
        pc = e / jnp.maximum(jnp.sum(e, axis=0, keepdims=True), 1e-30)
        oct_ref[r * HEAD_DIM:(r + 1) * HEAD_DIM, :] = _dot(vct, pc.astype(BF16))
        psum = psum + pc
    p0, p1, p2, p3 = (psum[j * NS:(j + 1) * NS, :] for j in range(4))
    n_ar = lax.broadcasted_iota(I32, (NS, 1), 0)
    p3s = jnp.where(n_ar == 0, 0.0, pltpu.roll(p3, 1, axis=0))
    imp = p0 + p1 + p2 + 0.5 * p3 + 0.5 * p3s
    cur = tq // NSA_SLC_BLOCK
    valid = NSA_SLC_BLOCK * n_ar <= tq
    forced = valid & ((n_ar == 0) | (n_ar > cur - NSA_LOCAL_BLOCKS))
    score = jnp.where(forced, 1e30, jnp.where(valid, imp, -1.0))
    sel = jnp.zeros((NS, T), jnp.bool_)
    for _ in range(min(NSA_SLC_TOPK, NS)):
        mx = jnp.max(score, axis=0, keepdims=True)
        first = jnp.min(jnp.where(score == mx, n_ar, NS), axis=0, keepdims=True)
        pick = n_ar == first
        sel = sel | pick
        score = jnp.where(pick, -2.0, score)
    selb = jnp.where(sel, 1.0, 0.0).astype(BF16)
    if NS < LANES:
        selb = jnp.concatenate([selb, jnp.zeros((LANES - NS, T), BF16)], axis=0)
    for w in range(selt_ref.shape[0]):
        selt_ref[w] = selb[w * LANES:(w + 1) * LANES, :]


def _cmp(nqt, kc, vct, bt, S):
    NS = S // NSA_SLC_BLOCK
    NCP = 4 * NS
    NW = max(NS // LANES, 1)
    G, R = NSA_KV_GROUPS, NSA_GROUP_SIZE
    return pl.pallas_call(
        functools.partial(_cmp_body, NS=NS),
        grid=(S // Q_BLOCK, G),
        in_specs=[pl.BlockSpec((R * HEAD_DIM, Q_BLOCK), lambda i, g: (g, i)),
                  pl.BlockSpec((None, NCP, HEAD_DIM), lambda i, g: (g, 0, 0)),
                  pl.BlockSpec((None, HEAD_DIM, NCP), lambda i, g: (g, 0, 0)),
                  pl.BlockSpec((R, 16, Q_BLOCK), lambda i, g: (g, 0, 0))],
        out_specs=[pl.BlockSpec((R * HEAD_DIM, Q_BLOCK), lambda i, g: (g, i)),
                   pl.BlockSpec((None, NW, LANES, Q_BLOCK), lambda i, g: (g, 0, 0, i))],
        out_shape=[jax.ShapeDtypeStruct((NSA_W, S), F32),
                   jax.ShapeDtypeStruct((G, NW, LANES, S), BF16)],
        scratch_shapes=[pltpu.VMEM((R, NCP, Q_BLOCK), F32)],
        compiler_params=_cparams(("arbitrary", "arbitrary")),
        name="cmp",
    )(nqt, kc, vct, bt)


def _flash_init(m_ref, l_ref, acc_ref):
    m_ref[...] = jnp.full(m_ref.shape, -1e30, F32)
    l_ref[...] = jnp.zeros(l_ref.shape, F32)
    acc_ref[...] = jnp.zeros(acc_ref.shape, F32)


def _flash_update(s, vt, m_ref, l_ref, acc_ref):
    m_old = m_ref[...]
    m_new = jnp.maximum(m_old, jnp.max(s, axis=0, keepdims=True))
    alpha = jnp.exp2(m_old - m_new)
    p = jnp.exp2(s - m_new)
    l_ref[...] = alpha * l_ref[...] + jnp.sum(p, axis=0, keepdims=True)
    acc_ref[...] = alpha * acc_ref[...] + _dot(vt, p.astype(BF16))
    m_ref[...] = m_new


def _slcwin_body(qt_ref, ks_ref, vst_ref, kw_ref, vwt_ref, selt_ref, tbt_ref, oct_ref, glt_ref, ot_ref,
                 ms_ref, ls_ref, as_ref, mw_ref, lw_ref, aw_ref, sbuf_ref):
    i = pl.program_id(0)
    t0 = i * Q_BLOCK
    R = NSA_GROUP_SIZE
    T = Q_BLOCK
    q4t = jnp.concatenate([qt_ref[r * HEAD_DIM:(r + 1) * HEAD_DIM, :] for r in range(R)], axis=1)
    tq = lax.broadcasted_iota(I32, (1, T), 1)
    rep = lambda mk: jnp.concatenate([mk] * R, axis=1)
    tb4 = jnp.concatenate([tbt_ref[r] for r in range(R)], axis=1)

    def expand(w, base, nkeys):
        keyi = lax.broadcasted_iota(I32, (nkeys, LANES), 0)
        blk = lax.broadcasted_iota(I32, (nkeys, LANES), 1)
        e = jnp.where(blk == base + keyi // NSA_SLC_BLOCK, 1.0, 0.0).astype(BF16)
        return _dot(e, selt_ref[w])

    _flash_init(ms_ref, ls_ref, as_ref)
    _flash_init(mw_ref, lw_ref, aw_ref)
    far_end = t0 - Q_BLOCK
    nfar = jnp.maximum(far_end + KEY_TILE - 1, 0) // KEY_TILE
    blocks_per_tile = KEY_TILE // NSA_SLC_BLOCK
    key512 = lax.broadcasted_iota(I32, (KEY_TILE, 1), 0)
    last_w = selt_ref.shape[0] - 1

    def score_far(kt, slot):
        r0 = pl.multiple_of(NSA_PADF + kt * KEY_TILE, LANES)
        s = _dot(ks_ref[pl.ds(r0, KEY_TILE), :], q4t)
        b0 = kt * blocks_per_tile
        mexp = expand(jnp.minimum(b0 // LANES, last_w), b0 % LANES, KEY_TILE)
        mask = (mexp > 0.5) & (kt * KEY_TILE + key512 < far_end)
        sbuf_ref[slot] = jnp.where(rep(mask), s, -jnp.inf)

    def update_far(kt, slot):
        r0 = pl.multiple_of(NSA_PADF + kt * KEY_TILE, LANES)
        _flash_update(sbuf_ref[slot], vst_ref[:, pl.ds(r0, KEY_TILE)], ms_ref, ls_ref, as_ref)

    score_far(0, 0)

    def far2(k2, carry):
        kt = 2 * k2
        score_far(kt + 1, 1)
        update_far(kt, 0)
        score_far(kt + 2, 0)
        update_far(kt + 1, 1)
        return carry

    lax.fori_loop(0, (nfar + 1) // 2, far2, 0)

    keyn = lax.broadcasted_iota(I32, (NEAR, 1), 0)
    near_ok = (keyn - Q_BLOCK <= tq) & (keyn + t0 - Q_BLOCK >= 0)
    rn = pl.multiple_of(t0 + NSA_PADF - Q_BLOCK, LANES)
    bp = jnp.maximum(2 * i - 2, 0)
    bd = 2 * i
    mexp = jnp.concatenate([expand(bp // LANES, bp % LANES, Q_BLOCK),
                            expand(bd // LANES, bd % LANES, Q_BLOCK)], axis=0)
    WF = NSA_WINDOW - Q_BLOCK
    rw = pl.multiple_of(t0 + NSA_PADF - NSA_WINDOW, LANES)
    keyw = lax.broadcasted_iota(I32, (WF, 1), 0)
    maskw = (keyw > tq) & (keyw + t0 - NSA_WINDOW >= 0)
    s_sn = jnp.where(rep((mexp > 0.5) & near_ok), _dot(ks_ref[pl.ds(rn, NEAR), :], q4t) + tb4, -jnp.inf)
    s_wf = jnp.where(rep(maskw), _dot(kw_ref[pl.ds(rw, WF), :], q4t), -jnp.inf)
    s_wn = jnp.where(rep(near_ok), _dot(kw_ref[pl.ds(rn, NEAR), :], q4t) + tb4, -jnp.inf)
    _flash_update(s_sn, vst_ref[:, pl.ds(rn, NEAR)], ms_ref, ls_ref, as_ref)
    _flash_update(s_wf, vwt_ref[:, pl.ds(rw, WF)], mw_ref, lw_ref, aw_ref)
    _flash_update(s_wn, vwt_ref[:, pl.ds(rn, NEAR)], mw_ref, lw_ref, aw_ref)
    o_s = as_ref[...] / jnp.maximum(ls_ref[...], 1e-30)
    o_w = aw_ref[...] / jnp.maximum(lw_ref[...], 1e-30)

    gates = jax.nn.sigmoid(glt_ref[...])
    for r in range(R):
        hs = slice(r * HEAD_DIM, (r + 1) * HEAD_DIM)
        qs = slice(r * T, (r + 1) * T)
        o = (gates[3 * r:3 * r + 1, :] * oct_ref[hs, :]
             + gates[3 * r + 1:3 * r + 2, :] * o_s[:, qs]
             + gates[3 * r + 2:3 * r + 3, :] * o_w[:, qs])
        ot_ref[hs, :] = o.astype(BF16)


def _slcwin(nqt, ks, vst, kw, vwt, selt, tbt, oct, glt, S):
    G, R = NSA_KV_GROUPS, NSA_GROUP_SIZE
    spad = ks.shape[1]
    NW = selt.shape[1]
    kspec = pl.BlockSpec((None, spad, HEAD_DIM), lambda i, g: (g, 0, 0))
    vspec = pl.BlockSpec((None, HEAD_DIM, spad), lambda i, g: (g, 0, 0))
    qspec = pl.BlockSpec((R * HEAD_DIM, Q_BLOCK), lambda i, g: (g, i))
    st = lambda n: pltpu.VMEM((n, R * Q_BLOCK), F32)
    return pl.pallas_call(
        _slcwin_body,
        grid=(S // Q_BLOCK, G),
        in_specs=[qspec, kspec, vspec, kspec, vspec,
                  pl.BlockSpec((None, NW, LANES, Q_BLOCK), lambda i, g: (g, 0, 0, i)),
                  pl.BlockSpec((R, NEAR, Q_BLOCK), lambda i, g: (g, 0, 0)),
                  qspec,
                  pl.BlockSpec((None, 16, Q_BLOCK), lambda i, g: (g, 0, i))],
        out_specs=qspec,
        out_shape=jax.ShapeDtypeStruct((NSA_W, S), BF16),
        scratch_shapes=[st(1), st(1), st(HEAD_DIM), st(1), st(1), st(HEAD_DIM),
                        pltpu.VMEM((2, KEY_TILE, R * Q_BLOCK), F32)],
        compiler_params=_cparams(("arbitrary", "arbitrary")),
        name="slcwin",
    )(nqt, ks, vst, kw, vwt, selt, tbt, oct, glt)


INT_MIN = -2 ** 31
INT_MAX = 2 ** 31 - 1
MAX_PEEL_ROUNDS = 64
NEG_INF_KEY = int(np.array(-np.inf, np.float32).view(np.int32)) ^ 0x7FFFFFFF


def _dsa_body(dqt_ref, dk_ref, dvt_ref, iqt_ref, ikt_ref, iwt_ref, tbt_ref, ltri_ref, ot_ref,
              keys_ref, m_ref, l_ref, acc_ref, sbuf_ref, *, S, KEEP):
    i = pl.program_id(0)
    t0 = i * Q_BLOCK
    T = Q_BLOCK
    H = DSA_HEADS
    ntile = (t0 + Q_BLOCK + KEY_TILE - 1) // KEY_TILE
    tq = t0 + lax.broadcasted_iota(I32, (1, T), 1)
    key512 = lax.broadcasted_iota(I32, (KEY_TILE, 1), 0)
    ktile = lambda kt: keys_ref[pl.ds(pl.multiple_of(Q_BLOCK + kt * KEY_TILE, LANES), KEY_TILE), :]

    keys_ref[0:Q_BLOCK, :] = jnp.full((Q_BLOCK, T), NEG_INF_KEY, I32)
    iqt = iqt_ref[...]
    iwt = iwt_ref[...] * (IDX_HEADS ** -0.5)
    qis = [jnp.concatenate([iqt[(2 * j) * IDX_DIM:(2 * j + 1) * IDX_DIM, :],
                            iqt[(2 * j + 1) * IDX_DIM:(2 * j + 2) * IDX_DIM, :]], axis=1)
           for j in range(IDX_HEADS // 2)]
    wrow = [iwt[h:h + 1, :] for h in range(IDX_HEADS)]

    def idx_dots(kt, slot):
        ik = ikt_ref[:, pl.ds(pl.multiple_of(kt * KEY_TILE, KEY_TILE), KEY_TILE)]
        for j in range(IDX_HEADS // 2):
            sbuf_ref[slot, j] = _dot_tn(ik, qis[j])

    def idx_keys(kt, slot):
        c0 = pl.multiple_of(kt * KEY_TILE, KEY_TILE)
        sc = jnp.zeros((KEY_TILE, T), F32)
        for j in range(IDX_HEADS // 2):
            d = jnp.maximum(sbuf_ref[slot, j], 0.0)
            sc = sc + wrow[2 * j] * d[:, 0:T]
            sc = sc + wrow[2 * j + 1] * d[:, T:2 * T]
        sc = jnp.where(sc == 0.0, 0.0, sc)
        sc = jnp.where(c0 + key512 <= tq, sc, -jnp.inf)
        bits = pltpu.bitcast(sc, I32)
        keys_ref[pl.ds(pl.multiple_of(Q_BLOCK + c0, LANES), KEY_TILE), :] = jnp.where(
            bits < 0, bits ^ 0x7FFFFFFF, bits)

    idx_dots(0, 0)

    def score2(k2, carry):
        kt = 2 * k2
        idx_dots(kt + 1, 1)
        idx_keys(kt, 0)
        idx_dots(kt + 2, 0)
        idx_keys(kt + 1, 1)
        return carry

    lax.fori_loop(0, (ntile + 1) // 2, score2, 0)

    def count(pred):
        def body(kt, acc):
            hit = jnp.where(pred(ktile(kt), kt * KEY_TILE + key512), 1, 0)
            return acc + jnp.sum(hit.reshape(KEY_TILE // SUBLANES, SUBLANES, T), axis=0)
        acc = lax.fori_loop(0, ntile, body, jnp.zeros((SUBLANES, T), I32))
        return jnp.sum(acc, axis=0, keepdims=True)

    def min_at_least(lowest):
        def body(kt, acc):
            kk = ktile(kt)
            v = jnp.where(kk >= lowest, kk, INT_MAX)
            return jnp.minimum(acc, jnp.min(v.reshape(KEY_TILE // SUBLANES, SUBLANES, T), axis=0))
        acc = lax.fori_loop(0, ntile, body, jnp.full((SUBLANES, T), INT_MAX, I32))
        return jnp.min(acc, axis=0, keepdims=True)

    def any_open(st, over):
        _, excess, fin, _ = st
        return jnp.max(jnp.where((fin == 0) & (excess > over), 1, 0)) > 0

    def bisect(b, st):
        thr, excess, fin, need = st
        cand = thr + lax.shift_left(jnp.int32(1), 31 - b)
        n = count(lambda kk, ab: kk >= cand)
        ok = (n >= KEEP) & (fin == 0)
        return jnp.where(ok, cand, thr), jnp.where(ok, n - KEEP, excess), fin, need

    def peel(st):
        thr, excess, fin, need = st
        low = min_at_least(thr)
        n_low = count(lambda kk, ab: kk == low)
        active = (fin == 0) & (excess > 0)
        tied = active & (n_low > excess)
        drop = active & (n_low <= excess)
        return (jnp.where(tied, low, jnp.where(drop, low + 1, thr)), jnp.where(drop, excess - n_low, excess),
                jnp.where(tied, 1, fin), jnp.where(tied, n_low - excess, need))

    zero = jnp.zeros((1, T), I32)
    st = (jnp.full((1, T), INT_MIN, I32), KEY_TILE * ntile - KEEP + zero, zero, zero)
    st = peel(bisect(0, st))
    _, st = lax.while_loop(lambda c: (c[0] < 32) & any_open(c[1], 1),
                           lambda c: (c[0] + 1, bisect(c[0], c[1])), (jnp.int32(1), st))
    _, st = lax.while_loop(lambda c: (c[0] < MAX_PEEL_ROUNDS) & any_open(c[1], 0),
                           lambda c: (c[0] + 1, peel(c[1])), (jnp.int32(0), st))
    thr, _, fin, need = st
    tie = (fin == 1) & (thr > NEG_INF_KEY)

    def tie_cut():
        needf = need.astype(F32)

        def body(kt, c):
            before, jmax = c
            is_tie = ktile(kt) == thr
            tied = jnp.where(is_tie, 1.0, 0.0)
            rank = before + _dot(ltri_ref[...], tied.astype(BF16))
            admit = is_tie & (rank < needf)
            jm = jnp.max(jnp.where(admit, kt * KEY_TILE + key512, -1), axis=0, keepdims=True)
            return before + jnp.sum(tied, axis=0, keepdims=True), jnp.maximum(jmax, jm)

        _, jmax = lax.fori_loop(0, ntile, body, (jnp.zeros((1, T), F32), jnp.full((1, T), -1, I32)))
        return jnp.where(tie, jmax + 1, S)

    jcut = lax.cond(jnp.max(jnp.where(tie, 1, 0)) > 0, tie_cut, lambda: jnp.full((1, T), S, I32))

    def chosen(kk, ab):
        return (kk > thr) | ((kk == thr) & (ab < jcut))

    _flash_init(m_ref, l_ref, acc_ref)
    lo = lax.broadcasted_iota(I32, (LANES, 1), 0) < HEAD_DIM
    qpairs = []
    for pr in range(H // 2):
        qp = dqt_ref[pr * LANES:(pr + 1) * LANES, :]
        zero = jnp.zeros_like(qp)
        qpairs.append(jnp.concatenate([jnp.where(lo, qp, zero), jnp.where(lo, zero, qp)], axis=1))

    def scores(r0, nkeys, mask, bias):
        mask2 = jnp.concatenate([mask, mask], axis=1)
        out = []
        for pr in range(H // 2):
            s = _dot(dk_ref[pl.ds(r0, nkeys), pr * LANES:(pr + 1) * LANES], qpairs[pr])
            if bias is not None:
                s = s + jnp.concatenate([bias[2 * pr], bias[2 * pr + 1]], axis=1)
            out.append(jnp.where(mask2, s, -jnp.inf))
        return out

    def update(pr, s, r0, nkeys):
        m_old = m_ref[pr]
        m_new = jnp.maximum(m_old, jnp.max(s, axis=0, keepdims=True))
        alpha = jnp.exp2(m_old - m_new)
        p = jnp.exp2(s - m_new)
        l_ref[pr] = alpha * l_ref[pr] + jnp.sum(p, axis=0, keepdims=True)
        m_ref[pr] = m_new
        vt = dvt_ref[pr * LANES:(pr + 1) * LANES, pl.ds(r0, nkeys)]
        acc_ref[pr] = alpha * acc_ref[pr] + _dot(vt, p.astype(BF16))

    far_end = t0 - Q_BLOCK
    nfar = jnp.maximum(far_end + KEY_TILE - 1, 0) // KEY_TILE
    far_row = lambda kt: pl.multiple_of(DSA_PADF + kt * KEY_TILE, LANES)

    def score_far(kt, slot):
        ab = kt * KEY_TILE + key512
        mask = chosen(ktile(jnp.minimum(kt, ntile - 1)), ab) & (ab < far_end)
        for pr, s in enumerate(scores(far_row(kt), KEY_TILE, mask, None)):
            sbuf_ref[slot, pr] = s

    def update_far(kt, slot):
        for pr in range(H // 2):
            update(pr, sbuf_ref[slot, pr], far_row(kt), KEY_TILE)

    score_far(0, 0)

    def far2(k2, carry):
        kt = 2 * k2
        score_far(kt + 1, 1)
        update_far(kt, 0)
        score_far(kt + 2, 0)
        update_far(kt + 1, 1)
        return carry

    lax.fori_loop(0, (nfar + 1) // 2, far2, 0)

    abn = lax.broadcasted_iota(I32, (NEAR, 1), 0) + t0 - Q_BLOCK
    kkn = keys_ref[pl.ds(pl.multiple_of(t0, LANES), NEAR), :]
    rn = pl.multiple_of(t0 + DSA_PADF - Q_BLOCK, LANES)
    for pr, s in enumerate(scores(rn, NEAR, chosen(kkn, abn) & (abn <= tq) & (abn >= 0), tbt_ref)):
        update(pr, s, rn, NEAR)

    for pr in range(H // 2):
        o = acc_ref[pr] / jnp.maximum(l_ref[pr], 1e-30)
        ot_ref[pr * LANES:(pr + 1) * LANES, :] = jnp.where(lo, o[:, 0:T], o[:, T:2 * T]).astype(BF16)


def _dsa(dqt, dkp, dvtp, iqt, ikt, iwt, tbt, ltri, S):
    KEEP = min(DSA_TOPK_MAX, S // 4)
    H = DSA_HEADS
    T = Q_BLOCK
    once = pl.Buffered(1)
    full = lambda a: pl.BlockSpec(a.shape, lambda i: (0,) * a.ndim, pipeline_mode=once)
    return pl.pallas_call(
        functools.partial(_dsa_body, S=S, KEEP=KEEP),
        grid=(S // T,),
        in_specs=[pl.BlockSpec((DSA_W, T), lambda i: (0, i)),
                  full(dkp), full(dvtp),
                  pl.BlockSpec((IDX_HEADS * IDX_DIM, T), lambda i: (0, i)),
                  full(ikt),
                  pl.BlockSpec((IDX_HEADS, T), lambda i: (0, i)),
                  full(tbt), full(ltri)],
        out_specs=pl.BlockSpec((DSA_W, T), lambda i: (0, i)),
        out_shape=jax.ShapeDtypeStruct((DSA_W, S), BF16),
        scratch_shapes=[pltpu.VMEM((Q_BLOCK + S + KEY_TILE, T), I32),
                        pltpu.VMEM((H // 2, 1, 2 * T), F32),
                        pltpu.VMEM((H // 2, 1, 2 * T), F32),
                        pltpu.VMEM((H // 2, LANES, 2 * T), F32),
                        pltpu.VMEM((2, H // 2, KEY_TILE, 2 * T), F32)],
        compiler_params=_cparams(("arbitrary",)),
        name="dsa",
    )(dqt, dkp, dvtp, iqt, ikt, iwt, tbt, ltri)


def _split_bf16(a):
    hi = a.astype(BF16)
    return hi, (a - hi.astype(F32)).astype(BF16)


def _merge_body(x_ref, oa_ref, ob_ref, gab_ref, wa_ref, wb_ref, wo_ref, gffn_ref, wr_ref, br_ref,
                x1_ref, h2_ref, ridx_ref, rw_ref):
    a = _dot(oa_ref[...], wa_ref[...])
    b = _dot(ob_ref[...], wb_ref[...])
    gab = gab_ref[...]
    merged = gab[:, :D_MODEL] * a + gab[:, D_MODEL:] * b
    x1 = x_ref[...] + _dot(merged.astype(BF16), wo_ref[...])
    x1_ref[...] = x1
    h2 = x1 * lax.rsqrt(jnp.mean(x1 * x1, axis=-1, keepdims=True) + EPS) * gffn_ref[...]
    h2_ref[...] = h2
    hh, hl = _split_bf16(h2)
    wh, wl = _split_bf16(wr_ref[...])
    logits = _dot(hh, wh) + _dot(hh, wl) + _dot(hl, wh) + br_ref[...]
    lane = lax.broadcasted_iota(I32, (1, LANES), 1)
    idx_out = jnp.zeros(logits.shape, I32)
    val_out = jnp.full(logits.shape, -jnp.inf, F32)
    for k in range(TOP_K):
        mx = jnp.max(logits, axis=-1, keepdims=True)
        first = jnp.min(jnp.where(logits == mx, lane, LANES), axis=-1, keepdims=True)
        idx_out = jnp.where(lane == k, first, idx_out)
        val_out = jnp.where(lane == k, mx, val_out)
        logits = jnp.where(lane == first, -jnp.inf, logits)
    e = jnp.exp(val_out - jnp.max(val_out, axis=-1, keepdims=True))
    ridx_ref[...] = idx_out
    rw_ref[...] = e / jnp.sum(e, axis=-1, keepdims=True)


def _merge(x2, oa, ob, gab, wa, wb, wo, gffn, wr, br, tm=256):
    S = x2.shape[0]
    row = lambda n: pl.BlockSpec((tm, n), lambda i: (i, 0))
    full = lambda a: pl.BlockSpec(a.shape, lambda i: (0,) * a.ndim)
    return pl.pallas_call(
        _merge_body,
        grid=(S // tm,),
        in_specs=[row(D_MODEL), row(NSA_W), row(DSA_W), row(2 * D_MODEL),
                  full(wa), full(wb), full(wo), full(gffn), full(wr), full(br)],
        out_specs=[row(D_MODEL), row(D_MODEL), row(LANES), row(LANES)],
        out_shape=[jax.ShapeDtypeStruct((S, D_MODEL), F32), jax.ShapeDtypeStruct((S, D_MODEL), F32),
                   jax.ShapeDtypeStruct((S, LANES), I32), jax.ShapeDtypeStruct((S, LANES), F32)],
        compiler_params=_cparams(("arbitrary",)),
        name="merge",
    )(x2, oa, ob, gab, wa, wb, wo, gffn, wr, br)


def _rank_body(ridx_ref, rank_ref, cnt_ref, carry_ref):
    @pl.when(pl.program_id(0) == 0)
    def _():
        carry_ref[...] = jnp.zeros(carry_ref.shape, F32)

    ridx = ridx_ref[...]
    tm = ridx.shape[0]
    lane = lax.broadcasted_iota(I32, (1, LANES), 1)
    hits = [lane == ridx[:, k:k + 1] for k in range(TOP_K)]
    member = jnp.zeros((tm, LANES), F32)
    for hk in hits:
        member = member + jnp.where(hk, 1.0, 0.0)
    ri = lax.broadcasted_iota(I32, (tm, tm), 0)
    ci = lax.broadcasted_iota(I32, (tm, tm), 1)
    lower = jnp.where(ci < ri, 1.0, 0.0).astype(BF16)
    before = _dot(lower, member.astype(BF16)) + carry_ref[0:1, :]
    out = jnp.zeros((tm, LANES), I32)
    for k, hk in enumerate(hits):
        rk = jnp.sum(jnp.where(hk, before, 0.0), axis=-1, keepdims=True)
        out = jnp.where(lane == k, rk.astype(I32), out)
    rank_ref[...] = out
    total = carry_ref[0:1, :] + jnp.sum(member, axis=0, keepdims=True)
    carry_ref[...] = jnp.broadcast_to(total, carry_ref.shape)
    cnt_ref[...] = jnp.broadcast_to(total, cnt_ref.shape).astype(I32)


def _rank(ridx, tm=512):
    S = ridx.shape[0]
    return pl.pallas_call(
        _rank_body,
        grid=(S // tm,),
        in_specs=[pl.BlockSpec((tm, LANES), lambda i: (i, 0))],
        out_specs=[pl.BlockSpec((tm, LANES), lambda i: (i, 0)), pl.BlockSpec((8, LANES), lambda i: (0, 0))],
        out_shape=[jax.ShapeDtypeStruct((S, LANES), I32), jax.ShapeDtypeStruct((8, LANES), I32)],
        scratch_shapes=[pltpu.VMEM((8, LANES), F32)],
        compiler_params=_cparams(("arbitrary",)),
        name="rank",
    )(ridx)


def _dispatch_body(dest_ref, h_ref, zero_ref, xe_ref, sem):
    del zero_ref
    i = pl.program_id(0)
    tm = h_ref.shape[0]

    def copy(r, k):
        d = dest_ref[(i * tm + r) * TOP_K + k]
        return pltpu.make_async_copy(h_ref.at[pl.ds(r, 1)], xe_ref.at[pl.ds(d, 1)], sem)

    def start(r, c):
        for k in range(TOP_K):
            copy(r, k).start(priority=k % 2)
        return c

    def wait(r, c):
        for k in range(TOP_K):
            copy(r, k).wait()
        return c

    lax.fori_loop(0, tm, start, 0)
    lax.fori_loop(0, tm, wait, 0)


def _dispatch(dest, h2, zeros_p, tm=128):
    S = h2.shape[0]
    return pl.pallas_call(
        _dispatch_body,
        grid_spec=pltpu.PrefetchScalarGridSpec(
            num_scalar_prefetch=1,
            grid=(S // tm,),
            in_specs=[pl.BlockSpec((tm, D_MODEL), lambda i, d: (i, 0)),
                      pl.BlockSpec(memory_space=pl.ANY)],
            out_specs=pl.BlockSpec(memory_space=pl.ANY),
            scratch_shapes=[pltpu.SemaphoreType.DMA(())]),
        out_shape=jax.ShapeDtypeStruct(zeros_p.shape, zeros_p.dtype),
        input_output_aliases={2: 0},
        compiler_params=_cparams(("arbitrary",)),
        name="dispatch",
    )(dest, h2, zeros_p)


def _experts_body(ce_ref, nu_ref, x_ref, w1_ref, b1_ref, w2_ref, b2_ref, y_ref, w1b_ref, w2b_ref):
    c = pl.program_id(0)

    @pl.when((c == 0) | (ce_ref[c] != ce_ref[jnp.maximum(c - 1, 0)]))
    def _():
        w1b_ref[...] = w1_ref[...].astype(BF16)
        w2b_ref[...] = w2_ref[...].astype(BF16)

    @pl.when(c < nu_ref[0])
    def _():
        gu = _dot(x_ref[...].astype(BF16), w1b_ref[...]) + b1_ref[...]
        gate = jnp.minimum(gu[:, :D_FF], SWIGLU_LIMIT)
        lin = jnp.clip(gu[:, D_FF:], -SWIGLU_LIMIT, SWIGLU_LIMIT)
        act = (lin + 1.0) * gate * jax.nn.sigmoid(SWIGLU_ALPHA * gate)
        y_ref[...] = _dot(act.astype(BF16), w2b_ref[...]) + b2_ref[...]

    @pl.when(c >= nu_ref[0])
    def _():
        y_ref[...] = jnp.zeros(y_ref.shape, F32)


def _experts(chunk_e, n_used, xe, w1, b1, w2, b2):
    P = xe.shape[0]
    nch = P // MOE_ROWS
    return pl.pallas_call(
        _experts_body,
        grid_spec=pltpu.PrefetchScalarGridSpec(
            num_scalar_prefetch=2,
            grid=(nch,),
            in_specs=[pl.BlockSpec((MOE_ROWS, D_MODEL), lambda c, ce, nu: (c, 0)),
                      pl.BlockSpec((None, D_MODEL, 2 * D_FF), lambda c, ce, nu: (ce[c], 0, 0)),
                      pl.BlockSpec((None, 1, 2 * D_FF), lambda c, ce, nu: (ce[c], 0, 0)),
                      pl.BlockSpec((None, D_FF, D_MODEL), lambda c, ce, nu: (ce[c], 0, 0)),
                      pl.BlockSpec((None, 1, D_MODEL), lambda c, ce, nu: (ce[c], 0, 0))],
            out_specs=pl.BlockSpec((MOE_ROWS, D_MODEL), lambda c, ce, nu: (c, 0)),
            scratch_shapes=[pltpu.VMEM((D_MODEL, 2 * D_FF), BF16), pltpu.VMEM((D_FF, D_MODEL), BF16)]),
        out_shape=jax.ShapeDtypeStruct((P, D_MODEL), F32),
        compiler_params=_cparams(("arbitrary",)),
        name="experts",
    )(chunk_e, n_used, xe, w1, b1, w2, b2)


def _final_body(dest_ref, x1_ref, rw_ref, p_ref, gple_ref, wg_ref, wp_ref, y_ref, o_ref, rows_ref, sem):
    i = pl.program_id(0)
    tm = x1_ref.shape[0]

    def copy(r, k):
        d = dest_ref[(i * tm + r) * TOP_K + k]
        return pltpu.make_async_copy(y_ref.at[pl.ds(d, 1)], rows_ref.at[k, pl.ds(r, 1)], sem)

    def start(r, c):
        for k in range(TOP_K):
            copy(r, k).start(priority=k % 2)
        return c

    def wait(r, c):
        for k in range(TOP_K):
            copy(r, k).wait()
        return c

    lax.fori_loop(0, tm, start, 0)
    lax.fori_loop(0, tm, wait, 0)
    rw = rw_ref[...]
    x2 = x1_ref[...]
    for k in range(TOP_K):
        x2 = x2 + rw[:, k:k + 1] * rows_ref[k]
    hn = x2 * lax.rsqrt(jnp.mean(x2 * x2, axis=-1, keepdims=True) + EPS) * gple_ref[...]
    gate = jax.nn.sigmoid(_dot(hn.astype(BF16), wg_ref[...]))
    o_ref[...] = x2 + gate * _dot(p_ref[...].astype(BF16), wp_ref[...])


def _final(dest, x1, rw, p2, gple, wg, wp, yexp, tm=128):
    S = x1.shape[0]
    row = lambda n: pl.BlockSpec((tm, n), lambda i, d: (i, 0))
    full = lambda a: pl.BlockSpec(a.shape, lambda i, d: (0,) * a.ndim)
    return pl.pallas_call(
        _final_body,
        grid_spec=pltpu.PrefetchScalarGridSpec(
            num_scalar_prefetch=1,
            grid=(S // tm,),
            in_specs=[row(D_MODEL), row(LANES), row(PLE_DIM), full(gple), full(wg), full(wp),
                      pl.BlockSpec(memory_space=pl.ANY)],
            out_specs=row(D_MODEL),
            scratch_shapes=[pltpu.VMEM((TOP_K, tm, D_MODEL), F32), pltpu.SemaphoreType.DMA(())]),
        out_shape=jax.ShapeDtypeStruct((S, D_MODEL), F32),
        compiler_params=_cparams(("arbitrary",)),
        name="final",
    )(dest, x1, rw, p2, gple, wg, wp, yexp)


def _distance_bias(rel_cols):
    return (rel_cols[_rel_bucket_table(NEAR)] - rel_cols[REL_BUCKETS - 1][None, :]).T * LOG2E


def _near_bias_t(rel_cols):
    h = rel_cols.shape[1]
    g = jnp.pad(_distance_bias(rel_cols), ((0, 0), (Q_BLOCK, Q_BLOCK)))
    period = 2 * NEAR
    skew = jnp.tile(g, (1, NEAR))[:, :NEAR * (period - 1)].reshape(h, NEAR, period - 1)
    return skew[:, :, NEAR:NEAR + Q_BLOCK]


def _cmp_bias_t(rel_cols):
    g = jnp.pad(_distance_bias(rel_cols), ((0, 0), (NEAR, Q_BLOCK)))
    rows = []
    for j in range(4):
        for k in range(4):
            c0 = -NSA_CMP_STRIDE * j - (NSA_CMP_LEN - 1) + NSA_SLC_BLOCK * (k - 1)
            rows.append(g[:, c0 + NEAR:c0 + NEAR + Q_BLOCK])
    return jnp.stack(rows, axis=1)


def kernel(x, p, w_in, g_mix, g_q_nsa, g_k_nsa, g_q_dsa, g_k_dsa, pe_ck, w_ck1, w_ck2, pe_cv, w_cv1, w_cv2,
           w_branch_a, w_branch_b, w_out, rel_bias, g_ffn, w_router, b_router, w_e1, b_e1, w_e2, b_e2,
           g_ple, w_ple, w_ple_gate):
    B, S, D = x.shape
    assert B == 1 and D == D_MODEL and S % KEY_TILE == 0 and w_in.shape[0] == 1
    G, R = NSA_KV_GROUPS, NSA_GROUP_SIZE
    x2 = x.reshape(S, D)

    perm = np.concatenate([np.arange(0, 1280), np.arange(1304, 3096), np.arange(3136, 5184),
                           np.arange(1280, 1304), np.arange(3096, 3136)])
    w_perm = jnp.pad(w_in[0][:, perm], ((0, 0), (0, D_IN_PAD - perm.size))).astype(BF16)
    two = lambda g: jnp.tile(g.reshape(1, HEAD_DIM), (1, 2))
    qscale = HEAD_DIM ** -0.5 * LOG2E
    gains = jnp.concatenate([two(g_q_nsa[0]) * qscale, two(g_k_nsa[0]),
                             two(g_q_dsa[0]) * qscale, two(g_k_dsa[0])], axis=0)
    (nq, cmpraw, kslc, vslc, kwin, vwin, dq, dk, dv, iq, gab, small) = _inproj(
        x2, g_mix[0].reshape(1, D), w_perm, gains)

    NS = S // NSA_SLC_BLOCK
    nchunk = S // NSA_CMP_STRIDE
    xc = cmpraw.reshape(nchunk, NSA_CMP_STRIDE, 2 * G, HEAD_DIM).transpose(2, 0, 1, 3).reshape(
        2 * G, nchunk, NSA_CMP_STRIDE * HEAD_DIM)
    planes = lambda a: a.reshape(2 * G, NS, 4, -1).transpose(0, 2, 1, 3).reshape(2 * G, nchunk, -1)
    xa = planes(xc)
    xb = planes(jnp.roll(xc, -1, axis=1))
    half = NSA_CMP_STRIDE * HEAD_DIM
    w1 = jnp.stack([w_ck1[0], w_cv1[0]]).reshape(2, 2 * half, HEAD_DIM)
    pe = jnp.stack([pe_ck[0], pe_cv[0]]).reshape(2, 2 * half, 1)
    w2 = jnp.stack([w_ck2[0], w_cv2[0]]).astype(BF16)
    kvc = _compress(xa, xb, w1[:, :half].astype(BF16), w1[:, half:].astype(BF16), w1, pe, w2,
                    g_k_nsa[0].reshape(1, HEAD_DIM))

    rel_nsa = rel_bias[:, :NSA_HEADS]
    rel_dsa = rel_bias[:, NSA_HEADS:]
    nqt = nq.T
    oct, selt = _cmp(nqt, kvc[:G], kvc[G:].transpose(0, 2, 1), _cmp_bias_t(rel_nsa), S)
    kgroups = lambda a: jnp.pad(a.reshape(S, G, HEAD_DIM).transpose(1, 0, 2),
                                ((0, 0), (NSA_PADF, TAIL_PAD), (0, 0)))
    vgroups = lambda a: jnp.pad(a.reshape(S, G, HEAD_DIM).transpose(1, 2, 0),
                                ((0, 0), (0, 0), (NSA_PADF, TAIL_PAD)))
    glt = small[:, SMALL_GATE:SMALL_GATE + 3 * NSA_HEADS].reshape(S, G, 3 * R).transpose(1, 2, 0)
    glt = jnp.pad(glt, ((0, 0), (0, 16 - 3 * R), (0, 0)))
    o_at = _slcwin(nqt, kgroups(kslc), vgroups(vslc), kgroups(kwin), vgroups(vwin), selt,
                   _near_bias_t(rel_nsa), oct, glt, S)

    dkp = jnp.pad(dk, ((DSA_PADF, TAIL_PAD), (0, 0)))
    dvtp = jnp.pad(dv.T, ((0, 0), (DSA_PADF, TAIL_PAD)))
    ikt = jnp.pad(small[:, SMALL_IK:SMALL_IK + IDX_DIM].T.astype(BF16), ((0, 0), (0, 2 * KEY_TILE)))
    iwt = small[:, SMALL_IW:SMALL_IW + IDX_HEADS].T
    ltri = jnp.asarray(np.tril(np.ones((KEY_TILE, KEY_TILE), np.float32), -1), BF16)
    o_bt = _dsa(dq.T, dkp, dvtp, iq.T, ikt, iwt, _near_bias_t(rel_dsa), ltri, S)

    wr = jnp.pad(w_router[0], ((0, 0), (0, LANES - N_EXPERTS)))
    br = jnp.pad(b_router[0].reshape(1, N_EXPERTS), ((0, 0), (0, LANES - N_EXPERTS)), constant_values=-1e30)
    x1, h2, ridx, rw = _merge(x2, o_at.T, o_bt.T, gab, w_branch_a[0].astype(BF16), w_branch_b[0].astype(BF16),
                              w_out[0].astype(BF16), g_ffn[0].reshape(1, D), wr, br)

    rank, cnt = _rank(ridx)
    counts = cnt[0, :N_EXPERTS]
    padded = (counts + MOE_ROWS - 1) // MOE_ROWS * MOE_ROWS
    pad_end = jnp.cumsum(padded)
    pad_start = pad_end - padded
    eidx = ridx[:, :TOP_K]
    dest = (jnp.sum(jnp.where(eidx[..., None] == jnp.arange(N_EXPERTS), pad_start, 0), axis=-1)
            + rank[:, :TOP_K]).reshape(-1).astype(I32)
    n_chunks = -(-(S * TOP_K) // MOE_ROWS) + N_EXPERTS
    chunk_start = jnp.arange(n_chunks, dtype=I32) * MOE_ROWS
    chunk_e = jnp.minimum(jnp.sum(chunk_start[:, None] >= pad_end[None, :], axis=-1), N_EXPERTS - 1).astype(I32)
    n_used = (pad_end[-1] // MOE_ROWS).astype(I32).reshape(1)
    xe = _dispatch(dest, h2, jnp.zeros((n_chunks * MOE_ROWS, D), F32))
    yexp = _experts(chunk_e, n_used, xe, w_e1[0], b_e1[0].reshape(N_EXPERTS, 1, 2 * D_FF),
                    w_e2[0], b_e2[0].reshape(N_EXPERTS, 1, D))

    out = _final(dest, x1, rw, p[0].reshape(S, PLE_DIM), g_ple[0].reshape(1, D),
                 w_ple_gate[0].astype(BF16), w_ple[0].astype(BF16), yexp)
    return out.reshape(B, S, D)
```

```python
import functools
import math

import numpy as np
import jax
import jax.numpy as jnp
from jax import lax
from jax.experimental import pallas as pl
from jax.experimental.pallas import tpu as pltpu

F32 = jnp.float32
BF16 = jnp.bfloat16
I32 = jnp.int32

D_MODEL = 1024
PLE_DIM = 256
HEAD_DIM = 64
NSA_HEADS = 8
NSA_KV_GROUPS = 2
NSA_GROUP_SIZE = NSA_HEADS // NSA_KV_GROUPS
NSA_CMP_LEN = 32
NSA_CMP_STRIDE = 16
NSA_SLC_BLOCK = 64
NSA_SLC_TOPK = 16
NSA_LOCAL_BLOCKS = 2
NSA_WINDOW = 512
DSA_HEADS = 8
IDX_HEADS = 8
IDX_DIM = 32
DSA_TOPK_MAX = 256
Q_BLOCK = 128
REL_BUCKETS = 32
REL_MAX_EXACT = 16
REL_MAX_DIST = 128
N_EXPERTS = 32
TOP_K = 4
D_FF = 1024
SWIGLU_LIMIT = 7.0
SWIGLU_ALPHA = 1.702
MOE_ROWS = 256
EPS = 1e-6
LOG2E = math.log2(math.e)

NSA_W = NSA_HEADS * HEAD_DIM
DSA_W = DSA_HEADS * HEAD_DIM

LANES = 128
SUBLANES = 8
KEY_TILE = 512
NEAR = 2 * Q_BLOCK
NSA_PADF = NSA_WINDOW
DSA_PADF = Q_BLOCK
TAIL_PAD = 3 * KEY_TILE
VMEM_LIMIT = 56 * 1024 * 1024
ROW_DMA_UNROLL = 8

OFF_NQ, OFF_CMP, OFF_KSLC, OFF_VSLC, OFF_KWIN, OFF_VWIN = 0, 512, 768, 896, 1024, 1152
OFF_DQ, OFF_DK, OFF_DV, OFF_IQ, OFF_GAB, OFF_SMALL = 1280, 1792, 2304, 2816, 3072, 5120
D_IN_PAD = 5248
SMALL_GATE, SMALL_IK, SMALL_IW = 0, 24, 56


def _rel_bucket_table(n):
    d = np.arange(n)
    nf = np.maximum(d, 1).astype(np.float64)
    large = REL_MAX_EXACT + (np.log(nf / REL_MAX_EXACT) / math.log(REL_MAX_DIST / REL_MAX_EXACT)
                             * (REL_BUCKETS - REL_MAX_EXACT)).astype(np.int64)
    large = np.minimum(large, REL_BUCKETS - 1)
    return np.where(d < REL_MAX_EXACT, d, large)


def _cparams(sem, vmem=VMEM_LIMIT):
    return pltpu.CompilerParams(dimension_semantics=sem, vmem_limit_bytes=vmem)


def _dot(a, b):
    return jnp.dot(a, b, preferred_element_type=F32)


def _dot_tn(a, b):
    return lax.dot_general(a, b, (((0,), (0,)), ((), ())), preferred_element_type=F32)


def _inproj_body(x_ref, gmix_ref, w_ref, gains_ref,
                 nq_ref, cmp_ref, kslc_ref, vslc_ref, kwin_ref, vwin_ref,
                 dq_ref, dk_ref, dv_ref, iq_ref, gab_ref, small_ref):
    x = x_ref[...]
    h = x * lax.rsqrt(jnp.mean(x * x, axis=-1, keepdims=True) + EPS) * gmix_ref[...]
    hb = h.astype(BF16)
    lo = lax.broadcasted_iota(I32, (1, LANES), 1) < HEAD_DIM

    def mm(off, n):
        return _dot(hb, w_ref[:, off:off + n])

    def headnorm(y, g):
        y2 = y * y
        s_lo = jnp.sum(jnp.where(lo, y2, 0.0), axis=-1, keepdims=True)
        s_hi = jnp.sum(jnp.where(lo, 0.0, y2), axis=-1, keepdims=True)
        r = jnp.where(lo, lax.rsqrt(s_lo * (1.0 / HEAD_DIM) + EPS), lax.rsqrt(s_hi * (1.0 / HEAD_DIM) + EPS))
        return y * r * g

    g_qn, g_kn, g_qd, g_kd = (gains_ref[k:k + 1, :] for k in range(4))
    for c in range(NSA_W // LANES):
        nq_ref[:, c * LANES:(c + 1) * LANES] = headnorm(mm(OFF_NQ + c * LANES, LANES), g_qn).astype(BF16)
    cmp_ref[...] = mm(OFF_CMP, 256).astype(BF16)
    kslc_ref[...] = headnorm(mm(OFF_KSLC, LANES), g_kn).astype(BF16)
    vslc_ref[...] = mm(OFF_VSLC, LANES).astype(BF16)
    kwin_ref[...] = headnorm(mm(OFF_KWIN, LANES), g_kn).astype(BF16)
    vwin_ref[...] = mm(OFF_VWIN, LANES).astype(BF16)
    for c in range(DSA_W // LANES):
        dq_ref[:, c * LANES:(c + 1) * LANES] = headnorm(mm(OFF_DQ + c * LANES, LANES), g_qd).astype(BF16)
        dk_ref[:, c * LANES:(c + 1) * LANES] = headnorm(mm(OFF_DK + c * LANES, LANES), g_kd).astype(BF16)
    dv_ref[...] = mm(OFF_DV, DSA_W).astype(BF16)
    iq_ref[...] = (mm(OFF_IQ, IDX_HEADS * IDX_DIM) * (IDX_DIM ** -0.5)).astype(BF16)
    gab_ref[...] = jax.nn.sigmoid(mm(OFF_GAB, 2 * D_MODEL))
    small_ref[...] = mm(OFF_SMALL, LANES)


def _inproj(x2, g_mix, w_perm, gains, tm=256):
    S = x2.shape[0]
    row = lambda n: pl.BlockSpec((tm, n), lambda i: (i, 0))
    full = lambda a: pl.BlockSpec(a.shape, lambda i: (0,) * a.ndim)
    widths = [(NSA_W, BF16), (256, BF16), (LANES, BF16), (LANES, BF16), (LANES, BF16), (LANES, BF16),
              (DSA_W, BF16), (DSA_W, BF16), (DSA_W, BF16), (IDX_HEADS * IDX_DIM, BF16),
              (2 * D_MODEL, F32), (LANES, F32)]
    return pl.pallas_call(
        _inproj_body,
        grid=(S // tm,),
        in_specs=[row(D_MODEL), full(g_mix), full(w_perm), full(gains)],
        out_specs=[row(n) for n, _ in widths],
        out_shape=[jax.ShapeDtypeStruct((S, n), dt) for n, dt in widths],
        compiler_params=_cparams(("arbitrary",)),
        name="inproj",
    )(x2, g_mix, w_perm, gains)


def _compress_body(xa_ref, xb_ref, w1a_ref, w1b_ref, w1f_ref, pe_ref, w2_ref, gk_ref, o_ref):
    is_k = pl.program_id(0) < NSA_KV_GROUPS
    pe_term = jnp.sum(pe_ref[...] * w1f_ref[...], axis=0, keepdims=True)
    pre = _dot(xa_ref[...], w1a_ref[...]) + _dot(xb_ref[...], w1b_ref[...]) + pe_term
    hdn = pre * jax.nn.sigmoid(pre)
    out = _dot(hdn.astype(BF16), w2_ref[...])
    normed = out * lax.rsqrt(jnp.mean(out * out, axis=-1, keepdims=True) + EPS) * gk_ref[...]
    o_ref[...] = jnp.where(is_k, normed, out).astype(BF16)


def _compress(xa, xb, w1a, w1b, w1f, pe, w2, gk):
    n4, ncp, kdim = xa.shape
    G = NSA_KV_GROUPS
    kv = lambda i: i // G
    return pl.pallas_call(
        _compress_body,
        grid=(n4,),
        in_specs=[pl.BlockSpec((None, ncp, kdim), lambda i: (i, 0, 0)),
                  pl.BlockSpec((None, ncp, kdim), lambda i: (i, 0, 0)),
                  pl.BlockSpec((None, kdim, HEAD_DIM), lambda i: (kv(i), 0, 0)),
                  pl.BlockSpec((None, kdim, HEAD_DIM), lambda i: (kv(i), 0, 0)),
                  pl.BlockSpec((None, 2 * kdim, HEAD_DIM), lambda i: (kv(i), 0, 0)),
                  pl.BlockSpec((None, 2 * kdim, 1), lambda i: (kv(i), 0, 0)),
                  pl.BlockSpec((None, HEAD_DIM, HEAD_DIM), lambda i: (kv(i), 0, 0)),
                  pl.BlockSpec((1, HEAD_DIM), lambda i: (0, 0))],
        out_specs=pl.BlockSpec((None, ncp, HEAD_DIM), lambda i: (i, 0, 0)),
        out_shape=jax.ShapeDtypeStruct((n4, ncp, HEAD_DIM), BF16),
        compiler_params=_cparams(("arbitrary",)),
        name="compress",
    )(xa, xb, w1a, w1b, w1f, pe, w2, gk)


def _cmp_body(qt_ref, kc_ref, vct_ref, bt_ref, oct_ref, selt_ref, s_ref, *, NS):
    i = pl.program_id(1)
    t0 = i * Q_BLOCK
    T = Q_BLOCK
    NCP = 4 * NS
    R = NSA_GROUP_SIZE
    row = lax.broadcasted_iota(I32, (NCP, 1), 0)
    plane = row // NS
    nblk = row - plane * NS
    cend = NSA_SLC_BLOCK * nblk + NSA_CMP_STRIDE * plane + (NSA_CMP_LEN - 1)
    tq = t0 + lax.broadcasted_iota(I32, (1, T), 1)
    mask = cend <= tq
    kc = kc_ref[...]
    vct = vct_ref[...]
    WIN = 2 * SUBLANES
    n_lo = 2 * i - 2
    wstart = jnp.clip(n_lo // SUBLANES * SUBLANES, 0, NS - WIN)
    wrow = lax.broadcasted_iota(I32, (WIN, 1), 0)
    psum = jnp.zeros((NCP, T), F32)
    for r in range(R):
        s_ref[r] = _dot(kc, qt_ref[r * HEAD_DIM:(r + 1) * HEAD_DIM, :])
    for r in range(R):
        bt = bt_ref[r]
        for j in range(4):
            add = jnp.zeros((WIN, T), F32)
            for k in range(4):
                n = 2 * i + 1 - k
                add = add + jnp.where((wrow == n - wstart) & (n >= 0), bt[4 * j + k:4 * j + k + 1, :], 0.0)
            sl = pl.ds(pl.multiple_of(j * NS + wstart, SUBLANES), WIN)
            s_ref[r, sl, :] = s_ref[r, sl, :] + add
        s = jnp.where(mask, s_ref[r], -jnp.inf)
        m = jnp.maximum(jnp.max(s, axis=0, keepdims=True), -1e30)
        e = jnp.exp2(s - m)
        pc = e / jnp.maximum(jnp.sum(e, axis=0, keepdims=True), 1e-30)
        oct_ref[r * HEAD_DIM:(r + 1) * HEAD_DIM, :] = _dot(vct, pc.astype(BF16))
        psum = psum + pc
    p0, p1, p2, p3 = (psum[j * NS:(j + 1) * NS, :] for j in range(4))
    n_ar = lax.broadcasted_iota(I32, (NS, 1), 0)
    p3s = jnp.where(n_ar == 0, 0.0, pltpu.roll(p3, 1, axis=0))
    imp = p0 + p1 + p2 + 0.5 * p3 + 0.5 * p3s
    cur = tq // NSA_SLC_BLOCK
    valid = NSA_SLC_BLOCK * n_ar <= tq
    forced = valid & ((n_ar == 0) | (n_ar > cur - NSA_LOCAL_BLOCKS))
    score = jnp.where(forced, 1e30, jnp.where(valid, imp, -1.0))
    sel = jnp.zeros((NS, T), jnp.bool_)
    for _ in range(min(NSA_SLC_TOPK, NS)):
        mx = jnp.max(score, axis=0, keepdims=True)
        first = jnp.min(jnp.where(score == mx, n_ar, NS), axis=0, keepdims=True)
        pick = n_ar == first
        sel = sel | pick
        score = jnp.where(pick, -2.0, score)
    selb = jnp.where(sel, 1.0, 0.0).astype(BF16)
    if NS < LANES:
        selb = jnp.concatenate([selb, jnp.zeros((LANES - NS, T), BF16)], axis=0)
    for w in range(selt_ref.shape[0]):
        selt_ref[w] = selb[w * LANES:(w + 1) * LANES, :]


def _cmp(nqt, kc, vct, bt, S):
    NS = S // NSA_SLC_BLOCK
    NCP = 4 * NS
    NW = max(NS // LANES, 1)
    G, R = NSA_KV_GROUPS, NSA_GROUP_SIZE
    return pl.pallas_call(
        functools.partial(_cmp_body, NS=NS),
        grid=(G, S // Q_BLOCK),
        in_specs=[pl.BlockSpec((R * HEAD_DIM, Q_BLOCK), lambda g, i: (g, i)),
                  pl.BlockSpec((None, NCP, HEAD_DIM), lambda g, i: (g, 0, 0)),
                  pl.BlockSpec((None, HEAD_DIM, NCP), lambda g, i: (g, 0, 0)),
                  pl.BlockSpec((R, 16, Q_BLOCK), lambda g, i: (g, 0, 0))],
        out_specs=[pl.BlockSpec((R * HEAD_DIM, Q_BLOCK), lambda g, i: (g, i)),
                   pl.BlockSpec((None, NW, LANES, Q_BLOCK), lambda g, i: (g, 0, 0, i))],
        out_shape=[jax.ShapeDtypeStruct((NSA_W, S), F32),
                   jax.ShapeDtypeStruct((G, NW, LANES, S), BF16)],
        scratch_shapes=[pltpu.VMEM((R, NCP, Q_BLOCK), F32)],
        compiler_params=_cparams(("arbitrary", "arbitrary")),
        name="cmp",
    )(nqt, kc, vct, bt)


def _flash_init(m_ref, l_ref, acc_ref):
    m_ref[...] = jnp.full(m_ref.shape, -1e30, F32)
    l_ref[...] = jnp.zeros(l_ref.shape, F32)
    acc_ref[...] = jnp.zeros(acc_ref.shape, F32)


def _flash_update(s, vt, m_ref, l_ref, acc_ref):
    m_old = m_ref[...]
    m_new = jnp.maximum(m_old, jnp.max(s, axis=0, keepdims=True))
    alpha = jnp.exp2(m_old - m_new)
    p = jnp.exp2(s - m_new)
    l_ref[...] = alpha * l_ref[...] + jnp.sum(p, axis=0, keepdims=True)
    acc_ref[...] = alpha * acc_ref[...] + _dot(vt, p.astype(BF16))
    m_ref[...] = m_new


def _slcwin_body(qt_ref, ks_ref, vst_ref, kw_ref, vwt_ref, selt_ref, tbt_ref, oct_ref, glt_ref, ot_ref,
                 ms_ref, ls_ref, as_ref, mw_ref, lw_ref, aw_ref, sbuf_ref):
    i = pl.program_id(1)
    t0 = i * Q_BLOCK
    R = NSA_GROUP_SIZE
    T = Q_BLOCK
    q4t = jnp.concatenate([qt_ref[r * HEAD_DIM:(r + 1) * HEAD_DIM, :] for r in range(R)], axis=1)
    tq = lax.broadcasted_iota(I32, (1, T), 1)
    rep = lambda mk: jnp.concatenate([mk] * R, axis=1)
    tb4 = jnp.concatenate([tbt_ref[r] for r in range(R)], axis=1)

    def expand(w, base, nkeys):
        keyi = lax.broadcasted_iota(I32, (nkeys, LANES), 0)
        blk = lax.broadcasted_iota(I32, (nkeys, LANES), 1)
        e = jnp.where(blk == base + keyi // NSA_SLC_BLOCK, 1.0, 0.0).astype(BF16)
        return _dot(e, selt_ref[w])

    _flash_init(ms_ref, ls_ref, as_ref)
    _flash_init(mw_ref, lw_ref, aw_ref)
    far_end = t0 - Q_BLOCK
    nfar = jnp.maximum(far_end + KEY_TILE - 1, 0) // KEY_TILE
    blocks_per_tile = KEY_TILE // NSA_SLC_BLOCK
    key512 = lax.broadcasted_iota(I32, (KEY_TILE, 1), 0)
    last_w = selt_ref.shape[0] - 1

    def score_far(kt, slot):
        r0 = pl.multiple_of(NSA_PADF + kt * KEY_TILE, LANES)
        s = _dot(ks_ref[pl.ds(r0, KEY_TILE), :], q4t)
        b0 = kt * blocks_per_tile
        mexp = expand(jnp.minimum(b0 // LANES, last_w), b0 % LANES, KEY_TILE)
        mask = (mexp > 0.5) & (kt * KEY_TILE + key512 < far_end)
        sbuf_ref[slot] = jnp.where(rep(mask), s, -jnp.inf)

    def update_far(kt, slot):
        r0 = pl.multiple_of(NSA_PADF + kt * KEY_TILE, LANES)
        _flash_update(sbuf_ref[slot], vst_ref[:, pl.ds(r0, KEY_TILE)], ms_ref, ls_ref, as_ref)

    score_far(0, 0)

    def far2(k2, carry):
        kt = 2 * k2
        score_far(kt + 1, 1)
        update_far(kt, 0)
        score_far(kt + 2, 0)
        update_far(kt + 1, 1)
        return carry

    lax.fori_loop(0, (nfar + 1) // 2, far2, 0)

    keyn = lax.broadcasted_iota(I32, (NEAR, 1), 0)
    near_ok = (keyn - Q_BLOCK <= tq) & (keyn + t0 - Q_BLOCK >= 0)
    rn = pl.multiple_of(t0 + NSA_PADF - Q_BLOCK, LANES)
    bp = jnp.maximum(2 * i - 2, 0)
    bd = 2 * i
    mexp = jnp.concatenate([expand(bp // LANES, bp % LANES, Q_BLOCK),
                            expand(bd // LANES, bd % LANES, Q_BLOCK)], axis=0)
    WF = NSA_WINDOW - Q_BLOCK
    rw = pl.multiple_of(t0 + NSA_PADF - NSA_WINDOW, LANES)
    keyw = lax.broadcasted_iota(I32, (WF, 1), 0)
    maskw = (keyw > tq) & (keyw + t0 - NSA_WINDOW >= 0)
    s_sn = jnp.where(rep((mexp > 0.5) & near_ok), _dot(ks_ref[pl.ds(rn, NEAR), :], q4t) + tb4, -jnp.inf)
    s_wf = jnp.where(rep(maskw), _dot(kw_ref[pl.ds(rw, WF), :], q4t), -jnp.inf)
    s_wn = jnp.where(rep(near_ok), _dot(kw_ref[pl.ds(rn, NEAR), :], q4t) + tb4, -jnp.inf)
    _flash_update(s_sn, vst_ref[:, pl.ds(rn, NEAR)], ms_ref, ls_ref, as_ref)
    _flash_update(s_wf, vwt_ref[:, pl.ds(rw, WF)], mw_ref, lw_ref, aw_ref)
    _flash_update(s_wn, vwt_ref[:, pl.ds(rn, NEAR)], mw_ref, lw_ref, aw_ref)
    o_s = as_ref[...] / jnp.maximum(ls_ref[...], 1e-30)
    o_w = aw_ref[...] / jnp.maximum(lw_ref[...], 1e-30)

    gates = jax.nn.sigmoid(glt_ref[...])
    for r in range(R):
        hs = slice(r * HEAD_DIM, (r + 1) * HEAD_DIM)
        qs = slice(r * T, (r + 1) * T)
        o = (gates[3 * r:3 * r + 1, :] * oct_ref[hs, :]
             + gates[3 * r + 1:3 * r + 2, :] * o_s[:, qs]
             + gates[3 * r + 2:3 * r + 3, :] * o_w[:, qs])
        ot_ref[hs, :] = o.astype(BF16)


def _slcwin(nqt, ks, vst, kw, vwt, selt, tbt, oct, glt, S):
    G, R = NSA_KV_GROUPS, NSA_GROUP_SIZE
    spad = ks.shape[1]
    NW = selt.shape[1]
    kspec = pl.BlockSpec((None, spad, HEAD_DIM), lambda g, i: (g, 0, 0))
    vspec = pl.BlockSpec((None, HEAD_DIM, spad), lambda g, i: (g, 0, 0))
    qspec = pl.BlockSpec((R * HEAD_DIM, Q_BLOCK), lambda g, i: (g, i))
    st = lambda n: pltpu.VMEM((n, R * Q_BLOCK), F32)
    return pl.pallas_call(
        _slcwin_body,
        grid=(G, S // Q_BLOCK),
        in_specs=[qspec, kspec, vspec, kspec, vspec,
                  pl.BlockSpec((None, NW, LANES, Q_BLOCK), lambda g, i: (g, 0, 0, i)),
                  pl.BlockSpec((R, NEAR, Q_BLOCK), lambda g, i: (g, 0, 0)),
                  qspec,
                  pl.BlockSpec((None, 16, Q_BLOCK), lambda g, i: (g, 0, i))],
        out_specs=qspec,
        out_shape=jax.ShapeDtypeStruct((NSA_W, S), BF16),
        scratch_shapes=[st(1), st(1), st(HEAD_DIM), st(1), st(1), st(HEAD_DIM),
                        pltpu.VMEM((2, KEY_TILE, R * Q_BLOCK), F32)],
        compiler_params=_cparams(("arbitrary", "arbitrary")),
        name="slcwin",
    )(nqt, ks, vst, kw, vwt, selt, tbt, oct, glt)


INT_MIN = -2 ** 31
INT_MAX = 2 ** 31 - 1
MAX_PEEL_ROUNDS = 64
UNTESTED_BITS = 14
NEG_INF_KEY = int(np.array(-np.inf, np.float32).view(np.int32)) ^ 0x7FFFFFFF


def _dsa_body(dqt_ref, dk_ref, dvt_ref, iqt_ref, ikt_ref, iwt_ref, tbt_ref, ltri_ref, ot_ref,
              keys_ref, m_ref, l_ref, acc_ref, sbuf_ref, *, S, KEEP):
    i = pl.program_id(0)
    t0 = i * Q_BLOCK
    T = Q_BLOCK
    H = DSA_HEADS
    ntile = (t0 + Q_BLOCK + KEY_TILE - 1) // KEY_TILE
    tq = t0 + lax.broadcasted_iota(I32, (1, T), 1)
    key512 = lax.broadcasted_iota(I32, (KEY_TILE, 1), 0)
    ktile = lambda kt: keys_ref[pl.ds(pl.multiple_of(Q_BLOCK + kt * KEY_TILE, LANES), KEY_TILE), :]

    keys_ref[0:Q_BLOCK, :] = jnp.full((Q_BLOCK, T), NEG_INF_KEY, I32)
    iqt = iqt_ref[...]
    iwt = iwt_ref[...] * (IDX_HEADS ** -0.5)
    qis = [jnp.concatenate([iqt[(2 * j) * IDX_DIM:(2 * j + 1) * IDX_DIM, :],
                            iqt[(2 * j + 1) * IDX_DIM:(2 * j + 2) * IDX_DIM, :]], axis=1)
           for j in range(IDX_HEADS // 2)]
    wrow = [iwt[h:h + 1, :] for h in range(IDX_HEADS)]

    def idx_dots(kt, slot):
        ik = ikt_ref[:, pl.ds(pl.multiple_of(kt * KEY_TILE, KEY_TILE), KEY_TILE)]
        for j in range(IDX_HEADS // 2):
            sbuf_ref[slot, j] = _dot_tn(ik, qis[j])

    def idx_keys(kt, slot):
        c0 = pl.multiple_of(kt * KEY_TILE, KEY_TILE)
        sc = jnp.zeros((KEY_TILE, T), F32)
        for j in range(IDX_HEADS // 2):
            d = jnp.maximum(sbuf_ref[slot, j], 0.0)
            sc = sc + wrow[2 * j] * d[:, 0:T]
            sc = sc + wrow[2 * j + 1] * d[:, T:2 * T]
        sc = jnp.where(sc == 0.0, 0.0, sc)
        sc = jnp.where(c0 + key512 <= tq, sc, -jnp.inf)
        bits = pltpu.bitcast(sc, I32)
        keys_ref[pl.ds(pl.multiple_of(Q_BLOCK + c0, LANES), KEY_TILE), :] = jnp.where(
            bits < 0, bits ^ 0x7FFFFFFF, bits)

    idx_dots(0, 0)

    def score2(k2, carry):
        kt = 2 * k2
        idx_dots(kt + 1, 1)
        idx_keys(kt, 0)
        idx_dots(kt + 2, 0)
        idx_keys(kt + 1, 1)
        return carry

    lax.fori_loop(0, (ntile + 1) // 2, score2, 0)

    def count(pred):
        def body(kt, acc):
            hit = jnp.where(pred(ktile(kt), kt * KEY_TILE + key512), 1, 0)
            return acc + jnp.sum(hit.reshape(KEY_TILE // SUBLANES, SUBLANES, T), axis=0)
        acc = lax.fori_loop(0, ntile, body, jnp.zeros((SUBLANES, T), I32))
        return jnp.sum(acc, axis=0, keepdims=True)

    def min_at_least(lowest):
        def body(kt, acc):
            kk = ktile(kt)
            v = jnp.where(kk >= lowest, kk, INT_MAX)
            return jnp.minimum(acc, jnp.min(v.reshape(KEY_TILE // SUBLANES, SUBLANES, T), axis=0))
        acc = lax.fori_loop(0, ntile, body, jnp.full((SUBLANES, T), INT_MAX, I32))
        return jnp.min(acc, axis=0, keepdims=True)

    def any_open(st, over):
        _, excess, fin, _ = st
        return jnp.max(jnp.where((fin == 0) & (excess > over), 1, 0)) > 0

    def bisect(b, st):
        thr, excess, fin, need = st
        cand = thr + lax.shift_left(jnp.int32(1), 31 - b)
        n = count(lambda kk, ab: kk >= cand)
        ok = (n >= KEEP) & (fin == 0)
        return jnp.where(ok, cand, thr), jnp.where(ok, n - KEEP, excess), fin, need

    def peel(st):
        thr, excess, fin, need = st
        low = min_at_least(thr)
        n_low = count(lambda kk, ab: kk == low)
        active = (fin == 0) & (excess > 0)
        tied = active & (n_low > excess)
        drop = active & (n_low <= excess)
        return (jnp.where(tied, low, jnp.where(drop, low + 1, thr)), jnp.where(drop, excess - n_low, excess),
                jnp.where(tied, 1, fin), jnp.where(tied, n_low - excess, need))

    zero = jnp.zeros((1, T), I32)
    st = (jnp.full((1, T), INT_MIN, I32), KEY_TILE * ntile - KEEP + zero, zero, zero)
    st = peel(bisect(0, st))
    st = lax.fori_loop(1, UNTESTED_BITS, bisect, st)
    _, st = lax.while_loop(lambda c: (c[0] < 32) & any_open(c[1], 1),
                           lambda c: (c[0] + 1, bisect(c[0], c[1])), (jnp.int32(UNTESTED_BITS), st))
    _, st = lax.while_loop(lambda c: (c[0] < MAX_PEEL_ROUNDS) & any_open(c[1], 0),
                           lambda c: (c[0] + 1, peel(c[1])), (jnp.int32(0), st))
    thr, _, fin, need = st
    tie = (fin == 1) & (thr > NEG_INF_KEY)

    def tie_cut():
        needf = need.astype(F32)

        def body(kt, c):
            before, jmax = c
            is_tie = ktile(kt) == thr
            tied = jnp.where(is_tie, 1.0, 0.0)
            rank = before + _dot(ltri_ref[...], tied.astype(BF16))
            admit = is_tie & (rank < needf)
            jm = jnp.max(jnp.where(admit, kt * KEY_TILE + key512, -1), axis=0, keepdims=True)
            return before + jnp.sum(tied, axis=0, keepdims=True), jnp.maximum(jmax, jm)

        _, jmax = lax.fori_loop(0, ntile, body, (jnp.zeros((1, T), F32), jnp.full((1, T), -1, I32)))
        return jnp.where(tie, jmax + 1, S)

    jcut = lax.cond(jnp.max(jnp.where(tie, 1, 0)) > 0, tie_cut, lambda: jnp.full((1, T), S, I32))

    def chosen(kk, ab):
        return (kk > thr) | ((kk == thr) & (ab < jcut))

    _flash_init(m_ref, l_ref, acc_ref)
    lo = lax.broadcasted_iota(I32, (LANES, 1), 0) < HEAD_DIM
    qpairs = []
    for pr in range(H // 2):
        qp = dqt_ref[pr * LANES:(pr + 1) * LANES, :]
        zero = jnp.zeros_like(qp)
        qpairs.append(jnp.concatenate([jnp.where(lo, qp, zero), jnp.where(lo, zero, qp)], axis=1))

    def scores(r0, nkeys, mask, bias):
        mask2 = jnp.concatenate([mask, mask], axis=1)
        out = []
        for pr in range(H // 2):
            s = _dot(dk_ref[pl.ds(r0, nkeys), pr * LANES:(pr + 1) * LANES], qpairs[pr])
            if bias is not None:
                s = s + jnp.concatenate([bias[2 * pr], bias[2 * pr + 1]], axis=1)
            out.append(jnp.where(mask2, s, -jnp.inf))
        return out

    def update(pr, s, r0, nkeys):
        m_old = m_ref[pr]
        m_new = jnp.maximum(m_old, jnp.max(s, axis=0, keepdims=True))
        alpha = jnp.exp2(m_old - m_new)
        p = jnp.exp2(s - m_new)
        l_ref[pr] = alpha * l_ref[pr] + jnp.sum(p, axis=0, keepdims=True)
        m_ref[pr] = m_new
        vt = dvt_ref[pr * LANES:(pr + 1) * LANES, pl.ds(r0, nkeys)]
        acc_ref[pr] = alpha * acc_ref[pr] + _dot(vt, p.astype(BF16))

    far_end = t0 - Q_BLOCK
    nfar = jnp.maximum(far_end + KEY_TILE - 1, 0) // KEY_TILE
    far_row = lambda kt: pl.multiple_of(DSA_PADF + kt * KEY_TILE, LANES)

    def score_far(kt, slot):
        ab = kt * KEY_TILE + key512
        mask = chosen(ktile(jnp.minimum(kt, ntile - 1)), ab) & (ab < far_end)
        for pr, s in enumerate(scores(far_row(kt), KEY_TILE, mask, None)):
            sbuf_ref[slot, pr] = s

    def update_far(kt, slot):
        for pr in range(H // 2):
            update(pr, sbuf_ref[slot, pr], far_row(kt), KEY_TILE)

    score_far(0, 0)

    def far2(k2, carry):
        kt = 2 * k2
        score_far(kt + 1, 1)
        update_far(kt, 0)
        score_far(kt + 2, 0)
        update_far(kt + 1, 1)
        return carry

    lax.fori_loop(0, (nfar + 1) // 2, far2, 0)

    abn = lax.broadcasted_iota(I32, (NEAR, 1), 0) + t0 - Q_BLOCK
    kkn = keys_ref[pl.ds(pl.multiple_of(t0, LANES), NEAR), :]
    rn = pl.multiple_of(t0 + DSA_PADF - Q_BLOCK, LANES)
    for pr, s in enumerate(scores(rn, NEAR, chosen(kkn, abn) & (abn <= tq) & (abn >= 0), tbt_ref)):
        update(pr, s, rn, NEAR)

    for pr in range(H // 2):
        o = acc_ref[pr] / jnp.maximum(l_ref[pr], 1e-30)
        ot_ref[pr * LANES:(pr + 1) * LANES, :] = jnp.where(lo, o[:, 0:T], o[:, T:2 * T]).astype(BF16)


def _dsa(dqt, dkp, dvtp, iqt, ikt, iwt, tbt, ltri, S):
    KEEP = min(DSA_TOPK_MAX, S // 4)
    H = DSA_HEADS
    T = Q_BLOCK
    once = pl.Buffered(1)
    full = lambda a: pl.BlockSpec(a.shape, lambda i: (0,) * a.ndim, pipeline_mode=once)
    return pl.pallas_call(
        functools.partial(_dsa_body, S=S, KEEP=KEEP),
        grid=(S // T,),
        in_specs=[pl.BlockSpec((DSA_W, T), lambda i: (0, i)),
                  full(dkp), full(dvtp),
                  pl.BlockSpec((IDX_HEADS * IDX_DIM, T), lambda i: (0, i)),
                  full(ikt),
                  pl.BlockSpec((IDX_HEADS, T), lambda i: (0, i)),
                  full(tbt), full(ltri)],
        out_specs=pl.BlockSpec((DSA_W, T), lambda i: (0, i)),
        out_shape=jax.ShapeDtypeStruct((DSA_W, S), BF16),
        scratch_shapes=[pltpu.VMEM((Q_BLOCK + S + KEY_TILE, T), I32),
                        pltpu.VMEM((H // 2, 1, 2 * T), F32),
                        pltpu.VMEM((H // 2, 1, 2 * T), F32),
                        pltpu.VMEM((H // 2, LANES, 2 * T), F32),
                        pltpu.VMEM((2, H // 2, KEY_TILE, 2 * T), F32)],
        compiler_params=_cparams(("arbitrary",)),
        name="dsa",
    )(dqt, dkp, dvtp, iqt, ikt, iwt, tbt, ltri)


def _split_bf16(a):
    hi = a.astype(BF16)
    return hi, (a - hi.astype(F32)).astype(BF16)


def _merge_body(x_ref, oa_ref, ob_ref, gab_ref, wa_ref, wb_ref, wo_ref, gffn_ref, wr_ref, br_ref,
                x1_ref, h2_ref, ridx_ref, rw_ref):
    a = _dot(oa_ref[...], wa_ref[...])
    b = _dot(ob_ref[...], wb_ref[...])
    gab = gab_ref[...]
    merged = gab[:, :D_MODEL] * a + gab[:, D_MODEL:] * b
    x1 = x_ref[...] + _dot(merged.astype(BF16), wo_ref[...])
    x1_ref[...] = x1
    h2 = x1 * lax.rsqrt(jnp.mean(x1 * x1, axis=-1, keepdims=True) + EPS) * gffn_ref[...]
    h2_ref[...] = h2
    hh, hl = _split_bf16(h2)
    wh, wl = _split_bf16(wr_ref[...])
    logits = _dot(hh, wh) + _dot(hh, wl) + _dot(hl, wh) + br_ref[...]
    lane = lax.broadcasted_iota(I32, (1, LANES), 1)
    idx_out = jnp.zeros(logits.shape, I32)
    val_out = jnp.full(logits.shape, -jnp.inf, F32)
    for k in range(TOP_K):
        mx = jnp.max(logits, axis=-1, keepdims=True)
        first = jnp.min(jnp.where(logits == mx, lane, LANES), axis=-1, keepdims=True)
        idx_out = jnp.where(lane == k, first, idx_out)
        val_out = jnp.where(lane == k, mx, val_out)
        logits = jnp.where(lane == first, -jnp.inf, logits)
    e = jnp.exp(val_out - jnp.max(val_out, axis=-1, keepdims=True))
    ridx_ref[...] = idx_out
    rw_ref[...] = e / jnp.sum(e, axis=-1, keepdims=True)


def _merge(x2, oa, ob, gab, wa, wb, wo, gffn, wr, br, tm=256):
    S = x2.shape[0]
    row = lambda n: pl.BlockSpec((tm, n), lambda i: (i, 0))
    full = lambda a: pl.BlockSpec(a.shape, lambda i: (0,) * a.ndim)
    return pl.pallas_call(
        _merge_body,
        grid=(S // tm,),
        in_specs=[row(D_MODEL), row(NSA_W), row(DSA_W), row(2 * D_MODEL),
                  full(wa), full(wb), full(wo), full(gffn), full(wr), full(br)],
        out_specs=[row(D_MODEL), row(D_MODEL), row(LANES), row(LANES)],
        out_shape=[jax.ShapeDtypeStruct((S, D_MODEL), F32), jax.ShapeDtypeStruct((S, D_MODEL), F32),
                   jax.ShapeDtypeStruct((S, LANES), I32), jax.ShapeDtypeStruct((S, LANES), F32)],
        compiler_params=_cparams(("arbitrary",)),
        name="merge",
    )(x2, oa, ob, gab, wa, wb, wo, gffn, wr, br)


def _rank_body(ridx_ref, rank_ref, cnt_ref, carry_ref):
    @pl.when(pl.program_id(0) == 0)
    def _():
        carry_ref[...] = jnp.zeros(carry_ref.shape, F32)

    ridx = ridx_ref[...]
    tm = ridx.shape[0]
    lane = lax.broadcasted_iota(I32, (1, LANES), 1)
    hits = [lane == ridx[:, k:k + 1] for k in range(TOP_K)]
    member = jnp.zeros((tm, LANES), F32)
    for hk in hits:
        member = member + jnp.where(hk, 1.0, 0.0)
    ri = lax.broadcasted_iota(I32, (tm, tm), 0)
    ci = lax.broadcasted_iota(I32, (tm, tm), 1)
    lower = jnp.where(ci < ri, 1.0, 0.0).astype(BF16)
    before = _dot(lower, member.astype(BF16)) + carry_ref[0:1, :]
    out = jnp.zeros((tm, LANES), I32)
    for k, hk in enumerate(hits):
        rk = jnp.sum(jnp.where(hk, before, 0.0), axis=-1, keepdims=True)
        out = jnp.where(lane == k, rk.astype(I32), out)
    rank_ref[...] = out
    total = carry_ref[0:1, :] + jnp.sum(member, axis=0, keepdims=True)
    carry_ref[...] = jnp.broadcast_to(total, carry_ref.shape)
    cnt_ref[...] = jnp.broadcast_to(total, cnt_ref.shape).astype(I32)


def _rank(ridx, tm=512):
    S = ridx.shape[0]
    return pl.pallas_call(
        _rank_body,
        grid=(S // tm,),
        in_specs=[pl.BlockSpec((tm, LANES), lambda i: (i, 0))],
        out_specs=[pl.BlockSpec((tm, LANES), lambda i: (i, 0)), pl.BlockSpec((8, LANES), lambda i: (0, 0))],
        out_shape=[jax.ShapeDtypeStruct((S, LANES), I32), jax.ShapeDtypeStruct((8, LANES), I32)],
        scratch_shapes=[pltpu.VMEM((8, LANES), F32)],
        compiler_params=_cparams(("arbitrary",)),
        name="rank",
    )(ridx)


def _dispatch_body(dest_ref, h_ref, zero_ref, xe_ref, sems, *, tm):
    del zero_ref
    i = pl.program_id(0)
    slot = i % 2

    def start(r, c):
        for k in range(TOP_K):
            d = dest_ref[(i * tm + r) * TOP_K + k]
            pltpu.make_async_copy(h_ref.at[pl.ds(i * tm + r, 1)], xe_ref.at[pl.ds(d, 1)],
                                  sems.at[slot]).start(priority=k % 2)
        return c

    lax.fori_loop(0, tm, start, 0, unroll=ROW_DMA_UNROLL)

    def drain(which):
        for _ in range(TOP_K):
            pltpu.make_async_copy(h_ref.at[pl.ds(0, tm)], xe_ref.at[pl.ds(0, tm)], sems.at[which]).wait()

    @pl.when(i > 0)
    def _():
        drain(1 - slot)

    @pl.when(i == pl.num_programs(0) - 1)
    def _():
        drain(slot)


def _dispatch(dest, h2, zeros_p, tm=128):
    S = h2.shape[0]
    return pl.pallas_call(
        functools.partial(_dispatch_body, tm=tm),
        grid_spec=pltpu.PrefetchScalarGridSpec(
            num_scalar_prefetch=1,
            grid=(S // tm,),
            in_specs=[pl.BlockSpec(memory_space=pl.ANY), pl.BlockSpec(memory_space=pl.ANY)],
            out_specs=pl.BlockSpec(memory_space=pl.ANY),
            scratch_shapes=[pltpu.SemaphoreType.DMA((2,))]),
        out_shape=jax.ShapeDtypeStruct(zeros_p.shape, zeros_p.dtype),
        input_output_aliases={2: 0},
        compiler_params=_cparams(("arbitrary",)),
        name="dispatch",
    )(dest, h2, zeros_p)


def _experts_body(ce_ref, nu_ref, x_ref, w1_ref, b1_ref, w2_ref, b2_ref, y_ref, w1b_ref, w2b_ref):
    c = pl.program_id(0)

    @pl.when((c == 0) | (ce_ref[c] != ce_ref[jnp.maximum(c - 1, 0)]))
    def _():
        w1b_ref[...] = w1_ref[...].astype(BF16)
        w2b_ref[...] = w2_ref[...].astype(BF16)

    @pl.when(c < nu_ref[0])
    def _():
        gu = _dot(x_ref[...].astype(BF16), w1b_ref[...]) + b1_ref[...]
        gate = jnp.minimum(gu[:, :D_FF], SWIGLU_LIMIT)
        lin = jnp.clip(gu[:, D_FF:], -SWIGLU_LIMIT, SWIGLU_LIMIT)
        act = (lin + 1.0) * gate * jax.nn.sigmoid(SWIGLU_ALPHA * gate)
        y_ref[...] = _dot(act.astype(BF16), w2b_ref[...]) + b2_ref[...]

    @pl.when(c >= nu_ref[0])
    def _():
        y_ref[...] = jnp.zeros(y_ref.shape, F32)


def _experts(chunk_e, n_used, xe, w1, b1, w2, b2):
    P = xe.shape[0]
    nch = P // MOE_ROWS
    return pl.pallas_call(
        _experts_body,
        grid_spec=pltpu.PrefetchScalarGridSpec(
            num_scalar_prefetch=2,
            grid=(nch,),
            in_specs=[pl.BlockSpec((MOE_ROWS, D_MODEL), lambda c, ce, nu: (c, 0)),
                      pl.BlockSpec((None, D_MODEL, 2 * D_FF), lambda c, ce, nu: (ce[c], 0, 0)),
                      pl.BlockSpec((None, 1, 2 * D_FF), lambda c, ce, nu: (ce[c], 0, 0)),
                      pl.BlockSpec((None, D_FF, D_MODEL), lambda c, ce, nu: (ce[c], 0, 0)),
                      pl.BlockSpec((None, 1, D_MODEL), lambda c, ce, nu: (ce[c], 0, 0))],
            out_specs=pl.BlockSpec((MOE_ROWS, D_MODEL), lambda c, ce, nu: (c, 0)),
            scratch_shapes=[pltpu.VMEM((D_MODEL, 2 * D_FF), BF16), pltpu.VMEM((D_FF, D_MODEL), BF16)]),
        out_shape=jax.ShapeDtypeStruct((P, D_MODEL), F32),
        compiler_params=_cparams(("arbitrary",)),
        name="experts",
    )(chunk_e, n_used, xe, w1, b1, w2, b2)


def _final_body(dest_ref, x1_ref, rw_ref, p_ref, gple_ref, wg_ref, wp_ref, y_ref, o_ref, rows_ref, sems):
    i = pl.program_id(0)
    tm = x1_ref.shape[0]

    def gather(blk, slot):
        def start(r, c):
            for k in range(TOP_K):
                d = dest_ref[(blk * tm + r) * TOP_K + k]
                pltpu.make_async_copy(y_ref.at[pl.ds(d, 1)], rows_ref.at[slot, k, pl.ds(r, 1)],
                                      sems.at[slot]).start(priority=k % 2)
            return c
        lax.fori_loop(0, tm, start, 0, unroll=ROW_DMA_UNROLL)

    @pl.when(i == 0)
    def _():
        gather(0, 0)

    @pl.when(i + 1 < pl.num_programs(0))
    def _():
        gather(i + 1, (i + 1) % 2)

    slot = i % 2
    for k in range(TOP_K):
        pltpu.make_async_copy(y_ref.at[pl.ds(0, tm)], rows_ref.at[slot, k], sems.at[slot]).wait()
    rw = rw_ref[...]
    x2 = x1_ref[...]
    for k in range(TOP_K):
        x2 = x2 + rw[:, k:k + 1] * rows_ref[slot, k]
    hn = x2 * lax.rsqrt(jnp.mean(x2 * x2, axis=-1, keepdims=True) + EPS) * gple_ref[...]
    gate = jax.nn.sigmoid(_dot(hn.astype(BF16), wg_ref[...]))
    o_ref[...] = x2 + gate * _dot(p_ref[...].astype(BF16), wp_ref[...])


def _final(dest, x1, rw, p2, gple, wg, wp, yexp, tm=128):
    S = x1.shape[0]
    row = lambda n: pl.BlockSpec((tm, n), lambda i, d: (i, 0))
    full = lambda a: pl.BlockSpec(a.shape, lambda i, d: (0,) * a.ndim)
    return pl.pallas_call(
        _final_body,
        grid_spec=pltpu.PrefetchScalarGridSpec(
            num_scalar_prefetch=1,
            grid=(S // tm,),
            in_specs=[row(D_MODEL), row(LANES), row(PLE_DIM), full(gple), full(wg), full(wp),
                      pl.BlockSpec(memory_space=pl.ANY)],
            out_specs=row(D_MODEL),
            scratch_shapes=[pltpu.VMEM((2, TOP_K, tm, D_MODEL), F32), pltpu.SemaphoreType.DMA((2,))]),
        out_shape=jax.ShapeDtypeStruct((S, D_MODEL), F32),
        compiler_params=_cparams(("arbitrary",)),
        name="final",
    )(dest, x1, rw, p2, gple, wg, wp, yexp)


def _distance_bias(rel_cols):
    return (rel_cols[_rel_bucket_table(NEAR)] - rel_cols[REL_BUCKETS - 1][None, :]).T * LOG2E


def _near_bias_t(rel_cols):
    h = rel_cols.shape[1]
    g = jnp.pad(_distance_bias(rel_cols), ((0, 0), (Q_BLOCK, Q_BLOCK)))
    period = 2 * NEAR
    skew = jnp.tile(g, (1, NEAR))[:, :NEAR * (period - 1)].reshape(h, NEAR, period - 1)
    return skew[:, :, NEAR:NEAR + Q_BLOCK]


def _cmp_bias_t(rel_cols):
    g = jnp.pad(_distance_bias(rel_cols), ((0, 0), (NEAR, Q_BLOCK)))
    rows = []
    for j in range(4):
        for k in range(4):
            c0 = -NSA_CMP_STRIDE * j - (NSA_CMP_LEN - 1) + NSA_SLC_BLOCK * (k - 1)
            rows.append(g[:, c0 + NEAR:c0 + NEAR + Q_BLOCK])
    return jnp.stack(rows, axis=1)


def kernel(x, p, w_in, g_mix, g_q_nsa, g_k_nsa, g_q_dsa, g_k_dsa, pe_ck, w_ck1, w_ck2, pe_cv, w_cv1, w_cv2,
           w_branch_a, w_branch_b, w_out, rel_bias, g_ffn, w_router, b_router, w_e1, b_e1, w_e2, b_e2,
           g_ple, w_ple, w_ple_gate):
    B, S, D = x.shape
    assert B == 1 and D == D_MODEL and S % KEY_TILE == 0 and w_in.shape[0] == 1
    G, R = NSA_KV_GROUPS, NSA_GROUP_SIZE
    x2 = x.reshape(S, D)

    perm = np.concatenate([np.arange(0, 1280), np.arange(1304, 3096), np.arange(3136, 5184),
                           np.arange(1280, 1304), np.arange(3096, 3136)])
    w_perm = jnp.pad(w_in[0][:, perm], ((0, 0), (0, D_IN_PAD - perm.size))).astype(BF16)
    two = lambda g: jnp.tile(g.reshape(1, HEAD_DIM), (1, 2))
    qscale = HEAD_DIM ** -0.5 * LOG2E
    gains = jnp.concatenate([two(g_q_nsa[0]) * qscale, two(g_k_nsa[0]),
                             two(g_q_dsa[0]) * qscale, two(g_k_dsa[0])], axis=0)
    (nq, cmpraw, kslc, vslc, kwin, vwin, dq, dk, dv, iq, gab, small) = _inproj(
        x2, g_mix[0].reshape(1, D), w_perm, gains)

    NS = S // NSA_SLC_BLOCK
    nchunk = S // NSA_CMP_STRIDE
    xc = cmpraw.reshape(nchunk, NSA_CMP_STRIDE, 2 * G, HEAD_DIM).transpose(2, 0, 1, 3).reshape(
        2 * G, nchunk, NSA_CMP_STRIDE * HEAD_DIM)
    planes = lambda a: a.reshape(2 * G, NS, 4, -1).transpose(0, 2, 1, 3).reshape(2 * G, nchunk, -1)
    xa = planes(xc)
    xb = planes(jnp.roll(xc, -1, axis=1))
    half = NSA_CMP_STRIDE * HEAD_DIM
    w1 = jnp.stack([w_ck1[0], w_cv1[0]]).reshape(2, 2 * half, HEAD_DIM)
    pe = jnp.stack([pe_ck[0], pe_cv[0]]).reshape(2, 2 * half, 1)
    w2 = jnp.stack([w_ck2[0], w_cv2[0]]).astype(BF16)
    kvc = _compress(xa, xb, w1[:, :half].astype(BF16), w1[:, half:].astype(BF16), w1, pe, w2,
                    g_k_nsa[0].reshape(1, HEAD_DIM))

    rel_nsa = rel_bias[:, :NSA_HEADS]
    rel_dsa = rel_bias[:, NSA_HEADS:]
    nqt = nq.T
    oct, selt = _cmp(nqt, kvc[:G], kvc[G:].transpose(0, 2, 1), _cmp_bias_t(rel_nsa), S)
    kgroups = lambda a: jnp.pad(a.reshape(S, G, HEAD_DIM).transpose(1, 0, 2),
                                ((0, 0), (NSA_PADF, TAIL_PAD), (0, 0)))
    vgroups = lambda a: jnp.pad(a.reshape(S, G, HEAD_DIM).transpose(1, 2, 0),
                                ((0, 0), (0, 0), (NSA_PADF, TAIL_PAD)))
    glt = small[:, SMALL_GATE:SMALL_GATE + 3 * NSA_HEADS].reshape(S, G, 3 * R).transpose(1, 2, 0)
    glt = jnp.pad(glt, ((0, 0), (0, 16 - 3 * R), (0, 0)))
    o_at = _slcwin(nqt, kgroups(kslc), vgroups(vslc), kgroups(kwin), vgroups(vwin), selt,
                   _near_bias_t(rel_nsa), oct, glt, S)

    dkp = jnp.pad(dk, ((DSA_PADF, TAIL_PAD), (0, 0)))
    dvtp = jnp.pad(dv.T, ((0, 0), (DSA_PADF, TAIL_PAD)))
    ikt = jnp.pad(small[:, SMALL_IK:SMALL_IK + IDX_DIM].T.astype(BF16), ((0, 0), (0, 2 * KEY_TILE)))
    iwt = small[:, SMALL_IW:SMALL_IW + IDX_HEADS].T
    ltri = jnp.asarray(np.tril(np.ones((KEY_TILE, KEY_TILE), np.float32), -1), BF16)
    o_bt = _dsa(dq.T, dkp, dvtp, iq.T, ikt, iwt, _near_bias_t(rel_dsa), ltri, S)

    wr = jnp.pad(w_router[0], ((0, 0), (0, LANES - N_EXPERTS)))
    br = jnp.pad(b_router[0].reshape(1, N_EXPERTS), ((0, 0), (0, LANES - N_EXPERTS)), constant_values=-1e30)
    x1, h2, ridx, rw = _merge(x2, o_at.T, o_bt.T, gab, w_branch_a[0].astype(BF16), w_branch_b[0].astype(BF16),
                              w_out[0].astype(BF16), g_ffn[0].reshape(1, D), wr, br)

    rank, cnt = _rank(ridx)
    counts = cnt[0, :N_EXPERTS]
    padded = (counts + MOE_ROWS - 1) // MOE_ROWS * MOE_ROWS
    pad_end = jnp.cumsum(padded)
    pad_start = pad_end - padded
    eidx = ridx[:, :TOP_K]
    dest = (jnp.sum(jnp.where(eidx[..., None] == jnp.arange(N_EXPERTS), pad_start, 0), axis=-1)
            + rank[:, :TOP_K]).reshape(-1).astype(I32)
    n_chunks = -(-(S * TOP_K) // MOE_ROWS) + N_EXPERTS
    chunk_start = jnp.arange(n_chunks, dtype=I32) * MOE_ROWS
    chunk_e = jnp.minimum(jnp.sum(chunk_start[:, None] >= pad_end[None, :], axis=-1), N_EXPERTS - 1).astype(I32)
    n_used = (pad_end[-1] // MOE_ROWS).astype(I32).reshape(1)
    xe = _dispatch(dest, h2, jnp.zeros((n_chunks * MOE_ROWS, D), F32))
    yexp = _experts(chunk_e, n_used, xe, w_e1[0], b_e1[0].reshape(N_EXPERTS, 1, 2 * D_FF),
                    w_e2[0], b_e2[0].reshape(N_EXPERTS, 1, D))

    out = _final(dest, x1, rw, p[0].reshape(S, PLE_DIM), g_ple[0].reshape(1, D),
                 w_ple_gate[0].astype(BF16), w_ple[0].astype(BF16), yexp)
    return out.reshape(B, S, D)
```

```python
import functools
import math

import numpy as np
import jax
import jax.numpy as jnp
from jax import lax
from jax.experimental import pallas as pl
from jax.experimental.pallas import tpu as pltpu

F32 = jnp.float32
BF16 = jnp.bfloat16
I32 = jnp.int32

D_MODEL = 1024
PLE_DIM = 256
HEAD_DIM = 64
NSA_HEADS = 8
NSA_KV_GROUPS = 2
NSA_GROUP_SIZE = NSA_HEADS // NSA_KV_GROUPS
NSA_CMP_LEN = 32
NSA_CMP_STRIDE = 16
NSA_SLC_BLOCK = 64
NSA_SLC_TOPK = 16
NSA_LOCAL_BLOCKS = 2
NSA_WINDOW = 512
DSA_HEADS = 8
IDX_HEADS = 8
IDX_DIM = 32
DSA_TOPK_MAX = 256
Q_BLOCK = 128
REL_BUCKETS = 32
REL_MAX_EXACT = 16
REL_MAX_DIST = 128
N_EXPERTS = 32
TOP_K = 4
D_FF = 1024
SWIGLU_LIMIT = 7.0
SWIGLU_ALPHA = 1.702
MOE_ROWS = 256
EPS = 1e-6
LOG2E = math.log2(math.e)

NSA_W = NSA_HEADS * HEAD_DIM
DSA_W = DSA_HEADS * HEAD_DIM

LANES = 128
SUBLANES = 8
KEY_TILE = 512
NEAR = 2 * Q_BLOCK
NSA_PADF = NSA_WINDOW
DSA_PADF = Q_BLOCK
KAUG_W = 2 * LANES
MASK_BIG = 2.0 ** 100
TAIL_PAD = 3 * KEY_TILE
VMEM_LIMIT = 56 * 1024 * 1024
ROW_DMA_UNROLL = 8

OFF_NQ, OFF_CMP, OFF_KSLC, OFF_VSLC, OFF_KWIN, OFF_VWIN = 0, 512, 768, 896, 1024, 1152
OFF_DQ, OFF_DK, OFF_DV, OFF_IQ, OFF_GAB, OFF_SMALL = 1280, 1792, 2304, 2816, 3072, 5120
D_IN_PAD = 5248
SMALL_GATE, SMALL_IK, SMALL_IW = 0, 24, 56


def _rel_bucket_table(n):
    d = np.arange(n)
    nf = np.maximum(d, 1).astype(np.float64)
    large = REL_MAX_EXACT + (np.log(nf / REL_MAX_EXACT) / math.log(REL_MAX_DIST / REL_MAX_EXACT)
                             * (REL_BUCKETS - REL_MAX_EXACT)).astype(np.int64)
    large = np.minimum(large, REL_BUCKETS - 1)
    return np.where(d < REL_MAX_EXACT, d, large)


def _cparams(sem, vmem=VMEM_LIMIT):
    return pltpu.CompilerParams(dimension_semantics=sem, vmem_limit_bytes=vmem)


def _dot(a, b):
    return jnp.dot(a, b, preferred_element_type=F32)


def _dot_tn(a, b):
    return lax.dot_general(a, b, (((0,), (0,)), ((), ())), preferred_element_type=F32)


def _inproj_body(x_ref, gmix_ref, w_ref, gains_ref,
                 nq_ref, cmp_ref, kslc_ref, vslc_ref, kwin_ref, vwin_ref,
                 dq_ref, dk_ref, dv_ref, iq_ref, gab_ref, small_ref):
    x = x_ref[...]
    h = x * lax.rsqrt(jnp.mean(x * x, axis=-1, keepdims=True) + EPS) * gmix_ref[...]
    hb = h.astype(BF16)
    lo = lax.broadcasted_iota(I32, (1, LANES), 1) < HEAD_DIM

    def mm(off, n):
        return _dot(hb, w_ref[:, off:off + n])

    def headnorm(y, g):
        y2 = y * y
        s_lo = jnp.sum(jnp.where(lo, y2, 0.0), axis=-1, keepdims=True)
        s_hi = jnp.sum(jnp.where(lo, 0.0, y2), axis=-1, keepdims=True)
        r = jnp.where(lo, lax.rsqrt(s_lo * (1.0 / HEAD_DIM) + EPS), lax.rsqrt(s_hi * (1.0 / HEAD_DIM) + EPS))
        return y * r * g

    g_qn, g_kn, g_qd, g_kd = (gains_ref[k:k + 1, :] for k in range(4))
    for c in range(NSA_W // LANES):
        nq_ref[:, c * LANES:(c + 1) * LANES] = headnorm(mm(OFF_NQ + c * LANES, LANES), g_qn).astype(BF16)
    cmp_ref[...] = mm(OFF_CMP, 256).astype(BF16)
    kslc_ref[...] = headnorm(mm(OFF_KSLC, LANES), g_kn).astype(BF16)
    vslc_ref[...] = mm(OFF_VSLC, LANES).astype(BF16)
    kwin_ref[...] = headnorm(mm(OFF_KWIN, LANES), g_kn).astype(BF16)
    vwin_ref[...] = mm(OFF_VWIN, LANES).astype(BF16)
    for c in range(DSA_W // LANES):
        dq_ref[:, c * LANES:(c + 1) * LANES] = headnorm(mm(OFF_DQ + c * LANES, LANES), g_qd).astype(BF16)
        dk_ref[:, c * LANES:(c + 1) * LANES] = headnorm(mm(OFF_DK + c * LANES, LANES), g_kd).astype(BF16)
    dv_ref[...] = mm(OFF_DV, DSA_W).astype(BF16)
    iq_ref[...] = (mm(OFF_IQ, IDX_HEADS * IDX_DIM) * (IDX_DIM ** -0.5)).astype(BF16)
    gab_ref[...] = jax.nn.sigmoid(mm(OFF_GAB, 2 * D_MODEL))
    small_ref[...] = mm(OFF_SMALL, LANES)


def _inproj(x2, g_mix, w_perm, gains, tm=256):
    S = x2.shape[0]
    row = lambda n: pl.BlockSpec((tm, n), lambda i: (i, 0))
    full = lambda a: pl.BlockSpec(a.shape, lambda i: (0,) * a.ndim)
    widths = [(NSA_W, BF16), (256, BF16), (LANES, BF16), (LANES, BF16), (LANES, BF16), (LANES, BF16),
              (DSA_W, BF16), (DSA_W, BF16), (DSA_W, BF16), (IDX_HEADS * IDX_DIM, BF16),
              (2 * D_MODEL, F32), (LANES, F32)]
    return pl.pallas_call(
        _inproj_body,
        grid=(S // tm,),
        in_specs=[row(D_MODEL), full(g_mix), full(w_perm), full(gains)],
        out_specs=[row(n) for n, _ in widths],
        out_shape=[jax.ShapeDtypeStruct((S, n), dt) for n, dt in widths],
        compiler_params=_cparams(("arbitrary",)),
        name="inproj",
    )(x2, g_mix, w_perm, gains)


def _compress_body(xa_ref, xb_ref, w1a_ref, w1b_ref, w1f_ref, pe_ref, w2_ref, gk_ref, o_ref):
    is_k = pl.program_id(0) < NSA_KV_GROUPS
    pe_term = jnp.sum(pe_ref[...] * w1f_ref[...], axis=0, keepdims=True)
    pre = _dot(xa_ref[...], w1a_ref[...]) + _dot(xb_ref[...], w1b_ref[...]) + pe_term
    hdn = pre * jax.nn.sigmoid(pre)
    out = _dot(hdn.astype(BF16), w2_ref[...])
    normed = out * lax.rsqrt(jnp.mean(out * out, axis=-1, keepdims=True) + EPS) * gk_ref[...]
    o_ref[...] = jnp.where(is_k, normed, out).astype(BF16)


def _compress(xa, xb, w1a, w1b, w1f, pe, w2, gk):
    n4, ncp, kdim = xa.shape
    G = NSA_KV_GROUPS
    kv = lambda i: i // G
    return pl.pallas_call(
        _compress_body,
        grid=(n4,),
        in_specs=[pl.BlockSpec((None, ncp, kdim), lambda i: (i, 0, 0)),
                  pl.BlockSpec((None, ncp, kdim), lambda i: (i, 0, 0)),
                  pl.BlockSpec((None, kdim, HEAD_DIM), lambda i: (kv(i), 0, 0)),
                  pl.BlockSpec((None, kdim, HEAD_DIM), lambda i: (kv(i), 0, 0)),
                  pl.BlockSpec((None, 2 * kdim, HEAD_DIM), lambda i: (kv(i), 0, 0)),
                  pl.BlockSpec((None, 2 * kdim, 1), lambda i: (kv(i), 0, 0)),
                  pl.BlockSpec((None, HEAD_DIM, HEAD_DIM), lambda i: (kv(i), 0, 0)),
                  pl.BlockSpec((1, HEAD_DIM), lambda i: (0, 0))],
        out_specs=pl.BlockSpec((None, ncp, HEAD_DIM), lambda i: (i, 0, 0)),
        out_shape=jax.ShapeDtypeStruct((n4, ncp, HEAD_DIM), BF16),
        compiler_params=_cparams(("arbitrary",)),
        name="compress",
    )(xa, xb, w1a, w1b, w1f, pe, w2, gk)


def _cmp_body(qt_ref, kc_ref, vct_ref, bt_ref, oct_ref, selt_ref, s_ref, *, NS):
    i = pl.program_id(1)
    t0 = i * Q_BLOCK
    T = Q_BLOCK
    NCP = 4 * NS
    R = NSA_GROUP_SIZE
    row = lax.broadcasted_iota(I32, (NCP, 1), 0)
    plane = row // NS
    nblk = row - plane * NS
    cend = NSA_SLC_BLOCK * nblk + NSA_CMP_STRIDE * plane + (NSA_CMP_LEN - 1)
    tq = t0 + lax.broadcasted_iota(I32, (1, T), 1)
    mask = cend <= tq
    kc = kc_ref[...]
    vct = vct_ref[...]
    WIN = 2 * SUBLANES
    n_lo = 2 * i - 2
    wstart = jnp.clip(n_lo // SUBLANES * SUBLANES, 0, NS - WIN)
    wrow = lax.broadcasted_iota(I32, (WIN, 1), 0)
    psum = jnp.zeros((NCP, T), F32)
    for r in range(R):
        s_ref[r] = _dot(kc, qt_ref[r * HEAD_DIM:(r + 1) * HEAD_DIM, :])
    for r in range(R):
        bt = bt_ref[r]
        for j in range(4):
            add = jnp.zeros((WIN, T), F32)
            for k in range(4):
                n = 2 * i + 1 - k
                add = add + jnp.where((wrow == n - wstart) & (n >= 0), bt[4 * j + k:4 * j + k + 1, :], 0.0)
            sl = pl.ds(pl.multiple_of(j * NS + wstart, SUBLANES), WIN)
            s_ref[r, sl, :] = s_ref[r, sl, :] + add
        s = jnp.where(mask, s_ref[r], -jnp.inf)
        m = jnp.maximum(jnp.max(s, axis=0, keepdims=True), -1e30)
        e = jnp.exp2(s - m)
        pc = e / jnp.maximum(jnp.sum(e, axis=0, keepdims=True), 1e-30)
        oct_ref[r * HEAD_DIM:(r + 1) * HEAD_DIM, :] = _dot(vct, pc.astype(BF16))
        psum = psum + pc
    p0, p1, p2, p3 = (psum[j * NS:(j + 1) * NS, :] for j in range(4))
    n_ar = lax.broadcasted_iota(I32, (NS, 1), 0)
    p3s = jnp.where(n_ar == 0, 0.0, pltpu.roll(p3, 1, axis=0))
    imp = p0 + p1 + p2 + 0.5 * p3 + 0.5 * p3s
    cur = tq // NSA_SLC_BLOCK
    valid = NSA_SLC_BLOCK * n_ar <= tq
    forced = valid & ((n_ar == 0) | (n_ar > cur - NSA_LOCAL_BLOCKS))
    score = jnp.where(forced, 1e30, jnp.where(valid, imp, -1.0))
    sel = jnp.zeros((NS, T), jnp.bool_)
    for _ in range(min(NSA_SLC_TOPK, NS)):
        mx = jnp.max(score, axis=0, keepdims=True)
        first = jnp.min(jnp.where(score == mx, n_ar, NS), axis=0, keepdims=True)
        pick = n_ar == first
        sel = sel | pick
        score = jnp.where(pick, -2.0, score)
    selb = jnp.where(sel, 1.0, 0.0).astype(BF16)
    if NS < LANES:
        selb = jnp.concatenate([selb, jnp.zeros((LANES - NS, T), BF16)], axis=0)
    for w in range(selt_ref.shape[0]):
        selt_ref[w] = selb[w * LANES:(w + 1) * LANES, :]


def _cmp(nqt, kc, vct, bt, S):
    NS = S // NSA_SLC_BLOCK
    NCP = 4 * NS
    NW = max(NS // LANES, 1)
    G, R = NSA_KV_GROUPS, NSA_GROUP_SIZE
    return pl.pallas_call(
        functools.partial(_cmp_body, NS=NS),
        grid=(G, S // Q_BLOCK),
        in_specs=[pl.BlockSpec((R * HEAD_DIM, Q_BLOCK), lambda g, i: (g, i)),
                  pl.BlockSpec((None, NCP, HEAD_DIM), lambda g, i: (g, 0, 0)),
                  pl.BlockSpec((None, HEAD_DIM, NCP), lambda g, i: (g, 0, 0)),
                  pl.BlockSpec((R, 16, Q_BLOCK), lambda g, i: (g, 0, 0))],
        out_specs=[pl.BlockSpec((R * HEAD_DIM, Q_BLOCK), lambda g, i: (g, i)),
                   pl.BlockSpec((None, NW, LANES, Q_BLOCK), lambda g, i: (g, 0, 0, i))],
        out_shape=[jax.ShapeDtypeStruct((NSA_W, S), F32),
                   jax.ShapeDtypeStruct((G, NW, LANES, S), BF16)],
        scratch_shapes=[pltpu.VMEM((R, NCP, Q_BLOCK), F32)],
        compiler_params=_cparams(("arbitrary", "arbitrary")),
        name="cmp",
    )(nqt, kc, vct, bt)


def _flash_init(m_ref, l_ref, acc_ref):
    m_ref[...] = jnp.full(m_ref.shape, -1e30, F32)
    l_ref[...] = jnp.zeros(l_ref.shape, F32)
    acc_ref[...] = jnp.zeros(acc_ref.shape, F32)


def _flash_update(s, vt, m_ref, l_ref, acc_ref):
    m_old = m_ref[...]
    m_new = jnp.maximum(m_old, jnp.max(s, axis=0, keepdims=True))
    alpha = jnp.exp2(m_old - m_new)
    p = jnp.exp2(s - m_new)
    l_ref[...] = alpha * l_ref[...] + jnp.sum(p, axis=0, keepdims=True)
    acc_ref[...] = alpha * acc_ref[...] + _dot(vt, p.astype(BF16))
    m_ref[...] = m_new


def _slcwin_body(qt_ref, ks_ref, vst_ref, kw_ref, vwt_ref, selt_ref, tbt_ref, oct_ref, glt_ref, ot_ref,
                 ms_ref, ls_ref, as_ref, mw_ref, lw_ref, aw_ref, sbuf_ref, qaug_ref):
    i = pl.program_id(1)
    t0 = i * Q_BLOCK
    R = NSA_GROUP_SIZE
    T = Q_BLOCK
    q4t = jnp.concatenate([qt_ref[r * HEAD_DIM:(r + 1) * HEAD_DIM, :] for r in range(R)], axis=1)
    tq = lax.broadcasted_iota(I32, (1, T), 1)
    rep = lambda mk: jnp.concatenate([mk] * R, axis=1)
    tb4 = jnp.concatenate([tbt_ref[r] for r in range(R)], axis=1)

    def expand(w, base, nkeys):
        keyi = lax.broadcasted_iota(I32, (nkeys, LANES), 0)
        blk = lax.broadcasted_iota(I32, (nkeys, LANES), 1)
        e = jnp.where(blk == base + keyi // NSA_SLC_BLOCK, 1.0, 0.0).astype(BF16)
        return _dot(e, selt_ref[w])

    _flash_init(ms_ref, ls_ref, as_ref)
    _flash_init(mw_ref, lw_ref, aw_ref)
    n_win = selt_ref.shape[0]
    zpad = jnp.zeros((KAUG_W - HEAD_DIM - LANES, R * T), BF16)
    for w in range(n_win):
        qaug_ref[w] = jnp.concatenate([q4t, rep(selt_ref[w] - jnp.asarray(1.0, BF16)), zpad], axis=0)
    far_end = jnp.maximum(t0 - Q_BLOCK, 0) // KEY_TILE * KEY_TILE
    nfar = far_end // KEY_TILE
    tiles_per_window = LANES * NSA_SLC_BLOCK // KEY_TILE

    def score_far(kt, slot):
        r0 = pl.multiple_of(NSA_PADF + kt * KEY_TILE, LANES)
        w = jnp.minimum(kt // tiles_per_window, n_win - 1)
        sbuf_ref[slot] = _dot(ks_ref[pl.ds(r0, KEY_TILE), :], qaug_ref[w])

    def update_far(kt, slot):
        r0 = pl.multiple_of(NSA_PADF + kt * KEY_TILE, LANES)
        _flash_update(sbuf_ref[slot], vst_ref[:, pl.ds(r0, KEY_TILE)], ms_ref, ls_ref, as_ref)

    score_far(0, 0)

    def far2(k2, carry):
        kt = 2 * k2
        score_far(kt + 1, 1)
        update_far(kt, 0)
        score_far(kt + 2, 0)
        update_far(kt + 1, 1)
        return carry

    lax.fori_loop(0, nfar // 2, far2, 0)

    @pl.when(nfar % 2 == 1)
    def _():
        update_far(nfar - 1, 0)

    kq = lambda r0, n: _dot(ks_ref[pl.ds(r0, n), 0:HEAD_DIM], q4t)
    keyn = lax.broadcasted_iota(I32, (NEAR, 1), 0)
    near_ok = (keyn - Q_BLOCK <= tq) & (keyn + t0 - Q_BLOCK >= 0)
    rn = pl.multiple_of(t0 + NSA_PADF - Q_BLOCK, LANES)
    WF = NSA_WINDOW - Q_BLOCK
    rw = pl.multiple_of(t0 + NSA_PADF - NSA_WINDOW, LANES)
    keyw = lax.broadcasted_iota(I32, (WF, 1), 0)
    maskw = (keyw > tq) & (keyw + t0 - NSA_WINDOW >= 0)
    blocks = [jnp.maximum(2 * i - (NSA_WINDOW // NSA_SLC_BLOCK) + 2 * j, 0) for j in range(NSA_WINDOW // Q_BLOCK)]
    mexp = [expand(b // LANES, b % LANES, Q_BLOCK) for b in blocks]
    mask_sm = (jnp.concatenate(mexp[:-1], axis=0) > 0.5) & (keyw + t0 - NSA_WINDOW >= far_end)
    mask_sn = (jnp.concatenate([mexp[-1], expand(2 * i // LANES, 2 * i % LANES, Q_BLOCK)], axis=0) > 0.5) & near_ok
    s_sm = jnp.where(rep(mask_sm), kq(rw, WF), -jnp.inf)
    s_sn = jnp.where(rep(mask_sn), kq(rn, NEAR) + tb4, -jnp.inf)
    s_wf = jnp.where(rep(maskw), _dot(kw_ref[pl.ds(rw, WF), :], q4t), -jnp.inf)
    s_wn = jnp.where(rep(near_ok), _dot(kw_ref[pl.ds(rn, NEAR), :], q4t) + tb4, -jnp.inf)
    _flash_update(s_sm, vst_ref[:, pl.ds(rw, WF)], ms_ref, ls_ref, as_ref)
    _flash_update(s_sn, vst_ref[:, pl.ds(rn, NEAR)], ms_ref, ls_ref, as_ref)
    _flash_update(s_wf, vwt_ref[:, pl.ds(rw, WF)], mw_ref, lw_ref, aw_ref)
    _flash_update(s_wn, vwt_ref[:, pl.ds(rn, NEAR)], mw_ref, lw_ref, aw_ref)
    o_s = as_ref[...] / jnp.maximum(ls_ref[...], 1e-30)
    o_w = aw_ref[...] / jnp.maximum(lw_ref[...], 1e-30)

    gates = jax.nn.sigmoid(glt_ref[...])
    for r in range(R):
        hs = slice(r * HEAD_DIM, (r + 1) * HEAD_DIM)
        qs = slice(r * T, (r + 1) * T)
        o = (gates[3 * r:3 * r + 1, :] * oct_ref[hs, :]
             + gates[3 * r + 1:3 * r + 2, :] * o_s[:, qs]
             + gates[3 * r + 2:3 * r + 3, :] * o_w[:, qs])
        ot_ref[hs, :] = o.astype(BF16)


def _slcwin(nqt, ks, vst, kw, vwt, selt, tbt, oct, glt, S):
    G, R = NSA_KV_GROUPS, NSA_GROUP_SIZE
    spad = ks.shape[1]
    NW = selt.shape[1]
    once = pl.Buffered(1)
    kaspec = pl.BlockSpec((None, spad, KAUG_W), lambda g, i: (g, 0, 0), pipeline_mode=once)
    kspec = pl.BlockSpec((None, spad, HEAD_DIM), lambda g, i: (g, 0, 0), pipeline_mode=once)
    vspec = pl.BlockSpec((None, HEAD_DIM, spad), lambda g, i: (g, 0, 0), pipeline_mode=once)
    qspec = pl.BlockSpec((R * HEAD_DIM, Q_BLOCK), lambda g, i: (g, i))
    st = lambda n: pltpu.VMEM((n, R * Q_BLOCK), F32)
    return pl.pallas_call(
        _slcwin_body,
        grid=(G, S // Q_BLOCK),
        in_specs=[qspec, kaspec, vspec, kspec, vspec,
                  pl.BlockSpec((None, NW, LANES, Q_BLOCK), lambda g, i: (g, 0, 0, i)),
                  pl.BlockSpec((R, NEAR, Q_BLOCK), lambda g, i: (g, 0, 0)),
                  qspec,
                  pl.BlockSpec((None, 16, Q_BLOCK), lambda g, i: (g, 0, i))],
        out_specs=qspec,
        out_shape=jax.ShapeDtypeStruct((NSA_W, S), BF16),
        scratch_shapes=[st(1), st(1), st(HEAD_DIM), st(1), st(1), st(HEAD_DIM),
                        pltpu.VMEM((2, KEY_TILE, R * Q_BLOCK), F32),
                        pltpu.VMEM((NW, KAUG_W, R * Q_BLOCK), BF16)],
        compiler_params=_cparams(("arbitrary", "arbitrary")),
        name="slcwin",
    )(nqt, ks, vst, kw, vwt, selt, tbt, oct, glt)


INT_MIN = -2 ** 31
INT_MAX = 2 ** 31 - 1
MAX_PEEL_ROUNDS = 64
UNTESTED_BITS = 14
NEG_INF_KEY = int(np.array(-np.inf, np.float32).view(np.int32)) ^ 0x7FFFFFFF


def _dsa_body(dqt_ref, dk_ref, dvt_ref, iqt_ref, ikt_ref, iwt_ref, tbt_ref, ltri_ref, ot_ref,
              keys_ref, m_ref, l_ref, acc_ref, sbuf_ref, *, S, KEEP):
    i = pl.program_id(0)
    t0 = i * Q_BLOCK
    T = Q_BLOCK
    H = DSA_HEADS
    ntile = (t0 + Q_BLOCK + KEY_TILE - 1) // KEY_TILE
    tq = t0 + lax.broadcasted_iota(I32, (1, T), 1)
    key512 = lax.broadcasted_iota(I32, (KEY_TILE, 1), 0)
    ktile = lambda kt: keys_ref[pl.ds(pl.multiple_of(Q_BLOCK + kt * KEY_TILE, LANES), KEY_TILE), :]

    keys_ref[0:Q_BLOCK, :] = jnp.full((Q_BLOCK, T), NEG_INF_KEY, I32)
    iqt = iqt_ref[...]
    iwt = iwt_ref[...] * (IDX_HEADS ** -0.5)
    qis = [jnp.concatenate([iqt[(2 * j) * IDX_DIM:(2 * j + 1) * IDX_DIM, :],
                            iqt[(2 * j + 1) * IDX_DIM:(2 * j + 2) * IDX_DIM, :]], axis=1)
           for j in range(IDX_HEADS // 2)]
    wrow = [iwt[h:h + 1, :] for h in range(IDX_HEADS)]

    def idx_dots(kt, slot):
        ik = ikt_ref[:, pl.ds(pl.multiple_of(kt * KEY_TILE, KEY_TILE), KEY_TILE)]
        for j in range(IDX_HEADS // 2):
            sbuf_ref[slot, j] = _dot_tn(ik, qis[j])

    def idx_keys(kt, slot):
        c0 = pl.multiple_of(kt * KEY_TILE, KEY_TILE)
        sc = jnp.zeros((KEY_TILE, T), F32)
        for j in range(IDX_HEADS // 2):
            d = jnp.maximum(sbuf_ref[slot, j], 0.0)
            sc = sc + wrow[2 * j] * d[:, 0:T]
            sc = sc + wrow[2 * j + 1] * d[:, T:2 * T]
        sc = jnp.where(sc == 0.0, 0.0, sc)
        sc = jnp.where(c0 + key512 <= tq, sc, -jnp.inf)
        bits = pltpu.bitcast(sc, I32)
        keys_ref[pl.ds(pl.multiple_of(Q_BLOCK + c0, LANES), KEY_TILE), :] = jnp.where(
            bits < 0, bits ^ 0x7FFFFFFF, bits)

    idx_dots(0, 0)

    def score2(k2, carry):
        kt = 2 * k2
        idx_dots(kt + 1, 1)
        idx_keys(kt, 0)
        idx_dots(kt + 2, 0)
        idx_keys(kt + 1, 1)
        return carry

    lax.fori_loop(0, (ntile + 1) // 2, score2, 0)

    def count(pred):
        def body(kt, acc):
            hit = jnp.where(pred(ktile(kt), kt * KEY_TILE + key512), 1, 0)
            return acc + jnp.sum(hit.reshape(KEY_TILE // SUBLANES, SUBLANES, T), axis=0)
        acc = lax.fori_loop(0, ntile, body, jnp.zeros((SUBLANES, T), I32))
        return jnp.sum(acc, axis=0, keepdims=True)

    def min_at_least(lowest):
        def body(kt, acc):
            kk = ktile(kt)
            v = jnp.where(kk >= lowest, kk, INT_MAX)
            return jnp.minimum(acc, jnp.min(v.reshape(KEY_TILE // SUBLANES, SUBLANES, T), axis=0))
        acc = lax.fori_loop(0, ntile, body, jnp.full((SUBLANES, T), INT_MAX, I32))
        return jnp.min(acc, axis=0, keepdims=True)

    def any_open(st, over):
        _, excess, fin, _ = st
        return jnp.max(jnp.where((fin == 0) & (excess > over), 1, 0)) > 0

    def bisect(b, st):
        thr, excess, fin, need = st
        cand = thr + lax.shift_left(jnp.int32(1), 31 - b)
        n = count(lambda kk, ab: kk >= cand)
        ok = (n >= KEEP) & (fin == 0)
        return jnp.where(ok, cand, thr), jnp.where(ok, n - KEEP, excess), fin, need

    def peel(st):
        thr, excess, fin, need = st
        low = min_at_least(thr)
        n_low = count(lambda kk, ab: kk == low)
        active = (fin == 0) & (excess > 0)
        tied = active & (n_low > excess)
        drop = active & (n_low <= excess)
        return (jnp.where(tied, low, jnp.where(drop, low + 1, thr)), jnp.where(drop, excess - n_low, excess),
                jnp.where(tied, 1, fin), jnp.where(tied, n_low - excess, need))

    zero = jnp.zeros((1, T), I32)
    st = (jnp.full((1, T), INT_MIN, I32), KEY_TILE * ntile - KEEP + zero, zero, zero)
    st = peel(bisect(0, st))
    st = lax.fori_loop(1, UNTESTED_BITS, bisect, st)
    _, st = lax.while_loop(lambda c: (c[0] < 32) & any_open(c[1], 1),
                           lambda c: (c[0] + 1, bisect(c[0], c[1])), (jnp.int32(UNTESTED_BITS), st))
    _, st = lax.while_loop(lambda c: (c[0] < MAX_PEEL_ROUNDS) & any_open(c[1], 0),
                           lambda c: (c[0] + 1, peel(c[1])), (jnp.int32(0), st))
    thr, _, fin, need = st
    tie = (fin == 1) & (thr > NEG_INF_KEY)

    def tie_cut():
        needf = need.astype(F32)

        def body(kt, c):
            before, jmax = c
            is_tie = ktile(kt) == thr
            tied = jnp.where(is_tie, 1.0, 0.0)
            rank = before + _dot(ltri_ref[...], tied.astype(BF16))
            admit = is_tie & (rank < needf)
            jm = jnp.max(jnp.where(admit, kt * KEY_TILE + key512, -1), axis=0, keepdims=True)
            return before + jnp.sum(tied, axis=0, keepdims=True), jnp.maximum(jmax, jm)

        _, jmax = lax.fori_loop(0, ntile, body, (jnp.zeros((1, T), F32), jnp.full((1, T), -1, I32)))
        return jnp.where(tie, jmax + 1, S)

    jcut = lax.cond(jnp.max(jnp.where(tie, 1, 0)) > 0, tie_cut, lambda: jnp.full((1, T), S, I32))

    def chosen(kk, ab):
        return (kk > thr) | ((kk == thr) & (ab < jcut))

    _flash_init(m_ref, l_ref, acc_ref)
    lo = lax.broadcasted_iota(I32, (LANES, 1), 0) < HEAD_DIM
    qpairs = []
    for pr in range(H // 2):
        qp = dqt_ref[pr * LANES:(pr + 1) * LANES, :]
        zero = jnp.zeros_like(qp)
        qpairs.append(jnp.concatenate([jnp.where(lo, qp, zero), jnp.where(lo, zero, qp)], axis=1))

    def scores(r0, nkeys, mask, bias):
        mask2 = jnp.concatenate([mask, mask], axis=1)
        out = []
        for pr in range(H // 2):
            s = _dot(dk_ref[pl.ds(r0, nkeys), pr * LANES:(pr + 1) * LANES], qpairs[pr])
            if bias is not None:
                s = s + jnp.concatenate([bias[2 * pr], bias[2 * pr + 1]], axis=1)
            out.append(jnp.where(mask2, s, -jnp.inf))
        return out

    def update(pr, s, r0, nkeys):
        m_old = m_ref[pr]
        m_new = jnp.maximum(m_old, jnp.max(s, axis=0, keepdims=True))
        alpha = jnp.exp2(m_old - m_new)
        p = jnp.exp2(s - m_new)
        l_ref[pr] = alpha * l_ref[pr] + jnp.sum(p, axis=0, keepdims=True)
        m_ref[pr] = m_new
        vt = dvt_ref[pr * LANES:(pr + 1) * LANES, pl.ds(r0, nkeys)]
        acc_ref[pr] = alpha * acc_ref[pr] + _dot(vt, p.astype(BF16))

    far_end = t0 - Q_BLOCK
    nfar = jnp.maximum(far_end + KEY_TILE - 1, 0) // KEY_TILE
    far_row = lambda kt: pl.multiple_of(DSA_PADF + kt * KEY_TILE, LANES)

    def score_far(kt, slot):
        ab = kt * KEY_TILE + key512
        mask = chosen(ktile(jnp.minimum(kt, ntile - 1)), ab) & (ab < far_end)
        for pr, s in enumerate(scores(far_row(kt), KEY_TILE, mask, None)):
            sbuf_ref[slot, pr] = s

    def update_far(kt, slot):
        for pr in range(H // 2):
            update(pr, sbuf_ref[slot, pr], far_row(kt), KEY_TILE)

    score_far(0, 0)

    def far2(k2, carry):
        kt = 2 * k2
        score_far(kt + 1, 1)
        update_far(kt, 0)
        score_far(kt + 2, 0)
        update_far(kt + 1, 1)
        return carry

    lax.fori_loop(0, (nfar + 1) // 2, far2, 0)

    abn = lax.broadcasted_iota(I32, (NEAR, 1), 0) + t0 - Q_BLOCK
    kkn = keys_ref[pl.ds(pl.multiple_of(t0, LANES), NEAR), :]
    rn = pl.multiple_of(t0 + DSA_PADF - Q_BLOCK, LANES)
    for pr, s in enumerate(scores(rn, NEAR, chosen(kkn, abn) & (abn <= tq) & (abn >= 0), tbt_ref)):
        update(pr, s, rn, NEAR)

    for pr in range(H // 2):
        o = acc_ref[pr] / jnp.maximum(l_ref[pr], 1e-30)
        ot_ref[pr * LANES:(pr + 1) * LANES, :] = jnp.where(lo, o[:, 0:T], o[:, T:2 * T]).astype(BF16)


def _dsa(dqt, dkp, dvtp, iqt, ikt, iwt, tbt, ltri, S):
    KEEP = min(DSA_TOPK_MAX, S // 4)
    H = DSA_HEADS
    T = Q_BLOCK
    once = pl.Buffered(1)
    full = lambda a: pl.BlockSpec(a.shape, lambda i: (0,) * a.ndim, pipeline_mode=once)
    return pl.pallas_call(
        functools.partial(_dsa_body, S=S, KEEP=KEEP),
        grid=(S // T,),
        in_specs=[pl.BlockSpec((DSA_W, T), lambda i: (0, i)),
                  full(dkp), full(dvtp),
                  pl.BlockSpec((IDX_HEADS * IDX_DIM, T), lambda i: (0, i)),
                  full(ikt),
                  pl.BlockSpec((IDX_HEADS, T), lambda i: (0, i)),
                  full(tbt), full(ltri)],
        out_specs=pl.BlockSpec((DSA_W, T), lambda i: (0, i)),
        out_shape=jax.ShapeDtypeStruct((DSA_W, S), BF16),
        scratch_shapes=[pltpu.VMEM((Q_BLOCK + S + KEY_TILE, T), I32),
                        pltpu.VMEM((H // 2, 1, 2 * T), F32),
                        pltpu.VMEM((H // 2, 1, 2 * T), F32),
                        pltpu.VMEM((H // 2, LANES, 2 * T), F32),
                        pltpu.VMEM((2, H // 2, KEY_TILE, 2 * T), F32)],
        compiler_params=_cparams(("arbitrary",)),
        name="dsa",
    )(dqt, dkp, dvtp, iqt, ikt, iwt, tbt, ltri)


def _split_bf16(a):
    hi = a.astype(BF16)
    return hi, (a - hi.astype(F32)).astype(BF16)


def _merge_body(x_ref, oa_ref, ob_ref, gab_ref, wa_ref, wb_ref, wo_ref, gffn_ref, wr_ref, br_ref,
                x1_ref, h2_ref, ridx_ref, rw_ref):
    a = _dot(oa_ref[...], wa_ref[...])
    b = _dot(ob_ref[...], wb_ref[...])
    gab = gab_ref[...]
    merged = gab[:, :D_MODEL] * a + gab[:, D_MODEL:] * b
    x1 = x_ref[...] + _dot(merged.astype(BF16), wo_ref[...])
    x1_ref[...] = x1
    h2 = x1 * lax.rsqrt(jnp.mean(x1 * x1, axis=-1, keepdims=True) + EPS) * gffn_ref[...]
    h2_ref[...] = h2
    hh, hl = _split_bf16(h2)
    wh, wl = _split_bf16(wr_ref[...])
    logits = _dot(hh, wh) + _dot(hh, wl) + _dot(hl, wh) + br_ref[...]
    lane = lax.broadcasted_iota(I32, (1, LANES), 1)
    idx_out = jnp.zeros(logits.shape, I32)
    val_out = jnp.full(logits.shape, -jnp.inf, F32)
    for k in range(TOP_K):
        mx = jnp.max(logits, axis=-1, keepdims=True)
        first = jnp.min(jnp.where(logits == mx, lane, LANES), axis=-1, keepdims=True)
        idx_out = jnp.where(lane == k, first, idx_out)
        val_out = jnp.where(lane == k, mx, val_out)
        logits = jnp.where(lane == first, -jnp.inf, logits)
    e = jnp.exp(val_out - jnp.max(val_out, axis=-1, keepdims=True))
    ridx_ref[...] = idx_out
    rw_ref[...] = e / jnp.sum(e, axis=-1, keepdims=True)


def _merge(x2, oa, ob, gab, wa, wb, wo, gffn, wr, br, tm=256):
    S = x2.shape[0]
    row = lambda n: pl.BlockSpec((tm, n), lambda i: (i, 0))
    full = lambda a: pl.BlockSpec(a.shape, lambda i: (0,) * a.ndim)
    return pl.pallas_call(
        _merge_body,
        grid=(S // tm,),
        in_specs=[row(D_MODEL), row(NSA_W), row(DSA_W), row(2 * D_MODEL),
                  full(wa), full(wb), full(wo), full(gffn), full(wr), full(br)],
        out_specs=[row(D_MODEL), row(D_MODEL), row(LANES), row(LANES)],
        out_shape=[jax.ShapeDtypeStruct((S, D_MODEL), F32), jax.ShapeDtypeStruct((S, D_MODEL), F32),
                   jax.ShapeDtypeStruct((S, LANES), I32), jax.ShapeDtypeStruct((S, LANES), F32)],
        compiler_params=_cparams(("arbitrary",)),
        name="merge",
    )(x2, oa, ob, gab, wa, wb, wo, gffn, wr, br)


def _rank_body(ridx_ref, rank_ref, cnt_ref, carry_ref):
    @pl.when(pl.program_id(0) == 0)
    def _():
        carry_ref[...] = jnp.zeros(carry_ref.shape, F32)

    ridx = ridx_ref[...]
    tm = ridx.shape[0]
    lane = lax.broadcasted_iota(I32, (1, LANES), 1)
    hits = [lane == ridx[:, k:k + 1] for k in range(TOP_K)]
    member = jnp.zeros((tm, LANES), F32)
    for hk in hits:
        member = member + jnp.where(hk, 1.0, 0.0)
    ri = lax.broadcasted_iota(I32, (tm, tm), 0)
    ci = lax.broadcasted_iota(I32, (tm, tm), 1)
    lower = jnp.where(ci < ri, 1.0, 0.0).astype(BF16)
    before = _dot(lower, member.astype(BF16)) + carry_ref[0:1, :]
    out = jnp.zeros((tm, LANES), I32)
    for k, hk in enumerate(hits):
        rk = jnp.sum(jnp.where(hk, before, 0.0), axis=-1, keepdims=True)
        out = jnp.where(lane == k, rk.astype(I32), out)
    rank_ref[...] = out
    total = carry_ref[0:1, :] + jnp.sum(member, axis=0, keepdims=True)
    carry_ref[...] = jnp.broadcast_to(total, carry_ref.shape)
    cnt_ref[...] = jnp.broadcast_to(total, cnt_ref.shape).astype(I32)


def _rank(ridx, tm=512):
    S = ridx.shape[0]
    return pl.pallas_call(
        _rank_body,
        grid=(S // tm,),
        in_specs=[pl.BlockSpec((tm, LANES), lambda i: (i, 0))],
        out_specs=[pl.BlockSpec((tm, LANES), lambda i: (i, 0)), pl.BlockSpec((8, LANES), lambda i: (0, 0))],
        out_shape=[jax.ShapeDtypeStruct((S, LANES), I32), jax.ShapeDtypeStruct((8, LANES), I32)],
        scratch_shapes=[pltpu.VMEM((8, LANES), F32)],
        compiler_params=_cparams(("arbitrary",)),
        name="rank",
    )(ridx)


def _dispatch_body(dest_ref, h_ref, zero_ref, xe_ref, sem):
    del zero_ref
    i = pl.program_id(0)
    tm = h_ref.shape[0]

    def start(r, c):
        for k in range(TOP_K):
            d = dest_ref[(i * tm + r) * TOP_K + k]
            pltpu.make_async_copy(h_ref.at[pl.ds(r, 1)], xe_ref.at[pl.ds(d, 1)],
                                  sem).start(priority=k % 2)
        return c

    lax.fori_loop(0, tm, start, 0, unroll=ROW_DMA_UNROLL)
    for _ in range(TOP_K):
        pltpu.make_async_copy(h_ref, xe_ref.at[pl.ds(0, tm)], sem).wait()


def _dispatch(dest, h2, zeros_p, tm=128):
    S = h2.shape[0]
    return pl.pallas_call(
        _dispatch_body,
        grid_spec=pltpu.PrefetchScalarGridSpec(
            num_scalar_prefetch=1,
            grid=(S // tm,),
            in_specs=[pl.BlockSpec((tm, D_MODEL), lambda i, d: (i, 0)),
                      pl.BlockSpec(memory_space=pl.ANY)],
            out_specs=pl.BlockSpec(memory_space=pl.ANY),
            scratch_shapes=[pltpu.SemaphoreType.DMA(())]),
        out_shape=jax.ShapeDtypeStruct(zeros_p.shape, zeros_p.dtype),
        input_output_aliases={2: 0},
        compiler_params=_cparams(("arbitrary",)),
        name="dispatch",
    )(dest, h2, zeros_p)


def _experts_body(ce_ref, nu_ref, x_ref, w1_ref, b1_ref, w2_ref, b2_ref, y_ref, w1b_ref, w2b_ref):
    c = pl.program_id(0)

    @pl.when((c == 0) | (ce_ref[c] != ce_ref[jnp.maximum(c - 1, 0)]))
    def _():
        w1b_ref[...] = w1_ref[...].astype(BF16)
        w2b_ref[...] = w2_ref[...].astype(BF16)

    @pl.when(c < nu_ref[0])
    def _():
        gu = _dot(x_ref[...].astype(BF16), w1b_ref[...]) + b1_ref[...]
        gate = jnp.minimum(gu[:, :D_FF], SWIGLU_LIMIT)
        lin = jnp.clip(gu[:, D_FF:], -SWIGLU_LIMIT, SWIGLU_LIMIT)
        act = (lin + 1.0) * gate * jax.nn.sigmoid(SWIGLU_ALPHA * gate)
        y_ref[...] = _dot(act.astype(BF16), w2b_ref[...]) + b2_ref[...]

    @pl.when(c >= nu_ref[0])
    def _():
        y_ref[...] = jnp.zeros(y_ref.shape, F32)


def _experts(chunk_e, n_used, xe, w1, b1, w2, b2):
    P = xe.shape[0]
    nch = P // MOE_ROWS
    return pl.pallas_call(
        _experts_body,
        grid_spec=pltpu.PrefetchScalarGridSpec(
            num_scalar_prefetch=2,
            grid=(nch,),
            in_specs=[pl.BlockSpec((MOE_ROWS, D_MODEL), lambda c, ce, nu: (c, 0)),
                      pl.BlockSpec((None, D_MODEL, 2 * D_FF), lambda c, ce, nu: (ce[c], 0, 0)),
                      pl.BlockSpec((None, 1, 2 * D_FF), lambda c, ce, nu: (ce[c], 0, 0)),
                      pl.BlockSpec((None, D_FF, D_MODEL), lambda c, ce, nu: (ce[c], 0, 0)),
                      pl.BlockSpec((None, 1, D_MODEL), lambda c, ce, nu: (ce[c], 0, 0))],
            out_specs=pl.BlockSpec((MOE_ROWS, D_MODEL), lambda c, ce, nu: (c, 0)),
            scratch_shapes=[pltpu.VMEM((D_MODEL, 2 * D_FF), BF16), pltpu.VMEM((D_FF, D_MODEL), BF16)]),
        out_shape=jax.ShapeDtypeStruct((P, D_MODEL), F32),
        compiler_params=_cparams(("arbitrary",)),
        name="experts",
    )(chunk_e, n_used, xe, w1, b1, w2, b2)


def _final_body(dest_ref, x1_ref, rw_ref, p_ref, gple_ref, wg_ref, wp_ref, y_ref, o_ref, rows_ref, sems):
    i = pl.program_id(0)
    tm = x1_ref.shape[0]

    def gather(blk, slot):
        def start(r, c):
            for k in range(TOP_K):
                d = dest_ref[(blk * tm + r) * TOP_K + k]
                pltpu.make_async_copy(y_ref.at[pl.ds(d, 1)], rows_ref.at[slot, k, pl.ds(r, 1)],
                                      sems.at[slot]).start(priority=k % 2)
            return c
        lax.fori_loop(0, tm, start, 0, unroll=ROW_DMA_UNROLL)

    @pl.when(i == 0)
    def _():
        gather(0, 0)

    @pl.when(i + 1 < pl.num_programs(0))
    def _():
        gather(i + 1, (i + 1) % 2)

    slot = i % 2
    for k in range(TOP_K):
        pltpu.make_async_copy(y_ref.at[pl.ds(0, tm)], rows_ref.at[slot, k], sems.at[slot]).wait()
    rw = rw_ref[...]
    x2 = x1_ref[...]
    for k in range(TOP_K):
        x2 = x2 + rw[:, k:k + 1] * rows_ref[slot, k]
    hn = x2 * lax.rsqrt(jnp.mean(x2 * x2, axis=-1, keepdims=True) + EPS) * gple_ref[...]
    gate = jax.nn.sigmoid(_dot(hn.astype(BF16), wg_ref[...]))
    o_ref[...] = x2 + gate * _dot(p_ref[...].astype(BF16), wp_ref[...])


def _final(dest, x1, rw, p2, gple, wg, wp, yexp, tm=128):
    S = x1.shape[0]
    row = lambda n: pl.BlockSpec((tm, n), lambda i, d: (i, 0))
    full = lambda a: pl.BlockSpec(a.shape, lambda i, d: (0,) * a.ndim)
    return pl.pallas_call(
        _final_body,
        grid_spec=pltpu.PrefetchScalarGridSpec(
            num_scalar_prefetch=1,
            grid=(S // tm,),
            in_specs=[row(D_MODEL), row(LANES), row(PLE_DIM), full(gple), full(wg), full(wp),
                      pl.BlockSpec(memory_space=pl.ANY)],
            out_specs=row(D_MODEL),
            scratch_shapes=[pltpu.VMEM((2, TOP_K, tm, D_MODEL), F32), pltpu.SemaphoreType.DMA((2,))]),
        out_shape=jax.ShapeDtypeStruct((S, D_MODEL), F32),
        compiler_params=_cparams(("arbitrary",)),
        name="final",
    )(dest, x1, rw, p2, gple, wg, wp, yexp)


def _distance_bias(rel_cols):
    return (rel_cols[_rel_bucket_table(NEAR)] - rel_cols[REL_BUCKETS - 1][None, :]).T * LOG2E


def _near_bias_t(rel_cols):
    h = rel_cols.shape[1]
    g = jnp.pad(_distance_bias(rel_cols), ((0, 0), (Q_BLOCK, Q_BLOCK)))
    period = 2 * NEAR
    skew = jnp.tile(g, (1, NEAR))[:, :NEAR * (period - 1)].reshape(h, NEAR, period - 1)
    return skew[:, :, NEAR:NEAR + Q_BLOCK]


def _cmp_bias_t(rel_cols):
    g = jnp.pad(_distance_bias(rel_cols), ((0, 0), (NEAR, Q_BLOCK)))
    rows = []
    for j in range(4):
        for k in range(4):
            c0 = -NSA_CMP_STRIDE * j - (NSA_CMP_LEN - 1) + NSA_SLC_BLOCK * (k - 1)
            rows.append(g[:, c0 + NEAR:c0 + NEAR + Q_BLOCK])
    return jnp.stack(rows, axis=1)


def kernel(x, p, w_in, g_mix, g_q_nsa, g_k_nsa, g_q_dsa, g_k_dsa, pe_ck, w_ck1, w_ck2, pe_cv, w_cv1, w_cv2,
           w_branch_a, w_branch_b, w_out, rel_bias, g_ffn, w_router, b_router, w_e1, b_e1, w_e2, b_e2,
           g_ple, w_ple, w_ple_gate):
    B, S, D = x.shape
    assert B == 1 and D == D_MODEL and S % KEY_TILE == 0 and w_in.shape[0] == 1
    G, R = NSA_KV_GROUPS, NSA_GROUP_SIZE
    x2 = x.reshape(S, D)

    perm = np.concatenate([np.arange(0, 1280), np.arange(1304, 3096), np.arange(3136, 5184),
                           np.arange(1280, 1304), np.arange(3096, 3136)])
    w_perm = jnp.pad(w_in[0][:, perm], ((0, 0), (0, D_IN_PAD - perm.size))).astype(BF16)
    two = lambda g: jnp.tile(g.reshape(1, HEAD_DIM), (1, 2))
    qscale = HEAD_DIM ** -0.5 * LOG2E
    gains = jnp.concatenate([two(g_q_nsa[0]) * qscale, two(g_k_nsa[0]),
                             two(g_q_dsa[0]) * qscale, two(g_k_dsa[0])], axis=0)
    (nq, cmpraw, kslc, vslc, kwin, vwin, dq, dk, dv, iq, gab, small) = _inproj(
        x2, g_mix[0].reshape(1, D), w_perm, gains)

    NS = S // NSA_SLC_BLOCK
    nchunk = S // NSA_CMP_STRIDE
    xc = cmpraw.reshape(nchunk, NSA_CMP_STRIDE, 2 * G, HEAD_DIM).transpose(2, 0, 1, 3).reshape(
        2 * G, nchunk, NSA_CMP_STRIDE * HEAD_DIM)
    planes = lambda a: a.reshape(2 * G, NS, 4, -1).transpose(0, 2, 1, 3).reshape(2 * G, nchunk, -1)
    xa = planes(xc)
    xb = planes(jnp.roll(xc, -1, axis=1))
    half = NSA_CMP_STRIDE * HEAD_DIM
    w1 = jnp.stack([w_ck1[0], w_cv1[0]]).reshape(2, 2 * half, HEAD_DIM)
    pe = jnp.stack([pe_ck[0], pe_cv[0]]).reshape(2, 2 * half, 1)
    w2 = jnp.stack([w_ck2[0], w_cv2[0]]).astype(BF16)
    kvc = _compress(xa, xb, w1[:, :half].astype(BF16), w1[:, half:].astype(BF16), w1, pe, w2,
                    g_k_nsa[0].reshape(1, HEAD_DIM))

    rel_nsa = rel_bias[:, :NSA_HEADS]
    rel_dsa = rel_bias[:, NSA_HEADS:]
    nqt = nq.T
    oct, selt = _cmp(nqt, kvc[:G], kvc[G:].transpose(0, 2, 1), _cmp_bias_t(rel_nsa), S)
    kgroups = lambda a: jnp.pad(a.reshape(S, G, HEAD_DIM).transpose(1, 0, 2),
                                ((0, 0), (NSA_PADF, TAIL_PAD), (0, 0)))
    vgroups = lambda a: jnp.pad(a.reshape(S, G, HEAD_DIM).transpose(1, 2, 0),
                                ((0, 0), (0, 0), (NSA_PADF, TAIL_PAD)))
    glt = small[:, SMALL_GATE:SMALL_GATE + 3 * NSA_HEADS].reshape(S, G, 3 * R).transpose(1, 2, 0)
    glt = jnp.pad(glt, ((0, 0), (0, 16 - 3 * R), (0, 0)))
    blk_lane = (np.arange(S) // NSA_SLC_BLOCK) % LANES
    onehot = jnp.asarray(np.eye(LANES, dtype=np.float32)[blk_lane] * MASK_BIG, BF16)
    kaug = jnp.concatenate([kslc.reshape(S, G, HEAD_DIM).transpose(1, 0, 2),
                            jnp.broadcast_to(onehot, (G, S, LANES)),
                            jnp.zeros((G, S, KAUG_W - HEAD_DIM - LANES), BF16)], axis=-1)
    kaug = jnp.pad(kaug, ((0, 0), (NSA_PADF, TAIL_PAD), (0, 0)))
    o_at = _slcwin(nqt, kaug, vgroups(vslc), kgroups(kwin), vgroups(vwin), selt,
                   _near_bias_t(rel_nsa), oct, glt, S)

    dkp = jnp.pad(dk, ((DSA_PADF, TAIL_PAD), (0, 0)))
    dvtp = jnp.pad(dv.T, ((0, 0), (DSA_PADF, TAIL_PAD)))
    ikt = jnp.pad(small[:, SMALL_IK:SMALL_IK + IDX_DIM].T.astype(BF16), ((0, 0), (0, 2 * KEY_TILE)))
    iwt = small[:, SMALL_IW:SMALL_IW + IDX_HEADS].T
    ltri = jnp.asarray(np.tril(np.ones((KEY_TILE, KEY_TILE), np.float32), -1), BF16)
    o_bt = _dsa(dq.T, dkp, dvtp, iq.T, ikt, iwt, _near_bias_t(rel_dsa), ltri, S)

    wr = jnp.pad(w_router[0], ((0, 0), (0, LANES - N_EXPERTS)))
    br = jnp.pad(b_router[0].reshape(1, N_EXPERTS), ((0, 0), (0, LANES - N_EXPERTS)), constant_values=-1e30)
    x1, h2, ridx, rw = _merge(x2, o_at.T, o_bt.T, gab, w_branch_a[0].astype(BF16), w_branch_b[0].astype(BF16),
                              w_out[0].astype(BF16), g_ffn[0].reshape(1, D), wr, br)

    rank, cnt = _rank(ridx)
    counts = cnt[0, :N_EXPERTS]
    padded = (counts + MOE_ROWS - 1) // MOE_ROWS * MOE_ROWS
    pad_end = jnp.cumsum(padded)
    pad_start = pad_end - padded
    eidx = ridx[:, :TOP_K]
    dest = (jnp.sum(jnp.where(eidx[..., None] == jnp.arange(N_EXPERTS), pad_start, 0), axis=-1)
            + rank[:, :TOP_K]).reshape(-1).astype(I32)
    n_chunks = -(-(S * TOP_K) // MOE_ROWS) + N_EXPERTS
    chunk_start = jnp.arange(n_chunks, dtype=I32) * MOE_ROWS
    chunk_e = jnp.minimum(jnp.sum(chunk_start[:, None] >= pad_end[None, :], axis=-1), N_EXPERTS - 1).astype(I32)
    n_used = (pad_end[-1] // MOE_ROWS).astype(I32).reshape(1)
    xe = _dispatch(dest, h2, jnp.zeros((n_chunks * MOE_ROWS, D), F32))
    yexp = _experts(chunk_e, n_used, xe, w_e1[0], b_e1[0].reshape(N_EXPERTS, 1, 2 * D_FF),
                    w_e2[0], b_e2[0].reshape(N_EXPERTS, 1, D))

    out = _final(dest, x1, rw, p[0].reshape(S, PLE_DIM), g_ple[0].reshape(1, D),
                 w_ple_gate[0].astype(BF16), w_ple[0].astype(BF16), yexp)
    return out.reshape(B, S, D)
```

```python
import functools
import math

import numpy as np
import jax
import jax.numpy as jnp
from jax import lax
from jax.experimental import pallas as pl
from jax.experimental.pallas import tpu as pltpu

F32 = jnp.float32
BF16 = jnp.bfloat16
I32 = jnp.int32

D_MODEL = 1024
PLE_DIM = 256
HEAD_DIM = 64
NSA_HEADS = 8
NSA_KV_GROUPS = 2
NSA_GROUP_SIZE = NSA_HEADS // NSA_KV_GROUPS
NSA_CMP_LEN = 32
NSA_CMP_STRIDE = 16
NSA_SLC_BLOCK = 64
NSA_SLC_TOPK = 16
NSA_LOCAL_BLOCKS = 2
NSA_WINDOW = 512
DSA_HEADS = 8
IDX_HEADS = 8
IDX_DIM = 32
DSA_TOPK_MAX = 256
Q_BLOCK = 128
REL_BUCKETS = 32
REL_MAX_EXACT = 16
REL_MAX_DIST = 128
N_EXPERTS = 32
TOP_K = 4
D_FF = 1024
SWIGLU_LIMIT = 7.0
SWIGLU_ALPHA = 1.702
MOE_ROWS = 256
EPS = 1e-6
LOG2E = math.log2(math.e)

NSA_W = NSA_HEADS * HEAD_DIM
DSA_W = DSA_HEADS * HEAD_DIM

LANES = 128
SUBLANES = 8
KEY_TILE = 512
NEAR = 2 * Q_BLOCK
NSA_PADF = NSA_WINDOW
DSA_PADF = Q_BLOCK
KAUG_W = 2 * LANES
MASK_BIG = 2.0 ** 100
TAIL_PAD = 3 * KEY_TILE
VMEM_LIMIT = 56 * 1024 * 1024
ROW_DMA_UNROLL = 8

OFF_NQ, OFF_CMP, OFF_KSLC, OFF_VSLC, OFF_KWIN, OFF_VWIN = 0, 512, 768, 896, 1024, 1152
OFF_DQ, OFF_DK, OFF_DV, OFF_IQ, OFF_GAB, OFF_SMALL = 1280, 1792, 2304, 2816, 3072, 5120
D_IN_PAD = 5248
SMALL_GATE, SMALL_IK, SMALL_IW = 0, 24, 56


def _rel_bucket_table(n):
    d = np.arange(n)
    nf = np.maximum(d, 1).astype(np.float64)
    large = REL_MAX_EXACT + (np.log(nf / REL_MAX_EXACT) / math.log(REL_MAX_DIST / REL_MAX_EXACT)
                             * (REL_BUCKETS - REL_MAX_EXACT)).astype(np.int64)
    large = np.minimum(large, REL_BUCKETS - 1)
    return np.where(d < REL_MAX_EXACT, d, large)


def _cparams(sem, vmem=VMEM_LIMIT):
    return pltpu.CompilerParams(dimension_semantics=sem, vmem_limit_bytes=vmem)


def _dot(a, b):
    return jnp.dot(a, b, preferred_element_type=F32)


def _dot_tn(a, b):
    return lax.dot_general(a, b, (((0,), (0,)), ((), ())), preferred_element_type=F32)


def _inproj_body(x_ref, gmix_ref, w_ref, gains_ref,
                 nq_ref, cmp_ref, kslc_ref, vslc_ref, kwin_ref, vwin_ref,
                 dq_ref, dk_ref, dv_ref, iq_ref, gab_ref, small_ref):
    x = x_ref[...]
    h = x * lax.rsqrt(jnp.mean(x * x, axis=-1, keepdims=True) + EPS) * gmix_ref[...]
    hb = h.astype(BF16)
    lo = lax.broadcasted_iota(I32, (1, LANES), 1) < HEAD_DIM

    def mm(off, n):
        return _dot(hb, w_ref[:, off:off + n])

    def headnorm(y, g):
        y2 = y * y
        s_lo = jnp.sum(jnp.where(lo, y2, 0.0), axis=-1, keepdims=True)
        s_hi = jnp.sum(jnp.where(lo, 0.0, y2), axis=-1, keepdims=True)
        r = jnp.where(lo, lax.rsqrt(s_lo * (1.0 / HEAD_DIM) + EPS), lax.rsqrt(s_hi * (1.0 / HEAD_DIM) + EPS))
        return y * r * g

    g_qn, g_kn, g_qd, g_kd = (gains_ref[k:k + 1, :] for k in range(4))
    for c in range(NSA_W // LANES):
        nq_ref[:, c * LANES:(c + 1) * LANES] = headnorm(mm(OFF_NQ + c * LANES, LANES), g_qn).astype(BF16)
    cmp_ref[...] = mm(OFF_CMP, 256).astype(BF16)
    kslc_ref[...] = headnorm(mm(OFF_KSLC, LANES), g_kn).astype(BF16)
    vslc_ref[...] = mm(OFF_VSLC, LANES).astype(BF16)
    kwin_ref[...] = headnorm(mm(OFF_KWIN, LANES), g_kn).astype(BF16)
    vwin_ref[...] = mm(OFF_VWIN, LANES).astype(BF16)
    for c in range(DSA_W // LANES):
        dq_ref[:, c * LANES:(c + 1) * LANES] = headnorm(mm(OFF_DQ + c * LANES, LANES), g_qd).astype(BF16)
        dk_ref[:, c * LANES:(c + 1) * LANES] = headnorm(mm(OFF_DK + c * LANES, LANES), g_kd).astype(BF16)
    dv_ref[...] = mm(OFF_DV, DSA_W).astype(BF16)
    iq_ref[...] = (mm(OFF_IQ, IDX_HEADS * IDX_DIM) * (IDX_DIM ** -0.5)).astype(BF16)
    gab_ref[...] = jax.nn.sigmoid(mm(OFF_GAB, 2 * D_MODEL))
    small_ref[...] = mm(OFF_SMALL, LANES)


def _inproj(x2, g_mix, w_perm, gains, tm=256):
    S = x2.shape[0]
    row = lambda n: pl.BlockSpec((tm, n), lambda i: (i, 0))
    full = lambda a: pl.BlockSpec(a.shape, lambda i: (0,) * a.ndim)
    widths = [(NSA_W, BF16), (256, BF16), (LANES, BF16), (LANES, BF16), (LANES, BF16), (LANES, BF16),
              (DSA_W, BF16), (DSA_W, BF16), (DSA_W, BF16), (IDX_HEADS * IDX_DIM, BF16),
              (2 * D_MODEL, F32), (LANES, F32)]
    return pl.pallas_call(
        _inproj_body,
        grid=(S // tm,),
        in_specs=[row(D_MODEL), full(g_mix), full(w_perm), full(gains)],
        out_specs=[row(n) for n, _ in widths],
        out_shape=[jax.ShapeDtypeStruct((S, n), dt) for n, dt in widths],
        compiler_params=_cparams(("arbitrary",)),
        name="inproj",
    )(x2, g_mix, w_perm, gains)


def _compress_body(xa_ref, xb_ref, w1a_ref, w1b_ref, w1f_ref, pe_ref, w2_ref, gk_ref, o_ref):
    is_k = pl.program_id(0) < NSA_KV_GROUPS
    pe_term = jnp.sum(pe_ref[...] * w1f_ref[...], axis=0, keepdims=True)
    pre = _dot(xa_ref[...], w1a_ref[...]) + _dot(xb_ref[...], w1b_ref[...]) + pe_term
    hdn = pre * jax.nn.sigmoid(pre)
    out = _dot(hdn.astype(BF16), w2_ref[...])
    normed = out * lax.rsqrt(jnp.mean(out * out, axis=-1, keepdims=True) + EPS) * gk_ref[...]
    o_ref[...] = jnp.where(is_k, normed, out).astype(BF16)


def _compress(xa, xb, w1a, w1b, w1f, pe, w2, gk):
    n4, ncp, kdim = xa.shape
    G = NSA_KV_GROUPS
    kv = lambda i: i // G
    return pl.pallas_call(
        _compress_body,
        grid=(n4,),
        in_specs=[pl.BlockSpec((None, ncp, kdim), lambda i: (i, 0, 0)),
                  pl.BlockSpec((None, ncp, kdim), lambda i: (i, 0, 0)),
                  pl.BlockSpec((None, kdim, HEAD_DIM), lambda i: (kv(i), 0, 0)),
                  pl.BlockSpec((None, kdim, HEAD_DIM), lambda i: (kv(i), 0, 0)),
                  pl.BlockSpec((None, 2 * kdim, HEAD_DIM), lambda i: (kv(i), 0, 0)),
                  pl.BlockSpec((None, 2 * kdim, 1), lambda i: (kv(i), 0, 0)),
                  pl.BlockSpec((None, HEAD_DIM, HEAD_DIM), lambda i: (kv(i), 0, 0)),
                  pl.BlockSpec((1, HEAD_DIM), lambda i: (0, 0))],
        out_specs=pl.BlockSpec((None, ncp, HEAD_DIM), lambda i: (i, 0, 0)),
        out_shape=jax.ShapeDtypeStruct((n4, ncp, HEAD_DIM), BF16),
        compiler_params=_cparams(("arbitrary",)),
        name="compress",
    )(xa, xb, w1a, w1b, w1f, pe, w2, gk)


def _cmp_body(qt_ref, kc_ref, vct_ref, bt_ref, oct_ref, selt_ref, s_ref, imp_ref, *, NS):
    i = pl.program_id(1)

    @pl.when((pl.program_id(0) == 0) & (i == 0))
    def _():
        imp_ref[...] = jnp.zeros(imp_ref.shape, F32)

    imp_prev = imp_ref[...]
    ib = jnp.minimum(i, pl.num_programs(1) - 2)
    t0 = ib * Q_BLOCK
    T = Q_BLOCK
    NCP = 4 * NS
    R = NSA_GROUP_SIZE
    row = lax.broadcasted_iota(I32, (NCP, 1), 0)
    plane = row // NS
    nblk = row - plane * NS
    cend = NSA_SLC_BLOCK * nblk + NSA_CMP_STRIDE * plane + (NSA_CMP_LEN - 1)
    tq = t0 + lax.broadcasted_iota(I32, (1, T), 1)
    mask = cend <= tq
    kc = kc_ref[...]
    vct = vct_ref[...]
    WIN = 2 * SUBLANES
    n_lo = 2 * ib - 2
    wstart = jnp.clip(n_lo // SUBLANES * SUBLANES, 0, NS - WIN)
    wrow = lax.broadcasted_iota(I32, (WIN, 1), 0)
    psum = jnp.zeros((NCP, T), F32)
    for r in range(R):
        s_ref[r] = _dot(kc, qt_ref[r * HEAD_DIM:(r + 1) * HEAD_DIM, :])
    for r in range(R):
        bt = bt_ref[r]
        for j in range(4):
            add = jnp.zeros((WIN, T), F32)
            for k in range(4):
                n = 2 * ib + 1 - k
                add = add + jnp.where((wrow == n - wstart) & (n >= 0), bt[4 * j + k:4 * j + k + 1, :], 0.0)
            sl = pl.ds(pl.multiple_of(j * NS + wstart, SUBLANES), WIN)
            s_ref[r, sl, :] = s_ref[r, sl, :] + add
        s = jnp.where(mask, s_ref[r], -jnp.inf)
        m = jnp.maximum(jnp.max(s, axis=0, keepdims=True), -1e30)
        e = jnp.exp2(s - m)
        pc = e / jnp.maximum(jnp.sum(e, axis=0, keepdims=True), 1e-30)
        oct_ref[r * HEAD_DIM:(r + 1) * HEAD_DIM, :] = _dot(vct, pc.astype(BF16))
        psum = psum + pc
    p0, p1, p2, p3 = (psum[j * NS:(j + 1) * NS, :] for j in range(4))
    n_ar = lax.broadcasted_iota(I32, (NS, 1), 0)
    p3s = jnp.where(n_ar == 0, 0.0, pltpu.roll(p3, 1, axis=0))
    imp_ref[...] = p0 + p1 + p2 + 0.5 * p3 + 0.5 * p3s

    imp = imp_prev
    tq = (i - 1) * Q_BLOCK + lax.broadcasted_iota(I32, (1, T), 1)
    cur = tq // NSA_SLC_BLOCK
    valid = NSA_SLC_BLOCK * n_ar <= tq
    forced = valid & ((n_ar == 0) | (n_ar > cur - NSA_LOCAL_BLOCKS))
    score = jnp.where(forced, 1e30, jnp.where(valid, imp, -1.0))
    sel = jnp.zeros((NS, T), jnp.bool_)
    for _ in range(min(NSA_SLC_TOPK, NS)):
        mx = jnp.max(score, axis=0, keepdims=True)
        first = jnp.min(jnp.where(score == mx, n_ar, NS), axis=0, keepdims=True)
        pick = n_ar == first
        sel = sel | pick
        score = jnp.where(pick, -2.0, score)
    selb = jnp.where(sel, 1.0, 0.0).astype(BF16)
    if NS < LANES:
        selb = jnp.concatenate([selb, jnp.zeros((LANES - NS, T), BF16)], axis=0)
    for w in range(selt_ref.shape[0]):
        selt_ref[w] = selb[w * LANES:(w + 1) * LANES, :]


def _cmp(nqt, kc, vct, bt, S):
    NS = S // NSA_SLC_BLOCK
    NCP = 4 * NS
    NW = max(NS // LANES, 1)
    G, R = NSA_KV_GROUPS, NSA_GROUP_SIZE
    nb = S // Q_BLOCK
    return pl.pallas_call(
        functools.partial(_cmp_body, NS=NS),
        grid=(G, nb + 1),
        in_specs=[pl.BlockSpec((R * HEAD_DIM, Q_BLOCK), lambda g, i: (g, jnp.minimum(i, nb - 1))),
                  pl.BlockSpec((None, NCP, HEAD_DIM), lambda g, i: (g, 0, 0)),
                  pl.BlockSpec((None, HEAD_DIM, NCP), lambda g, i: (g, 0, 0)),
                  pl.BlockSpec((R, 16, Q_BLOCK), lambda g, i: (g, 0, 0))],
        out_specs=[pl.BlockSpec((R * HEAD_DIM, Q_BLOCK), lambda g, i: (g, jnp.minimum(i, nb - 1))),
                   pl.BlockSpec((None, NW, LANES, Q_BLOCK), lambda g, i: (g, 0, 0, jnp.maximum(i - 1, 0)))],
        out_shape=[jax.ShapeDtypeStruct((NSA_W, S), F32),
                   jax.ShapeDtypeStruct((G, NW, LANES, S), BF16)],
        scratch_shapes=[pltpu.VMEM((R, NCP, Q_BLOCK), F32), pltpu.VMEM((NS, Q_BLOCK), F32)],
        compiler_params=_cparams(("arbitrary", "arbitrary")),
        name="cmp",
    )(nqt, kc, vct, bt)


def _flash_init(m_ref, l_ref, acc_ref):
    m_ref[...] = jnp.full(m_ref.shape, -1e30, F32)
    l_ref[...] = jnp.zeros(l_ref.shape, F32)
    acc_ref[...] = jnp.zeros(acc_ref.shape, F32)


def _flash_update(s, vt, m_ref, l_ref, acc_ref):
    m_old = m_ref[...]
    m_new = jnp.maximum(m_old, jnp.max(s, axis=0, keepdims=True))
    alpha = jnp.exp2(m_old - m_new)
    p = jnp.exp2(s - m_new)
    l_ref[...] = alpha * l_ref[...] + jnp.sum(p, axis=0, keepdims=True)
    acc_ref[...] = alpha * acc_ref[...] + _dot(vt, p.astype(BF16))
    m_ref[...] = m_new


def _slcwin_body(qt_ref, ks_ref, vst_ref, kw_ref, vwt_ref, selt_ref, tbt_ref, oct_ref, glt_ref, ot_ref,
                 ms_ref, ls_ref, as_ref, mw_ref, lw_ref, aw_ref, sbuf_ref, qaug_ref):
    i = pl.program_id(1)
    t0 = i * Q_BLOCK
    R = NSA_GROUP_SIZE
    T = Q_BLOCK
    q4t = jnp.concatenate([qt_ref[r * HEAD_DIM:(r + 1) * HEAD_DIM, :] for r in range(R)], axis=1)
    tq = lax.broadcasted_iota(I32, (1, T), 1)
    rep = lambda mk: jnp.concatenate([mk] * R, axis=1)
    tb4 = jnp.concatenate([tbt_ref[r] for r in range(R)], axis=1)

    def expand(w, base, nkeys):
        keyi = lax.broadcasted_iota(I32, (nkeys, LANES), 0)
        blk = lax.broadcasted_iota(I32, (nkeys, LANES), 1)
        e = jnp.where(blk == base + keyi // NSA_SLC_BLOCK, 1.0, 0.0).astype(BF16)
        return _dot(e, selt_ref[w])

    _flash_init(ms_ref, ls_ref, as_ref)
    _flash_init(mw_ref, lw_ref, aw_ref)
    n_win = selt_ref.shape[0]
    zpad = jnp.zeros((KAUG_W - HEAD_DIM - LANES, R * T), BF16)
    for w in range(n_win):
        qaug_ref[w] = jnp.concatenate([q4t, rep(selt_ref[w] - jnp.asarray(1.0, BF16)), zpad], axis=0)
    far_end = jnp.maximum(t0 - Q_BLOCK, 0) // KEY_TILE * KEY_TILE
    nfar = far_end // KEY_TILE
    tiles_per_window = LANES * NSA_SLC_BLOCK // KEY_TILE

    def score_far(kt, slot):
        r0 = pl.multiple_of(NSA_PADF + kt * KEY_TILE, LANES)
        w = jnp.minimum(kt // tiles_per_window, n_win - 1)
        sbuf_ref[slot] = _dot(ks_ref[pl.ds(r0, KEY_TILE), :], qaug_ref[w])

    def update_far(kt, slot):
        r0 = pl.multiple_of(NSA_PADF + kt * KEY_TILE, LANES)
        _flash_update(sbuf_ref[slot], vst_ref[:, pl.ds(r0, KEY_TILE)], ms_ref, ls_ref, as_ref)

    score_far(0, 0)

    def far2(k2, carry):
        kt = 2 * k2
        score_far(kt + 1, 1)
        update_far(kt, 0)
        score_far(kt + 2, 0)
        update_far(kt + 1, 1)
        return carry

    lax.fori_loop(0, nfar // 2, far2, 0)

    @pl.when(nfar % 2 == 1)
    def _():
        update_far(nfar - 1, 0)

    kq = lambda r0, n: _dot(ks_ref[pl.ds(r0, n), 0:HEAD_DIM], q4t)
    keyn = lax.broadcasted_iota(I32, (NEAR, 1), 0)
    near_ok = (keyn - Q_BLOCK <= tq) & (keyn + t0 - Q_BLOCK >= 0)
    rn = pl.multiple_of(t0 + NSA_PADF - Q_BLOCK, LANES)
    WF = NSA_WINDOW - Q_BLOCK
    rw = pl.multiple_of(t0 + NSA_PADF - NSA_WINDOW, LANES)
    keyw = lax.broadcasted_iota(I32, (WF, 1), 0)
    maskw = (keyw > tq) & (keyw + t0 - NSA_WINDOW >= 0)
    blocks = [jnp.maximum(2 * i - (NSA_WINDOW // NSA_SLC_BLOCK) + 2 * j, 0) for j in range(NSA_WINDOW // Q_BLOCK)]
    mexp = [expand(b // LANES, b % LANES, Q_BLOCK) for b in blocks]
    mask_sm = (jnp.concatenate(mexp[:-1], axis=0) > 0.5) & (keyw + t0 - NSA_WINDOW >= far_end)
    mask_sn = (jnp.concatenate([mexp[-1], expand(2 * i // LANES, 2 * i % LANES, Q_BLOCK)], axis=0) > 0.5) & near_ok
    s_sm = jnp.where(rep(mask_sm), kq(rw, WF), -jnp.inf)
    s_sn = jnp.where(rep(mask_sn), kq(rn, NEAR) + tb4, -jnp.inf)
    s_wf = jnp.where(rep(maskw), _dot(kw_ref[pl.ds(rw, WF), :], q4t), -jnp.inf)
    s_wn = jnp.where(rep(near_ok), _dot(kw_ref[pl.ds(rn, NEAR), :], q4t) + tb4, -jnp.inf)
    _flash_update(s_sm, vst_ref[:, pl.ds(rw, WF)], ms_ref, ls_ref, as_ref)
    _flash_update(s_sn, vst_ref[:, pl.ds(rn, NEAR)], ms_ref, ls_ref, as_ref)
    _flash_update(s_wf, vwt_ref[:, pl.ds(rw, WF)], mw_ref, lw_ref, aw_ref)
    _flash_update(s_wn, vwt_ref[:, pl.ds(rn, NEAR)], mw_ref, lw_ref, aw_ref)
    o_s = as_ref[...] / jnp.maximum(ls_ref[...], 1e-30)
    o_w = aw_ref[...] / jnp.maximum(lw_ref[...], 1e-30)

    gates = jax.nn.sigmoid(glt_ref[...])
    for r in range(R):
        hs = slice(r * HEAD_DIM, (r + 1) * HEAD_DIM)
        qs = slice(r * T, (r + 1) * T)
        o = (gates[3 * r:3 * r + 1, :] * oct_ref[hs, :]
             + gates[3 * r + 1:3 * r + 2, :] * o_s[:, qs]
             + gates[3 * r + 2:3 * r + 3, :] * o_w[:, qs])
        ot_ref[hs, :] = o.astype(BF16)


def _slcwin(nqt, ks, vst, kw, vwt, selt, tbt, oct, glt, S):
    G, R = NSA_KV_GROUPS, NSA_GROUP_SIZE
    spad = ks.shape[1]
    NW = selt.shape[1]
    once = pl.Buffered(1)
    kaspec = pl.BlockSpec((None, spad, KAUG_W), lambda g, i: (g, 0, 0), pipeline_mode=once)
    kspec = pl.BlockSpec((None, spad, HEAD_DIM), lambda g, i: (g, 0, 0), pipeline_mode=once)
    vspec = pl.BlockSpec((None, HEAD_DIM, spad), lambda g, i: (g, 0, 0), pipeline_mode=once)
    qspec = pl.BlockSpec((R * HEAD_DIM, Q_BLOCK), lambda g, i: (g, i))
    st = lambda n: pltpu.VMEM((n, R * Q_BLOCK), F32)
    return pl.pallas_call(
        _slcwin_body,
        grid=(G, S // Q_BLOCK),
        in_specs=[qspec, kaspec, vspec, kspec, vspec,
                  pl.BlockSpec((None, NW, LANES, Q_BLOCK), lambda g, i: (g, 0, 0, i)),
                  pl.BlockSpec((R, NEAR, Q_BLOCK), lambda g, i: (g, 0, 0)),
                  qspec,
                  pl.BlockSpec((None, 16, Q_BLOCK), lambda g, i: (g, 0, i))],
        out_specs=qspec,
        out_shape=jax.ShapeDtypeStruct((NSA_W, S), BF16),
        scratch_shapes=[st(1), st(1), st(HEAD_DIM), st(1), st(1), st(HEAD_DIM),
                        pltpu.VMEM((2, KEY_TILE, R * Q_BLOCK), F32),
                        pltpu.VMEM((NW, KAUG_W, R * Q_BLOCK), BF16)],
        compiler_params=_cparams(("arbitrary", "arbitrary")),
        name="slcwin",
    )(nqt, ks, vst, kw, vwt, selt, tbt, oct, glt)


INT_MIN = -2 ** 31
INT_MAX = 2 ** 31 - 1
MAX_PEEL_ROUNDS = 64
UNTESTED_BITS = 14
NEG_INF_KEY = int(np.array(-np.inf, np.float32).view(np.int32)) ^ 0x7FFFFFFF


def _dsa_body(dqt_ref, dk_ref, dvt_ref, iqt_ref, ikt_ref, iwt_ref, tbt_ref, ltri_ref, ot_ref,
              keys_ref, m_ref, l_ref, acc_ref, sbuf_ref, *, S, KEEP):
    i = pl.program_id(0)
    t0 = i * Q_BLOCK
    T = Q_BLOCK
    H = DSA_HEADS
    ntile = (t0 + Q_BLOCK + KEY_TILE - 1) // KEY_TILE
    tq = t0 + lax.broadcasted_iota(I32, (1, T), 1)
    key512 = lax.broadcasted_iota(I32, (KEY_TILE, 1), 0)
    ktile = lambda kt: keys_ref[pl.ds(pl.multiple_of(Q_BLOCK + kt * KEY_TILE, LANES), KEY_TILE), :]

    keys_ref[0:Q_BLOCK, :] = jnp.full((Q_BLOCK, T), NEG_INF_KEY, I32)
    iqt = iqt_ref[...]
    iwt = iwt_ref[...] * (IDX_HEADS ** -0.5)
    qis = [jnp.concatenate([iqt[(2 * j) * IDX_DIM:(2 * j + 1) * IDX_DIM, :],
                            iqt[(2 * j + 1) * IDX_DIM:(2 * j + 2) * IDX_DIM, :]], axis=1)
           for j in range(IDX_HEADS // 2)]
    wrow = [iwt[h:h + 1, :] for h in range(IDX_HEADS)]

    def idx_dots(kt, slot):
        ik = ikt_ref[:, pl.ds(pl.multiple_of(kt * KEY_TILE, KEY_TILE), KEY_TILE)]
        for j in range(IDX_HEADS // 2):
            sbuf_ref[slot, j] = _dot_tn(ik, qis[j])

    def idx_keys(kt, slot):
        c0 = pl.multiple_of(kt * KEY_TILE, KEY_TILE)
        sc = jnp.zeros((KEY_TILE, T), F32)
        for j in range(IDX_HEADS // 2):
            d = jnp.maximum(sbuf_ref[slot, j], 0.0)
            sc = sc + wrow[2 * j] * d[:, 0:T]
            sc = sc + wrow[2 * j + 1] * d[:, T:2 * T]
        sc = jnp.where(sc == 0.0, 0.0, sc)
        sc = jnp.where(c0 + key512 <= tq, sc, -jnp.inf)
        bits = pltpu.bitcast(sc, I32)
        keys_ref[pl.ds(pl.multiple_of(Q_BLOCK + c0, LANES), KEY_TILE), :] = jnp.where(
            bits < 0, bits ^ 0x7FFFFFFF, bits)

    idx_dots(0, 0)

    def score2(k2, carry):
        kt = 2 * k2
        idx_dots(kt + 1, 1)
        idx_keys(kt, 0)
        idx_dots(kt + 2, 0)
        idx_keys(kt + 1, 1)
        return carry

    lax.fori_loop(0, (ntile + 1) // 2, score2, 0)

    def count(pred):
        def body(kt, acc):
            hit = jnp.where(pred(ktile(kt), kt * KEY_TILE + key512), 1, 0)
            return acc + jnp.sum(hit.reshape(KEY_TILE // SUBLANES, SUBLANES, T), axis=0)
        acc = lax.fori_loop(0, ntile, body, jnp.zeros((SUBLANES, T), I32))
        return jnp.sum(acc, axis=0, keepdims=True)

    def min_at_least(lowest):
        def body(kt, acc):
            kk = ktile(kt)
            v = jnp.where(kk >= lowest, kk, INT_MAX)
            return jnp.minimum(acc, jnp.min(v.reshape(KEY_TILE // SUBLANES, SUBLANES, T), axis=0))
        acc = lax.fori_loop(0, ntile, body, jnp.full((SUBLANES, T), INT_MAX, I32))
        return jnp.min(acc, axis=0, keepdims=True)

    def any_open(st, over):
        _, excess, fin, _ = st
        return jnp.max(jnp.where((fin == 0) & (excess > over), 1, 0)) > 0

    def bisect(b, st):
        thr, excess, fin, need = st
        cand = thr + lax.shift_left(jnp.int32(1), 31 - b)
        n = count(lambda kk, ab: kk >= cand)
        ok = (n >= KEEP) & (fin == 0)
        return jnp.where(ok, cand, thr), jnp.where(ok, n - KEEP, excess), fin, need

    def peel(st):
        thr, excess, fin, need = st
        low = min_at_least(thr)
        n_low = count(lambda kk, ab: kk == low)
        active = (fin == 0) & (excess > 0)
        tied = active & (n_low > excess)
        drop = active & (n_low <= excess)
        return (jnp.where(tied, low, jnp.where(drop, low + 1, thr)), jnp.where(drop, excess - n_low, excess),
                jnp.where(tied, 1, fin), jnp.where(tied, n_low - excess, need))

    zero = jnp.zeros((1, T), I32)
    st = (jnp.full((1, T), INT_MIN, I32), KEY_TILE * ntile - KEEP + zero, zero, zero)
    st = peel(bisect(0, st))
    st = lax.fori_loop(1, UNTESTED_BITS, bisect, st)
    _, st = lax.while_loop(lambda c: (c[0] < 32) & any_open(c[1], 1),
                           lambda c: (c[0] + 1, bisect(c[0], c[1])), (jnp.int32(UNTESTED_BITS), st))
    _, st = lax.while_loop(lambda c: (c[0] < MAX_PEEL_ROUNDS) & any_open(c[1], 0),
                           lambda c: (c[0] + 1, peel(c[1])), (jnp.int32(0), st))
    thr, _, fin, need = st
    tie = (fin == 1) & (thr > NEG_INF_KEY)

    def tie_cut():
        needf = need.astype(F32)

        def body(kt, c):
            before, jmax = c
            is_tie = ktile(kt) == thr

            def ranked():
                tied = jnp.where(is_tie, 1.0, 0.0)
                rank = before + _dot(ltri_ref[...], tied.astype(BF16))
                admit = is_tie & (rank < needf)
                jm = jnp.max(jnp.where(admit, kt * KEY_TILE + key512, -1), axis=0, keepdims=True)
                return before + jnp.sum(tied, axis=0, keepdims=True), jnp.maximum(jmax, jm)

            return lax.cond(jnp.max(jnp.where(is_tie, 1, 0)) > 0, ranked, lambda: (before, jmax))

        _, jmax = lax.fori_loop(0, ntile, body, (jnp.zeros((1, T), F32), jnp.full((1, T), -1, I32)))
        return jnp.where(tie, jmax + 1, S)

    jcut = lax.cond(jnp.max(jnp.where(tie, 1, 0)) > 0, tie_cut, lambda: jnp.full((1, T), S, I32))

    def chosen(kk, ab):
        return (kk > thr) | ((kk == thr) & (ab < jcut))

    _flash_init(m_ref, l_ref, acc_ref)
    lo = lax.broadcasted_iota(I32, (LANES, 1), 0) < HEAD_DIM
    qpairs = []
    for pr in range(H // 2):
        qp = dqt_ref[pr * LANES:(pr + 1) * LANES, :]
        zero = jnp.zeros_like(qp)
        qpairs.append(jnp.concatenate([jnp.where(lo, qp, zero), jnp.where(lo, zero, qp)], axis=1))

    def scores(r0, nkeys, mask, bias):
        mask2 = jnp.concatenate([mask, mask], axis=1)
        out = []
        for pr in range(H // 2):
            s = _dot(dk_ref[pl.ds(r0, nkeys), pr * LANES:(pr + 1) * LANES], qpairs[pr])
            if bias is not None:
                s = s + jnp.concatenate([bias[2 * pr], bias[2 * pr + 1]], axis=1)
            out.append(jnp.where(mask2, s, -jnp.inf))
        return out

    def update(pr, s, r0, nkeys):
        m_old = m_ref[pr]
        m_new = jnp.maximum(m_old, jnp.max(s, axis=0, keepdims=True))
        alpha = jnp.exp2(m_old - m_new)
        p = jnp.exp2(s - m_new)
        l_ref[pr] = alpha * l_ref[pr] + jnp.sum(p, axis=0, keepdims=True)
        m_ref[pr] = m_new
        vt = dvt_ref[pr * LANES:(pr + 1) * LANES, pl.ds(r0, nkeys)]
        acc_ref[pr] = alpha * acc_ref[pr] + _dot(vt, p.astype(BF16))

    far_end = t0 - Q_BLOCK
    nfar = jnp.maximum(far_end + KEY_TILE - 1, 0) // KEY_TILE
    far_row = lambda kt: pl.multiple_of(DSA_PADF + kt * KEY_TILE, LANES)

    def score_far(kt, slot):
        ab = kt * KEY_TILE + key512
        mask = chosen(ktile(jnp.minimum(kt, ntile - 1)), ab) & (ab < far_end)
        for pr, s in enumerate(scores(far_row(kt), KEY_TILE, mask, None)):
            sbuf_ref[slot, pr] = s

    def update_far(kt, slot):
        for pr in range(H // 2):
            update(pr, sbuf_ref[slot, pr], far_row(kt), KEY_TILE)

    score_far(0, 0)

    def far2(k2, carry):
        kt = 2 * k2
        score_far(kt + 1, 1)
        update_far(kt, 0)
        score_far(kt + 2, 0)
        update_far(kt + 1, 1)
        return carry

    lax.fori_loop(0, (nfar + 1) // 2, far2, 0)

    abn = lax.broadcasted_iota(I32, (NEAR, 1), 0) + t0 - Q_BLOCK
    kkn = keys_ref[pl.ds(pl.multiple_of(t0, LANES), NEAR), :]
    rn = pl.multiple_of(t0 + DSA_PADF - Q_BLOCK, LANES)
    for pr, s in enumerate(scores(rn, NEAR, chosen(kkn, abn) & (abn <= tq) & (abn >= 0), tbt_ref)):
        update(pr, s, rn, NEAR)

    for pr in range(H // 2):
        o = acc_ref[pr] / jnp.maximum(l_ref[pr], 1e-30)
        ot_ref[pr * LANES:(pr + 1) * LANES, :] = jnp.where(lo, o[:, 0:T], o[:, T:2 * T]).astype(BF16)


def _dsa(dqt, dkp, dvtp, iqt, ikt, iwt, tbt, ltri, S):
    KEEP = min(DSA_TOPK_MAX, S // 4)
    H = DSA_HEADS
    T = Q_BLOCK
    once = pl.Buffered(1)
    full = lambda a: pl.BlockSpec(a.shape, lambda i: (0,) * a.ndim, pipeline_mode=once)
    return pl.pallas_call(
        functools.partial(_dsa_body, S=S, KEEP=KEEP),
        grid=(S // T,),
        in_specs=[pl.BlockSpec((DSA_W, T), lambda i: (0, i)),
                  full(dkp), full(dvtp),
                  pl.BlockSpec((IDX_HEADS * IDX_DIM, T), lambda i: (0, i)),
                  full(ikt),
                  pl.BlockSpec((IDX_HEADS, T), lambda i: (0, i)),
                  full(tbt), full(ltri)],
        out_specs=pl.BlockSpec((DSA_W, T), lambda i: (0, i)),
        out_shape=jax.ShapeDtypeStruct((DSA_W, S), BF16),
        scratch_shapes=[pltpu.VMEM((Q_BLOCK + S + KEY_TILE, T), I32),
                        pltpu.VMEM((H // 2, 1, 2 * T), F32),
                        pltpu.VMEM((H // 2, 1, 2 * T), F32),
                        pltpu.VMEM((H // 2, LANES, 2 * T), F32),
                        pltpu.VMEM((2, H // 2, KEY_TILE, 2 * T), F32)],
        compiler_params=_cparams(("arbitrary",)),
        name="dsa",
    )(dqt, dkp, dvtp, iqt, ikt, iwt, tbt, ltri)


def _split_bf16(a):
    hi = a.astype(BF16)
    return hi, (a - hi.astype(F32)).astype(BF16)


def _merge_body(x_ref, oa_ref, ob_ref, gab_ref, wa_ref, wb_ref, wo_ref, gffn_ref, wr_ref, br_ref,
                x1_ref, h2_ref, ridx_ref, rw_ref):
    a = _dot(oa_ref[...], wa_ref[...])
    b = _dot(ob_ref[...], wb_ref[...])
    gab = gab_ref[...]
    merged = gab[:, :D_MODEL] * a + gab[:, D_MODEL:] * b
    x1 = x_ref[...] + _dot(merged.astype(BF16), wo_ref[...])
    x1_ref[...] = x1
    h2 = x1 * lax.rsqrt(jnp.mean(x1 * x1, axis=-1, keepdims=True) + EPS) * gffn_ref[...]
    h2_ref[...] = h2
    hh, hl = _split_bf16(h2)
    wh, wl = _split_bf16(wr_ref[...])
    logits = _dot(hh, wh) + _dot(hh, wl) + _dot(hl, wh) + br_ref[...]
    lane = lax.broadcasted_iota(I32, (1, LANES), 1)
    idx_out = jnp.zeros(logits.shape, I32)
    val_out = jnp.full(logits.shape, -jnp.inf, F32)
    for k in range(TOP_K):
        mx = jnp.max(logits, axis=-1, keepdims=True)
        first = jnp.min(jnp.where(logits == mx, lane, LANES), axis=-1, keepdims=True)
        idx_out = jnp.where(lane == k, first, idx_out)
        val_out = jnp.where(lane == k, mx, val_out)
        logits = jnp.where(lane == first, -jnp.inf, logits)
    e = jnp.exp(val_out - jnp.max(val_out, axis=-1, keepdims=True))
    ridx_ref[...] = idx_out
    rw_ref[...] = e / jnp.sum(e, axis=-1, keepdims=True)


def _merge(x2, oa, ob, gab, wa, wb, wo, gffn, wr, br, tm=256):
    S = x2.shape[0]
    row = lambda n: pl.BlockSpec((tm, n), lambda i: (i, 0))
    full = lambda a: pl.BlockSpec(a.shape, lambda i: (0,) * a.ndim)
    return pl.pallas_call(
        _merge_body,
        grid=(S // tm,),
        in_specs=[row(D_MODEL), row(NSA_W), row(DSA_W), row(2 * D_MODEL),
                  full(wa), full(wb), full(wo), full(gffn), full(wr), full(br)],
        out_specs=[row(D_MODEL), row(D_MODEL), row(LANES), row(LANES)],
        out_shape=[jax.ShapeDtypeStruct((S, D_MODEL), F32), jax.ShapeDtypeStruct((S, D_MODEL), F32),
                   jax.ShapeDtypeStruct((S, LANES), I32), jax.ShapeDtypeStruct((S, LANES), F32)],
        compiler_params=_cparams(("arbitrary",)),
        name="merge",
    )(x2, oa, ob, gab, wa, wb, wo, gffn, wr, br)


def _rank_body(ridx_ref, rank_ref, cnt_ref, carry_ref):
    @pl.when(pl.program_id(0) == 0)
    def _():
        carry_ref[...] = jnp.zeros(carry_ref.shape, F32)

    ridx = ridx_ref[...]
    tm = ridx.shape[0]
    lane = lax.broadcasted_iota(I32, (1, LANES), 1)
    hits = [lane == ridx[:, k:k + 1] for k in range(TOP_K)]
    member = jnp.zeros((tm, LANES), F32)
    for hk in hits:
        member = member + jnp.where(hk, 1.0, 0.0)
    ri = lax.broadcasted_iota(I32, (tm, tm), 0)
    ci = lax.broadcasted_iota(I32, (tm, tm), 1)
    lower = jnp.where(ci < ri, 1.0, 0.0).astype(BF16)
    before = _dot(lower, member.astype(BF16)) + carry_ref[0:1, :]
    out = jnp.zeros((tm, LANES), I32)
    for k, hk in enumerate(hits):
        rk = jnp.sum(jnp.where(hk, before, 0.0), axis=-1, keepdims=True)
        out = jnp.where(lane == k, rk.astype(I32), out)
    rank_ref[...] = out
    total = carry_ref[0:1, :] + jnp.sum(member, axis=0, keepdims=True)
    carry_ref[...] = jnp.broadcast_to(total, carry_ref.shape)
    cnt_ref[...] = jnp.broadcast_to(total, cnt_ref.shape).astype(I32)


def _rank(ridx, tm=512):
    S = ridx.shape[0]
    return pl.pallas_call(
        _rank_body,
        grid=(S // tm,),
        in_specs=[pl.BlockSpec((tm, LANES), lambda i: (i, 0))],
        out_specs=[pl.BlockSpec((tm, LANES), lambda i: (i, 0)), pl.BlockSpec((8, LANES), lambda i: (0, 0))],
        out_shape=[jax.ShapeDtypeStruct((S, LANES), I32), jax.ShapeDtypeStruct((8, LANES), I32)],
        scratch_shapes=[pltpu.VMEM((8, LANES), F32)],
        compiler_params=_cparams(("arbitrary",)),
        name="rank",
    )(ridx)


def _dispatch_body(dest_ref, h_ref, zero_ref, xe_ref, sem):
    del zero_ref
    i = pl.program_id(0)
    tm = h_ref.shape[0]

    def start(r, c):
        for k in range(TOP_K):
            d = dest_ref[(i * tm + r) * TOP_K + k]
            pltpu.make_async_copy(h_ref.at[pl.ds(r, 1)], xe_ref.at[pl.ds(d, 1)],
                                  sem).start(priority=k % 2)
        return c

    lax.fori_loop(0, tm, start, 0, unroll=ROW_DMA_UNROLL)
    for _ in range(TOP_K):
        pltpu.make_async_copy(h_ref, xe_ref.at[pl.ds(0, tm)], sem).wait()


def _dispatch(dest, h2, zeros_p, tm=128):
    S = h2.shape[0]
    return pl.pallas_call(
        _dispatch_body,
        grid_spec=pltpu.PrefetchScalarGridSpec(
            num_scalar_prefetch=1,
            grid=(S // tm,),
            in_specs=[pl.BlockSpec((tm, D_MODEL), lambda i, d: (i, 0)),
                      pl.BlockSpec(memory_space=pl.ANY)],
            out_specs=pl.BlockSpec(memory_space=pl.ANY),
            scratch_shapes=[pltpu.SemaphoreType.DMA(())]),
        out_shape=jax.ShapeDtypeStruct(zeros_p.shape, zeros_p.dtype),
        input_output_aliases={2: 0},
        compiler_params=_cparams(("arbitrary",)),
        name="dispatch",
    )(dest, h2, zeros_p)


def _experts_body(ce_ref, nu_ref, x_ref, w1_ref, b1_ref, w2_ref, b2_ref, y_ref, w1b_ref, w2b_ref):
    c = pl.program_id(0)

    @pl.when((c == 0) | (ce_ref[c] != ce_ref[jnp.maximum(c - 1, 0)]))
    def _():
        w1b_ref[...] = w1_ref[...].astype(BF16)
        w2b_ref[...] = w2_ref[...].astype(BF16)

    @pl.when(c < nu_ref[0])
    def _():
        gu = _dot(x_ref[...].astype(BF16), w1b_ref[...]) + b1_ref[...]
        gate = jnp.minimum(gu[:, :D_FF], SWIGLU_LIMIT)
        lin = jnp.clip(gu[:, D_FF:], -SWIGLU_LIMIT, SWIGLU_LIMIT)
        act = (lin + 1.0) * gate * jax.nn.sigmoid(SWIGLU_ALPHA * gate)
        y_ref[...] = _dot(act.astype(BF16), w2b_ref[...]) + b2_ref[...]

    @pl.when(c >= nu_ref[0])
    def _():
        y_ref[...] = jnp.zeros(y_ref.shape, F32)


def _experts(chunk_e, n_used, xe, w1, b1, w2, b2):
    P = xe.shape[0]
    nch = P // MOE_ROWS
    return pl.pallas_call(
        _experts_body,
        grid_spec=pltpu.PrefetchScalarGridSpec(
            num_scalar_prefetch=2,
            grid=(nch,),
            in_specs=[pl.BlockSpec((MOE_ROWS, D_MODEL), lambda c, ce, nu: (c, 0)),
                      pl.BlockSpec((None, D_MODEL, 2 * D_FF), lambda c, ce, nu: (ce[c], 0, 0)),
                      pl.BlockSpec((None, 1, 2 * D_FF), lambda c, ce, nu: (ce[c], 0, 0)),
                      pl.BlockSpec((None, D_FF, D_MODEL), lambda c, ce, nu: (ce[c], 0, 0)),
                      pl.BlockSpec((None, 1, D_MODEL), lambda c, ce, nu: (ce[c], 0, 0))],
            out_specs=pl.BlockSpec((MOE_ROWS, D_MODEL), lambda c, ce, nu: (c, 0)),
            scratch_shapes=[pltpu.VMEM((D_MODEL, 2 * D_FF), BF16), pltpu.VMEM((D_FF, D_MODEL), BF16)]),
        out_shape=jax.ShapeDtypeStruct((P, D_MODEL), F32),
        compiler_params=_cparams(("arbitrary",)),
        name="experts",
    )(chunk_e, n_used, xe, w1, b1, w2, b2)


def _final_body(dest_ref, x1_ref, rw_ref, p_ref, gple_ref, wg_ref, wp_ref, y_ref, o_ref, rows_ref, sems):
    i = pl.program_id(0)
    tm = x1_ref.shape[0]

    def gather(blk, slot):
        def start(r, c):
            for k in range(TOP_K):
                d = dest_ref[(blk * tm + r) * TOP_K + k]
                pltpu.make_async_copy(y_ref.at[pl.ds(d, 1)], rows_ref.at[slot, k, pl.ds(r, 1)],
                                      sems.at[slot]).start(priority=k % 2)
            return c
        lax.fori_loop(0, tm, start, 0, unroll=ROW_DMA_UNROLL)

    @pl.when(i == 0)
    def _():
        gather(0, 0)

    @pl.when(i + 1 < pl.num_programs(0))
    def _():
        gather(i + 1, (i + 1) % 2)

    slot = i % 2
    for k in range(TOP_K):
        pltpu.make_async_copy(y_ref.at[pl.ds(0, tm)], rows_ref.at[slot, k], sems.at[slot]).wait()
    rw = rw_ref[...]
    x2 = x1_ref[...]
    for k in range(TOP_K):
        x2 = x2 + rw[:, k:k + 1] * rows_ref[slot, k]
    hn = x2 * lax.rsqrt(jnp.mean(x2 * x2, axis=-1, keepdims=True) + EPS) * gple_ref[...]
    gate = jax.nn.sigmoid(_dot(hn.astype(BF16), wg_ref[...]))
    o_ref[...] = x2 + gate * _dot(p_ref[...].astype(BF16), wp_ref[...])


def _final(dest, x1, rw, p2, gple, wg, wp, yexp, tm=128):
    S = x1.shape[0]
    row = lambda n: pl.BlockSpec((tm, n), lambda i, d: (i, 0))
    full = lambda a: pl.BlockSpec(a.shape, lambda i, d: (0,) * a.ndim)
    return pl.pallas_call(
        _final_body,
        grid_spec=pltpu.PrefetchScalarGridSpec(
            num_scalar_prefetch=1,
            grid=(S // tm,),
            in_specs=[row(D_MODEL), row(LANES), row(PLE_DIM), full(gple), full(wg), full(wp),
                      pl.BlockSpec(memory_space=pl.ANY)],
            out_specs=row(D_MODEL),
            scratch_shapes=[pltpu.VMEM((2, TOP_K, tm, D_MODEL), F32), pltpu.SemaphoreType.DMA((2,))]),
        out_shape=jax.ShapeDtypeStruct((S, D_MODEL), F32),
        compiler_params=_cparams(("arbitrary",)),
        name="final",
    )(dest, x1, rw, p2, gple, wg, wp, yexp)


def _distance_bias(rel_cols):
    return (rel_cols[_rel_bucket_table(NEAR)] - rel_cols[REL_BUCKETS - 1][None, :]).T * LOG2E


def _near_bias_t(rel_cols):
    h = rel_cols.shape[1]
    g = jnp.pad(_distance_bias(rel_cols), ((0, 0), (Q_BLOCK, Q_BLOCK)))
    period = 2 * NEAR
    skew = jnp.tile(g, (1, NEAR))[:, :NEAR * (period - 1)].reshape(h, NEAR, period - 1)
    return skew[:, :, NEAR:NEAR + Q_BLOCK]


def _cmp_bias_t(rel_cols):
    g = jnp.pad(_distance_bias(rel_cols), ((0, 0), (NEAR, Q_BLOCK)))
    rows = []
    for j in range(4):
        for k in range(4):
            c0 = -NSA_CMP_STRIDE * j - (NSA_CMP_LEN - 1) + NSA_SLC_BLOCK * (k - 1)
            rows.append(g[:, c0 + NEAR:c0 + NEAR + Q_BLOCK])
    return jnp.stack(rows, axis=1)


def kernel(x, p, w_in, g_mix, g_q_nsa, g_k_nsa, g_q_dsa, g_k_dsa, pe_ck, w_ck1, w_ck2, pe_cv, w_cv1, w_cv2,
           w_branch_a, w_branch_b, w_out, rel_bias, g_ffn, w_router, b_router, w_e1, b_e1, w_e2, b_e2,
           g_ple, w_ple, w_ple_gate):
    B, S, D = x.shape
    assert B == 1 and D == D_MODEL and S % KEY_TILE == 0 and w_in.shape[0] == 1
    G, R = NSA_KV_GROUPS, NSA_GROUP_SIZE
    x2 = x.reshape(S, D)

    w0 = w_in[0]
    w_perm = jnp.concatenate([w0[:, 0:1280], w0[:, 1304:3096], w0[:, 3136:5184], w0[:, 1280:1304],
                              w0[:, 3096:3136], jnp.zeros((D, D_IN_PAD - w0.shape[1]), F32)], axis=1).astype(BF16)
    two = lambda g: jnp.tile(g.reshape(1, HEAD_DIM), (1, 2))
    qscale = HEAD_DIM ** -0.5 * LOG2E
    gains = jnp.concatenate([two(g_q_nsa[0]) * qscale, two(g_k_nsa[0]),
                             two(g_q_dsa[0]) * qscale, two(g_k_dsa[0])], axis=0)
    (nq, cmpraw, kslc, vslc, kwin, vwin, dq, dk, dv, iq, gab, small) = _inproj(
        x2, g_mix[0].reshape(1, D), w_perm, gains)

    NS = S // NSA_SLC_BLOCK
    nchunk = S // NSA_CMP_STRIDE
    xc = cmpraw.reshape(nchunk, NSA_CMP_STRIDE, 2 * G, HEAD_DIM).transpose(2, 0, 1, 3).reshape(
        2 * G, nchunk, NSA_CMP_STRIDE * HEAD_DIM)
    planes = lambda a: a.reshape(2 * G, NS, 4, -1).transpose(0, 2, 1, 3).reshape(2 * G, nchunk, -1)
    xa = planes(xc)
    xb = planes(jnp.roll(xc, -1, axis=1))
    half = NSA_CMP_STRIDE * HEAD_DIM
    w1 = jnp.stack([w_ck1[0], w_cv1[0]]).reshape(2, 2 * half, HEAD_DIM)
    pe = jnp.stack([pe_ck[0], pe_cv[0]]).reshape(2, 2 * half, 1)
    w2 = jnp.stack([w_ck2[0], w_cv2[0]]).astype(BF16)
    kvc = _compress(xa, xb, w1[:, :half].astype(BF16), w1[:, half:].astype(BF16), w1, pe, w2,
                    g_k_nsa[0].reshape(1, HEAD_DIM))

    rel_nsa = rel_bias[:, :NSA_HEADS]
    rel_dsa = rel_bias[:, NSA_HEADS:]
    nqt = nq.T
    oct, selt = _cmp(nqt, kvc[:G], kvc[G:].transpose(0, 2, 1), _cmp_bias_t(rel_nsa), S)
    kgroups = lambda a: jnp.pad(a.reshape(S, G, HEAD_DIM).transpose(1, 0, 2),
                                ((0, 0), (NSA_PADF, TAIL_PAD), (0, 0)))
    vgroups = lambda a: jnp.pad(a.reshape(S, G, HEAD_DIM).transpose(1, 2, 0),
                                ((0, 0), (0, 0), (NSA_PADF, TAIL_PAD)))
    glt = small[:, SMALL_GATE:SMALL_GATE + 3 * NSA_HEADS].reshape(S, G, 3 * R).transpose(1, 2, 0)
    glt = jnp.pad(glt, ((0, 0), (0, 16 - 3 * R), (0, 0)))
    blk_lane = (np.arange(S) // NSA_SLC_BLOCK) % LANES
    onehot = jnp.asarray(np.eye(LANES, dtype=np.float32)[blk_lane] * MASK_BIG, BF16)
    kaug = jnp.concatenate([kslc.reshape(S, G, HEAD_DIM).transpose(1, 0, 2),
                            jnp.broadcast_to(onehot, (G, S, LANES)),
                            jnp.zeros((G, S, KAUG_W - HEAD_DIM - LANES), BF16)], axis=-1)
    kaug = jnp.pad(kaug, ((0, 0), (NSA_PADF, TAIL_PAD), (0, 0)))
    o_at = _slcwin(nqt, kaug, vgroups(vslc), kgroups(kwin), vgroups(vwin), selt,
                   _near_bias_t(rel_nsa), oct, glt, S)

    dkp = jnp.pad(dk, ((DSA_PADF, TAIL_PAD), (0, 0)))
    dvtp = jnp.pad(dv.T, ((0, 0), (DSA_PADF, TAIL_PAD)))
    ikt = jnp.pad(small[:, SMALL_IK:SMALL_IK + IDX_DIM].T.astype(BF16), ((0, 0), (0, 2 * KEY_TILE)))
    iwt = small[:, SMALL_IW:SMALL_IW + IDX_HEADS].T
    ltri = jnp.asarray(np.tril(np.ones((KEY_TILE, KEY_TILE), np.float32), -1), BF16)
    o_bt = _dsa(dq.T, dkp, dvtp, iq.T, ikt, iwt, _near_bias_t(rel_dsa), ltri, S)

    wr = jnp.pad(w_router[0], ((0, 0), (0, LANES - N_EXPERTS)))
    br = jnp.pad(b_router[0].reshape(1, N_EXPERTS), ((0, 0), (0, LANES - N_EXPERTS)), constant_values=-1e30)
    x1, h2, ridx, rw = _merge(x2, o_at.T, o_bt.T, gab, w_branch_a[0].astype(BF16), w_branch_b[0].astype(BF16),
                              w_out[0].astype(BF16), g_ffn[0].reshape(1, D), wr, br)

    rank, cnt = _rank(ridx)
    counts = cnt[0, :N_EXPERTS]
    padded = (counts + MOE_ROWS - 1) // MOE_ROWS * MOE_ROWS
    pad_end = jnp.cumsum(padded)
    pad_start = pad_end - padded
    eidx = ridx[:, :TOP_K]
    dest = (jnp.sum(jnp.where(eidx[..., None] == jnp.arange(N_EXPERTS), pad_start, 0), axis=-1)
            + rank[:, :TOP_K]).reshape(-1).astype(I32)
    n_chunks = -(-(S * TOP_K) // MOE_ROWS) + N_EXPERTS
    chunk_start = jnp.arange(n_chunks, dtype=I32) * MOE_ROWS
    chunk_e = jnp.minimum(jnp.sum(chunk_start[:, None] >= pad_end[None, :], axis=-1), N_EXPERTS - 1).astype(I32)
    n_used = (pad_end[-1] // MOE_ROWS).astype(I32).reshape(1)
    xe = _dispatch(dest, h2, jnp.zeros((n_chunks * MOE_ROWS, D), F32))
    yexp = _experts(chunk_e, n_used, xe, w_e1[0], b_e1[0].reshape(N_EXPERTS, 1, 2 * D_FF),
                    w_e2[0], b_e2[0].reshape(N_EXPERTS, 1, D))

    out = _final(dest, x1, rw, p[0].reshape(S, PLE_DIM), g_ple[0].reshape(1, D),
                 w_ple_gate[0].astype(BF16), w_ple[0].astype(BF16), yexp)
    return out.reshape(B, S, D)
```

```python
import functools
import math

import numpy as np
import jax
import jax.numpy as jnp
from jax import lax
from jax.experimental import pallas as pl
from jax.experimental.pallas import tpu as pltpu

F32 = jnp.float32
BF16 = jnp.bfloat16
I32 = jnp.int32

D_MODEL = 1024
PLE_DIM = 256
HEAD_DIM = 64
NSA_HEADS = 8
NSA_KV_GROUPS = 2
NSA_GROUP_SIZE = NSA_HEADS // NSA_KV_GROUPS
NSA_CMP_LEN = 32
NSA_CMP_STRIDE = 16
NSA_SLC_BLOCK = 64
NSA_SLC_TOPK = 16
NSA_LOCAL_BLOCKS = 2
NSA_WINDOW = 512
DSA_HEADS = 8
IDX_HEADS = 8
IDX_DIM = 32
DSA_TOPK_MAX = 256
Q_BLOCK = 128
REL_BUCKETS = 32
REL_MAX_EXACT = 16
REL_MAX_DIST = 128
N_EXPERTS = 32
TOP_K = 4
D_FF = 1024
SWIGLU_LIMIT = 7.0
SWIGLU_ALPHA = 1.702
MOE_ROWS = 256
EPS = 1e-6
LOG2E = math.log2(math.e)

NSA_W = NSA_HEADS * HEAD_DIM
DSA_W = DSA_HEADS * HEAD_DIM

LANES = 128
SUBLANES = 8
KEY_TILE = 512
NEAR = 2 * Q_BLOCK
NSA_PADF = NSA_WINDOW
DSA_PADF = Q_BLOCK
KAUG_W = 2 * LANES
MASK_BIG = 2.0 ** 100
TAIL_PAD = 3 * KEY_TILE
VMEM_LIMIT = 56 * 1024 * 1024
ROW_DMA_UNROLL = 8

OFF_NQ, OFF_CMP, OFF_KSLC, OFF_VSLC, OFF_KWIN, OFF_VWIN = 0, 512, 768, 896, 1024, 1152
OFF_DQ, OFF_DK, OFF_DV, OFF_IQ, OFF_GAB, OFF_SMALL = 1280, 1792, 2304, 2816, 3072, 5120
D_IN_PAD = 5248
SMALL_GATE, SMALL_IK, SMALL_IW = 0, 24, 56


def _rel_bucket_table(n):
    d = np.arange(n)
    nf = np.maximum(d, 1).astype(np.float64)
    large = REL_MAX_EXACT + (np.log(nf / REL_MAX_EXACT) / math.log(REL_MAX_DIST / REL_MAX_EXACT)
                             * (REL_BUCKETS - REL_MAX_EXACT)).astype(np.int64)
    large = np.minimum(large, REL_BUCKETS - 1)
    return np.where(d < REL_MAX_EXACT, d, large)


def _cparams(sem, vmem=VMEM_LIMIT):
    return pltpu.CompilerParams(dimension_semantics=sem, vmem_limit_bytes=vmem)


def _dot(a, b):
    return jnp.dot(a, b, preferred_element_type=F32)


def _dot_tn(a, b):
    return lax.dot_general(a, b, (((0,), (0,)), ((), ())), preferred_element_type=F32)


def _inproj_body(x_ref, gmix_ref, w_ref, gains_ref,
                 nq_ref, cmp_ref, kslc_ref, vslc_ref, kwin_ref, vwin_ref,
                 dq_ref, dk_ref, dv_ref, iq_ref, gab_ref, small_ref):
    x = x_ref[...]
    h = x * lax.rsqrt(jnp.mean(x * x, axis=-1, keepdims=True) + EPS) * gmix_ref[...]
    hb = h.astype(BF16)
    lo = lax.broadcasted_iota(I32, (1, LANES), 1) < HEAD_DIM

    def mm(off, n):
        return _dot(hb, w_ref[:, off:off + n])

    def headnorm(y, g):
        y2 = y * y
        s_lo = jnp.sum(jnp.where(lo, y2, 0.0), axis=-1, keepdims=True)
        s_hi = jnp.sum(jnp.where(lo, 0.0, y2), axis=-1, keepdims=True)
        r = jnp.where(lo, lax.rsqrt(s_lo * (1.0 / HEAD_DIM) + EPS), lax.rsqrt(s_hi * (1.0 / HEAD_DIM) + EPS))
        return y * r * g

    g_qn, g_kn, g_qd, g_kd = (gains_ref[k:k + 1, :] for k in range(4))
    for c in range(NSA_W // LANES):
        nq_ref[:, c * LANES:(c + 1) * LANES] = headnorm(mm(OFF_NQ + c * LANES, LANES), g_qn).astype(BF16)
    cmp_ref[...] = mm(OFF_CMP, 256).astype(BF16)
    kslc_ref[...] = headnorm(mm(OFF_KSLC, LANES), g_kn).astype(BF16)
    vslc_ref[...] = mm(OFF_VSLC, LANES).astype(BF16)
    kwin_ref[...] = headnorm(mm(OFF_KWIN, LANES), g_kn).astype(BF16)
    vwin_ref[...] = mm(OFF_VWIN, LANES).astype(BF16)
    for c in range(DSA_W // LANES):
        dq_ref[:, c * LANES:(c + 1) * LANES] = headnorm(mm(OFF_DQ + c * LANES, LANES), g_qd).astype(BF16)
        dk_ref[:, c * LANES:(c + 1) * LANES] = headnorm(mm(OFF_DK + c * LANES, LANES), g_kd).astype(BF16)
    dv_ref[...] = mm(OFF_DV, DSA_W).astype(BF16)
    iq_ref[...] = (mm(OFF_IQ, IDX_HEADS * IDX_DIM) * (IDX_DIM ** -0.5)).astype(BF16)
    gab_ref[...] = jax.nn.sigmoid(mm(OFF_GAB, 2 * D_MODEL))
    small_ref[...] = mm(OFF_SMALL, LANES)


def _inproj(x2, g_mix, w_perm, gains, tm=512):
    S = x2.shape[0]
    row = lambda n: pl.BlockSpec((tm, n), lambda i: (i, 0))
    full = lambda a: pl.BlockSpec(a.shape, lambda i: (0,) * a.ndim)
    widths = [(NSA_W, BF16), (256, BF16), (LANES, BF16), (LANES, BF16), (LANES, BF16), (LANES, BF16),
              (DSA_W, BF16), (DSA_W, BF16), (DSA_W, BF16), (IDX_HEADS * IDX_DIM, BF16),
              (2 * D_MODEL, F32), (LANES, F32)]
    return pl.pallas_call(
        _inproj_body,
        grid=(S // tm,),
        in_specs=[row(D_MODEL), full(g_mix), full(w_perm), full(gains)],
        out_specs=[row(n) for n, _ in widths],
        out_shape=[jax.ShapeDtypeStruct((S, n), dt) for n, dt in widths],
        compiler_params=_cparams(("arbitrary",)),
        name="inproj",
    )(x2, g_mix, w_perm, gains)


def _compress_body(xa_ref, xb_ref, w1a_ref, w1b_ref, w1f_ref, pe_ref, w2_ref, gk_ref, o_ref):
    is_k = pl.program_id(0) < NSA_KV_GROUPS
    pe_term = jnp.sum(pe_ref[...] * w1f_ref[...], axis=0, keepdims=True)
    pre = _dot(xa_ref[...], w1a_ref[...]) + _dot(xb_ref[...], w1b_ref[...]) + pe_term
    hdn = pre * jax.nn.sigmoid(pre)
    out = _dot(hdn.astype(BF16), w2_ref[...])
    normed = out * lax.rsqrt(jnp.mean(out * out, axis=-1, keepdims=True) + EPS) * gk_ref[...]
    o_ref[...] = jnp.where(is_k, normed, out).astype(BF16)


def _compress(xa, xb, w1a, w1b, w1f, pe, w2, gk):
    n4, ncp, kdim = xa.shape
    G = NSA_KV_GROUPS
    kv = lambda i: i // G
    return pl.pallas_call(
        _compress_body,
        grid=(n4,),
        in_specs=[pl.BlockSpec((None, ncp, kdim), lambda i: (i, 0, 0)),
                  pl.BlockSpec((None, ncp, kdim), lambda i: (i, 0, 0)),
                  pl.BlockSpec((None, kdim, HEAD_DIM), lambda i: (kv(i), 0, 0)),
                  pl.BlockSpec((None, kdim, HEAD_DIM), lambda i: (kv(i), 0, 0)),
                  pl.BlockSpec((None, 2 * kdim, HEAD_DIM), lambda i: (kv(i), 0, 0)),
                  pl.BlockSpec((None, 2 * kdim, 1), lambda i: (kv(i), 0, 0)),
                  pl.BlockSpec((None, HEAD_DIM, HEAD_DIM), lambda i: (kv(i), 0, 0)),
                  pl.BlockSpec((1, HEAD_DIM), lambda i: (0, 0))],
        out_specs=pl.BlockSpec((None, ncp, HEAD_DIM), lambda i: (i, 0, 0)),
        out_shape=jax.ShapeDtypeStruct((n4, ncp, HEAD_DIM), BF16),
        compiler_params=_cparams(("arbitrary",)),
        name="compress",
    )(xa, xb, w1a, w1b, w1f, pe, w2, gk)


def _cmp_body(qt_ref, kc_ref, vct_ref, bt_ref, oct_ref, selt_ref, s_ref, imp_ref, *, NS):
    i = pl.program_id(1)

    @pl.when((pl.program_id(0) == 0) & (i == 0))
    def _():
        imp_ref[...] = jnp.zeros(imp_ref.shape, F32)

    imp_prev = imp_ref[...]
    ib = jnp.minimum(i, pl.num_programs(1) - 2)
    t0 = ib * Q_BLOCK
    T = Q_BLOCK
    NCP = 4 * NS
    R = NSA_GROUP_SIZE
    row = lax.broadcasted_iota(I32, (NCP, 1), 0)
    plane = row // NS
    nblk = row - plane * NS
    cend = NSA_SLC_BLOCK * nblk + NSA_CMP_STRIDE * plane + (NSA_CMP_LEN - 1)
    tq = t0 + lax.broadcasted_iota(I32, (1, T), 1)
    mask = cend <= tq
    kc = kc_ref[...]
    vct = vct_ref[...]
    WIN = 2 * SUBLANES
    n_lo = 2 * ib - 2
    wstart = jnp.clip(n_lo // SUBLANES * SUBLANES, 0, NS - WIN)
    wrow = lax.broadcasted_iota(I32, (WIN, 1), 0)
    psum = jnp.zeros((NCP, T), F32)
    for r in range(R):
        s_ref[r] = _dot(kc, qt_ref[r * HEAD_DIM:(r + 1) * HEAD_DIM, :])
    for r in range(R):
        bt = bt_ref[r]
        for j in range(4):
            add = jnp.zeros((WIN, T), F32)
            for k in range(4):
                n = 2 * ib + 1 - k
                add = add + jnp.where((wrow == n - wstart) & (n >= 0), bt[4 * j + k:4 * j + k + 1, :], 0.0)
            sl = pl.ds(pl.multiple_of(j * NS + wstart, SUBLANES), WIN)
            s_ref[r, sl, :] = s_ref[r, sl, :] + add
        s = jnp.where(mask, s_ref[r], -jnp.inf)
        m = jnp.maximum(jnp.max(s, axis=0, keepdims=True), -1e30)
        e = jnp.exp2(s - m)
        pc = e / jnp.maximum(jnp.sum(e, axis=0, keepdims=True), 1e-30)
        oct_ref[r * HEAD_DIM:(r + 1) * HEAD_DIM, :] = _dot(vct, pc.astype(BF16))
        psum = psum + pc
    p0, p1, p2, p3 = (psum[j * NS:(j + 1) * NS, :] for j in range(4))
    n_ar = lax.broadcasted_iota(I32, (NS, 1), 0)
    p3s = jnp.where(n_ar == 0, 0.0, pltpu.roll(p3, 1, axis=0))
    imp_ref[...] = p0 + p1 + p2 + 0.5 * p3 + 0.5 * p3s

    imp = imp_prev
    tq = (i - 1) * Q_BLOCK + lax.broadcasted_iota(I32, (1, T), 1)
    cur = tq // NSA_SLC_BLOCK
    valid = NSA_SLC_BLOCK * n_ar <= tq
    forced = valid & ((n_ar == 0) | (n_ar > cur - NSA_LOCAL_BLOCKS))
    score = jnp.where(forced, 1e30, jnp.where(valid, imp, -1.0))
    sel = jnp.zeros((NS, T), jnp.bool_)
    for _ in range(min(NSA_SLC_TOPK, NS)):
        mx = jnp.max(score, axis=0, keepdims=True)
        first = jnp.min(jnp.where(score == mx, n_ar, NS), axis=0, keepdims=True)
        pick = n_ar == first
        sel = sel | pick
        score = jnp.where(pick, -2.0, score)
    selb = jnp.where(sel, 1.0, 0.0).astype(BF16)
    if NS < LANES:
        selb = jnp.concatenate([selb, jnp.zeros((LANES - NS, T), BF16)], axis=0)
    for w in range(selt_ref.shape[0]):
        selt_ref[w] = selb[w * LANES:(w + 1) * LANES, :]


def _cmp(nqt, kc, vct, bt, S):
    NS = S // NSA_SLC_BLOCK
    NCP = 4 * NS
    NW = max(NS // LANES, 1)
    G, R = NSA_KV_GROUPS, NSA_GROUP_SIZE
    nb = S // Q_BLOCK
    return pl.pallas_call(
        functools.partial(_cmp_body, NS=NS),
        grid=(G, nb + 1),
        in_specs=[pl.BlockSpec((R * HEAD_DIM, Q_BLOCK), lambda g, i: (g, jnp.minimum(i, nb - 1))),
                  pl.BlockSpec((None, NCP, HEAD_DIM), lambda g, i: (g, 0, 0)),
                  pl.BlockSpec((None, HEAD_DIM, NCP), lambda g, i: (g, 0, 0)),
                  pl.BlockSpec((R, 16, Q_BLOCK), lambda g, i: (g, 0, 0))],
        out_specs=[pl.BlockSpec((R * HEAD_DIM, Q_BLOCK), lambda g, i: (g, jnp.minimum(i, nb - 1))),
                   pl.BlockSpec((None, NW, LANES, Q_BLOCK), lambda g, i: (g, 0, 0, jnp.maximum(i - 1, 0)))],
        out_shape=[jax.ShapeDtypeStruct((NSA_W, S), F32),
                   jax.ShapeDtypeStruct((G, NW, LANES, S), BF16)],
        scratch_shapes=[pltpu.VMEM((R, NCP, Q_BLOCK), F32), pltpu.VMEM((NS, Q_BLOCK), F32)],
        compiler_params=_cparams(("arbitrary", "arbitrary")),
        name="cmp",
    )(nqt, kc, vct, bt)


def _flash_init(m_ref, l_ref, acc_ref):
    m_ref[...] = jnp.full(m_ref.shape, -1e30, F32)
    l_ref[...] = jnp.zeros(l_ref.shape, F32)
    acc_ref[...] = jnp.zeros(acc_ref.shape, F32)


def _flash_update(s, vt, m_ref, l_ref, acc_ref):
    m_old = m_ref[...]
    m_new = jnp.maximum(m_old, jnp.max(s, axis=0, keepdims=True))
    alpha = jnp.exp2(m_old - m_new)
    p = jnp.exp2(s - m_new)
    l_ref[...] = alpha * l_ref[...] + jnp.sum(p, axis=0, keepdims=True)
    acc_ref[...] = alpha * acc_ref[...] + _dot(vt, p.astype(BF16))
    m_ref[...] = m_new


def _slcwin_body(qt_ref, ks_ref, vst_ref, kw_ref, vwt_ref, selt_ref, tbt_ref, oct_ref, glt_ref, ot_ref,
                 ms_ref, ls_ref, as_ref, mw_ref, lw_ref, aw_ref, sbuf_ref, qaug_ref):
    i = pl.program_id(1)
    t0 = i * Q_BLOCK
    R = NSA_GROUP_SIZE
    T = Q_BLOCK
    q4t = jnp.concatenate([qt_ref[r * HEAD_DIM:(r + 1) * HEAD_DIM, :] for r in range(R)], axis=1)
    tq = lax.broadcasted_iota(I32, (1, T), 1)
    rep = lambda mk: jnp.concatenate([mk] * R, axis=1)
    tb4 = jnp.concatenate([tbt_ref[r] for r in range(R)], axis=1)

    def expand(w, base, nkeys):
        keyi = lax.broadcasted_iota(I32, (nkeys, LANES), 0)
        blk = lax.broadcasted_iota(I32, (nkeys, LANES), 1)
        e = jnp.where(blk == base + keyi // NSA_SLC_BLOCK, 1.0, 0.0).astype(BF16)
        return _dot(e, selt_ref[w])

    _flash_init(ms_ref, ls_ref, as_ref)
    _flash_init(mw_ref, lw_ref, aw_ref)
    n_win = selt_ref.shape[0]
    zpad = jnp.zeros((KAUG_W - HEAD_DIM - LANES, R * T), BF16)
    for w in range(n_win):
        qaug_ref[w] = jnp.concatenate([q4t, rep(selt_ref[w] - jnp.asarray(1.0, BF16)), zpad], axis=0)
    far_end = jnp.maximum(t0 - Q_BLOCK, 0) // KEY_TILE * KEY_TILE
    nfar = far_end // KEY_TILE
    tiles_per_window = LANES * NSA_SLC_BLOCK // KEY_TILE

    def score_far(kt, slot):
        r0 = pl.multiple_of(NSA_PADF + kt * KEY_TILE, LANES)
        w = jnp.minimum(kt // tiles_per_window, n_win - 1)
        sbuf_ref[slot] = _dot(ks_ref[pl.ds(r0, KEY_TILE), :], qaug_ref[w])

    def update_far(kt, slot):
        r0 = pl.multiple_of(NSA_PADF + kt * KEY_TILE, LANES)
        _flash_update(sbuf_ref[slot], vst_ref[:, pl.ds(r0, KEY_TILE)], ms_ref, ls_ref, as_ref)

    score_far(0, 0)

    def far2(k2, carry):
        kt = 2 * k2
        score_far(kt + 1, 1)
        update_far(kt, 0)
        score_far(kt + 2, 0)
        update_far(kt + 1, 1)
        return carry

    lax.fori_loop(0, nfar // 2, far2, 0)

    @pl.when(nfar % 2 == 1)
    def _():
        update_far(nfar - 1, 0)

    kq = lambda r0, n: _dot(ks_ref[pl.ds(r0, n), 0:HEAD_DIM], q4t)
    keyn = lax.broadcasted_iota(I32, (NEAR, 1), 0)
    near_ok = (keyn - Q_BLOCK <= tq) & (keyn + t0 - Q_BLOCK >= 0)
    rn = pl.multiple_of(t0 + NSA_PADF - Q_BLOCK, LANES)
    WF = NSA_WINDOW - Q_BLOCK
    rw = pl.multiple_of(t0 + NSA_PADF - NSA_WINDOW, LANES)
    keyw = lax.broadcasted_iota(I32, (WF, 1), 0)
    maskw = (keyw > tq) & (keyw + t0 - NSA_WINDOW >= 0)
    blocks = [jnp.maximum(2 * i - (NSA_WINDOW // NSA_SLC_BLOCK) + 2 * j, 0) for j in range(NSA_WINDOW // Q_BLOCK)]
    mexp = [expand(b // LANES, b % LANES, Q_BLOCK) for b in blocks]
    mask_sm = (jnp.concatenate(mexp[:-1], axis=0) > 0.5) & (keyw + t0 - NSA_WINDOW >= far_end)
    mask_sn = (jnp.concatenate([mexp[-1], expand(2 * i // LANES, 2 * i % LANES, Q_BLOCK)], axis=0) > 0.5) & near_ok
    s_sm = jnp.where(rep(mask_sm), kq(rw, WF), -jnp.inf)
    s_sn = jnp.where(rep(mask_sn), kq(rn, NEAR) + tb4, -jnp.inf)
    s_wf = jnp.where(rep(maskw), _dot(kw_ref[pl.ds(rw, WF), :], q4t), -jnp.inf)
    s_wn = jnp.where(rep(near_ok), _dot(kw_ref[pl.ds(rn, NEAR), :], q4t) + tb4, -jnp.inf)
    _flash_update(s_sm, vst_ref[:, pl.ds(rw, WF)], ms_ref, ls_ref, as_ref)
    _flash_update(s_sn, vst_ref[:, pl.ds(rn, NEAR)], ms_ref, ls_ref, as_ref)
    _flash_update(s_wf, vwt_ref[:, pl.ds(rw, WF)], mw_ref, lw_ref, aw_ref)
    _flash_update(s_wn, vwt_ref[:, pl.ds(rn, NEAR)], mw_ref, lw_ref, aw_ref)
    o_s = as_ref[...] / jnp.maximum(ls_ref[...], 1e-30)
    o_w = aw_ref[...] / jnp.maximum(lw_ref[...], 1e-30)

    gates = jax.nn.sigmoid(glt_ref[...])
    for r in range(R):
        hs = slice(r * HEAD_DIM, (r + 1) * HEAD_DIM)
        qs = slice(r * T, (r + 1) * T)
        o = (gates[3 * r:3 * r + 1, :] * oct_ref[hs, :]
             + gates[3 * r + 1:3 * r + 2, :] * o_s[:, qs]
             + gates[3 * r + 2:3 * r + 3, :] * o_w[:, qs])
        ot_ref[hs, :] = o.astype(BF16)


def _slcwin(nqt, ks, vst, kw, vwt, selt, tbt, oct, glt, S):
    G, R = NSA_KV_GROUPS, NSA_GROUP_SIZE
    spad = ks.shape[1]
    NW = selt.shape[1]
    once = pl.Buffered(1)
    kaspec = pl.BlockSpec((None, spad, KAUG_W), lambda g, i: (g, 0, 0), pipeline_mode=once)
    kspec = pl.BlockSpec((None, spad, HEAD_DIM), lambda g, i: (g, 0, 0), pipeline_mode=once)
    vspec = pl.BlockSpec((None, HEAD_DIM, spad), lambda g, i: (g, 0, 0), pipeline_mode=once)
    qspec = pl.BlockSpec((R * HEAD_DIM, Q_BLOCK), lambda g, i: (g, i))
    st = lambda n: pltpu.VMEM((n, R * Q_BLOCK), F32)
    return pl.pallas_call(
        _slcwin_body,
        grid=(G, S // Q_BLOCK),
        in_specs=[qspec, kaspec, vspec, kspec, vspec,
                  pl.BlockSpec((None, NW, LANES, Q_BLOCK), lambda g, i: (g, 0, 0, i)),
                  pl.BlockSpec((R, NEAR, Q_BLOCK), lambda g, i: (g, 0, 0)),
                  qspec,
                  pl.BlockSpec((None, 16, Q_BLOCK), lambda g, i: (g, 0, i))],
        out_specs=qspec,
        out_shape=jax.ShapeDtypeStruct((NSA_W, S), BF16),
        scratch_shapes=[st(1), st(1), st(HEAD_DIM), st(1), st(1), st(HEAD_DIM),
                        pltpu.VMEM((2, KEY_TILE, R * Q_BLOCK), F32),
                        pltpu.VMEM((NW, KAUG_W, R * Q_BLOCK), BF16)],
        compiler_params=_cparams(("arbitrary", "arbitrary")),
        name="slcwin",
    )(nqt, ks, vst, kw, vwt, selt, tbt, oct, glt)


INT_MIN = -2 ** 31
INT_MAX = 2 ** 31 - 1
MAX_PEEL_ROUNDS = 64
UNTESTED_BITS = 16
NEG_INF_KEY = int(np.array(-np.inf, np.float32).view(np.int32)) ^ 0x7FFFFFFF


def _dsa_body(dqt_ref, dk_ref, dvt_ref, iqt_ref, ikt_ref, iwt_ref, tbt_ref, ltri_ref, ot_ref,
              keys_ref, m_ref, l_ref, acc_ref, sbuf_ref, *, S, KEEP):
    i = pl.program_id(0)
    t0 = i * Q_BLOCK
    T = Q_BLOCK
    H = DSA_HEADS
    ntile = (t0 + Q_BLOCK + KEY_TILE - 1) // KEY_TILE
    tq = t0 + lax.broadcasted_iota(I32, (1, T), 1)
    key512 = lax.broadcasted_iota(I32, (KEY_TILE, 1), 0)
    ktile = lambda kt: keys_ref[pl.ds(pl.multiple_of(Q_BLOCK + kt * KEY_TILE, LANES), KEY_TILE), :]

    keys_ref[0:Q_BLOCK, :] = jnp.full((Q_BLOCK, T), NEG_INF_KEY, I32)
    iqt = iqt_ref[...]
    iwt = iwt_ref[...] * (IDX_HEADS ** -0.5)
    qis = [jnp.concatenate([iqt[(2 * j) * IDX_DIM:(2 * j + 1) * IDX_DIM, :],
                            iqt[(2 * j + 1) * IDX_DIM:(2 * j + 2) * IDX_DIM, :]], axis=1)
           for j in range(IDX_HEADS // 2)]
    wrow = [iwt[h:h + 1, :] for h in range(IDX_HEADS)]

    def idx_dots(kt, slot):
        ik = ikt_ref[:, pl.ds(pl.multiple_of(kt * KEY_TILE, KEY_TILE), KEY_TILE)]
        for j in range(IDX_HEADS // 2):
            sbuf_ref[slot, j] = _dot_tn(ik, qis[j])

    def idx_keys(kt, slot):
        c0 = pl.multiple_of(kt * KEY_TILE, KEY_TILE)
        sc = jnp.zeros((KEY_TILE, T), F32)
        for j in range(IDX_HEADS // 2):
            d = jnp.maximum(sbuf_ref[slot, j], 0.0)
            sc = sc + wrow[2 * j] * d[:, 0:T]
            sc = sc + wrow[2 * j + 1] * d[:, T:2 * T]
        sc = jnp.where(sc == 0.0, 0.0, sc)
        sc = jnp.where(c0 + key512 <= tq, sc, -jnp.inf)
        bits = pltpu.bitcast(sc, I32)
        keys_ref[pl.ds(pl.multiple_of(Q_BLOCK + c0, LANES), KEY_TILE), :] = jnp.where(
            bits < 0, bits ^ 0x7FFFFFFF, bits)

    idx_dots(0, 0)

    def score2(k2, carry):
        kt = 2 * k2
        idx_dots(kt + 1, 1)
        idx_keys(kt, 0)
        idx_dots(kt + 2, 0)
        idx_keys(kt + 1, 1)
        return carry

    lax.fori_loop(0, (ntile + 1) // 2, score2, 0)

    def count(pred):
        def body(kt, acc):
            hit = jnp.where(pred(ktile(kt), kt * KEY_TILE + key512), 1, 0)
            return acc + jnp.sum(hit.reshape(KEY_TILE // SUBLANES, SUBLANES, T), axis=0)
        acc = lax.fori_loop(0, ntile, body, jnp.zeros((SUBLANES, T), I32))
        return jnp.sum(acc, axis=0, keepdims=True)

    def min_at_least(lowest):
        def body(kt, acc):
            kk = ktile(kt)
            v = jnp.where(kk >= lowest, kk, INT_MAX)
            return jnp.minimum(acc, jnp.min(v.reshape(KEY_TILE // SUBLANES, SUBLANES, T), axis=0))
        acc = lax.fori_loop(0, ntile, body, jnp.full((SUBLANES, T), INT_MAX, I32))
        return jnp.min(acc, axis=0, keepdims=True)

    def any_open(st, over):
        _, excess, fin, _ = st
        return jnp.max(jnp.where((fin == 0) & (excess > over), 1, 0)) > 0

    def bisect(b, st):
        thr, excess, fin, need = st
        cand = thr + lax.shift_left(jnp.int32(1), 31 - b)
        n = count(lambda kk, ab: kk >= cand)
        ok = (n >= KEEP) & (fin == 0)
        return jnp.where(ok, cand, thr), jnp.where(ok, n - KEEP, excess), fin, need

    def peel(st):
        thr, excess, fin, need = st
        low = min_at_least(thr)
        n_low = count(lambda kk, ab: kk == low)
        active = (fin == 0) & (excess > 0)
        tied = active & (n_low > excess)
        drop = active & (n_low <= excess)
        return (jnp.where(tied, low, jnp.where(drop, low + 1, thr)), jnp.where(drop, excess - n_low, excess),
                jnp.where(tied, 1, fin), jnp.where(tied, n_low - excess, need))

    zero = jnp.zeros((1, T), I32)
    st = (jnp.full((1, T), INT_MIN, I32), KEY_TILE * ntile - KEEP + zero, zero, zero)
    st = peel(bisect(0, st))
    st = lax.fori_loop(1, UNTESTED_BITS, bisect, st)
    _, st = lax.while_loop(lambda c: (c[0] < 32) & any_open(c[1], 1),
                           lambda c: (c[0] + 1, bisect(c[0], c[1])), (jnp.int32(UNTESTED_BITS), st))
    _, st = lax.while_loop(lambda c: (c[0] < MAX_PEEL_ROUNDS) & any_open(c[1], 0),
                           lambda c: (c[0] + 1, peel(c[1])), (jnp.int32(0), st))
    thr, _, fin, need = st
    tie = (fin == 1) & (thr > NEG_INF_KEY)

    def tie_cut():
        needf = need.astype(F32)

        def body(kt, c):
            before, jmax = c
            is_tie = ktile(kt) == thr
            tied = jnp.where(is_tie, 1.0, 0.0)
            rank = before + _dot(ltri_ref[...], tied.astype(BF16))
            admit = is_tie & (rank < needf)
            jm = jnp.max(jnp.where(admit, kt * KEY_TILE + key512, -1), axis=0, keepdims=True)
            return before + jnp.sum(tied, axis=0, keepdims=True), jnp.maximum(jmax, jm)

        _, jmax = lax.fori_loop(0, ntile, body, (jnp.zeros((1, T), F32), jnp.full((1, T), -1, I32)))
        return jnp.where(tie, jmax + 1, S)

    jcut = lax.cond(jnp.max(jnp.where(tie, 1, 0)) > 0, tie_cut, lambda: jnp.full((1, T), S, I32))

    def chosen(kk, ab):
        return (kk > thr) | ((kk == thr) & (ab < jcut))

    _flash_init(m_ref, l_ref, acc_ref)
    lo = lax.broadcasted_iota(I32, (LANES, 1), 0) < HEAD_DIM
    qpairs = []
    for pr in range(H // 2):
        qp = dqt_ref[pr * LANES:(pr + 1) * LANES, :]
        zero = jnp.zeros_like(qp)
        qpairs.append(jnp.concatenate([jnp.where(lo, qp, zero), jnp.where(lo, zero, qp)], axis=1))

    def scores(r0, nkeys, mask, bias):
        mask2 = jnp.concatenate([mask, mask], axis=1)
        out = []
        for pr in range(H // 2):
            s = _dot(dk_ref[pl.ds(r0, nkeys), pr * LANES:(pr + 1) * LANES], qpairs[pr])
            if bias is not None:
                s = s + jnp.concatenate([bias[2 * pr], bias[2 * pr + 1]], axis=1)
            out.append(jnp.where(mask2, s, -jnp.inf))
        return out

    def update(pr, s, r0, nkeys):
        m_old = m_ref[pr]
        m_new = jnp.maximum(m_old, jnp.max(s, axis=0, keepdims=True))
        alpha = jnp.exp2(m_old - m_new)
        p = jnp.exp2(s - m_new)
        l_ref[pr] = alpha * l_ref[pr] + jnp.sum(p, axis=0, keepdims=True)
        m_ref[pr] = m_new
        vt = dvt_ref[pr * LANES:(pr + 1) * LANES, pl.ds(r0, nkeys)]
        acc_ref[pr] = alpha * acc_ref[pr] + _dot(vt, p.astype(BF16))

    far_end = t0 - Q_BLOCK
    nfar = jnp.maximum(far_end + KEY_TILE - 1, 0) // KEY_TILE
    far_row = lambda kt: pl.multiple_of(DSA_PADF + kt * KEY_TILE, LANES)

    def score_far(kt, slot):
        ab = kt * KEY_TILE + key512
        mask = chosen(ktile(jnp.minimum(kt, ntile - 1)), ab) & (ab < far_end)
        for pr, s in enumerate(scores(far_row(kt), KEY_TILE, mask, None)):
            sbuf_ref[slot, pr] = s

    def update_far(kt, slot):
        for pr in range(H // 2):
            update(pr, sbuf_ref[slot, pr], far_row(kt), KEY_TILE)

    score_far(0, 0)

    def far2(k2, carry):
        kt = 2 * k2
        score_far(kt + 1, 1)
        update_far(kt, 0)
        score_far(kt + 2, 0)
        update_far(kt + 1, 1)
        return carry

    lax.fori_loop(0, (nfar + 1) // 2, far2, 0)

    abn = lax.broadcasted_iota(I32, (NEAR, 1), 0) + t0 - Q_BLOCK
    kkn = keys_ref[pl.ds(pl.multiple_of(t0, LANES), NEAR), :]
    rn = pl.multiple_of(t0 + DSA_PADF - Q_BLOCK, LANES)
    for pr, s in enumerate(scores(rn, NEAR, chosen(kkn, abn) & (abn <= tq) & (abn >= 0), tbt_ref)):
        update(pr, s, rn, NEAR)

    for pr in range(H // 2):
        o = acc_ref[pr] / jnp.maximum(l_ref[pr], 1e-30)
        ot_ref[pr * LANES:(pr + 1) * LANES, :] = jnp.where(lo, o[:, 0:T], o[:, T:2 * T]).astype(BF16)


def _dsa(dqt, dkp, dvtp, iqt, ikt, iwt, tbt, ltri, S):
    KEEP = min(DSA_TOPK_MAX, S // 4)
    H = DSA_HEADS
    T = Q_BLOCK
    once = pl.Buffered(1)
    full = lambda a: pl.BlockSpec(a.shape, lambda i: (0,) * a.ndim, pipeline_mode=once)
    return pl.pallas_call(
        functools.partial(_dsa_body, S=S, KEEP=KEEP),
        grid=(S // T,),
        in_specs=[pl.BlockSpec((DSA_W, T), lambda i: (0, i)),
                  full(dkp), full(dvtp),
                  pl.BlockSpec((IDX_HEADS * IDX_DIM, T), lambda i: (0, i)),
                  full(ikt),
                  pl.BlockSpec((IDX_HEADS, T), lambda i: (0, i)),
                  full(tbt), full(ltri)],
        out_specs=pl.BlockSpec((DSA_W, T), lambda i: (0, i)),
        out_shape=jax.ShapeDtypeStruct((DSA_W, S), BF16),
        scratch_shapes=[pltpu.VMEM((Q_BLOCK + S + KEY_TILE, T), I32),
                        pltpu.VMEM((H // 2, 1, 2 * T), F32),
                        pltpu.VMEM((H // 2, 1, 2 * T), F32),
                        pltpu.VMEM((H // 2, LANES, 2 * T), F32),
                        pltpu.VMEM((2, H // 2, KEY_TILE, 2 * T), F32)],
        compiler_params=_cparams(("arbitrary",)),
        name="dsa",
    )(dqt, dkp, dvtp, iqt, ikt, iwt, tbt, ltri)


def _split_bf16(a):
    hi = a.astype(BF16)
    return hi, (a - hi.astype(F32)).astype(BF16)


def _merge_body(x_ref, oa_ref, ob_ref, gab_ref, wa_ref, wb_ref, wo_ref, gffn_ref, wr_ref, br_ref,
                x1_ref, h2_ref, ridx_ref, rw_ref):
    a = _dot(oa_ref[...], wa_ref[...])
    b = _dot(ob_ref[...], wb_ref[...])
    gab = gab_ref[...]
    merged = gab[:, :D_MODEL] * a + gab[:, D_MODEL:] * b
    x1 = x_ref[...] + _dot(merged.astype(BF16), wo_ref[...])
    x1_ref[...] = x1
    h2 = x1 * lax.rsqrt(jnp.mean(x1 * x1, axis=-1, keepdims=True) + EPS) * gffn_ref[...]
    h2_ref[...] = h2
    hh, hl = _split_bf16(h2)
    wh, wl = _split_bf16(wr_ref[...])
    logits = _dot(hh, wh) + _dot(hh, wl) + _dot(hl, wh) + br_ref[...]
    lane = lax.broadcasted_iota(I32, (1, LANES), 1)
    idx_out = jnp.zeros(logits.shape, I32)
    val_out = jnp.full(logits.shape, -jnp.inf, F32)
    for k in range(TOP_K):
        mx = jnp.max(logits, axis=-1, keepdims=True)
        first = jnp.min(jnp.where(logits == mx, lane, LANES), axis=-1, keepdims=True)
        idx_out = jnp.where(lane == k, first, idx_out)
        val_out = jnp.where(lane == k, mx, val_out)
        logits = jnp.where(lane == first, -jnp.inf, logits)
    e = jnp.exp(val_out - jnp.max(val_out, axis=-1, keepdims=True))
    ridx_ref[...] = idx_out
    rw_ref[...] = e / jnp.sum(e, axis=-1, keepdims=True)


def _merge(x2, oa, ob, gab, wa, wb, wo, gffn, wr, br, tm=512):
    S = x2.shape[0]
    row = lambda n: pl.BlockSpec((tm, n), lambda i: (i, 0))
    full = lambda a: pl.BlockSpec(a.shape, lambda i: (0,) * a.ndim)
    return pl.pallas_call(
        _merge_body,
        grid=(S // tm,),
        in_specs=[row(D_MODEL), row(NSA_W), row(DSA_W), row(2 * D_MODEL),
                  full(wa), full(wb), full(wo), full(gffn), full(wr), full(br)],
        out_specs=[row(D_MODEL), row(D_MODEL), row(LANES), row(LANES)],
        out_shape=[jax.ShapeDtypeStruct((S, D_MODEL), F32), jax.ShapeDtypeStruct((S, D_MODEL), F32),
                   jax.ShapeDtypeStruct((S, LANES), I32), jax.ShapeDtypeStruct((S, LANES), F32)],
        compiler_params=_cparams(("arbitrary",)),
        name="merge",
    )(x2, oa, ob, gab, wa, wb, wo, gffn, wr, br)


def _rank_body(ridx_ref, rank_ref, cnt_ref, carry_ref):
    @pl.when(pl.program_id(0) == 0)
    def _():
        carry_ref[...] = jnp.zeros(carry_ref.shape, F32)

    ridx = ridx_ref[...]
    tm = ridx.shape[0]
    lane = lax.broadcasted_iota(I32, (1, LANES), 1)
    hits = [lane == ridx[:, k:k + 1] for k in range(TOP_K)]
    member = jnp.zeros((tm, LANES), F32)
    for hk in hits:
        member = member + jnp.where(hk, 1.0, 0.0)
    ri = lax.broadcasted_iota(I32, (tm, tm), 0)
    ci = lax.broadcasted_iota(I32, (tm, tm), 1)
    lower = jnp.where(ci < ri, 1.0, 0.0).astype(BF16)
    before = _dot(lower, member.astype(BF16)) + carry_ref[0:1, :]
    out = jnp.zeros((tm, LANES), I32)
    for k, hk in enumerate(hits):
        rk = jnp.sum(jnp.where(hk, before, 0.0), axis=-1, keepdims=True)
        out = jnp.where(lane == k, rk.astype(I32), out)
    rank_ref[...] = out
    total = carry_ref[0:1, :] + jnp.sum(member, axis=0, keepdims=True)
    carry_ref[...] = jnp.broadcast_to(total, carry_ref.shape)
    cnt_ref[...] = jnp.broadcast_to(total, cnt_ref.shape).astype(I32)


def _rank(ridx, tm=512):
    S = ridx.shape[0]
    return pl.pallas_call(
        _rank_body,
        grid=(S // tm,),
        in_specs=[pl.BlockSpec((tm, LANES), lambda i: (i, 0))],
        out_specs=[pl.BlockSpec((tm, LANES), lambda i: (i, 0)), pl.BlockSpec((8, LANES), lambda i: (0, 0))],
        out_shape=[jax.ShapeDtypeStruct((S, LANES), I32), jax.ShapeDtypeStruct((8, LANES), I32)],
        scratch_shapes=[pltpu.VMEM((8, LANES), F32)],
        compiler_params=_cparams(("arbitrary",)),
        name="rank",
    )(ridx)


def _dispatch_body(dest_ref, h_ref, zero_ref, xe_ref, sem):
    del zero_ref
    i = pl.program_id(0)
    tm = h_ref.shape[0]

    def start(r, c):
        for k in range(TOP_K):
            d = dest_ref[(i * tm + r) * TOP_K + k]
            pltpu.make_async_copy(h_ref.at[pl.ds(r, 1)], xe_ref.at[pl.ds(d, 1)],
                                  sem).start(priority=k % 2)
        return c

    lax.fori_loop(0, tm, start, 0, unroll=ROW_DMA_UNROLL)
    for _ in range(TOP_K):
        pltpu.make_async_copy(h_ref, xe_ref.at[pl.ds(0, tm)], sem).wait()


def _dispatch(dest, h2, zeros_p, tm=128):
    S = h2.shape[0]
    return pl.pallas_call(
        _dispatch_body,
        grid_spec=pltpu.PrefetchScalarGridSpec(
            num_scalar_prefetch=1,
            grid=(S // tm,),
            in_specs=[pl.BlockSpec((tm, D_MODEL), lambda i, d: (i, 0)),
                      pl.BlockSpec(memory_space=pl.ANY)],
            out_specs=pl.BlockSpec(memory_space=pl.ANY),
            scratch_shapes=[pltpu.SemaphoreType.DMA(())]),
        out_shape=jax.ShapeDtypeStruct(zeros_p.shape, zeros_p.dtype),
        input_output_aliases={2: 0},
        compiler_params=_cparams(("arbitrary",)),
        name="dispatch",
    )(dest, h2, zeros_p)


def _experts_body(ce_ref, nu_ref, x_ref, w1_ref, b1_ref, w2_ref, b2_ref, y_ref, w1b_ref, w2b_ref):
    c = pl.program_id(0)

    @pl.when((c == 0) | (ce_ref[c] != ce_ref[jnp.maximum(c - 1, 0)]))
    def _():
        w1b_ref[...] = w1_ref[...].astype(BF16)
        w2b_ref[...] = w2_ref[...].astype(BF16)

    @pl.when(c < nu_ref[0])
    def _():
        gu = _dot(x_ref[...].astype(BF16), w1b_ref[...]) + b1_ref[...]
        gate = jnp.minimum(gu[:, :D_FF], SWIGLU_LIMIT)
        lin = jnp.clip(gu[:, D_FF:], -SWIGLU_LIMIT, SWIGLU_LIMIT)
        act = (lin + 1.0) * gate * jax.nn.sigmoid(SWIGLU_ALPHA * gate)
        y_ref[...] = _dot(act.astype(BF16), w2b_ref[...]) + b2_ref[...]

    @pl.when(c >= nu_ref[0])
    def _():
        y_ref[...] = jnp.zeros(y_ref.shape, F32)


def _experts(chunk_e, n_used, xe, w1, b1, w2, b2):
    P = xe.shape[0]
    nch = P // MOE_ROWS
    return pl.pallas_call(
        _experts_body,
        grid_spec=pltpu.PrefetchScalarGridSpec(
            num_scalar_prefetch=2,
            grid=(nch,),
            in_specs=[pl.BlockSpec((MOE_ROWS, D_MODEL), lambda c, ce, nu: (c, 0)),
                      pl.BlockSpec((None, D_MODEL, 2 * D_FF), lambda c, ce, nu: (ce[c], 0, 0)),
                      pl.BlockSpec((None, 1, 2 * D_FF), lambda c, ce, nu: (ce[c], 0, 0)),
                      pl.BlockSpec((None, D_FF, D_MODEL), lambda c, ce, nu: (ce[c], 0, 0)),
                      pl.BlockSpec((None, 1, D_MODEL), lambda c, ce, nu: (ce[c], 0, 0))],
            out_specs=pl.BlockSpec((MOE_ROWS, D_MODEL), lambda c, ce, nu: (c, 0)),
            scratch_shapes=[pltpu.VMEM((D_MODEL, 2 * D_FF), BF16), pltpu.VMEM((D_FF, D_MODEL), BF16)]),
        out_shape=jax.ShapeDtypeStruct((P, D_MODEL), F32),
        compiler_params=_cparams(("arbitrary",)),
        name="experts",
    )(chunk_e, n_used, xe, w1, b1, w2, b2)


def _final_body(dest_ref, x1_ref, rw_ref, p_ref, gple_ref, wg_ref, wp_ref, y_ref, o_ref, rows_ref, sems):
    i = pl.program_id(0)
    tm = x1_ref.shape[0]

    def gather(blk, slot):
        def start(r, c):
            for k in range(TOP_K):
                d = dest_ref[(blk * tm + r) * TOP_K + k]
                pltpu.make_async_copy(y_ref.at[pl.ds(d, 1)], rows_ref.at[slot, k, pl.ds(r, 1)],
                                      sems.at[slot]).start(priority=k % 2)
            return c
        lax.fori_loop(0, tm, start, 0, unroll=ROW_DMA_UNROLL)

    @pl.when(i == 0)
    def _():
        gather(0, 0)

    @pl.when(i + 1 < pl.num_programs(0))
    def _():
        gather(i + 1, (i + 1) % 2)

    slot = i % 2
    for k in range(TOP_K):
        pltpu.make_async_copy(y_ref.at[pl.ds(0, tm)], rows_ref.at[slot, k], sems.at[slot]).wait()
    rw = rw_ref[...]
    x2 = x1_ref[...]
    for k in range(TOP_K):
        x2 = x2 + rw[:, k:k + 1] * rows_ref[slot, k]
    hn = x2 * lax.rsqrt(jnp.mean(x2 * x2, axis=-1, keepdims=True) + EPS) * gple_ref[...]
    gate = jax.nn.sigmoid(_dot(hn.astype(BF16), wg_ref[...]))
    o_ref[...] = x2 + gate * _dot(p_ref[...].astype(BF16), wp_ref[...])


def _final(dest, x1, rw, p2, gple, wg, wp, yexp, tm=128):
    S = x1.shape[0]
    row = lambda n: pl.BlockSpec((tm, n), lambda i, d: (i, 0))
    full = lambda a: pl.BlockSpec(a.shape, lambda i, d: (0,) * a.ndim)
    return pl.pallas_call(
        _final_body,
        grid_spec=pltpu.PrefetchScalarGridSpec(
            num_scalar_prefetch=1,
            grid=(S // tm,),
            in_specs=[row(D_MODEL), row(LANES), row(PLE_DIM), full(gple), full(wg), full(wp),
                      pl.BlockSpec(memory_space=pl.ANY)],
            out_specs=row(D_MODEL),
            scratch_shapes=[pltpu.VMEM((2, TOP_K, tm, D_MODEL), F32), pltpu.SemaphoreType.DMA((2,))]),
        out_shape=jax.ShapeDtypeStruct((S, D_MODEL), F32),
        compiler_params=_cparams(("arbitrary",)),
        name="final",
    )(dest, x1, rw, p2, gple, wg, wp, yexp)


def _distance_bias(rel_cols):
    return (rel_cols[_rel_bucket_table(NEAR)] - rel_cols[REL_BUCKETS - 1][None, :]).T * LOG2E


def _near_bias_t(rel_cols):
    h = rel_cols.shape[1]
    g = jnp.pad(_distance_bias(rel_cols), ((0, 0), (Q_BLOCK, Q_BLOCK)))
    period = 2 * NEAR
    skew = jnp.tile(g, (1, NEAR))[:, :NEAR * (period - 1)].reshape(h, NEAR, period - 1)
    return skew[:, :, NEAR:NEAR + Q_BLOCK]


def _cmp_bias_t(rel_cols):
    g = jnp.pad(_distance_bias(rel_cols), ((0, 0), (NEAR, Q_BLOCK)))
    rows = []
    for j in range(4):
        for k in range(4):
            c0 = -NSA_CMP_STRIDE * j - (NSA_CMP_LEN - 1) + NSA_SLC_BLOCK * (k - 1)
            rows.append(g[:, c0 + NEAR:c0 + NEAR + Q_BLOCK])
    return jnp.stack(rows, axis=1)


def kernel(x, p, w_in, g_mix, g_q_nsa, g_k_nsa, g_q_dsa, g_k_dsa, pe_ck, w_ck1, w_ck2, pe_cv, w_cv1, w_cv2,
           w_branch_a, w_branch_b, w_out, rel_bias, g_ffn, w_router, b_router, w_e1, b_e1, w_e2, b_e2,
           g_ple, w_ple, w_ple_gate):
    B, S, D = x.shape
    assert B == 1 and D == D_MODEL and S % KEY_TILE == 0 and w_in.shape[0] == 1
    G, R = NSA_KV_GROUPS, NSA_GROUP_SIZE
    x2 = x.reshape(S, D)

    w0 = w_in[0]
    w_perm = jnp.concatenate([w0[:, 0:1280], w0[:, 1304:3096], w0[:, 3136:5184], w0[:, 1280:1304],
                              w0[:, 3096:3136], jnp.zeros((D, D_IN_PAD - w0.shape[1]), F32)], axis=1).astype(BF16)
    two = lambda g: jnp.tile(g.reshape(1, HEAD_DIM), (1, 2))
    qscale = HEAD_DIM ** -0.5 * LOG2E
    gains = jnp.concatenate([two(g_q_nsa[0]) * qscale, two(g_k_nsa[0]),
                             two(g_q_dsa[0]) * qscale, two(g_k_dsa[0])], axis=0)
    (nq, cmpraw, kslc, vslc, kwin, vwin, dq, dk, dv, iq, gab, small) = _inproj(
        x2, g_mix[0].reshape(1, D), w_perm, gains)

    NS = S // NSA_SLC_BLOCK
    nchunk = S // NSA_CMP_STRIDE
    xc = cmpraw.reshape(nchunk, NSA_CMP_STRIDE, 2 * G, HEAD_DIM).transpose(2, 0, 1, 3).reshape(
        2 * G, nchunk, NSA_CMP_STRIDE * HEAD_DIM)
    planes = lambda a: a.reshape(2 * G, NS, 4, -1).transpose(0, 2, 1, 3).reshape(2 * G, nchunk, -1)
    xa = planes(xc)
    xb = planes(jnp.roll(xc, -1, axis=1))
    half = NSA_CMP_STRIDE * HEAD_DIM
    w1 = jnp.stack([w_ck1[0], w_cv1[0]]).reshape(2, 2 * half, HEAD_DIM)
    pe = jnp.stack([pe_ck[0], pe_cv[0]]).reshape(2, 2 * half, 1)
    w2 = jnp.stack([w_ck2[0], w_cv2[0]]).astype(BF16)
    kvc = _compress(xa, xb, w1[:, :half].astype(BF16), w1[:, half:].astype(BF16), w1, pe, w2,
                    g_k_nsa[0].reshape(1, HEAD_DIM))

    rel_nsa = rel_bias[:, :NSA_HEADS]
    rel_dsa = rel_bias[:, NSA_HEADS:]
    nqt = nq.T
    oct, selt = _cmp(nqt, kvc[:G], kvc[G:].transpose(0, 2, 1), _cmp_bias_t(rel_nsa), S)
    kgroups = lambda a: jnp.pad(a.reshape(S, G, HEAD_DIM).transpose(1, 0, 2),
                                ((0, 0), (NSA_PADF, TAIL_PAD), (0, 0)))
    vgroups = lambda a: jnp.pad(a.reshape(S, G, HEAD_DIM).transpose(1, 2, 0),
                                ((0, 0), (0, 0), (NSA_PADF, TAIL_PAD)))
    glt = small[:, SMALL_GATE:SMALL_GATE + 3 * NSA_HEADS].reshape(S, G, 3 * R).transpose(1, 2, 0)
    glt = jnp.pad(glt, ((0, 0), (0, 16 - 3 * R), (0, 0)))
    blk_lane = (np.arange(S) // NSA_SLC_BLOCK) % LANES
    onehot = jnp.asarray(np.eye(LANES, dtype=np.float32)[blk_lane] * MASK_BIG, BF16)
    kaug = jnp.concatenate([kslc.reshape(S, G, HEAD_DIM).transpose(1, 0, 2),
                            jnp.broadcast_to(onehot, (G, S, LANES)),
                            jnp.zeros((G, S, KAUG_W - HEAD_DIM - LANES), BF16)], axis=-1)
    kaug = jnp.pad(kaug, ((0, 0), (NSA_PADF, TAIL_PAD), (0, 0)))
    o_at = _slcwin(nqt, kaug, vgroups(vslc), kgroups(kwin), vgroups(vwin), selt,
                   _near_bias_t(rel_nsa), oct, glt, S)

    dkp = jnp.pad(dk, ((DSA_PADF, TAIL_PAD), (0, 0)))
    dvtp = jnp.pad(dv.T, ((0, 0), (DSA_PADF, TAIL_PAD)))
    ikt = jnp.pad(small[:, SMALL_IK:SMALL_IK + IDX_DIM].T.astype(BF16), ((0, 0), (0, 2 * KEY_TILE)))
    iwt = small[:, SMALL_IW:SMALL_IW + IDX_HEADS].T
    ltri = jnp.asarray(np.tril(np.ones((KEY_TILE, KEY_TILE), np.float32), -1), BF16)
    o_bt = _dsa(dq.T, dkp, dvtp, iq.T, ikt, iwt, _near_bias_t(rel_dsa), ltri, S)

    wr = jnp.pad(w_router[0], ((0, 0), (0, LANES - N_EXPERTS)))
    br = jnp.pad(b_router[0].reshape(1, N_EXPERTS), ((0, 0), (0, LANES - N_EXPERTS)), constant_values=-1e30)
    x1, h2, ridx, rw = _merge(x2, o_at.T, o_bt.T, gab, w_branch_a[0].astype(BF16), w_branch_b[0].astype(BF16),
                              w_out[0].astype(BF16), g_ffn[0].reshape(1, D), wr, br)

    rank, cnt = _rank(ridx)
    counts = cnt[0, :N_EXPERTS]
    padded = (counts + MOE_ROWS - 1) // MOE_ROWS * MOE_ROWS
    pad_end = jnp.cumsum(padded)
    pad_start = pad_end - padded
    eidx = ridx[:, :TOP_K]
    dest = (jnp.sum(jnp.where(eidx[..., None] == jnp.arange(N_EXPERTS), pad_start, 0), axis=-1)
            + rank[:, :TOP_K]).reshape(-1).astype(I32)
    n_chunks = -(-(S * TOP_K) // MOE_ROWS) + N_EXPERTS
    chunk_start = jnp.arange(n_chunks, dtype=I32) * MOE_ROWS
    chunk_e = jnp.minimum(jnp.sum(chunk_start[:, None] >= pad_end[None, :], axis=-1), N_EXPERTS - 1).astype(I32)
    n_used = (pad_end[-1] // MOE_ROWS).astype(I32).reshape(1)
    xe = _dispatch(dest, h2, jnp.zeros((n_chunks * MOE_ROWS, D), F32))
    yexp = _experts(chunk_e, n_used, xe, w_e1[0], b_e1[0].reshape(N_EXPERTS, 1, 2 * D_FF),
                    w_e2[0], b_e2[0].reshape(N_EXPERTS, 1, D))

    out = _final(dest, x1, rw, p[0].reshape(S, PLE_DIM), g_ple[0].reshape(1, D),
                 w_ple_gate[0].astype(BF16), w_ple[0].astype(BF16), yexp)
    return out.reshape(B, S, D)
```

```python
import functools
import math

import numpy as np
import jax
import jax.numpy as jnp
from jax import lax
from jax.experimental import pallas as pl
from jax.experimental.pallas import tpu as pltpu

F32 = jnp.float32
BF16 = jnp.bfloat16
I32 = jnp.int32

D_MODEL = 1024
PLE_DIM = 256
HEAD_DIM = 64
NSA_HEADS = 8
NSA_KV_GROUPS = 2
NSA_GROUP_SIZE = NSA_HEADS // NSA_KV_GROUPS
NSA_CMP_LEN = 32
NSA_CMP_STRIDE = 16
NSA_SLC_BLOCK = 64
NSA_SLC_TOPK = 16
NSA_LOCAL_BLOCKS = 2
NSA_WINDOW = 512
DSA_HEADS = 8
IDX_HEADS = 8
IDX_DIM = 32
DSA_TOPK_MAX = 256
Q_BLOCK = 128
REL_BUCKETS = 32
REL_MAX_EXACT = 16
REL_MAX_DIST = 128
N_EXPERTS = 32
TOP_K = 4
D_FF = 1024
SWIGLU_LIMIT = 7.0
SWIGLU_ALPHA = 1.702
MOE_ROWS = 512
EPS = 1e-6
LOG2E = math.log2(math.e)

NSA_W = NSA_HEADS * HEAD_DIM
DSA_W = DSA_HEADS * HEAD_DIM

LANES = 128
SUBLANES = 8
KEY_TILE = 512
NEAR = 2 * Q_BLOCK
NSA_PADF = NSA_WINDOW
DSA_PADF = Q_BLOCK
KAUG_W = 2 * LANES
MASK_BIG = 2.0 ** 100
TAIL_PAD = 3 * KEY_TILE
VMEM_LIMIT = 56 * 1024 * 1024
ROW_DMA_UNROLL = 8

NAT_CMP, NAT_KSLC, NAT_KWIN, NAT_DK, NAT_GAB, NAT_SMALL = 0, 256, 384, 512, 1024, 3072
TR_NQ, TR_DQ, TR_DV, TR_IQ, TR_VSLC, TR_VWIN = 0, 512, 1024, 1536, 1792, 1920
SMALL_GATE, SMALL_IK, SMALL_IW = 0, 24, 56


def _rel_bucket_table(n):
    d = np.arange(n)
    nf = np.maximum(d, 1).astype(np.float64)
    large = REL_MAX_EXACT + (np.log(nf / REL_MAX_EXACT) / math.log(REL_MAX_DIST / REL_MAX_EXACT)
                             * (REL_BUCKETS - REL_MAX_EXACT)).astype(np.int64)
    large = np.minimum(large, REL_BUCKETS - 1)
    return np.where(d < REL_MAX_EXACT, d, large)


def _cparams(sem, vmem=VMEM_LIMIT):
    return pltpu.CompilerParams(dimension_semantics=sem, vmem_limit_bytes=vmem)


def _dot(a, b):
    return jnp.dot(a, b, preferred_element_type=F32)


def _dot_tn(a, b):
    return lax.dot_general(a, b, (((0,), (0,)), ((), ())), preferred_element_type=F32)


def _inproj_body(x_ref, gmix_ref, w_ref, wt_ref, gains_ref, gcol_ref,
                 cmp_ref, kslc_ref, kwin_ref, dk_ref, gab_ref, small_ref,
                 nqt_ref, dqt_ref, dvt_ref, iqt_ref, vst_ref, vwt_ref):
    x = x_ref[...]
    h = x * lax.rsqrt(jnp.mean(x * x, axis=-1, keepdims=True) + EPS) * gmix_ref[...]
    hb = h.astype(BF16)
    tm = hb.shape[0]
    lo = lax.broadcasted_iota(I32, (1, LANES), 1) < HEAD_DIM

    def mm(off, n):
        return _dot(hb, w_ref[:, off:off + n])

    def headnorm(y, g):
        y2 = y * y
        s_lo = jnp.sum(jnp.where(lo, y2, 0.0), axis=-1, keepdims=True)
        s_hi = jnp.sum(jnp.where(lo, 0.0, y2), axis=-1, keepdims=True)
        r = jnp.where(lo, lax.rsqrt(s_lo * (1.0 / HEAD_DIM) + EPS), lax.rsqrt(s_hi * (1.0 / HEAD_DIM) + EPS))
        return y * r * g

    def headnorm_t(y, g):
        y3 = y.reshape(y.shape[0] // HEAD_DIM, HEAD_DIM, tm)
        r = lax.rsqrt(jnp.mean(y3 * y3, axis=1, keepdims=True) + EPS)
        return (y3 * r * g).reshape(y.shape)

    g_kn, g_kd = gains_ref[0:1, :], gains_ref[1:2, :]
    cmp_ref[...] = mm(NAT_CMP, 256).astype(BF16)
    kslc_ref[...] = headnorm(mm(NAT_KSLC, LANES), g_kn).astype(BF16)
    kwin_ref[...] = headnorm(mm(NAT_KWIN, LANES), g_kn).astype(BF16)
    for c in range(DSA_W // LANES):
        dk_ref[:, c * LANES:(c + 1) * LANES] = headnorm(mm(NAT_DK + c * LANES, LANES), g_kd).astype(BF16)
    gab_ref[...] = jax.nn.sigmoid(mm(NAT_GAB, 2 * D_MODEL))
    small_ref[...] = mm(NAT_SMALL, LANES)

    def mm_t(off, n):
        return lax.dot_general(wt_ref[off:off + n, :], hb, (((1,), (1,)), ((), ())), preferred_element_type=F32)

    nqt_ref[...] = headnorm_t(mm_t(TR_NQ, NSA_W), gcol_ref[0]).astype(BF16)
    dqt_ref[...] = headnorm_t(mm_t(TR_DQ, DSA_W), gcol_ref[1]).astype(BF16)
    dvt_ref[...] = mm_t(TR_DV, DSA_W).astype(BF16)
    iqt_ref[...] = (mm_t(TR_IQ, IDX_HEADS * IDX_DIM) * (IDX_DIM ** -0.5)).astype(BF16)
    vst_ref[...] = mm_t(TR_VSLC, LANES).astype(BF16)
    vwt_ref[...] = mm_t(TR_VWIN, LANES).astype(BF16)


def _inproj(x2, g_mix, w_nat, w_tr, gains, gcol, tm=512):
    S = x2.shape[0]
    row = lambda n: pl.BlockSpec((tm, n), lambda i: (i, 0))
    col = lambda n: pl.BlockSpec((n, tm), lambda i: (0, i))
    full = lambda a: pl.BlockSpec(a.shape, lambda i: (0,) * a.ndim)
    nat = [(256, BF16), (LANES, BF16), (LANES, BF16), (DSA_W, BF16), (2 * D_MODEL, F32), (LANES, F32)]
    tr = [NSA_W, DSA_W, DSA_W, IDX_HEADS * IDX_DIM, LANES, LANES]
    return pl.pallas_call(
        _inproj_body,
        grid=(S // tm,),
        in_specs=[row(D_MODEL), full(g_mix), full(w_nat), full(w_tr), full(gains), full(gcol)],
        out_specs=[row(n) for n, _ in nat] + [col(n) for n in tr],
        out_shape=[jax.ShapeDtypeStruct((S, n), dt) for n, dt in nat]
                  + [jax.ShapeDtypeStruct((n, S), BF16) for n in tr],
        compiler_params=_cparams(("arbitrary",)),
        name="inproj",
    )(x2, g_mix, w_nat, w_tr, gains, gcol)


def _compress_body(xa_ref, xb_ref, w1a_ref, w1b_ref, w1f_ref, pe_ref, w2_ref, gk_ref, o_ref):
    is_k = pl.program_id(0) < NSA_KV_GROUPS
    pe_term = jnp.sum(pe_ref[...] * w1f_ref[...], axis=0, keepdims=True)
    pre = _dot(xa_ref[...], w1a_ref[...]) + _dot(xb_ref[...], w1b_ref[...]) + pe_term
    hdn = pre * jax.nn.sigmoid(pre)
    out = _dot(hdn.astype(BF16), w2_ref[...])
    normed = out * lax.rsqrt(jnp.mean(out * out, axis=-1, keepdims=True) + EPS) * gk_ref[...]
    o_ref[...] = jnp.where(is_k, normed, out).astype(BF16)


def _compress(xa, xb, w1a, w1b, w1f, pe, w2, gk):
    n4, ncp, kdim = xa.shape
    G = NSA_KV_GROUPS
    kv = lambda i: i // G
    return pl.pallas_call(
        _compress_body,
        grid=(n4,),
        in_specs=[pl.BlockSpec((None, ncp, kdim), lambda i: (i, 0, 0)),
                  pl.BlockSpec((None, ncp, kdim), lambda i: (i, 0, 0)),
                  pl.BlockSpec((None, kdim, HEAD_DIM), lambda i: (kv(i), 0, 0)),
                  pl.BlockSpec((None, kdim, HEAD_DIM), lambda i: (kv(i), 0, 0)),
                  pl.BlockSpec((None, 2 * kdim, HEAD_DIM), lambda i: (kv(i), 0, 0)),
                  pl.BlockSpec((None, 2 * kdim, 1), lambda i: (kv(i), 0, 0)),
                  pl.BlockSpec((None, HEAD_DIM, HEAD_DIM), lambda i: (kv(i), 0, 0)),
                  pl.BlockSpec((1, HEAD_DIM), lambda i: (0, 0))],
        out_specs=pl.BlockSpec((None, ncp, HEAD_DIM), lambda i: (i, 0, 0)),
        out_shape=jax.ShapeDtypeStruct((n4, ncp, HEAD_DIM), BF16),
        compiler_params=_cparams(("arbitrary",)),
        name="compress",
    )(xa, xb, w1a, w1b, w1f, pe, w2, gk)


def _cmp_body(qt_ref, kc_ref, vct_ref, bt_ref, oct_ref, selt_ref, s_ref, imp_ref, *, NS):
    i = pl.program_id(1)

    @pl.when((pl.program_id(0) == 0) & (i == 0))
    def _():
        imp_ref[...] = jnp.zeros(imp_ref.shape, F32)

    imp_prev = imp_ref[...]
    ib = jnp.minimum(i, pl.num_programs(1) - 2)
    t0 = ib * Q_BLOCK
    T = Q_BLOCK
    NCP = 4 * NS
    R = NSA_GROUP_SIZE
    row = lax.broadcasted_iota(I32, (NCP, 1), 0)
    plane = row // NS
    nblk = row - plane * NS
    cend = NSA_SLC_BLOCK * nblk + NSA_CMP_STRIDE * plane + (NSA_CMP_LEN - 1)
    tq = t0 + lax.broadcasted_iota(I32, (1, T), 1)
    mask = cend <= tq
    kc = kc_ref[...]
    vct = vct_ref[...]
    WIN = 2 * SUBLANES
    n_lo = 2 * ib - 2
    wstart = jnp.clip(n_lo // SUBLANES * SUBLANES, 0, NS - WIN)
    wrow = lax.broadcasted_iota(I32, (WIN, 1), 0)
    psum = jnp.zeros((NCP, T), F32)
    for r in range(R):
        s_ref[r] = _dot(kc, qt_ref[r * HEAD_DIM:(r + 1) * HEAD_DIM, :])
    for r in range(R):
        bt = bt_ref[r]
        for j in range(4):
            add = jnp.zeros((WIN, T), F32)
            for k in range(4):
                n = 2 * ib + 1 - k
                add = add + jnp.where((wrow == n - wstart) & (n >= 0), bt[4 * j + k:4 * j + k + 1, :], 0.0)
            sl = pl.ds(pl.multiple_of(j * NS + wstart, SUBLANES), WIN)
            s_ref[r, sl, :] = s_ref[r, sl, :] + add
        s = jnp.where(mask, s_ref[r], -jnp.inf)
        m = jnp.maximum(jnp.max(s, axis=0, keepdims=True), -1e30)
        e = jnp.exp2(s - m)
        pc = e / jnp.maximum(jnp.sum(e, axis=0, keepdims=True), 1e-30)
        oct_ref[r * HEAD_DIM:(r + 1) * HEAD_DIM, :] = _dot(vct, pc.astype(BF16))
        psum = psum + pc
    p0, p1, p2, p3 = (psum[j * NS:(j + 1) * NS, :] for j in range(4))
    n_ar = lax.broadcasted_iota(I32, (NS, 1), 0)
    p3s = jnp.where(n_ar == 0, 0.0, pltpu.roll(p3, 1, axis=0))
    imp_ref[...] = p0 + p1 + p2 + 0.5 * p3 + 0.5 * p3s

    imp = imp_prev
    tq = (i - 1) * Q_BLOCK + lax.broadcasted_iota(I32, (1, T), 1)
    cur = tq // NSA_SLC_BLOCK
    valid = NSA_SLC_BLOCK * n_ar <= tq
    forced = valid & ((n_ar == 0) | (n_ar > cur - NSA_LOCAL_BLOCKS))
    score = jnp.where(forced, 1e30, jnp.where(valid, imp, -1.0))
    sel = jnp.zeros((NS, T), jnp.bool_)
    for _ in range(min(NSA_SLC_TOPK, NS)):
        mx = jnp.max(score, axis=0, keepdims=True)
        first = jnp.min(jnp.where(score == mx, n_ar, NS), axis=0, keepdims=True)
        pick = n_ar == first
        sel = sel | pick
        score = jnp.where(pick, -2.0, score)
    selb = jnp.where(sel, 1.0, 0.0).astype(BF16)
    if NS < LANES:
        selb = jnp.concatenate([selb, jnp.zeros((LANES - NS, T), BF16)], axis=0)
    for w in range(selt_ref.shape[0]):
        selt_ref[w] = selb[w * LANES:(w + 1) * LANES, :]


def _cmp(nqt, kc, vct, bt, S):
    NS = S // NSA_SLC_BLOCK
    NCP = 4 * NS
    NW = max(NS // LANES, 1)
    G, R = NSA_KV_GROUPS, NSA_GROUP_SIZE
    nb = S // Q_BLOCK
    return pl.pallas_call(
        functools.partial(_cmp_body, NS=NS),
        grid=(G, nb + 1),
        in_specs=[pl.BlockSpec((R * HEAD_DIM, Q_BLOCK), lambda g, i: (g, jnp.minimum(i, nb - 1))),
                  pl.BlockSpec((None, NCP, HEAD_DIM), lambda g, i: (g, 0, 0)),
                  pl.BlockSpec((None, HEAD_DIM, NCP), lambda g, i: (g, 0, 0)),
                  pl.BlockSpec((R, 16, Q_BLOCK), lambda g, i: (g, 0, 0))],
        out_specs=[pl.BlockSpec((R * HEAD_DIM, Q_BLOCK), lambda g, i: (g, jnp.minimum(i, nb - 1))),
                   pl.BlockSpec((None, NW, LANES, Q_BLOCK), lambda g, i: (g, 0, 0, jnp.maximum(i - 1, 0)))],
        out_shape=[jax.ShapeDtypeStruct((NSA_W, S), F32),
                   jax.ShapeDtypeStruct((G, NW, LANES, S), BF16)],
        scratch_shapes=[pltpu.VMEM((R, NCP, Q_BLOCK), F32), pltpu.VMEM((NS, Q_BLOCK), F32)],
        compiler_params=_cparams(("arbitrary", "arbitrary")),
        name="cmp",
    )(nqt, kc, vct, bt)


def _flash_init(m_ref, l_ref, acc_ref):
    m_ref[...] = jnp.full(m_ref.shape, -1e30, F32)
    l_ref[...] = jnp.zeros(l_ref.shape, F32)
    acc_ref[...] = jnp.zeros(acc_ref.shape, F32)


def _flash_update(s, vt, m_ref, l_ref, acc_ref):
    m_old = m_ref[...]
    m_new = jnp.maximum(m_old, jnp.max(s, axis=0, keepdims=True))
    alpha = jnp.exp2(m_old - m_new)
    p = jnp.exp2(s - m_new)
    l_ref[...] = alpha * l_ref[...] + jnp.sum(p, axis=0, keepdims=True)
    acc_ref[...] = alpha * acc_ref[...] + _dot(vt, p.astype(BF16))
    m_ref[...] = m_new


def _slcwin_body(qt_ref, ks_ref, vst_ref, kw_ref, vwt_ref, selt_ref, tbt_ref, oct_ref, glt_ref, ot_ref,
                 ms_ref, ls_ref, as_ref, mw_ref, lw_ref, aw_ref, sbuf_ref, qaug_ref):
    i = pl.program_id(1)
    t0 = i * Q_BLOCK
    R = NSA_GROUP_SIZE
    T = Q_BLOCK
    q4t = jnp.concatenate([qt_ref[r * HEAD_DIM:(r + 1) * HEAD_DIM, :] for r in range(R)], axis=1)
    tq = lax.broadcasted_iota(I32, (1, T), 1)
    rep = lambda mk: jnp.concatenate([mk] * R, axis=1)
    tb4 = jnp.concatenate([tbt_ref[r] for r in range(R)], axis=1)

    def expand(w, base, nkeys):
        keyi = lax.broadcasted_iota(I32, (nkeys, LANES), 0)
        blk = lax.broadcasted_iota(I32, (nkeys, LANES), 1)
        e = jnp.where(blk == base + keyi // NSA_SLC_BLOCK, 1.0, 0.0).astype(BF16)
        return _dot(e, selt_ref[w])

    _flash_init(ms_ref, ls_ref, as_ref)
    _flash_init(mw_ref, lw_ref, aw_ref)
    n_win = selt_ref.shape[0]
    zpad = jnp.zeros((KAUG_W - HEAD_DIM - LANES, R * T), BF16)
    for w in range(n_win):
        qaug_ref[w] = jnp.concatenate([q4t, rep(selt_ref[w] - jnp.asarray(1.0, BF16)), zpad], axis=0)
    far_end = jnp.maximum(t0 - Q_BLOCK, 0) // KEY_TILE * KEY_TILE
    nfar = far_end // KEY_TILE
    tiles_per_window = LANES * NSA_SLC_BLOCK // KEY_TILE

    def score_far(kt, slot):
        r0 = pl.multiple_of(NSA_PADF + kt * KEY_TILE, LANES)
        w = jnp.minimum(kt // tiles_per_window, n_win - 1)
        sbuf_ref[slot] = _dot(ks_ref[pl.ds(r0, KEY_TILE), :], qaug_ref[w])

    def update_far(kt, slot):
        r0 = pl.multiple_of(NSA_PADF + kt * KEY_TILE, LANES)
        _flash_update(sbuf_ref[slot], vst_ref[:, pl.ds(r0, KEY_TILE)], ms_ref, ls_ref, as_ref)

    score_far(0, 0)

    def far2(k2, carry):
        kt = 2 * k2
        score_far(kt + 1, 1)
        update_far(kt, 0)
        score_far(kt + 2, 0)
        update_far(kt + 1, 1)
        return carry

    lax.fori_loop(0, nfar // 2, far2, 0)

    @pl.when(nfar % 2 == 1)
    def _():
        update_far(nfar - 1, 0)

    kq = lambda r0, n: _dot(ks_ref[pl.ds(r0, n), 0:HEAD_DIM], q4t)
    keyn = lax.broadcasted_iota(I32, (NEAR, 1), 0)
    near_ok = (keyn - Q_BLOCK <= tq) & (keyn + t0 - Q_BLOCK >= 0)
    rn = pl.multiple_of(t0 + NSA_PADF - Q_BLOCK, LANES)
    WF = NSA_WINDOW - Q_BLOCK
    rw = pl.multiple_of(t0 + NSA_PADF - NSA_WINDOW, LANES)
    keyw = lax.broadcasted_iota(I32, (WF, 1), 0)
    maskw = (keyw > tq) & (keyw + t0 - NSA_WINDOW >= 0)
    blocks = [jnp.maximum(2 * i - (NSA_WINDOW // NSA_SLC_BLOCK) + 2 * j, 0) for j in range(NSA_WINDOW // Q_BLOCK)]
    mexp = [expand(b // LANES, b % LANES, Q_BLOCK) for b in blocks]
    mask_sm = (jnp.concatenate(mexp[:-1], axis=0) > 0.5) & (keyw + t0 - NSA_WINDOW >= far_end)
    mask_sn = (jnp.concatenate([mexp[-1], expand(2 * i // LANES, 2 * i % LANES, Q_BLOCK)], axis=0) > 0.5) & near_ok
    s_sm = jnp.where(rep(mask_sm), kq(rw, WF), -jnp.inf)
    s_sn = jnp.where(rep(mask_sn), kq(rn, NEAR) + tb4, -jnp.inf)
    s_wf = jnp.where(rep(maskw), _dot(kw_ref[pl.ds(rw, WF), :], q4t), -jnp.inf)
    s_wn = jnp.where(rep(near_ok), _dot(kw_ref[pl.ds(rn, NEAR), :], q4t) + tb4, -jnp.inf)
    _flash_update(s_sm, vst_ref[:, pl.ds(rw, WF)], ms_ref, ls_ref, as_ref)
    _flash_update(s_sn, vst_ref[:, pl.ds(rn, NEAR)], ms_ref, ls_ref, as_ref)
    _flash_update(s_wf, vwt_ref[:, pl.ds(rw, WF)], mw_ref, lw_ref, aw_ref)
    _flash_update(s_wn, vwt_ref[:, pl.ds(rn, NEAR)], mw_ref, lw_ref, aw_ref)
    o_s = as_ref[...] / jnp.maximum(ls_ref[...], 1e-30)
    o_w = aw_ref[...] / jnp.maximum(lw_ref[...], 1e-30)

    gates = jax.nn.sigmoid(glt_ref[...])
    for r in range(R):
        hs = slice(r * HEAD_DIM, (r + 1) * HEAD_DIM)
        qs = slice(r * T, (r + 1) * T)
        o = (gates[3 * r:3 * r + 1, :] * oct_ref[hs, :]
             + gates[3 * r + 1:3 * r + 2, :] * o_s[:, qs]
             + gates[3 * r + 2:3 * r + 3, :] * o_w[:, qs])
        ot_ref[hs, :] = o.astype(BF16)


def _slcwin(nqt, ks, vst, kw, vwt, selt, tbt, oct, glt, S):
    G, R = NSA_KV_GROUPS, NSA_GROUP_SIZE
    spad = ks.shape[1]
    NW = selt.shape[1]
    once = pl.Buffered(1)
    kaspec = pl.BlockSpec((None, spad, KAUG_W), lambda g, i: (g, 0, 0), pipeline_mode=once)
    kspec = pl.BlockSpec((None, spad, HEAD_DIM), lambda g, i: (g, 0, 0), pipeline_mode=once)
    vspec = pl.BlockSpec((None, HEAD_DIM, spad), lambda g, i: (g, 0, 0), pipeline_mode=once)
    qspec = pl.BlockSpec((R * HEAD_DIM, Q_BLOCK), lambda g, i: (g, i))
    st = lambda n: pltpu.VMEM((n, R * Q_BLOCK), F32)
    return pl.pallas_call(
        _slcwin_body,
        grid=(G, S // Q_BLOCK),
        in_specs=[qspec, kaspec, vspec, kspec, vspec,
                  pl.BlockSpec((None, NW, LANES, Q_BLOCK), lambda g, i: (g, 0, 0, i)),
                  pl.BlockSpec((R, NEAR, Q_BLOCK), lambda g, i: (g, 0, 0)),
                  qspec,
                  pl.BlockSpec((None, 16, Q_BLOCK), lambda g, i: (g, 0, i))],
        out_specs=qspec,
        out_shape=jax.ShapeDtypeStruct((NSA_W, S), BF16),
        scratch_shapes=[st(1), st(1), st(HEAD_DIM), st(1), st(1), st(HEAD_DIM),
                        pltpu.VMEM((2, KEY_TILE, R * Q_BLOCK), F32),
                        pltpu.VMEM((NW, KAUG_W, R * Q_BLOCK), BF16)],
        compiler_params=_cparams(("arbitrary", "arbitrary")),
        name="slcwin",
    )(nqt, ks, vst, kw, vwt, selt, tbt, oct, glt)


INT_MIN = -2 ** 31
INT_MAX = 2 ** 31 - 1
MAX_PEEL_ROUNDS = 64
UNTESTED_BITS = 16
NEG_INF_KEY = int(np.array(-np.inf, np.float32).view(np.int32)) ^ 0x7FFFFFFF


def _dsa_body(dqt_ref, dk_ref, dvt_ref, iqt_ref, ikt_ref, iwt_ref, tbt_ref, ltri_ref, ot_ref,
              keys_ref, m_ref, l_ref, acc_ref, sbuf_ref, *, S, KEEP):
    i = pl.program_id(0)
    t0 = i * Q_BLOCK
    T = Q_BLOCK
    H = DSA_HEADS
    ntile = (t0 + Q_BLOCK + KEY_TILE - 1) // KEY_TILE
    tq = t0 + lax.broadcasted_iota(I32, (1, T), 1)
    key512 = lax.broadcasted_iota(I32, (KEY_TILE, 1), 0)
    ktile = lambda kt: keys_ref[pl.ds(pl.multiple_of(Q_BLOCK + kt * KEY_TILE, LANES), KEY_TILE), :]

    keys_ref[0:Q_BLOCK, :] = jnp.full((Q_BLOCK, T), NEG_INF_KEY, I32)
    iqt = iqt_ref[...]
    iwt = iwt_ref[...] * (IDX_HEADS ** -0.5)
    qis = [jnp.concatenate([iqt[(2 * j) * IDX_DIM:(2 * j + 1) * IDX_DIM, :],
                            iqt[(2 * j + 1) * IDX_DIM:(2 * j + 2) * IDX_DIM, :]], axis=1)
           for j in range(IDX_HEADS // 2)]
    wrow = [iwt[h:h + 1, :] for h in range(IDX_HEADS)]

    def idx_dots(kt, slot):
        ik = ikt_ref[:, pl.ds(pl.multiple_of(kt * KEY_TILE, KEY_TILE), KEY_TILE)]
        for j in range(IDX_HEADS // 2):
            sbuf_ref[slot, j] = _dot_tn(ik, qis[j])

    def idx_keys(kt, slot):
        c0 = pl.multiple_of(kt * KEY_TILE, KEY_TILE)
        sc = jnp.zeros((KEY_TILE, T), F32)
        for j in range(IDX_HEADS // 2):
            d = jnp.maximum(sbuf_ref[slot, j], 0.0)
            sc = sc + wrow[2 * j] * d[:, 0:T]
            sc = sc + wrow[2 * j + 1] * d[:, T:2 * T]
        sc = jnp.where(sc == 0.0, 0.0, sc)
        sc = jnp.where(c0 + key512 <= tq, sc, -jnp.inf)
        bits = pltpu.bitcast(sc, I32)
        keys_ref[pl.ds(pl.multiple_of(Q_BLOCK + c0, LANES), KEY_TILE), :] = jnp.where(
            bits < 0, bits ^ 0x7FFFFFFF, bits)

    idx_dots(0, 0)

    def score2(k2, carry):
        kt = 2 * k2
        idx_dots(kt + 1, 1)
        idx_keys(kt, 0)
        idx_dots(kt + 2, 0)
        idx_keys(kt + 1, 1)
        return carry

    lax.fori_loop(0, (ntile + 1) // 2, score2, 0)

    def count(pred):
        def body(kt, acc):
            hit = jnp.where(pred(ktile(kt), kt * KEY_TILE + key512), 1, 0)
            return acc + jnp.sum(hit.reshape(KEY_TILE // SUBLANES, SUBLANES, T), axis=0)
        acc = lax.fori_loop(0, ntile, body, jnp.zeros((SUBLANES, T), I32))
        return jnp.sum(acc, axis=0, keepdims=True)

    def min_at_least(lowest):
        def body(kt, acc):
            kk = ktile(kt)
            v = jnp.where(kk >= lowest, kk, INT_MAX)
            return jnp.minimum(acc, jnp.min(v.reshape(KEY_TILE // SUBLANES, SUBLANES, T), axis=0))
        acc = lax.fori_loop(0, ntile, body, jnp.full((SUBLANES, T), INT_MAX, I32))
        return jnp.min(acc, axis=0, keepdims=True)

    def any_open(st, over):
        _, excess, fin, _ = st
        return jnp.max(jnp.where((fin == 0) & (excess > over), 1, 0)) > 0

    def bisect(b, st):
        thr, excess, fin, need = st
        cand = thr + lax.shift_left(jnp.int32(1), 31 - b)
        n = count(lambda kk, ab: kk >= cand)
        ok = (n >= KEEP) & (fin == 0)
        return jnp.where(ok, cand, thr), jnp.where(ok, n - KEEP, excess), fin, need

    def peel(st):
        thr, excess, fin, need = st
        low = min_at_least(thr)
        n_low = count(lambda kk, ab: kk == low)
        active = (fin == 0) & (excess > 0)
        tied = active & (n_low > excess)
        drop = active & (n_low <= excess)
        return (jnp.where(tied, low, jnp.where(drop, low + 1, thr)), jnp.where(drop, excess - n_low, excess),
                jnp.where(tied, 1, fin), jnp.where(tied, n_low - excess, need))

    zero = jnp.zeros((1, T), I32)
    st = (jnp.full((1, T), INT_MIN, I32), KEY_TILE * ntile - KEEP + zero, zero, zero)
    st = peel(bisect(0, st))
    st = lax.fori_loop(1, UNTESTED_BITS, bisect, st)
    _, st = lax.while_loop(lambda c: (c[0] < 32) & any_open(c[1], 1),
                           lambda c: (c[0] + 1, bisect(c[0], c[1])), (jnp.int32(UNTESTED_BITS), st))
    _, st = lax.while_loop(lambda c: (c[0] < MAX_PEEL_ROUNDS) & any_open(c[1], 0),
                           lambda c: (c[0] + 1, peel(c[1])), (jnp.int32(0), st))
    thr, _, fin, need = st
    tie = (fin == 1) & (thr > NEG_INF_KEY)

    def tie_cut():
        needf = need.astype(F32)

        def body(kt, c):
            before, jmax = c
            is_tie = ktile(kt) == thr
            tied = jnp.where(is_tie, 1.0, 0.0)
            rank = before + _dot(ltri_ref[...], tied.astype(BF16))
            admit = is_tie & (rank < needf)
            jm = jnp.max(jnp.where(admit, kt * KEY_TILE + key512, -1), axis=0, keepdims=True)
            return before + jnp.sum(tied, axis=0, keepdims=True), jnp.maximum(jmax, jm)

        _, jmax = lax.fori_loop(0, ntile, body, (jnp.zeros((1, T), F32), jnp.full((1, T), -1, I32)))
        return jnp.where(tie, jmax + 1, S)

    jcut = lax.cond(jnp.max(jnp.where(tie, 1, 0)) > 0, tie_cut, lambda: jnp.full((1, T), S, I32))

    def chosen(kk, ab):
        return (kk > thr) | ((kk == thr) & (ab < jcut))

    _flash_init(m_ref, l_ref, acc_ref)
    lo = lax.broadcasted_iota(I32, (LANES, 1), 0) < HEAD_DIM
    qpairs = []
    for pr in range(H // 2):
        qp = dqt_ref[pr * LANES:(pr + 1) * LANES, :]
        zero = jnp.zeros_like(qp)
        qpairs.append(jnp.concatenate([jnp.where(lo, qp, zero), jnp.where(lo, zero, qp)], axis=1))

    def scores(r0, nkeys, mask, bias):
        mask2 = jnp.concatenate([mask, mask], axis=1)
        out = []
        for pr in range(H // 2):
            s = _dot(dk_ref[pl.ds(r0, nkeys), pr * LANES:(pr + 1) * LANES], qpairs[pr])
            if bias is not None:
                s = s + jnp.concatenate([bias[2 * pr], bias[2 * pr + 1]], axis=1)
            out.append(jnp.where(mask2, s, -jnp.inf))
        return out

    def update(pr, s, r0, nkeys):
        m_old = m_ref[pr]
        m_new = jnp.maximum(m_old, jnp.max(s, axis=0, keepdims=True))
        alpha = jnp.exp2(m_old - m_new)
        p = jnp.exp2(s - m_new)
        l_ref[pr] = alpha * l_ref[pr] + jnp.sum(p, axis=0, keepdims=True)
        m_ref[pr] = m_new
        vt = dvt_ref[pr * LANES:(pr + 1) * LANES, pl.ds(r0, nkeys)]
        acc_ref[pr] = alpha * acc_ref[pr] + _dot(vt, p.astype(BF16))

    far_end = t0 - Q_BLOCK
    nfar = jnp.maximum(far_end + KEY_TILE - 1, 0) // KEY_TILE
    far_row = lambda kt: pl.multiple_of(DSA_PADF + kt * KEY_TILE, LANES)

    def score_far(kt, slot):
        ab = kt * KEY_TILE + key512
        mask = chosen(ktile(jnp.minimum(kt, ntile - 1)), ab) & (ab < far_end)
        for pr, s in enumerate(scores(far_row(kt), KEY_TILE, mask, None)):
            sbuf_ref[slot, pr] = s

    def update_far(kt, slot):
        for pr in range(H // 2):
            update(pr, sbuf_ref[slot, pr], far_row(kt), KEY_TILE)

    score_far(0, 0)

    def far2(k2, carry):
        kt = 2 * k2
        score_far(kt + 1, 1)
        update_far(kt, 0)
        score_far(kt + 2, 0)
        update_far(kt + 1, 1)
        return carry

    lax.fori_loop(0, (nfar + 1) // 2, far2, 0)

    abn = lax.broadcasted_iota(I32, (NEAR, 1), 0) + t0 - Q_BLOCK
    kkn = keys_ref[pl.ds(pl.multiple_of(t0, LANES), NEAR), :]
    rn = pl.multiple_of(t0 + DSA_PADF - Q_BLOCK, LANES)
    for pr, s in enumerate(scores(rn, NEAR, chosen(kkn, abn) & (abn <= tq) & (abn >= 0), tbt_ref)):
        update(pr, s, rn, NEAR)

    for pr in range(H // 2):
        o = acc_ref[pr] / jnp.maximum(l_ref[pr], 1e-30)
        ot_ref[pr * LANES:(pr + 1) * LANES, :] = jnp.where(lo, o[:, 0:T], o[:, T:2 * T]).astype(BF16)


def _dsa(dqt, dkp, dvtp, iqt, ikt, iwt, tbt, ltri, S):
    KEEP = min(DSA_TOPK_MAX, S // 4)
    H = DSA_HEADS
    T = Q_BLOCK
    once = pl.Buffered(1)
    full = lambda a: pl.BlockSpec(a.shape, lambda i: (0,) * a.ndim, pipeline_mode=once)
    return pl.pallas_call(
        functools.partial(_dsa_body, S=S, KEEP=KEEP),
        grid=(S // T,),
        in_specs=[pl.BlockSpec((DSA_W, T), lambda i: (0, i)),
                  full(dkp), full(dvtp),
                  pl.BlockSpec((IDX_HEADS * IDX_DIM, T), lambda i: (0, i)),
                  full(ikt),
                  pl.BlockSpec((IDX_HEADS, T), lambda i: (0, i)),
                  full(tbt), full(ltri)],
        out_specs=pl.BlockSpec((DSA_W, T), lambda i: (0, i)),
        out_shape=jax.ShapeDtypeStruct((DSA_W, S), BF16),
        scratch_shapes=[pltpu.VMEM((Q_BLOCK + S + KEY_TILE, T), I32),
                        pltpu.VMEM((H // 2, 1, 2 * T), F32),
                        pltpu.VMEM((H // 2, 1, 2 * T), F32),
                        pltpu.VMEM((H // 2, LANES, 2 * T), F32),
                        pltpu.VMEM((2, H // 2, KEY_TILE, 2 * T), F32)],
        compiler_params=_cparams(("arbitrary",)),
        name="dsa",
    )(dqt, dkp, dvtp, iqt, ikt, iwt, tbt, ltri)


def _split_bf16(a):
    hi = a.astype(BF16)
    return hi, (a - hi.astype(F32)).astype(BF16)


def _merge_body(x_ref, oa_ref, ob_ref, gab_ref, wa_ref, wb_ref, wo_ref, gffn_ref, wr_ref, br_ref,
                x1_ref, h2_ref, ridx_ref, rw_ref):
    a = _dot(oa_ref[...], wa_ref[...])
    b = _dot(ob_ref[...], wb_ref[...])
    gab = gab_ref[...]
    merged = gab[:, :D_MODEL] * a + gab[:, D_MODEL:] * b
    x1 = x_ref[...] + _dot(merged.astype(BF16), wo_ref[...])
    x1_ref[...] = x1
    h2 = x1 * lax.rsqrt(jnp.mean(x1 * x1, axis=-1, keepdims=True) + EPS) * gffn_ref[...]
    h2_ref[...] = h2
    hh, hl = _split_bf16(h2)
    wh, wl = _split_bf16(wr_ref[...])
    logits = _dot(hh, wh) + _dot(hh, wl) + _dot(hl, wh) + br_ref[...]
    lane = lax.broadcasted_iota(I32, (1, LANES), 1)
    idx_out = jnp.zeros(logits.shape, I32)
    val_out = jnp.full(logits.shape, -jnp.inf, F32)
    for k in range(TOP_K):
        mx = jnp.max(logits, axis=-1, keepdims=True)
        first = jnp.min(jnp.where(logits == mx, lane, LANES), axis=-1, keepdims=True)
        idx_out = jnp.where(lane == k, first, idx_out)
        val_out = jnp.where(lane == k, mx, val_out)
        logits = jnp.where(lane == first, -jnp.inf, logits)
    e = jnp.exp(val_out - jnp.max(val_out, axis=-1, keepdims=True))
    ridx_ref[...] = idx_out
    rw_ref[...] = e / jnp.sum(e, axis=-1, keepdims=True)


def _merge(x2, oa, ob, gab, wa, wb, wo, gffn, wr, br, tm=512):
    S = x2.shape[0]
    row = lambda n: pl.BlockSpec((tm, n), lambda i: (i, 0))
    full = lambda a: pl.BlockSpec(a.shape, lambda i: (0,) * a.ndim)
    return pl.pallas_call(
        _merge_body,
        grid=(S // tm,),
        in_specs=[row(D_MODEL), row(NSA_W), row(DSA_W), row(2 * D_MODEL),
                  full(wa), full(wb), full(wo), full(gffn), full(wr), full(br)],
        out_specs=[row(D_MODEL), row(D_MODEL), row(LANES), row(LANES)],
        out_shape=[jax.ShapeDtypeStruct((S, D_MODEL), F32), jax.ShapeDtypeStruct((S, D_MODEL), F32),
                   jax.ShapeDtypeStruct((S, LANES), I32), jax.ShapeDtypeStruct((S, LANES), F32)],
        compiler_params=_cparams(("arbitrary",)),
        name="merge",
    )(x2, oa, ob, gab, wa, wb, wo, gffn, wr, br)


def _rank_body(ridx_ref, rank_ref, cnt_ref, carry_ref):
    @pl.when(pl.program_id(0) == 0)
    def _():
        carry_ref[...] = jnp.zeros(carry_ref.shape, F32)

    ridx = ridx_ref[...]
    tm = ridx.shape[0]
    lane = lax.broadcasted_iota(I32, (1, LANES), 1)
    hits = [lane == ridx[:, k:k + 1] for k in range(TOP_K)]
    member = jnp.zeros((tm, LANES), F32)
    for hk in hits:
        member = member + jnp.where(hk, 1.0, 0.0)
    ri = lax.broadcasted_iota(I32, (tm, tm), 0)
    ci = lax.broadcasted_iota(I32, (tm, tm), 1)
    lower = jnp.where(ci < ri, 1.0, 0.0).astype(BF16)
    before = _dot(lower, member.astype(BF16)) + carry_ref[0:1, :]
    out = jnp.zeros((tm, LANES), I32)
    for k, hk in enumerate(hits):
        rk = jnp.sum(jnp.where(hk, before, 0.0), axis=-1, keepdims=True)
        out = jnp.where(lane == k, rk.astype(I32), out)
    rank_ref[...] = out
    total = carry_ref[0:1, :] + jnp.sum(member, axis=0, keepdims=True)
    carry_ref[...] = jnp.broadcast_to(total, carry_ref.shape)
    cnt_ref[...] = jnp.broadcast_to(total, cnt_ref.shape).astype(I32)


def _rank(ridx, tm=512):
    S = ridx.shape[0]
    return pl.pallas_call(
        _rank_body,
        grid=(S // tm,),
        in_specs=[pl.BlockSpec((tm, LANES), lambda i: (i, 0))],
        out_specs=[pl.BlockSpec((tm, LANES), lambda i: (i, 0)), pl.BlockSpec((8, LANES), lambda i: (0, 0))],
        out_shape=[jax.ShapeDtypeStruct((S, LANES), I32), jax.ShapeDtypeStruct((8, LANES), I32)],
        scratch_shapes=[pltpu.VMEM((8, LANES), F32)],
        compiler_params=_cparams(("arbitrary",)),
        name="rank",
    )(ridx)


def _dispatch_body(dest_ref, h_ref, zero_ref, xe_ref, sem):
    del zero_ref
    i = pl.program_id(0)
    tm = h_ref.shape[0]

    def start(r, c):
        for k in range(TOP_K):
            d = dest_ref[(i * tm + r) * TOP_K + k]
            pltpu.make_async_copy(h_ref.at[pl.ds(r, 1)], xe_ref.at[pl.ds(d, 1)],
                                  sem).start(priority=k % 2)
        return c

    lax.fori_loop(0, tm, start, 0, unroll=ROW_DMA_UNROLL)
    for _ in range(TOP_K):
        pltpu.make_async_copy(h_ref, xe_ref.at[pl.ds(0, tm)], sem).wait()


def _dispatch(dest, h2, zeros_p, tm=128):
    S = h2.shape[0]
    return pl.pallas_call(
        _dispatch_body,
        grid_spec=pltpu.PrefetchScalarGridSpec(
            num_scalar_prefetch=1,
            grid=(S // tm,),
            in_specs=[pl.BlockSpec((tm, D_MODEL), lambda i, d: (i, 0)),
                      pl.BlockSpec(memory_space=pl.ANY)],
            out_specs=pl.BlockSpec(memory_space=pl.ANY),
            scratch_shapes=[pltpu.SemaphoreType.DMA(())]),
        out_shape=jax.ShapeDtypeStruct(zeros_p.shape, zeros_p.dtype),
        input_output_aliases={2: 0},
        compiler_params=_cparams(("arbitrary",)),
        name="dispatch",
    )(dest, h2, zeros_p)


def _experts_body(ce_ref, nu_ref, x_ref, w1_ref, b1_ref, w2_ref, b2_ref, y_ref, w1b_ref, w2b_ref):
    c = pl.program_id(0)

    @pl.when((c == 0) | (ce_ref[c] != ce_ref[jnp.maximum(c - 1, 0)]))
    def _():
        w1b_ref[...] = w1_ref[...].astype(BF16)
        w2b_ref[...] = w2_ref[...].astype(BF16)

    @pl.when(c < nu_ref[0])
    def _():
        gu = _dot(x_ref[...].astype(BF16), w1b_ref[...]) + b1_ref[...]
        gate = jnp.minimum(gu[:, :D_FF], SWIGLU_LIMIT)
        lin = jnp.clip(gu[:, D_FF:], -SWIGLU_LIMIT, SWIGLU_LIMIT)
        act = (lin + 1.0) * gate * jax.nn.sigmoid(SWIGLU_ALPHA * gate)
        y_ref[...] = _dot(act.astype(BF16), w2b_ref[...]) + b2_ref[...]

    @pl.when(c >= nu_ref[0])
    def _():
        y_ref[...] = jnp.zeros(y_ref.shape, F32)


def _experts(chunk_e, n_used, xe, w1, b1, w2, b2):
    P = xe.shape[0]
    nch = P // MOE_ROWS
    return pl.pallas_call(
        _experts_body,
        grid_spec=pltpu.PrefetchScalarGridSpec(
            num_scalar_prefetch=2,
            grid=(nch,),
            in_specs=[pl.BlockSpec((MOE_ROWS, D_MODEL), lambda c, ce, nu: (c, 0)),
                      pl.BlockSpec((None, D_MODEL, 2 * D_FF), lambda c, ce, nu: (ce[c], 0, 0)),
                      pl.BlockSpec((None, 1, 2 * D_FF), lambda c, ce, nu: (ce[c], 0, 0)),
                      pl.BlockSpec((None, D_FF, D_MODEL), lambda c, ce, nu: (ce[c], 0, 0)),
                      pl.BlockSpec((None, 1, D_MODEL), lambda c, ce, nu: (ce[c], 0, 0))],
            out_specs=pl.BlockSpec((MOE_ROWS, D_MODEL), lambda c, ce, nu: (c, 0)),
            scratch_shapes=[pltpu.VMEM((D_MODEL, 2 * D_FF), BF16), pltpu.VMEM((D_FF, D_MODEL), BF16)]),
        out_shape=jax.ShapeDtypeStruct((P, D_MODEL), F32),
        compiler_params=_cparams(("arbitrary",)),
        name="experts",
    )(chunk_e, n_used, xe, w1, b1, w2, b2)


def _final_body(dest_ref, x1_ref, rw_ref, p_ref, gple_ref, wg_ref, wp_ref, y_ref, o_ref, rows_ref, sems):
    i = pl.program_id(0)
    tm = x1_ref.shape[0]

    def gather(blk, slot):
        def start(r, c):
            for k in range(TOP_K):
                d = dest_ref[(blk * tm + r) * TOP_K + k]
                pltpu.make_async_copy(y_ref.at[pl.ds(d, 1)], rows_ref.at[slot, k, pl.ds(r, 1)],
                                      sems.at[slot]).start(priority=k % 2)
            return c
        lax.fori_loop(0, tm, start, 0, unroll=ROW_DMA_UNROLL)

    @pl.when(i == 0)
    def _():
        gather(0, 0)

    @pl.when(i + 1 < pl.num_programs(0))
    def _():
        gather(i + 1, (i + 1) % 2)

    slot = i % 2
    for k in range(TOP_K):
        pltpu.make_async_copy(y_ref.at[pl.ds(0, tm)], rows_ref.at[slot, k], sems.at[slot]).wait()
    rw = rw_ref[...]
    x2 = x1_ref[...]
    for k in range(TOP_K):
        x2 = x2 + rw[:, k:k + 1] * rows_ref[slot, k]
    hn = x2 * lax.rsqrt(jnp.mean(x2 * x2, axis=-1, keepdims=True) + EPS) * gple_ref[...]
    gate = jax.nn.sigmoid(_dot(hn.astype(BF16), wg_ref[...]))
    o_ref[...] = x2 + gate * _dot(p_ref[...].astype(BF16), wp_ref[...])


def _final(dest, x1, rw, p2, gple, wg, wp, yexp, tm=128):
    S = x1.shape[0]
    row = lambda n: pl.BlockSpec((tm, n), lambda i, d: (i, 0))
    full = lambda a: pl.BlockSpec(a.shape, lambda i, d: (0,) * a.ndim)
    return pl.pallas_call(
        _final_body,
        grid_spec=pltpu.PrefetchScalarGridSpec(
            num_scalar_prefetch=1,
            grid=(S // tm,),
            in_specs=[row(D_MODEL), row(LANES), row(PLE_DIM), full(gple), full(wg), full(wp),
                      pl.BlockSpec(memory_space=pl.ANY)],
            out_specs=row(D_MODEL),
            scratch_shapes=[pltpu.VMEM((2, TOP_K, tm, D_MODEL), F32), pltpu.SemaphoreType.DMA((2,))]),
        out_shape=jax.ShapeDtypeStruct((S, D_MODEL), F32),
        compiler_params=_cparams(("arbitrary",)),
        name="final",
    )(dest, x1, rw, p2, gple, wg, wp, yexp)


def _distance_bias(rel_cols):
    return (rel_cols[_rel_bucket_table(NEAR)] - rel_cols[REL_BUCKETS - 1][None, :]).T * LOG2E


def _near_bias_t(rel_cols):
    h = rel_cols.shape[1]
    g = jnp.pad(_distance_bias(rel_cols), ((0, 0), (Q_BLOCK, Q_BLOCK)))
    period = 2 * NEAR
    skew = jnp.tile(g, (1, NEAR))[:, :NEAR * (period - 1)].reshape(h, NEAR, period - 1)
    return skew[:, :, NEAR:NEAR + Q_BLOCK]


def _cmp_bias_t(rel_cols):
    g = jnp.pad(_distance_bias(rel_cols), ((0, 0), (NEAR, Q_BLOCK)))
    rows = []
    for j in range(4):
        for k in range(4):
            c0 = -NSA_CMP_STRIDE * j - (NSA_CMP_LEN - 1) + NSA_SLC_BLOCK * (k - 1)
            rows.append(g[:, c0 + NEAR:c0 + NEAR + Q_BLOCK])
    return jnp.stack(rows, axis=1)


def kernel(x, p, w_in, g_mix, g_q_nsa, g_k_nsa, g_q_dsa, g_k_dsa, pe_ck, w_ck1, w_ck2, pe_cv, w_cv1, w_cv2,
           w_branch_a, w_branch_b, w_out, rel_bias, g_ffn, w_router, b_router, w_e1, b_e1, w_e2, b_e2,
           g_ple, w_ple, w_ple_gate):
    B, S, D = x.shape
    assert B == 1 and D == D_MODEL and S % KEY_TILE == 0 and w_in.shape[0] == 1
    G, R = NSA_KV_GROUPS, NSA_GROUP_SIZE
    x2 = x.reshape(S, D)

    w0 = w_in[0]
    w_nat = jnp.concatenate([w0[:, 512:768], w0[:, 768:896], w0[:, 1024:1152], w0[:, 1816:2328], w0[:, 3136:5184],
                             w0[:, 1280:1304], w0[:, 3096:3136], jnp.zeros((D, LANES - 64), F32)], axis=1).astype(BF16)
    w_tr = jnp.concatenate([w0[:, 0:512], w0[:, 1304:1816], w0[:, 2328:2840], w0[:, 2840:3096],
                            w0[:, 896:1024], w0[:, 1152:1280]], axis=1).T.astype(BF16)
    two = lambda g: jnp.tile(g.reshape(1, HEAD_DIM), (1, 2))
    qscale = HEAD_DIM ** -0.5 * LOG2E
    gains = jnp.concatenate([two(g_k_nsa[0]), two(g_k_dsa[0])], axis=0)
    gcol = jnp.stack([g_q_nsa[0] * qscale, g_q_dsa[0] * qscale]).reshape(2, HEAD_DIM, 1)
    (cmpraw, kslc, kwin, dk, gab, small, nqt, dqt, dvt, iqt, vst2, vwt2) = _inproj(
        x2, g_mix[0].reshape(1, D), w_nat, w_tr, gains, gcol)

    NS = S // NSA_SLC_BLOCK
    nchunk = S // NSA_CMP_STRIDE
    xc = cmpraw.reshape(nchunk, NSA_CMP_STRIDE, 2 * G, HEAD_DIM).transpose(2, 0, 1, 3).reshape(
        2 * G, nchunk, NSA_CMP_STRIDE * HEAD_DIM)
    planes = lambda a: a.reshape(2 * G, NS, 4, -1).transpose(0, 2, 1, 3).reshape(2 * G, nchunk, -1)
    xa = planes(xc)
    xb = planes(jnp.roll(xc, -1, axis=1))
    half = NSA_CMP_STRIDE * HEAD_DIM
    w1 = jnp.stack([w_ck1[0], w_cv1[0]]).reshape(2, 2 * half, HEAD_DIM)
    pe = jnp.stack([pe_ck[0], pe_cv[0]]).reshape(2, 2 * half, 1)
    w2 = jnp.stack([w_ck2[0], w_cv2[0]]).astype(BF16)
    kvc = _compress(xa, xb, w1[:, :half].astype(BF16), w1[:, half:].astype(BF16), w1, pe, w2,
                    g_k_nsa[0].reshape(1, HEAD_DIM))

    rel_nsa = rel_bias[:, :NSA_HEADS]
    rel_dsa = rel_bias[:, NSA_HEADS:]
    oct, selt = _cmp(nqt, kvc[:G], kvc[G:].transpose(0, 2, 1), _cmp_bias_t(rel_nsa), S)
    kgroups = lambda a: jnp.pad(a.reshape(S, G, HEAD_DIM).transpose(1, 0, 2),
                                ((0, 0), (NSA_PADF, TAIL_PAD), (0, 0)))
    vgroups = lambda a: jnp.pad(a.reshape(G, HEAD_DIM, S), ((0, 0), (0, 0), (NSA_PADF, TAIL_PAD)))
    glt = small[:, SMALL_GATE:SMALL_GATE + 3 * NSA_HEADS].reshape(S, G, 3 * R).transpose(1, 2, 0)
    glt = jnp.pad(glt, ((0, 0), (0, 16 - 3 * R), (0, 0)))
    blk_lane = (np.arange(S) // NSA_SLC_BLOCK) % LANES
    onehot = jnp.asarray(np.eye(LANES, dtype=np.float32)[blk_lane] * MASK_BIG, BF16)
    kaug = jnp.concatenate([kslc.reshape(S, G, HEAD_DIM).transpose(1, 0, 2),
                            jnp.broadcast_to(onehot, (G, S, LANES)),
                            jnp.zeros((G, S, KAUG_W - HEAD_DIM - LANES), BF16)], axis=-1)
    kaug = jnp.pad(kaug, ((0, 0), (NSA_PADF, TAIL_PAD), (0, 0)))
    o_at = _slcwin(nqt, kaug, vgroups(vst2), kgroups(kwin), vgroups(vwt2), selt,
                   _near_bias_t(rel_nsa), oct, glt, S)

    dkp = jnp.pad(dk, ((DSA_PADF, TAIL_PAD), (0, 0)))
    dvtp = jnp.pad(dvt, ((0, 0), (DSA_PADF, TAIL_PAD)))
    ikt = jnp.pad(small[:, SMALL_IK:SMALL_IK + IDX_DIM].T.astype(BF16), ((0, 0), (0, 2 * KEY_TILE)))
    iwt = small[:, SMALL_IW:SMALL_IW + IDX_HEADS].T
    ltri = jnp.asarray(np.tril(np.ones((KEY_TILE, KEY_TILE), np.float32), -1), BF16)
    o_bt = _dsa(dqt, dkp, dvtp, iqt, ikt, iwt, _near_bias_t(rel_dsa), ltri, S)

    wr = jnp.pad(w_router[0], ((0, 0), (0, LANES - N_EXPERTS)))
    br = jnp.pad(b_router[0].reshape(1, N_EXPERTS), ((0, 0), (0, LANES - N_EXPERTS)), constant_values=-1e30)
    x1, h2, ridx, rw = _merge(x2, o_at.T, o_bt.T, gab, w_branch_a[0].astype(BF16), w_branch_b[0].astype(BF16),
                              w_out[0].astype(BF16), g_ffn[0].reshape(1, D), wr, br)

    rank, cnt = _rank(ridx)
    counts = cnt[0, :N_EXPERTS]
    padded = (counts + MOE_ROWS - 1) // MOE_ROWS * MOE_ROWS
    pad_end = jnp.cumsum(padded)
    pad_start = pad_end - padded
    eidx = ridx[:, :TOP_K]
    dest = (jnp.sum(jnp.where(eidx[..., None] == jnp.arange(N_EXPERTS), pad_start, 0), axis=-1)
            + rank[:, :TOP_K]).reshape(-1).astype(I32)
    n_chunks = -(-(S * TOP_K) // MOE_ROWS) + N_EXPERTS
    chunk_start = jnp.arange(n_chunks, dtype=I32) * MOE_ROWS
    chunk_e = jnp.minimum(jnp.sum(chunk_start[:, None] >= pad_end[None, :], axis=-1), N_EXPERTS - 1).astype(I32)
    n_used = (pad_end[-1] // MOE_ROWS).astype(I32).reshape(1)
    xe = _dispatch(dest, h2, jnp.zeros((n_chunks * MOE_ROWS, D), F32))
    yexp = _experts(chunk_e, n_used, xe, w_e1[0], b_e1[0].reshape(N_EXPERTS, 1, 2 * D_FF),
                    w_e2[0], b_e2[0].reshape(N_EXPERTS, 1, D))

    out = _final(dest, x1, rw, p[0].reshape(S, PLE_DIM), g_ple[0].reshape(1, D),
                 w_ple_gate[0].astype(BF16), w_ple[0].astype(BF16), yexp)
    return out.reshape(B, S, D)
```

```python
import functools
import math

import numpy as np
import jax
import jax.numpy as jnp
from jax import lax
from jax.experimental import pallas as pl
from jax.experimental.pallas import tpu as pltpu

F32 = jnp.float32
BF16 = jnp.bfloat16
I32 = jnp.int32

D_MODEL = 1024
PLE_DIM = 256
HEAD_DIM = 64
NSA_HEADS = 8
NSA_KV_GROUPS = 2
NSA_GROUP_SIZE = NSA_HEADS // NSA_KV_GROUPS
NSA_CMP_LEN = 32
NSA_CMP_STRIDE = 16
NSA_SLC_BLOCK = 64
NSA_SLC_TOPK = 16
NSA_LOCAL_BLOCKS = 2
NSA_WINDOW = 512
DSA_HEADS = 8
IDX_HEADS = 8
IDX_DIM = 32
DSA_TOPK_MAX = 256
Q_BLOCK = 128
REL_BUCKETS = 32
REL_MAX_EXACT = 16
REL_MAX_DIST = 128
N_EXPERTS = 32
TOP_K = 4
D_FF = 1024
SWIGLU_LIMIT = 7.0
SWIGLU_ALPHA = 1.702
MOE_ROWS = 512
EPS = 1e-6
LOG2E = math.log2(math.e)

NSA_W = NSA_HEADS * HEAD_DIM
DSA_W = DSA_HEADS * HEAD_DIM

LANES = 128
SUBLANES = 8
KEY_TILE = 512
NEAR = 2 * Q_BLOCK
NSA_PADF = NSA_WINDOW
DSA_PADF = Q_BLOCK
KAUG_W = 2 * LANES
MASK_BIG = 2.0 ** 100
TAIL_PAD = 3 * KEY_TILE
VMEM_LIMIT = 56 * 1024 * 1024
ROW_DMA_UNROLL = 8

NAT_CMP, NAT_KSLC, NAT_KWIN, NAT_DK, NAT_GAB, NAT_SMALL = 0, 256, 384, 512, 1024, 3072
TR_NQ, TR_DQ, TR_DV, TR_IQ, TR_VSLC, TR_VWIN = 0, 512, 1024, 1536, 1792, 1920
SMALL_GATE, SMALL_IK, SMALL_IW = 0, 24, 56


def _rel_bucket_table(n):
    d = np.arange(n)
    nf = np.maximum(d, 1).astype(np.float64)
    large = REL_MAX_EXACT + (np.log(nf / REL_MAX_EXACT) / math.log(REL_MAX_DIST / REL_MAX_EXACT)
                             * (REL_BUCKETS - REL_MAX_EXACT)).astype(np.int64)
    large = np.minimum(large, REL_BUCKETS - 1)
    return np.where(d < REL_MAX_EXACT, d, large)


def _cparams(sem, vmem=VMEM_LIMIT):
    return pltpu.CompilerParams(dimension_semantics=sem, vmem_limit_bytes=vmem)


def _dot(a, b):
    return jnp.dot(a, b, preferred_element_type=F32)


def _dot_tn(a, b):
    return lax.dot_general(a, b, (((0,), (0,)), ((), ())), preferred_element_type=F32)


def _inproj_body(x_ref, gmix_ref, w_ref, wt_ref, gains_ref, gcol_ref,
                 cmp_ref, kslc_ref, kwin_ref, dk_ref, gab_ref, small_ref,
                 nqt_ref, dqt_ref, dvt_ref, iqt_ref, vst_ref, vwt_ref):
    x = x_ref[...]
    h = x * lax.rsqrt(jnp.mean(x * x, axis=-1, keepdims=True) + EPS) * gmix_ref[...]
    hb = h.astype(BF16)
    tm = hb.shape[0]
    lo = lax.broadcasted_iota(I32, (1, LANES), 1) < HEAD_DIM

    def mm(off, n):
        return _dot(hb, w_ref[:, off:off + n])

    def headnorm(y, g):
        y2 = y * y
        s_lo = jnp.sum(jnp.where(lo, y2, 0.0), axis=-1, keepdims=True)
        s_hi = jnp.sum(jnp.where(lo, 0.0, y2), axis=-1, keepdims=True)
        r = jnp.where(lo, lax.rsqrt(s_lo * (1.0 / HEAD_DIM) + EPS), lax.rsqrt(s_hi * (1.0 / HEAD_DIM) + EPS))
        return y * r * g

    def headnorm_t(y, g):
        y3 = y.reshape(y.shape[0] // HEAD_DIM, HEAD_DIM, tm)
        r = lax.rsqrt(jnp.mean(y3 * y3, axis=1, keepdims=True) + EPS)
        return (y3 * r * g).reshape(y.shape)

    g_kn, g_kd = gains_ref[0:1, :], gains_ref[1:2, :]
    cmp_ref[...] = mm(NAT_CMP, 256).astype(BF16)
    kslc_ref[...] = headnorm(mm(NAT_KSLC, LANES), g_kn).astype(BF16)
    kwin_ref[...] = headnorm(mm(NAT_KWIN, LANES), g_kn).astype(BF16)
    for c in range(DSA_W // LANES):
        dk_ref[:, c * LANES:(c + 1) * LANES] = headnorm(mm(NAT_DK + c * LANES, LANES), g_kd).astype(BF16)
    gab_ref[...] = jax.nn.sigmoid(mm(NAT_GAB, 2 * D_MODEL))
    small_ref[...] = mm(NAT_SMALL, LANES)

    def mm_t(off, n):
        return lax.dot_general(wt_ref[off:off + n, :], hb, (((1,), (1,)), ((), ())), preferred_element_type=F32)

    nqt_ref[...] = headnorm_t(mm_t(TR_NQ, NSA_W), gcol_ref[0]).astype(BF16)
    dqt_ref[...] = headnorm_t(mm_t(TR_DQ, DSA_W), gcol_ref[1]).astype(BF16)
    dvt_ref[...] = mm_t(TR_DV, DSA_W).astype(BF16)
    iqt_ref[...] = (mm_t(TR_IQ, IDX_HEADS * IDX_DIM) * (IDX_DIM ** -0.5)).astype(BF16)
    vst_ref[...] = mm_t(TR_VSLC, LANES).astype(BF16)
    vwt_ref[...] = mm_t(TR_VWIN, LANES).astype(BF16)


def _inproj(x2, g_mix, w_nat, w_tr, gains, gcol, tm=512):
    S = x2.shape[0]
    row = lambda n: pl.BlockSpec((tm, n), lambda i: (i, 0))
    col = lambda n: pl.BlockSpec((n, tm), lambda i: (0, i))
    full = lambda a: pl.BlockSpec(a.shape, lambda i: (0,) * a.ndim)
    nat = [(256, BF16), (LANES, BF16), (LANES, BF16), (DSA_W, BF16), (2 * D_MODEL, F32), (LANES, F32)]
    tr = [NSA_W, DSA_W, DSA_W, IDX_HEADS * IDX_DIM, LANES, LANES]
    return pl.pallas_call(
        _inproj_body,
        grid=(S // tm,),
        in_specs=[row(D_MODEL), full(g_mix), full(w_nat), full(w_tr), full(gains), full(gcol)],
        out_specs=[row(n) for n, _ in nat] + [col(n) for n in tr],
        out_shape=[jax.ShapeDtypeStruct((S, n), dt) for n, dt in nat]
                  + [jax.ShapeDtypeStruct((n, S), BF16) for n in tr],
        compiler_params=_cparams(("arbitrary",)),
        name="inproj",
    )(x2, g_mix, w_nat, w_tr, gains, gcol)


def _compress_body(xa_ref, xb_ref, w1a_ref, w1b_ref, w1f_ref, pe_ref, w2_ref, gk_ref, o_ref):
    is_k = pl.program_id(0) < NSA_KV_GROUPS
    pe_term = jnp.sum(pe_ref[...] * w1f_ref[...], axis=0, keepdims=True)
    pre = _dot(xa_ref[...], w1a_ref[...]) + _dot(xb_ref[...], w1b_ref[...]) + pe_term
    hdn = pre * jax.nn.sigmoid(pre)
    out = _dot(hdn.astype(BF16), w2_ref[...])
    normed = out * lax.rsqrt(jnp.mean(out * out, axis=-1, keepdims=True) + EPS) * gk_ref[...]
    o_ref[...] = jnp.where(is_k, normed, out).astype(BF16)


def _compress(xa, xb, w1a, w1b, w1f, pe, w2, gk):
    n4, ncp, kdim = xa.shape
    G = NSA_KV_GROUPS
    kv = lambda i: i // G
    return pl.pallas_call(
        _compress_body,
        grid=(n4,),
        in_specs=[pl.BlockSpec((None, ncp, kdim), lambda i: (i, 0, 0)),
                  pl.BlockSpec((None, ncp, kdim), lambda i: (i, 0, 0)),
                  pl.BlockSpec((None, kdim, HEAD_DIM), lambda i: (kv(i), 0, 0)),
                  pl.BlockSpec((None, kdim, HEAD_DIM), lambda i: (kv(i), 0, 0)),
                  pl.BlockSpec((None, 2 * kdim, HEAD_DIM), lambda i: (kv(i), 0, 0)),
                  pl.BlockSpec((None, 2 * kdim, 1), lambda i: (kv(i), 0, 0)),
                  pl.BlockSpec((None, HEAD_DIM, HEAD_DIM), lambda i: (kv(i), 0, 0)),
                  pl.BlockSpec((1, HEAD_DIM), lambda i: (0, 0))],
        out_specs=pl.BlockSpec((None, ncp, HEAD_DIM), lambda i: (i, 0, 0)),
        out_shape=jax.ShapeDtypeStruct((n4, ncp, HEAD_DIM), BF16),
        compiler_params=_cparams(("arbitrary",)),
        name="compress",
    )(xa, xb, w1a, w1b, w1f, pe, w2, gk)


def _cmp_body(qt_ref, kc_ref, vct_ref, bt_ref, oct_ref, selt_ref, s_ref, imp_ref, *, NS):
    i = pl.program_id(1)

    @pl.when((pl.program_id(0) == 0) & (i == 0))
    def _():
        imp_ref[...] = jnp.zeros(imp_ref.shape, F32)

    imp_prev = imp_ref[...]
    ib = jnp.minimum(i, pl.num_programs(1) - 2)
    t0 = ib * Q_BLOCK
    T = Q_BLOCK
    R = NSA_GROUP_SIZE
    WIN = 2 * SUBLANES

    def step(ne):
        NE = 4 * ne
        row = lax.broadcasted_iota(I32, (NE, 1), 0)
        plane = row // ne
        nblk = row - plane * ne
        cend = NSA_SLC_BLOCK * nblk + NSA_CMP_STRIDE * plane + (NSA_CMP_LEN - 1)
        tq = t0 + lax.broadcasted_iota(I32, (1, T), 1)
        mask = cend <= tq
        kc = jnp.concatenate([kc_ref[j * NS:j * NS + ne, :] for j in range(4)], axis=0)
        vct = jnp.concatenate([vct_ref[:, j * NS:j * NS + ne] for j in range(4)], axis=1)
        wstart = jnp.clip((2 * ib - 2) // SUBLANES * SUBLANES, 0, ne - WIN)
        wrow = lax.broadcasted_iota(I32, (WIN, 1), 0)
        psum = jnp.zeros((NE, T), F32)
        for r in range(R):
            s_ref[r, 0:NE, :] = _dot(kc, qt_ref[r * HEAD_DIM:(r + 1) * HEAD_DIM, :])
        for r in range(R):
            bt = bt_ref[r]
            for j in range(4):
                add = jnp.zeros((WIN, T), F32)
                for k in range(4):
                    n = 2 * ib + 1 - k
                    add = add + jnp.where((wrow == n - wstart) & (n >= 0), bt[4 * j + k:4 * j + k + 1, :], 0.0)
                sl = pl.ds(pl.multiple_of(j * ne + wstart, SUBLANES), WIN)
                s_ref[r, sl, :] = s_ref[r, sl, :] + add
            sc = jnp.where(mask, s_ref[r, 0:NE, :], -jnp.inf)
            m = jnp.maximum(jnp.max(sc, axis=0, keepdims=True), -1e30)
            e = jnp.exp2(sc - m)
            pc = e / jnp.maximum(jnp.sum(e, axis=0, keepdims=True), 1e-30)
            oct_ref[r * HEAD_DIM:(r + 1) * HEAD_DIM, :] = _dot(vct, pc.astype(BF16))
            psum = psum + pc
        p0, p1, p2, p3 = (psum[j * ne:(j + 1) * ne, :] for j in range(4))
        p3s = jnp.where(lax.broadcasted_iota(I32, (ne, 1), 0) == 0, 0.0, pltpu.roll(p3, 1, axis=0))
        imp_ref[0:ne, :] = p0 + p1 + p2 + 0.5 * p3 + 0.5 * p3s
        if ne < NS:
            imp_ref[ne:NS, :] = jnp.zeros((NS - ne, T), F32)

        n_ar = lax.broadcasted_iota(I32, (NS, 1), 0)
        tqs = (i - 1) * Q_BLOCK + lax.broadcasted_iota(I32, (1, T), 1)
        cur = tqs // NSA_SLC_BLOCK
        valid = NSA_SLC_BLOCK * n_ar <= tqs
        forced = valid & ((n_ar == 0) | (n_ar > cur - NSA_LOCAL_BLOCKS))
        score = jnp.where(forced, 1e30, jnp.where(valid, imp_prev, -1.0))
        sel = jnp.zeros((NS, T), jnp.bool_)
        for _ in range(min(NSA_SLC_TOPK, NS)):
            mx = jnp.max(score, axis=0, keepdims=True)
            first = jnp.min(jnp.where(score == mx, n_ar, NS), axis=0, keepdims=True)
            pick = n_ar == first
            sel = sel | pick
            score = jnp.where(pick, -2.0, score)
        selb = jnp.where(sel, 1.0, 0.0).astype(BF16)
        if NS < LANES:
            selb = jnp.concatenate([selb, jnp.zeros((LANES - NS, T), BF16)], axis=0)
        for w in range(selt_ref.shape[0]):
            selt_ref[w] = selb[w * LANES:(w + 1) * LANES, :]

    quarter = NS // 4
    if quarter % WIN == 0:
        which = (2 * ib + 2 + quarter - 1) // quarter - 1
        for v in range(4):
            pl.when(which == v)(functools.partial(step, (v + 1) * quarter))
    else:
        step(NS)


def _cmp(nqt, kc, vct, bt, S):
    NS = S // NSA_SLC_BLOCK
    NCP = 4 * NS
    NW = max(NS // LANES, 1)
    G, R = NSA_KV_GROUPS, NSA_GROUP_SIZE
    nb = S // Q_BLOCK
    return pl.pallas_call(
        functools.partial(_cmp_body, NS=NS),
        grid=(G, nb + 1),
        in_specs=[pl.BlockSpec((R * HEAD_DIM, Q_BLOCK), lambda g, i: (g, jnp.minimum(i, nb - 1))),
                  pl.BlockSpec((None, NCP, HEAD_DIM), lambda g, i: (g, 0, 0)),
                  pl.BlockSpec((None, HEAD_DIM, NCP), lambda g, i: (g, 0, 0)),
                  pl.BlockSpec((R, 16, Q_BLOCK), lambda g, i: (g, 0, 0))],
        out_specs=[pl.BlockSpec((R * HEAD_DIM, Q_BLOCK), lambda g, i: (g, jnp.minimum(i, nb - 1))),
                   pl.BlockSpec((None, NW, LANES, Q_BLOCK), lambda g, i: (g, 0, 0, jnp.maximum(i - 1, 0)))],
        out_shape=[jax.ShapeDtypeStruct((NSA_W, S), F32),
                   jax.ShapeDtypeStruct((G, NW, LANES, S), BF16)],
        scratch_shapes=[pltpu.VMEM((R, NCP, Q_BLOCK), F32), pltpu.VMEM((NS, Q_BLOCK), F32)],
        compiler_params=_cparams(("arbitrary", "arbitrary")),
        name="cmp",
    )(nqt, kc, vct, bt)


def _flash_init(m_ref, l_ref, acc_ref):
    m_ref[...] = jnp.full(m_ref.shape, -1e30, F32)
    l_ref[...] = jnp.zeros(l_ref.shape, F32)
    acc_ref[...] = jnp.zeros(acc_ref.shape, F32)


def _flash_update(s, vt, m_ref, l_ref, acc_ref):
    m_old = m_ref[...]
    m_new = jnp.maximum(m_old, jnp.max(s, axis=0, keepdims=True))
    alpha = jnp.exp2(m_old - m_new)
    p = jnp.exp2(s - m_new)
    l_ref[...] = alpha * l_ref[...] + jnp.sum(p, axis=0, keepdims=True)
    acc_ref[...] = alpha * acc_ref[...] + _dot(vt, p.astype(BF16))
    m_ref[...] = m_new


def _slcwin_body(qt_ref, ks_ref, vst_ref, kw_ref, vwt_ref, selt_ref, tbt_ref, oct_ref, glt_ref, ot_ref,
                 ms_ref, ls_ref, as_ref, mw_ref, lw_ref, aw_ref, sbuf_ref, qaug_ref):
    i = pl.program_id(1)
    t0 = i * Q_BLOCK
    R = NSA_GROUP_SIZE
    T = Q_BLOCK
    q4t = jnp.concatenate([qt_ref[r * HEAD_DIM:(r + 1) * HEAD_DIM, :] for r in range(R)], axis=1)
    tq = lax.broadcasted_iota(I32, (1, T), 1)
    rep = lambda mk: jnp.concatenate([mk] * R, axis=1)
    tb4 = jnp.concatenate([tbt_ref[r] for r in range(R)], axis=1)

    def expand(w, base, nkeys):
        keyi = lax.broadcasted_iota(I32, (nkeys, LANES), 0)
        blk = lax.broadcasted_iota(I32, (nkeys, LANES), 1)
        e = jnp.where(blk == base + keyi // NSA_SLC_BLOCK, 1.0, 0.0).astype(BF16)
        return _dot(e, selt_ref[w])

    _flash_init(ms_ref, ls_ref, as_ref)
    _flash_init(mw_ref, lw_ref, aw_ref)
    n_win = selt_ref.shape[0]
    zpad = jnp.zeros((KAUG_W - HEAD_DIM - LANES, R * T), BF16)
    for w in range(n_win):
        qaug_ref[w] = jnp.concatenate([q4t, rep(selt_ref[w] - jnp.asarray(1.0, BF16)), zpad], axis=0)
    far_end = jnp.maximum(t0 - Q_BLOCK, 0) // KEY_TILE * KEY_TILE
    nfar = far_end // KEY_TILE
    tiles_per_window = LANES * NSA_SLC_BLOCK // KEY_TILE

    def score_far(kt, slot):
        r0 = pl.multiple_of(NSA_PADF + kt * KEY_TILE, LANES)
        w = jnp.minimum(kt // tiles_per_window, n_win - 1)
        sbuf_ref[slot] = _dot(ks_ref[pl.ds(r0, KEY_TILE), :], qaug_ref[w])

    def update_far(kt, slot):
        r0 = pl.multiple_of(NSA_PADF + kt * KEY_TILE, LANES)
        _flash_update(sbuf_ref[slot], vst_ref[:, pl.ds(r0, KEY_TILE)], ms_ref, ls_ref, as_ref)

    score_far(0, 0)

    def far2(k2, carry):
        kt = 2 * k2
        score_far(kt + 1, 1)
        update_far(kt, 0)
        score_far(kt + 2, 0)
        update_far(kt + 1, 1)
        return carry

    lax.fori_loop(0, nfar // 2, far2, 0)

    @pl.when(nfar % 2 == 1)
    def _():
        update_far(nfar - 1, 0)

    kq = lambda r0, n: _dot(ks_ref[pl.ds(r0, n), 0:HEAD_DIM], q4t)
    keyn = lax.broadcasted_iota(I32, (NEAR, 1), 0)
    near_ok = (keyn - Q_BLOCK <= tq) & (keyn + t0 - Q_BLOCK >= 0)
    rn = pl.multiple_of(t0 + NSA_PADF - Q_BLOCK, LANES)
    WF = NSA_WINDOW - Q_BLOCK
    rw = pl.multiple_of(t0 + NSA_PADF - NSA_WINDOW, LANES)
    keyw = lax.broadcasted_iota(I32, (WF, 1), 0)
    maskw = (keyw > tq) & (keyw + t0 - NSA_WINDOW >= 0)
    blocks = [jnp.maximum(2 * i - (NSA_WINDOW // NSA_SLC_BLOCK) + 2 * j, 0) for j in range(NSA_WINDOW // Q_BLOCK)]
    mexp = [expand(b // LANES, b % LANES, Q_BLOCK) for b in blocks]
    mask_sm = (jnp.concatenate(mexp[:-1], axis=0) > 0.5) & (keyw + t0 - NSA_WINDOW >= far_end)
    mask_sn = (jnp.concatenate([mexp[-1], expand(2 * i // LANES, 2 * i % LANES, Q_BLOCK)], axis=0) > 0.5) & near_ok
    s_sm = jnp.where(rep(mask_sm), kq(rw, WF), -jnp.inf)
    s_sn = jnp.where(rep(mask_sn), kq(rn, NEAR) + tb4, -jnp.inf)
    s_wf = jnp.where(rep(maskw), _dot(kw_ref[pl.ds(rw, WF), :], q4t), -jnp.inf)
    s_wn = jnp.where(rep(near_ok), _dot(kw_ref[pl.ds(rn, NEAR), :], q4t) + tb4, -jnp.inf)
    _flash_update(s_sm, vst_ref[:, pl.ds(rw, WF)], ms_ref, ls_ref, as_ref)
    _flash_update(s_sn, vst_ref[:, pl.ds(rn, NEAR)], ms_ref, ls_ref, as_ref)
    _flash_update(s_wf, vwt_ref[:, pl.ds(rw, WF)], mw_ref, lw_ref, aw_ref)
    _flash_update(s_wn, vwt_ref[:, pl.ds(rn, NEAR)], mw_ref, lw_ref, aw_ref)
    o_s = as_ref[...] / jnp.maximum(ls_ref[...], 1e-30)
    o_w = aw_ref[...] / jnp.maximum(lw_ref[...], 1e-30)

    gates = jax.nn.sigmoid(glt_ref[...])
    for r in range(R):
        hs = slice(r * HEAD_DIM, (r + 1) * HEAD_DIM)
        qs = slice(r * T, (r + 1) * T)
        o = (gates[3 * r:3 * r + 1, :] * oct_ref[hs, :]
             + gates[3 * r + 1:3 * r + 2, :] * o_s[:, qs]
             + gates[3 * r + 2:3 * r + 3, :] * o_w[:, qs])
        ot_ref[hs, :] = o.astype(BF16)


def _slcwin(nqt, ks, vst, kw, vwt, selt, tbt, oct, glt, S):
    G, R = NSA_KV_GROUPS, NSA_GROUP_SIZE
    spad = ks.shape[1]
    NW = selt.shape[1]
    once = pl.Buffered(1)
    kaspec = pl.BlockSpec((None, spad, KAUG_W), lambda g, i: (g, 0, 0), pipeline_mode=once)
    kspec = pl.BlockSpec((None, spad, HEAD_DIM), lambda g, i: (g, 0, 0), pipeline_mode=once)
    vspec = pl.BlockSpec((None, HEAD_DIM, spad), lambda g, i: (g, 0, 0), pipeline_mode=once)
    qspec = pl.BlockSpec((R * HEAD_DIM, Q_BLOCK), lambda g, i: (g, i))
    st = lambda n: pltpu.VMEM((n, R * Q_BLOCK), F32)
    return pl.pallas_call(
        _slcwin_body,
        grid=(G, S // Q_BLOCK),
        in_specs=[qspec, kaspec, vspec, kspec, vspec,
                  pl.BlockSpec((None, NW, LANES, Q_BLOCK), lambda g, i: (g, 0, 0, i)),
                  pl.BlockSpec((R, NEAR, Q_BLOCK), lambda g, i: (g, 0, 0)),
                  qspec,
                  pl.BlockSpec((None, 16, Q_BLOCK), lambda g, i: (g, 0, i))],
        out_specs=qspec,
        out_shape=jax.ShapeDtypeStruct((NSA_W, S), BF16),
        scratch_shapes=[st(1), st(1), st(HEAD_DIM), st(1), st(1), st(HEAD_DIM),
                        pltpu.VMEM((2, KEY_TILE, R * Q_BLOCK), F32),
                        pltpu.VMEM((NW, KAUG_W, R * Q_BLOCK), BF16)],
        compiler_params=_cparams(("arbitrary", "arbitrary")),
        name="slcwin",
    )(nqt, ks, vst, kw, vwt, selt, tbt, oct, glt)


INT_MIN = -2 ** 31
INT_MAX = 2 ** 31 - 1
MAX_PEEL_ROUNDS = 64
UNTESTED_BITS = 16
NEG_INF_KEY = int(np.array(-np.inf, np.float32).view(np.int32)) ^ 0x7FFFFFFF


def _dsa_body(dqt_ref, dk_ref, dvt_ref, iqt_ref, ikt_ref, iwt_ref, tbt_ref, ltri_ref, ot_ref,
              keys_ref, m_ref, l_ref, acc_ref, sbuf_ref, *, S, KEEP):
    i = pl.program_id(0)
    t0 = i * Q_BLOCK
    T = Q_BLOCK
    H = DSA_HEADS
    ntile = (t0 + Q_BLOCK + KEY_TILE - 1) // KEY_TILE
    tq = t0 + lax.broadcasted_iota(I32, (1, T), 1)
    key512 = lax.broadcasted_iota(I32, (KEY_TILE, 1), 0)
    ktile = lambda kt: keys_ref[pl.ds(pl.multiple_of(Q_BLOCK + kt * KEY_TILE, LANES), KEY_TILE), :]

    keys_ref[0:Q_BLOCK, :] = jnp.full((Q_BLOCK, T), NEG_INF_KEY, I32)
    iqt = iqt_ref[...]
    iwt = iwt_ref[...] * (IDX_HEADS ** -0.5)
    qis = [jnp.concatenate([iqt[(2 * j) * IDX_DIM:(2 * j + 1) * IDX_DIM, :],
                            iqt[(2 * j + 1) * IDX_DIM:(2 * j + 2) * IDX_DIM, :]], axis=1)
           for j in range(IDX_HEADS // 2)]
    wrow = [iwt[h:h + 1, :] for h in range(IDX_HEADS)]

    def idx_dots(kt, slot):
        ik = ikt_ref[:, pl.ds(pl.multiple_of(kt * KEY_TILE, KEY_TILE), KEY_TILE)]
        for j in range(IDX_HEADS // 2):
            sbuf_ref[slot, j] = _dot_tn(ik, qis[j])

    def idx_keys(kt, slot):
        c0 = pl.multiple_of(kt * KEY_TILE, KEY_TILE)
        sc = jnp.zeros((KEY_TILE, T), F32)
        for j in range(IDX_HEADS // 2):
            d = jnp.maximum(sbuf_ref[slot, j], 0.0)
            sc = sc + wrow[2 * j] * d[:, 0:T]
            sc = sc + wrow[2 * j + 1] * d[:, T:2 * T]
        sc = jnp.where(sc == 0.0, 0.0, sc)
        sc = jnp.where(c0 + key512 <= tq, sc, -jnp.inf)
        bits = pltpu.bitcast(sc, I32)
        keys_ref[pl.ds(pl.multiple_of(Q_BLOCK + c0, LANES), KEY_TILE), :] = jnp.where(
            bits < 0, bits ^ 0x7FFFFFFF, bits)

    idx_dots(0, 0)

    def score2(k2, carry):
        kt = 2 * k2
        idx_dots(kt + 1, 1)
        idx_keys(kt, 0)
        idx_dots(kt + 2, 0)
        idx_keys(kt + 1, 1)
        return carry

    lax.fori_loop(0, (ntile + 1) // 2, score2, 0)

    def count(pred):
        def body(kt, acc):
            hit = jnp.where(pred(ktile(kt), kt * KEY_TILE + key512), 1, 0)
            return acc + jnp.sum(hit.reshape(KEY_TILE // SUBLANES, SUBLANES, T), axis=0)
        acc = lax.fori_loop(0, ntile, body, jnp.zeros((SUBLANES, T), I32))
        return jnp.sum(acc, axis=0, keepdims=True)

    def min_at_least(lowest):
        def body(kt, acc):
            kk = ktile(kt)
            v = jnp.where(kk >= lowest, kk, INT_MAX)
            return jnp.minimum(acc, jnp.min(v.reshape(KEY_TILE // SUBLANES, SUBLANES, T), axis=0))
        acc = lax.fori_loop(0, ntile, body, jnp.full((SUBLANES, T), INT_MAX, I32))
        return jnp.min(acc, axis=0, keepdims=True)

    def any_open(st, over):
        _, excess, fin, _ = st
        return jnp.max(jnp.where((fin == 0) & (excess > over), 1, 0)) > 0

    def bisect(b, st):
        thr, excess, fin, need = st
        cand = thr + lax.shift_left(jnp.int32(1), 31 - b)
        n = count(lambda kk, ab: kk >= cand)
        ok = (n >= KEEP) & (fin == 0)
        return jnp.where(ok, cand, thr), jnp.where(ok, n - KEEP, excess), fin, need

    def peel(st):
        thr, excess, fin, need = st
        low = min_at_least(thr)
        n_low = count(lambda kk, ab: kk == low)
        active = (fin == 0) & (excess > 0)
        tied = active & (n_low > excess)
        drop = active & (n_low <= excess)
        return (jnp.where(tied, low, jnp.where(drop, low + 1, thr)), jnp.where(drop, excess - n_low, excess),
                jnp.where(tied, 1, fin), jnp.where(tied, n_low - excess, need))

    zero = jnp.zeros((1, T), I32)
    st = (jnp.full((1, T), INT_MIN, I32), KEY_TILE * ntile - KEEP + zero, zero, zero)
    st = peel(bisect(0, st))
    st = lax.fori_loop(1, UNTESTED_BITS, bisect, st)
    _, st = lax.while_loop(lambda c: (c[0] < 32) & any_open(c[1], 1),
                           lambda c: (c[0] + 1, bisect(c[0], c[1])), (jnp.int32(UNTESTED_BITS), st))
    _, st = lax.while_loop(lambda c: (c[0] < MAX_PEEL_ROUNDS) & any_open(c[1], 0),
                           lambda c: (c[0] + 1, peel(c[1])), (jnp.int32(0), st))
    thr, _, fin, need = st
    tie = (fin == 1) & (thr > NEG_INF_KEY)

    def tie_cut():
        needf = need.astype(F32)

        def body(kt, c):
            before, jmax = c
            is_tie = ktile(kt) == thr
            tied = jnp.where(is_tie, 1.0, 0.0)
            rank = before + _dot(ltri_ref[...], tied.astype(BF16))
            admit = is_tie & (rank < needf)
            jm = jnp.max(jnp.where(admit, kt * KEY_TILE + key512, -1), axis=0, keepdims=True)
            return before + jnp.sum(tied, axis=0, keepdims=True), jnp.maximum(jmax, jm)

        _, jmax = lax.fori_loop(0, ntile, body, (jnp.zeros((1, T), F32), jnp.full((1, T), -1, I32)))
        return jnp.where(tie, jmax + 1, S)

    jcut = lax.cond(jnp.max(jnp.where(tie, 1, 0)) > 0, tie_cut, lambda: jnp.full((1, T), S, I32))

    def chosen(kk, ab):
        return (kk > thr) | ((kk == thr) & (ab < jcut))

    _flash_init(m_ref, l_ref, acc_ref)
    lo = lax.broadcasted_iota(I32, (LANES, 1), 0) < HEAD_DIM
    qpairs = []
    for pr in range(H // 2):
        qp = dqt_ref[pr * LANES:(pr + 1) * LANES, :]
        zero = jnp.zeros_like(qp)
        qpairs.append(jnp.concatenate([jnp.where(lo, qp, zero), jnp.where(lo, zero, qp)], axis=1))

    def scores(r0, nkeys, mask, bias):
        mask2 = jnp.concatenate([mask, mask], axis=1)
        out = []
        for pr in range(H // 2):
            s = _dot(dk_ref[pl.ds(r0, nkeys), pr * LANES:(pr + 1) * LANES], qpairs[pr])
            if bias is not None:
                s = s + jnp.concatenate([bias[2 * pr], bias[2 * pr + 1]], axis=1)
            out.append(jnp.where(mask2, s, -jnp.inf))
        return out

    def update(pr, s, r0, nkeys):
        m_old = m_ref[pr]
        m_new = jnp.maximum(m_old, jnp.max(s, axis=0, keepdims=True))
        alpha = jnp.exp2(m_old - m_new)
        p = jnp.exp2(s - m_new)
        l_ref[pr] = alpha * l_ref[pr] + jnp.sum(p, axis=0, keepdims=True)
        m_ref[pr] = m_new
        vt = dvt_ref[pr * LANES:(pr + 1) * LANES, pl.ds(r0, nkeys)]
        acc_ref[pr] = alpha * acc_ref[pr] + _dot(vt, p.astype(BF16))

    far_end = t0 - Q_BLOCK
    nfar = jnp.maximum(far_end + KEY_TILE - 1, 0) // KEY_TILE
    far_row = lambda kt: pl.multiple_of(DSA_PADF + kt * KEY_TILE, LANES)

    def score_far(kt, slot):
        ab = kt * KEY_TILE + key512
        mask = chosen(ktile(jnp.minimum(kt, ntile - 1)), ab) & (ab < far_end)
        for pr, s in enumerate(scores(far_row(kt), KEY_TILE, mask, None)):
            sbuf_ref[slot, pr] = s

    def update_far(kt, slot):
        for pr in range(H // 2):
            update(pr, sbuf_ref[slot, pr], far_row(kt), KEY_TILE)

    score_far(0, 0)

    def far2(k2, carry):
        kt = 2 * k2
        score_far(kt + 1, 1)
        update_far(kt, 0)
        score_far(kt + 2, 0)
        update_far(kt + 1, 1)
        return carry

    lax.fori_loop(0, (nfar + 1) // 2, far2, 0)

    abn = lax.broadcasted_iota(I32, (NEAR, 1), 0) + t0 - Q_BLOCK
    kkn = keys_ref[pl.ds(pl.multiple_of(t0, LANES), NEAR), :]
    rn = pl.multiple_of(t0 + DSA_PADF - Q_BLOCK, LANES)
    for pr, s in enumerate(scores(rn, NEAR, chosen(kkn, abn) & (abn <= tq) & (abn >= 0), tbt_ref)):
        update(pr, s, rn, NEAR)

    for pr in range(H // 2):
        o = acc_ref[pr] / jnp.maximum(l_ref[pr], 1e-30)
        ot_ref[pr * LANES:(pr + 1) * LANES, :] = jnp.where(lo, o[:, 0:T], o[:, T:2 * T]).astype(BF16)


def _dsa(dqt, dkp, dvtp, iqt, ikt, iwt, tbt, ltri, S):
    KEEP = min(DSA_TOPK_MAX, S // 4)
    H = DSA_HEADS
    T = Q_BLOCK
    once = pl.Buffered(1)
    full = lambda a: pl.BlockSpec(a.shape, lambda i: (0,) * a.ndim, pipeline_mode=once)
    return pl.pallas_call(
        functools.partial(_dsa_body, S=S, KEEP=KEEP),
        grid=(S // T,),
        in_specs=[pl.BlockSpec((DSA_W, T), lambda i: (0, i)),
                  full(dkp), full(dvtp),
                  pl.BlockSpec((IDX_HEADS * IDX_DIM, T), lambda i: (0, i)),
                  full(ikt),
                  pl.BlockSpec((IDX_HEADS, T), lambda i: (0, i)),
                  full(tbt), full(ltri)],
        out_specs=pl.BlockSpec((DSA_W, T), lambda i: (0, i)),
        out_shape=jax.ShapeDtypeStruct((DSA_W, S), BF16),
        scratch_shapes=[pltpu.VMEM((Q_BLOCK + S + KEY_TILE, T), I32),
                        pltpu.VMEM((H // 2, 1, 2 * T), F32),
                        pltpu.VMEM((H // 2, 1, 2 * T), F32),
                        pltpu.VMEM((H // 2, LANES, 2 * T), F32),
                        pltpu.VMEM((2, H // 2, KEY_TILE, 2 * T), F32)],
        compiler_params=_cparams(("arbitrary",)),
        name="dsa",
    )(dqt, dkp, dvtp, iqt, ikt, iwt, tbt, ltri)


def _split_bf16(a):
    hi = a.astype(BF16)
    return hi, (a - hi.astype(F32)).astype(BF16)


def _merge_body(x_ref, oa_ref, ob_ref, gab_ref, wa_ref, wb_ref, wo_ref, gffn_ref, wr_ref, br_ref,
                x1_ref, h2_ref, ridx_ref, rw_ref):
    a = _dot(oa_ref[...], wa_ref[...])
    b = _dot(ob_ref[...], wb_ref[...])
    gab = gab_ref[...]
    merged = gab[:, :D_MODEL] * a + gab[:, D_MODEL:] * b
    x1 = x_ref[...] + _dot(merged.astype(BF16), wo_ref[...])
    x1_ref[...] = x1
    h2 = x1 * lax.rsqrt(jnp.mean(x1 * x1, axis=-1, keepdims=True) + EPS) * gffn_ref[...]
    h2_ref[...] = h2
    hh, hl = _split_bf16(h2)
    wh, wl = _split_bf16(wr_ref[...])
    logits = _dot(hh, wh) + _dot(hh, wl) + _dot(hl, wh) + br_ref[...]
    lane = lax.broadcasted_iota(I32, (1, LANES), 1)
    idx_out = jnp.zeros(logits.shape, I32)
    val_out = jnp.full(logits.shape, -jnp.inf, F32)
    for k in range(TOP_K):
        mx = jnp.max(logits, axis=-1, keepdims=True)
        first = jnp.min(jnp.where(logits == mx, lane, LANES), axis=-1, keepdims=True)
        idx_out = jnp.where(lane == k, first, idx_out)
        val_out = jnp.where(lane == k, mx, val_out)
        logits = jnp.where(lane == first, -jnp.inf, logits)
    e = jnp.exp(val_out - jnp.max(val_out, axis=-1, keepdims=True))
    ridx_ref[...] = idx_out
    rw_ref[...] = e / jnp.sum(e, axis=-1, keepdims=True)


def _merge(x2, oa, ob, gab, wa, wb, wo, gffn, wr, br, tm=512):
    S = x2.shape[0]
    row = lambda n: pl.BlockSpec((tm, n), lambda i: (i, 0))
    full = lambda a: pl.BlockSpec(a.shape, lambda i: (0,) * a.ndim)
    return pl.pallas_call(
        _merge_body,
        grid=(S // tm,),
        in_specs=[row(D_MODEL), row(NSA_W), row(DSA_W), row(2 * D_MODEL),
                  full(wa), full(wb), full(wo), full(gffn), full(wr), full(br)],
        out_specs=[row(D_MODEL), row(D_MODEL), row(LANES), row(LANES)],
        out_shape=[jax.ShapeDtypeStruct((S, D_MODEL), F32), jax.ShapeDtypeStruct((S, D_MODEL), F32),
                   jax.ShapeDtypeStruct((S, LANES), I32), jax.ShapeDtypeStruct((S, LANES), F32)],
        compiler_params=_cparams(("arbitrary",)),
        name="merge",
    )(x2, oa, ob, gab, wa, wb, wo, gffn, wr, br)


def _rank_body(ridx_ref, rank_ref, cnt_ref, carry_ref):
    @pl.when(pl.program_id(0) == 0)
    def _():
        carry_ref[...] = jnp.zeros(carry_ref.shape, F32)

    ridx = ridx_ref[...]
    tm = ridx.shape[0]
    lane = lax.broadcasted_iota(I32, (1, LANES), 1)
    hits = [lane == ridx[:, k:k + 1] for k in range(TOP_K)]
    member = jnp.zeros((tm, LANES), F32)
    for hk in hits:
        member = member + jnp.where(hk, 1.0, 0.0)
    ri = lax.broadcasted_iota(I32, (tm, tm), 0)
    ci = lax.broadcasted_iota(I32, (tm, tm), 1)
    lower = jnp.where(ci < ri, 1.0, 0.0).astype(BF16)
    before = _dot(lower, member.astype(BF16)) + carry_ref[0:1, :]
    out = jnp.zeros((tm, LANES), I32)
    for k, hk in enumerate(hits):
        rk = jnp.sum(jnp.where(hk, before, 0.0), axis=-1, keepdims=True)
        out = jnp.where(lane == k, rk.astype(I32), out)
    rank_ref[...] = out
    total = carry_ref[0:1, :] + jnp.sum(member, axis=0, keepdims=True)
    carry_ref[...] = jnp.broadcast_to(total, carry_ref.shape)
    cnt_ref[...] = jnp.broadcast_to(total, cnt_ref.shape).astype(I32)


def _rank(ridx, tm=512):
    S = ridx.shape[0]
    return pl.pallas_call(
        _rank_body,
        grid=(S // tm,),
        in_specs=[pl.BlockSpec((tm, LANES), lambda i: (i, 0))],
        out_specs=[pl.BlockSpec((tm, LANES), lambda i: (i, 0)), pl.BlockSpec((8, LANES), lambda i: (0, 0))],
        out_shape=[jax.ShapeDtypeStruct((S, LANES), I32), jax.ShapeDtypeStruct((8, LANES), I32)],
        scratch_shapes=[pltpu.VMEM((8, LANES), F32)],
        compiler_params=_cparams(("arbitrary",)),
        name="rank",
    )(ridx)


def _dispatch_body(dest_ref, h_ref, zero_ref, xe_ref, sem):
    del zero_ref
    i = pl.program_id(0)
    tm = h_ref.shape[0]

    def start(r, c):
        for k in range(TOP_K):
            d = dest_ref[(i * tm + r) * TOP_K + k]
            pltpu.make_async_copy(h_ref.at[pl.ds(r, 1)], xe_ref.at[pl.ds(d, 1)],
                                  sem).start(priority=k % 2)
        return c

    lax.fori_loop(0, tm, start, 0, unroll=ROW_DMA_UNROLL)
    for _ in range(TOP_K):
        pltpu.make_async_copy(h_ref, xe_ref.at[pl.ds(0, tm)], sem).wait()


def _dispatch(dest, h2, zeros_p, tm=128):
    S = h2.shape[0]
    return pl.pallas_call(
        _dispatch_body,
        grid_spec=pltpu.PrefetchScalarGridSpec(
            num_scalar_prefetch=1,
            grid=(S // tm,),
            in_specs=[pl.BlockSpec((tm, D_MODEL), lambda i, d: (i, 0)),
                      pl.BlockSpec(memory_space=pl.ANY)],
            out_specs=pl.BlockSpec(memory_space=pl.ANY),
            scratch_shapes=[pltpu.SemaphoreType.DMA(())]),
        out_shape=jax.ShapeDtypeStruct(zeros_p.shape, zeros_p.dtype),
        input_output_aliases={2: 0},
        compiler_params=_cparams(("arbitrary",)),
        name="dispatch",
    )(dest, h2, zeros_p)


def _experts_body(ce_ref, nu_ref, x_ref, w1_ref, b1_ref, w2_ref, b2_ref, y_ref, w1b_ref, w2b_ref):
    c = pl.program_id(0)

    @pl.when((c == 0) | (ce_ref[c] != ce_ref[jnp.maximum(c - 1, 0)]))
    def _():
        w1b_ref[...] = w1_ref[...].astype(BF16)
        w2b_ref[...] = w2_ref[...].astype(BF16)

    @pl.when(c < nu_ref[0])
    def _():
        gu = _dot(x_ref[...].astype(BF16), w1b_ref[...]) + b1_ref[...]
        gate = jnp.minimum(gu[:, :D_FF], SWIGLU_LIMIT)
        lin = jnp.clip(gu[:, D_FF:], -SWIGLU_LIMIT, SWIGLU_LIMIT)
        act = (lin + 1.0) * gate * jax.nn.sigmoid(SWIGLU_ALPHA * gate)
        y_ref[...] = _dot(act.astype(BF16), w2b_ref[...]) + b2_ref[...]

    @pl.when(c >= nu_ref[0])
    def _():
        y_ref[...] = jnp.zeros(y_ref.shape, F32)


def _experts(chunk_e, n_used, xe, w1, b1, w2, b2):
    P = xe.shape[0]
    nch = P // MOE_ROWS
    return pl.pallas_call(
        _experts_body,
        grid_spec=pltpu.PrefetchScalarGridSpec(
            num_scalar_prefetch=2,
            grid=(nch,),
            in_specs=[pl.BlockSpec((MOE_ROWS, D_MODEL), lambda c, ce, nu: (c, 0)),
                      pl.BlockSpec((None, D_MODEL, 2 * D_FF), lambda c, ce, nu: (ce[c], 0, 0)),
                      pl.BlockSpec((None, 1, 2 * D_FF), lambda c, ce, nu: (ce[c], 0, 0)),
                      pl.BlockSpec((None, D_FF, D_MODEL), lambda c, ce, nu: (ce[c], 0, 0)),
                      pl.BlockSpec((None, 1, D_MODEL), lambda c, ce, nu: (ce[c], 0, 0))],
            out_specs=pl.BlockSpec((MOE_ROWS, D_MODEL), lambda c, ce, nu: (c, 0)),
            scratch_shapes=[pltpu.VMEM((D_MODEL, 2 * D_FF), BF16), pltpu.VMEM((D_FF, D_MODEL), BF16)]),
        out_shape=jax.ShapeDtypeStruct((P, D_MODEL), F32),
        compiler_params=_cparams(("arbitrary",)),
        name="experts",
    )(chunk_e, n_used, xe, w1, b1, w2, b2)


def _final_body(dest_ref, x1_ref, rw_ref, p_ref, gple_ref, wg_ref, wp_ref, y_ref, o_ref, rows_ref, sems):
    i = pl.program_id(0)
    tm = x1_ref.shape[0]

    def gather(blk, slot):
        def start(r, c):
            for k in range(TOP_K):
                d = dest_ref[(blk * tm + r) * TOP_K + k]
                pltpu.make_async_copy(y_ref.at[pl.ds(d, 1)], rows_ref.at[slot, k, pl.ds(r, 1)],
                                      sems.at[slot]).start(priority=k % 2)
            return c
        lax.fori_loop(0, tm, start, 0, unroll=ROW_DMA_UNROLL)

    @pl.when(i == 0)
    def _():
        gather(0, 0)

    @pl.when(i + 1 < pl.num_programs(0))
    def _():
        gather(i + 1, (i + 1) % 2)

    slot = i % 2
    for k in range(TOP_K):
        pltpu.make_async_copy(y_ref.at[pl.ds(0, tm)], rows_ref.at[slot, k], sems.at[slot]).wait()
    rw = rw_ref[...]
    x2 = x1_ref[...]
    for k in range(TOP_K):
        x2 = x2 + rw[:, k:k + 1] * rows_ref[slot, k]
    hn = x2 * lax.rsqrt(jnp.mean(x2 * x2, axis=-1, keepdims=True) + EPS) * gple_ref[...]
    gate = jax.nn.sigmoid(_dot(hn.astype(BF16), wg_ref[...]))
    o_ref[...] = x2 + gate * _dot(p_ref[...].astype(BF16), wp_ref[...])


def _final(dest, x1, rw, p2, gple, wg, wp, yexp, tm=128):
    S = x1.shape[0]
    row = lambda n: pl.BlockSpec((tm, n), lambda i, d: (i, 0))
    full = lambda a: pl.BlockSpec(a.shape, lambda i, d: (0,) * a.ndim)
    return pl.pallas_call(
        _final_body,
        grid_spec=pltpu.PrefetchScalarGridSpec(
            num_scalar_prefetch=1,
            grid=(S // tm,),
            in_specs=[row(D_MODEL), row(LANES), row(PLE_DIM), full(gple), full(wg), full(wp),
                      pl.BlockSpec(memory_space=pl.ANY)],
            out_specs=row(D_MODEL),
            scratch_shapes=[pltpu.VMEM((2, TOP_K, tm, D_MODEL), F32), pltpu.SemaphoreType.DMA((2,))]),
        out_shape=jax.ShapeDtypeStruct((S, D_MODEL), F32),
        compiler_params=_cparams(("arbitrary",)),
        name="final",
    )(dest, x1, rw, p2, gple, wg, wp, yexp)


def _distance_bias(rel_cols):
    return (rel_cols[_rel_bucket_table(NEAR)] - rel_cols[REL_BUCKETS - 1][None, :]).T * LOG2E


def _near_bias_t(rel_cols):
    h = rel_cols.shape[1]
    g = jnp.pad(_distance_bias(rel_cols), ((0, 0), (Q_BLOCK, Q_BLOCK)))
    period = 2 * NEAR
    skew = jnp.tile(g, (1, NEAR))[:, :NEAR * (period - 1)].reshape(h, NEAR, period - 1)
    return skew[:, :, NEAR:NEAR + Q_BLOCK]


def _cmp_bias_t(rel_cols):
    g = jnp.pad(_distance_bias(rel_cols), ((0, 0), (NEAR, Q_BLOCK)))
    rows = []
    for j in range(4):
        for k in range(4):
            c0 = -NSA_CMP_STRIDE * j - (NSA_CMP_LEN - 1) + NSA_SLC_BLOCK * (k - 1)
            rows.append(g[:, c0 + NEAR:c0 + NEAR + Q_BLOCK])
    return jnp.stack(rows, axis=1)


def kernel(x, p, w_in, g_mix, g_q_nsa, g_k_nsa, g_q_dsa, g_k_dsa, pe_ck, w_ck1, w_ck2, pe_cv, w_cv1, w_cv2,
           w_branch_a, w_branch_b, w_out, rel_bias, g_ffn, w_router, b_router, w_e1, b_e1, w_e2, b_e2,
           g_ple, w_ple, w_ple_gate):
    B, S, D = x.shape
    assert B == 1 and D == D_MODEL and S % KEY_TILE == 0 and w_in.shape[0] == 1
    G, R = NSA_KV_GROUPS, NSA_GROUP_SIZE
    x2 = x.reshape(S, D)

    w0 = w_in[0]
    w_nat = jnp.concatenate([w0[:, 512:768], w0[:, 768:896], w0[:, 1024:1152], w0[:, 1816:2328], w0[:, 3136:5184],
                             w0[:, 1280:1304], w0[:, 3096:3136], jnp.zeros((D, LANES - 64), F32)], axis=1).astype(BF16)
    w_tr = jnp.concatenate([w0[:, 0:512], w0[:, 1304:1816], w0[:, 2328:2840], w0[:, 2840:3096],
                            w0[:, 896:1024], w0[:, 1152:1280]], axis=1).T.astype(BF16)
    two = lambda g: jnp.tile(g.reshape(1, HEAD_DIM), (1, 2))
    qscale = HEAD_DIM ** -0.5 * LOG2E
    gains = jnp.concatenate([two(g_k_nsa[0]), two(g_k_dsa[0])], axis=0)
    gcol = jnp.stack([g_q_nsa[0] * qscale, g_q_dsa[0] * qscale]).reshape(2, HEAD_DIM, 1)
    (cmpraw, kslc, kwin, dk, gab, small, nqt, dqt, dvt, iqt, vst2, vwt2) = _inproj(
        x2, g_mix[0].reshape(1, D), w_nat, w_tr, gains, gcol)

    NS = S // NSA_SLC_BLOCK
    nchunk = S // NSA_CMP_STRIDE
    xc = cmpraw.reshape(nchunk, NSA_CMP_STRIDE, 2 * G, HEAD_DIM).transpose(2, 0, 1, 3).reshape(
        2 * G, nchunk, NSA_CMP_STRIDE * HEAD_DIM)
    planes = lambda a: a.reshape(2 * G, NS, 4, -1).transpose(0, 2, 1, 3).reshape(2 * G, nchunk, -1)
    xa = planes(xc)
    xb = planes(jnp.roll(xc, -1, axis=1))
    half = NSA_CMP_STRIDE * HEAD_DIM
    w1 = jnp.stack([w_ck1[0], w_cv1[0]]).reshape(2, 2 * half, HEAD_DIM)
    pe = jnp.stack([pe_ck[0], pe_cv[0]]).reshape(2, 2 * half, 1)
    w2 = jnp.stack([w_ck2[0], w_cv2[0]]).astype(BF16)
    kvc = _compress(xa, xb, w1[:, :half].astype(BF16), w1[:, half:].astype(BF16), w1, pe, w2,
                    g_k_nsa[0].reshape(1, HEAD_DIM))

    rel_nsa = rel_bias[:, :NSA_HEADS]
    rel_dsa = rel_bias[:, NSA_HEADS:]
    oct, selt = _cmp(nqt, kvc[:G], kvc[G:].transpose(0, 2, 1), _cmp_bias_t(rel_nsa), S)
    kgroups = lambda a: jnp.pad(a.reshape(S, G, HEAD_DIM).transpose(1, 0, 2),
                                ((0, 0), (NSA_PADF, TAIL_PAD), (0, 0)))
    vgroups = lambda a: jnp.pad(a.reshape(G, HEAD_DIM, S), ((0, 0), (0, 0), (NSA_PADF, TAIL_PAD)))
    glt = small[:, SMALL_GATE:SMALL_GATE + 3 * NSA_HEADS].reshape(S, G, 3 * R).transpose(1, 2, 0)
    glt = jnp.pad(glt, ((0, 0), (0, 16 - 3 * R), (0, 0)))
    blk_lane = (np.arange(S) // NSA_SLC_BLOCK) % LANES
    onehot = jnp.asarray(np.eye(LANES, dtype=np.float32)[blk_lane] * MASK_BIG, BF16)
    kaug = jnp.concatenate([kslc.reshape(S, G, HEAD_DIM).transpose(1, 0, 2),
                            jnp.broadcast_to(onehot, (G, S, LANES)),
                            jnp.zeros((G, S, KAUG_W - HEAD_DIM - LANES), BF16)], axis=-1)
    kaug = jnp.pad(kaug, ((0, 0), (NSA_PADF, TAIL_PAD), (0, 0)))
    o_at = _slcwin(nqt, kaug, vgroups(vst2), kgroups(kwin), vgroups(vwt2), selt,
                   _near_bias_t(rel_nsa), oct, glt, S)

    dkp = jnp.pad(dk, ((DSA_PADF, TAIL_PAD), (0, 0)))
    dvtp = jnp.pad(dvt, ((0, 0), (DSA_PADF, TAIL_PAD)))
    ikt = jnp.pad(small[:, SMALL_IK:SMALL_IK + IDX_DIM].T.astype(BF16), ((0, 0), (0, 2 * KEY_TILE)))
    iwt = small[:, SMALL_IW:SMALL_IW + IDX_HEADS].T
    ltri = jnp.asarray(np.tril(np.ones((KEY_TILE, KEY_TILE), np.float32), -1), BF16)
    o_bt = _dsa(dqt, dkp, dvtp, iqt, ikt, iwt, _near_bias_t(rel_dsa), ltri, S)

    wr = jnp.pad(w_router[0], ((0, 0), (0, LANES - N_EXPERTS)))
    br = jnp.pad(b_router[0].reshape(1, N_EXPERTS), ((0, 0), (0, LANES - N_EXPERTS)), constant_values=-1e30)
    x1, h2, ridx, rw = _merge(x2, o_at.T, o_bt.T, gab, w_branch_a[0].astype(BF16), w_branch_b[0].astype(BF16),
                              w_out[0].astype(BF16), g_ffn[0].reshape(1, D), wr, br)

    rank, cnt = _rank(ridx)
    counts = cnt[0, :N_EXPERTS]
    padded = (counts + MOE_ROWS - 1) // MOE_ROWS * MOE_ROWS
    pad_end = jnp.cumsum(padded)
    pad_start = pad_end - padded
    eidx = ridx[:, :TOP_K]
    dest = (jnp.sum(jnp.where(eidx[..., None] == jnp.arange(N_EXPERTS), pad_start, 0), axis=-1)
            + rank[:, :TOP_K]).reshape(-1).astype(I32)
    n_chunks = -(-(S * TOP_K) // MOE_ROWS) + N_EXPERTS
    chunk_start = jnp.arange(n_chunks, dtype=I32) * MOE_ROWS
    chunk_e = jnp.minimum(jnp.sum(chunk_start[:, None] >= pad_end[None, :], axis=-1), N_EXPERTS - 1).astype(I32)
    n_used = (pad_end[-1] // MOE_ROWS).astype(I32).reshape(1)
    xe = _dispatch(dest, h2, jnp.zeros((n_chunks * MOE_ROWS, D), F32))
    yexp = _experts(chunk_e, n_used, xe, w_e1[0], b_e1[0].reshape(N_EXPERTS, 1, 2 * D_FF),
                    w_e2[0], b_e2[0].reshape(N_EXPERTS, 1, D))

    out = _final(dest, x1, rw, p[0].reshape(S, PLE_DIM), g_ple[0].reshape(1, D),
                 w_ple_gate[0].astype(BF16), w_ple[0].astype(BF16), yexp)
    return out.reshape(B, S, D)
```

```python
import functools
import math

import numpy as np
import jax
import jax.numpy as jnp
from jax import lax
from jax.experimental import pallas as pl
from jax.experimental.pallas import tpu as pltpu

F32 = jnp.float32
BF16 = jnp.bfloat16
I32 = jnp.int32

D_MODEL = 1024
PLE_DIM = 256
HEAD_DIM = 64
NSA_HEADS = 8
NSA_KV_GROUPS = 2
NSA_GROUP_SIZE = NSA_HEADS // NSA_KV_GROUPS
NSA_CMP_LEN = 32
NSA_CMP_STRIDE = 16
NSA_SLC_BLOCK = 64
NSA_SLC_TOPK = 16
NSA_LOCAL_BLOCKS = 2
NSA_WINDOW = 512
DSA_HEADS = 8
IDX_HEADS = 8
IDX_DIM = 32
DSA_TOPK_MAX = 256
Q_BLOCK = 128
REL_BUCKETS = 32
REL_MAX_EXACT = 16
REL_MAX_DIST = 128
N_EXPERTS = 32
TOP_K = 4
D_FF = 1024
SWIGLU_LIMIT = 7.0
SWIGLU_ALPHA = 1.702
MOE_ROWS = 512
EPS = 1e-6
LOG2E = math.log2(math.e)

NSA_W = NSA_HEADS * HEAD_DIM
DSA_W = DSA_HEADS * HEAD_DIM

LANES = 128
SUBLANES = 8
KEY_TILE = 512
NEAR = 2 * Q_BLOCK
NSA_PADF = NSA_WINDOW
DSA_PADF = Q_BLOCK
KAUG_W = 2 * LANES
MASK_BIG = 2.0 ** 100
TAIL_PAD = 3 * KEY_TILE
VMEM_LIMIT = 56 * 1024 * 1024
ROW_DMA_UNROLL = 8

NAT_CMP, NAT_KSLC, NAT_KWIN, NAT_DK, NAT_GAB, NAT_SMALL = 0, 256, 384, 512, 1024, 3072
TR_NQ, TR_DQ, TR_DV, TR_IQ, TR_VSLC, TR_VWIN = 0, 512, 1024, 1536, 1792, 1920
SMALL_GATE, SMALL_IK, SMALL_IW = 0, 24, 56


def _rel_bucket_table(n):
    d = np.arange(n)
    nf = np.maximum(d, 1).astype(np.float64)
    large = REL_MAX_EXACT + (np.log(nf / REL_MAX_EXACT) / math.log(REL_MAX_DIST / REL_MAX_EXACT)
                             * (REL_BUCKETS - REL_MAX_EXACT)).astype(np.int64)
    large = np.minimum(large, REL_BUCKETS - 1)
    return np.where(d < REL_MAX_EXACT, d, large)


def _cparams(sem, vmem=VMEM_LIMIT):
    return pltpu.CompilerParams(dimension_semantics=sem, vmem_limit_bytes=vmem)


def _dot(a, b):
    return jnp.dot(a, b, preferred_element_type=F32)


def _dot_tn(a, b):
    return lax.dot_general(a, b, (((0,), (0,)), ((), ())), preferred_element_type=F32)


def _inproj_body(x_ref, gmix_ref, w_ref, wt_ref, gains_ref, gcol_ref,
                 cmp_ref, kslc_ref, kwin_ref, dk_ref, gab_ref, small_ref,
                 nqt_ref, dqt_ref, dvt_ref, iqt_ref, vst_ref, vwt_ref):
    x = x_ref[...]
    h = x * lax.rsqrt(jnp.mean(x * x, axis=-1, keepdims=True) + EPS) * gmix_ref[...]
    hb = h.astype(BF16)
    tm = hb.shape[0]
    lo = lax.broadcasted_iota(I32, (1, LANES), 1) < HEAD_DIM

    def mm(off, n):
        return _dot(hb, w_ref[:, off:off + n])

    def headnorm(y, g):
        y2 = y * y
        s_lo = jnp.sum(jnp.where(lo, y2, 0.0), axis=-1, keepdims=True)
        s_hi = jnp.sum(jnp.where(lo, 0.0, y2), axis=-1, keepdims=True)
        r = jnp.where(lo, lax.rsqrt(s_lo * (1.0 / HEAD_DIM) + EPS), lax.rsqrt(s_hi * (1.0 / HEAD_DIM) + EPS))
        return y * r * g

    def headnorm_t(y, g):
        y3 = y.reshape(y.shape[0] // HEAD_DIM, HEAD_DIM, tm)
        r = lax.rsqrt(jnp.mean(y3 * y3, axis=1, keepdims=True) + EPS)
        return (y3 * r * g).reshape(y.shape)

    g_kn, g_kd = gains_ref[0:1, :], gains_ref[1:2, :]
    cmp_ref[...] = mm(NAT_CMP, 256).astype(BF16)
    kslc_ref[...] = headnorm(mm(NAT_KSLC, LANES), g_kn).astype(BF16)
    kwin_ref[...] = headnorm(mm(NAT_KWIN, LANES), g_kn).astype(BF16)
    for c in range(DSA_W // LANES):
        dk_ref[:, c * LANES:(c + 1) * LANES] = headnorm(mm(NAT_DK + c * LANES, LANES), g_kd).astype(BF16)
    gab_ref[...] = jax.nn.sigmoid(mm(NAT_GAB, 2 * D_MODEL))
    small_ref[...] = mm(NAT_SMALL, LANES)

    def mm_t(off, n):
        return lax.dot_general(wt_ref[off:off + n, :], hb, (((1,), (1,)), ((), ())), preferred_element_type=F32)

    nqt_ref[...] = headnorm_t(mm_t(TR_NQ, NSA_W), gcol_ref[0]).astype(BF16)
    dqt_ref[...] = headnorm_t(mm_t(TR_DQ, DSA_W), gcol_ref[1]).astype(BF16)
    dvt_ref[...] = mm_t(TR_DV, DSA_W).astype(BF16)
    iqt_ref[...] = (mm_t(TR_IQ, IDX_HEADS * IDX_DIM) * (IDX_DIM ** -0.5)).astype(BF16)
    vst_ref[...] = mm_t(TR_VSLC, LANES).astype(BF16)
    vwt_ref[...] = mm_t(TR_VWIN, LANES).astype(BF16)


def _inproj(x2, g_mix, w_nat, w_tr, gains, gcol, tm=512):
    S = x2.shape[0]
    row = lambda n: pl.BlockSpec((tm, n), lambda i: (i, 0))
    col = lambda n: pl.BlockSpec((n, tm), lambda i: (0, i))
    full = lambda a: pl.BlockSpec(a.shape, lambda i: (0,) * a.ndim)
    nat = [(256, BF16), (LANES, BF16), (LANES, BF16), (DSA_W, BF16), (2 * D_MODEL, F32), (LANES, F32)]
    tr = [NSA_W, DSA_W, DSA_W, IDX_HEADS * IDX_DIM, LANES, LANES]
    return pl.pallas_call(
        _inproj_body,
        grid=(S // tm,),
        in_specs=[row(D_MODEL), full(g_mix), full(w_nat), full(w_tr), full(gains), full(gcol)],
        out_specs=[row(n) for n, _ in nat] + [col(n) for n in tr],
        out_shape=[jax.ShapeDtypeStruct((S, n), dt) for n, dt in nat]
                  + [jax.ShapeDtypeStruct((n, S), BF16) for n in tr],
        compiler_params=_cparams(("arbitrary",)),
        name="inproj",
    )(x2, g_mix, w_nat, w_tr, gains, gcol)


def _compress_body(xa_ref, xb_ref, w1a_ref, w1b_ref, w1f_ref, pe_ref, w2_ref, gk_ref, o_ref):
    is_k = pl.program_id(0) < NSA_KV_GROUPS
    ns = xa_ref.shape[0]
    pe_term = jnp.sum(pe_ref[...] * w1f_ref[...], axis=0, keepdims=True)
    second = _dot(xb_ref[...], w1b_ref[...])
    second = jnp.where(pl.program_id(1) == 3, pltpu.roll(second, ns - 1, axis=0), second)
    pre = _dot(xa_ref[...], w1a_ref[...]) + second + pe_term
    hdn = pre * jax.nn.sigmoid(pre)
    out = _dot(hdn.astype(BF16), w2_ref[...])
    normed = out * lax.rsqrt(jnp.mean(out * out, axis=-1, keepdims=True) + EPS) * gk_ref[...]
    o_ref[...] = jnp.where(is_k, normed, out).astype(BF16)


def _compress(xc, w1a, w1b, w1f, pe, w2, gk):
    n4, ns, kdim = xc.shape[0], xc.shape[1], xc.shape[2] // 4
    G = NSA_KV_GROUPS
    kv = lambda i: i // G
    return pl.pallas_call(
        _compress_body,
        grid=(n4, 4),
        in_specs=[pl.BlockSpec((None, ns, kdim), lambda i, j: (i, 0, j)),
                  pl.BlockSpec((None, ns, kdim), lambda i, j: (i, 0, (j + 1) % 4)),
                  pl.BlockSpec((None, kdim, HEAD_DIM), lambda i, j: (kv(i), 0, 0)),
                  pl.BlockSpec((None, kdim, HEAD_DIM), lambda i, j: (kv(i), 0, 0)),
                  pl.BlockSpec((None, 2 * kdim, HEAD_DIM), lambda i, j: (kv(i), 0, 0)),
                  pl.BlockSpec((None, 2 * kdim, 1), lambda i, j: (kv(i), 0, 0)),
                  pl.BlockSpec((None, HEAD_DIM, HEAD_DIM), lambda i, j: (kv(i), 0, 0)),
                  pl.BlockSpec((1, HEAD_DIM), lambda i, j: (0, 0))],
        out_specs=pl.BlockSpec((None, ns, HEAD_DIM), lambda i, j: (i, j, 0)),
        out_shape=jax.ShapeDtypeStruct((n4, 4 * ns, HEAD_DIM), BF16),
        compiler_params=_cparams(("arbitrary", "arbitrary")),
        name="compress",
    )(xc, xc, w1a, w1b, w1f, pe, w2, gk)


def _cmp_body(qt_ref, kc_ref, vct_ref, bt_ref, oct_ref, selt_ref, s_ref, imp_ref, *, NS):
    i = pl.program_id(1)

    @pl.when((pl.program_id(0) == 0) & (i == 0))
    def _():
        imp_ref[...] = jnp.zeros(imp_ref.shape, F32)

    imp_prev = imp_ref[...]
    ib = jnp.minimum(i, pl.num_programs(1) - 2)
    t0 = ib * Q_BLOCK
    T = Q_BLOCK
    R = NSA_GROUP_SIZE
    WIN = 2 * SUBLANES

    def step(ne):
        NE = 4 * ne
        row = lax.broadcasted_iota(I32, (NE, 1), 0)
        plane = row // ne
        nblk = row - plane * ne
        cend = NSA_SLC_BLOCK * nblk + NSA_CMP_STRIDE * plane + (NSA_CMP_LEN - 1)
        tq = t0 + lax.broadcasted_iota(I32, (1, T), 1)
        mask = cend <= tq
        kc = jnp.concatenate([kc_ref[j * NS:j * NS + ne, :] for j in range(4)], axis=0)
        vct = jnp.concatenate([vct_ref[:, j * NS:j * NS + ne] for j in range(4)], axis=1)
        wstart = jnp.clip((2 * ib - 2) // SUBLANES * SUBLANES, 0, ne - WIN)
        wrow = lax.broadcasted_iota(I32, (WIN, 1), 0)
        psum = jnp.zeros((NE, T), F32)
        for r in range(R):
            s_ref[r, 0:NE, :] = _dot(kc, qt_ref[r * HEAD_DIM:(r + 1) * HEAD_DIM, :])
        for r in range(R):
            bt = bt_ref[r]
            for j in range(4):
                add = jnp.zeros((WIN, T), F32)
                for k in range(4):
                    n = 2 * ib + 1 - k
                    add = add + jnp.where((wrow == n - wstart) & (n >= 0), bt[4 * j + k:4 * j + k + 1, :], 0.0)
                sl = pl.ds(pl.multiple_of(j * ne + wstart, SUBLANES), WIN)
                s_ref[r, sl, :] = s_ref[r, sl, :] + add
            sc = jnp.where(mask, s_ref[r, 0:NE, :], -jnp.inf)
            m = jnp.maximum(jnp.max(sc, axis=0, keepdims=True), -1e30)
            e = jnp.exp2(sc - m)
            pc = e / jnp.maximum(jnp.sum(e, axis=0, keepdims=True), 1e-30)
            oct_ref[r * HEAD_DIM:(r + 1) * HEAD_DIM, :] = _dot(vct, pc.astype(BF16))
            psum = psum + pc
        p0, p1, p2, p3 = (psum[j * ne:(j + 1) * ne, :] for j in range(4))
        p3s = jnp.where(lax.broadcasted_iota(I32, (ne, 1), 0) == 0, 0.0, pltpu.roll(p3, 1, axis=0))
        imp_ref[0:ne, :] = p0 + p1 + p2 + 0.5 * p3 + 0.5 * p3s
        if ne < NS:
            imp_ref[ne:NS, :] = jnp.zeros((NS - ne, T), F32)

        n_ar = lax.broadcasted_iota(I32, (NS, 1), 0)
        tqs = (i - 1) * Q_BLOCK + lax.broadcasted_iota(I32, (1, T), 1)
        cur = tqs // NSA_SLC_BLOCK
        valid = NSA_SLC_BLOCK * n_ar <= tqs
        forced = valid & ((n_ar == 0) | (n_ar > cur - NSA_LOCAL_BLOCKS))
        score = jnp.where(forced, 1e30, jnp.where(valid, imp_prev, -1.0))
        sel = jnp.zeros((NS, T), jnp.bool_)
        for _ in range(min(NSA_SLC_TOPK, NS)):
            mx = jnp.max(score, axis=0, keepdims=True)
            first = jnp.min(jnp.where(score == mx, n_ar, NS), axis=0, keepdims=True)
            pick = n_ar == first
            sel = sel | pick
            score = jnp.where(pick, -2.0, score)
        selb = jnp.where(sel, 1.0, 0.0).astype(BF16)
        if NS < LANES:
            selb = jnp.concatenate([selb, jnp.zeros((LANES - NS, T), BF16)], axis=0)
        for w in range(selt_ref.shape[0]):
            selt_ref[w] = selb[w * LANES:(w + 1) * LANES, :]

    quarter = NS // 4
    if quarter % WIN == 0:
        which = (2 * ib + 2 + quarter - 1) // quarter - 1
        for v in range(4):
            pl.when(which == v)(functools.partial(step, (v + 1) * quarter))
    else:
        step(NS)


def _cmp(nqt, kc, vct, bt, S):
    NS = S // NSA_SLC_BLOCK
    NCP = 4 * NS
    NW = max(NS // LANES, 1)
    G, R = NSA_KV_GROUPS, NSA_GROUP_SIZE
    nb = S // Q_BLOCK
    return pl.pallas_call(
        functools.partial(_cmp_body, NS=NS),
        grid=(G, nb + 1),
        in_specs=[pl.BlockSpec((R * HEAD_DIM, Q_BLOCK), lambda g, i: (g, jnp.minimum(i, nb - 1))),
                  pl.BlockSpec((None, NCP, HEAD_DIM), lambda g, i: (g, 0, 0)),
                  pl.BlockSpec((None, HEAD_DIM, NCP), lambda g, i: (g, 0, 0)),
                  pl.BlockSpec((R, 16, Q_BLOCK), lambda g, i: (g, 0, 0))],
        out_specs=[pl.BlockSpec((R * HEAD_DIM, Q_BLOCK), lambda g, i: (g, jnp.minimum(i, nb - 1))),
                   pl.BlockSpec((None, NW, LANES, Q_BLOCK), lambda g, i: (g, 0, 0, jnp.maximum(i - 1, 0)))],
        out_shape=[jax.ShapeDtypeStruct((NSA_W, S), F32),
                   jax.ShapeDtypeStruct((G, NW, LANES, S), BF16)],
        scratch_shapes=[pltpu.VMEM((R, NCP, Q_BLOCK), F32), pltpu.VMEM((NS, Q_BLOCK), F32)],
        compiler_params=_cparams(("arbitrary", "arbitrary")),
        name="cmp",
    )(nqt, kc, vct, bt)


def _flash_init(m_ref, l_ref, acc_ref):
    m_ref[...] = jnp.full(m_ref.shape, -1e30, F32)
    l_ref[...] = jnp.zeros(l_ref.shape, F32)
    acc_ref[...] = jnp.zeros(acc_ref.shape, F32)


def _flash_update(s, vt, m_ref, l_ref, acc_ref):
    m_old = m_ref[...]
    m_new = jnp.maximum(m_old, jnp.max(s, axis=0, keepdims=True))
    alpha = jnp.exp2(m_old - m_new)
    p = jnp.exp2(s - m_new)
    l_ref[...] = alpha * l_ref[...] + jnp.sum(p, axis=0, keepdims=True)
    acc_ref[...] = alpha * acc_ref[...] + _dot(vt, p.astype(BF16))
    m_ref[...] = m_new


def _slcwin_body(qt_ref, ks_ref, vst_ref, kw_ref, vwt_ref, selt_ref, tbt_ref, oct_ref, glt_ref, ot_ref,
                 ms_ref, ls_ref, as_ref, mw_ref, lw_ref, aw_ref, sbuf_ref, qaug_ref):
    i = pl.program_id(1)
    t0 = i * Q_BLOCK
    R = NSA_GROUP_SIZE
    T = Q_BLOCK
    q4t = jnp.concatenate([qt_ref[r * HEAD_DIM:(r + 1) * HEAD_DIM, :] for r in range(R)], axis=1)
    tq = lax.broadcasted_iota(I32, (1, T), 1)
    rep = lambda mk: jnp.concatenate([mk] * R, axis=1)
    tb4 = jnp.concatenate([tbt_ref[r] for r in range(R)], axis=1)

    def expand(w, base, nkeys):
        keyi = lax.broadcasted_iota(I32, (nkeys, LANES), 0)
        blk = lax.broadcasted_iota(I32, (nkeys, LANES), 1)
        e = jnp.where(blk == base + keyi // NSA_SLC_BLOCK, 1.0, 0.0).astype(BF16)
        return _dot(e, selt_ref[w])

    _flash_init(ms_ref, ls_ref, as_ref)
    _flash_init(mw_ref, lw_ref, aw_ref)
    n_win = selt_ref.shape[0]
    zpad = jnp.zeros((KAUG_W - HEAD_DIM - LANES, R * T), BF16)
    for w in range(n_win):
        qaug_ref[w] = jnp.concatenate([q4t, rep(selt_ref[w] - jnp.asarray(1.0, BF16)), zpad], axis=0)
    far_end = jnp.maximum(t0 - Q_BLOCK, 0) // KEY_TILE * KEY_TILE
    nfar = far_end // KEY_TILE
    tiles_per_window = LANES * NSA_SLC_BLOCK // KEY_TILE

    def score_far(kt, slot):
        r0 = pl.multiple_of(NSA_PADF + kt * KEY_TILE, LANES)
        w = jnp.minimum(kt // tiles_per_window, n_win - 1)
        sbuf_ref[slot] = _dot(ks_ref[pl.ds(r0, KEY_TILE), :], qaug_ref[w])

    def update_far(kt, slot):
        r0 = pl.multiple_of(NSA_PADF + kt * KEY_TILE, LANES)
        _flash_update(sbuf_ref[slot], vst_ref[:, pl.ds(r0, KEY_TILE)], ms_ref, ls_ref, as_ref)

    score_far(0, 0)

    def far2(k2, carry):
        kt = 2 * k2
        score_far(kt + 1, 1)
        update_far(kt, 0)
        score_far(kt + 2, 0)
        update_far(kt + 1, 1)
        return carry

    lax.fori_loop(0, nfar // 2, far2, 0)

    @pl.when(nfar % 2 == 1)
    def _():
        update_far(nfar - 1, 0)

    kq = lambda r0, n: _dot(ks_ref[pl.ds(r0, n), 0:HEAD_DIM], q4t)
    keyn = lax.broadcasted_iota(I32, (NEAR, 1), 0)
    near_ok = (keyn - Q_BLOCK <= tq) & (keyn + t0 - Q_BLOCK >= 0)
    rn = pl.multiple_of(t0 + NSA_PADF - Q_BLOCK, LANES)
    WF = NSA_WINDOW - Q_BLOCK
    rw = pl.multiple_of(t0 + NSA_PADF - NSA_WINDOW, LANES)
    keyw = lax.broadcasted_iota(I32, (WF, 1), 0)
    maskw = (keyw > tq) & (keyw + t0 - NSA_WINDOW >= 0)
    blocks = [jnp.maximum(2 * i - (NSA_WINDOW // NSA_SLC_BLOCK) + 2 * j, 0) for j in range(NSA_WINDOW // Q_BLOCK)]
    mexp = [expand(b // LANES, b % LANES, Q_BLOCK) for b in blocks]
    mask_sm = (jnp.concatenate(mexp[:-1], axis=0) > 0.5) & (keyw + t0 - NSA_WINDOW >= far_end)
    mask_sn = (jnp.concatenate([mexp[-1], expand(2 * i // LANES, 2 * i % LANES, Q_BLOCK)], axis=0) > 0.5) & near_ok
    s_sm = jnp.where(rep(mask_sm), kq(rw, WF), -jnp.inf)
    s_sn = jnp.where(rep(mask_sn), kq(rn, NEAR) + tb4, -jnp.inf)
    s_wf = jnp.where(rep(maskw), _dot(kw_ref[pl.ds(rw, WF), :], q4t), -jnp.inf)
    s_wn = jnp.where(rep(near_ok), _dot(kw_ref[pl.ds(rn, NEAR), :], q4t) + tb4, -jnp.inf)
    _flash_update(s_sm, vst_ref[:, pl.ds(rw, WF)], ms_ref, ls_ref, as_ref)
    _flash_update(s_sn, vst_ref[:, pl.ds(rn, NEAR)], ms_ref, ls_ref, as_ref)
    _flash_update(s_wf, vwt_ref[:, pl.ds(rw, WF)], mw_ref, lw_ref, aw_ref)
    _flash_update(s_wn, vwt_ref[:, pl.ds(rn, NEAR)], mw_ref, lw_ref, aw_ref)
    o_s = as_ref[...] / jnp.maximum(ls_ref[...], 1e-30)
    o_w = aw_ref[...] / jnp.maximum(lw_ref[...], 1e-30)

    gates = jax.nn.sigmoid(glt_ref[...])
    for r in range(R):
        hs = slice(r * HEAD_DIM, (r + 1) * HEAD_DIM)
        qs = slice(r * T, (r + 1) * T)
        o = (gates[3 * r:3 * r + 1, :] * oct_ref[hs, :]
             + gates[3 * r + 1:3 * r + 2, :] * o_s[:, qs]
             + gates[3 * r + 2:3 * r + 3, :] * o_w[:, qs])
        ot_ref[hs, :] = o.astype(BF16)


def _slcwin(nqt, ks, vst, kw, vwt, selt, tbt, oct, glt, S):
    G, R = NSA_KV_GROUPS, NSA_GROUP_SIZE
    spad = ks.shape[1]
    NW = selt.shape[1]
    once = pl.Buffered(1)
    kaspec = pl.BlockSpec((None, spad, KAUG_W), lambda g, i: (g, 0, 0), pipeline_mode=once)
    kspec = pl.BlockSpec((None, spad, HEAD_DIM), lambda g, i: (g, 0, 0), pipeline_mode=once)
    vspec = pl.BlockSpec((None, HEAD_DIM, spad), lambda g, i: (g, 0, 0), pipeline_mode=once)
    qspec = pl.BlockSpec((R * HEAD_DIM, Q_BLOCK), lambda g, i: (g, i))
    st = lambda n: pltpu.VMEM((n, R * Q_BLOCK), F32)
    return pl.pallas_call(
        _slcwin_body,
        grid=(G, S // Q_BLOCK),
        in_specs=[qspec, kaspec, vspec, kspec, vspec,
                  pl.BlockSpec((None, NW, LANES, Q_BLOCK), lambda g, i: (g, 0, 0, i)),
                  pl.BlockSpec((R, NEAR, Q_BLOCK), lambda g, i: (g, 0, 0)),
                  qspec,
                  pl.BlockSpec((None, 16, Q_BLOCK), lambda g, i: (g, 0, i))],
        out_specs=qspec,
        out_shape=jax.ShapeDtypeStruct((NSA_W, S), BF16),
        scratch_shapes=[st(1), st(1), st(HEAD_DIM), st(1), st(1), st(HEAD_DIM),
                        pltpu.VMEM((2, KEY_TILE, R * Q_BLOCK), F32),
                        pltpu.VMEM((NW, KAUG_W, R * Q_BLOCK), BF16)],
        compiler_params=_cparams(("arbitrary", "arbitrary")),
        name="slcwin",
    )(nqt, ks, vst, kw, vwt, selt, tbt, oct, glt)


INT_MIN = -2 ** 31
INT_MAX = 2 ** 31 - 1
MAX_PEEL_ROUNDS = 64
UNTESTED_BITS = 16
NEG_INF_KEY = int(np.array(-np.inf, np.float32).view(np.int32)) ^ 0x7FFFFFFF


def _dsa_body(dqt_ref, dk_ref, dvt_ref, iqt_ref, ikt_ref, iwt_ref, tbt_ref, ltri_ref, ot_ref,
              keys_ref, m_ref, l_ref, acc_ref, sbuf_ref, *, S, KEEP):
    i = pl.program_id(0)
    t0 = i * Q_BLOCK
    T = Q_BLOCK
    H = DSA_HEADS
    ntile = (t0 + Q_BLOCK + KEY_TILE - 1) // KEY_TILE
    tq = t0 + lax.broadcasted_iota(I32, (1, T), 1)
    key512 = lax.broadcasted_iota(I32, (KEY_TILE, 1), 0)
    ktile = lambda kt: keys_ref[pl.ds(pl.multiple_of(Q_BLOCK + kt * KEY_TILE, LANES), KEY_TILE), :]

    keys_ref[0:Q_BLOCK, :] = jnp.full((Q_BLOCK, T), NEG_INF_KEY, I32)
    iqt = iqt_ref[...]
    iwt = iwt_ref[...] * (IDX_HEADS ** -0.5)
    qis = [jnp.concatenate([iqt[(2 * j) * IDX_DIM:(2 * j + 1) * IDX_DIM, :],
                            iqt[(2 * j + 1) * IDX_DIM:(2 * j + 2) * IDX_DIM, :]], axis=1)
           for j in range(IDX_HEADS // 2)]
    wrow = [iwt[h:h + 1, :] for h in range(IDX_HEADS)]

    def idx_dots(kt, slot):
        ik = ikt_ref[:, pl.ds(pl.multiple_of(kt * KEY_TILE, KEY_TILE), KEY_TILE)]
        for j in range(IDX_HEADS // 2):
            sbuf_ref[slot, j] = _dot_tn(ik, qis[j])

    def idx_keys(kt, slot):
        c0 = pl.multiple_of(kt * KEY_TILE, KEY_TILE)
        sc = jnp.zeros((KEY_TILE, T), F32)
        for j in range(IDX_HEADS // 2):
            d = jnp.maximum(sbuf_ref[slot, j], 0.0)
            sc = sc + wrow[2 * j] * d[:, 0:T]
            sc = sc + wrow[2 * j + 1] * d[:, T:2 * T]
        sc = jnp.where(sc == 0.0, 0.0, sc)
        sc = jnp.where(c0 + key512 <= tq, sc, -jnp.inf)
        bits = pltpu.bitcast(sc, I32)
        keys_ref[pl.ds(pl.multiple_of(Q_BLOCK + c0, LANES), KEY_TILE), :] = jnp.where(
            bits < 0, bits ^ 0x7FFFFFFF, bits)

    idx_dots(0, 0)

    def score2(k2, carry):
        kt = 2 * k2
        idx_dots(kt + 1, 1)
        idx_keys(kt, 0)
        idx_dots(kt + 2, 0)
        idx_keys(kt + 1, 1)
        return carry

    lax.fori_loop(0, (ntile + 1) // 2, score2, 0)

    def count(pred):
        def body(kt, acc):
            hit = jnp.where(pred(ktile(kt), kt * KEY_TILE + key512), 1, 0)
            return acc + jnp.sum(hit.reshape(KEY_TILE // SUBLANES, SUBLANES, T), axis=0)
        acc = lax.fori_loop(0, ntile, body, jnp.zeros((SUBLANES, T), I32))
        return jnp.sum(acc, axis=0, keepdims=True)

    def min_at_least(lowest):
        def body(kt, acc):
            kk = ktile(kt)
            v = jnp.where(kk >= lowest, kk, INT_MAX)
            return jnp.minimum(acc, jnp.min(v.reshape(KEY_TILE // SUBLANES, SUBLANES, T), axis=0))
        acc = lax.fori_loop(0, ntile, body, jnp.full((SUBLANES, T), INT_MAX, I32))
        return jnp.min(acc, axis=0, keepdims=True)

    def any_open(st, over):
        _, excess, fin, _ = st
        return jnp.max(jnp.where((fin == 0) & (excess > over), 1, 0)) > 0

    def bisect(b, st):
        thr, excess, fin, need = st
        cand = thr + lax.shift_left(jnp.int32(1), 31 - b)
        n = count(lambda kk, ab: kk >= cand)
        ok = (n >= KEEP) & (fin == 0)
        return jnp.where(ok, cand, thr), jnp.where(ok, n - KEEP, excess), fin, need

    def peel(st):
        thr, excess, fin, need = st
        low = min_at_least(thr)
        n_low = count(lambda kk, ab: kk == low)
        active = (fin == 0) & (excess > 0)
        tied = active & (n_low > excess)
        drop = active & (n_low <= excess)
        return (jnp.where(tied, low, jnp.where(drop, low + 1, thr)), jnp.where(drop, excess - n_low, excess),
                jnp.where(tied, 1, fin), jnp.where(tied, n_low - excess, need))

    zero = jnp.zeros((1, T), I32)
    st = (jnp.full((1, T), INT_MIN, I32), KEY_TILE * ntile - KEEP + zero, zero, zero)
    st = peel(bisect(0, st))
    st = lax.fori_loop(1, UNTESTED_BITS, bisect, st)
    _, st = lax.while_loop(lambda c: (c[0] < 32) & any_open(c[1], 1),
                           lambda c: (c[0] + 1, bisect(c[0], c[1])), (jnp.int32(UNTESTED_BITS), st))
    _, st = lax.while_loop(lambda c: (c[0] < MAX_PEEL_ROUNDS) & any_open(c[1], 0),
                           lambda c: (c[0] + 1, peel(c[1])), (jnp.int32(0), st))
    thr, _, fin, need = st
    tie = (fin == 1) & (thr > NEG_INF_KEY)

    def tie_cut():
        needf = need.astype(F32)

        def body(kt, c):
            before, jmax = c
            is_tie = ktile(kt) == thr
            tied = jnp.where(is_tie, 1.0, 0.0)
            rank = before + _dot(ltri_ref[...], tied.astype(BF16))
            admit = is_tie & (rank < needf)
            jm = jnp.max(jnp.where(admit, kt * KEY_TILE + key512, -1), axis=0, keepdims=True)
            return before + jnp.sum(tied, axis=0, keepdims=True), jnp.maximum(jmax, jm)

        _, jmax = lax.fori_loop(0, ntile, body, (jnp.zeros((1, T), F32), jnp.full((1, T), -1, I32)))
        return jnp.where(tie, jmax + 1, S)

    jcut = lax.cond(jnp.max(jnp.where(tie, 1, 0)) > 0, tie_cut, lambda: jnp.full((1, T), S, I32))

    def chosen(kk, ab):
        return (kk > thr) | ((kk == thr) & (ab < jcut))

    _flash_init(m_ref, l_ref, acc_ref)
    lo = lax.broadcasted_iota(I32, (LANES, 1), 0) < HEAD_DIM
    qpairs = []
    for pr in range(H // 2):
        qp = dqt_ref[pr * LANES:(pr + 1) * LANES, :]
        zero = jnp.zeros_like(qp)
        qpairs.append(jnp.concatenate([jnp.where(lo, qp, zero), jnp.where(lo, zero, qp)], axis=1))

    def scores(r0, nkeys, mask, bias):
        mask2 = jnp.concatenate([mask, mask], axis=1)
        out = []
        for pr in range(H // 2):
            s = _dot(dk_ref[pl.ds(r0, nkeys), pr * LANES:(pr + 1) * LANES], qpairs[pr])
            if bias is not None:
                s = s + jnp.concatenate([bias[2 * pr], bias[2 * pr + 1]], axis=1)
            out.append(jnp.where(mask2, s, -jnp.inf))
        return out

    def update(pr, s, r0, nkeys):
        m_old = m_ref[pr]
        m_new = jnp.maximum(m_old, jnp.max(s, axis=0, keepdims=True))
        alpha = jnp.exp2(m_old - m_new)
        p = jnp.exp2(s - m_new)
        l_ref[pr] = alpha * l_ref[pr] + jnp.sum(p, axis=0, keepdims=True)
        m_ref[pr] = m_new
        vt = dvt_ref[pr * LANES:(pr + 1) * LANES, pl.ds(r0, nkeys)]
        acc_ref[pr] = alpha * acc_ref[pr] + _dot(vt, p.astype(BF16))

    far_end = t0 - Q_BLOCK
    nfar = jnp.maximum(far_end + KEY_TILE - 1, 0) // KEY_TILE
    far_row = lambda kt: pl.multiple_of(DSA_PADF + kt * KEY_TILE, LANES)

    def score_far(kt, slot):
        ab = kt * KEY_TILE + key512
        mask = chosen(ktile(jnp.minimum(kt, ntile - 1)), ab) & (ab < far_end)
        for pr, s in enumerate(scores(far_row(kt), KEY_TILE, mask, None)):
            sbuf_ref[slot, pr] = s

    def update_far(kt, slot):
        for pr in range(H // 2):
            update(pr, sbuf_ref[slot, pr], far_row(kt), KEY_TILE)

    score_far(0, 0)

    def far2(k2, carry):
        kt = 2 * k2
        score_far(kt + 1, 1)
        update_far(kt, 0)
        score_far(kt + 2, 0)
        update_far(kt + 1, 1)
        return carry

    lax.fori_loop(0, (nfar + 1) // 2, far2, 0)

    abn = lax.broadcasted_iota(I32, (NEAR, 1), 0) + t0 - Q_BLOCK
    kkn = keys_ref[pl.ds(pl.multiple_of(t0, LANES), NEAR), :]
    rn = pl.multiple_of(t0 + DSA_PADF - Q_BLOCK, LANES)
    for pr, s in enumerate(scores(rn, NEAR, chosen(kkn, abn) & (abn <= tq) & (abn >= 0), tbt_ref)):
        update(pr, s, rn, NEAR)

    for pr in range(H // 2):
        o = acc_ref[pr] / jnp.maximum(l_ref[pr], 1e-30)
        ot_ref[pr * LANES:(pr + 1) * LANES, :] = jnp.where(lo, o[:, 0:T], o[:, T:2 * T]).astype(BF16)


def _dsa(dqt, dkp, dvtp, iqt, ikt, iwt, tbt, ltri, S):
    KEEP = min(DSA_TOPK_MAX, S // 4)
    H = DSA_HEADS
    T = Q_BLOCK
    once = pl.Buffered(1)
    full = lambda a: pl.BlockSpec(a.shape, lambda i: (0,) * a.ndim, pipeline_mode=once)
    return pl.pallas_call(
        functools.partial(_dsa_body, S=S, KEEP=KEEP),
        grid=(S // T,),
        in_specs=[pl.BlockSpec((DSA_W, T), lambda i: (0, i)),
                  full(dkp), full(dvtp),
                  pl.BlockSpec((IDX_HEADS * IDX_DIM, T), lambda i: (0, i)),
                  full(ikt),
                  pl.BlockSpec((IDX_HEADS, T), lambda i: (0, i)),
                  full(tbt), full(ltri)],
        out_specs=pl.BlockSpec((DSA_W, T), lambda i: (0, i)),
        out_shape=jax.ShapeDtypeStruct((DSA_W, S), BF16),
        scratch_shapes=[pltpu.VMEM((Q_BLOCK + S + KEY_TILE, T), I32),
                        pltpu.VMEM((H // 2, 1, 2 * T), F32),
                        pltpu.VMEM((H // 2, 1, 2 * T), F32),
                        pltpu.VMEM((H // 2, LANES, 2 * T), F32),
                        pltpu.VMEM((2, H // 2, KEY_TILE, 2 * T), F32)],
        compiler_params=_cparams(("arbitrary",)),
        name="dsa",
    )(dqt, dkp, dvtp, iqt, ikt, iwt, tbt, ltri)


def _split_bf16(a):
    hi = a.astype(BF16)
    return hi, (a - hi.astype(F32)).astype(BF16)


def _merge_body(x_ref, oa_ref, ob_ref, gab_ref, wa_ref, wb_ref, wo_ref, gffn_ref, wr_ref, br_ref,
                x1_ref, h2_ref, ridx_ref, rw_ref):
    a = _dot_tn(oa_ref[...], wa_ref[...])
    b = _dot_tn(ob_ref[...], wb_ref[...])
    gab = gab_ref[...]
    merged = gab[:, :D_MODEL] * a + gab[:, D_MODEL:] * b
    x1 = x_ref[...] + _dot(merged.astype(BF16), wo_ref[...])
    x1_ref[...] = x1
    h2 = x1 * lax.rsqrt(jnp.mean(x1 * x1, axis=-1, keepdims=True) + EPS) * gffn_ref[...]
    h2_ref[...] = h2
    hh, hl = _split_bf16(h2)
    wh, wl = _split_bf16(wr_ref[...])
    logits = _dot(hh, wh) + _dot(hh, wl) + _dot(hl, wh) + br_ref[...]
    lane = lax.broadcasted_iota(I32, (1, LANES), 1)
    idx_out = jnp.zeros(logits.shape, I32)
    val_out = jnp.full(logits.shape, -jnp.inf, F32)
    for k in range(TOP_K):
        mx = jnp.max(logits, axis=-1, keepdims=True)
        first = jnp.min(jnp.where(logits == mx, lane, LANES), axis=-1, keepdims=True)
        idx_out = jnp.where(lane == k, first, idx_out)
        val_out = jnp.where(lane == k, mx, val_out)
        logits = jnp.where(lane == first, -jnp.inf, logits)
    e = jnp.exp(val_out - jnp.max(val_out, axis=-1, keepdims=True))
    ridx_ref[...] = idx_out
    rw_ref[...] = e / jnp.sum(e, axis=-1, keepdims=True)


def _merge(x2, oa, ob, gab, wa, wb, wo, gffn, wr, br, tm=512):
    S = x2.shape[0]
    row = lambda n: pl.BlockSpec((tm, n), lambda i: (i, 0))
    full = lambda a: pl.BlockSpec(a.shape, lambda i: (0,) * a.ndim)
    return pl.pallas_call(
        _merge_body,
        grid=(S // tm,),
        in_specs=[row(D_MODEL), pl.BlockSpec((NSA_W, tm), lambda i: (0, i)), pl.BlockSpec((DSA_W, tm), lambda i: (0, i)),
                  row(2 * D_MODEL),
                  full(wa), full(wb), full(wo), full(gffn), full(wr), full(br)],
        out_specs=[row(D_MODEL), row(D_MODEL), row(LANES), row(LANES)],
        out_shape=[jax.ShapeDtypeStruct((S, D_MODEL), F32), jax.ShapeDtypeStruct((S, D_MODEL), F32),
                   jax.ShapeDtypeStruct((S, LANES), I32), jax.ShapeDtypeStruct((S, LANES), F32)],
        compiler_params=_cparams(("arbitrary",)),
        name="merge",
    )(x2, oa, ob, gab, wa, wb, wo, gffn, wr, br)


def _rank_body(ridx_ref, rank_ref, cnt_ref, carry_ref):
    @pl.when(pl.program_id(0) == 0)
    def _():
        carry_ref[...] = jnp.zeros(carry_ref.shape, F32)

    ridx = ridx_ref[...]
    tm = ridx.shape[0]
    lane = lax.broadcasted_iota(I32, (1, LANES), 1)
    hits = [lane == ridx[:, k:k + 1] for k in range(TOP_K)]
    member = jnp.zeros((tm, LANES), F32)
    for hk in hits:
        member = member + jnp.where(hk, 1.0, 0.0)
    ri = lax.broadcasted_iota(I32, (tm, tm), 0)
    ci = lax.broadcasted_iota(I32, (tm, tm), 1)
    lower = jnp.where(ci < ri, 1.0, 0.0).astype(BF16)
    before = _dot(lower, member.astype(BF16)) + carry_ref[0:1, :]
    out = jnp.zeros((tm, LANES), I32)
    for k, hk in enumerate(hits):
        rk = jnp.sum(jnp.where(hk, before, 0.0), axis=-1, keepdims=True)
        out = jnp.where(lane == k, rk.astype(I32), out)
    rank_ref[...] = out
    total = carry_ref[0:1, :] + jnp.sum(member, axis=0, keepdims=True)
    carry_ref[...] = jnp.broadcast_to(total, carry_ref.shape)
    cnt_ref[...] = jnp.broadcast_to(total, cnt_ref.shape).astype(I32)


def _rank(ridx, tm=512):
    S = ridx.shape[0]
    return pl.pallas_call(
        _rank_body,
        grid=(S // tm,),
        in_specs=[pl.BlockSpec((tm, LANES), lambda i: (i, 0))],
        out_specs=[pl.BlockSpec((tm, LANES), lambda i: (i, 0)), pl.BlockSpec((8, LANES), lambda i: (0, 0))],
        out_shape=[jax.ShapeDtypeStruct((S, LANES), I32), jax.ShapeDtypeStruct((8, LANES), I32)],
        scratch_shapes=[pltpu.VMEM((8, LANES), F32)],
        compiler_params=_cparams(("arbitrary",)),
        name="rank",
    )(ridx)


def _dispatch_body(dest_ref, h_ref, zero_ref, xe_ref, sem):
    del zero_ref
    i = pl.program_id(0)
    tm = h_ref.shape[0]

    def start(r, c):
        for k in range(TOP_K):
            d = dest_ref[(i * tm + r) * TOP_K + k]
            pltpu.make_async_copy(h_ref.at[pl.ds(r, 1)], xe_ref.at[pl.ds(d, 1)],
                                  sem).start(priority=k % 2)
        return c

    lax.fori_loop(0, tm, start, 0, unroll=ROW_DMA_UNROLL)
    for _ in range(TOP_K):
        pltpu.make_async_copy(h_ref, xe_ref.at[pl.ds(0, tm)], sem).wait()


def _dispatch(dest, h2, zeros_p, tm=128):
    S = h2.shape[0]
    return pl.pallas_call(
        _dispatch_body,
        grid_spec=pltpu.PrefetchScalarGridSpec(
            num_scalar_prefetch=1,
            grid=(S // tm,),
            in_specs=[pl.BlockSpec((tm, D_MODEL), lambda i, d: (i, 0)),
                      pl.BlockSpec(memory_space=pl.ANY)],
            out_specs=pl.BlockSpec(memory_space=pl.ANY),
            scratch_shapes=[pltpu.SemaphoreType.DMA(())]),
        out_shape=jax.ShapeDtypeStruct(zeros_p.shape, zeros_p.dtype),
        input_output_aliases={2: 0},
        compiler_params=_cparams(("arbitrary",)),
        name="dispatch",
    )(dest, h2, zeros_p)


def _experts_body(ce_ref, nu_ref, x_ref, w1_ref, b1_ref, w2_ref, b2_ref, y_ref, w1b_ref, w2b_ref):
    c = pl.program_id(0)

    @pl.when((c == 0) | (ce_ref[c] != ce_ref[jnp.maximum(c - 1, 0)]))
    def _():
        w1b_ref[...] = w1_ref[...].astype(BF16)
        w2b_ref[...] = w2_ref[...].astype(BF16)

    @pl.when(c < nu_ref[0])
    def _():
        gu = _dot(x_ref[...].astype(BF16), w1b_ref[...]) + b1_ref[...]
        gate = jnp.minimum(gu[:, :D_FF], SWIGLU_LIMIT)
        lin = jnp.clip(gu[:, D_FF:], -SWIGLU_LIMIT, SWIGLU_LIMIT)
        act = (lin + 1.0) * gate * jax.nn.sigmoid(SWIGLU_ALPHA * gate)
        y_ref[...] = _dot(act.astype(BF16), w2b_ref[...]) + b2_ref[...]

    @pl.when(c >= nu_ref[0])
    def _():
        y_ref[...] = jnp.zeros(y_ref.shape, F32)


def _experts(chunk_e, n_used, xe, w1, b1, w2, b2):
    P = xe.shape[0]
    nch = P // MOE_ROWS
    return pl.pallas_call(
        _experts_body,
        grid_spec=pltpu.PrefetchScalarGridSpec(
            num_scalar_prefetch=2,
            grid=(nch,),
            in_specs=[pl.BlockSpec((MOE_ROWS, D_MODEL), lambda c, ce, nu: (c, 0)),
                      pl.BlockSpec((None, D_MODEL, 2 * D_FF), lambda c, ce, nu: (ce[c], 0, 0)),
                      pl.BlockSpec((None, 1, 2 * D_FF), lambda c, ce, nu: (ce[c], 0, 0)),
                      pl.BlockSpec((None, D_FF, D_MODEL), lambda c, ce, nu: (ce[c], 0, 0)),
                      pl.BlockSpec((None, 1, D_MODEL), lambda c, ce, nu: (ce[c], 0, 0))],
            out_specs=pl.BlockSpec((MOE_ROWS, D_MODEL), lambda c, ce, nu: (c, 0)),
            scratch_shapes=[pltpu.VMEM((D_MODEL, 2 * D_FF), BF16), pltpu.VMEM((D_FF, D_MODEL), BF16)]),
        out_shape=jax.ShapeDtypeStruct((P, D_MODEL), F32),
        compiler_params=_cparams(("arbitrary",)),
        name="experts",
    )(chunk_e, n_used, xe, w1, b1, w2, b2)


def _final_body(dest_ref, x1_ref, rw_ref, p_ref, gple_ref, wg_ref, wp_ref, y_ref, o_ref, rows_ref, sems):
    i = pl.program_id(0)
    tm = x1_ref.shape[0]

    def gather(blk, slot):
        def start(r, c):
            for k in range(TOP_K):
                d = dest_ref[(blk * tm + r) * TOP_K + k]
                pltpu.make_async_copy(y_ref.at[pl.ds(d, 1)], rows_ref.at[slot, k, pl.ds(r, 1)],
                                      sems.at[slot]).start(priority=k % 2)
            return c
        lax.fori_loop(0, tm, start, 0, unroll=ROW_DMA_UNROLL)

    @pl.when(i == 0)
    def _():
        gather(0, 0)

    @pl.when(i + 1 < pl.num_programs(0))
    def _():
        gather(i + 1, (i + 1) % 2)

    slot = i % 2
    for k in range(TOP_K):
        pltpu.make_async_copy(y_ref.at[pl.ds(0, tm)], rows_ref.at[slot, k], sems.at[slot]).wait()
    rw = rw_ref[...]
    x2 = x1_ref[...]
    for k in range(TOP_K):
        x2 = x2 + rw[:, k:k + 1] * rows_ref[slot, k]
    hn = x2 * lax.rsqrt(jnp.mean(x2 * x2, axis=-1, keepdims=True) + EPS) * gple_ref[...]
    gate = jax.nn.sigmoid(_dot(hn.astype(BF16), wg_ref[...]))
    o_ref[...] = x2 + gate * _dot(p_ref[...].astype(BF16), wp_ref[...])


def _final(dest, x1, rw, p2, gple, wg, wp, yexp, tm=128):
    S = x1.shape[0]
    row = lambda n: pl.BlockSpec((tm, n), lambda i, d: (i, 0))
    full = lambda a: pl.BlockSpec(a.shape, lambda i, d: (0,) * a.ndim)
    return pl.pallas_call(
        _final_body,
        grid_spec=pltpu.PrefetchScalarGridSpec(
            num_scalar_prefetch=1,
            grid=(S // tm,),
            in_specs=[row(D_MODEL), row(LANES), row(PLE_DIM), full(gple), full(wg), full(wp),
                      pl.BlockSpec(memory_space=pl.ANY)],
            out_specs=row(D_MODEL),
            scratch_shapes=[pltpu.VMEM((2, TOP_K, tm, D_MODEL), F32), pltpu.SemaphoreType.DMA((2,))]),
        out_shape=jax.ShapeDtypeStruct((S, D_MODEL), F32),
        compiler_params=_cparams(("arbitrary",)),
        name="final",
    )(dest, x1, rw, p2, gple, wg, wp, yexp)


def _distance_bias(rel_cols):
    return (rel_cols[_rel_bucket_table(NEAR)] - rel_cols[REL_BUCKETS - 1][None, :]).T * LOG2E


def _near_bias_t(rel_cols):
    h = rel_cols.shape[1]
    g = jnp.pad(_distance_bias(rel_cols), ((0, 0), (Q_BLOCK, Q_BLOCK)))
    period = 2 * NEAR
    skew = jnp.tile(g, (1, NEAR))[:, :NEAR * (period - 1)].reshape(h, NEAR, period - 1)
    return skew[:, :, NEAR:NEAR + Q_BLOCK]


def _cmp_bias_t(rel_cols):
    g = jnp.pad(_distance_bias(rel_cols), ((0, 0), (NEAR, Q_BLOCK)))
    rows = []
    for j in range(4):
        for k in range(4):
            c0 = -NSA_CMP_STRIDE * j - (NSA_CMP_LEN - 1) + NSA_SLC_BLOCK * (k - 1)
            rows.append(g[:, c0 + NEAR:c0 + NEAR + Q_BLOCK])
    return jnp.stack(rows, axis=1)


def kernel(x, p, w_in, g_mix, g_q_nsa, g_k_nsa, g_q_dsa, g_k_dsa, pe_ck, w_ck1, w_ck2, pe_cv, w_cv1, w_cv2,
           w_branch_a, w_branch_b, w_out, rel_bias, g_ffn, w_router, b_router, w_e1, b_e1, w_e2, b_e2,
           g_ple, w_ple, w_ple_gate):
    B, S, D = x.shape
    assert B == 1 and D == D_MODEL and S % KEY_TILE == 0 and w_in.shape[0] == 1
    G, R = NSA_KV_GROUPS, NSA_GROUP_SIZE
    x2 = x.reshape(S, D)

    w0 = w_in[0]
    w_nat = jnp.concatenate([w0[:, 512:768], w0[:, 768:896], w0[:, 1024:1152], w0[:, 1816:2328], w0[:, 3136:5184],
                             w0[:, 1280:1304], w0[:, 3096:3136], jnp.zeros((D, LANES - 64), F32)], axis=1).astype(BF16)
    w_tr = jnp.concatenate([w0[:, 0:512], w0[:, 1304:1816], w0[:, 2328:2840], w0[:, 2840:3096],
                            w0[:, 896:1024], w0[:, 1152:1280]], axis=1).T.astype(BF16)
    two = lambda g: jnp.tile(g.reshape(1, HEAD_DIM), (1, 2))
    qscale = HEAD_DIM ** -0.5 * LOG2E
    gains = jnp.concatenate([two(g_k_nsa[0]), two(g_k_dsa[0])], axis=0)
    gcol = jnp.stack([g_q_nsa[0] * qscale, g_q_dsa[0] * qscale]).reshape(2, HEAD_DIM, 1)
    (cmpraw, kslc, kwin, dk, gab, small, nqt, dqt, dvt, iqt, vst2, vwt2) = _inproj(
        x2, g_mix[0].reshape(1, D), w_nat, w_tr, gains, gcol)

    NS = S // NSA_SLC_BLOCK
    nchunk = S // NSA_CMP_STRIDE
    xc = cmpraw.reshape(nchunk, NSA_CMP_STRIDE, 2 * G, HEAD_DIM).transpose(2, 0, 1, 3).reshape(
        2 * G, NS, 4 * NSA_CMP_STRIDE * HEAD_DIM)
    half = NSA_CMP_STRIDE * HEAD_DIM
    w1 = jnp.stack([w_ck1[0], w_cv1[0]]).reshape(2, 2 * half, HEAD_DIM)
    pe = jnp.stack([pe_ck[0], pe_cv[0]]).reshape(2, 2 * half, 1)
    w2 = jnp.stack([w_ck2[0], w_cv2[0]]).astype(BF16)
    kvc = _compress(xc, w1[:, :half].astype(BF16), w1[:, half:].astype(BF16), w1, pe, w2,
                    g_k_nsa[0].reshape(1, HEAD_DIM))

    rel_nsa = rel_bias[:, :NSA_HEADS]
    rel_dsa = rel_bias[:, NSA_HEADS:]
    oct, selt = _cmp(nqt, kvc[:G], kvc[G:].transpose(0, 2, 1), _cmp_bias_t(rel_nsa), S)
    kgroups = lambda a: jnp.pad(a.reshape(S, G, HEAD_DIM).transpose(1, 0, 2),
                                ((0, 0), (NSA_PADF, TAIL_PAD), (0, 0)))
    vgroups = lambda a: jnp.pad(a.reshape(G, HEAD_DIM, S), ((0, 0), (0, 0), (NSA_PADF, TAIL_PAD)))
    glt = small[:, SMALL_GATE:SMALL_GATE + 3 * NSA_HEADS].reshape(S, G, 3 * R).transpose(1, 2, 0)
    glt = jnp.pad(glt, ((0, 0), (0, 16 - 3 * R), (0, 0)))
    blk_lane = (np.arange(S) // NSA_SLC_BLOCK) % LANES
    onehot = jnp.asarray(np.eye(LANES, dtype=np.float32)[blk_lane] * MASK_BIG, BF16)
    kaug = jnp.concatenate([kslc.reshape(S, G, HEAD_DIM).transpose(1, 0, 2),
                            jnp.broadcast_to(onehot, (G, S, LANES)),
                            jnp.zeros((G, S, KAUG_W - HEAD_DIM - LANES), BF16)], axis=-1)
    kaug = jnp.pad(kaug, ((0, 0), (NSA_PADF, TAIL_PAD), (0, 0)))
    o_at = _slcwin(nqt, kaug, vgroups(vst2), kgroups(kwin), vgroups(vwt2), selt,
                   _near_bias_t(rel_nsa), oct, glt, S)

    dkp = jnp.pad(dk, ((DSA_PADF, TAIL_PAD), (0, 0)))
    dvtp = jnp.pad(dvt, ((0, 0), (DSA_PADF, TAIL_PAD)))
    ikt = jnp.pad(small[:, SMALL_IK:SMALL_IK + IDX_DIM].T.astype(BF16), ((0, 0), (0, 2 * KEY_TILE)))
    iwt = small[:, SMALL_IW:SMALL_IW + IDX_HEADS].T
    ltri = jnp.asarray(np.tril(np.ones((KEY_TILE, KEY_TILE), np.float32), -1), BF16)
    o_bt = _dsa(dqt, dkp, dvtp, iqt, ikt, iwt, _near_bias_t(rel_dsa), ltri, S)

    wr = jnp.pad(w_router[0], ((0, 0), (0, LANES - N_EXPERTS)))
    br = jnp.pad(b_router[0].reshape(1, N_EXPERTS), ((0, 0), (0, LANES - N_EXPERTS)), constant_values=-1e30)
    x1, h2, ridx, rw = _merge(x2, o_at, o_bt, gab, w_branch_a[0].astype(BF16), w_branch_b[0].astype(BF16),
                              w_out[0].astype(BF16), g_ffn[0].reshape(1, D), wr, br)

    rank, cnt = _rank(ridx)
    counts = cnt[0, :N_EXPERTS]
    padded = (counts + MOE_ROWS - 1) // MOE_ROWS * MOE_ROWS
    pad_end = jnp.cumsum(padded)
    pad_start = pad_end - padded
    eidx = ridx[:, :TOP_K]
    dest = (jnp.sum(jnp.where(eidx[..., None] == jnp.arange(N_EXPERTS), pad_start, 0), axis=-1)
            + rank[:, :TOP_K]).reshape(-1).astype(I32)
    n_chunks = -(-(S * TOP_K) // MOE_ROWS) + N_EXPERTS
    chunk_start = jnp.arange(n_chunks, dtype=I32) * MOE_ROWS
    chunk_e = jnp.minimum(jnp.sum(chunk_start[:, None] >= pad_end[None, :], axis=-1), N_EXPERTS - 1).astype(I32)
    n_used = (pad_end[-1] // MOE_ROWS).astype(I32).reshape(1)
    xe = _dispatch(dest, h2, jnp.zeros((n_chunks * MOE_ROWS, D), F32))
    yexp = _experts(chunk_e, n_used, xe, w_e1[0], b_e1[0].reshape(N_EXPERTS, 1, 2 * D_FF),
                    w_e2[0], b_e2[0].reshape(N_EXPERTS, 1, D))

    out = _final(dest, x1, rw, p[0].reshape(S, PLE_DIM), g_ple[0].reshape(1, D),
                 w_ple_gate[0].astype(BF16), w_ple[0].astype(BF16), yexp)
    return out.reshape(B, S, D)
```

```python
import functools
import math

import numpy as np
import jax
import jax.numpy as jnp
from jax import lax
from jax.experimental import pallas as pl
from jax.experimental.pallas import tpu as pltpu

F32 = jnp.float32
BF16 = jnp.bfloat16
I32 = jnp.int32

D_MODEL = 1024
PLE_DIM = 256
HEAD_DIM = 64
NSA_HEADS = 8
NSA_KV_GROUPS = 2
NSA_GROUP_SIZE = NSA_HEADS // NSA_KV_GROUPS
NSA_CMP_LEN = 32
NSA_CMP_STRIDE = 16
NSA_SLC_BLOCK = 64
NSA_SLC_TOPK = 16
NSA_LOCAL_BLOCKS = 2
NSA_WINDOW = 512
DSA_HEADS = 8
IDX_HEADS = 8
IDX_DIM = 32
DSA_TOPK_MAX = 256
Q_BLOCK = 128
REL_BUCKETS = 32
REL_MAX_EXACT = 16
REL_MAX_DIST = 128
N_EXPERTS = 32
TOP_K = 4
D_FF = 1024
SWIGLU_LIMIT = 7.0
SWIGLU_ALPHA = 1.702
MOE_ROWS = 512
EPS = 1e-6
LOG2E = math.log2(math.e)

NSA_W = NSA_HEADS * HEAD_DIM
DSA_W = DSA_HEADS * HEAD_DIM

LANES = 128
SUBLANES = 8
KEY_TILE = 512
NEAR = 2 * Q_BLOCK
NSA_PADF = NSA_WINDOW
DSA_PADF = Q_BLOCK
KAUG_W = 2 * LANES
MASK_BIG = 2.0 ** 100
TAIL_PAD = 3 * KEY_TILE
VMEM_LIMIT = 56 * 1024 * 1024
ROW_DMA_UNROLL = 8

NAT_CMP, NAT_KSLC, NAT_KWIN, NAT_DK, NAT_GAB, NAT_SMALL = 0, 256, 384, 512, 1024, 3072
TR_NQ, TR_DQ, TR_DV, TR_IQ, TR_VSLC, TR_VWIN = 0, 512, 1024, 1536, 1792, 1920
SMALL_GATE, SMALL_IK, SMALL_IW = 0, 24, 56


def _rel_bucket_table(n):
    d = np.arange(n)
    nf = np.maximum(d, 1).astype(np.float64)
    large = REL_MAX_EXACT + (np.log(nf / REL_MAX_EXACT) / math.log(REL_MAX_DIST / REL_MAX_EXACT)
                             * (REL_BUCKETS - REL_MAX_EXACT)).astype(np.int64)
    large = np.minimum(large, REL_BUCKETS - 1)
    return np.where(d < REL_MAX_EXACT, d, large)


def _cparams(sem, vmem=VMEM_LIMIT):
    return pltpu.CompilerParams(dimension_semantics=sem, vmem_limit_bytes=vmem)


def _dot(a, b):
    return jnp.dot(a, b, preferred_element_type=F32)


def _dot_tn(a, b):
    return lax.dot_general(a, b, (((0,), (0,)), ((), ())), preferred_element_type=F32)


def _inproj_body(x_ref, gmix_ref, w_ref, wt_ref, gains_ref, gcol_ref,
                 cmp_ref, kslc_ref, kwin_ref, dk_ref, gab_ref, small_ref,
                 nqt_ref, dqt_ref, dvt_ref, iqt_ref, vst_ref, vwt_ref):
    x = x_ref[...]
    h = x * lax.rsqrt(jnp.mean(x * x, axis=-1, keepdims=True) + EPS) * gmix_ref[...]
    hb = h.astype(BF16)
    tm = hb.shape[0]
    lo = lax.broadcasted_iota(I32, (1, LANES), 1) < HEAD_DIM

    def mm(off, n):
        return _dot(hb, w_ref[:, off:off + n])

    def headnorm(y, g):
        y2 = y * y
        s_lo = jnp.sum(jnp.where(lo, y2, 0.0), axis=-1, keepdims=True)
        s_hi = jnp.sum(jnp.where(lo, 0.0, y2), axis=-1, keepdims=True)
        r = jnp.where(lo, lax.rsqrt(s_lo * (1.0 / HEAD_DIM) + EPS), lax.rsqrt(s_hi * (1.0 / HEAD_DIM) + EPS))
        return y * r * g

    def headnorm_t(y, g):
        y3 = y.reshape(y.shape[0] // HEAD_DIM, HEAD_DIM, tm)
        r = lax.rsqrt(jnp.mean(y3 * y3, axis=1, keepdims=True) + EPS)
        return (y3 * r * g).reshape(y.shape)

    g_kn, g_kd = gains_ref[0:1, :], gains_ref[1:2, :]
    def put_heads(ref, y):
        for k in range(ref.shape[0]):
            ref[k] = y[:, k * HEAD_DIM:(k + 1) * HEAD_DIM].astype(BF16)

    put_heads(cmp_ref, mm(NAT_CMP, 256))
    put_heads(kslc_ref, headnorm(mm(NAT_KSLC, LANES), g_kn))
    put_heads(kwin_ref, headnorm(mm(NAT_KWIN, LANES), g_kn))
    for c in range(DSA_W // LANES):
        dk_ref[:, c * LANES:(c + 1) * LANES] = headnorm(mm(NAT_DK + c * LANES, LANES), g_kd).astype(BF16)
    gab_ref[...] = jax.nn.sigmoid(mm(NAT_GAB, 2 * D_MODEL))
    small_ref[...] = mm(NAT_SMALL, LANES)

    def mm_t(off, n):
        return lax.dot_general(wt_ref[off:off + n, :], hb, (((1,), (1,)), ((), ())), preferred_element_type=F32)

    nqt_ref[...] = headnorm_t(mm_t(TR_NQ, NSA_W), gcol_ref[0]).astype(BF16)
    dqt_ref[...] = headnorm_t(mm_t(TR_DQ, DSA_W), gcol_ref[1]).astype(BF16)
    dvt_ref[...] = mm_t(TR_DV, DSA_W).astype(BF16)
    iqt_ref[...] = (mm_t(TR_IQ, IDX_HEADS * IDX_DIM) * (IDX_DIM ** -0.5)).astype(BF16)
    vst_ref[...] = mm_t(TR_VSLC, LANES).astype(BF16)
    vwt_ref[...] = mm_t(TR_VWIN, LANES).astype(BF16)


def _inproj(x2, g_mix, w_nat, w_tr, gains, gcol, tm=512):
    S = x2.shape[0]
    row = lambda n: pl.BlockSpec((tm, n), lambda i: (i, 0))
    col = lambda n: pl.BlockSpec((n, tm), lambda i: (0, i))
    full = lambda a: pl.BlockSpec(a.shape, lambda i: (0,) * a.ndim)
    heads = lambda n: pl.BlockSpec((n, tm, HEAD_DIM), lambda i: (0, i, 0))
    grouped = [2 * NSA_KV_GROUPS, NSA_KV_GROUPS, NSA_KV_GROUPS]
    nat = [(DSA_W, BF16), (2 * D_MODEL, F32), (LANES, F32)]
    tr = [NSA_W, DSA_W, DSA_W, IDX_HEADS * IDX_DIM, LANES, LANES]
    return pl.pallas_call(
        _inproj_body,
        grid=(S // tm,),
        in_specs=[row(D_MODEL), full(g_mix), full(w_nat), full(w_tr), full(gains), full(gcol)],
        out_specs=[heads(n) for n in grouped] + [row(n) for n, _ in nat] + [col(n) for n in tr],
        out_shape=[jax.ShapeDtypeStruct((n, S, HEAD_DIM), BF16) for n in grouped]
                  + [jax.ShapeDtypeStruct((S, n), dt) for n, dt in nat]
                  + [jax.ShapeDtypeStruct((n, S), BF16) for n in tr],
        compiler_params=_cparams(("arbitrary",)),
        name="inproj",
    )(x2, g_mix, w_nat, w_tr, gains, gcol)


def _compress_body(xa_ref, xb_ref, w1a_ref, w1b_ref, w1f_ref, pe_ref, w2_ref, gk_ref, o_ref):
    is_k = pl.program_id(0) < NSA_KV_GROUPS
    ns = xa_ref.shape[0]
    pe_term = jnp.sum(pe_ref[...] * w1f_ref[...], axis=0, keepdims=True)
    second = _dot(xb_ref[...], w1b_ref[...])
    second = jnp.where(pl.program_id(1) == 3, pltpu.roll(second, ns - 1, axis=0), second)
    pre = _dot(xa_ref[...], w1a_ref[...]) + second + pe_term
    hdn = pre * jax.nn.sigmoid(pre)
    out = _dot(hdn.astype(BF16), w2_ref[...])
    normed = out * lax.rsqrt(jnp.mean(out * out, axis=-1, keepdims=True) + EPS) * gk_ref[...]
    o_ref[...] = jnp.where(is_k, normed, out).astype(BF16)


def _compress(xc, w1a, w1b, w1f, pe, w2, gk):
    n4, ns, kdim = xc.shape[0], xc.shape[1], xc.shape[2] // 4
    G = NSA_KV_GROUPS
    kv = lambda i: i // G
    return pl.pallas_call(
        _compress_body,
        grid=(n4, 4),
        in_specs=[pl.BlockSpec((None, ns, kdim), lambda i, j: (i, 0, j)),
                  pl.BlockSpec((None, ns, kdim), lambda i, j: (i, 0, (j + 1) % 4)),
                  pl.BlockSpec((None, kdim, HEAD_DIM), lambda i, j: (kv(i), 0, 0)),
                  pl.BlockSpec((None, kdim, HEAD_DIM), lambda i, j: (kv(i), 0, 0)),
                  pl.BlockSpec((None, 2 * kdim, HEAD_DIM), lambda i, j: (kv(i), 0, 0)),
                  pl.BlockSpec((None, 2 * kdim, 1), lambda i, j: (kv(i), 0, 0)),
                  pl.BlockSpec((None, HEAD_DIM, HEAD_DIM), lambda i, j: (kv(i), 0, 0)),
                  pl.BlockSpec((1, HEAD_DIM), lambda i, j: (0, 0))],
        out_specs=pl.BlockSpec((None, ns, HEAD_DIM), lambda i, j: (i, j, 0)),
        out_shape=jax.ShapeDtypeStruct((n4, 4 * ns, HEAD_DIM), BF16),
        compiler_params=_cparams(("arbitrary", "arbitrary")),
        name="compress",
    )(xc, xc, w1a, w1b, w1f, pe, w2, gk)


def _cmp_body(qt_ref, kc_ref, vct_ref, bt_ref, oct_ref, selt_ref, s_ref, imp_ref, *, NS):
    i = pl.program_id(1)

    @pl.when((pl.program_id(0) == 0) & (i == 0))
    def _():
        imp_ref[...] = jnp.zeros(imp_ref.shape, F32)

    imp_prev = imp_ref[...]
    ib = jnp.minimum(i, pl.num_programs(1) - 2)
    t0 = ib * Q_BLOCK
    T = Q_BLOCK
    R = NSA_GROUP_SIZE
    WIN = 2 * SUBLANES

    def step(ne):
        NE = 4 * ne
        row = lax.broadcasted_iota(I32, (NE, 1), 0)
        plane = row // ne
        nblk = row - plane * ne
        cend = NSA_SLC_BLOCK * nblk + NSA_CMP_STRIDE * plane + (NSA_CMP_LEN - 1)
        tq = t0 + lax.broadcasted_iota(I32, (1, T), 1)
        mask = cend <= tq
        kc = jnp.concatenate([kc_ref[j * NS:j * NS + ne, :] for j in range(4)], axis=0)
        vct = jnp.concatenate([vct_ref[:, j * NS:j * NS + ne] for j in range(4)], axis=1)
        wstart = jnp.clip((2 * ib - 2) // SUBLANES * SUBLANES, 0, ne - WIN)
        wrow = lax.broadcasted_iota(I32, (WIN, 1), 0)
        psum = jnp.zeros((NE, T), F32)
        for r in range(R):
            s_ref[r, 0:NE, :] = _dot(kc, qt_ref[r * HEAD_DIM:(r + 1) * HEAD_DIM, :])
        for r in range(R):
            bt = bt_ref[r]
            for j in range(4):
                add = jnp.zeros((WIN, T), F32)
                for k in range(4):
                    n = 2 * ib + 1 - k
                    add = add + jnp.where((wrow == n - wstart) & (n >= 0), bt[4 * j + k:4 * j + k + 1, :], 0.0)
                sl = pl.ds(pl.multiple_of(j * ne + wstart, SUBLANES), WIN)
                s_ref[r, sl, :] = s_ref[r, sl, :] + add
            sc = jnp.where(mask, s_ref[r, 0:NE, :], -jnp.inf)
            m = jnp.maximum(jnp.max(sc, axis=0, keepdims=True), -1e30)
            e = jnp.exp2(sc - m)
            pc = e / jnp.maximum(jnp.sum(e, axis=0, keepdims=True), 1e-30)
            oct_ref[r * HEAD_DIM:(r + 1) * HEAD_DIM, :] = _dot(vct, pc.astype(BF16))
            psum = psum + pc
        p0, p1, p2, p3 = (psum[j * ne:(j + 1) * ne, :] for j in range(4))
        p3s = jnp.where(lax.broadcasted_iota(I32, (ne, 1), 0) == 0, 0.0, pltpu.roll(p3, 1, axis=0))
        imp_ref[0:ne, :] = p0 + p1 + p2 + 0.5 * p3 + 0.5 * p3s
        if ne < NS:
            imp_ref[ne:NS, :] = jnp.zeros((NS - ne, T), F32)

        n_ar = lax.broadcasted_iota(I32, (NS, 1), 0)
        tqs = (i - 1) * Q_BLOCK + lax.broadcasted_iota(I32, (1, T), 1)
        cur = tqs // NSA_SLC_BLOCK
        valid = NSA_SLC_BLOCK * n_ar <= tqs
        forced = valid & ((n_ar == 0) | (n_ar > cur - NSA_LOCAL_BLOCKS))
        score = jnp.where(forced, 1e30, jnp.where(valid, imp_prev, -1.0))
        sel = jnp.zeros((NS, T), jnp.bool_)
        for _ in range(min(NSA_SLC_TOPK, NS)):
            mx = jnp.max(score, axis=0, keepdims=True)
            first = jnp.min(jnp.where(score == mx, n_ar, NS), axis=0, keepdims=True)
            pick = n_ar == first
            sel = sel | pick
            score = jnp.where(pick, -2.0, score)
        selb = jnp.where(sel, 1.0, 0.0).astype(BF16)
        if NS < LANES:
            selb = jnp.concatenate([selb, jnp.zeros((LANES - NS, T), BF16)], axis=0)
        for w in range(selt_ref.shape[0]):
            selt_ref[w] = selb[w * LANES:(w + 1) * LANES, :]

    quarter = NS // 4
    if quarter % WIN == 0:
        which = (2 * ib + 2 + quarter - 1) // quarter - 1
        for v in range(4):
            pl.when(which == v)(functools.partial(step, (v + 1) * quarter))
    else:
        step(NS)


def _cmp(nqt, kc, vct, bt, S):
    NS = S // NSA_SLC_BLOCK
    NCP = 4 * NS
    NW = max(NS // LANES, 1)
    G, R = NSA_KV_GROUPS, NSA_GROUP_SIZE
    nb = S // Q_BLOCK
    return pl.pallas_call(
        functools.partial(_cmp_body, NS=NS),
        grid=(G, nb + 1),
        in_specs=[pl.BlockSpec((R * HEAD_DIM, Q_BLOCK), lambda g, i: (g, jnp.minimum(i, nb - 1))),
                  pl.BlockSpec((None, NCP, HEAD_DIM), lambda g, i: (g, 0, 0)),
                  pl.BlockSpec((None, HEAD_DIM, NCP), lambda g, i: (g, 0, 0)),
                  pl.BlockSpec((R, 16, Q_BLOCK), lambda g, i: (g, 0, 0))],
        out_specs=[pl.BlockSpec((R * HEAD_DIM, Q_BLOCK), lambda g, i: (g, jnp.minimum(i, nb - 1))),
                   pl.BlockSpec((None, NW, LANES, Q_BLOCK), lambda g, i: (g, 0, 0, jnp.maximum(i - 1, 0)))],
        out_shape=[jax.ShapeDtypeStruct((NSA_W, S), F32),
                   jax.ShapeDtypeStruct((G, NW, LANES, S), BF16)],
        scratch_shapes=[pltpu.VMEM((R, NCP, Q_BLOCK), F32), pltpu.VMEM((NS, Q_BLOCK), F32)],
        compiler_params=_cparams(("arbitrary", "arbitrary")),
        name="cmp",
    )(nqt, kc, vct, bt)


def _flash_init(m_ref, l_ref, acc_ref):
    m_ref[...] = jnp.full(m_ref.shape, -1e30, F32)
    l_ref[...] = jnp.zeros(l_ref.shape, F32)
    acc_ref[...] = jnp.zeros(acc_ref.shape, F32)


def _flash_update(s, vt, m_ref, l_ref, acc_ref):
    m_old = m_ref[...]
    m_new = jnp.maximum(m_old, jnp.max(s, axis=0, keepdims=True))
    alpha = jnp.exp2(m_old - m_new)
    p = jnp.exp2(s - m_new)
    l_ref[...] = alpha * l_ref[...] + jnp.sum(p, axis=0, keepdims=True)
    acc_ref[...] = alpha * acc_ref[...] + _dot(vt, p.astype(BF16))
    m_ref[...] = m_new


def _slcwin_body(qt_ref, ks_ref, vst_ref, kw_ref, vwt_ref, selt_ref, tbt_ref, oct_ref, glt_ref, ot_ref,
                 ms_ref, ls_ref, as_ref, mw_ref, lw_ref, aw_ref, sbuf_ref, qaug_ref):
    i = pl.program_id(1)
    t0 = i * Q_BLOCK
    R = NSA_GROUP_SIZE
    T = Q_BLOCK
    q4t = jnp.concatenate([qt_ref[r * HEAD_DIM:(r + 1) * HEAD_DIM, :] for r in range(R)], axis=1)
    tq = lax.broadcasted_iota(I32, (1, T), 1)
    rep = lambda mk: jnp.concatenate([mk] * R, axis=1)
    tb4 = jnp.concatenate([tbt_ref[r] for r in range(R)], axis=1)

    def expand(w, base, nkeys):
        keyi = lax.broadcasted_iota(I32, (nkeys, LANES), 0)
        blk = lax.broadcasted_iota(I32, (nkeys, LANES), 1)
        e = jnp.where(blk == base + keyi // NSA_SLC_BLOCK, 1.0, 0.0).astype(BF16)
        return _dot(e, selt_ref[w])

    _flash_init(ms_ref, ls_ref, as_ref)
    _flash_init(mw_ref, lw_ref, aw_ref)
    n_win = selt_ref.shape[0]
    zpad = jnp.zeros((KAUG_W - HEAD_DIM - LANES, R * T), BF16)
    for w in range(n_win):
        qaug_ref[w] = jnp.concatenate([q4t, rep(selt_ref[w] - jnp.asarray(1.0, BF16)), zpad], axis=0)
    far_end = jnp.maximum(t0 - Q_BLOCK, 0) // KEY_TILE * KEY_TILE
    nfar = far_end // KEY_TILE
    tiles_per_window = LANES * NSA_SLC_BLOCK // KEY_TILE

    def score_far(kt, slot):
        r0 = pl.multiple_of(NSA_PADF + kt * KEY_TILE, LANES)
        w = jnp.minimum(kt // tiles_per_window, n_win - 1)
        sbuf_ref[slot] = _dot(ks_ref[pl.ds(r0, KEY_TILE), :], qaug_ref[w])

    def update_far(kt, slot):
        r0 = pl.multiple_of(NSA_PADF + kt * KEY_TILE, LANES)
        _flash_update(sbuf_ref[slot], vst_ref[:, pl.ds(r0, KEY_TILE)], ms_ref, ls_ref, as_ref)

    score_far(0, 0)

    def far2(k2, carry):
        kt = 2 * k2
        score_far(kt + 1, 1)
        update_far(kt, 0)
        score_far(kt + 2, 0)
        update_far(kt + 1, 1)
        return carry

    lax.fori_loop(0, nfar // 2, far2, 0)

    @pl.when(nfar % 2 == 1)
    def _():
        update_far(nfar - 1, 0)

    kq = lambda r0, n: _dot(ks_ref[pl.ds(r0, n), 0:HEAD_DIM], q4t)
    keyn = lax.broadcasted_iota(I32, (NEAR, 1), 0)
    near_ok = (keyn - Q_BLOCK <= tq) & (keyn + t0 - Q_BLOCK >= 0)
    rn = pl.multiple_of(t0 + NSA_PADF - Q_BLOCK, LANES)
    WF = NSA_WINDOW - Q_BLOCK
    rw = pl.multiple_of(t0 + NSA_PADF - NSA_WINDOW, LANES)
    keyw = lax.broadcasted_iota(I32, (WF, 1), 0)
    maskw = (keyw > tq) & (keyw + t0 - NSA_WINDOW >= 0)
    blocks = [jnp.maximum(2 * i - (NSA_WINDOW // NSA_SLC_BLOCK) + 2 * j, 0) for j in range(NSA_WINDOW // Q_BLOCK)]
    mexp = [expand(b // LANES, b % LANES, Q_BLOCK) for b in blocks]
    mask_sm = (jnp.concatenate(mexp[:-1], axis=0) > 0.5) & (keyw + t0 - NSA_WINDOW >= far_end)
    mask_sn = (jnp.concatenate([mexp[-1], expand(2 * i // LANES, 2 * i % LANES, Q_BLOCK)], axis=0) > 0.5) & near_ok
    s_sm = jnp.where(rep(mask_sm), kq(rw, WF), -jnp.inf)
    s_sn = jnp.where(rep(mask_sn), kq(rn, NEAR) + tb4, -jnp.inf)
    s_wf = jnp.where(rep(maskw), _dot(kw_ref[pl.ds(rw, WF), :], q4t), -jnp.inf)
    s_wn = jnp.where(rep(near_ok), _dot(kw_ref[pl.ds(rn, NEAR), :], q4t) + tb4, -jnp.inf)
    _flash_update(s_sm, vst_ref[:, pl.ds(rw, WF)], ms_ref, ls_ref, as_ref)
    _flash_update(s_sn, vst_ref[:, pl.ds(rn, NEAR)], ms_ref, ls_ref, as_ref)
    _flash_update(s_wf, vwt_ref[:, pl.ds(rw, WF)], mw_ref, lw_ref, aw_ref)
    _flash_update(s_wn, vwt_ref[:, pl.ds(rn, NEAR)], mw_ref, lw_ref, aw_ref)
    o_s = as_ref[...] / jnp.maximum(ls_ref[...], 1e-30)
    o_w = aw_ref[...] / jnp.maximum(lw_ref[...], 1e-30)

    gates = jax.nn.sigmoid(glt_ref[...])
    for r in range(R):
        hs = slice(r * HEAD_DIM, (r + 1) * HEAD_DIM)
        qs = slice(r * T, (r + 1) * T)
        o = (gates[3 * r:3 * r + 1, :] * oct_ref[hs, :]
             + gates[3 * r + 1:3 * r + 2, :] * o_s[:, qs]
             + gates[3 * r + 2:3 * r + 3, :] * o_w[:, qs])
        ot_ref[hs, :] = o.astype(BF16)


def _slcwin(nqt, ks, vst, kw, vwt, selt, tbt, oct, glt, S):
    G, R = NSA_KV_GROUPS, NSA_GROUP_SIZE
    spad = ks.shape[1]
    NW = selt.shape[1]
    once = pl.Buffered(1)
    kaspec = pl.BlockSpec((None, spad, KAUG_W), lambda g, i: (g, 0, 0), pipeline_mode=once)
    kspec = pl.BlockSpec((None, spad, HEAD_DIM), lambda g, i: (g, 0, 0), pipeline_mode=once)
    vspec = pl.BlockSpec((None, HEAD_DIM, spad), lambda g, i: (g, 0, 0), pipeline_mode=once)
    qspec = pl.BlockSpec((R * HEAD_DIM, Q_BLOCK), lambda g, i: (g, i))
    st = lambda n: pltpu.VMEM((n, R * Q_BLOCK), F32)
    return pl.pallas_call(
        _slcwin_body,
        grid=(G, S // Q_BLOCK),
        in_specs=[qspec, kaspec, vspec, kspec, vspec,
                  pl.BlockSpec((None, NW, LANES, Q_BLOCK), lambda g, i: (g, 0, 0, i)),
                  pl.BlockSpec((R, NEAR, Q_BLOCK), lambda g, i: (g, 0, 0)),
                  qspec,
                  pl.BlockSpec((None, 16, Q_BLOCK), lambda g, i: (g, 0, i))],
        out_specs=qspec,
        out_shape=jax.ShapeDtypeStruct((NSA_W, S), BF16),
        scratch_shapes=[st(1), st(1), st(HEAD_DIM), st(1), st(1), st(HEAD_DIM),
                        pltpu.VMEM((2, KEY_TILE, R * Q_BLOCK), F32),
                        pltpu.VMEM((NW, KAUG_W, R * Q_BLOCK), BF16)],
        compiler_params=_cparams(("arbitrary", "arbitrary")),
        name="slcwin",
    )(nqt, ks, vst, kw, vwt, selt, tbt, oct, glt)


INT_MIN = -2 ** 31
INT_MAX = 2 ** 31 - 1
MAX_PEEL_ROUNDS = 64
UNTESTED_BITS = 16
NEG_INF_KEY = int(np.array(-np.inf, np.float32).view(np.int32)) ^ 0x7FFFFFFF


def _dsa_body(dqt_ref, dk_ref, dvt_ref, iqt_ref, ikt_ref, iwt_ref, tbt_ref, ltri_ref, ot_ref,
              keys_ref, m_ref, l_ref, acc_ref, sbuf_ref, *, S, KEEP):
    i = pl.program_id(0)
    t0 = i * Q_BLOCK
    T = Q_BLOCK
    H = DSA_HEADS
    ntile = (t0 + Q_BLOCK + KEY_TILE - 1) // KEY_TILE
    tq = t0 + lax.broadcasted_iota(I32, (1, T), 1)
    key512 = lax.broadcasted_iota(I32, (KEY_TILE, 1), 0)
    ktile = lambda kt: keys_ref[pl.ds(pl.multiple_of(Q_BLOCK + kt * KEY_TILE, LANES), KEY_TILE), :]

    keys_ref[0:Q_BLOCK, :] = jnp.full((Q_BLOCK, T), NEG_INF_KEY, I32)
    iqt = iqt_ref[...]
    iwt = iwt_ref[...] * (IDX_HEADS ** -0.5)
    qis = [jnp.concatenate([iqt[(2 * j) * IDX_DIM:(2 * j + 1) * IDX_DIM, :],
                            iqt[(2 * j + 1) * IDX_DIM:(2 * j + 2) * IDX_DIM, :]], axis=1)
           for j in range(IDX_HEADS // 2)]
    wrow = [iwt[h:h + 1, :] for h in range(IDX_HEADS)]

    def idx_dots(kt, slot):
        ik = ikt_ref[:, pl.ds(pl.multiple_of(kt * KEY_TILE, KEY_TILE), KEY_TILE)]
        for j in range(IDX_HEADS // 2):
            sbuf_ref[slot, j] = _dot_tn(ik, qis[j])

    def idx_keys(kt, slot):
        c0 = pl.multiple_of(kt * KEY_TILE, KEY_TILE)
        sc = jnp.zeros((KEY_TILE, T), F32)
        for j in range(IDX_HEADS // 2):
            d = jnp.maximum(sbuf_ref[slot, j], 0.0)
            sc = sc + wrow[2 * j] * d[:, 0:T]
            sc = sc + wrow[2 * j + 1] * d[:, T:2 * T]
        sc = jnp.where(sc == 0.0, 0.0, sc)
        sc = jnp.where(c0 + key512 <= tq, sc, -jnp.inf)
        bits = pltpu.bitcast(sc, I32)
        keys_ref[pl.ds(pl.multiple_of(Q_BLOCK + c0, LANES), KEY_TILE), :] = jnp.where(
            bits < 0, bits ^ 0x7FFFFFFF, bits)

    idx_dots(0, 0)

    def score2(k2, carry):
        kt = 2 * k2
        idx_dots(kt + 1, 1)
        idx_keys(kt, 0)
        idx_dots(kt + 2, 0)
        idx_keys(kt + 1, 1)
        return carry

    lax.fori_loop(0, (ntile + 1) // 2, score2, 0)

    def count(pred):
        def body(kt, acc):
            hit = jnp.where(pred(ktile(kt), kt * KEY_TILE + key512), 1, 0)
            return acc + jnp.sum(hit.reshape(KEY_TILE // SUBLANES, SUBLANES, T), axis=0)
        acc = lax.fori_loop(0, ntile, body, jnp.zeros((SUBLANES, T), I32))
        return jnp.sum(acc, axis=0, keepdims=True)

    def min_at_least(lowest):
        def body(kt, acc):
            kk = ktile(kt)
            v = jnp.where(kk >= lowest, kk, INT_MAX)
            return jnp.minimum(acc, jnp.min(v.reshape(KEY_TILE // SUBLANES, SUBLANES, T), axis=0))
        acc = lax.fori_loop(0, ntile, body, jnp.full((SUBLANES, T), INT_MAX, I32))
        return jnp.min(acc, axis=0, keepdims=True)

    def any_open(st, over):
        _, excess, fin, _ = st
        return jnp.max(jnp.where((fin == 0) & (excess > over), 1, 0)) > 0

    def bisect(b, st):
        thr, excess, fin, need = st
        cand = thr + lax.shift_left(jnp.int32(1), 31 - b)
        n = count(lambda kk, ab: kk >= cand)
        ok = (n >= KEEP) & (fin == 0)
        return jnp.where(ok, cand, thr), jnp.where(ok, n - KEEP, excess), fin, need

    def peel(st):
        thr, excess, fin, need = st
        low = min_at_least(thr)
        n_low = count(lambda kk, ab: kk == low)
        active = (fin == 0) & (excess > 0)
        tied = active & (n_low > excess)
        drop = active & (n_low <= excess)
        return (jnp.where(tied, low, jnp.where(drop, low + 1, thr)), jnp.where(drop, excess - n_low, excess),
                jnp.where(tied, 1, fin), jnp.where(tied, n_low - excess, need))

    zero = jnp.zeros((1, T), I32)
    st = (jnp.full((1, T), INT_MIN, I32), KEY_TILE * ntile - KEEP + zero, zero, zero)
    st = peel(bisect(0, st))
    st = lax.fori_loop(1, UNTESTED_BITS, bisect, st)
    _, st = lax.while_loop(lambda c: (c[0] < 32) & any_open(c[1], 1),
                           lambda c: (c[0] + 1, bisect(c[0], c[1])), (jnp.int32(UNTESTED_BITS), st))
    _, st = lax.while_loop(lambda c: (c[0] < MAX_PEEL_ROUNDS) & any_open(c[1], 0),
                           lambda c: (c[0] + 1, peel(c[1])), (jnp.int32(0), st))
    thr, _, fin, need = st
    tie = (fin == 1) & (thr > NEG_INF_KEY)

    def tie_cut():
        needf = need.astype(F32)

        def body(kt, c):
            before, jmax = c
            is_tie = ktile(kt) == thr
            tied = jnp.where(is_tie, 1.0, 0.0)
            rank = before + _dot(ltri_ref[...], tied.astype(BF16))
            admit = is_tie & (rank < needf)
            jm = jnp.max(jnp.where(admit, kt * KEY_TILE + key512, -1), axis=0, keepdims=True)
            return before + jnp.sum(tied, axis=0, keepdims=True), jnp.maximum(jmax, jm)

        _, jmax = lax.fori_loop(0, ntile, body, (jnp.zeros((1, T), F32), jnp.full((1, T), -1, I32)))
        return jnp.where(tie, jmax + 1, S)

    jcut = lax.cond(jnp.max(jnp.where(tie, 1, 0)) > 0, tie_cut, lambda: jnp.full((1, T), S, I32))

    def chosen(kk, ab):
        return (kk > thr) | ((kk == thr) & (ab < jcut))

    _flash_init(m_ref, l_ref, acc_ref)
    lo = lax.broadcasted_iota(I32, (LANES, 1), 0) < HEAD_DIM
    qpairs = []
    for pr in range(H // 2):
        qp = dqt_ref[pr * LANES:(pr + 1) * LANES, :]
        zero = jnp.zeros_like(qp)
        qpairs.append(jnp.concatenate([jnp.where(lo, qp, zero), jnp.where(lo, zero, qp)], axis=1))

    def scores(r0, nkeys, mask, bias):
        mask2 = jnp.concatenate([mask, mask], axis=1)
        out = []
        for pr in range(H // 2):
            s = _dot(dk_ref[pl.ds(r0, nkeys), pr * LANES:(pr + 1) * LANES], qpairs[pr])
            if bias is not None:
                s = s + jnp.concatenate([bias[2 * pr], bias[2 * pr + 1]], axis=1)
            out.append(jnp.where(mask2, s, -jnp.inf))
        return out

    def update(pr, s, r0, nkeys):
        m_old = m_ref[pr]
        m_new = jnp.maximum(m_old, jnp.max(s, axis=0, keepdims=True))
        alpha = jnp.exp2(m_old - m_new)
        p = jnp.exp2(s - m_new)
        l_ref[pr] = alpha * l_ref[pr] + jnp.sum(p, axis=0, keepdims=True)
        m_ref[pr] = m_new
        vt = dvt_ref[pr * LANES:(pr + 1) * LANES, pl.ds(r0, nkeys)]
        acc_ref[pr] = alpha * acc_ref[pr] + _dot(vt, p.astype(BF16))

    far_end = t0 - Q_BLOCK
    nfar = jnp.maximum(far_end + KEY_TILE - 1, 0) // KEY_TILE
    far_row = lambda kt: pl.multiple_of(DSA_PADF + kt * KEY_TILE, LANES)

    def score_far(kt, slot):
        ab = kt * KEY_TILE + key512
        mask = chosen(ktile(jnp.minimum(kt, ntile - 1)), ab) & (ab < far_end)
        for pr, s in enumerate(scores(far_row(kt), KEY_TILE, mask, None)):
            sbuf_ref[slot, pr] = s

    def update_far(kt, slot):
        for pr in range(H // 2):
            update(pr, sbuf_ref[slot, pr], far_row(kt), KEY_TILE)

    score_far(0, 0)

    def far2(k2, carry):
        kt = 2 * k2
        score_far(kt + 1, 1)
        update_far(kt, 0)
        score_far(kt + 2, 0)
        update_far(kt + 1, 1)
        return carry

    lax.fori_loop(0, (nfar + 1) // 2, far2, 0)

    abn = lax.broadcasted_iota(I32, (NEAR, 1), 0) + t0 - Q_BLOCK
    kkn = keys_ref[pl.ds(pl.multiple_of(t0, LANES), NEAR), :]
    rn = pl.multiple_of(t0 + DSA_PADF - Q_BLOCK, LANES)
    for pr, s in enumerate(scores(rn, NEAR, chosen(kkn, abn) & (abn <= tq) & (abn >= 0), tbt_ref)):
        update(pr, s, rn, NEAR)

    for pr in range(H // 2):
        o = acc_ref[pr] / jnp.maximum(l_ref[pr], 1e-30)
        ot_ref[pr * LANES:(pr + 1) * LANES, :] = jnp.where(lo, o[:, 0:T], o[:, T:2 * T]).astype(BF16)


def _dsa(dqt, dkp, dvtp, iqt, ikt, iwt, tbt, ltri, S):
    KEEP = min(DSA_TOPK_MAX, S // 4)
    H = DSA_HEADS
    T = Q_BLOCK
    once = pl.Buffered(1)
    full = lambda a: pl.BlockSpec(a.shape, lambda i: (0,) * a.ndim, pipeline_mode=once)
    return pl.pallas_call(
        functools.partial(_dsa_body, S=S, KEEP=KEEP),
        grid=(S // T,),
        in_specs=[pl.BlockSpec((DSA_W, T), lambda i: (0, i)),
                  full(dkp), full(dvtp),
                  pl.BlockSpec((IDX_HEADS * IDX_DIM, T), lambda i: (0, i)),
                  full(ikt),
                  pl.BlockSpec((IDX_HEADS, T), lambda i: (0, i)),
                  full(tbt), full(ltri)],
        out_specs=pl.BlockSpec((DSA_W, T), lambda i: (0, i)),
        out_shape=jax.ShapeDtypeStruct((DSA_W, S), BF16),
        scratch_shapes=[pltpu.VMEM((Q_BLOCK + S + KEY_TILE, T), I32),
                        pltpu.VMEM((H // 2, 1, 2 * T), F32),
                        pltpu.VMEM((H // 2, 1, 2 * T), F32),
                        pltpu.VMEM((H // 2, LANES, 2 * T), F32),
                        pltpu.VMEM((2, H // 2, KEY_TILE, 2 * T), F32)],
        compiler_params=_cparams(("arbitrary",)),
        name="dsa",
    )(dqt, dkp, dvtp, iqt, ikt, iwt, tbt, ltri)


def _split_bf16(a):
    hi = a.astype(BF16)
    return hi, (a - hi.astype(F32)).astype(BF16)


def _merge_body(x_ref, oa_ref, ob_ref, gab_ref, wa_ref, wb_ref, wo_ref, gffn_ref, wr_ref, br_ref,
                x1_ref, h2_ref, ridx_ref, rw_ref):
    a = _dot_tn(oa_ref[...], wa_ref[...])
    b = _dot_tn(ob_ref[...], wb_ref[...])
    gab = gab_ref[...]
    merged = gab[:, :D_MODEL] * a + gab[:, D_MODEL:] * b
    x1 = x_ref[...] + _dot(merged.astype(BF16), wo_ref[...])
    x1_ref[...] = x1
    h2 = x1 * lax.rsqrt(jnp.mean(x1 * x1, axis=-1, keepdims=True) + EPS) * gffn_ref[...]
    h2_ref[...] = h2
    hh, hl = _split_bf16(h2)
    wh, wl = _split_bf16(wr_ref[...])
    logits = _dot(hh, wh) + _dot(hh, wl) + _dot(hl, wh) + br_ref[...]
    lane = lax.broadcasted_iota(I32, (1, LANES), 1)
    idx_out = jnp.zeros(logits.shape, I32)
    val_out = jnp.full(logits.shape, -jnp.inf, F32)
    for k in range(TOP_K):
        mx = jnp.max(logits, axis=-1, keepdims=True)
        first = jnp.min(jnp.where(logits == mx, lane, LANES), axis=-1, keepdims=True)
        idx_out = jnp.where(lane == k, first, idx_out)
        val_out = jnp.where(lane == k, mx, val_out)
        logits = jnp.where(lane == first, -jnp.inf, logits)
    e = jnp.exp(val_out - jnp.max(val_out, axis=-1, keepdims=True))
    ridx_ref[...] = idx_out
    rw_ref[...] = e / jnp.sum(e, axis=-1, keepdims=True)


def _merge(x2, oa, ob, gab, wa, wb, wo, gffn, wr, br, tm=512):
    S = x2.shape[0]
    row = lambda n: pl.BlockSpec((tm, n), lambda i: (i, 0))
    full = lambda a: pl.BlockSpec(a.shape, lambda i: (0,) * a.ndim)
    return pl.pallas_call(
        _merge_body,
        grid=(S // tm,),
        in_specs=[row(D_MODEL), pl.BlockSpec((NSA_W, tm), lambda i: (0, i)), pl.BlockSpec((DSA_W, tm), lambda i: (0, i)),
                  row(2 * D_MODEL),
                  full(wa), full(wb), full(wo), full(gffn), full(wr), full(br)],
        out_specs=[row(D_MODEL), row(D_MODEL), row(LANES), row(LANES)],
        out_shape=[jax.ShapeDtypeStruct((S, D_MODEL), F32), jax.ShapeDtypeStruct((S, D_MODEL), F32),
                   jax.ShapeDtypeStruct((S, LANES), I32), jax.ShapeDtypeStruct((S, LANES), F32)],
        compiler_params=_cparams(("arbitrary",)),
        name="merge",
    )(x2, oa, ob, gab, wa, wb, wo, gffn, wr, br)


def _rank_body(ridx_ref, rank_ref, cnt_ref, carry_ref):
    @pl.when(pl.program_id(0) == 0)
    def _():
        carry_ref[...] = jnp.zeros(carry_ref.shape, F32)

    ridx = ridx_ref[...]
    tm = ridx.shape[0]
    lane = lax.broadcasted_iota(I32, (1, LANES), 1)
    hits = [lane == ridx[:, k:k + 1] for k in range(TOP_K)]
    member = jnp.zeros((tm, LANES), F32)
    for hk in hits:
        member = member + jnp.where(hk, 1.0, 0.0)
    ri = lax.broadcasted_iota(I32, (tm, tm), 0)
    ci = lax.broadcasted_iota(I32, (tm, tm), 1)
    lower = jnp.where(ci < ri, 1.0, 0.0).astype(BF16)
    before = _dot(lower, member.astype(BF16)) + carry_ref[0:1, :]
    out = jnp.zeros((tm, LANES), I32)
    for k, hk in enumerate(hits):
        rk = jnp.sum(jnp.where(hk, before, 0.0), axis=-1, keepdims=True)
        out = jnp.where(lane == k, rk.astype(I32), out)
    rank_ref[...] = out
    total = carry_ref[0:1, :] + jnp.sum(member, axis=0, keepdims=True)
    carry_ref[...] = jnp.broadcast_to(total, carry_ref.shape)
    cnt_ref[...] = jnp.broadcast_to(total, cnt_ref.shape).astype(I32)


def _rank(ridx, tm=512):
    S = ridx.shape[0]
    return pl.pallas_call(
        _rank_body,
        grid=(S // tm,),
        in_specs=[pl.BlockSpec((tm, LANES), lambda i: (i, 0))],
        out_specs=[pl.BlockSpec((tm, LANES), lambda i: (i, 0)), pl.BlockSpec((8, LANES), lambda i: (0, 0))],
        out_shape=[jax.ShapeDtypeStruct((S, LANES), I32), jax.ShapeDtypeStruct((8, LANES), I32)],
        scratch_shapes=[pltpu.VMEM((8, LANES), F32)],
        compiler_params=_cparams(("arbitrary",)),
        name="rank",
    )(ridx)


def _dispatch_body(dest_ref, h_ref, zero_ref, xe_ref, sem):
    del zero_ref
    i = pl.program_id(0)
    tm = h_ref.shape[0]

    def start(r, c):
        for k in range(TOP_K):
            d = dest_ref[(i * tm + r) * TOP_K + k]
            pltpu.make_async_copy(h_ref.at[pl.ds(r, 1)], xe_ref.at[pl.ds(d, 1)],
                                  sem).start(priority=k % 2)
        return c

    lax.fori_loop(0, tm, start, 0, unroll=ROW_DMA_UNROLL)
    for _ in range(TOP_K):
        pltpu.make_async_copy(h_ref, xe_ref.at[pl.ds(0, tm)], sem).wait()


def _dispatch(dest, h2, zeros_p, tm=128):
    S = h2.shape[0]
    return pl.pallas_call(
        _dispatch_body,
        grid_spec=pltpu.PrefetchScalarGridSpec(
            num_scalar_prefetch=1,
            grid=(S // tm,),
            in_specs=[pl.BlockSpec((tm, D_MODEL), lambda i, d: (i, 0)),
                      pl.BlockSpec(memory_space=pl.ANY)],
            out_specs=pl.BlockSpec(memory_space=pl.ANY),
            scratch_shapes=[pltpu.SemaphoreType.DMA(())]),
        out_shape=jax.ShapeDtypeStruct(zeros_p.shape, zeros_p.dtype),
        input_output_aliases={2: 0},
        compiler_params=_cparams(("arbitrary",)),
        name="dispatch",
    )(dest, h2, zeros_p)


def _experts_body(ce_ref, nu_ref, x_ref, w1_ref, b1_ref, w2_ref, b2_ref, y_ref, w1b_ref, w2b_ref):
    c = pl.program_id(0)

    @pl.when((c == 0) | (ce_ref[c] != ce_ref[jnp.maximum(c - 1, 0)]))
    def _():
        w1b_ref[...] = w1_ref[...].astype(BF16)
        w2b_ref[...] = w2_ref[...].astype(BF16)

    @pl.when(c < nu_ref[0])
    def _():
        gu = _dot(x_ref[...].astype(BF16), w1b_ref[...]) + b1_ref[...]
        gate = jnp.minimum(gu[:, :D_FF], SWIGLU_LIMIT)
        lin = jnp.clip(gu[:, D_FF:], -SWIGLU_LIMIT, SWIGLU_LIMIT)
        act = (lin + 1.0) * gate * jax.nn.sigmoid(SWIGLU_ALPHA * gate)
        y_ref[...] = _dot(act.astype(BF16), w2b_ref[...]) + b2_ref[...]

    @pl.when(c >= nu_ref[0])
    def _():
        y_ref[...] = jnp.zeros(y_ref.shape, F32)


def _experts(chunk_e, n_used, xe, w1, b1, w2, b2):
    P = xe.shape[0]
    nch = P // MOE_ROWS
    return pl.pallas_call(
        _experts_body,
        grid_spec=pltpu.PrefetchScalarGridSpec(
            num_scalar_prefetch=2,
            grid=(nch,),
            in_specs=[pl.BlockSpec((MOE_ROWS, D_MODEL), lambda c, ce, nu: (c, 0)),
                      pl.BlockSpec((None, D_MODEL, 2 * D_FF), lambda c, ce, nu: (ce[c], 0, 0)),
                      pl.BlockSpec((None, 1, 2 * D_FF), lambda c, ce, nu: (ce[c], 0, 0)),
                      pl.BlockSpec((None, D_FF, D_MODEL), lambda c, ce, nu: (ce[c], 0, 0)),
                      pl.BlockSpec((None, 1, D_MODEL), lambda c, ce, nu: (ce[c], 0, 0))],
            out_specs=pl.BlockSpec((MOE_ROWS, D_MODEL), lambda c, ce, nu: (c, 0)),
            scratch_shapes=[pltpu.VMEM((D_MODEL, 2 * D_FF), BF16), pltpu.VMEM((D_FF, D_MODEL), BF16)]),
        out_shape=jax.ShapeDtypeStruct((P, D_MODEL), F32),
        compiler_params=_cparams(("arbitrary",)),
        name="experts",
    )(chunk_e, n_used, xe, w1, b1, w2, b2)


def _final_body(dest_ref, x1_ref, rw_ref, p_ref, gple_ref, wg_ref, wp_ref, y_ref, o_ref, rows_ref, sems):
    i = pl.program_id(0)
    tm = x1_ref.shape[0]

    def gather(blk, slot):
        def start(r, c):
            for k in range(TOP_K):
                d = dest_ref[(blk * tm + r) * TOP_K + k]
                pltpu.make_async_copy(y_ref.at[pl.ds(d, 1)], rows_ref.at[slot, k, pl.ds(r, 1)],
                                      sems.at[slot]).start(priority=k % 2)
            return c
        lax.fori_loop(0, tm, start, 0, unroll=ROW_DMA_UNROLL)

    @pl.when(i == 0)
    def _():
        gather(0, 0)

    @pl.when(i + 1 < pl.num_programs(0))
    def _():
        gather(i + 1, (i + 1) % 2)

    slot = i % 2
    for k in range(TOP_K):
        pltpu.make_async_copy(y_ref.at[pl.ds(0, tm)], rows_ref.at[slot, k], sems.at[slot]).wait()
    rw = rw_ref[...]
    x2 = x1_ref[...]
    for k in range(TOP_K):
        x2 = x2 + rw[:, k:k + 1] * rows_ref[slot, k]
    hn = x2 * lax.rsqrt(jnp.mean(x2 * x2, axis=-1, keepdims=True) + EPS) * gple_ref[...]
    gate = jax.nn.sigmoid(_dot(hn.astype(BF16), wg_ref[...]))
    o_ref[...] = x2 + gate * _dot(p_ref[...].astype(BF16), wp_ref[...])


def _final(dest, x1, rw, p2, gple, wg, wp, yexp, tm=128):
    S = x1.shape[0]
    row = lambda n: pl.BlockSpec((tm, n), lambda i, d: (i, 0))
    full = lambda a: pl.BlockSpec(a.shape, lambda i, d: (0,) * a.ndim)
    return pl.pallas_call(
        _final_body,
        grid_spec=pltpu.PrefetchScalarGridSpec(
            num_scalar_prefetch=1,
            grid=(S // tm,),
            in_specs=[row(D_MODEL), row(LANES), row(PLE_DIM), full(gple), full(wg), full(wp),
                      pl.BlockSpec(memory_space=pl.ANY)],
            out_specs=row(D_MODEL),
            scratch_shapes=[pltpu.VMEM((2, TOP_K, tm, D_MODEL), F32), pltpu.SemaphoreType.DMA((2,))]),
        out_shape=jax.ShapeDtypeStruct((S, D_MODEL), F32),
        compiler_params=_cparams(("arbitrary",)),
        name="final",
    )(dest, x1, rw, p2, gple, wg, wp, yexp)


def _distance_bias(rel_cols):
    return (rel_cols[_rel_bucket_table(NEAR)] - rel_cols[REL_BUCKETS - 1][None, :]).T * LOG2E


def _near_bias_t(rel_cols):
    h = rel_cols.shape[1]
    g = jnp.pad(_distance_bias(rel_cols), ((0, 0), (Q_BLOCK, Q_BLOCK)))
    period = 2 * NEAR
    skew = jnp.tile(g, (1, NEAR))[:, :NEAR * (period - 1)].reshape(h, NEAR, period - 1)
    return skew[:, :, NEAR:NEAR + Q_BLOCK]


def _cmp_bias_t(rel_cols):
    g = jnp.pad(_distance_bias(rel_cols), ((0, 0), (NEAR, Q_BLOCK)))
    rows = []
    for j in range(4):
        for k in range(4):
            c0 = -NSA_CMP_STRIDE * j - (NSA_CMP_LEN - 1) + NSA_SLC_BLOCK * (k - 1)
            rows.append(g[:, c0 + NEAR:c0 + NEAR + Q_BLOCK])
    return jnp.stack(rows, axis=1)


def kernel(x, p, w_in, g_mix, g_q_nsa, g_k_nsa, g_q_dsa, g_k_dsa, pe_ck, w_ck1, w_ck2, pe_cv, w_cv1, w_cv2,
           w_branch_a, w_branch_b, w_out, rel_bias, g_ffn, w_router, b_router, w_e1, b_e1, w_e2, b_e2,
           g_ple, w_ple, w_ple_gate):
    B, S, D = x.shape
    assert B == 1 and D == D_MODEL and S % KEY_TILE == 0 and w_in.shape[0] == 1
    G, R = NSA_KV_GROUPS, NSA_GROUP_SIZE
    x2 = x.reshape(S, D)

    w0 = w_in[0]
    w_nat = jnp.concatenate([w0[:, 512:768], w0[:, 768:896], w0[:, 1024:1152], w0[:, 1816:2328], w0[:, 3136:5184],
                             w0[:, 1280:1304], w0[:, 3096:3136], jnp.zeros((D, LANES - 64), F32)], axis=1).astype(BF16)
    w_tr = jnp.concatenate([w0[:, 0:512], w0[:, 1304:1816], w0[:, 2328:2840], w0[:, 2840:3096],
                            w0[:, 896:1024], w0[:, 1152:1280]], axis=1).T.astype(BF16)
    two = lambda g: jnp.tile(g.reshape(1, HEAD_DIM), (1, 2))
    qscale = HEAD_DIM ** -0.5 * LOG2E
    gains = jnp.concatenate([two(g_k_nsa[0]), two(g_k_dsa[0])], axis=0)
    gcol = jnp.stack([g_q_nsa[0] * qscale, g_q_dsa[0] * qscale]).reshape(2, HEAD_DIM, 1)
    (cmpraw, kslc, kwin, dk, gab, small, nqt, dqt, dvt, iqt, vst2, vwt2) = _inproj(
        x2, g_mix[0].reshape(1, D), w_nat, w_tr, gains, gcol)

    NS = S // NSA_SLC_BLOCK
    xc = cmpraw.reshape(2 * G, NS, 4 * NSA_CMP_STRIDE * HEAD_DIM)
    half = NSA_CMP_STRIDE * HEAD_DIM
    w1 = jnp.stack([w_ck1[0], w_cv1[0]]).reshape(2, 2 * half, HEAD_DIM)
    pe = jnp.stack([pe_ck[0], pe_cv[0]]).reshape(2, 2 * half, 1)
    w2 = jnp.stack([w_ck2[0], w_cv2[0]]).astype(BF16)
    kvc = _compress(xc, w1[:, :half].astype(BF16), w1[:, half:].astype(BF16), w1, pe, w2,
                    g_k_nsa[0].reshape(1, HEAD_DIM))

    rel_nsa = rel_bias[:, :NSA_HEADS]
    rel_dsa = rel_bias[:, NSA_HEADS:]
    oct, selt = _cmp(nqt, kvc[:G], kvc[G:].transpose(0, 2, 1), _cmp_bias_t(rel_nsa), S)
    kgroups = lambda a: jnp.pad(a, ((0, 0), (NSA_PADF, TAIL_PAD), (0, 0)))
    vgroups = lambda a: jnp.pad(a.reshape(G, HEAD_DIM, S), ((0, 0), (0, 0), (NSA_PADF, TAIL_PAD)))
    glt = small[:, SMALL_GATE:SMALL_GATE + 3 * NSA_HEADS].reshape(S, G, 3 * R).transpose(1, 2, 0)
    glt = jnp.pad(glt, ((0, 0), (0, 16 - 3 * R), (0, 0)))
    blk_lane = (np.arange(S) // NSA_SLC_BLOCK) % LANES
    onehot = jnp.asarray(np.eye(LANES, dtype=np.float32)[blk_lane] * MASK_BIG, BF16)
    kaug = jnp.concatenate([kslc,
                            jnp.broadcast_to(onehot, (G, S, LANES)),
                            jnp.zeros((G, S, KAUG_W - HEAD_DIM - LANES), BF16)], axis=-1)
    kaug = jnp.pad(kaug, ((0, 0), (NSA_PADF, TAIL_PAD), (0, 0)))
    o_at = _slcwin(nqt, kaug, vgroups(vst2), kgroups(kwin), vgroups(vwt2), selt,
                   _near_bias_t(rel_nsa), oct, glt, S)

    dkp = jnp.pad(dk, ((DSA_PADF, TAIL_PAD), (0, 0)))
    dvtp = jnp.pad(dvt, ((0, 0), (DSA_PADF, TAIL_PAD)))
    ikt = jnp.pad(small[:, SMALL_IK:SMALL_IK + IDX_DIM].T.astype(BF16), ((0, 0), (0, 2 * KEY_TILE)))
    iwt = small[:, SMALL_IW:SMALL_IW + IDX_HEADS].T
    ltri = jnp.asarray(np.tril(np.ones((KEY_TILE, KEY_TILE), np.float32), -1), BF16)
    o_bt = _dsa(dqt, dkp, dvtp, iqt, ikt, iwt, _near_bias_t(rel_dsa), ltri, S)

    wr = jnp.pad(w_router[0], ((0, 0), (0, LANES - N_EXPERTS)))
    br = jnp.pad(b_router[0].reshape(1, N_EXPERTS), ((0, 0), (0, LANES - N_EXPERTS)), constant_values=-1e30)
    x1, h2, ridx, rw = _merge(x2, o_at, o_bt, gab, w_branch_a[0].astype(BF16), w_branch_b[0].astype(BF16),
                              w_out[0].astype(BF16), g_ffn[0].reshape(1, D), wr, br)

    rank, cnt = _rank(ridx)
    counts = cnt[0, :N_EXPERTS]
    padded = (counts + MOE_ROWS - 1) // MOE_ROWS * MOE_ROWS
    pad_end = jnp.cumsum(padded)
    pad_start = pad_end - padded
    eidx = ridx[:, :TOP_K]
    dest = (jnp.sum(jnp.where(eidx[..., None] == jnp.arange(N_EXPERTS), pad_start, 0), axis=-1)
            + rank[:, :TOP_K]).reshape(-1).astype(I32)
    n_chunks = -(-(S * TOP_K) // MOE_ROWS) + N_EXPERTS
    chunk_start = jnp.arange(n_chunks, dtype=I32) * MOE_ROWS
    chunk_e = jnp.minimum(jnp.sum(chunk_start[:, None] >= pad_end[None, :], axis=-1), N_EXPERTS - 1).astype(I32)
    n_used = (pad_end[-1] // MOE_ROWS).astype(I32).reshape(1)
    xe = _dispatch(dest, h2, jnp.zeros((n_chunks * MOE_ROWS, D), F32))
    yexp = _experts(chunk_e, n_used, xe, w_e1[0], b_e1[0].reshape(N_EXPERTS, 1, 2 * D_FF),
                    w_e2[0], b_e2[0].reshape(N_EXPERTS, 1, D))

    out = _final(dest, x1, rw, p[0].reshape(S, PLE_DIM), g_ple[0].reshape(1, D),
                 w_ple_gate[0].astype(BF16), w_ple[0].astype(BF16), yexp)
    return out.reshape(B, S, D)
```

```python
import functools
import math

import numpy as np
import jax
import jax.numpy as jnp
from jax import lax
from jax.experimental import pallas as pl
from jax.experimental.pallas import tpu as pltpu

F32 = jnp.float32
BF16 = jnp.bfloat16
I32 = jnp.int32

D_MODEL = 1024
PLE_DIM = 256
HEAD_DIM = 64
NSA_HEADS = 8
NSA_KV_GROUPS = 2
NSA_GROUP_SIZE = NSA_HEADS // NSA_KV_GROUPS
NSA_CMP_LEN = 32
NSA_CMP_STRIDE = 16
NSA_SLC_BLOCK = 64
NSA_SLC_TOPK = 16
NSA_LOCAL_BLOCKS = 2
NSA_WINDOW = 512
DSA_HEADS = 8
IDX_HEADS = 8
IDX_DIM = 32
DSA_TOPK_MAX = 256
Q_BLOCK = 128
REL_BUCKETS = 32
REL_MAX_EXACT = 16
REL_MAX_DIST = 128
N_EXPERTS = 32
TOP_K = 4
D_FF = 1024
SWIGLU_LIMIT = 7.0
SWIGLU_ALPHA = 1.702
MOE_ROWS = 512
EPS = 1e-6
LOG2E = math.log2(math.e)

NSA_W = NSA_HEADS * HEAD_DIM
DSA_W = DSA_HEADS * HEAD_DIM

LANES = 128
SUBLANES = 8
KEY_TILE = 512
NEAR = 2 * Q_BLOCK
NSA_PADF = NSA_WINDOW
DSA_PADF = Q_BLOCK
KAUG_W = 2 * LANES
MASK_BIG = 2.0 ** 100
TAIL_PAD = 3 * KEY_TILE
VMEM_LIMIT = 56 * 1024 * 1024
ROW_DMA_UNROLL = 8

NAT_CMP, NAT_KSLC, NAT_KWIN, NAT_DK, NAT_GAB, NAT_SMALL = 0, 256, 384, 512, 1024, 3072
TR_NQ, TR_DQ, TR_DV, TR_IQ, TR_VSLC, TR_VWIN = 0, 512, 1024, 1536, 1792, 1920
SMALL_GATE, SMALL_IK, SMALL_IW = 0, 24, 56


def _rel_bucket_table(n):
    d = np.arange(n)
    nf = np.maximum(d, 1).astype(np.float64)
    large = REL_MAX_EXACT + (np.log(nf / REL_MAX_EXACT) / math.log(REL_MAX_DIST / REL_MAX_EXACT)
                             * (REL_BUCKETS - REL_MAX_EXACT)).astype(np.int64)
    large = np.minimum(large, REL_BUCKETS - 1)
    return np.where(d < REL_MAX_EXACT, d, large)


def _cparams(sem, vmem=VMEM_LIMIT):
    return pltpu.CompilerParams(dimension_semantics=sem, vmem_limit_bytes=vmem)


def _dot(a, b):
    return jnp.dot(a, b, preferred_element_type=F32)


def _dot_tn(a, b):
    return lax.dot_general(a, b, (((0,), (0,)), ((), ())), preferred_element_type=F32)


def _inproj_body(x_ref, gmix_ref, w_ref, wt_ref, gains_ref, gcol_ref,
                 cmp_ref, kslc_ref, kwin_ref, dk_ref, gab_ref, small_ref,
                 nqt_ref, dqt_ref, dvt_ref, iqt_ref, vst_ref, vwt_ref):
    x = x_ref[...]
    h = x * lax.rsqrt(jnp.mean(x * x, axis=-1, keepdims=True) + EPS) * gmix_ref[...]
    hb = h.astype(BF16)
    tm = hb.shape[0]
    lo = lax.broadcasted_iota(I32, (1, LANES), 1) < HEAD_DIM

    def mm(off, n):
        return _dot(hb, w_ref[:, off:off + n])

    def headnorm(y, g):
        y2 = y * y
        s_lo = jnp.sum(jnp.where(lo, y2, 0.0), axis=-1, keepdims=True)
        s_hi = jnp.sum(jnp.where(lo, 0.0, y2), axis=-1, keepdims=True)
        r = jnp.where(lo, lax.rsqrt(s_lo * (1.0 / HEAD_DIM) + EPS), lax.rsqrt(s_hi * (1.0 / HEAD_DIM) + EPS))
        return y * r * g

    def headnorm_t(y, g):
        y3 = y.reshape(y.shape[0] // HEAD_DIM, HEAD_DIM, tm)
        r = lax.rsqrt(jnp.mean(y3 * y3, axis=1, keepdims=True) + EPS)
        return (y3 * r * g).reshape(y.shape)

    g_kn, g_kd = gains_ref[0:1, :], gains_ref[1:2, :]
    def put_heads(ref, y):
        for k in range(ref.shape[0]):
            ref[k] = y[:, k * HEAD_DIM:(k + 1) * HEAD_DIM].astype(BF16)

    put_heads(cmp_ref, mm(NAT_CMP, 256))
    put_heads(kslc_ref, headnorm(mm(NAT_KSLC, LANES), g_kn))
    put_heads(kwin_ref, headnorm(mm(NAT_KWIN, LANES), g_kn))
    for c in range(DSA_W // LANES):
        dk_ref[:, c * LANES:(c + 1) * LANES] = headnorm(mm(NAT_DK + c * LANES, LANES), g_kd).astype(BF16)
    gab_ref[...] = jax.nn.sigmoid(mm(NAT_GAB, 2 * D_MODEL))
    small_ref[...] = mm(NAT_SMALL, LANES)

    def mm_t(off, n):
        return lax.dot_general(wt_ref[off:off + n, :], hb, (((1,), (1,)), ((), ())), preferred_element_type=F32)

    nqt_ref[...] = headnorm_t(mm_t(TR_NQ, NSA_W), gcol_ref[0]).astype(BF16)
    dqt_ref[...] = headnorm_t(mm_t(TR_DQ, DSA_W), gcol_ref[1]).astype(BF16)
    dvt_ref[...] = mm_t(TR_DV, DSA_W).astype(BF16)
    iqt_ref[...] = (mm_t(TR_IQ, IDX_HEADS * IDX_DIM) * (IDX_DIM ** -0.5)).astype(BF16)
    vst_ref[...] = mm_t(TR_VSLC, LANES).astype(BF16)
    vwt_ref[...] = mm_t(TR_VWIN, LANES).astype(BF16)


def _inproj(x2, g_mix, w_nat, w_tr, gains, gcol, tm=512):
    S = x2.shape[0]
    row = lambda n: pl.BlockSpec((tm, n), lambda i: (i, 0))
    col = lambda n: pl.BlockSpec((n, tm), lambda i: (0, i))
    full = lambda a: pl.BlockSpec(a.shape, lambda i: (0,) * a.ndim)
    heads = lambda n: pl.BlockSpec((n, tm, HEAD_DIM), lambda i: (0, i, 0))
    grouped = [2 * NSA_KV_GROUPS, NSA_KV_GROUPS, NSA_KV_GROUPS]
    nat = [(DSA_W, BF16), (2 * D_MODEL, F32), (LANES, F32)]
    tr = [NSA_W, DSA_W, DSA_W, IDX_HEADS * IDX_DIM, LANES, LANES]
    return pl.pallas_call(
        _inproj_body,
        grid=(S // tm,),
        in_specs=[row(D_MODEL), full(g_mix), full(w_nat), full(w_tr), full(gains), full(gcol)],
        out_specs=[heads(n) for n in grouped] + [row(n) for n, _ in nat] + [col(n) for n in tr],
        out_shape=[jax.ShapeDtypeStruct((n, S, HEAD_DIM), BF16) for n in grouped]
                  + [jax.ShapeDtypeStruct((S, n), dt) for n, dt in nat]
                  + [jax.ShapeDtypeStruct((n, S), BF16) for n in tr],
        compiler_params=_cparams(("arbitrary",)),
        name="inproj",
    )(x2, g_mix, w_nat, w_tr, gains, gcol)


def _compress_body(xa_ref, xb_ref, w1a_ref, w1b_ref, w1f_ref, pe_ref, w2_ref, gk_ref, o_ref):
    is_k = pl.program_id(0) < NSA_KV_GROUPS
    ns = xa_ref.shape[0]
    pe_term = jnp.sum(pe_ref[...] * w1f_ref[...], axis=0, keepdims=True)
    second = _dot(xb_ref[...], w1b_ref[...])
    second = jnp.where(pl.program_id(1) == 3, pltpu.roll(second, ns - 1, axis=0), second)
    pre = _dot(xa_ref[...], w1a_ref[...]) + second + pe_term
    hdn = pre * jax.nn.sigmoid(pre)
    out = _dot(hdn.astype(BF16), w2_ref[...])
    normed = out * lax.rsqrt(jnp.mean(out * out, axis=-1, keepdims=True) + EPS) * gk_ref[...]
    o_ref[...] = jnp.where(is_k, normed, out).astype(BF16)


def _compress(xc, w1a, w1b, w1f, pe, w2, gk):
    n4, ns, kdim = xc.shape[0], xc.shape[1], xc.shape[2] // 4
    G = NSA_KV_GROUPS
    kv = lambda i: i // G
    return pl.pallas_call(
        _compress_body,
        grid=(n4, 4),
        in_specs=[pl.BlockSpec((None, ns, kdim), lambda i, j: (i, 0, j)),
                  pl.BlockSpec((None, ns, kdim), lambda i, j: (i, 0, (j + 1) % 4)),
                  pl.BlockSpec((None, kdim, HEAD_DIM), lambda i, j: (kv(i), 0, 0)),
                  pl.BlockSpec((None, kdim, HEAD_DIM), lambda i, j: (kv(i), 0, 0)),
                  pl.BlockSpec((None, 2 * kdim, HEAD_DIM), lambda i, j: (kv(i), 0, 0)),
                  pl.BlockSpec((None, 2 * kdim, 1), lambda i, j: (kv(i), 0, 0)),
                  pl.BlockSpec((None, HEAD_DIM, HEAD_DIM), lambda i, j: (kv(i), 0, 0)),
                  pl.BlockSpec((1, HEAD_DIM), lambda i, j: (0, 0))],
        out_specs=pl.BlockSpec((None, ns, HEAD_DIM), lambda i, j: (i, j, 0)),
        out_shape=jax.ShapeDtypeStruct((n4, 4 * ns, HEAD_DIM), BF16),
        compiler_params=_cparams(("arbitrary", "arbitrary")),
        name="compress",
    )(xc, xc, w1a, w1b, w1f, pe, w2, gk)


def _cmp_body(qt_ref, kc_ref, vct_ref, bt_ref, oct_ref, selt_ref, s_ref, imp_ref, *, NS):
    i = pl.program_id(1)

    @pl.when((pl.program_id(0) == 0) & (i == 0))
    def _():
        imp_ref[...] = jnp.zeros(imp_ref.shape, F32)

    imp_prev = imp_ref[...]
    ib = jnp.minimum(i, pl.num_programs(1) - 2)
    t0 = ib * Q_BLOCK
    T = Q_BLOCK
    R = NSA_GROUP_SIZE
    WIN = 2 * SUBLANES

    def step(ne):
        NE = 4 * ne
        row = lax.broadcasted_iota(I32, (NE, 1), 0)
        plane = row // ne
        nblk = row - plane * ne
        cend = NSA_SLC_BLOCK * nblk + NSA_CMP_STRIDE * plane + (NSA_CMP_LEN - 1)
        tq = t0 + lax.broadcasted_iota(I32, (1, T), 1)
        mask = cend <= tq
        kc = jnp.concatenate([kc_ref[j * NS:j * NS + ne, :] for j in range(4)], axis=0)
        vct = jnp.concatenate([vct_ref[:, j * NS:j * NS + ne] for j in range(4)], axis=1)
        wstart = jnp.clip((2 * ib - 2) // SUBLANES * SUBLANES, 0, ne - WIN)
        wrow = lax.broadcasted_iota(I32, (WIN, 1), 0)
        psum = jnp.zeros((NE, T), F32)
        for r in range(R):
            s_ref[r, 0:NE, :] = _dot(kc, qt_ref[r * HEAD_DIM:(r + 1) * HEAD_DIM, :])
        for r in range(R):
            bt = bt_ref[r]
            for j in range(4):
                add = jnp.zeros((WIN, T), F32)
                for k in range(4):
                    n = 2 * ib + 1 - k
                    add = add + jnp.where((wrow == n - wstart) & (n >= 0), bt[4 * j + k:4 * j + k + 1, :], 0.0)
                sl = pl.ds(pl.multiple_of(j * ne + wstart, SUBLANES), WIN)
                s_ref[r, sl, :] = s_ref[r, sl, :] + add
            sc = jnp.where(mask, s_ref[r, 0:NE, :], -jnp.inf)
            m = jnp.maximum(jnp.max(sc, axis=0, keepdims=True), -1e30)
            e = jnp.exp2(sc - m)
            pc = e / jnp.maximum(jnp.sum(e, axis=0, keepdims=True), 1e-30)
            oct_ref[r * HEAD_DIM:(r + 1) * HEAD_DIM, :] = _dot(vct, pc.astype(BF16))
            psum = psum + pc
        p0, p1, p2, p3 = (psum[j * ne:(j + 1) * ne, :] for j in range(4))
        p3s = jnp.where(lax.broadcasted_iota(I32, (ne, 1), 0) == 0, 0.0, pltpu.roll(p3, 1, axis=0))
        imp_ref[0:ne, :] = p0 + p1 + p2 + 0.5 * p3 + 0.5 * p3s
        if ne < NS:
            imp_ref[ne:NS, :] = jnp.zeros((NS - ne, T), F32)

        n_ar = lax.broadcasted_iota(I32, (NS, 1), 0)
        tqs = (i - 1) * Q_BLOCK + lax.broadcasted_iota(I32, (1, T), 1)
        cur = tqs // NSA_SLC_BLOCK
        valid = NSA_SLC_BLOCK * n_ar <= tqs
        forced = valid & ((n_ar == 0) | (n_ar > cur - NSA_LOCAL_BLOCKS))
        score = jnp.where(forced, 1e30, jnp.where(valid, imp_prev, -1.0))
        sel = jnp.zeros((NS, T), jnp.bool_)
        for _ in range(min(NSA_SLC_TOPK, NS)):
            mx = jnp.max(score, axis=0, keepdims=True)
            first = jnp.min(jnp.where(score == mx, n_ar, NS), axis=0, keepdims=True)
            pick = n_ar == first
            sel = sel | pick
            score = jnp.where(pick, -2.0, score)
        selb = jnp.where(sel, 1.0, 0.0).astype(BF16)
        if NS < LANES:
            selb = jnp.concatenate([selb, jnp.zeros((LANES - NS, T), BF16)], axis=0)
        for w in range(selt_ref.shape[0]):
            selt_ref[w] = selb[w * LANES:(w + 1) * LANES, :]

    quarter = NS // 4
    if quarter % WIN == 0:
        which = (2 * ib + 2 + quarter - 1) // quarter - 1
        for v in range(4):
            pl.when(which == v)(functools.partial(step, (v + 1) * quarter))
    else:
        step(NS)


def _cmp(nqt, kc, vct, bt, S):
    NS = S // NSA_SLC_BLOCK
    NCP = 4 * NS
    NW = max(NS // LANES, 1)
    G, R = NSA_KV_GROUPS, NSA_GROUP_SIZE
    nb = S // Q_BLOCK
    return pl.pallas_call(
        functools.partial(_cmp_body, NS=NS),
        grid=(G, nb + 1),
        in_specs=[pl.BlockSpec((R * HEAD_DIM, Q_BLOCK), lambda g, i: (g, jnp.minimum(i, nb - 1))),
                  pl.BlockSpec((None, NCP, HEAD_DIM), lambda g, i: (g, 0, 0)),
                  pl.BlockSpec((None, HEAD_DIM, NCP), lambda g, i: (g, 0, 0)),
                  pl.BlockSpec((R, 16, Q_BLOCK), lambda g, i: (g, 0, 0))],
        out_specs=[pl.BlockSpec((R * HEAD_DIM, Q_BLOCK), lambda g, i: (g, jnp.minimum(i, nb - 1))),
                   pl.BlockSpec((None, NW, LANES, Q_BLOCK), lambda g, i: (g, 0, 0, jnp.maximum(i - 1, 0)))],
        out_shape=[jax.ShapeDtypeStruct((NSA_W, S), F32),
                   jax.ShapeDtypeStruct((G, NW, LANES, S), BF16)],
        scratch_shapes=[pltpu.VMEM((R, NCP, Q_BLOCK), F32), pltpu.VMEM((NS, Q_BLOCK), F32)],
        compiler_params=_cparams(("arbitrary", "arbitrary")),
        name="cmp",
    )(nqt, kc, vct, bt)


def _flash_init(m_ref, l_ref, acc_ref):
    m_ref[...] = jnp.full(m_ref.shape, -1e30, F32)
    l_ref[...] = jnp.zeros(l_ref.shape, F32)
    acc_ref[...] = jnp.zeros(acc_ref.shape, F32)


def _flash_update(s, vt, m_ref, l_ref, acc_ref):
    m_old = m_ref[...]
    m_new = jnp.maximum(m_old, jnp.max(s, axis=0, keepdims=True))
    alpha = jnp.exp2(m_old - m_new)
    p = jnp.exp2(s - m_new)
    l_ref[...] = alpha * l_ref[...] + jnp.sum(p, axis=0, keepdims=True)
    acc_ref[...] = alpha * acc_ref[...] + _dot(vt, p.astype(BF16))
    m_ref[...] = m_new


def _slcwin_body(qt_ref, ks_ref, vst_ref, kw_ref, vwt_ref, selt_ref, tbt_ref, oct_ref, glt_ref, ot_ref,
                 ms_ref, ls_ref, as_ref, mw_ref, lw_ref, aw_ref, sbuf_ref, qaug_ref):
    i = pl.program_id(1)
    t0 = i * Q_BLOCK
    R = NSA_GROUP_SIZE
    T = Q_BLOCK
    q4t = jnp.concatenate([qt_ref[r * HEAD_DIM:(r + 1) * HEAD_DIM, :] for r in range(R)], axis=1)
    tq = lax.broadcasted_iota(I32, (1, T), 1)
    rep = lambda mk: jnp.concatenate([mk] * R, axis=1)
    tb4 = jnp.concatenate([tbt_ref[r] for r in range(R)], axis=1)

    def expand(w, base, nkeys):
        keyi = lax.broadcasted_iota(I32, (nkeys, LANES), 0)
        blk = lax.broadcasted_iota(I32, (nkeys, LANES), 1)
        e = jnp.where(blk == base + keyi // NSA_SLC_BLOCK, 1.0, 0.0).astype(BF16)
        return _dot(e, selt_ref[w])

    _flash_init(ms_ref, ls_ref, as_ref)
    _flash_init(mw_ref, lw_ref, aw_ref)
    n_win = selt_ref.shape[0]
    zpad = jnp.zeros((KAUG_W - HEAD_DIM - LANES, R * T), BF16)
    for w in range(n_win):
        qaug_ref[w] = jnp.concatenate([q4t, rep(selt_ref[w] - jnp.asarray(1.0, BF16)), zpad], axis=0)
    far_end = jnp.maximum(t0 - Q_BLOCK, 0) // KEY_TILE * KEY_TILE
    nfar = far_end // KEY_TILE
    tiles_per_window = LANES * NSA_SLC_BLOCK // KEY_TILE

    def score_far(kt, slot):
        r0 = pl.multiple_of(NSA_PADF + kt * KEY_TILE, LANES)
        w = jnp.minimum(kt // tiles_per_window, n_win - 1)
        sbuf_ref[slot] = _dot(ks_ref[pl.ds(r0, KEY_TILE), :], qaug_ref[w])

    def update_far(kt, slot):
        r0 = pl.multiple_of(NSA_PADF + kt * KEY_TILE, LANES)
        _flash_update(sbuf_ref[slot], vst_ref[:, pl.ds(r0, KEY_TILE)], ms_ref, ls_ref, as_ref)

    score_far(0, 0)

    def far2(k2, carry):
        kt = 2 * k2
        score_far(kt + 1, 1)
        update_far(kt, 0)
        score_far(kt + 2, 0)
        update_far(kt + 1, 1)
        return carry

    lax.fori_loop(0, nfar // 2, far2, 0)

    @pl.when(nfar % 2 == 1)
    def _():
        update_far(nfar - 1, 0)

    kq = lambda r0, n: _dot(ks_ref[pl.ds(r0, n), 0:HEAD_DIM], q4t)
    keyn = lax.broadcasted_iota(I32, (NEAR, 1), 0)
    near_ok = (keyn - Q_BLOCK <= tq) & (keyn + t0 - Q_BLOCK >= 0)
    rn = pl.multiple_of(t0 + NSA_PADF - Q_BLOCK, LANES)
    WF = NSA_WINDOW - Q_BLOCK
    rw = pl.multiple_of(t0 + NSA_PADF - NSA_WINDOW, LANES)
    keyw = lax.broadcasted_iota(I32, (WF, 1), 0)
    maskw = (keyw > tq) & (keyw + t0 - NSA_WINDOW >= 0)
    blocks = [jnp.maximum(2 * i - (NSA_WINDOW // NSA_SLC_BLOCK) + 2 * j, 0) for j in range(NSA_WINDOW // Q_BLOCK)]
    mexp = [expand(b // LANES, b % LANES, Q_BLOCK) for b in blocks]
    mask_sm = (jnp.concatenate(mexp[:-1], axis=0) > 0.5) & (keyw + t0 - NSA_WINDOW >= far_end)
    mask_sn = (jnp.concatenate([mexp[-1], expand(2 * i // LANES, 2 * i % LANES, Q_BLOCK)], axis=0) > 0.5) & near_ok
    s_sm = jnp.where(rep(mask_sm), kq(rw, WF), -jnp.inf)
    s_sn = jnp.where(rep(mask_sn), kq(rn, NEAR) + tb4, -jnp.inf)
    s_wf = jnp.where(rep(maskw), _dot(kw_ref[pl.ds(rw, WF), :], q4t), -jnp.inf)
    s_wn = jnp.where(rep(near_ok), _dot(kw_ref[pl.ds(rn, NEAR), :], q4t) + tb4, -jnp.inf)
    _flash_update(s_sm, vst_ref[:, pl.ds(rw, WF)], ms_ref, ls_ref, as_ref)
    _flash_update(s_sn, vst_ref[:, pl.ds(rn, NEAR)], ms_ref, ls_ref, as_ref)
    _flash_update(s_wf, vwt_ref[:, pl.ds(rw, WF)], mw_ref, lw_ref, aw_ref)
    _flash_update(s_wn, vwt_ref[:, pl.ds(rn, NEAR)], mw_ref, lw_ref, aw_ref)
    o_s = as_ref[...] / jnp.maximum(ls_ref[...], 1e-30)
    o_w = aw_ref[...] / jnp.maximum(lw_ref[...], 1e-30)

    gates = jax.nn.sigmoid(glt_ref[...])
    for r in range(R):
        hs = slice(r * HEAD_DIM, (r + 1) * HEAD_DIM)
        qs = slice(r * T, (r + 1) * T)
        o = (gates[3 * r:3 * r + 1, :] * oct_ref[hs, :]
             + gates[3 * r + 1:3 * r + 2, :] * o_s[:, qs]
             + gates[3 * r + 2:3 * r + 3, :] * o_w[:, qs])
        ot_ref[hs, :] = o.astype(BF16)


def _slcwin(nqt, ks, vst, kw, vwt, selt, tbt, oct, glt, S):
    G, R = NSA_KV_GROUPS, NSA_GROUP_SIZE
    spad = ks.shape[1]
    NW = selt.shape[1]
    once = pl.Buffered(1)
    kaspec = pl.BlockSpec((None, spad, KAUG_W), lambda g, i: (g, 0, 0), pipeline_mode=once)
    kspec = pl.BlockSpec((None, spad, HEAD_DIM), lambda g, i: (g, 0, 0), pipeline_mode=once)
    vspec = pl.BlockSpec((None, HEAD_DIM, spad), lambda g, i: (g, 0, 0), pipeline_mode=once)
    qspec = pl.BlockSpec((R * HEAD_DIM, Q_BLOCK), lambda g, i: (g, i))
    st = lambda n: pltpu.VMEM((n, R * Q_BLOCK), F32)
    return pl.pallas_call(
        _slcwin_body,
        grid=(G, S // Q_BLOCK),
        in_specs=[qspec, kaspec, vspec, kspec, vspec,
                  pl.BlockSpec((None, NW, LANES, Q_BLOCK), lambda g, i: (g, 0, 0, i)),
                  pl.BlockSpec((R, NEAR, Q_BLOCK), lambda g, i: (g, 0, 0)),
                  qspec,
                  pl.BlockSpec((None, 16, Q_BLOCK), lambda g, i: (g, 0, i))],
        out_specs=qspec,
        out_shape=jax.ShapeDtypeStruct((NSA_W, S), BF16),
        scratch_shapes=[st(1), st(1), st(HEAD_DIM), st(1), st(1), st(HEAD_DIM),
                        pltpu.VMEM((2, KEY_TILE, R * Q_BLOCK), F32),
                        pltpu.VMEM((NW, KAUG_W, R * Q_BLOCK), BF16)],
        compiler_params=_cparams(("arbitrary", "arbitrary")),
        name="slcwin",
    )(nqt, ks, vst, kw, vwt, selt, tbt, oct, glt)


INT_MIN = -2 ** 31
INT_MAX = 2 ** 31 - 1
MAX_PEEL_ROUNDS = 64
UNTESTED_BITS = 16
NEG_INF_KEY = int(np.array(-np.inf, np.float32).view(np.int32)) ^ 0x7FFFFFFF


def _dsa_body(dqt_ref, dk_ref, dvt_ref, iqt_ref, ikt_ref, iwt_ref, tbt_ref, ltri_ref, ot_ref,
              keys_ref, m_ref, l_ref, acc_ref, sbuf_ref, *, S, KEEP):
    i = pl.program_id(0)
    t0 = i * Q_BLOCK
    T = Q_BLOCK
    H = DSA_HEADS
    ntile = (t0 + Q_BLOCK + KEY_TILE - 1) // KEY_TILE
    tq = t0 + lax.broadcasted_iota(I32, (1, T), 1)
    key512 = lax.broadcasted_iota(I32, (KEY_TILE, 1), 0)
    ktile = lambda kt: keys_ref[pl.ds(pl.multiple_of(Q_BLOCK + kt * KEY_TILE, LANES), KEY_TILE), :]

    keys_ref[0:Q_BLOCK, :] = jnp.full((Q_BLOCK, T), NEG_INF_KEY, I32)
    iqt = iqt_ref[...]
    iwt = iwt_ref[...] * (IDX_HEADS ** -0.5)
    qis = [jnp.concatenate([iqt[(2 * j) * IDX_DIM:(2 * j + 1) * IDX_DIM, :],
                            iqt[(2 * j + 1) * IDX_DIM:(2 * j + 2) * IDX_DIM, :]], axis=1)
           for j in range(IDX_HEADS // 2)]
    wrow = [iwt[h:h + 1, :] for h in range(IDX_HEADS)]

    def idx_dots(kt, slot):
        ik = ikt_ref[:, pl.ds(pl.multiple_of(kt * KEY_TILE, KEY_TILE), KEY_TILE)]
        for j in range(IDX_HEADS // 2):
            sbuf_ref[slot, j] = _dot_tn(ik, qis[j])

    def idx_keys(kt, slot):
        c0 = pl.multiple_of(kt * KEY_TILE, KEY_TILE)
        sc = jnp.zeros((KEY_TILE, T), F32)
        for j in range(IDX_HEADS // 2):
            d = jnp.maximum(sbuf_ref[slot, j], 0.0)
            sc = sc + wrow[2 * j] * d[:, 0:T]
            sc = sc + wrow[2 * j + 1] * d[:, T:2 * T]
        sc = jnp.where(sc == 0.0, 0.0, sc)
        sc = jnp.where(c0 + key512 <= tq, sc, -jnp.inf)
        bits = pltpu.bitcast(sc, I32)
        keys_ref[pl.ds(pl.multiple_of(Q_BLOCK + c0, LANES), KEY_TILE), :] = jnp.where(
            bits < 0, bits ^ 0x7FFFFFFF, bits)

    idx_dots(0, 0)

    def score2(k2, carry):
        kt = 2 * k2
        idx_dots(kt + 1, 1)
        idx_keys(kt, 0)
        idx_dots(kt + 2, 0)
        idx_keys(kt + 1, 1)
        return carry

    lax.fori_loop(0, (ntile + 1) // 2, score2, 0)

    def count(pred):
        def body(kt, acc):
            hit = jnp.where(pred(ktile(kt), kt * KEY_TILE + key512), 1, 0)
            return acc + jnp.sum(hit.reshape(KEY_TILE // SUBLANES, SUBLANES, T), axis=0)
        acc = lax.fori_loop(0, ntile, body, jnp.zeros((SUBLANES, T), I32))
        return jnp.sum(acc, axis=0, keepdims=True)

    def min_at_least(lowest):
        def body(kt, acc):
            kk = ktile(kt)
            v = jnp.where(kk >= lowest, kk, INT_MAX)
            return jnp.minimum(acc, jnp.min(v.reshape(KEY_TILE // SUBLANES, SUBLANES, T), axis=0))
        acc = lax.fori_loop(0, ntile, body, jnp.full((SUBLANES, T), INT_MAX, I32))
        return jnp.min(acc, axis=0, keepdims=True)

    def any_open(st, over):
        _, excess, fin, _ = st
        return jnp.max(jnp.where((fin == 0) & (excess > over), 1, 0)) > 0

    def bisect(b, st):
        thr, excess, fin, need = st
        cand = thr + lax.shift_left(jnp.int32(1), 31 - b)
        n = count(lambda kk, ab: kk >= cand)
        ok = (n >= KEEP) & (fin == 0)
        return jnp.where(ok, cand, thr), jnp.where(ok, n - KEEP, excess), fin, need

    def peel(st):
        thr, excess, fin, need = st
        low = min_at_least(thr)
        n_low = count(lambda kk, ab: kk == low)
        active = (fin == 0) & (excess > 0)
        tied = active & (n_low > excess)
        drop = active & (n_low <= excess)
        return (jnp.where(tied, low, jnp.where(drop, low + 1, thr)), jnp.where(drop, excess - n_low, excess),
                jnp.where(tied, 1, fin), jnp.where(tied, n_low - excess, need))

    zero = jnp.zeros((1, T), I32)
    st = (jnp.full((1, T), INT_MIN, I32), KEY_TILE * ntile - KEEP + zero, zero, zero)
    st = peel(bisect(0, st))
    st = lax.fori_loop(1, UNTESTED_BITS, bisect, st)
    _, st = lax.while_loop(lambda c: (c[0] < 32) & any_open(c[1], 1),
                           lambda c: (c[0] + 1, bisect(c[0], c[1])), (jnp.int32(UNTESTED_BITS), st))
    _, st = lax.while_loop(lambda c: (c[0] < MAX_PEEL_ROUNDS) & any_open(c[1], 0),
                           lambda c: (c[0] + 1, peel(c[1])), (jnp.int32(0), st))
    thr, _, fin, need = st
    tie = (fin == 1) & (thr > NEG_INF_KEY)

    def tie_cut():
        needf = need.astype(F32)

        def body(kt, c):
            before, jmax = c
            is_tie = ktile(kt) == thr
            tied = jnp.where(is_tie, 1.0, 0.0)
            rank = before + _dot(ltri_ref[...], tied.astype(BF16))
            admit = is_tie & (rank < needf)
            jm = jnp.max(jnp.where(admit, kt * KEY_TILE + key512, -1), axis=0, keepdims=True)
            return before + jnp.sum(tied, axis=0, keepdims=True), jnp.maximum(jmax, jm)

        _, jmax = lax.fori_loop(0, ntile, body, (jnp.zeros((1, T), F32), jnp.full((1, T), -1, I32)))
        return jnp.where(tie, jmax + 1, S)

    jcut = lax.cond(jnp.max(jnp.where(tie, 1, 0)) > 0, tie_cut, lambda: jnp.full((1, T), S, I32))

    def chosen(kk, ab):
        return (kk > thr) | ((kk == thr) & (ab < jcut))

    _flash_init(m_ref, l_ref, acc_ref)
    lo = lax.broadcasted_iota(I32, (LANES, 1), 0) < HEAD_DIM
    qpairs = []
    for pr in range(H // 2):
        qp = dqt_ref[pr * LANES:(pr + 1) * LANES, :]
        zero = jnp.zeros_like(qp)
        qpairs.append(jnp.concatenate([jnp.where(lo, qp, zero), jnp.where(lo, zero, qp)], axis=1))

    def scores(r0, nkeys, mask, bias):
        mask2 = jnp.concatenate([mask, mask], axis=1)
        out = []
        for pr in range(H // 2):
            s = _dot(dk_ref[pl.ds(r0, nkeys), pr * LANES:(pr + 1) * LANES], qpairs[pr])
            if bias is not None:
                s = s + jnp.concatenate([bias[2 * pr], bias[2 * pr + 1]], axis=1)
            out.append(jnp.where(mask2, s, -jnp.inf))
        return out

    def update(pr, s, r0, nkeys):
        m_old = m_ref[pr]
        m_new = jnp.maximum(m_old, jnp.max(s, axis=0, keepdims=True))
        alpha = jnp.exp2(m_old - m_new)
        p = jnp.exp2(s - m_new)
        l_ref[pr] = alpha * l_ref[pr] + jnp.sum(p, axis=0, keepdims=True)
        m_ref[pr] = m_new
        vt = dvt_ref[pr * LANES:(pr + 1) * LANES, pl.ds(r0, nkeys)]
        acc_ref[pr] = alpha * acc_ref[pr] + _dot(vt, p.astype(BF16))

    far_end = t0 - Q_BLOCK
    nfar = jnp.maximum(far_end + KEY_TILE - 1, 0) // KEY_TILE
    far_row = lambda kt: pl.multiple_of(DSA_PADF + kt * KEY_TILE, LANES)

    def score_far(kt, slot):
        ab = kt * KEY_TILE + key512
        mask = chosen(ktile(jnp.minimum(kt, ntile - 1)), ab) & (ab < far_end)
        for pr, s in enumerate(scores(far_row(kt), KEY_TILE, mask, None)):
            sbuf_ref[slot, pr] = s

    def update_far(kt, slot):
        for pr in range(H // 2):
            update(pr, sbuf_ref[slot, pr], far_row(kt), KEY_TILE)

    score_far(0, 0)

    def far2(k2, carry):
        kt = 2 * k2
        score_far(kt + 1, 1)
        update_far(kt, 0)
        score_far(kt + 2, 0)
        update_far(kt + 1, 1)
        return carry

    lax.fori_loop(0, (nfar + 1) // 2, far2, 0)

    abn = lax.broadcasted_iota(I32, (NEAR, 1), 0) + t0 - Q_BLOCK
    kkn = keys_ref[pl.ds(pl.multiple_of(t0, LANES), NEAR), :]
    rn = pl.multiple_of(t0 + DSA_PADF - Q_BLOCK, LANES)
    for pr, s in enumerate(scores(rn, NEAR, chosen(kkn, abn) & (abn <= tq) & (abn >= 0), tbt_ref)):
        update(pr, s, rn, NEAR)

    for pr in range(H // 2):
        o = acc_ref[pr] / jnp.maximum(l_ref[pr], 1e-30)
        ot_ref[pr * LANES:(pr + 1) * LANES, :] = jnp.where(lo, o[:, 0:T], o[:, T:2 * T]).astype(BF16)


def _dsa(dqt, dkp, dvtp, iqt, ikt, iwt, tbt, ltri, S):
    KEEP = min(DSA_TOPK_MAX, S // 4)
    H = DSA_HEADS
    T = Q_BLOCK
    once = pl.Buffered(1)
    full = lambda a: pl.BlockSpec(a.shape, lambda i: (0,) * a.ndim, pipeline_mode=once)
    return pl.pallas_call(
        functools.partial(_dsa_body, S=S, KEEP=KEEP),
        grid=(S // T,),
        in_specs=[pl.BlockSpec((DSA_W, T), lambda i: (0, i)),
                  full(dkp), full(dvtp),
                  pl.BlockSpec((IDX_HEADS * IDX_DIM, T), lambda i: (0, i)),
                  full(ikt),
                  pl.BlockSpec((IDX_HEADS, T), lambda i: (0, i)),
                  full(tbt), full(ltri)],
        out_specs=pl.BlockSpec((DSA_W, T), lambda i: (0, i)),
        out_shape=jax.ShapeDtypeStruct((DSA_W, S), BF16),
        scratch_shapes=[pltpu.VMEM((Q_BLOCK + S + KEY_TILE, T), I32),
                        pltpu.VMEM((H // 2, 1, 2 * T), F32),
                        pltpu.VMEM((H // 2, 1, 2 * T), F32),
                        pltpu.VMEM((H // 2, LANES, 2 * T), F32),
                        pltpu.VMEM((2, H // 2, KEY_TILE, 2 * T), F32)],
        compiler_params=_cparams(("arbitrary",)),
        name="dsa",
    )(dqt, dkp, dvtp, iqt, ikt, iwt, tbt, ltri)


def _split_bf16(a):
    hi = a.astype(BF16)
    return hi, (a - hi.astype(F32)).astype(BF16)


def _merge_body(x_ref, oa_ref, ob_ref, gab_ref, wa_ref, wb_ref, wo_ref, gffn_ref, wr_ref, br_ref,
                x1_ref, h2_ref, ridx_ref, rw_ref):
    a = _dot_tn(oa_ref[...], wa_ref[...])
    b = _dot_tn(ob_ref[...], wb_ref[...])
    gab = gab_ref[...]
    merged = gab[:, :D_MODEL] * a + gab[:, D_MODEL:] * b
    x1 = x_ref[...] + _dot(merged.astype(BF16), wo_ref[...])
    x1_ref[...] = x1
    h2 = x1 * lax.rsqrt(jnp.mean(x1 * x1, axis=-1, keepdims=True) + EPS) * gffn_ref[...]
    h2_ref[...] = h2
    hh, hl = _split_bf16(h2)
    wh, wl = _split_bf16(wr_ref[...])
    logits = _dot(hh, wh) + _dot(hh, wl) + _dot(hl, wh) + br_ref[...]
    lane = lax.broadcasted_iota(I32, (1, LANES), 1)
    idx_out = jnp.zeros(logits.shape, I32)
    val_out = jnp.full(logits.shape, -jnp.inf, F32)
    for k in range(TOP_K):
        mx = jnp.max(logits, axis=-1, keepdims=True)
        first = jnp.min(jnp.where(logits == mx, lane, LANES), axis=-1, keepdims=True)
        idx_out = jnp.where(lane == k, first, idx_out)
        val_out = jnp.where(lane == k, mx, val_out)
        logits = jnp.where(lane == first, -jnp.inf, logits)
    e = jnp.exp(val_out - jnp.max(val_out, axis=-1, keepdims=True))
    ridx_ref[...] = idx_out
    rw_ref[...] = e / jnp.sum(e, axis=-1, keepdims=True)


def _merge(x2, oa, ob, gab, wa, wb, wo, gffn, wr, br, tm=512):
    S = x2.shape[0]
    row = lambda n: pl.BlockSpec((tm, n), lambda i: (i, 0))
    full = lambda a: pl.BlockSpec(a.shape, lambda i: (0,) * a.ndim)
    return pl.pallas_call(
        _merge_body,
        grid=(S // tm,),
        in_specs=[row(D_MODEL), pl.BlockSpec((NSA_W, tm), lambda i: (0, i)), pl.BlockSpec((DSA_W, tm), lambda i: (0, i)),
                  row(2 * D_MODEL),
                  full(wa), full(wb), full(wo), full(gffn), full(wr), full(br)],
        out_specs=[row(D_MODEL), row(D_MODEL), row(LANES), row(LANES)],
        out_shape=[jax.ShapeDtypeStruct((S, D_MODEL), F32), jax.ShapeDtypeStruct((S, D_MODEL), F32),
                   jax.ShapeDtypeStruct((S, LANES), I32), jax.ShapeDtypeStruct((S, LANES), F32)],
        compiler_params=_cparams(("arbitrary",)),
        name="merge",
    )(x2, oa, ob, gab, wa, wb, wo, gffn, wr, br)


def _rank_body(ridx_ref, rank_ref, cnt_ref, carry_ref):
    @pl.when(pl.program_id(0) == 0)
    def _():
        carry_ref[...] = jnp.zeros(carry_ref.shape, F32)

    ridx = ridx_ref[...]
    tm = ridx.shape[0]
    lane = lax.broadcasted_iota(I32, (1, LANES), 1)
    hits = [lane == ridx[:, k:k + 1] for k in range(TOP_K)]
    member = jnp.zeros((tm, LANES), F32)
    for hk in hits:
        member = member + jnp.where(hk, 1.0, 0.0)
    ri = lax.broadcasted_iota(I32, (tm, tm), 0)
    ci = lax.broadcasted_iota(I32, (tm, tm), 1)
    lower = jnp.where(ci < ri, 1.0, 0.0).astype(BF16)
    before = _dot(lower, member.astype(BF16)) + carry_ref[0:1, :]
    out = jnp.zeros((tm, LANES), I32)
    for k, hk in enumerate(hits):
        rk = jnp.sum(jnp.where(hk, before, 0.0), axis=-1, keepdims=True)
        out = jnp.where(lane == k, rk.astype(I32), out)
    rank_ref[...] = out
    total = carry_ref[0:1, :] + jnp.sum(member, axis=0, keepdims=True)
    carry_ref[...] = jnp.broadcast_to(total, carry_ref.shape)
    cnt_ref[...] = jnp.broadcast_to(total, cnt_ref.shape).astype(I32)


def _rank(ridx, tm=512):
    S = ridx.shape[0]
    return pl.pallas_call(
        _rank_body,
        grid=(S // tm,),
        in_specs=[pl.BlockSpec((tm, LANES), lambda i: (i, 0))],
        out_specs=[pl.BlockSpec((tm, LANES), lambda i: (i, 0)), pl.BlockSpec((8, LANES), lambda i: (0, 0))],
        out_shape=[jax.ShapeDtypeStruct((S, LANES), I32), jax.ShapeDtypeStruct((8, LANES), I32)],
        scratch_shapes=[pltpu.VMEM((8, LANES), F32)],
        compiler_params=_cparams(("arbitrary",)),
        name="rank",
    )(ridx)


def _dispatch_body(dest_ref, h_ref, zero_ref, xe_ref, sem):
    del zero_ref
    i = pl.program_id(0)
    tm = h_ref.shape[0]

    def start(r, c):
        for k in range(TOP_K):
            d = dest_ref[(i * tm + r) * TOP_K + k]
            pltpu.make_async_copy(h_ref.at[pl.ds(r, 1)], xe_ref.at[pl.ds(d, 1)],
                                  sem).start(priority=k % 2)
        return c

    lax.fori_loop(0, tm, start, 0, unroll=ROW_DMA_UNROLL)
    for _ in range(TOP_K):
        pltpu.make_async_copy(h_ref, xe_ref.at[pl.ds(0, tm)], sem).wait()


def _dispatch(dest, h2, zeros_p, tm=128):
    S = h2.shape[0]
    return pl.pallas_call(
        _dispatch_body,
        grid_spec=pltpu.PrefetchScalarGridSpec(
            num_scalar_prefetch=1,
            grid=(S // tm,),
            in_specs=[pl.BlockSpec((tm, D_MODEL), lambda i, d: (i, 0)),
                      pl.BlockSpec(memory_space=pl.ANY)],
            out_specs=pl.BlockSpec(memory_space=pl.ANY),
            scratch_shapes=[pltpu.SemaphoreType.DMA(())]),
        out_shape=jax.ShapeDtypeStruct(zeros_p.shape, zeros_p.dtype),
        input_output_aliases={2: 0},
        compiler_params=_cparams(("arbitrary",)),
        name="dispatch",
    )(dest, h2, zeros_p)


def _experts_body(ce_ref, nu_ref, x_ref, w1_ref, b1_ref, w2_ref, b2_ref, y_ref, w1b_ref, w2b_ref):
    c = pl.program_id(0)

    @pl.when((c == 0) | (ce_ref[c] != ce_ref[jnp.maximum(c - 1, 0)]))
    def _():
        w1b_ref[...] = w1_ref[...].astype(BF16)
        w2b_ref[...] = w2_ref[...].astype(BF16)

    @pl.when(c < nu_ref[0])
    def _():
        gu = _dot(x_ref[...].astype(BF16), w1b_ref[...]) + b1_ref[...]
        gate = jnp.minimum(gu[:, :D_FF], SWIGLU_LIMIT)
        lin = jnp.clip(gu[:, D_FF:], -SWIGLU_LIMIT, SWIGLU_LIMIT)
        act = (lin + 1.0) * gate * jax.nn.sigmoid(SWIGLU_ALPHA * gate)
        y_ref[...] = _dot(act.astype(BF16), w2b_ref[...]) + b2_ref[...]

    @pl.when(c >= nu_ref[0])
    def _():
        y_ref[...] = jnp.zeros(y_ref.shape, F32)


def _experts(chunk_e, n_used, xe, w1, b1, w2, b2):
    P = xe.shape[0]
    nch = P // MOE_ROWS
    return pl.pallas_call(
        _experts_body,
        grid_spec=pltpu.PrefetchScalarGridSpec(
            num_scalar_prefetch=2,
            grid=(nch,),
            in_specs=[pl.BlockSpec((MOE_ROWS, D_MODEL), lambda c, ce, nu: (c, 0)),
                      pl.BlockSpec((None, D_MODEL, 2 * D_FF), lambda c, ce, nu: (ce[c], 0, 0)),
                      pl.BlockSpec((None, 1, 2 * D_FF), lambda c, ce, nu: (ce[c], 0, 0)),
                      pl.BlockSpec((None, D_FF, D_MODEL), lambda c, ce, nu: (ce[c], 0, 0)),
                      pl.BlockSpec((None, 1, D_MODEL), lambda c, ce, nu: (ce[c], 0, 0))],
            out_specs=pl.BlockSpec((MOE_ROWS, D_MODEL), lambda c, ce, nu: (c, 0)),
            scratch_shapes=[pltpu.VMEM((D_MODEL, 2 * D_FF), BF16), pltpu.VMEM((D_FF, D_MODEL), BF16)]),
        out_shape=jax.ShapeDtypeStruct((P, D_MODEL), F32),
        compiler_params=_cparams(("arbitrary",)),
        name="experts",
    )(chunk_e, n_used, xe, w1, b1, w2, b2)


def _final_body(dest_ref, x1_ref, rw_ref, p_ref, gple_ref, wg_ref, wp_ref, y_ref, o_ref, rows_ref, sems):
    i = pl.program_id(0)
    tm = x1_ref.shape[0]

    def gather(blk, slot):
        def start(r, c):
            for k in range(TOP_K):
                d = dest_ref[(blk * tm + r) * TOP_K + k]
                pltpu.make_async_copy(y_ref.at[pl.ds(d, 1)], rows_ref.at[slot, k, pl.ds(r, 1)],
                                      sems.at[slot]).start(priority=k % 2)
            return c
        lax.fori_loop(0, tm, start, 0, unroll=ROW_DMA_UNROLL)

    def drain(which):
        for k in range(TOP_K):
            pltpu.make_async_copy(y_ref.at[pl.ds(0, tm)], rows_ref.at[which, k], sems.at[which]).wait()

    @pl.when(i == 0)
    def _():
        gather(0, 0)

    slot = i % 2
    last = pl.num_programs(0) - 1
    drain(slot)
    nxt = jnp.minimum(i + 1, last)
    for r in range(tm):
        for k in range(TOP_K):
            d = dest_ref[(nxt * tm + r) * TOP_K + k]
            pltpu.make_async_copy(y_ref.at[pl.ds(d, 1)], rows_ref.at[1 - slot, k, pl.ds(r, 1)],
                                  sems.at[1 - slot]).start(priority=k % 2)
    rw = rw_ref[...]
    x2 = x1_ref[...]
    for k in range(TOP_K):
        x2 = x2 + rw[:, k:k + 1] * rows_ref[slot, k]
    hn = x2 * lax.rsqrt(jnp.mean(x2 * x2, axis=-1, keepdims=True) + EPS) * gple_ref[...]
    gate = jax.nn.sigmoid(_dot(hn.astype(BF16), wg_ref[...]))
    o_ref[...] = x2 + gate * _dot(p_ref[...].astype(BF16), wp_ref[...])

    @pl.when(i == last)
    def _():
        drain(1 - slot)


def _final(dest, x1, rw, p2, gple, wg, wp, yexp, tm=128):
    S = x1.shape[0]
    row = lambda n: pl.BlockSpec((tm, n), lambda i, d: (i, 0))
    full = lambda a: pl.BlockSpec(a.shape, lambda i, d: (0,) * a.ndim)
    return pl.pallas_call(
        _final_body,
        grid_spec=pltpu.PrefetchScalarGridSpec(
            num_scalar_prefetch=1,
            grid=(S // tm,),
            in_specs=[row(D_MODEL), row(LANES), row(PLE_DIM), full(gple), full(wg), full(wp),
                      pl.BlockSpec(memory_space=pl.ANY)],
            out_specs=row(D_MODEL),
            scratch_shapes=[pltpu.VMEM((2, TOP_K, tm, D_MODEL), F32), pltpu.SemaphoreType.DMA((2,))]),
        out_shape=jax.ShapeDtypeStruct((S, D_MODEL), F32),
        compiler_params=_cparams(("arbitrary",)),
        name="final",
    )(dest, x1, rw, p2, gple, wg, wp, yexp)


def _distance_bias(rel_cols):
    return (rel_cols[_rel_bucket_table(NEAR)] - rel_cols[REL_BUCKETS - 1][None, :]).T * LOG2E


def _near_bias_t(rel_cols):
    h = rel_cols.shape[1]
    g = jnp.pad(_distance_bias(rel_cols), ((0, 0), (Q_BLOCK, Q_BLOCK)))
    period = 2 * NEAR
    skew = jnp.tile(g, (1, NEAR))[:, :NEAR * (period - 1)].reshape(h, NEAR, period - 1)
    return skew[:, :, NEAR:NEAR + Q_BLOCK]


def _cmp_bias_t(rel_cols):
    g = jnp.pad(_distance_bias(rel_cols), ((0, 0), (NEAR, Q_BLOCK)))
    rows = []
    for j in range(4):
        for k in range(4):
            c0 = -NSA_CMP_STRIDE * j - (NSA_CMP_LEN - 1) + NSA_SLC_BLOCK * (k - 1)
            rows.append(g[:, c0 + NEAR:c0 + NEAR + Q_BLOCK])
    return jnp.stack(rows, axis=1)


def kernel(x, p, w_in, g_mix, g_q_nsa, g_k_nsa, g_q_dsa, g_k_dsa, pe_ck, w_ck1, w_ck2, pe_cv, w_cv1, w_cv2,
           w_branch_a, w_branch_b, w_out, rel_bias, g_ffn, w_router, b_router, w_e1, b_e1, w_e2, b_e2,
           g_ple, w_ple, w_ple_gate):
    B, S, D = x.shape
    assert B == 1 and D == D_MODEL and S % KEY_TILE == 0 and w_in.shape[0] == 1
    G, R = NSA_KV_GROUPS, NSA_GROUP_SIZE
    x2 = x.reshape(S, D)

    w0 = w_in[0]
    w_nat = jnp.concatenate([w0[:, 512:768], w0[:, 768:896], w0[:, 1024:1152], w0[:, 1816:2328], w0[:, 3136:5184],
                             w0[:, 1280:1304], w0[:, 3096:3136], jnp.zeros((D, LANES - 64), F32)], axis=1).astype(BF16)
    w_tr = jnp.concatenate([w0[:, 0:512], w0[:, 1304:1816], w0[:, 2328:2840], w0[:, 2840:3096],
                            w0[:, 896:1024], w0[:, 1152:1280]], axis=1).T.astype(BF16)
    two = lambda g: jnp.tile(g.reshape(1, HEAD_DIM), (1, 2))
    qscale = HEAD_DIM ** -0.5 * LOG2E
    gains = jnp.concatenate([two(g_k_nsa[0]), two(g_k_dsa[0])], axis=0)
    gcol = jnp.stack([g_q_nsa[0] * qscale, g_q_dsa[0] * qscale]).reshape(2, HEAD_DIM, 1)
    (cmpraw, kslc, kwin, dk, gab, small, nqt, dqt, dvt, iqt, vst2, vwt2) = _inproj(
        x2, g_mix[0].reshape(1, D), w_nat, w_tr, gains, gcol)

    NS = S // NSA_SLC_BLOCK
    xc = cmpraw.reshape(2 * G, NS, 4 * NSA_CMP_STRIDE * HEAD_DIM)
    half = NSA_CMP_STRIDE * HEAD_DIM
    w1 = jnp.stack([w_ck1[0], w_cv1[0]]).reshape(2, 2 * half, HEAD_DIM)
    pe = jnp.stack([pe_ck[0], pe_cv[0]]).reshape(2, 2 * half, 1)
    w2 = jnp.stack([w_ck2[0], w_cv2[0]]).astype(BF16)
    kvc = _compress(xc, w1[:, :half].astype(BF16), w1[:, half:].astype(BF16), w1, pe, w2,
                    g_k_nsa[0].reshape(1, HEAD_DIM))

    rel_nsa = rel_bias[:, :NSA_HEADS]
    rel_dsa = rel_bias[:, NSA_HEADS:]
    oct, selt = _cmp(nqt, kvc[:G], kvc[G:].transpose(0, 2, 1), _cmp_bias_t(rel_nsa), S)
    kgroups = lambda a: jnp.pad(a, ((0, 0), (NSA_PADF, TAIL_PAD), (0, 0)))
    vgroups = lambda a: jnp.pad(a.reshape(G, HEAD_DIM, S), ((0, 0), (0, 0), (NSA_PADF, TAIL_PAD)))
    glt = small[:, SMALL_GATE:SMALL_GATE + 3 * NSA_HEADS].reshape(S, G, 3 * R).transpose(1, 2, 0)
    glt = jnp.pad(glt, ((0, 0), (0, 16 - 3 * R), (0, 0)))
    blk_lane = (np.arange(S) // NSA_SLC_BLOCK) % LANES
    onehot = jnp.asarray(np.eye(LANES, dtype=np.float32)[blk_lane] * MASK_BIG, BF16)
    kaug = jnp.concatenate([kslc,
                            jnp.broadcast_to(onehot, (G, S, LANES)),
                            jnp.zeros((G, S, KAUG_W - HEAD_DIM - LANES), BF16)], axis=-1)
    kaug = jnp.pad(kaug, ((0, 0), (NSA_PADF, TAIL_PAD), (0, 0)))
    o_at = _slcwin(nqt, kaug, vgroups(vst2), kgroups(kwin), vgroups(vwt2), selt,
                   _near_bias_t(rel_nsa), oct, glt, S)

    dkp = jnp.pad(dk, ((DSA_PADF, TAIL_PAD), (0, 0)))
    dvtp = jnp.pad(dvt, ((0, 0), (DSA_PADF, TAIL_PAD)))
    ikt = jnp.pad(small[:, SMALL_IK:SMALL_IK + IDX_DIM].T.astype(BF16), ((0, 0), (0, 2 * KEY_TILE)))
    iwt = small[:, SMALL_IW:SMALL_IW + IDX_HEADS].T
    ltri = jnp.asarray(np.tril(np.ones((KEY_TILE, KEY_TILE), np.float32), -1), BF16)
    o_bt = _dsa(dqt, dkp, dvtp, iqt, ikt, iwt, _near_bias_t(rel_dsa), ltri, S)

    wr = jnp.pad(w_router[0], ((0, 0), (0, LANES - N_EXPERTS)))
    br = jnp.pad(b_router[0].reshape(1, N_EXPERTS), ((0, 0), (0, LANES - N_EXPERTS)), constant_values=-1e30)
    x1, h2, ridx, rw = _merge(x2, o_at, o_bt, gab, w_branch_a[0].astype(BF16), w_branch_b[0].astype(BF16),
                              w_out[0].astype(BF16), g_ffn[0].reshape(1, D), wr, br)

    rank, cnt = _rank(ridx)
    counts = cnt[0, :N_EXPERTS]
    padded = (counts + MOE_ROWS - 1) // MOE_ROWS * MOE_ROWS
    pad_end = jnp.cumsum(padded)
    pad_start = pad_end - padded
    eidx = ridx[:, :TOP_K]
    dest = (jnp.sum(jnp.where(eidx[..., None] == jnp.arange(N_EXPERTS), pad_start, 0), axis=-1)
            + rank[:, :TOP_K]).reshape(-1).astype(I32)
    n_chunks = -(-(S * TOP_K) // MOE_ROWS) + N_EXPERTS
    chunk_start = jnp.arange(n_chunks, dtype=I32) * MOE_ROWS
    chunk_e = jnp.minimum(jnp.sum(chunk_start[:, None] >= pad_end[None, :], axis=-1), N_EXPERTS - 1).astype(I32)
    n_used = (pad_end[-1] // MOE_ROWS).astype(I32).reshape(1)
    xe = _dispatch(dest, h2, jnp.zeros((n_chunks * MOE_ROWS, D), F32))
    yexp = _experts(chunk_e, n_used, xe, w_e1[0], b_e1[0].reshape(N_EXPERTS, 1, 2 * D_FF),
                    w_e2[0], b_e2[0].reshape(N_EXPERTS, 1, D))

    out = _final(dest, x1, rw, p[0].reshape(S, PLE_DIM), g_ple[0].reshape(1, D),
                 w_ple_gate[0].astype(BF16), w_ple[0].astype(BF16), yexp)
    return out.reshape(B, S, D)
```

```python
import functools
import math

import numpy as np
import jax
import jax.numpy as jnp
from jax import lax
from jax.experimental import pallas as pl
from jax.experimental.pallas import tpu as pltpu

F32 = jnp.float32
BF16 = jnp.bfloat16
I32 = jnp.int32

D_MODEL = 1024
PLE_DIM = 256
HEAD_DIM = 64
NSA_HEADS = 8
NSA_KV_GROUPS = 2
NSA_GROUP_SIZE = NSA_HEADS // NSA_KV_GROUPS
NSA_CMP_LEN = 32
NSA_CMP_STRIDE = 16
NSA_SLC_BLOCK = 64
NSA_SLC_TOPK = 16
NSA_LOCAL_BLOCKS = 2
NSA_WINDOW = 512
DSA_HEADS = 8
IDX_HEADS = 8
IDX_DIM = 32
DSA_TOPK_MAX = 256
Q_BLOCK = 128
REL_BUCKETS = 32
REL_MAX_EXACT = 16
REL_MAX_DIST = 128
N_EXPERTS = 32
TOP_K = 4
D_FF = 1024
SWIGLU_LIMIT = 7.0
SWIGLU_ALPHA = 1.702
MOE_ROWS = 512
EPS = 1e-6
LOG2E = math.log2(math.e)

NSA_W = NSA_HEADS * HEAD_DIM
DSA_W = DSA_HEADS * HEAD_DIM

LANES = 128
SUBLANES = 8
KEY_TILE = 512
NEAR = 2 * Q_BLOCK
NSA_PADF = NSA_WINDOW
DSA_PADF = Q_BLOCK
KAUG_W = 2 * LANES
MASK_BIG = 2.0 ** 100
TAIL_PAD = 3 * KEY_TILE
VMEM_LIMIT = 56 * 1024 * 1024
ROW_DMA_UNROLL = 8

NAT_CMP, NAT_KSLC, NAT_KWIN, NAT_DK, NAT_GAB, NAT_SMALL = 0, 256, 384, 512, 1024, 3072
TR_NQ, TR_DQ, TR_DV, TR_IQ, TR_VSLC, TR_VWIN = 0, 512, 1024, 1536, 1792, 1920
SMALL_GATE, SMALL_IK, SMALL_IW = 0, 24, 56


def _rel_bucket_table(n):
    d = np.arange(n)
    nf = np.maximum(d, 1).astype(np.float64)
    large = REL_MAX_EXACT + (np.log(nf / REL_MAX_EXACT) / math.log(REL_MAX_DIST / REL_MAX_EXACT)
                             * (REL_BUCKETS - REL_MAX_EXACT)).astype(np.int64)
    large = np.minimum(large, REL_BUCKETS - 1)
    return np.where(d < REL_MAX_EXACT, d, large)


def _cparams(sem, vmem=VMEM_LIMIT):
    return pltpu.CompilerParams(dimension_semantics=sem, vmem_limit_bytes=vmem)


def _dot(a, b):
    return jnp.dot(a, b, preferred_element_type=F32)


def _dot_tn(a, b):
    return lax.dot_general(a, b, (((0,), (0,)), ((), ())), preferred_element_type=F32)


def _inproj_body(x_ref, gmix_ref, w_ref, wt_ref, gains_ref, gcol_ref,
                 cmp_ref, kslc_ref, kwin_ref, dk_ref, gab_ref, small_ref,
                 nqt_ref, dqt_ref, dvt_ref, iqt_ref, vst_ref, vwt_ref):
    x = x_ref[...]
    h = x * lax.rsqrt(jnp.mean(x * x, axis=-1, keepdims=True) + EPS) * gmix_ref[...]
    hb = h.astype(BF16)
    tm = hb.shape[0]
    lo = lax.broadcasted_iota(I32, (1, LANES), 1) < HEAD_DIM

    def mm(off, n):
        return _dot(hb, w_ref[:, off:off + n])

    def headnorm(y, g):
        y2 = y * y
        s_lo = jnp.sum(jnp.where(lo, y2, 0.0), axis=-1, keepdims=True)
        s_hi = jnp.sum(jnp.where(lo, 0.0, y2), axis=-1, keepdims=True)
        r = jnp.where(lo, lax.rsqrt(s_lo * (1.0 / HEAD_DIM) + EPS), lax.rsqrt(s_hi * (1.0 / HEAD_DIM) + EPS))
        return y * r * g

    def headnorm_t(y, g):
        y3 = y.reshape(y.shape[0] // HEAD_DIM, HEAD_DIM, tm)
        r = lax.rsqrt(jnp.mean(y3 * y3, axis=1, keepdims=True) + EPS)
        return (y3 * r * g).reshape(y.shape)

    g_kn, g_kd = gains_ref[0:1, :], gains_ref[1:2, :]
    def put_heads(ref, y):
        for k in range(ref.shape[0]):
            ref[k] = y[:, k * HEAD_DIM:(k + 1) * HEAD_DIM].astype(BF16)

    put_heads(cmp_ref, mm(NAT_CMP, 256))
    put_heads(kslc_ref, headnorm(mm(NAT_KSLC, LANES), g_kn))
    put_heads(kwin_ref, headnorm(mm(NAT_KWIN, LANES), g_kn))
    for c in range(DSA_W // LANES):
        dk_ref[:, c * LANES:(c + 1) * LANES] = headnorm(mm(NAT_DK + c * LANES, LANES), g_kd).astype(BF16)
    gab_ref[...] = jax.nn.sigmoid(mm(NAT_GAB, 2 * D_MODEL))
    small_ref[...] = mm(NAT_SMALL, LANES)

    def mm_t(off, n):
        return lax.dot_general(wt_ref[off:off + n, :], hb, (((1,), (1,)), ((), ())), preferred_element_type=F32)

    nqt_ref[...] = headnorm_t(mm_t(TR_NQ, NSA_W), gcol_ref[0]).astype(BF16)
    dqt_ref[...] = headnorm_t(mm_t(TR_DQ, DSA_W), gcol_ref[1]).astype(BF16)
    dvt_ref[...] = mm_t(TR_DV, DSA_W).astype(BF16)
    iqt_ref[...] = (mm_t(TR_IQ, IDX_HEADS * IDX_DIM) * (IDX_DIM ** -0.5)).astype(BF16)
    vst_ref[...] = mm_t(TR_VSLC, LANES).astype(BF16)
    vwt_ref[...] = mm_t(TR_VWIN, LANES).astype(BF16)


def _inproj(x2, g_mix, w_nat, w_tr, gains, gcol, tm=512):
    S = x2.shape[0]
    row = lambda n: pl.BlockSpec((tm, n), lambda i: (i, 0))
    col = lambda n: pl.BlockSpec((n, tm), lambda i: (0, i))
    full = lambda a: pl.BlockSpec(a.shape, lambda i: (0,) * a.ndim)
    heads = lambda n: pl.BlockSpec((n, tm, HEAD_DIM), lambda i: (0, i, 0))
    grouped = [2 * NSA_KV_GROUPS, NSA_KV_GROUPS, NSA_KV_GROUPS]
    nat = [(DSA_W, BF16), (2 * D_MODEL, F32), (LANES, F32)]
    tr = [NSA_W, DSA_W, DSA_W, IDX_HEADS * IDX_DIM, LANES, LANES]
    return pl.pallas_call(
        _inproj_body,
        grid=(S // tm,),
        in_specs=[row(D_MODEL), full(g_mix), full(w_nat), full(w_tr), full(gains), full(gcol)],
        out_specs=[heads(n) for n in grouped] + [row(n) for n, _ in nat] + [col(n) for n in tr],
        out_shape=[jax.ShapeDtypeStruct((n, S, HEAD_DIM), BF16) for n in grouped]
                  + [jax.ShapeDtypeStruct((S, n), dt) for n, dt in nat]
                  + [jax.ShapeDtypeStruct((n, S), BF16) for n in tr],
        compiler_params=_cparams(("arbitrary",)),
        name="inproj",
    )(x2, g_mix, w_nat, w_tr, gains, gcol)


def _compress_body(xa_ref, xb_ref, w1a_ref, w1b_ref, w1f_ref, pe_ref, w2_ref, gk_ref, o_ref):
    is_k = pl.program_id(0) < NSA_KV_GROUPS
    ns = xa_ref.shape[0]
    pe_term = jnp.sum(pe_ref[...] * w1f_ref[...], axis=0, keepdims=True)
    second = _dot(xb_ref[...], w1b_ref[...])
    second = jnp.where(pl.program_id(1) == 3, pltpu.roll(second, ns - 1, axis=0), second)
    pre = _dot(xa_ref[...], w1a_ref[...]) + second + pe_term
    hdn = pre * jax.nn.sigmoid(pre)
    out = _dot(hdn.astype(BF16), w2_ref[...])
    normed = out * lax.rsqrt(jnp.mean(out * out, axis=-1, keepdims=True) + EPS) * gk_ref[...]
    o_ref[...] = jnp.where(is_k, normed, out).astype(BF16)


def _compress(xc, w1a, w1b, w1f, pe, w2, gk):
    n4, ns, kdim = xc.shape[0], xc.shape[1], xc.shape[2] // 4
    G = NSA_KV_GROUPS
    kv = lambda i: i // G
    return pl.pallas_call(
        _compress_body,
        grid=(n4, 4),
        in_specs=[pl.BlockSpec((None, ns, kdim), lambda i, j: (i, 0, j)),
                  pl.BlockSpec((None, ns, kdim), lambda i, j: (i, 0, (j + 1) % 4)),
                  pl.BlockSpec((None, kdim, HEAD_DIM), lambda i, j: (kv(i), 0, 0)),
                  pl.BlockSpec((None, kdim, HEAD_DIM), lambda i, j: (kv(i), 0, 0)),
                  pl.BlockSpec((None, 2 * kdim, HEAD_DIM), lambda i, j: (kv(i), 0, 0)),
                  pl.BlockSpec((None, 2 * kdim, 1), lambda i, j: (kv(i), 0, 0)),
                  pl.BlockSpec((None, HEAD_DIM, HEAD_DIM), lambda i, j: (kv(i), 0, 0)),
                  pl.BlockSpec((1, HEAD_DIM), lambda i, j: (0, 0))],
        out_specs=pl.BlockSpec((None, ns, HEAD_DIM), lambda i, j: (i, j, 0)),
        out_shape=jax.ShapeDtypeStruct((n4, 4 * ns, HEAD_DIM), BF16),
        compiler_params=_cparams(("arbitrary", "arbitrary")),
        name="compress",
    )(xc, xc, w1a, w1b, w1f, pe, w2, gk)


def _cmp_body(qt_ref, kc_ref, vct_ref, bt_ref, oct_ref, selt_ref, s_ref, imp_ref, *, NS):
    i = pl.program_id(1)

    @pl.when((pl.program_id(0) == 0) & (i == 0))
    def _():
        imp_ref[...] = jnp.zeros(imp_ref.shape, F32)

    imp_prev = imp_ref[...]
    ib = jnp.minimum(i, pl.num_programs(1) - 2)
    t0 = ib * Q_BLOCK
    T = Q_BLOCK
    R = NSA_GROUP_SIZE
    WIN = 2 * SUBLANES

    def step(ne):
        NE = 4 * ne
        row = lax.broadcasted_iota(I32, (NE, 1), 0)
        plane = row // ne
        nblk = row - plane * ne
        cend = NSA_SLC_BLOCK * nblk + NSA_CMP_STRIDE * plane + (NSA_CMP_LEN - 1)
        tq = t0 + lax.broadcasted_iota(I32, (1, T), 1)
        mask = cend <= tq
        kc = jnp.concatenate([kc_ref[j * NS:j * NS + ne, :] for j in range(4)], axis=0)
        vct = jnp.concatenate([vct_ref[:, j * NS:j * NS + ne] for j in range(4)], axis=1)
        wstart = jnp.clip((2 * ib - 2) // SUBLANES * SUBLANES, 0, ne - WIN)
        wrow = lax.broadcasted_iota(I32, (WIN, 1), 0)
        psum = jnp.zeros((NE, T), F32)
        for r in range(R):
            s_ref[r, 0:NE, :] = _dot(kc, qt_ref[r * HEAD_DIM:(r + 1) * HEAD_DIM, :])
        for r in range(R):
            bt = bt_ref[r]
            for j in range(4):
                add = jnp.zeros((WIN, T), F32)
                for k in range(4):
                    n = 2 * ib + 1 - k
                    add = add + jnp.where((wrow == n - wstart) & (n >= 0), bt[4 * j + k:4 * j + k + 1, :], 0.0)
                sl = pl.ds(pl.multiple_of(j * ne + wstart, SUBLANES), WIN)
                s_ref[r, sl, :] = s_ref[r, sl, :] + add
            sc = jnp.where(mask, s_ref[r, 0:NE, :], -jnp.inf)
            m = jnp.maximum(jnp.max(sc, axis=0, keepdims=True), -1e30)
            e = jnp.exp2(sc - m)
            pc = e / jnp.maximum(jnp.sum(e, axis=0, keepdims=True), 1e-30)
            oct_ref[r * HEAD_DIM:(r + 1) * HEAD_DIM, :] = _dot(vct, pc.astype(BF16))
            psum = psum + pc
        p0, p1, p2, p3 = (psum[j * ne:(j + 1) * ne, :] for j in range(4))
        p3s = jnp.where(lax.broadcasted_iota(I32, (ne, 1), 0) == 0, 0.0, pltpu.roll(p3, 1, axis=0))
        imp_ref[0:ne, :] = p0 + p1 + p2 + 0.5 * p3 + 0.5 * p3s
        if ne < NS:
            imp_ref[ne:NS, :] = jnp.zeros((NS - ne, T), F32)

        n_ar = lax.broadcasted_iota(I32, (NS, 1), 0)
        tqs = (i - 1) * Q_BLOCK + lax.broadcasted_iota(I32, (1, T), 1)
        cur = tqs // NSA_SLC_BLOCK
        valid = NSA_SLC_BLOCK * n_ar <= tqs
        forced = valid & ((n_ar == 0) | (n_ar > cur - NSA_LOCAL_BLOCKS))
        score = jnp.where(forced, 1e30, jnp.where(valid, imp_prev, -1.0))
        sel = jnp.zeros((NS, T), jnp.bool_)
        for _ in range(min(NSA_SLC_TOPK, NS)):
            mx = jnp.max(score, axis=0, keepdims=True)
            first = jnp.min(jnp.where(score == mx, n_ar, NS), axis=0, keepdims=True)
            pick = n_ar == first
            sel = sel | pick
            score = jnp.where(pick, -2.0, score)
        selb = jnp.where(sel, 1.0, 0.0).astype(BF16)
        if NS < LANES:
            selb = jnp.concatenate([selb, jnp.zeros((LANES - NS, T), BF16)], axis=0)
        for w in range(selt_ref.shape[0]):
            selt_ref[w] = selb[w * LANES:(w + 1) * LANES, :]

    quarter = NS // 4
    if quarter % WIN == 0:
        which = (2 * ib + 2 + quarter - 1) // quarter - 1
        for v in range(4):
            pl.when(which == v)(functools.partial(step, (v + 1) * quarter))
    else:
        step(NS)


def _cmp(nqt, kc, vct, bt, S):
    NS = S // NSA_SLC_BLOCK
    NCP = 4 * NS
    NW = max(NS // LANES, 1)
    G, R = NSA_KV_GROUPS, NSA_GROUP_SIZE
    nb = S // Q_BLOCK
    return pl.pallas_call(
        functools.partial(_cmp_body, NS=NS),
        grid=(G, nb + 1),
        in_specs=[pl.BlockSpec((R * HEAD_DIM, Q_BLOCK), lambda g, i: (g, jnp.minimum(i, nb - 1))),
                  pl.BlockSpec((None, NCP, HEAD_DIM), lambda g, i: (g, 0, 0)),
                  pl.BlockSpec((None, HEAD_DIM, NCP), lambda g, i: (g, 0, 0)),
                  pl.BlockSpec((R, 16, Q_BLOCK), lambda g, i: (g, 0, 0))],
        out_specs=[pl.BlockSpec((R * HEAD_DIM, Q_BLOCK), lambda g, i: (g, jnp.minimum(i, nb - 1))),
                   pl.BlockSpec((None, NW, LANES, Q_BLOCK), lambda g, i: (g, 0, 0, jnp.maximum(i - 1, 0)))],
        out_shape=[jax.ShapeDtypeStruct((NSA_W, S), F32),
                   jax.ShapeDtypeStruct((G, NW, LANES, S), BF16)],
        scratch_shapes=[pltpu.VMEM((R, NCP, Q_BLOCK), F32), pltpu.VMEM((NS, Q_BLOCK), F32)],
        compiler_params=_cparams(("arbitrary", "arbitrary")),
        name="cmp",
    )(nqt, kc, vct, bt)


def _flash_init(m_ref, l_ref, acc_ref):
    m_ref[...] = jnp.full(m_ref.shape, -1e30, F32)
    l_ref[...] = jnp.zeros(l_ref.shape, F32)
    acc_ref[...] = jnp.zeros(acc_ref.shape, F32)


def _flash_update(s, vt, m_ref, l_ref, acc_ref):
    m_old = m_ref[...]
    m_new = jnp.maximum(m_old, jnp.max(s, axis=0, keepdims=True))
    alpha = jnp.exp2(m_old - m_new)
    p = jnp.exp2(s - m_new)
    l_ref[...] = alpha * l_ref[...] + jnp.sum(p, axis=0, keepdims=True)
    acc_ref[...] = alpha * acc_ref[...] + _dot(vt, p.astype(BF16))
    m_ref[...] = m_new


def _slcwin_body(qt_ref, ks_ref, vst_ref, kw_ref, vwt_ref, selt_ref, tbt_ref, oct_ref, glt_ref, ot_ref,
                 ms_ref, ls_ref, as_ref, mw_ref, lw_ref, aw_ref, sbuf_ref, qaug_ref):
    i = pl.program_id(1)
    t0 = i * Q_BLOCK
    R = NSA_GROUP_SIZE
    T = Q_BLOCK
    q4t = jnp.concatenate([qt_ref[r * HEAD_DIM:(r + 1) * HEAD_DIM, :] for r in range(R)], axis=1)
    tq = lax.broadcasted_iota(I32, (1, T), 1)
    rep = lambda mk: jnp.concatenate([mk] * R, axis=1)
    tb4 = jnp.concatenate([tbt_ref[r] for r in range(R)], axis=1)

    def expand(w, base, nkeys):
        keyi = lax.broadcasted_iota(I32, (nkeys, LANES), 0)
        blk = lax.broadcasted_iota(I32, (nkeys, LANES), 1)
        e = jnp.where(blk == base + keyi // NSA_SLC_BLOCK, 1.0, 0.0).astype(BF16)
        return _dot(e, selt_ref[w])

    _flash_init(ms_ref, ls_ref, as_ref)
    _flash_init(mw_ref, lw_ref, aw_ref)
    n_win = selt_ref.shape[0]
    zpad = jnp.zeros((KAUG_W - HEAD_DIM - LANES, R * T), BF16)
    for w in range(n_win):
        qaug_ref[w] = jnp.concatenate([q4t, rep(selt_ref[w] - jnp.asarray(1.0, BF16)), zpad], axis=0)
    far_end = jnp.maximum(t0 - Q_BLOCK, 0) // KEY_TILE * KEY_TILE
    nfar = far_end // KEY_TILE
    tiles_per_window = LANES * NSA_SLC_BLOCK // KEY_TILE

    def score_far(kt, slot):
        r0 = pl.multiple_of(NSA_PADF + kt * KEY_TILE, LANES)
        w = jnp.minimum(kt // tiles_per_window, n_win - 1)
        sbuf_ref[slot] = _dot(ks_ref[pl.ds(r0, KEY_TILE), :], qaug_ref[w])

    def update_far(kt, slot):
        r0 = pl.multiple_of(NSA_PADF + kt * KEY_TILE, LANES)
        _flash_update(sbuf_ref[slot], vst_ref[:, pl.ds(r0, KEY_TILE)], ms_ref, ls_ref, as_ref)

    score_far(0, 0)

    def far2(k2, carry):
        kt = 2 * k2
        score_far(kt + 1, 1)
        update_far(kt, 0)
        score_far(kt + 2, 0)
        update_far(kt + 1, 1)
        return carry

    lax.fori_loop(0, nfar // 2, far2, 0)

    @pl.when(nfar % 2 == 1)
    def _():
        update_far(nfar - 1, 0)

    kq = lambda r0, n: _dot(ks_ref[pl.ds(r0, n), 0:HEAD_DIM], q4t)
    keyn = lax.broadcasted_iota(I32, (NEAR, 1), 0)
    near_ok = (keyn - Q_BLOCK <= tq) & (keyn + t0 - Q_BLOCK >= 0)
    rn = pl.multiple_of(t0 + NSA_PADF - Q_BLOCK, LANES)
    WF = NSA_WINDOW - Q_BLOCK
    rw = pl.multiple_of(t0 + NSA_PADF - NSA_WINDOW, LANES)
    keyw = lax.broadcasted_iota(I32, (WF, 1), 0)
    maskw = (keyw > tq) & (keyw + t0 - NSA_WINDOW >= 0)
    blocks = [jnp.maximum(2 * i - (NSA_WINDOW // NSA_SLC_BLOCK) + 2 * j, 0) for j in range(NSA_WINDOW // Q_BLOCK)]
    mexp = [expand(b // LANES, b % LANES, Q_BLOCK) for b in blocks]
    mask_sm = (jnp.concatenate(mexp[:-1], axis=0) > 0.5) & (keyw + t0 - NSA_WINDOW >= far_end)
    mask_sn = (jnp.concatenate([mexp[-1], expand(2 * i // LANES, 2 * i % LANES, Q_BLOCK)], axis=0) > 0.5) & near_ok
    s_sm = jnp.where(rep(mask_sm), kq(rw, WF), -jnp.inf)
    s_sn = jnp.where(rep(mask_sn), kq(rn, NEAR) + tb4, -jnp.inf)
    s_wf = jnp.where(rep(maskw), _dot(kw_ref[pl.ds(rw, WF), :], q4t), -jnp.inf)
    s_wn = jnp.where(rep(near_ok), _dot(kw_ref[pl.ds(rn, NEAR), :], q4t) + tb4, -jnp.inf)
    _flash_update(s_sm, vst_ref[:, pl.ds(rw, WF)], ms_ref, ls_ref, as_ref)
    _flash_update(s_sn, vst_ref[:, pl.ds(rn, NEAR)], ms_ref, ls_ref, as_ref)
    _flash_update(s_wf, vwt_ref[:, pl.ds(rw, WF)], mw_ref, lw_ref, aw_ref)
    _flash_update(s_wn, vwt_ref[:, pl.ds(rn, NEAR)], mw_ref, lw_ref, aw_ref)
    o_s = as_ref[...] / jnp.maximum(ls_ref[...], 1e-30)
    o_w = aw_ref[...] / jnp.maximum(lw_ref[...], 1e-30)

    gates = jax.nn.sigmoid(glt_ref[...])
    for r in range(R):
        hs = slice(r * HEAD_DIM, (r + 1) * HEAD_DIM)
        qs = slice(r * T, (r + 1) * T)
        o = (gates[3 * r:3 * r + 1, :] * oct_ref[hs, :]
             + gates[3 * r + 1:3 * r + 2, :] * o_s[:, qs]
             + gates[3 * r + 2:3 * r + 3, :] * o_w[:, qs])
        ot_ref[hs, :] = o.astype(BF16)


def _slcwin(nqt, ks, vst, kw, vwt, selt, tbt, oct, glt, S):
    G, R = NSA_KV_GROUPS, NSA_GROUP_SIZE
    spad = ks.shape[1]
    NW = selt.shape[1]
    once = pl.Buffered(1)
    kaspec = pl.BlockSpec((None, spad, KAUG_W), lambda g, i: (g, 0, 0), pipeline_mode=once)
    kspec = pl.BlockSpec((None, spad, HEAD_DIM), lambda g, i: (g, 0, 0), pipeline_mode=once)
    vspec = pl.BlockSpec((None, HEAD_DIM, spad), lambda g, i: (g, 0, 0), pipeline_mode=once)
    qspec = pl.BlockSpec((R * HEAD_DIM, Q_BLOCK), lambda g, i: (g, i))
    st = lambda n: pltpu.VMEM((n, R * Q_BLOCK), F32)
    return pl.pallas_call(
        _slcwin_body,
        grid=(G, S // Q_BLOCK),
        in_specs=[qspec, kaspec, vspec, kspec, vspec,
                  pl.BlockSpec((None, NW, LANES, Q_BLOCK), lambda g, i: (g, 0, 0, i)),
                  pl.BlockSpec((R, NEAR, Q_BLOCK), lambda g, i: (g, 0, 0)),
                  qspec,
                  pl.BlockSpec((None, 16, Q_BLOCK), lambda g, i: (g, 0, i))],
        out_specs=qspec,
        out_shape=jax.ShapeDtypeStruct((NSA_W, S), BF16),
        scratch_shapes=[st(1), st(1), st(HEAD_DIM), st(1), st(1), st(HEAD_DIM),
                        pltpu.VMEM((2, KEY_TILE, R * Q_BLOCK), F32),
                        pltpu.VMEM((NW, KAUG_W, R * Q_BLOCK), BF16)],
        compiler_params=_cparams(("arbitrary", "arbitrary")),
        name="slcwin",
    )(nqt, ks, vst, kw, vwt, selt, tbt, oct, glt)


INT_MIN = -2 ** 31
INT_MAX = 2 ** 31 - 1
MAX_PEEL_ROUNDS = 64
UNTESTED_BITS = 16
NEG_INF_KEY = int(np.array(-np.inf, np.float32).view(np.int32)) ^ 0x7FFFFFFF


def _dsa_body(dqt_ref, dk_ref, dvt_ref, iqt_ref, ikt_ref, iwt_ref, tbt_ref, ltri_ref, ot_ref,
              keys_ref, m_ref, l_ref, acc_ref, sbuf_ref, *, S, KEEP):
    i = pl.program_id(0)
    t0 = i * Q_BLOCK
    T = Q_BLOCK
    H = DSA_HEADS
    ntile = (t0 + Q_BLOCK + KEY_TILE - 1) // KEY_TILE
    tq = t0 + lax.broadcasted_iota(I32, (1, T), 1)
    key512 = lax.broadcasted_iota(I32, (KEY_TILE, 1), 0)
    ktile = lambda kt: keys_ref[pl.ds(pl.multiple_of(Q_BLOCK + kt * KEY_TILE, LANES), KEY_TILE), :]

    keys_ref[0:Q_BLOCK, :] = jnp.full((Q_BLOCK, T), NEG_INF_KEY, I32)
    iqt = iqt_ref[...]
    iwt = iwt_ref[...] * (IDX_HEADS ** -0.5)
    qis = [jnp.concatenate([iqt[(2 * j) * IDX_DIM:(2 * j + 1) * IDX_DIM, :],
                            iqt[(2 * j + 1) * IDX_DIM:(2 * j + 2) * IDX_DIM, :]], axis=1)
           for j in range(IDX_HEADS // 2)]
    wrow = [iwt[h:h + 1, :] for h in range(IDX_HEADS)]

    def idx_dots(kt, slot):
        ik = ikt_ref[:, pl.ds(pl.multiple_of(kt * KEY_TILE, KEY_TILE), KEY_TILE)]
        for j in range(IDX_HEADS // 2):
            sbuf_ref[slot, j] = _dot_tn(ik, qis[j])

    def idx_keys(kt, slot):
        c0 = pl.multiple_of(kt * KEY_TILE, KEY_TILE)
        sc = jnp.zeros((KEY_TILE, T), F32)
        for j in range(IDX_HEADS // 2):
            d = jnp.maximum(sbuf_ref[slot, j], 0.0)
            sc = sc + wrow[2 * j] * d[:, 0:T]
            sc = sc + wrow[2 * j + 1] * d[:, T:2 * T]
        sc = jnp.where(sc == 0.0, 0.0, sc)
        sc = jnp.where(c0 + key512 <= tq, sc, -jnp.inf)
        bits = pltpu.bitcast(sc, I32)
        keys_ref[pl.ds(pl.multiple_of(Q_BLOCK + c0, LANES), KEY_TILE), :] = jnp.where(
            bits < 0, bits ^ 0x7FFFFFFF, bits)

    idx_dots(0, 0)

    def score2(k2, carry):
        kt = 2 * k2
        idx_dots(kt + 1, 1)
        idx_keys(kt, 0)
        idx_dots(kt + 2, 0)
        idx_keys(kt + 1, 1)
        return carry

    lax.fori_loop(0, (ntile + 1) // 2, score2, 0)

    def count(pred):
        def body(kt, acc):
            hit = jnp.where(pred(ktile(kt), kt * KEY_TILE + key512), 1, 0)
            return acc + jnp.sum(hit.reshape(KEY_TILE // SUBLANES, SUBLANES, T), axis=0)
        acc = lax.fori_loop(0, ntile, body, jnp.zeros((SUBLANES, T), I32))
        return jnp.sum(acc, axis=0, keepdims=True)

    def min_at_least(lowest):
        def body(kt, acc):
            kk = ktile(kt)
            v = jnp.where(kk >= lowest, kk, INT_MAX)
            return jnp.minimum(acc, jnp.min(v.reshape(KEY_TILE // SUBLANES, SUBLANES, T), axis=0))
        acc = lax.fori_loop(0, ntile, body, jnp.full((SUBLANES, T), INT_MAX, I32))
        return jnp.min(acc, axis=0, keepdims=True)

    def any_open(st, over):
        _, excess, fin, _ = st
        return jnp.max(jnp.where((fin == 0) & (excess > over), 1, 0)) > 0

    def bisect(b, st):
        thr, excess, fin, need = st
        cand = thr + lax.shift_left(jnp.int32(1), 31 - b)
        n = count(lambda kk, ab: kk >= cand)
        ok = (n >= KEEP) & (fin == 0)
        return jnp.where(ok, cand, thr), jnp.where(ok, n - KEEP, excess), fin, need

    def peel(st):
        thr, excess, fin, need = st
        low = min_at_least(thr)
        n_low = count(lambda kk, ab: kk == low)
        active = (fin == 0) & (excess > 0)
        tied = active & (n_low > excess)
        drop = active & (n_low <= excess)
        return (jnp.where(tied, low, jnp.where(drop, low + 1, thr)), jnp.where(drop, excess - n_low, excess),
                jnp.where(tied, 1, fin), jnp.where(tied, n_low - excess, need))

    zero = jnp.zeros((1, T), I32)
    st = (jnp.full((1, T), INT_MIN, I32), KEY_TILE * ntile - KEEP + zero, zero, zero)
    st = peel(bisect(0, st))
    st = lax.fori_loop(1, UNTESTED_BITS, bisect, st)
    _, st = lax.while_loop(lambda c: (c[0] < 32) & any_open(c[1], 1),
                           lambda c: (c[0] + 1, bisect(c[0], c[1])), (jnp.int32(UNTESTED_BITS), st))
    _, st = lax.while_loop(lambda c: (c[0] < MAX_PEEL_ROUNDS) & any_open(c[1], 0),
                           lambda c: (c[0] + 1, peel(c[1])), (jnp.int32(0), st))
    thr, _, fin, need = st
    tie = (fin == 1) & (thr > NEG_INF_KEY)

    def tie_cut():
        needf = need.astype(F32)

        def body(kt, c):
            before, jmax = c
            is_tie = ktile(kt) == thr
            tied = jnp.where(is_tie, 1.0, 0.0)
            rank = before + _dot(ltri_ref[...], tied.astype(BF16))
            admit = is_tie & (rank < needf)
            jm = jnp.max(jnp.where(admit, kt * KEY_TILE + key512, -1), axis=0, keepdims=True)
            return before + jnp.sum(tied, axis=0, keepdims=True), jnp.maximum(jmax, jm)

        _, jmax = lax.fori_loop(0, ntile, body, (jnp.zeros((1, T), F32), jnp.full((1, T), -1, I32)))
        return jnp.where(tie, jmax + 1, S)

    jcut = lax.cond(jnp.max(jnp.where(tie, 1, 0)) > 0, tie_cut, lambda: jnp.full((1, T), S, I32))

    def chosen(kk, ab):
        return (kk > thr) | ((kk == thr) & (ab < jcut))

    _flash_init(m_ref, l_ref, acc_ref)
    lo = lax.broadcasted_iota(I32, (LANES, 1), 0) < HEAD_DIM
    qpairs = []
    for pr in range(H // 2):
        qp = dqt_ref[pr * LANES:(pr + 1) * LANES, :]
        zero = jnp.zeros_like(qp)
        qpairs.append(jnp.concatenate([jnp.where(lo, qp, zero), jnp.where(lo, zero, qp)], axis=1))

    def scores(r0, nkeys, mask, bias):
        mask2 = jnp.concatenate([mask, mask], axis=1)
        out = []
        for pr in range(H // 2):
            s = _dot(dk_ref[pl.ds(r0, nkeys), pr * LANES:(pr + 1) * LANES], qpairs[pr])
            if bias is not None:
                s = s + jnp.concatenate([bias[2 * pr], bias[2 * pr + 1]], axis=1)
            out.append(jnp.where(mask2, s, -jnp.inf))
        return out

    def update(pr, s, r0, nkeys):
        m_old = m_ref[pr]
        m_new = jnp.maximum(m_old, jnp.max(s, axis=0, keepdims=True))
        alpha = jnp.exp2(m_old - m_new)
        p = jnp.exp2(s - m_new)
        l_ref[pr] = alpha * l_ref[pr] + jnp.sum(p, axis=0, keepdims=True)
        m_ref[pr] = m_new
        vt = dvt_ref[pr * LANES:(pr + 1) * LANES, pl.ds(r0, nkeys)]
        acc_ref[pr] = alpha * acc_ref[pr] + _dot(vt, p.astype(BF16))

    far_end = t0 - Q_BLOCK
    nfar = jnp.maximum(far_end + KEY_TILE - 1, 0) // KEY_TILE
    far_row = lambda kt: pl.multiple_of(DSA_PADF + kt * KEY_TILE, LANES)

    def score_far(kt, slot):
        ab = kt * KEY_TILE + key512
        mask = chosen(ktile(jnp.minimum(kt, ntile - 1)), ab) & (ab < far_end)
        for pr, s in enumerate(scores(far_row(kt), KEY_TILE, mask, None)):
            sbuf_ref[slot, pr] = s

    def update_far(kt, slot):
        for pr in range(H // 2):
            update(pr, sbuf_ref[slot, pr], far_row(kt), KEY_TILE)

    score_far(0, 0)

    def far2(k2, carry):
        kt = 2 * k2
        score_far(kt + 1, 1)
        update_far(kt, 0)
        score_far(kt + 2, 0)
        update_far(kt + 1, 1)
        return carry

    lax.fori_loop(0, (nfar + 1) // 2, far2, 0)

    abn = lax.broadcasted_iota(I32, (NEAR, 1), 0) + t0 - Q_BLOCK
    kkn = keys_ref[pl.ds(pl.multiple_of(t0, LANES), NEAR), :]
    rn = pl.multiple_of(t0 + DSA_PADF - Q_BLOCK, LANES)
    for pr, s in enumerate(scores(rn, NEAR, chosen(kkn, abn) & (abn <= tq) & (abn >= 0), tbt_ref)):
        update(pr, s, rn, NEAR)

    for pr in range(H // 2):
        o = acc_ref[pr] / jnp.maximum(l_ref[pr], 1e-30)
        ot_ref[pr * LANES:(pr + 1) * LANES, :] = jnp.where(lo, o[:, 0:T], o[:, T:2 * T]).astype(BF16)


def _dsa(dqt, dkp, dvtp, iqt, ikt, iwt, tbt, ltri, S):
    KEEP = min(DSA_TOPK_MAX, S // 4)
    H = DSA_HEADS
    T = Q_BLOCK
    once = pl.Buffered(1)
    full = lambda a: pl.BlockSpec(a.shape, lambda i: (0,) * a.ndim, pipeline_mode=once)
    return pl.pallas_call(
        functools.partial(_dsa_body, S=S, KEEP=KEEP),
        grid=(S // T,),
        in_specs=[pl.BlockSpec((DSA_W, T), lambda i: (0, i)),
                  full(dkp), full(dvtp),
                  pl.BlockSpec((IDX_HEADS * IDX_DIM, T), lambda i: (0, i)),
                  full(ikt),
                  pl.BlockSpec((IDX_HEADS, T), lambda i: (0, i)),
                  full(tbt), full(ltri)],
        out_specs=pl.BlockSpec((DSA_W, T), lambda i: (0, i)),
        out_shape=jax.ShapeDtypeStruct((DSA_W, S), BF16),
        scratch_shapes=[pltpu.VMEM((Q_BLOCK + S + KEY_TILE, T), I32),
                        pltpu.VMEM((H // 2, 1, 2 * T), F32),
                        pltpu.VMEM((H // 2, 1, 2 * T), F32),
                        pltpu.VMEM((H // 2, LANES, 2 * T), F32),
                        pltpu.VMEM((2, H // 2, KEY_TILE, 2 * T), F32)],
        compiler_params=_cparams(("arbitrary",)),
        name="dsa",
    )(dqt, dkp, dvtp, iqt, ikt, iwt, tbt, ltri)


def _split_bf16(a):
    hi = a.astype(BF16)
    return hi, (a - hi.astype(F32)).astype(BF16)


def _merge_body(x_ref, oa_ref, ob_ref, gab_ref, wa_ref, wb_ref, wo_ref, gffn_ref, wr_ref, br_ref,
                x1_ref, h2_ref, ridx_ref, rw_ref):
    a = _dot_tn(oa_ref[...], wa_ref[...])
    b = _dot_tn(ob_ref[...], wb_ref[...])
    gab = gab_ref[...]
    merged = gab[:, :D_MODEL] * a + gab[:, D_MODEL:] * b
    x1 = x_ref[...] + _dot(merged.astype(BF16), wo_ref[...])
    x1_ref[...] = x1
    h2 = x1 * lax.rsqrt(jnp.mean(x1 * x1, axis=-1, keepdims=True) + EPS) * gffn_ref[...]
    h2_ref[...] = h2
    hh, hl = _split_bf16(h2)
    wh, wl = _split_bf16(wr_ref[...])
    logits = _dot(hh, wh) + _dot(hh, wl) + _dot(hl, wh) + br_ref[...]
    lane = lax.broadcasted_iota(I32, (1, LANES), 1)
    idx_out = jnp.zeros(logits.shape, I32)
    val_out = jnp.full(logits.shape, -jnp.inf, F32)
    for k in range(TOP_K):
        mx = jnp.max(logits, axis=-1, keepdims=True)
        first = jnp.min(jnp.where(logits == mx, lane, LANES), axis=-1, keepdims=True)
        idx_out = jnp.where(lane == k, first, idx_out)
        val_out = jnp.where(lane == k, mx, val_out)
        logits = jnp.where(lane == first, -jnp.inf, logits)
    e = jnp.exp(val_out - jnp.max(val_out, axis=-1, keepdims=True))
    ridx_ref[...] = idx_out
    rw_ref[...] = e / jnp.sum(e, axis=-1, keepdims=True)


def _merge(x2, oa, ob, gab, wa, wb, wo, gffn, wr, br, tm=512):
    S = x2.shape[0]
    row = lambda n: pl.BlockSpec((tm, n), lambda i: (i, 0))
    full = lambda a: pl.BlockSpec(a.shape, lambda i: (0,) * a.ndim)
    return pl.pallas_call(
        _merge_body,
        grid=(S // tm,),
        in_specs=[row(D_MODEL), pl.BlockSpec((NSA_W, tm), lambda i: (0, i)), pl.BlockSpec((DSA_W, tm), lambda i: (0, i)),
                  row(2 * D_MODEL),
                  full(wa), full(wb), full(wo), full(gffn), full(wr), full(br)],
        out_specs=[row(D_MODEL), row(D_MODEL), row(LANES), row(LANES)],
        out_shape=[jax.ShapeDtypeStruct((S, D_MODEL), F32), jax.ShapeDtypeStruct((S, D_MODEL), F32),
                   jax.ShapeDtypeStruct((S, LANES), I32), jax.ShapeDtypeStruct((S, LANES), F32)],
        compiler_params=_cparams(("arbitrary",)),
        name="merge",
    )(x2, oa, ob, gab, wa, wb, wo, gffn, wr, br)


def _rank_body(ridx_ref, rank_ref, cnt_ref, carry_ref):
    @pl.when(pl.program_id(0) == 0)
    def _():
        carry_ref[...] = jnp.zeros(carry_ref.shape, F32)

    ridx = ridx_ref[...]
    tm = ridx.shape[0]
    lane = lax.broadcasted_iota(I32, (1, LANES), 1)
    hits = [lane == ridx[:, k:k + 1] for k in range(TOP_K)]
    member = jnp.zeros((tm, LANES), F32)
    for hk in hits:
        member = member + jnp.where(hk, 1.0, 0.0)
    ri = lax.broadcasted_iota(I32, (tm, tm), 0)
    ci = lax.broadcasted_iota(I32, (tm, tm), 1)
    lower = jnp.where(ci < ri, 1.0, 0.0).astype(BF16)
    before = _dot(lower, member.astype(BF16)) + carry_ref[0:1, :]
    out = jnp.zeros((tm, LANES), I32)
    for k, hk in enumerate(hits):
        rk = jnp.sum(jnp.where(hk, before, 0.0), axis=-1, keepdims=True)
        out = jnp.where(lane == k, rk.astype(I32), out)
    rank_ref[...] = out
    total = carry_ref[0:1, :] + jnp.sum(member, axis=0, keepdims=True)
    carry_ref[...] = jnp.broadcast_to(total, carry_ref.shape)
    cnt_ref[...] = jnp.broadcast_to(total, cnt_ref.shape).astype(I32)


def _rank(ridx, tm=512):
    S = ridx.shape[0]
    return pl.pallas_call(
        _rank_body,
        grid=(S // tm,),
        in_specs=[pl.BlockSpec((tm, LANES), lambda i: (i, 0))],
        out_specs=[pl.BlockSpec((tm, LANES), lambda i: (i, 0)), pl.BlockSpec((8, LANES), lambda i: (0, 0))],
        out_shape=[jax.ShapeDtypeStruct((S, LANES), I32), jax.ShapeDtypeStruct((8, LANES), I32)],
        scratch_shapes=[pltpu.VMEM((8, LANES), F32)],
        compiler_params=_cparams(("arbitrary",)),
        name="rank",
    )(ridx)


def _dispatch_body(dest_ref, zrow_ref, nu_ref, h_ref, xe_ref, zbuf_ref, sem, zsem):
    i = pl.program_id(0)
    tm = h_ref.shape[0]

    @pl.when(i == 0)
    def _():
        zbuf_ref[...] = jnp.zeros(zbuf_ref.shape, F32)
        fill = lambda row: pltpu.make_async_copy(zbuf_ref, xe_ref.at[pl.ds(row, MOE_ROWS)], zsem)
        n_chunks = xe_ref.shape[0] // MOE_ROWS
        for wait in (False, True):
            for e in range(N_EXPERTS):
                @pl.when(zrow_ref[e] >= 0)
                def _():
                    cp = fill(pl.multiple_of(jnp.maximum(zrow_ref[e], 0), MOE_ROWS))
                    cp.wait() if wait else cp.start()

            def tail(j, c):
                cp = fill(pl.multiple_of(j * MOE_ROWS, MOE_ROWS))
                cp.wait() if wait else cp.start()
                return c
            lax.fori_loop(nu_ref[0], n_chunks, tail, 0)

    def start(r, c):
        for k in range(TOP_K):
            d = dest_ref[(i * tm + r) * TOP_K + k]
            pltpu.make_async_copy(h_ref.at[pl.ds(r, 1)], xe_ref.at[pl.ds(d, 1)],
                                  sem).start(priority=k % 2)
        return c

    lax.fori_loop(0, tm, start, 0, unroll=ROW_DMA_UNROLL)
    for _ in range(TOP_K):
        pltpu.make_async_copy(h_ref, xe_ref.at[pl.ds(0, tm)], sem).wait()


def _dispatch(dest, zrow, n_used, h2, n_rows, tm=128):
    S = h2.shape[0]
    return pl.pallas_call(
        _dispatch_body,
        grid_spec=pltpu.PrefetchScalarGridSpec(
            num_scalar_prefetch=3,
            grid=(S // tm,),
            in_specs=[pl.BlockSpec((tm, D_MODEL), lambda i, d, z, n: (i, 0))],
            out_specs=pl.BlockSpec(memory_space=pl.ANY),
            scratch_shapes=[pltpu.VMEM((MOE_ROWS, D_MODEL), F32), pltpu.SemaphoreType.DMA(()),
                            pltpu.SemaphoreType.DMA(())]),
        out_shape=jax.ShapeDtypeStruct((n_rows, D_MODEL), F32),
        compiler_params=_cparams(("arbitrary",)),
        name="dispatch",
    )(dest, zrow, n_used, h2)


def _experts_body(ce_ref, nu_ref, x_ref, w1_ref, b1_ref, w2_ref, b2_ref, y_ref, w1b_ref, w2b_ref):
    c = pl.program_id(0)

    @pl.when((c == 0) | (ce_ref[c] != ce_ref[jnp.maximum(c - 1, 0)]))
    def _():
        w1b_ref[...] = w1_ref[...].astype(BF16)
        w2b_ref[...] = w2_ref[...].astype(BF16)

    @pl.when(c < nu_ref[0])
    def _():
        gu = _dot(x_ref[...].astype(BF16), w1b_ref[...]) + b1_ref[...]
        gate = jnp.minimum(gu[:, :D_FF], SWIGLU_LIMIT)
        lin = jnp.clip(gu[:, D_FF:], -SWIGLU_LIMIT, SWIGLU_LIMIT)
        act = (lin + 1.0) * gate * jax.nn.sigmoid(SWIGLU_ALPHA * gate)
        y_ref[...] = _dot(act.astype(BF16), w2b_ref[...]) + b2_ref[...]

    @pl.when(c >= nu_ref[0])
    def _():
        y_ref[...] = jnp.zeros(y_ref.shape, F32)


def _experts(chunk_e, n_used, xe, w1, b1, w2, b2):
    P = xe.shape[0]
    nch = P // MOE_ROWS
    return pl.pallas_call(
        _experts_body,
        grid_spec=pltpu.PrefetchScalarGridSpec(
            num_scalar_prefetch=2,
            grid=(nch,),
            in_specs=[pl.BlockSpec((MOE_ROWS, D_MODEL), lambda c, ce, nu: (c, 0)),
                      pl.BlockSpec((None, D_MODEL, 2 * D_FF), lambda c, ce, nu: (ce[c], 0, 0)),
                      pl.BlockSpec((None, 1, 2 * D_FF), lambda c, ce, nu: (ce[c], 0, 0)),
                      pl.BlockSpec((None, D_FF, D_MODEL), lambda c, ce, nu: (ce[c], 0, 0)),
                      pl.BlockSpec((None, 1, D_MODEL), lambda c, ce, nu: (ce[c], 0, 0))],
            out_specs=pl.BlockSpec((MOE_ROWS, D_MODEL), lambda c, ce, nu: (c, 0)),
            scratch_shapes=[pltpu.VMEM((D_MODEL, 2 * D_FF), BF16), pltpu.VMEM((D_FF, D_MODEL), BF16)]),
        out_shape=jax.ShapeDtypeStruct((P, D_MODEL), F32),
        compiler_params=_cparams(("arbitrary",)),
        name="experts",
    )(chunk_e, n_used, xe, w1, b1, w2, b2)


def _final_body(dest_ref, x1_ref, rw_ref, p_ref, gple_ref, wg_ref, wp_ref, y_ref, o_ref, rows_ref, sems):
    i = pl.program_id(0)
    tm = x1_ref.shape[0]

    def gather(blk, slot):
        def start(r, c):
            for k in range(TOP_K):
                d = dest_ref[(blk * tm + r) * TOP_K + k]
                pltpu.make_async_copy(y_ref.at[pl.ds(d, 1)], rows_ref.at[slot, k, pl.ds(r, 1)],
                                      sems.at[slot]).start(priority=k % 2)
            return c
        lax.fori_loop(0, tm, start, 0, unroll=ROW_DMA_UNROLL)

    def drain(which):
        for k in range(TOP_K):
            pltpu.make_async_copy(y_ref.at[pl.ds(0, tm)], rows_ref.at[which, k], sems.at[which]).wait()

    @pl.when(i == 0)
    def _():
        gather(0, 0)

    slot = i % 2
    last = pl.num_programs(0) - 1
    drain(slot)
    nxt = jnp.minimum(i + 1, last)
    for r in range(tm):
        for k in range(TOP_K):
            d = dest_ref[(nxt * tm + r) * TOP_K + k]
            pltpu.make_async_copy(y_ref.at[pl.ds(d, 1)], rows_ref.at[1 - slot, k, pl.ds(r, 1)],
                                  sems.at[1 - slot]).start(priority=k % 2)
    rw = rw_ref[...]
    x2 = x1_ref[...]
    for k in range(TOP_K):
        x2 = x2 + rw[:, k:k + 1] * rows_ref[slot, k]
    hn = x2 * lax.rsqrt(jnp.mean(x2 * x2, axis=-1, keepdims=True) + EPS) * gple_ref[...]
    gate = jax.nn.sigmoid(_dot(hn.astype(BF16), wg_ref[...]))
    o_ref[...] = x2 + gate * _dot(p_ref[...].astype(BF16), wp_ref[...])

    @pl.when(i == last)
    def _():
        drain(1 - slot)


def _final(dest, x1, rw, p2, gple, wg, wp, yexp, tm=128):
    S = x1.shape[0]
    row = lambda n: pl.BlockSpec((tm, n), lambda i, d: (i, 0))
    full = lambda a: pl.BlockSpec(a.shape, lambda i, d: (0,) * a.ndim)
    return pl.pallas_call(
        _final_body,
        grid_spec=pltpu.PrefetchScalarGridSpec(
            num_scalar_prefetch=1,
            grid=(S // tm,),
            in_specs=[row(D_MODEL), row(LANES), row(PLE_DIM), full(gple), full(wg), full(wp),
                      pl.BlockSpec(memory_space=pl.ANY)],
            out_specs=row(D_MODEL),
            scratch_shapes=[pltpu.VMEM((2, TOP_K, tm, D_MODEL), F32), pltpu.SemaphoreType.DMA((2,))]),
        out_shape=jax.ShapeDtypeStruct((S, D_MODEL), F32),
        compiler_params=_cparams(("arbitrary",)),
        name="final",
    )(dest, x1, rw, p2, gple, wg, wp, yexp)


def _distance_bias(rel_cols):
    return (rel_cols[_rel_bucket_table(NEAR)] - rel_cols[REL_BUCKETS - 1][None, :]).T * LOG2E


def _near_bias_t(rel_cols):
    h = rel_cols.shape[1]
    g = jnp.pad(_distance_bias(rel_cols), ((0, 0), (Q_BLOCK, Q_BLOCK)))
    period = 2 * NEAR
    skew = jnp.tile(g, (1, NEAR))[:, :NEAR * (period - 1)].reshape(h, NEAR, period - 1)
    return skew[:, :, NEAR:NEAR + Q_BLOCK]


def _cmp_bias_t(rel_cols):
    g = jnp.pad(_distance_bias(rel_cols), ((0, 0), (NEAR, Q_BLOCK)))
    rows = []
    for j in range(4):
        for k in range(4):
            c0 = -NSA_CMP_STRIDE * j - (NSA_CMP_LEN - 1) + NSA_SLC_BLOCK * (k - 1)
            rows.append(g[:, c0 + NEAR:c0 + NEAR + Q_BLOCK])
    return jnp.stack(rows, axis=1)


def kernel(x, p, w_in, g_mix, g_q_nsa, g_k_nsa, g_q_dsa, g_k_dsa, pe_ck, w_ck1, w_ck2, pe_cv, w_cv1, w_cv2,
           w_branch_a, w_branch_b, w_out, rel_bias, g_ffn, w_router, b_router, w_e1, b_e1, w_e2, b_e2,
           g_ple, w_ple, w_ple_gate):
    B, S, D = x.shape
    assert B == 1 and D == D_MODEL and S % KEY_TILE == 0 and w_in.shape[0] == 1
    G, R = NSA_KV_GROUPS, NSA_GROUP_SIZE
    x2 = x.reshape(S, D)

    w0 = w_in[0]
    w_nat = jnp.concatenate([w0[:, 512:768], w0[:, 768:896], w0[:, 1024:1152], w0[:, 1816:2328], w0[:, 3136:5184],
                             w0[:, 1280:1304], w0[:, 3096:3136], jnp.zeros((D, LANES - 64), F32)], axis=1).astype(BF16)
    w_tr = jnp.concatenate([w0[:, 0:512], w0[:, 1304:1816], w0[:, 2328:2840], w0[:, 2840:3096],
                            w0[:, 896:1024], w0[:, 1152:1280]], axis=1).T.astype(BF16)
    two = lambda g: jnp.tile(g.reshape(1, HEAD_DIM), (1, 2))
    qscale = HEAD_DIM ** -0.5 * LOG2E
    gains = jnp.concatenate([two(g_k_nsa[0]), two(g_k_dsa[0])], axis=0)
    gcol = jnp.stack([g_q_nsa[0] * qscale, g_q_dsa[0] * qscale]).reshape(2, HEAD_DIM, 1)
    (cmpraw, kslc, kwin, dk, gab, small, nqt, dqt, dvt, iqt, vst2, vwt2) = _inproj(
        x2, g_mix[0].reshape(1, D), w_nat, w_tr, gains, gcol)

    NS = S // NSA_SLC_BLOCK
    xc = cmpraw.reshape(2 * G, NS, 4 * NSA_CMP_STRIDE * HEAD_DIM)
    half = NSA_CMP_STRIDE * HEAD_DIM
    w1 = jnp.stack([w_ck1[0], w_cv1[0]]).reshape(2, 2 * half, HEAD_DIM)
    pe = jnp.stack([pe_ck[0], pe_cv[0]]).reshape(2, 2 * half, 1)
    w2 = jnp.stack([w_ck2[0], w_cv2[0]]).astype(BF16)
    kvc = _compress(xc, w1[:, :half].astype(BF16), w1[:, half:].astype(BF16), w1, pe, w2,
                    g_k_nsa[0].reshape(1, HEAD_DIM))

    rel_nsa = rel_bias[:, :NSA_HEADS]
    rel_dsa = rel_bias[:, NSA_HEADS:]
    oct, selt = _cmp(nqt, kvc[:G], kvc[G:].transpose(0, 2, 1), _cmp_bias_t(rel_nsa), S)
    kgroups = lambda a: jnp.pad(a, ((0, 0), (NSA_PADF, TAIL_PAD), (0, 0)))
    vgroups = lambda a: jnp.pad(a.reshape(G, HEAD_DIM, S), ((0, 0), (0, 0), (NSA_PADF, TAIL_PAD)))
    glt = small[:, SMALL_GATE:SMALL_GATE + 3 * NSA_HEADS].reshape(S, G, 3 * R).transpose(1, 2, 0)
    glt = jnp.pad(glt, ((0, 0), (0, 16 - 3 * R), (0, 0)))
    blk_lane = (np.arange(S) // NSA_SLC_BLOCK) % LANES
    onehot = jnp.asarray(np.eye(LANES, dtype=np.float32)[blk_lane] * MASK_BIG, BF16)
    kaug = jnp.concatenate([kslc,
                            jnp.broadcast_to(onehot, (G, S, LANES)),
                            jnp.zeros((G, S, KAUG_W - HEAD_DIM - LANES), BF16)], axis=-1)
    kaug = jnp.pad(kaug, ((0, 0), (NSA_PADF, TAIL_PAD), (0, 0)))
    o_at = _slcwin(nqt, kaug, vgroups(vst2), kgroups(kwin), vgroups(vwt2), selt,
                   _near_bias_t(rel_nsa), oct, glt, S)

    dkp = jnp.pad(dk, ((DSA_PADF, TAIL_PAD), (0, 0)))
    dvtp = jnp.pad(dvt, ((0, 0), (DSA_PADF, TAIL_PAD)))
    ikt = jnp.pad(small[:, SMALL_IK:SMALL_IK + IDX_DIM].T.astype(BF16), ((0, 0), (0, 2 * KEY_TILE)))
    iwt = small[:, SMALL_IW:SMALL_IW + IDX_HEADS].T
    ltri = jnp.asarray(np.tril(np.ones((KEY_TILE, KEY_TILE), np.float32), -1), BF16)
    o_bt = _dsa(dqt, dkp, dvtp, iqt, ikt, iwt, _near_bias_t(rel_dsa), ltri, S)

    wr = jnp.pad(w_router[0], ((0, 0), (0, LANES - N_EXPERTS)))
    br = jnp.pad(b_router[0].reshape(1, N_EXPERTS), ((0, 0), (0, LANES - N_EXPERTS)), constant_values=-1e30)
    x1, h2, ridx, rw = _merge(x2, o_at, o_bt, gab, w_branch_a[0].astype(BF16), w_branch_b[0].astype(BF16),
                              w_out[0].astype(BF16), g_ffn[0].reshape(1, D), wr, br)

    rank, cnt = _rank(ridx)
    counts = cnt[0, :N_EXPERTS]
    padded = (counts + MOE_ROWS - 1) // MOE_ROWS * MOE_ROWS
    pad_end = jnp.cumsum(padded)
    pad_start = pad_end - padded
    eidx = ridx[:, :TOP_K]
    dest = (jnp.sum(jnp.where(eidx[..., None] == jnp.arange(N_EXPERTS), pad_start, 0), axis=-1)
            + rank[:, :TOP_K]).reshape(-1).astype(I32)
    n_chunks = -(-(S * TOP_K) // MOE_ROWS) + N_EXPERTS
    chunk_start = jnp.arange(n_chunks, dtype=I32) * MOE_ROWS
    chunk_e = jnp.minimum(jnp.sum(chunk_start[:, None] >= pad_end[None, :], axis=-1), N_EXPERTS - 1).astype(I32)
    n_used = (pad_end[-1] // MOE_ROWS).astype(I32).reshape(1)
    zrow = jnp.where(padded > 0, pad_end - MOE_ROWS, -1).astype(I32)
    xe = _dispatch(dest, zrow, n_used, h2, n_chunks * MOE_ROWS)
    yexp = _experts(chunk_e, n_used, xe, w_e1[0], b_e1[0].reshape(N_EXPERTS, 1, 2 * D_FF),
                    w_e2[0], b_e2[0].reshape(N_EXPERTS, 1, D))

    out = _final(dest, x1, rw, p[0].reshape(S, PLE_DIM), g_ple[0].reshape(1, D),
                 w_ple_gate[0].astype(BF16), w_ple[0].astype(BF16), yexp)
    return out.reshape(B, S, D)
```

```python
import functools
import math

import numpy as np
import jax
import jax.numpy as jnp
from jax import lax
from jax.experimental import pallas as pl
from jax.experimental.pallas import tpu as pltpu

F32 = jnp.float32
BF16 = jnp.bfloat16
I32 = jnp.int32

D_MODEL = 1024
PLE_DIM = 256
HEAD_DIM = 64
NSA_HEADS = 8
NSA_KV_GROUPS = 2
NSA_GROUP_SIZE = NSA_HEADS // NSA_KV_GROUPS
NSA_CMP_LEN = 32
NSA_CMP_STRIDE = 16
NSA_SLC_BLOCK = 64
NSA_SLC_TOPK = 16
NSA_LOCAL_BLOCKS = 2
NSA_WINDOW = 512
DSA_HEADS = 8
IDX_HEADS = 8
IDX_DIM = 32
DSA_TOPK_MAX = 256
Q_BLOCK = 128
REL_BUCKETS = 32
REL_MAX_EXACT = 16
REL_MAX_DIST = 128
N_EXPERTS = 32
TOP_K = 4
D_FF = 1024
SWIGLU_LIMIT = 7.0
SWIGLU_ALPHA = 1.702
MOE_ROWS = 512
EPS = 1e-6
LOG2E = math.log2(math.e)

NSA_W = NSA_HEADS * HEAD_DIM
DSA_W = DSA_HEADS * HEAD_DIM

LANES = 128
SUBLANES = 8
KEY_TILE = 512
NEAR = 2 * Q_BLOCK
NSA_PADF = NSA_WINDOW
DSA_PADF = Q_BLOCK
KAUG_W = 2 * LANES
MASK_BIG = 2.0 ** 100
TAIL_PAD = 3 * KEY_TILE
VMEM_LIMIT = 56 * 1024 * 1024
ROW_DMA_UNROLL = 8

NAT_CMP, NAT_KSLC, NAT_KWIN, NAT_DK, NAT_GAB, NAT_SMALL = 0, 256, 384, 512, 1024, 3072
TR_NQ, TR_DQ, TR_DV, TR_IQ, TR_VSLC, TR_VWIN = 0, 512, 1024, 1536, 1792, 1920
SMALL_GATE, SMALL_IK, SMALL_IW = 0, 24, 56


def _rel_bucket_table(n):
    d = np.arange(n)
    nf = np.maximum(d, 1).astype(np.float64)
    large = REL_MAX_EXACT + (np.log(nf / REL_MAX_EXACT) / math.log(REL_MAX_DIST / REL_MAX_EXACT)
                             * (REL_BUCKETS - REL_MAX_EXACT)).astype(np.int64)
    large = np.minimum(large, REL_BUCKETS - 1)
    return np.where(d < REL_MAX_EXACT, d, large)


def _cparams(sem, vmem=VMEM_LIMIT):
    return pltpu.CompilerParams(dimension_semantics=sem, vmem_limit_bytes=vmem)


def _dot(a, b):
    return jnp.dot(a, b, preferred_element_type=F32)


def _dot_tn(a, b):
    return lax.dot_general(a, b, (((0,), (0,)), ((), ())), preferred_element_type=F32)


def _inproj_body(x_ref, gmix_ref, w_ref, wt_ref, gains_ref, gcol_ref,
                 cmp_ref, kslc_ref, kwin_ref, dk_ref, gab_ref, small_ref,
                 nqt_ref, dqt_ref, dvt_ref, iqt_ref, vst_ref, vwt_ref):
    x = x_ref[...]
    h = x * lax.rsqrt(jnp.mean(x * x, axis=-1, keepdims=True) + EPS) * gmix_ref[...]
    hb = h.astype(BF16)
    tm = hb.shape[0]
    lo = lax.broadcasted_iota(I32, (1, LANES), 1) < HEAD_DIM

    def mm(off, n):
        return _dot(hb, w_ref[:, off:off + n])

    def headnorm(y, g):
        y2 = y * y
        s_lo = jnp.sum(jnp.where(lo, y2, 0.0), axis=-1, keepdims=True)
        s_hi = jnp.sum(jnp.where(lo, 0.0, y2), axis=-1, keepdims=True)
        r = jnp.where(lo, lax.rsqrt(s_lo * (1.0 / HEAD_DIM) + EPS), lax.rsqrt(s_hi * (1.0 / HEAD_DIM) + EPS))
        return y * r * g

    def headnorm_t(y, g):
        y3 = y.reshape(y.shape[0] // HEAD_DIM, HEAD_DIM, tm)
        r = lax.rsqrt(jnp.mean(y3 * y3, axis=1, keepdims=True) + EPS)
        return (y3 * r * g).reshape(y.shape)

    g_kn, g_kd = gains_ref[0:1, :], gains_ref[1:2, :]
    def put_heads(ref, y):
        for k in range(ref.shape[0]):
            ref[k] = y[:, k * HEAD_DIM:(k + 1) * HEAD_DIM].astype(BF16)

    put_heads(cmp_ref, mm(NAT_CMP, 256))
    put_heads(kslc_ref, headnorm(mm(NAT_KSLC, LANES), g_kn))
    put_heads(kwin_ref, headnorm(mm(NAT_KWIN, LANES), g_kn))
    for c in range(DSA_W // LANES):
        dk_ref[:, c * LANES:(c + 1) * LANES] = headnorm(mm(NAT_DK + c * LANES, LANES), g_kd).astype(BF16)
    gab_ref[...] = jax.nn.sigmoid(mm(NAT_GAB, 2 * D_MODEL))
    small_ref[...] = mm(NAT_SMALL, LANES)

    def mm_t(off, n):
        return lax.dot_general(wt_ref[off:off + n, :], hb, (((1,), (1,)), ((), ())), preferred_element_type=F32)

    nqt_ref[...] = headnorm_t(mm_t(TR_NQ, NSA_W), gcol_ref[0]).astype(BF16)
    dqt_ref[...] = headnorm_t(mm_t(TR_DQ, DSA_W), gcol_ref[1]).astype(BF16)
    dvt_ref[...] = mm_t(TR_DV, DSA_W).astype(BF16)
    iqt_ref[...] = (mm_t(TR_IQ, IDX_HEADS * IDX_DIM) * (IDX_DIM ** -0.5)).astype(BF16)
    vst_ref[...] = mm_t(TR_VSLC, LANES).astype(BF16)
    vwt_ref[...] = mm_t(TR_VWIN, LANES).astype(BF16)


def _inproj(x2, g_mix, w_nat, w_tr, gains, gcol, tm=512):
    S = x2.shape[0]
    row = lambda n: pl.BlockSpec((tm, n), lambda i: (i, 0))
    col = lambda n: pl.BlockSpec((n, tm), lambda i: (0, i))
    full = lambda a: pl.BlockSpec(a.shape, lambda i: (0,) * a.ndim)
    heads = lambda n: pl.BlockSpec((n, tm, HEAD_DIM), lambda i: (0, i, 0))
    grouped = [2 * NSA_KV_GROUPS, NSA_KV_GROUPS, NSA_KV_GROUPS]
    nat = [(DSA_W, BF16), (2 * D_MODEL, F32), (LANES, F32)]
    tr = [NSA_W, DSA_W, DSA_W, IDX_HEADS * IDX_DIM, LANES, LANES]
    return pl.pallas_call(
        _inproj_body,
        grid=(S // tm,),
        in_specs=[row(D_MODEL), full(g_mix), full(w_nat), full(w_tr), full(gains), full(gcol)],
        out_specs=[heads(n) for n in grouped] + [row(n) for n, _ in nat] + [col(n) for n in tr],
        out_shape=[jax.ShapeDtypeStruct((n, S, HEAD_DIM), BF16) for n in grouped]
                  + [jax.ShapeDtypeStruct((S, n), dt) for n, dt in nat]
                  + [jax.ShapeDtypeStruct((n, S), BF16) for n in tr],
        compiler_params=_cparams(("arbitrary",)),
        name="inproj",
    )(x2, g_mix, w_nat, w_tr, gains, gcol)


def _compress_body(xa_ref, xb_ref, w1a_ref, w1b_ref, w1f_ref, pe_ref, w2_ref, gk_ref, o_ref):
    is_k = pl.program_id(0) < NSA_KV_GROUPS
    ns = xa_ref.shape[0]
    pe_term = jnp.sum(pe_ref[...] * w1f_ref[...], axis=0, keepdims=True)
    second = _dot(xb_ref[...], w1b_ref[...])
    second = jnp.where(pl.program_id(1) == 3, pltpu.roll(second, ns - 1, axis=0), second)
    pre = _dot(xa_ref[...], w1a_ref[...]) + second + pe_term
    hdn = pre * jax.nn.sigmoid(pre)
    out = _dot(hdn.astype(BF16), w2_ref[...])
    normed = out * lax.rsqrt(jnp.mean(out * out, axis=-1, keepdims=True) + EPS) * gk_ref[...]
    o_ref[...] = jnp.where(is_k, normed, out).astype(BF16)


def _compress(xc, w1a, w1b, w1f, pe, w2, gk):
    n4, ns, kdim = xc.shape[0], xc.shape[1], xc.shape[2] // 4
    G = NSA_KV_GROUPS
    kv = lambda i: i // G
    return pl.pallas_call(
        _compress_body,
        grid=(n4, 4),
        in_specs=[pl.BlockSpec((None, ns, kdim), lambda i, j: (i, 0, j)),
                  pl.BlockSpec((None, ns, kdim), lambda i, j: (i, 0, (j + 1) % 4)),
                  pl.BlockSpec((None, kdim, HEAD_DIM), lambda i, j: (kv(i), 0, 0)),
                  pl.BlockSpec((None, kdim, HEAD_DIM), lambda i, j: (kv(i), 0, 0)),
                  pl.BlockSpec((None, 2 * kdim, HEAD_DIM), lambda i, j: (kv(i), 0, 0)),
                  pl.BlockSpec((None, 2 * kdim, 1), lambda i, j: (kv(i), 0, 0)),
                  pl.BlockSpec((None, HEAD_DIM, HEAD_DIM), lambda i, j: (kv(i), 0, 0)),
                  pl.BlockSpec((1, HEAD_DIM), lambda i, j: (0, 0))],
        out_specs=pl.BlockSpec((None, ns, HEAD_DIM), lambda i, j: (i, j, 0)),
        out_shape=jax.ShapeDtypeStruct((n4, 4 * ns, HEAD_DIM), BF16),
        compiler_params=_cparams(("arbitrary", "arbitrary")),
        name="compress",
    )(xc, xc, w1a, w1b, w1f, pe, w2, gk)


def _cmp_body(qt_ref, kc_ref, vct_ref, bt_ref, oct_ref, selt_ref, s_ref, imp_ref, *, NS):
    i = pl.program_id(1)

    @pl.when((pl.program_id(0) == 0) & (i == 0))
    def _():
        imp_ref[...] = jnp.zeros(imp_ref.shape, F32)

    imp_prev = imp_ref[...]
    ib = jnp.minimum(i, pl.num_programs(1) - 2)
    t0 = ib * Q_BLOCK
    T = Q_BLOCK
    R = NSA_GROUP_SIZE
    WIN = 2 * SUBLANES

    def step(ne):
        NE = 4 * ne
        row = lax.broadcasted_iota(I32, (NE, 1), 0)
        plane = row // ne
        nblk = row - plane * ne
        cend = NSA_SLC_BLOCK * nblk + NSA_CMP_STRIDE * plane + (NSA_CMP_LEN - 1)
        tq = t0 + lax.broadcasted_iota(I32, (1, T), 1)
        mask = cend <= tq
        kc = jnp.concatenate([kc_ref[j * NS:j * NS + ne, :] for j in range(4)], axis=0)
        vct = jnp.concatenate([vct_ref[:, j * NS:j * NS + ne] for j in range(4)], axis=1)
        wstart = jnp.clip((2 * ib - 2) // SUBLANES * SUBLANES, 0, ne - WIN)
        wrow = lax.broadcasted_iota(I32, (WIN, 1), 0)
        psum = jnp.zeros((NE, T), F32)
        for r in range(R):
            s_ref[r, 0:NE, :] = _dot(kc, qt_ref[r * HEAD_DIM:(r + 1) * HEAD_DIM, :])
        for r in range(R):
            bt = bt_ref[r]
            for j in range(4):
                add = jnp.zeros((WIN, T), F32)
                for k in range(4):
                    n = 2 * ib + 1 - k
                    add = add + jnp.where((wrow == n - wstart) & (n >= 0), bt[4 * j + k:4 * j + k + 1, :], 0.0)
                sl = pl.ds(pl.multiple_of(j * ne + wstart, SUBLANES), WIN)
                s_ref[r, sl, :] = s_ref[r, sl, :] + add
            sc = jnp.where(mask, s_ref[r, 0:NE, :], -jnp.inf)
            m = jnp.maximum(jnp.max(sc, axis=0, keepdims=True), -1e30)
            e = jnp.exp2(sc - m)
            pc = e / jnp.maximum(jnp.sum(e, axis=0, keepdims=True), 1e-30)
            oct_ref[r * HEAD_DIM:(r + 1) * HEAD_DIM, :] = _dot(vct, pc.astype(BF16))
            psum = psum + pc
        p0, p1, p2, p3 = (psum[j * ne:(j + 1) * ne, :] for j in range(4))
        p3s = jnp.where(lax.broadcasted_iota(I32, (ne, 1), 0) == 0, 0.0, pltpu.roll(p3, 1, axis=0))
        imp_ref[0:ne, :] = p0 + p1 + p2 + 0.5 * p3 + 0.5 * p3s
        if ne < NS:
            imp_ref[ne:NS, :] = jnp.zeros((NS - ne, T), F32)

        n_ar = lax.broadcasted_iota(I32, (NS, 1), 0)
        tqs = (i - 1) * Q_BLOCK + lax.broadcasted_iota(I32, (1, T), 1)
        cur = tqs // NSA_SLC_BLOCK
        valid = NSA_SLC_BLOCK * n_ar <= tqs
        forced = valid & ((n_ar == 0) | (n_ar > cur - NSA_LOCAL_BLOCKS))
        score = jnp.where(forced, 1e30, jnp.where(valid, imp_prev, -1.0))
        sel = jnp.zeros((NS, T), jnp.bool_)
        for _ in range(min(NSA_SLC_TOPK, NS)):
            mx = jnp.max(score, axis=0, keepdims=True)
            first = jnp.min(jnp.where(score == mx, n_ar, NS), axis=0, keepdims=True)
            pick = n_ar == first
            sel = sel | pick
            score = jnp.where(pick, -2.0, score)
        selb = jnp.where(sel, 1.0, 0.0).astype(BF16)
        if NS < LANES:
            selb = jnp.concatenate([selb, jnp.zeros((LANES - NS, T), BF16)], axis=0)
        for w in range(selt_ref.shape[0]):
            selt_ref[w] = selb[w * LANES:(w + 1) * LANES, :]

    quarter = NS // 4
    if quarter % WIN == 0:
        which = (2 * ib + 2 + quarter - 1) // quarter - 1
        for v in range(4):
            pl.when(which == v)(functools.partial(step, (v + 1) * quarter))
    else:
        step(NS)


def _cmp(nqt, kc, vct, bt, S):
    NS = S // NSA_SLC_BLOCK
    NCP = 4 * NS
    NW = max(NS // LANES, 1)
    G, R = NSA_KV_GROUPS, NSA_GROUP_SIZE
    nb = S // Q_BLOCK
    return pl.pallas_call(
        functools.partial(_cmp_body, NS=NS),
        grid=(G, nb + 1),
        in_specs=[pl.BlockSpec((R * HEAD_DIM, Q_BLOCK), lambda g, i: (g, jnp.minimum(i, nb - 1))),
                  pl.BlockSpec((None, NCP, HEAD_DIM), lambda g, i: (g, 0, 0)),
                  pl.BlockSpec((None, HEAD_DIM, NCP), lambda g, i: (g, 0, 0)),
                  pl.BlockSpec((R, 16, Q_BLOCK), lambda g, i: (g, 0, 0))],
        out_specs=[pl.BlockSpec((R * HEAD_DIM, Q_BLOCK), lambda g, i: (g, jnp.minimum(i, nb - 1))),
                   pl.BlockSpec((None, NW, LANES, Q_BLOCK), lambda g, i: (g, 0, 0, jnp.maximum(i - 1, 0)))],
        out_shape=[jax.ShapeDtypeStruct((NSA_W, S), F32),
                   jax.ShapeDtypeStruct((G, NW, LANES, S), BF16)],
        scratch_shapes=[pltpu.VMEM((R, NCP, Q_BLOCK), F32), pltpu.VMEM((NS, Q_BLOCK), F32)],
        compiler_params=_cparams(("arbitrary", "arbitrary")),
        name="cmp",
    )(nqt, kc, vct, bt)


def _flash_init(m_ref, l_ref, acc_ref):
    m_ref[...] = jnp.full(m_ref.shape, -1e30, F32)
    l_ref[...] = jnp.zeros(l_ref.shape, F32)
    acc_ref[...] = jnp.zeros(acc_ref.shape, F32)


def _flash_update(s, vt, m_ref, l_ref, acc_ref):
    m_old = m_ref[...]
    m_new = jnp.maximum(m_old, jnp.max(s, axis=0, keepdims=True))
    alpha = jnp.exp2(m_old - m_new)
    p = jnp.exp2(s - m_new)
    l_ref[...] = alpha * l_ref[...] + jnp.sum(p, axis=0, keepdims=True)
    acc_ref[...] = alpha * acc_ref[...] + _dot(vt, p.astype(BF16))
    m_ref[...] = m_new


def _slcwin_body(qt_ref, ks_ref, vst_ref, kw_ref, vwt_ref, selt_ref, tbt_ref, oct_ref, glt_ref, ot_ref,
                 ms_ref, ls_ref, as_ref, mw_ref, lw_ref, aw_ref, sbuf_ref, qaug_ref):
    i = pl.program_id(1)
    t0 = i * Q_BLOCK
    R = NSA_GROUP_SIZE
    T = Q_BLOCK
    q4t = jnp.concatenate([qt_ref[r * HEAD_DIM:(r + 1) * HEAD_DIM, :] for r in range(R)], axis=1)
    tq = lax.broadcasted_iota(I32, (1, T), 1)
    rep = lambda mk: jnp.concatenate([mk] * R, axis=1)
    tb4 = jnp.concatenate([tbt_ref[r] for r in range(R)], axis=1)

    def expand(w, base, nkeys):
        keyi = lax.broadcasted_iota(I32, (nkeys, LANES), 0)
        blk = lax.broadcasted_iota(I32, (nkeys, LANES), 1)
        e = jnp.where(blk == base + keyi // NSA_SLC_BLOCK, 1.0, 0.0).astype(BF16)
        return _dot(e, selt_ref[w])

    _flash_init(ms_ref, ls_ref, as_ref)
    _flash_init(mw_ref, lw_ref, aw_ref)
    n_win = selt_ref.shape[0]
    zpad = jnp.zeros((KAUG_W - HEAD_DIM - LANES, R * T), BF16)
    for w in range(n_win):
        qaug_ref[w] = jnp.concatenate([q4t, rep(selt_ref[w] - jnp.asarray(1.0, BF16)), zpad], axis=0)
    far_end = jnp.maximum(t0 - Q_BLOCK, 0) // KEY_TILE * KEY_TILE
    nfar = far_end // KEY_TILE
    tiles_per_window = LANES * NSA_SLC_BLOCK // KEY_TILE

    def score_far(kt, slot):
        r0 = pl.multiple_of(NSA_PADF + kt * KEY_TILE, LANES)
        w = jnp.minimum(kt // tiles_per_window, n_win - 1)
        sbuf_ref[slot] = _dot(ks_ref[pl.ds(r0, KEY_TILE), :], qaug_ref[w])

    def update_far(kt, slot):
        r0 = pl.multiple_of(NSA_PADF + kt * KEY_TILE, LANES)
        _flash_update(sbuf_ref[slot], vst_ref[:, pl.ds(r0, KEY_TILE)], ms_ref, ls_ref, as_ref)

    score_far(0, 0)

    def far2(k2, carry):
        kt = 2 * k2
        score_far(kt + 1, 1)
        update_far(kt, 0)
        score_far(kt + 2, 0)
        update_far(kt + 1, 1)
        return carry

    lax.fori_loop(0, nfar // 2, far2, 0)

    @pl.when(nfar % 2 == 1)
    def _():
        update_far(nfar - 1, 0)

    kq = lambda r0, n: _dot(ks_ref[pl.ds(r0, n), 0:HEAD_DIM], q4t)
    keyn = lax.broadcasted_iota(I32, (NEAR, 1), 0)
    near_ok = (keyn - Q_BLOCK <= tq) & (keyn + t0 - Q_BLOCK >= 0)
    rn = pl.multiple_of(t0 + NSA_PADF - Q_BLOCK, LANES)
    WF = NSA_WINDOW - Q_BLOCK
    rw = pl.multiple_of(t0 + NSA_PADF - NSA_WINDOW, LANES)
    keyw = lax.broadcasted_iota(I32, (WF, 1), 0)
    maskw = (keyw > tq) & (keyw + t0 - NSA_WINDOW >= 0)
    blocks = [jnp.maximum(2 * i - (NSA_WINDOW // NSA_SLC_BLOCK) + 2 * j, 0) for j in range(NSA_WINDOW // Q_BLOCK)]
    mexp = [expand(b // LANES, b % LANES, Q_BLOCK) for b in blocks]
    mask_sm = (jnp.concatenate(mexp[:-1], axis=0) > 0.5) & (keyw + t0 - NSA_WINDOW >= far_end)
    mask_sn = (jnp.concatenate([mexp[-1], expand(2 * i // LANES, 2 * i % LANES, Q_BLOCK)], axis=0) > 0.5) & near_ok
    s_sm = jnp.where(rep(mask_sm), kq(rw, WF), -jnp.inf)
    s_sn = jnp.where(rep(mask_sn), kq(rn, NEAR) + tb4, -jnp.inf)
    s_wf = jnp.where(rep(maskw), _dot(kw_ref[pl.ds(rw, WF), :], q4t), -jnp.inf)
    s_wn = jnp.where(rep(near_ok), _dot(kw_ref[pl.ds(rn, NEAR), :], q4t) + tb4, -jnp.inf)
    _flash_update(s_sm, vst_ref[:, pl.ds(rw, WF)], ms_ref, ls_ref, as_ref)
    _flash_update(s_sn, vst_ref[:, pl.ds(rn, NEAR)], ms_ref, ls_ref, as_ref)
    _flash_update(s_wf, vwt_ref[:, pl.ds(rw, WF)], mw_ref, lw_ref, aw_ref)
    _flash_update(s_wn, vwt_ref[:, pl.ds(rn, NEAR)], mw_ref, lw_ref, aw_ref)
    o_s = as_ref[...] / jnp.maximum(ls_ref[...], 1e-30)
    o_w = aw_ref[...] / jnp.maximum(lw_ref[...], 1e-30)

    gates = jax.nn.sigmoid(glt_ref[...])
    for r in range(R):
        hs = slice(r * HEAD_DIM, (r + 1) * HEAD_DIM)
        qs = slice(r * T, (r + 1) * T)
        o = (gates[3 * r:3 * r + 1, :] * oct_ref[hs, :]
             + gates[3 * r + 1:3 * r + 2, :] * o_s[:, qs]
             + gates[3 * r + 2:3 * r + 3, :] * o_w[:, qs])
        ot_ref[hs, :] = o.astype(BF16)


def _slcwin(nqt, ks, vst, kw, vwt, selt, tbt, oct, glt, S):
    G, R = NSA_KV_GROUPS, NSA_GROUP_SIZE
    spad = ks.shape[1]
    NW = selt.shape[1]
    once = pl.Buffered(1)
    kaspec = pl.BlockSpec((None, spad, KAUG_W), lambda g, i: (g, 0, 0), pipeline_mode=once)
    kspec = pl.BlockSpec((None, spad, HEAD_DIM), lambda g, i: (g, 0, 0), pipeline_mode=once)
    vspec = pl.BlockSpec((None, HEAD_DIM, spad), lambda g, i: (g, 0, 0), pipeline_mode=once)
    qspec = pl.BlockSpec((R * HEAD_DIM, Q_BLOCK), lambda g, i: (g, i))
    st = lambda n: pltpu.VMEM((n, R * Q_BLOCK), F32)
    return pl.pallas_call(
        _slcwin_body,
        grid=(G, S // Q_BLOCK),
        in_specs=[qspec, kaspec, vspec, kspec, vspec,
                  pl.BlockSpec((None, NW, LANES, Q_BLOCK), lambda g, i: (g, 0, 0, i)),
                  pl.BlockSpec((R, NEAR, Q_BLOCK), lambda g, i: (g, 0, 0)),
                  qspec,
                  pl.BlockSpec((None, 16, Q_BLOCK), lambda g, i: (g, 0, i))],
        out_specs=qspec,
        out_shape=jax.ShapeDtypeStruct((NSA_W, S), BF16),
        scratch_shapes=[st(1), st(1), st(HEAD_DIM), st(1), st(1), st(HEAD_DIM),
                        pltpu.VMEM((2, KEY_TILE, R * Q_BLOCK), F32),
                        pltpu.VMEM((NW, KAUG_W, R * Q_BLOCK), BF16)],
        compiler_params=_cparams(("arbitrary", "arbitrary")),
        name="slcwin",
    )(nqt, ks, vst, kw, vwt, selt, tbt, oct, glt)


INT_MIN = -2 ** 31
INT_MAX = 2 ** 31 - 1
MAX_PEEL_ROUNDS = 64
UNTESTED_BITS = 16
NEG_INF_KEY = int(np.array(-np.inf, np.float32).view(np.int32)) ^ 0x7FFFFFFF


def _dsa_body(dqt_ref, dk_ref, dvt_ref, iqt_ref, ikt_ref, iwt_ref, tbt_ref, ltri_ref, ot_ref,
              keys_ref, m_ref, l_ref, acc_ref, sbuf_ref, *, S, KEEP):
    i = pl.program_id(0)
    t0 = i * Q_BLOCK
    T = Q_BLOCK
    H = DSA_HEADS
    ntile = (t0 + Q_BLOCK + KEY_TILE - 1) // KEY_TILE
    tq = t0 + lax.broadcasted_iota(I32, (1, T), 1)
    key512 = lax.broadcasted_iota(I32, (KEY_TILE, 1), 0)
    ktile = lambda kt: keys_ref[pl.ds(pl.multiple_of(Q_BLOCK + kt * KEY_TILE, LANES), KEY_TILE), :]

    keys_ref[0:Q_BLOCK, :] = jnp.full((Q_BLOCK, T), NEG_INF_KEY, I32)
    iqt = iqt_ref[...]
    iwt = iwt_ref[...] * (IDX_HEADS ** -0.5)
    qis = [jnp.concatenate([iqt[(2 * j) * IDX_DIM:(2 * j + 1) * IDX_DIM, :],
                            iqt[(2 * j + 1) * IDX_DIM:(2 * j + 2) * IDX_DIM, :]], axis=1)
           for j in range(IDX_HEADS // 2)]
    wrow = [iwt[h:h + 1, :] for h in range(IDX_HEADS)]

    def idx_dots(kt, slot):
        ik = ikt_ref[:, pl.ds(pl.multiple_of(kt * KEY_TILE, KEY_TILE), KEY_TILE)]
        for j in range(IDX_HEADS // 2):
            sbuf_ref[slot, j] = _dot_tn(ik, qis[j])

    def idx_keys(kt, slot):
        c0 = pl.multiple_of(kt * KEY_TILE, KEY_TILE)
        sc = jnp.zeros((KEY_TILE, T), F32)
        for j in range(IDX_HEADS // 2):
            d = jnp.maximum(sbuf_ref[slot, j], 0.0)
            sc = sc + wrow[2 * j] * d[:, 0:T]
            sc = sc + wrow[2 * j + 1] * d[:, T:2 * T]
        sc = jnp.where(sc == 0.0, 0.0, sc)
        sc = jnp.where(c0 + key512 <= tq, sc, -jnp.inf)
        bits = pltpu.bitcast(sc, I32)
        keys_ref[pl.ds(pl.multiple_of(Q_BLOCK + c0, LANES), KEY_TILE), :] = jnp.where(
            bits < 0, bits ^ 0x7FFFFFFF, bits)

    idx_dots(0, 0)

    def score2(k2, carry):
        kt = 2 * k2
        idx_dots(kt + 1, 1)
        idx_keys(kt, 0)
        idx_dots(kt + 2, 0)
        idx_keys(kt + 1, 1)
        return carry

    lax.fori_loop(0, (ntile + 1) // 2, score2, 0)

    def count(pred):
        def body(kt, acc):
            hit = jnp.where(pred(ktile(kt), kt * KEY_TILE + key512), 1, 0)
            return acc + jnp.sum(hit.reshape(KEY_TILE // SUBLANES, SUBLANES, T), axis=0)
        acc = lax.fori_loop(0, ntile, body, jnp.zeros((SUBLANES, T), I32))
        return jnp.sum(acc, axis=0, keepdims=True)

    def min_at_least(lowest):
        def body(kt, acc):
            kk = ktile(kt)
            v = jnp.where(kk >= lowest, kk, INT_MAX)
            return jnp.minimum(acc, jnp.min(v.reshape(KEY_TILE // SUBLANES, SUBLANES, T), axis=0))
        acc = lax.fori_loop(0, ntile, body, jnp.full((SUBLANES, T), INT_MAX, I32))
        return jnp.min(acc, axis=0, keepdims=True)

    def any_open(st, over):
        _, excess, fin, _ = st
        return jnp.max(jnp.where((fin == 0) & (excess > over), 1, 0)) > 0

    def bisect(b, st):
        thr, excess, fin, need = st
        cand = thr + lax.shift_left(jnp.int32(1), 31 - b)
        n = count(lambda kk, ab: kk >= cand)
        ok = (n >= KEEP) & (fin == 0)
        return jnp.where(ok, cand, thr), jnp.where(ok, n - KEEP, excess), fin, need

    def peel(st):
        thr, excess, fin, need = st
        low = min_at_least(thr)
        n_low = count(lambda kk, ab: kk == low)
        active = (fin == 0) & (excess > 0)
        tied = active & (n_low > excess)
        drop = active & (n_low <= excess)
        return (jnp.where(tied, low, jnp.where(drop, low + 1, thr)), jnp.where(drop, excess - n_low, excess),
                jnp.where(tied, 1, fin), jnp.where(tied, n_low - excess, need))

    zero = jnp.zeros((1, T), I32)
    st = (jnp.full((1, T), INT_MIN, I32), KEY_TILE * ntile - KEEP + zero, zero, zero)
    st = peel(bisect(0, st))
    st = lax.fori_loop(1, UNTESTED_BITS, bisect, st)
    _, st = lax.while_loop(lambda c: (c[0] < 32) & any_open(c[1], 1),
                           lambda c: (c[0] + 1, bisect(c[0], c[1])), (jnp.int32(UNTESTED_BITS), st))
    _, st = lax.while_loop(lambda c: (c[0] < MAX_PEEL_ROUNDS) & any_open(c[1], 0),
                           lambda c: (c[0] + 1, peel(c[1])), (jnp.int32(0), st))
    thr, _, fin, need = st
    tie = (fin == 1) & (thr > NEG_INF_KEY)

    def tie_cut():
        needf = need.astype(F32)

        def body(kt, c):
            before, jmax = c
            is_tie = ktile(kt) == thr
            tied = jnp.where(is_tie, 1.0, 0.0)
            rank = before + _dot(ltri_ref[...], tied.astype(BF16))
            admit = is_tie & (rank < needf)
            jm = jnp.max(jnp.where(admit, kt * KEY_TILE + key512, -1), axis=0, keepdims=True)
            return before + jnp.sum(tied, axis=0, keepdims=True), jnp.maximum(jmax, jm)

        _, jmax = lax.fori_loop(0, ntile, body, (jnp.zeros((1, T), F32), jnp.full((1, T), -1, I32)))
        return jnp.where(tie, jmax + 1, S)

    jcut = lax.cond(jnp.max(jnp.where(tie, 1, 0)) > 0, tie_cut, lambda: jnp.full((1, T), S, I32))

    def chosen(kk, ab):
        return (kk > thr) | ((kk == thr) & (ab < jcut))

    _flash_init(m_ref, l_ref, acc_ref)
    lo = lax.broadcasted_iota(I32, (LANES, 1), 0) < HEAD_DIM
    qpairs = []
    for pr in range(H // 2):
        qp = dqt_ref[pr * LANES:(pr + 1) * LANES, :]
        zero = jnp.zeros_like(qp)
        qpairs.append(jnp.concatenate([jnp.where(lo, qp, zero), jnp.where(lo, zero, qp)], axis=1))

    def scores(r0, nkeys, mask, bias):
        mask2 = jnp.concatenate([mask, mask], axis=1)
        out = []
        for pr in range(H // 2):
            s = _dot(dk_ref[pl.ds(r0, nkeys), pr * LANES:(pr + 1) * LANES], qpairs[pr])
            if bias is not None:
                s = s + jnp.concatenate([bias[2 * pr], bias[2 * pr + 1]], axis=1)
            out.append(jnp.where(mask2, s, -jnp.inf))
        return out

    def update(pr, s, r0, nkeys):
        m_old = m_ref[pr]
        m_new = jnp.maximum(m_old, jnp.max(s, axis=0, keepdims=True))
        alpha = jnp.exp2(m_old - m_new)
        p = jnp.exp2(s - m_new)
        l_ref[pr] = alpha * l_ref[pr] + jnp.sum(p, axis=0, keepdims=True)
        m_ref[pr] = m_new
        vt = dvt_ref[pr * LANES:(pr + 1) * LANES, pl.ds(r0, nkeys)]
        acc_ref[pr] = alpha * acc_ref[pr] + _dot(vt, p.astype(BF16))

    far_end = t0 - Q_BLOCK
    nfar = jnp.maximum(far_end + KEY_TILE - 1, 0) // KEY_TILE
    far_row = lambda kt: pl.multiple_of(DSA_PADF + kt * KEY_TILE, LANES)

    def score_far(kt, slot):
        ab = kt * KEY_TILE + key512
        mask = chosen(ktile(jnp.minimum(kt, ntile - 1)), ab) & (ab < far_end)
        for pr, s in enumerate(scores(far_row(kt), KEY_TILE, mask, None)):
            sbuf_ref[slot, pr] = s

    def update_far(kt, slot):
        for pr in range(H // 2):
            update(pr, sbuf_ref[slot, pr], far_row(kt), KEY_TILE)

    score_far(0, 0)

    def far2(k2, carry):
        kt = 2 * k2
        score_far(kt + 1, 1)
        update_far(kt, 0)
        score_far(kt + 2, 0)
        update_far(kt + 1, 1)
        return carry

    lax.fori_loop(0, (nfar + 1) // 2, far2, 0)

    abn = lax.broadcasted_iota(I32, (NEAR, 1), 0) + t0 - Q_BLOCK
    kkn = keys_ref[pl.ds(pl.multiple_of(t0, LANES), NEAR), :]
    rn = pl.multiple_of(t0 + DSA_PADF - Q_BLOCK, LANES)
    for pr, s in enumerate(scores(rn, NEAR, chosen(kkn, abn) & (abn <= tq) & (abn >= 0), tbt_ref)):
        update(pr, s, rn, NEAR)

    for pr in range(H // 2):
        o = acc_ref[pr] / jnp.maximum(l_ref[pr], 1e-30)
        ot_ref[pr * LANES:(pr + 1) * LANES, :] = jnp.where(lo, o[:, 0:T], o[:, T:2 * T]).astype(BF16)


def _dsa(dqt, dkp, dvtp, iqt, ikt, iwt, tbt, ltri, S):
    KEEP = min(DSA_TOPK_MAX, S // 4)
    H = DSA_HEADS
    T = Q_BLOCK
    once = pl.Buffered(1)
    full = lambda a: pl.BlockSpec(a.shape, lambda i: (0,) * a.ndim, pipeline_mode=once)
    return pl.pallas_call(
        functools.partial(_dsa_body, S=S, KEEP=KEEP),
        grid=(S // T,),
        in_specs=[pl.BlockSpec((DSA_W, T), lambda i: (0, i)),
                  full(dkp), full(dvtp),
                  pl.BlockSpec((IDX_HEADS * IDX_DIM, T), lambda i: (0, i)),
                  full(ikt),
                  pl.BlockSpec((IDX_HEADS, T), lambda i: (0, i)),
                  full(tbt), full(ltri)],
        out_specs=pl.BlockSpec((DSA_W, T), lambda i: (0, i)),
        out_shape=jax.ShapeDtypeStruct((DSA_W, S), BF16),
        scratch_shapes=[pltpu.VMEM((Q_BLOCK + S + KEY_TILE, T), I32),
                        pltpu.VMEM((H // 2, 1, 2 * T), F32),
                        pltpu.VMEM((H // 2, 1, 2 * T), F32),
                        pltpu.VMEM((H // 2, LANES, 2 * T), F32),
                        pltpu.VMEM((2, H // 2, KEY_TILE, 2 * T), F32)],
        compiler_params=_cparams(("arbitrary",)),
        name="dsa",
    )(dqt, dkp, dvtp, iqt, ikt, iwt, tbt, ltri)


def _split_bf16(a):
    hi = a.astype(BF16)
    return hi, (a - hi.astype(F32)).astype(BF16)


def _merge_body(x_ref, oa_ref, ob_ref, gab_ref, wa_ref, wb_ref, wo_ref, gffn_ref, wr_ref, br_ref,
                x1_ref, h2_ref, ridx_ref, rw_ref):
    a = _dot_tn(oa_ref[...], wa_ref[...])
    b = _dot_tn(ob_ref[...], wb_ref[...])
    gab = gab_ref[...]
    merged = gab[:, :D_MODEL] * a + gab[:, D_MODEL:] * b
    x1 = x_ref[...] + _dot(merged.astype(BF16), wo_ref[...])
    x1_ref[...] = x1
    h2 = x1 * lax.rsqrt(jnp.mean(x1 * x1, axis=-1, keepdims=True) + EPS) * gffn_ref[...]
    h2_ref[...] = h2
    hh, hl = _split_bf16(h2)
    wh, wl = _split_bf16(wr_ref[...])
    logits = _dot(hh, wh) + _dot(hh, wl) + _dot(hl, wh) + br_ref[...]
    lane = lax.broadcasted_iota(I32, (1, LANES), 1)
    idx_out = jnp.zeros(logits.shape, I32)
    val_out = jnp.full(logits.shape, -jnp.inf, F32)
    for k in range(TOP_K):
        mx = jnp.max(logits, axis=-1, keepdims=True)
        first = jnp.min(jnp.where(logits == mx, lane, LANES), axis=-1, keepdims=True)
        idx_out = jnp.where(lane == k, first, idx_out)
        val_out = jnp.where(lane == k, mx, val_out)
        logits = jnp.where(lane == first, -jnp.inf, logits)
    e = jnp.exp(val_out - jnp.max(val_out, axis=-1, keepdims=True))
    ridx_ref[...] = idx_out
    rw_ref[...] = e / jnp.sum(e, axis=-1, keepdims=True)


def _merge(x2, oa, ob, gab, wa, wb, wo, gffn, wr, br, tm=512):
    S = x2.shape[0]
    row = lambda n: pl.BlockSpec((tm, n), lambda i: (i, 0))
    full = lambda a: pl.BlockSpec(a.shape, lambda i: (0,) * a.ndim)
    return pl.pallas_call(
        _merge_body,
        grid=(S // tm,),
        in_specs=[row(D_MODEL), pl.BlockSpec((NSA_W, tm), lambda i: (0, i)), pl.BlockSpec((DSA_W, tm), lambda i: (0, i)),
                  row(2 * D_MODEL),
                  full(wa), full(wb), full(wo), full(gffn), full(wr), full(br)],
        out_specs=[row(D_MODEL), row(D_MODEL), row(LANES), row(LANES)],
        out_shape=[jax.ShapeDtypeStruct((S, D_MODEL), F32), jax.ShapeDtypeStruct((S, D_MODEL), F32),
                   jax.ShapeDtypeStruct((S, LANES), I32), jax.ShapeDtypeStruct((S, LANES), F32)],
        compiler_params=_cparams(("arbitrary",)),
        name="merge",
    )(x2, oa, ob, gab, wa, wb, wo, gffn, wr, br)


def _rank_body(ridx_ref, rank_ref, cnt_ref, carry_ref):
    @pl.when(pl.program_id(0) == 0)
    def _():
        carry_ref[...] = jnp.zeros(carry_ref.shape, F32)

    ridx = ridx_ref[...]
    tm = ridx.shape[0]
    lane = lax.broadcasted_iota(I32, (1, LANES), 1)
    hits = [lane == ridx[:, k:k + 1] for k in range(TOP_K)]
    member = jnp.zeros((tm, LANES), F32)
    for hk in hits:
        member = member + jnp.where(hk, 1.0, 0.0)
    ri = lax.broadcasted_iota(I32, (tm, tm), 0)
    ci = lax.broadcasted_iota(I32, (tm, tm), 1)
    lower = jnp.where(ci < ri, 1.0, 0.0).astype(BF16)
    before = _dot(lower, member.astype(BF16)) + carry_ref[0:1, :]
    out = jnp.zeros((tm, LANES), I32)
    for k, hk in enumerate(hits):
        rk = jnp.sum(jnp.where(hk, before, 0.0), axis=-1, keepdims=True)
        out = jnp.where(lane == k, rk.astype(I32), out)
    rank_ref[...] = out
    total = carry_ref[0:1, :] + jnp.sum(member, axis=0, keepdims=True)
    carry_ref[...] = jnp.broadcast_to(total, carry_ref.shape)
    cnt_ref[...] = jnp.broadcast_to(total, cnt_ref.shape).astype(I32)


def _rank(ridx, tm=512):
    S = ridx.shape[0]
    return pl.pallas_call(
        _rank_body,
        grid=(S // tm,),
        in_specs=[pl.BlockSpec((tm, LANES), lambda i: (i, 0))],
        out_specs=[pl.BlockSpec((tm, LANES), lambda i: (i, 0)), pl.BlockSpec((8, LANES), lambda i: (0, 0))],
        out_shape=[jax.ShapeDtypeStruct((S, LANES), I32), jax.ShapeDtypeStruct((8, LANES), I32)],
        scratch_shapes=[pltpu.VMEM((8, LANES), F32)],
        compiler_params=_cparams(("arbitrary",)),
        name="rank",
    )(ridx)


def _dispatch_body(dest_ref, zrow_ref, nu_ref, h_ref, xe_ref, zbuf_ref, sem, zsem):
    i = pl.program_id(0)
    tm = h_ref.shape[0]

    @pl.when(i == 0)
    def _():
        zbuf_ref[...] = jnp.zeros(zbuf_ref.shape, F32)
        fill = lambda row: pltpu.make_async_copy(zbuf_ref, xe_ref.at[pl.ds(row, MOE_ROWS)], zsem)
        n_chunks = xe_ref.shape[0] // MOE_ROWS
        for wait in (False, True):
            for e in range(N_EXPERTS):
                @pl.when(zrow_ref[e] >= 0)
                def _():
                    cp = fill(pl.multiple_of(jnp.maximum(zrow_ref[e], 0), MOE_ROWS))
                    cp.wait() if wait else cp.start()

            def tail(j, c):
                cp = fill(pl.multiple_of(j * MOE_ROWS, MOE_ROWS))
                cp.wait() if wait else cp.start()
                return c
            lax.fori_loop(nu_ref[0], n_chunks, tail, 0)

    def start(r, c):
        for k in range(TOP_K):
            d = dest_ref[(i * tm + r) * TOP_K + k]
            pltpu.make_async_copy(h_ref.at[pl.ds(r, 1)], xe_ref.at[pl.ds(d, 1)],
                                  sem).start(priority=k % 2)
        return c

    lax.fori_loop(0, tm, start, 0, unroll=ROW_DMA_UNROLL)
    for _ in range(TOP_K):
        pltpu.make_async_copy(h_ref, xe_ref.at[pl.ds(0, tm)], sem).wait()


def _dispatch(dest, zrow, n_used, h2, n_rows, tm=256):
    S = h2.shape[0]
    return pl.pallas_call(
        _dispatch_body,
        grid_spec=pltpu.PrefetchScalarGridSpec(
            num_scalar_prefetch=3,
            grid=(S // tm,),
            in_specs=[pl.BlockSpec((tm, D_MODEL), lambda i, d, z, n: (i, 0))],
            out_specs=pl.BlockSpec(memory_space=pl.ANY),
            scratch_shapes=[pltpu.VMEM((MOE_ROWS, D_MODEL), F32), pltpu.SemaphoreType.DMA(()),
                            pltpu.SemaphoreType.DMA(())]),
        out_shape=jax.ShapeDtypeStruct((n_rows, D_MODEL), F32),
        compiler_params=_cparams(("arbitrary",)),
        name="dispatch",
    )(dest, zrow, n_used, h2)


def _experts_body(ce_ref, nu_ref, x_ref, w1_ref, b1_ref, w2_ref, b2_ref, y_ref, w1b_ref, w2b_ref):
    c = pl.program_id(0)

    @pl.when((c == 0) | (ce_ref[c] != ce_ref[jnp.maximum(c - 1, 0)]))
    def _():
        w1b_ref[...] = w1_ref[...].astype(BF16)
        w2b_ref[...] = w2_ref[...].astype(BF16)

    @pl.when(c < nu_ref[0])
    def _():
        gu = _dot(x_ref[...].astype(BF16), w1b_ref[...]) + b1_ref[...]
        gate = jnp.minimum(gu[:, :D_FF], SWIGLU_LIMIT)
        lin = jnp.clip(gu[:, D_FF:], -SWIGLU_LIMIT, SWIGLU_LIMIT)
        act = (lin + 1.0) * gate * jax.nn.sigmoid(SWIGLU_ALPHA * gate)
        y_ref[...] = _dot(act.astype(BF16), w2b_ref[...]) + b2_ref[...]

    @pl.when(c >= nu_ref[0])
    def _():
        y_ref[...] = jnp.zeros(y_ref.shape, F32)


def _experts(chunk_e, n_used, xe, w1, b1, w2, b2):
    P = xe.shape[0]
    nch = P // MOE_ROWS
    return pl.pallas_call(
        _experts_body,
        grid_spec=pltpu.PrefetchScalarGridSpec(
            num_scalar_prefetch=2,
            grid=(nch,),
            in_specs=[pl.BlockSpec((MOE_ROWS, D_MODEL), lambda c, ce, nu: (c, 0)),
                      pl.BlockSpec((None, D_MODEL, 2 * D_FF), lambda c, ce, nu: (ce[c], 0, 0)),
                      pl.BlockSpec((None, 1, 2 * D_FF), lambda c, ce, nu: (ce[c], 0, 0)),
                      pl.BlockSpec((None, D_FF, D_MODEL), lambda c, ce, nu: (ce[c], 0, 0)),
                      pl.BlockSpec((None, 1, D_MODEL), lambda c, ce, nu: (ce[c], 0, 0))],
            out_specs=pl.BlockSpec((MOE_ROWS, D_MODEL), lambda c, ce, nu: (c, 0)),
            scratch_shapes=[pltpu.VMEM((D_MODEL, 2 * D_FF), BF16), pltpu.VMEM((D_FF, D_MODEL), BF16)]),
        out_shape=jax.ShapeDtypeStruct((P, D_MODEL), F32),
        compiler_params=_cparams(("arbitrary",)),
        name="experts",
    )(chunk_e, n_used, xe, w1, b1, w2, b2)


def _final_body(dest_ref, x1_ref, rw_ref, p_ref, gple_ref, wg_ref, wp_ref, y_ref, o_ref, rows_ref, sems):
    i = pl.program_id(0)
    tm = x1_ref.shape[0]

    def gather(blk, slot):
        def start(r, c):
            for k in range(TOP_K):
                d = dest_ref[(blk * tm + r) * TOP_K + k]
                pltpu.make_async_copy(y_ref.at[pl.ds(d, 1)], rows_ref.at[slot, k, pl.ds(r, 1)],
                                      sems.at[slot]).start(priority=k % 2)
            return c
        lax.fori_loop(0, tm, start, 0, unroll=ROW_DMA_UNROLL)

    def drain(which):
        for k in range(TOP_K):
            pltpu.make_async_copy(y_ref.at[pl.ds(0, tm)], rows_ref.at[which, k], sems.at[which]).wait()

    @pl.when(i == 0)
    def _():
        gather(0, 0)

    slot = i % 2
    last = pl.num_programs(0) - 1
    drain(slot)
    nxt = jnp.minimum(i + 1, last)
    for r in range(tm):
        for k in range(TOP_K):
            d = dest_ref[(nxt * tm + r) * TOP_K + k]
            pltpu.make_async_copy(y_ref.at[pl.ds(d, 1)], rows_ref.at[1 - slot, k, pl.ds(r, 1)],
                                  sems.at[1 - slot]).start(priority=k % 2)
    rw = rw_ref[...]
    x2 = x1_ref[...]
    for k in range(TOP_K):
        x2 = x2 + rw[:, k:k + 1] * rows_ref[slot, k]
    hn = x2 * lax.rsqrt(jnp.mean(x2 * x2, axis=-1, keepdims=True) + EPS) * gple_ref[...]
    gate = jax.nn.sigmoid(_dot(hn.astype(BF16), wg_ref[...]))
    o_ref[...] = x2 + gate * _dot(p_ref[...].astype(BF16), wp_ref[...])

    @pl.when(i == last)
    def _():
        drain(1 - slot)


def _final(dest, x1, rw, p2, gple, wg, wp, yexp, tm=256):
    S = x1.shape[0]
    row = lambda n: pl.BlockSpec((tm, n), lambda i, d: (i, 0))
    full = lambda a: pl.BlockSpec(a.shape, lambda i, d: (0,) * a.ndim)
    return pl.pallas_call(
        _final_body,
        grid_spec=pltpu.PrefetchScalarGridSpec(
            num_scalar_prefetch=1,
            grid=(S // tm,),
            in_specs=[row(D_MODEL), row(LANES), row(PLE_DIM), full(gple), full(wg), full(wp),
                      pl.BlockSpec(memory_space=pl.ANY)],
            out_specs=row(D_MODEL),
            scratch_shapes=[pltpu.VMEM((2, TOP_K, tm, D_MODEL), F32), pltpu.SemaphoreType.DMA((2,))]),
        out_shape=jax.ShapeDtypeStruct((S, D_MODEL), F32),
        compiler_params=_cparams(("arbitrary",)),
        name="final",
    )(dest, x1, rw, p2, gple, wg, wp, yexp)


def _distance_bias(rel_cols):
    return (rel_cols[_rel_bucket_table(NEAR)] - rel_cols[REL_BUCKETS - 1][None, :]).T * LOG2E


def _near_bias_t(rel_cols):
    h = rel_cols.shape[1]
    g = jnp.pad(_distance_bias(rel_cols), ((0, 0), (Q_BLOCK, Q_BLOCK)))
    period = 2 * NEAR
    skew = jnp.tile(g, (1, NEAR))[:, :NEAR * (period - 1)].reshape(h, NEAR, period - 1)
    return skew[:, :, NEAR:NEAR + Q_BLOCK]


def _cmp_bias_t(rel_cols):
    g = jnp.pad(_distance_bias(rel_cols), ((0, 0), (NEAR, Q_BLOCK)))
    rows = []
    for j in range(4):
        for k in range(4):
            c0 = -NSA_CMP_STRIDE * j - (NSA_CMP_LEN - 1) + NSA_SLC_BLOCK * (k - 1)
            rows.append(g[:, c0 + NEAR:c0 + NEAR + Q_BLOCK])
    return jnp.stack(rows, axis=1)


def kernel(x, p, w_in, g_mix, g_q_nsa, g_k_nsa, g_q_dsa, g_k_dsa, pe_ck, w_ck1, w_ck2, pe_cv, w_cv1, w_cv2,
           w_branch_a, w_branch_b, w_out, rel_bias, g_ffn, w_router, b_router, w_e1, b_e1, w_e2, b_e2,
           g_ple, w_ple, w_ple_gate):
    B, S, D = x.shape
    assert B == 1 and D == D_MODEL and S % KEY_TILE == 0 and w_in.shape[0] == 1
    G, R = NSA_KV_GROUPS, NSA_GROUP_SIZE
    x2 = x.reshape(S, D)

    w0 = w_in[0]
    w_nat = jnp.concatenate([w0[:, 512:768], w0[:, 768:896], w0[:, 1024:1152], w0[:, 1816:2328], w0[:, 3136:5184],
                             w0[:, 1280:1304], w0[:, 3096:3136], jnp.zeros((D, LANES - 64), F32)], axis=1).astype(BF16)
    w_tr = jnp.concatenate([w0[:, 0:512], w0[:, 1304:1816], w0[:, 2328:2840], w0[:, 2840:3096],
                            w0[:, 896:1024], w0[:, 1152:1280]], axis=1).T.astype(BF16)
    two = lambda g: jnp.tile(g.reshape(1, HEAD_DIM), (1, 2))
    qscale = HEAD_DIM ** -0.5 * LOG2E
    gains = jnp.concatenate([two(g_k_nsa[0]), two(g_k_dsa[0])], axis=0)
    gcol = jnp.stack([g_q_nsa[0] * qscale, g_q_dsa[0] * qscale]).reshape(2, HEAD_DIM, 1)
    (cmpraw, kslc, kwin, dk, gab, small, nqt, dqt, dvt, iqt, vst2, vwt2) = _inproj(
        x2, g_mix[0].reshape(1, D), w_nat, w_tr, gains, gcol)

    NS = S // NSA_SLC_BLOCK
    xc = cmpraw.reshape(2 * G, NS, 4 * NSA_CMP_STRIDE * HEAD_DIM)
    half = NSA_CMP_STRIDE * HEAD_DIM
    w1 = jnp.stack([w_ck1[0], w_cv1[0]]).reshape(2, 2 * half, HEAD_DIM)
    pe = jnp.stack([pe_ck[0], pe_cv[0]]).reshape(2, 2 * half, 1)
    w2 = jnp.stack([w_ck2[0], w_cv2[0]]).astype(BF16)
    kvc = _compress(xc, w1[:, :half].astype(BF16), w1[:, half:].astype(BF16), w1, pe, w2,
                    g_k_nsa[0].reshape(1, HEAD_DIM))

    rel_nsa = rel_bias[:, :NSA_HEADS]
    rel_dsa = rel_bias[:, NSA_HEADS:]
    oct, selt = _cmp(nqt, kvc[:G], kvc[G:].transpose(0, 2, 1), _cmp_bias_t(rel_nsa), S)
    kgroups = lambda a: jnp.pad(a, ((0, 0), (NSA_PADF, TAIL_PAD), (0, 0)))
    vgroups = lambda a: jnp.pad(a.reshape(G, HEAD_DIM, S), ((0, 0), (0, 0), (NSA_PADF, TAIL_PAD)))
    glt = small[:, SMALL_GATE:SMALL_GATE + 3 * NSA_HEADS].reshape(S, G, 3 * R).transpose(1, 2, 0)
    glt = jnp.pad(glt, ((0, 0), (0, 16 - 3 * R), (0, 0)))
    blk_lane = (np.arange(S) // NSA_SLC_BLOCK) % LANES
    onehot = jnp.asarray(np.eye(LANES, dtype=np.float32)[blk_lane] * MASK_BIG, BF16)
    kaug = jnp.concatenate([kslc,
                            jnp.broadcast_to(onehot, (G, S, LANES)),
                            jnp.zeros((G, S, KAUG_W - HEAD_DIM - LANES), BF16)], axis=-1)
    kaug = jnp.pad(kaug, ((0, 0), (NSA_PADF, TAIL_PAD), (0, 0)))
    o_at = _slcwin(nqt, kaug, vgroups(vst2), kgroups(kwin), vgroups(vwt2), selt,
                   _near_bias_t(rel_nsa), oct, glt, S)

    dkp = jnp.pad(dk, ((DSA_PADF, TAIL_PAD), (0, 0)))
    dvtp = jnp.pad(dvt, ((0, 0), (DSA_PADF, TAIL_PAD)))
    ikt = jnp.pad(small[:, SMALL_IK:SMALL_IK + IDX_DIM].T.astype(BF16), ((0, 0), (0, 2 * KEY_TILE)))
    iwt = small[:, SMALL_IW:SMALL_IW + IDX_HEADS].T
    ltri = jnp.asarray(np.tril(np.ones((KEY_TILE, KEY_TILE), np.float32), -1), BF16)
    o_bt = _dsa(dqt, dkp, dvtp, iqt, ikt, iwt, _near_bias_t(rel_dsa), ltri, S)

    wr = jnp.pad(w_router[0], ((0, 0), (0, LANES - N_EXPERTS)))
    br = jnp.pad(b_router[0].reshape(1, N_EXPERTS), ((0, 0), (0, LANES - N_EXPERTS)), constant_values=-1e30)
    x1, h2, ridx, rw = _merge(x2, o_at, o_bt, gab, w_branch_a[0].astype(BF16), w_branch_b[0].astype(BF16),
                              w_out[0].astype(BF16), g_ffn[0].reshape(1, D), wr, br)

    rank, cnt = _rank(ridx)
    counts = cnt[0, :N_EXPERTS]
    padded = (counts + MOE_ROWS - 1) // MOE_ROWS * MOE_ROWS
    pad_end = jnp.cumsum(padded)
    pad_start = pad_end - padded
    eidx = ridx[:, :TOP_K]
    dest = (jnp.sum(jnp.where(eidx[..., None] == jnp.arange(N_EXPERTS), pad_start, 0), axis=-1)
            + rank[:, :TOP_K]).reshape(-1).astype(I32)
    n_chunks = -(-(S * TOP_K) // MOE_ROWS) + N_EXPERTS
    chunk_start = jnp.arange(n_chunks, dtype=I32) * MOE_ROWS
    chunk_e = jnp.minimum(jnp.sum(chunk_start[:, None] >= pad_end[None, :], axis=-1), N_EXPERTS - 1).astype(I32)
    n_used = (pad_end[-1] // MOE_ROWS).astype(I32).reshape(1)
    zrow = jnp.where(padded > 0, pad_end - MOE_ROWS, -1).astype(I32)
    xe = _dispatch(dest, zrow, n_used, h2, n_chunks * MOE_ROWS)
    yexp = _experts(chunk_e, n_used, xe, w_e1[0], b_e1[0].reshape(N_EXPERTS, 1, 2 * D_FF),
                    w_e2[0], b_e2[0].reshape(N_EXPERTS, 1, D))

    out = _final(dest, x1, rw, p[0].reshape(S, PLE_DIM), g_ple[0].reshape(1, D),
                 w_ple_gate[0].astype(BF16), w_ple[0].astype(BF16), yexp)
    return out.reshape(B, S, D)
```

```python
import functools
import math

import numpy as np
import jax
import jax.numpy as jnp
from jax import lax
from jax.experimental import pallas as pl
from jax.experimental.pallas import tpu as pltpu

F32 = jnp.float32
BF16 = jnp.bfloat16
I32 = jnp.int32

D_MODEL = 1024
PLE_DIM = 256
HEAD_DIM = 64
NSA_HEADS = 8
NSA_KV_GROUPS = 2
NSA_GROUP_SIZE = NSA_HEADS // NSA_KV_GROUPS
NSA_CMP_LEN = 32
NSA_CMP_STRIDE = 16
NSA_SLC_BLOCK = 64
NSA_SLC_TOPK = 16
NSA_LOCAL_BLOCKS = 2
NSA_WINDOW = 512
DSA_HEADS = 8
IDX_HEADS = 8
IDX_DIM = 32
DSA_TOPK_MAX = 256
Q_BLOCK = 128
REL_BUCKETS = 32
REL_MAX_EXACT = 16
REL_MAX_DIST = 128
N_EXPERTS = 32
TOP_K = 4
D_FF = 1024
SWIGLU_LIMIT = 7.0
SWIGLU_ALPHA = 1.702
MOE_ROWS = 512
EPS = 1e-6
LOG2E = math.log2(math.e)

NSA_W = NSA_HEADS * HEAD_DIM
DSA_W = DSA_HEADS * HEAD_DIM

LANES = 128
SUBLANES = 8
KEY_TILE = 512
NEAR = 2 * Q_BLOCK
NSA_PADF = NSA_WINDOW
DSA_PADF = Q_BLOCK
KAUG_W = 2 * LANES
MASK_BIG = 2.0 ** 100
TAIL_PAD = 3 * KEY_TILE
VMEM_LIMIT = 56 * 1024 * 1024
ROW_DMA_UNROLL = 8

NAT_CMP, NAT_KSLC, NAT_KWIN, NAT_DK, NAT_GAB, NAT_SMALL = 0, 256, 384, 512, 1024, 3072
TR_NQ, TR_DQ, TR_DV, TR_IQ, TR_VSLC, TR_VWIN = 0, 512, 1024, 1536, 1792, 1920
SMALL_GATE, SMALL_IK, SMALL_IW = 0, 24, 56


def _rel_bucket_table(n):
    d = np.arange(n)
    nf = np.maximum(d, 1).astype(np.float64)
    large = REL_MAX_EXACT + (np.log(nf / REL_MAX_EXACT) / math.log(REL_MAX_DIST / REL_MAX_EXACT)
                             * (REL_BUCKETS - REL_MAX_EXACT)).astype(np.int64)
    large = np.minimum(large, REL_BUCKETS - 1)
    return np.where(d < REL_MAX_EXACT, d, large)


def _cparams(sem, vmem=VMEM_LIMIT):
    return pltpu.CompilerParams(dimension_semantics=sem, vmem_limit_bytes=vmem)


def _dot(a, b):
    return jnp.dot(a, b, preferred_element_type=F32)


def _dot_tn(a, b):
    return lax.dot_general(a, b, (((0,), (0,)), ((), ())), preferred_element_type=F32)


def _inproj_body(x_ref, gmix_ref, w_ref, wt_ref, gains_ref, gcol_ref,
                 cmp_ref, kslc_ref, kwin_ref, dk_ref, gab_ref, small_ref,
                 nqt_ref, dqt_ref, dvt_ref, iqt_ref, vst_ref, vwt_ref):
    x = x_ref[...]
    h = x * lax.rsqrt(jnp.mean(x * x, axis=-1, keepdims=True) + EPS) * gmix_ref[...]
    hb = h.astype(BF16)
    tm = hb.shape[0]
    lo = lax.broadcasted_iota(I32, (1, LANES), 1) < HEAD_DIM

    def mm(off, n):
        return _dot(hb, w_ref[:, off:off + n])

    def headnorm(y, g):
        y2 = y * y
        s_lo = jnp.sum(jnp.where(lo, y2, 0.0), axis=-1, keepdims=True)
        s_hi = jnp.sum(jnp.where(lo, 0.0, y2), axis=-1, keepdims=True)
        r = jnp.where(lo, lax.rsqrt(s_lo * (1.0 / HEAD_DIM) + EPS), lax.rsqrt(s_hi * (1.0 / HEAD_DIM) + EPS))
        return y * r * g

    def headnorm_t(y, g):
        y3 = y.reshape(y.shape[0] // HEAD_DIM, HEAD_DIM, tm)
        r = lax.rsqrt(jnp.mean(y3 * y3, axis=1, keepdims=True) + EPS)
        return (y3 * r * g).reshape(y.shape)

    g_kn, g_kd = gains_ref[0:1, :], gains_ref[1:2, :]
    def put_heads(ref, y):
        for k in range(ref.shape[0]):
            ref[k] = y[:, k * HEAD_DIM:(k + 1) * HEAD_DIM].astype(BF16)

    put_heads(cmp_ref, mm(NAT_CMP, 256))
    put_heads(kslc_ref, headnorm(mm(NAT_KSLC, LANES), g_kn))
    put_heads(kwin_ref, headnorm(mm(NAT_KWIN, LANES), g_kn))
    for c in range(DSA_W // LANES):
        dk_ref[:, c * LANES:(c + 1) * LANES] = headnorm(mm(NAT_DK + c * LANES, LANES), g_kd).astype(BF16)
    gab_ref[...] = jax.nn.sigmoid(mm(NAT_GAB, 2 * D_MODEL))
    small_ref[...] = mm(NAT_SMALL, LANES)

    def mm_t(off, n):
        return lax.dot_general(wt_ref[off:off + n, :], hb, (((1,), (1,)), ((), ())), preferred_element_type=F32)

    nqt_ref[...] = headnorm_t(mm_t(TR_NQ, NSA_W), gcol_ref[0]).astype(BF16)
    dqt_ref[...] = headnorm_t(mm_t(TR_DQ, DSA_W), gcol_ref[1]).astype(BF16)
    dvt_ref[...] = mm_t(TR_DV, DSA_W).astype(BF16)
    iqt_ref[...] = (mm_t(TR_IQ, IDX_HEADS * IDX_DIM) * (IDX_DIM ** -0.5)).astype(BF16)
    vst_ref[...] = mm_t(TR_VSLC, LANES).astype(BF16)
    vwt_ref[...] = mm_t(TR_VWIN, LANES).astype(BF16)


def _inproj(x2, g_mix, w_nat, w_tr, gains, gcol, tm=512):
    S = x2.shape[0]
    row = lambda n: pl.BlockSpec((tm, n), lambda i: (i, 0))
    col = lambda n: pl.BlockSpec((n, tm), lambda i: (0, i))
    full = lambda a: pl.BlockSpec(a.shape, lambda i: (0,) * a.ndim)
    heads = lambda n: pl.BlockSpec((n, tm, HEAD_DIM), lambda i: (0, i, 0))
    grouped = [2 * NSA_KV_GROUPS, NSA_KV_GROUPS, NSA_KV_GROUPS]
    nat = [(DSA_W, BF16), (2 * D_MODEL, F32), (LANES, F32)]
    tr = [NSA_W, DSA_W, DSA_W, IDX_HEADS * IDX_DIM, LANES, LANES]
    return pl.pallas_call(
        _inproj_body,
        grid=(S // tm,),
        in_specs=[row(D_MODEL), full(g_mix), full(w_nat), full(w_tr), full(gains), full(gcol)],
        out_specs=[heads(n) for n in grouped] + [row(n) for n, _ in nat] + [col(n) for n in tr],
        out_shape=[jax.ShapeDtypeStruct((n, S, HEAD_DIM), BF16) for n in grouped]
                  + [jax.ShapeDtypeStruct((S, n), dt) for n, dt in nat]
                  + [jax.ShapeDtypeStruct((n, S), BF16) for n in tr],
        compiler_params=_cparams(("arbitrary",)),
        name="inproj",
    )(x2, g_mix, w_nat, w_tr, gains, gcol)


def _compress_body(xa_ref, xb_ref, w1a_ref, w1b_ref, w1f_ref, pe_ref, w2_ref, gk_ref, o_ref):
    is_k = pl.program_id(0) < NSA_KV_GROUPS
    ns = xa_ref.shape[0]
    pe_term = jnp.sum(pe_ref[...] * w1f_ref[...], axis=0, keepdims=True)
    second = _dot(xb_ref[...], w1b_ref[...])
    second = jnp.where(pl.program_id(1) == 3, pltpu.roll(second, ns - 1, axis=0), second)
    pre = _dot(xa_ref[...], w1a_ref[...]) + second + pe_term
    hdn = pre * jax.nn.sigmoid(pre)
    out = _dot(hdn.astype(BF16), w2_ref[...])
    normed = out * lax.rsqrt(jnp.mean(out * out, axis=-1, keepdims=True) + EPS) * gk_ref[...]
    o_ref[...] = jnp.where(is_k, normed, out).astype(BF16)


def _compress(xc, w1a, w1b, w1f, pe, w2, gk):
    n4, ns, kdim = xc.shape[0], xc.shape[1], xc.shape[2] // 4
    G = NSA_KV_GROUPS
    kv = lambda i: i // G
    return pl.pallas_call(
        _compress_body,
        grid=(n4, 4),
        in_specs=[pl.BlockSpec((None, ns, kdim), lambda i, j: (i, 0, j)),
                  pl.BlockSpec((None, ns, kdim), lambda i, j: (i, 0, (j + 1) % 4)),
                  pl.BlockSpec((None, kdim, HEAD_DIM), lambda i, j: (kv(i), 0, 0)),
                  pl.BlockSpec((None, kdim, HEAD_DIM), lambda i, j: (kv(i), 0, 0)),
                  pl.BlockSpec((None, 2 * kdim, HEAD_DIM), lambda i, j: (kv(i), 0, 0)),
                  pl.BlockSpec((None, 2 * kdim, 1), lambda i, j: (kv(i), 0, 0)),
                  pl.BlockSpec((None, HEAD_DIM, HEAD_DIM), lambda i, j: (kv(i), 0, 0)),
                  pl.BlockSpec((1, HEAD_DIM), lambda i, j: (0, 0))],
        out_specs=pl.BlockSpec((None, ns, HEAD_DIM), lambda i, j: (i, j, 0)),
        out_shape=jax.ShapeDtypeStruct((n4, 4 * ns, HEAD_DIM), BF16),
        compiler_params=_cparams(("arbitrary", "arbitrary")),
        name="compress",
    )(xc, xc, w1a, w1b, w1f, pe, w2, gk)


def _cmp_body(qt_ref, kc_ref, vct_ref, bt_ref, oct_ref, selt_ref, s_ref, imp_ref, *, NS):
    i = pl.program_id(1)

    @pl.when((pl.program_id(0) == 0) & (i == 0))
    def _():
        imp_ref[...] = jnp.zeros(imp_ref.shape, F32)

    imp_prev = imp_ref[...]
    ib = jnp.minimum(i, pl.num_programs(1) - 2)
    t0 = ib * Q_BLOCK
    T = Q_BLOCK
    R = NSA_GROUP_SIZE
    WIN = 2 * SUBLANES

    def step(ne):
        NE = 4 * ne
        row = lax.broadcasted_iota(I32, (NE, 1), 0)
        plane = row // ne
        nblk = row - plane * ne
        cend = NSA_SLC_BLOCK * nblk + NSA_CMP_STRIDE * plane + (NSA_CMP_LEN - 1)
        tq = t0 + lax.broadcasted_iota(I32, (1, T), 1)
        mask = cend <= tq
        kc = jnp.concatenate([kc_ref[j * NS:j * NS + ne, :] for j in range(4)], axis=0)
        vct = jnp.concatenate([vct_ref[:, j * NS:j * NS + ne] for j in range(4)], axis=1)
        wstart = jnp.clip((2 * ib - 2) // SUBLANES * SUBLANES, 0, ne - WIN)
        wrow = lax.broadcasted_iota(I32, (WIN, 1), 0)
        psum = jnp.zeros((NE, T), F32)
        for r in range(R):
            s_ref[r, 0:NE, :] = _dot(kc, qt_ref[r * HEAD_DIM:(r + 1) * HEAD_DIM, :])
        for r in range(R):
            bt = bt_ref[r]
            for j in range(4):
                add = jnp.zeros((WIN, T), F32)
                for k in range(4):
                    n = 2 * ib + 1 - k
                    add = add + jnp.where((wrow == n - wstart) & (n >= 0), bt[4 * j + k:4 * j + k + 1, :], 0.0)
                sl = pl.ds(pl.multiple_of(j * ne + wstart, SUBLANES), WIN)
                s_ref[r, sl, :] = s_ref[r, sl, :] + add
            sc = jnp.where(mask, s_ref[r, 0:NE, :], -jnp.inf)
            m = jnp.maximum(jnp.max(sc, axis=0, keepdims=True), -1e30)
            e = jnp.exp2(sc - m)
            pc = e / jnp.maximum(jnp.sum(e, axis=0, keepdims=True), 1e-30)
            oct_ref[r * HEAD_DIM:(r + 1) * HEAD_DIM, :] = _dot(vct, pc.astype(BF16))
            psum = psum + pc
        p0, p1, p2, p3 = (psum[j * ne:(j + 1) * ne, :] for j in range(4))
        p3s = jnp.where(lax.broadcasted_iota(I32, (ne, 1), 0) == 0, 0.0, pltpu.roll(p3, 1, axis=0))
        imp_ref[0:ne, :] = p0 + p1 + p2 + 0.5 * p3 + 0.5 * p3s
        if ne < NS:
            imp_ref[ne:NS, :] = jnp.zeros((NS - ne, T), F32)

        n_ar = lax.broadcasted_iota(I32, (NS, 1), 0)
        tqs = (i - 1) * Q_BLOCK + lax.broadcasted_iota(I32, (1, T), 1)
        cur = tqs // NSA_SLC_BLOCK
        valid = NSA_SLC_BLOCK * n_ar <= tqs
        forced = valid & ((n_ar == 0) | (n_ar > cur - NSA_LOCAL_BLOCKS))
        score = jnp.where(forced, 1e30, jnp.where(valid, imp_prev, -1.0))
        sel = jnp.zeros((NS, T), jnp.bool_)
        for _ in range(min(NSA_SLC_TOPK, NS)):
            mx = jnp.max(score, axis=0, keepdims=True)
            first = jnp.min(jnp.where(score == mx, n_ar, NS), axis=0, keepdims=True)
            pick = n_ar == first
            sel = sel | pick
            score = jnp.where(pick, -2.0, score)
        selb = jnp.where(sel, 1.0, 0.0).astype(BF16)
        if NS < LANES:
            selb = jnp.concatenate([selb, jnp.zeros((LANES - NS, T), BF16)], axis=0)
        for w in range(selt_ref.shape[0]):
            selt_ref[w] = selb[w * LANES:(w + 1) * LANES, :]

    quarter = NS // 4
    if quarter % WIN == 0:
        which = (2 * ib + 2 + quarter - 1) // quarter - 1
        for v in range(4):
            pl.when(which == v)(functools.partial(step, (v + 1) * quarter))
    else:
        step(NS)


def _cmp(nqt, kc, vct, bt, S):
    NS = S // NSA_SLC_BLOCK
    NCP = 4 * NS
    NW = max(NS // LANES, 1)
    G, R = NSA_KV_GROUPS, NSA_GROUP_SIZE
    nb = S // Q_BLOCK
    return pl.pallas_call(
        functools.partial(_cmp_body, NS=NS),
        grid=(G, nb + 1),
        in_specs=[pl.BlockSpec((R * HEAD_DIM, Q_BLOCK), lambda g, i: (g, jnp.minimum(i, nb - 1))),
                  pl.BlockSpec((None, NCP, HEAD_DIM), lambda g, i: (g, 0, 0)),
                  pl.BlockSpec((None, HEAD_DIM, NCP), lambda g, i: (g, 0, 0)),
                  pl.BlockSpec((R, 16, Q_BLOCK), lambda g, i: (g, 0, 0))],
        out_specs=[pl.BlockSpec((R * HEAD_DIM, Q_BLOCK), lambda g, i: (g, jnp.minimum(i, nb - 1))),
                   pl.BlockSpec((None, NW, LANES, Q_BLOCK), lambda g, i: (g, 0, 0, jnp.maximum(i - 1, 0)))],
        out_shape=[jax.ShapeDtypeStruct((NSA_W, S), F32),
                   jax.ShapeDtypeStruct((G, NW, LANES, S), BF16)],
        scratch_shapes=[pltpu.VMEM((R, NCP, Q_BLOCK), F32), pltpu.VMEM((NS, Q_BLOCK), F32)],
        compiler_params=_cparams(("arbitrary", "arbitrary")),
        name="cmp",
    )(nqt, kc, vct, bt)


def _flash_init(m_ref, l_ref, acc_ref):
    m_ref[...] = jnp.full(m_ref.shape, -1e30, F32)
    l_ref[...] = jnp.zeros(l_ref.shape, F32)
    acc_ref[...] = jnp.zeros(acc_ref.shape, F32)


def _flash_update(s, vt, m_ref, l_ref, acc_ref):
    m_old = m_ref[...]
    m_new = jnp.maximum(m_old, jnp.max(s, axis=0, keepdims=True))
    alpha = jnp.exp2(m_old - m_new)
    p = jnp.exp2(s - m_new)
    l_ref[...] = alpha * l_ref[...] + jnp.sum(p, axis=0, keepdims=True)
    acc_ref[...] = alpha * acc_ref[...] + _dot(vt, p.astype(BF16))
    m_ref[...] = m_new


def _slcwin_body(qt_ref, ks_ref, vst_ref, kw_ref, vwt_ref, selt_ref, tbt_ref, oct_ref, glt_ref, ot_ref,
                 ms_ref, ls_ref, as_ref, mw_ref, lw_ref, aw_ref, sbuf_ref, qaug_ref):
    i = pl.program_id(1)
    t0 = i * Q_BLOCK
    R = NSA_GROUP_SIZE
    T = Q_BLOCK
    q4t = jnp.concatenate([qt_ref[r * HEAD_DIM:(r + 1) * HEAD_DIM, :] for r in range(R)], axis=1)
    tq = lax.broadcasted_iota(I32, (1, T), 1)
    rep = lambda mk: jnp.concatenate([mk] * R, axis=1)
    tb4 = jnp.concatenate([tbt_ref[r] for r in range(R)], axis=1)

    def expand(w, base, nkeys):
        keyi = lax.broadcasted_iota(I32, (nkeys, LANES), 0)
        blk = lax.broadcasted_iota(I32, (nkeys, LANES), 1)
        e = jnp.where(blk == base + keyi // NSA_SLC_BLOCK, 1.0, 0.0).astype(BF16)
        return _dot(e, selt_ref[w])

    _flash_init(ms_ref, ls_ref, as_ref)
    _flash_init(mw_ref, lw_ref, aw_ref)
    n_win = selt_ref.shape[0]
    zpad = jnp.zeros((KAUG_W - HEAD_DIM - LANES, R * T), BF16)
    for w in range(n_win):
        qaug_ref[w] = jnp.concatenate([q4t, rep(selt_ref[w] - jnp.asarray(1.0, BF16)), zpad], axis=0)
    far_end = jnp.maximum(t0 - Q_BLOCK, 0) // KEY_TILE * KEY_TILE
    nfar = far_end // KEY_TILE
    tiles_per_window = LANES * NSA_SLC_BLOCK // KEY_TILE

    def score_far(kt, slot):
        r0 = pl.multiple_of(NSA_PADF + kt * KEY_TILE, LANES)
        w = jnp.minimum(kt // tiles_per_window, n_win - 1)
        sbuf_ref[slot] = _dot(ks_ref[pl.ds(r0, KEY_TILE), :], qaug_ref[w])

    def update_far(kt, slot):
        r0 = pl.multiple_of(NSA_PADF + kt * KEY_TILE, LANES)
        _flash_update(sbuf_ref[slot], vst_ref[:, pl.ds(r0, KEY_TILE)], ms_ref, ls_ref, as_ref)

    score_far(0, 0)

    def far2(k2, carry):
        kt = 2 * k2
        score_far(kt + 1, 1)
        update_far(kt, 0)
        score_far(kt + 2, 0)
        update_far(kt + 1, 1)
        return carry

    lax.fori_loop(0, nfar // 2, far2, 0)

    @pl.when(nfar % 2 == 1)
    def _():
        update_far(nfar - 1, 0)

    kq = lambda r0, n: _dot(ks_ref[pl.ds(r0, n), 0:HEAD_DIM], q4t)
    keyn = lax.broadcasted_iota(I32, (NEAR, 1), 0)
    near_ok = (keyn - Q_BLOCK <= tq) & (keyn + t0 - Q_BLOCK >= 0)
    rn = pl.multiple_of(t0 + NSA_PADF - Q_BLOCK, LANES)
    WF = NSA_WINDOW - Q_BLOCK
    rw = pl.multiple_of(t0 + NSA_PADF - NSA_WINDOW, LANES)
    keyw = lax.broadcasted_iota(I32, (WF, 1), 0)
    maskw = (keyw > tq) & (keyw + t0 - NSA_WINDOW >= 0)
    blocks = [jnp.maximum(2 * i - (NSA_WINDOW // NSA_SLC_BLOCK) + 2 * j, 0) for j in range(NSA_WINDOW // Q_BLOCK)]
    mexp = [expand(b // LANES, b % LANES, Q_BLOCK) for b in blocks]
    mask_sm = (jnp.concatenate(mexp[:-1], axis=0) > 0.5) & (keyw + t0 - NSA_WINDOW >= far_end)
    mask_sn = (jnp.concatenate([mexp[-1], expand(2 * i // LANES, 2 * i % LANES, Q_BLOCK)], axis=0) > 0.5) & near_ok
    s_sm = jnp.where(rep(mask_sm), kq(rw, WF), -jnp.inf)
    s_sn = jnp.where(rep(mask_sn), kq(rn, NEAR) + tb4, -jnp.inf)
    s_wf = jnp.where(rep(maskw), _dot(kw_ref[pl.ds(rw, WF), :], q4t), -jnp.inf)
    s_wn = jnp.where(rep(near_ok), _dot(kw_ref[pl.ds(rn, NEAR), :], q4t) + tb4, -jnp.inf)
    _flash_update(s_sm, vst_ref[:, pl.ds(rw, WF)], ms_ref, ls_ref, as_ref)
    _flash_update(s_sn, vst_ref[:, pl.ds(rn, NEAR)], ms_ref, ls_ref, as_ref)
    _flash_update(s_wf, vwt_ref[:, pl.ds(rw, WF)], mw_ref, lw_ref, aw_ref)
    _flash_update(s_wn, vwt_ref[:, pl.ds(rn, NEAR)], mw_ref, lw_ref, aw_ref)
    o_s = as_ref[...] / jnp.maximum(ls_ref[...], 1e-30)
    o_w = aw_ref[...] / jnp.maximum(lw_ref[...], 1e-30)

    gates = jax.nn.sigmoid(glt_ref[...])
    for r in range(R):
        hs = slice(r * HEAD_DIM, (r + 1) * HEAD_DIM)
        qs = slice(r * T, (r + 1) * T)
        o = (gates[3 * r:3 * r + 1, :] * oct_ref[hs, :]
             + gates[3 * r + 1:3 * r + 2, :] * o_s[:, qs]
             + gates[3 * r + 2:3 * r + 3, :] * o_w[:, qs])
        ot_ref[hs, :] = o.astype(BF16)


def _slcwin(nqt, ks, vst, kw, vwt, selt, tbt, oct, glt, S):
    G, R = NSA_KV_GROUPS, NSA_GROUP_SIZE
    spad = ks.shape[1]
    NW = selt.shape[1]
    once = pl.Buffered(1)
    kaspec = pl.BlockSpec((None, spad, KAUG_W), lambda g, i: (g, 0, 0), pipeline_mode=once)
    kspec = pl.BlockSpec((None, spad, HEAD_DIM), lambda g, i: (g, 0, 0), pipeline_mode=once)
    vspec = pl.BlockSpec((None, HEAD_DIM, spad), lambda g, i: (g, 0, 0), pipeline_mode=once)
    qspec = pl.BlockSpec((R * HEAD_DIM, Q_BLOCK), lambda g, i: (g, i))
    st = lambda n: pltpu.VMEM((n, R * Q_BLOCK), F32)
    return pl.pallas_call(
        _slcwin_body,
        grid=(G, S // Q_BLOCK),
        in_specs=[qspec, kaspec, vspec, kspec, vspec,
                  pl.BlockSpec((None, NW, LANES, Q_BLOCK), lambda g, i: (g, 0, 0, i)),
                  pl.BlockSpec((R, NEAR, Q_BLOCK), lambda g, i: (g, 0, 0)),
                  qspec,
                  pl.BlockSpec((None, 16, Q_BLOCK), lambda g, i: (g, 0, i))],
        out_specs=qspec,
        out_shape=jax.ShapeDtypeStruct((NSA_W, S), BF16),
        scratch_shapes=[st(1), st(1), st(HEAD_DIM), st(1), st(1), st(HEAD_DIM),
                        pltpu.VMEM((2, KEY_TILE, R * Q_BLOCK), F32),
                        pltpu.VMEM((NW, KAUG_W, R * Q_BLOCK), BF16)],
        compiler_params=_cparams(("arbitrary", "arbitrary")),
        name="slcwin",
    )(nqt, ks, vst, kw, vwt, selt, tbt, oct, glt)


INT_MIN = -2 ** 31
INT_MAX = 2 ** 31 - 1
MAX_PEEL_ROUNDS = 64
UNTESTED_BITS = 16
NEG_INF_KEY = int(np.array(-np.inf, np.float32).view(np.int32)) ^ 0x7FFFFFFF


def _dsa_body(dqt_ref, dk_ref, dvt_ref, iqt_ref, ikt_ref, iwt_ref, tbt_ref, ltri_ref, ot_ref,
              keys_ref, m_ref, l_ref, acc_ref, sbuf_ref, *, S, KEEP):
    i = pl.program_id(0)
    t0 = i * Q_BLOCK
    T = Q_BLOCK
    H = DSA_HEADS
    ntile = (t0 + Q_BLOCK + KEY_TILE - 1) // KEY_TILE
    tq = t0 + lax.broadcasted_iota(I32, (1, T), 1)
    key512 = lax.broadcasted_iota(I32, (KEY_TILE, 1), 0)
    ktile = lambda kt: keys_ref[pl.ds(pl.multiple_of(Q_BLOCK + kt * KEY_TILE, LANES), KEY_TILE), :]

    keys_ref[0:Q_BLOCK, :] = jnp.full((Q_BLOCK, T), NEG_INF_KEY, I32)
    iqt = iqt_ref[...]
    iwt = iwt_ref[...] * (IDX_HEADS ** -0.5)
    qis = [jnp.concatenate([iqt[(2 * j) * IDX_DIM:(2 * j + 1) * IDX_DIM, :],
                            iqt[(2 * j + 1) * IDX_DIM:(2 * j + 2) * IDX_DIM, :]], axis=1)
           for j in range(IDX_HEADS // 2)]
    wrow = [iwt[h:h + 1, :] for h in range(IDX_HEADS)]

    def idx_dots(kt, slot):
        ik = ikt_ref[:, pl.ds(pl.multiple_of(kt * KEY_TILE, KEY_TILE), KEY_TILE)]
        for j in range(IDX_HEADS // 2):
            sbuf_ref[slot, j] = _dot_tn(ik, qis[j])

    def idx_keys(kt, slot):
        c0 = pl.multiple_of(kt * KEY_TILE, KEY_TILE)
        sc = jnp.zeros((KEY_TILE, T), F32)
        for j in range(IDX_HEADS // 2):
            d = jnp.maximum(sbuf_ref[slot, j], 0.0)
            sc = sc + wrow[2 * j] * d[:, 0:T]
            sc = sc + wrow[2 * j + 1] * d[:, T:2 * T]
        sc = jnp.where(sc == 0.0, 0.0, sc)
        sc = jnp.where(c0 + key512 <= tq, sc, -jnp.inf)
        bits = pltpu.bitcast(sc, I32)
        keys_ref[pl.ds(pl.multiple_of(Q_BLOCK + c0, LANES), KEY_TILE), :] = jnp.where(
            bits < 0, bits ^ 0x7FFFFFFF, bits)

    idx_dots(0, 0)

    def score2(k2, carry):
        kt = 2 * k2
        idx_dots(kt + 1, 1)
        idx_keys(kt, 0)
        idx_dots(kt + 2, 0)
        idx_keys(kt + 1, 1)
        return carry

    lax.fori_loop(0, (ntile + 1) // 2, score2, 0)

    def count(pred):
        def body(kt, acc):
            hit = jnp.where(pred(ktile(kt), kt * KEY_TILE + key512), 1, 0)
            return acc + jnp.sum(hit.reshape(KEY_TILE // SUBLANES, SUBLANES, T), axis=0)
        acc = lax.fori_loop(0, ntile, body, jnp.zeros((SUBLANES, T), I32))
        return jnp.sum(acc, axis=0, keepdims=True)

    def min_at_least(lowest):
        def body(kt, acc):
            kk = ktile(kt)
            v = jnp.where(kk >= lowest, kk, INT_MAX)
            return jnp.minimum(acc, jnp.min(v.reshape(KEY_TILE // SUBLANES, SUBLANES, T), axis=0))
        acc = lax.fori_loop(0, ntile, body, jnp.full((SUBLANES, T), INT_MAX, I32))
        return jnp.min(acc, axis=0, keepdims=True)

    def any_open(st, over):
        _, excess, fin, _ = st
        return jnp.max(jnp.where((fin == 0) & (excess > over), 1, 0)) > 0

    def bisect(b, st):
        thr, excess, fin, need = st
        cand = thr + lax.shift_left(jnp.int32(1), 31 - b)
        n = count(lambda kk, ab: kk >= cand)
        ok = (n >= KEEP) & (fin == 0)
        return jnp.where(ok, cand, thr), jnp.where(ok, n - KEEP, excess), fin, need

    def peel(st):
        thr, excess, fin, need = st
        low = min_at_least(thr)
        n_low = count(lambda kk, ab: kk == low)
        active = (fin == 0) & (excess > 0)
        tied = active & (n_low > excess)
        drop = active & (n_low <= excess)
        return (jnp.where(tied, low, jnp.where(drop, low + 1, thr)), jnp.where(drop, excess - n_low, excess),
                jnp.where(tied, 1, fin), jnp.where(tied, n_low - excess, need))

    zero = jnp.zeros((1, T), I32)
    st = (jnp.full((1, T), INT_MIN, I32), KEY_TILE * ntile - KEEP + zero, zero, zero)
    st = peel(bisect(0, st))
    st = lax.fori_loop(1, UNTESTED_BITS, bisect, st)
    _, st = lax.while_loop(lambda c: (c[0] < 32) & any_open(c[1], 1),
                           lambda c: (c[0] + 1, bisect(c[0], c[1])), (jnp.int32(UNTESTED_BITS), st))
    _, st = lax.while_loop(lambda c: (c[0] < MAX_PEEL_ROUNDS) & any_open(c[1], 0),
                           lambda c: (c[0] + 1, peel(c[1])), (jnp.int32(0), st))
    thr, _, fin, need = st
    tie = (fin == 1) & (thr > NEG_INF_KEY)

    def tie_cut():
        needf = need.astype(F32)

        def body(kt, c):
            before, jmax = c
            is_tie = ktile(kt) == thr
            tied = jnp.where(is_tie, 1.0, 0.0)
            rank = before + _dot(ltri_ref[...], tied.astype(BF16))
            admit = is_tie & (rank < needf)
            jm = jnp.max(jnp.where(admit, kt * KEY_TILE + key512, -1), axis=0, keepdims=True)
            return before + jnp.sum(tied, axis=0, keepdims=True), jnp.maximum(jmax, jm)

        _, jmax = lax.fori_loop(0, ntile, body, (jnp.zeros((1, T), F32), jnp.full((1, T), -1, I32)))
        return jnp.where(tie, jmax + 1, S)

    jcut = lax.cond(jnp.max(jnp.where(tie, 1, 0)) > 0, tie_cut, lambda: jnp.full((1, T), S, I32))

    def chosen(kk, ab):
        return (kk > thr) | ((kk == thr) & (ab < jcut))

    _flash_init(m_ref, l_ref, acc_ref)
    lo = lax.broadcasted_iota(I32, (LANES, 1), 0) < HEAD_DIM
    qpairs = []
    for pr in range(H // 2):
        qp = dqt_ref[pr * LANES:(pr + 1) * LANES, :]
        zero = jnp.zeros_like(qp)
        qpairs.append(jnp.concatenate([jnp.where(lo, qp, zero), jnp.where(lo, zero, qp)], axis=1))

    def scores(r0, nkeys, mask, bias):
        mask2 = jnp.concatenate([mask, mask], axis=1)
        out = []
        for pr in range(H // 2):
            s = _dot(dk_ref[pl.ds(r0, nkeys), pr * LANES:(pr + 1) * LANES], qpairs[pr])
            if bias is not None:
                s = s + jnp.concatenate([bias[2 * pr], bias[2 * pr + 1]], axis=1)
            out.append(jnp.where(mask2, s, -jnp.inf))
        return out

    def update(pr, s, r0, nkeys):
        m_old = m_ref[pr]
        m_new = jnp.maximum(m_old, jnp.max(s, axis=0, keepdims=True))
        alpha = jnp.exp2(m_old - m_new)
        p = jnp.exp2(s - m_new)
        l_ref[pr] = alpha * l_ref[pr] + jnp.sum(p, axis=0, keepdims=True)
        m_ref[pr] = m_new
        vt = dvt_ref[pr * LANES:(pr + 1) * LANES, pl.ds(r0, nkeys)]
        acc_ref[pr] = alpha * acc_ref[pr] + _dot(vt, p.astype(BF16))

    far_end = t0 - Q_BLOCK
    nfar = jnp.maximum(far_end + KEY_TILE - 1, 0) // KEY_TILE
    far_row = lambda kt: pl.multiple_of(DSA_PADF + kt * KEY_TILE, LANES)

    def score_far(kt, slot):
        ab = kt * KEY_TILE + key512
        mask = chosen(ktile(jnp.minimum(kt, ntile - 1)), ab) & (ab < far_end)
        for pr, s in enumerate(scores(far_row(kt), KEY_TILE, mask, None)):
            sbuf_ref[slot, pr] = s

    def update_far(kt, slot):
        for pr in range(H // 2):
            update(pr, sbuf_ref[slot, pr], far_row(kt), KEY_TILE)

    score_far(0, 0)

    def far2(k2, carry):
        kt = 2 * k2
        score_far(kt + 1, 1)
        update_far(kt, 0)
        score_far(kt + 2, 0)
        update_far(kt + 1, 1)
        return carry

    lax.fori_loop(0, (nfar + 1) // 2, far2, 0)

    abn = lax.broadcasted_iota(I32, (NEAR, 1), 0) + t0 - Q_BLOCK
    kkn = keys_ref[pl.ds(pl.multiple_of(t0, LANES), NEAR), :]
    rn = pl.multiple_of(t0 + DSA_PADF - Q_BLOCK, LANES)
    for pr, s in enumerate(scores(rn, NEAR, chosen(kkn, abn) & (abn <= tq) & (abn >= 0), tbt_ref)):
        update(pr, s, rn, NEAR)

    for pr in range(H // 2):
        o = acc_ref[pr] / jnp.maximum(l_ref[pr], 1e-30)
        ot_ref[pr * LANES:(pr + 1) * LANES, :] = jnp.where(lo, o[:, 0:T], o[:, T:2 * T]).astype(BF16)


def _dsa(dqt, dkp, dvtp, iqt, ikt, iwt, tbt, ltri, S):
    KEEP = min(DSA_TOPK_MAX, S // 4)
    H = DSA_HEADS
    T = Q_BLOCK
    once = pl.Buffered(1)
    full = lambda a: pl.BlockSpec(a.shape, lambda i: (0,) * a.ndim, pipeline_mode=once)
    return pl.pallas_call(
        functools.partial(_dsa_body, S=S, KEEP=KEEP),
        grid=(S // T,),
        in_specs=[pl.BlockSpec((DSA_W, T), lambda i: (0, i)),
                  full(dkp), full(dvtp),
                  pl.BlockSpec((IDX_HEADS * IDX_DIM, T), lambda i: (0, i)),
                  full(ikt),
                  pl.BlockSpec((IDX_HEADS, T), lambda i: (0, i)),
                  full(tbt), full(ltri)],
        out_specs=pl.BlockSpec((DSA_W, T), lambda i: (0, i)),
        out_shape=jax.ShapeDtypeStruct((DSA_W, S), BF16),
        scratch_shapes=[pltpu.VMEM((Q_BLOCK + S + KEY_TILE, T), I32),
                        pltpu.VMEM((H // 2, 1, 2 * T), F32),
                        pltpu.VMEM((H // 2, 1, 2 * T), F32),
                        pltpu.VMEM((H // 2, LANES, 2 * T), F32),
                        pltpu.VMEM((2, H // 2, KEY_TILE, 2 * T), F32)],
        compiler_params=_cparams(("arbitrary",)),
        name="dsa",
    )(dqt, dkp, dvtp, iqt, ikt, iwt, tbt, ltri)


def _split_bf16(a):
    hi = a.astype(BF16)
    return hi, (a - hi.astype(F32)).astype(BF16)


def _merge_body(x_ref, oa_ref, ob_ref, gab_ref, wa_ref, wb_ref, wo_ref, gffn_ref, wr_ref, br_ref,
                x1_ref, h2_ref, ridx_ref, rw_ref):
    a = _dot_tn(oa_ref[...], wa_ref[...])
    b = _dot_tn(ob_ref[...], wb_ref[...])
    gab = gab_ref[...]
    merged = gab[:, :D_MODEL] * a + gab[:, D_MODEL:] * b
    x1 = x_ref[...] + _dot(merged.astype(BF16), wo_ref[...])
    x1_ref[...] = x1
    h2 = x1 * lax.rsqrt(jnp.mean(x1 * x1, axis=-1, keepdims=True) + EPS) * gffn_ref[...]
    h2_ref[...] = h2
    hh, hl = _split_bf16(h2)
    wh, wl = _split_bf16(wr_ref[...])
    logits = _dot(hh, wh) + _dot(hh, wl) + _dot(hl, wh) + br_ref[...]
    lane = lax.broadcasted_iota(I32, (1, LANES), 1)
    idx_out = jnp.zeros(logits.shape, I32)
    val_out = jnp.full(logits.shape, -jnp.inf, F32)
    for k in range(TOP_K):
        mx = jnp.max(logits, axis=-1, keepdims=True)
        first = jnp.min(jnp.where(logits == mx, lane, LANES), axis=-1, keepdims=True)
        idx_out = jnp.where(lane == k, first, idx_out)
        val_out = jnp.where(lane == k, mx, val_out)
        logits = jnp.where(lane == first, -jnp.inf, logits)
    e = jnp.exp(val_out - jnp.max(val_out, axis=-1, keepdims=True))
    ridx_ref[...] = idx_out
    rw_ref[...] = e / jnp.sum(e, axis=-1, keepdims=True)


def _merge(x2, oa, ob, gab, wa, wb, wo, gffn, wr, br, tm=512):
    S = x2.shape[0]
    row = lambda n: pl.BlockSpec((tm, n), lambda i: (i, 0))
    full = lambda a: pl.BlockSpec(a.shape, lambda i: (0,) * a.ndim)
    return pl.pallas_call(
        _merge_body,
        grid=(S // tm,),
        in_specs=[row(D_MODEL), pl.BlockSpec((NSA_W, tm), lambda i: (0, i)), pl.BlockSpec((DSA_W, tm), lambda i: (0, i)),
                  row(2 * D_MODEL),
                  full(wa), full(wb), full(wo), full(gffn), full(wr), full(br)],
        out_specs=[row(D_MODEL), row(D_MODEL), row(LANES), row(LANES)],
        out_shape=[jax.ShapeDtypeStruct((S, D_MODEL), F32), jax.ShapeDtypeStruct((S, D_MODEL), F32),
                   jax.ShapeDtypeStruct((S, LANES), I32), jax.ShapeDtypeStruct((S, LANES), F32)],
        compiler_params=_cparams(("arbitrary",)),
        name="merge",
    )(x2, oa, ob, gab, wa, wb, wo, gffn, wr, br)


def _rank_body(ridx_ref, rank_ref, cnt_ref, carry_ref):
    @pl.when(pl.program_id(0) == 0)
    def _():
        carry_ref[...] = jnp.zeros(carry_ref.shape, F32)

    ridx = ridx_ref[...]
    tm = ridx.shape[0]
    lane = lax.broadcasted_iota(I32, (1, LANES), 1)
    hits = [lane == ridx[:, k:k + 1] for k in range(TOP_K)]
    member = jnp.zeros((tm, LANES), F32)
    for hk in hits:
        member = member + jnp.where(hk, 1.0, 0.0)
    ri = lax.broadcasted_iota(I32, (tm, tm), 0)
    ci = lax.broadcasted_iota(I32, (tm, tm), 1)
    lower = jnp.where(ci < ri, 1.0, 0.0).astype(BF16)
    before = _dot(lower, member.astype(BF16)) + carry_ref[0:1, :]
    out = jnp.zeros((tm, LANES), I32)
    for k, hk in enumerate(hits):
        rk = jnp.sum(jnp.where(hk, before, 0.0), axis=-1, keepdims=True)
        out = jnp.where(lane == k, rk.astype(I32), out)
    rank_ref[...] = out
    total = carry_ref[0:1, :] + jnp.sum(member, axis=0, keepdims=True)
    carry_ref[...] = jnp.broadcast_to(total, carry_ref.shape)
    cnt_ref[...] = jnp.broadcast_to(total, cnt_ref.shape).astype(I32)


def _rank(ridx, tm=512):
    S = ridx.shape[0]
    return pl.pallas_call(
        _rank_body,
        grid=(S // tm,),
        in_specs=[pl.BlockSpec((tm, LANES), lambda i: (i, 0))],
        out_specs=[pl.BlockSpec((tm, LANES), lambda i: (i, 0)), pl.BlockSpec((8, LANES), lambda i: (0, 0))],
        out_shape=[jax.ShapeDtypeStruct((S, LANES), I32), jax.ShapeDtypeStruct((8, LANES), I32)],
        scratch_shapes=[pltpu.VMEM((8, LANES), F32)],
        compiler_params=_cparams(("arbitrary",)),
        name="rank",
    )(ridx)


def _dispatch_body(dest_ref, zrow_ref, nu_ref, h_ref, xe_ref, zbuf_ref, sem, zsem):
    i = pl.program_id(0)
    tm = h_ref.shape[0]

    @pl.when(i == 0)
    def _():
        zbuf_ref[...] = jnp.zeros(zbuf_ref.shape, F32)
        fill = lambda row: pltpu.make_async_copy(zbuf_ref, xe_ref.at[pl.ds(row, MOE_ROWS)], zsem)
        n_chunks = xe_ref.shape[0] // MOE_ROWS
        for wait in (False, True):
            for e in range(N_EXPERTS):
                @pl.when(zrow_ref[e] >= 0)
                def _():
                    cp = fill(pl.multiple_of(jnp.maximum(zrow_ref[e], 0), MOE_ROWS))
                    cp.wait() if wait else cp.start()

            def tail(j, c):
                cp = fill(pl.multiple_of(j * MOE_ROWS, MOE_ROWS))
                cp.wait() if wait else cp.start()
                return c
            lax.fori_loop(nu_ref[0], n_chunks, tail, 0)

    def start(r, c):
        for k in range(TOP_K):
            d = dest_ref[(i * tm + r) * TOP_K + k]
            pltpu.make_async_copy(h_ref.at[pl.ds(r, 1)], xe_ref.at[pl.ds(d, 1)],
                                  sem).start(priority=k % 2)
        return c

    lax.fori_loop(0, tm, start, 0, unroll=ROW_DMA_UNROLL)
    for _ in range(TOP_K):
        pltpu.make_async_copy(h_ref, xe_ref.at[pl.ds(0, tm)], sem).wait()


def _dispatch(dest, zrow, n_used, h2, n_rows, tm=256):
    S = h2.shape[0]
    return pl.pallas_call(
        _dispatch_body,
        grid_spec=pltpu.PrefetchScalarGridSpec(
            num_scalar_prefetch=3,
            grid=(S // tm,),
            in_specs=[pl.BlockSpec((tm, D_MODEL), lambda i, d, z, n: (i, 0))],
            out_specs=pl.BlockSpec(memory_space=pl.ANY),
            scratch_shapes=[pltpu.VMEM((MOE_ROWS, D_MODEL), F32), pltpu.SemaphoreType.DMA(()),
                            pltpu.SemaphoreType.DMA(())]),
        out_shape=jax.ShapeDtypeStruct((n_rows, D_MODEL), F32),
        compiler_params=_cparams(("arbitrary",)),
        name="dispatch",
    )(dest, zrow, n_used, h2)


def _experts_body(ce_ref, nu_ref, x_ref, w1_ref, b1_ref, w2_ref, b2_ref, y_ref, w1b_ref, w2b_ref):
    c = pl.program_id(0)

    @pl.when((c == 0) | (ce_ref[c] != ce_ref[jnp.maximum(c - 1, 0)]))
    def _():
        w1b_ref[...] = w1_ref[...].astype(BF16)
        w2b_ref[...] = w2_ref[...].astype(BF16)

    @pl.when(c < nu_ref[0])
    def _():
        gu = _dot(x_ref[...].astype(BF16), w1b_ref[...]) + b1_ref[...]
        gate = jnp.minimum(gu[:, :D_FF], SWIGLU_LIMIT)
        lin = jnp.clip(gu[:, D_FF:], -SWIGLU_LIMIT, SWIGLU_LIMIT)
        act = (lin + 1.0) * gate * jax.nn.sigmoid(SWIGLU_ALPHA * gate)
        y_ref[...] = _dot(act.astype(BF16), w2b_ref[...]) + b2_ref[...]

    @pl.when(c >= nu_ref[0])
    def _():
        y_ref[...] = jnp.zeros(y_ref.shape, F32)


def _experts(chunk_e, n_used, xe, w1, b1, w2, b2):
    P = xe.shape[0]
    nch = P // MOE_ROWS
    return pl.pallas_call(
        _experts_body,
        grid_spec=pltpu.PrefetchScalarGridSpec(
            num_scalar_prefetch=2,
            grid=(nch,),
            in_specs=[pl.BlockSpec((MOE_ROWS, D_MODEL), lambda c, ce, nu: (c, 0)),
                      pl.BlockSpec((None, D_MODEL, 2 * D_FF), lambda c, ce, nu: (ce[c], 0, 0)),
                      pl.BlockSpec((None, 1, 2 * D_FF), lambda c, ce, nu: (ce[c], 0, 0)),
                      pl.BlockSpec((None, D_FF, D_MODEL), lambda c, ce, nu: (ce[c], 0, 0)),
                      pl.BlockSpec((None, 1, D_MODEL), lambda c, ce, nu: (ce[c], 0, 0))],
            out_specs=pl.BlockSpec((MOE_ROWS, D_MODEL), lambda c, ce, nu: (c, 0)),
            scratch_shapes=[pltpu.VMEM((D_MODEL, 2 * D_FF), BF16), pltpu.VMEM((D_FF, D_MODEL), BF16)]),
        out_shape=jax.ShapeDtypeStruct((P, D_MODEL), F32),
        compiler_params=_cparams(("arbitrary",)),
        name="experts",
    )(chunk_e, n_used, xe, w1, b1, w2, b2)


def _final_body(dest_ref, x1_ref, rw_ref, p_ref, gple_ref, wg_ref, wp_ref, y_ref, o_ref, rows_ref, sems):
    i = pl.program_id(0)
    tm = x1_ref.shape[0]

    def gather(blk, slot):
        def start(r, c):
            for k in range(TOP_K):
                d = dest_ref[(blk * tm + r) * TOP_K + k]
                pltpu.make_async_copy(y_ref.at[pl.ds(d, 1)], rows_ref.at[slot, k, pl.ds(r, 1)],
                                      sems.at[slot]).start(priority=k % 2)
            return c
        lax.fori_loop(0, tm, start, 0, unroll=ROW_DMA_UNROLL)

    def drain(which):
        for k in range(TOP_K):
            pltpu.make_async_copy(y_ref.at[pl.ds(0, tm)], rows_ref.at[which, k], sems.at[which]).wait()

    @pl.when(i == 0)
    def _():
        gather(0, 0)

    slot = i % 2
    last = pl.num_programs(0) - 1
    drain(slot)
    nxt = jnp.minimum(i + 1, last)
    for r in range(tm):
        for k in range(TOP_K):
            d = dest_ref[(nxt * tm + r) * TOP_K + k]
            pltpu.make_async_copy(y_ref.at[pl.ds(d, 1)], rows_ref.at[1 - slot, k, pl.ds(r, 1)],
                                  sems.at[1 - slot]).start(priority=k % 2)
    rw = rw_ref[...]
    x2 = x1_ref[...]
    for k in range(TOP_K):
        x2 = x2 + rw[:, k:k + 1] * rows_ref[slot, k]
    hn = x2 * lax.rsqrt(jnp.mean(x2 * x2, axis=-1, keepdims=True) + EPS) * gple_ref[...]
    gate = jax.nn.sigmoid(_dot(hn.astype(BF16), wg_ref[...]))
    o_ref[...] = x2 + gate * _dot(p_ref[...].astype(BF16), wp_ref[...])

    @pl.when(i == last)
    def _():
        drain(1 - slot)


def _final(dest, x1, rw, p2, gple, wg, wp, yexp, tm=256):
    S = x1.shape[0]
    row = lambda n: pl.BlockSpec((tm, n), lambda i, d: (i, 0))
    full = lambda a: pl.BlockSpec(a.shape, lambda i, d: (0,) * a.ndim)
    return pl.pallas_call(
        _final_body,
        grid_spec=pltpu.PrefetchScalarGridSpec(
            num_scalar_prefetch=1,
            grid=(S // tm,),
            in_specs=[row(D_MODEL), row(LANES), row(PLE_DIM), full(gple), full(wg), full(wp),
                      pl.BlockSpec(memory_space=pl.ANY)],
            out_specs=row(D_MODEL),
            scratch_shapes=[pltpu.VMEM((2, TOP_K, tm, D_MODEL), F32), pltpu.SemaphoreType.DMA((2,))]),
        out_shape=jax.ShapeDtypeStruct((S, D_MODEL), F32),
        compiler_params=_cparams(("arbitrary",)),
        name="final",
    )(dest, x1, rw, p2, gple, wg, wp, yexp)


def _distance_bias(rel_cols):
    return (rel_cols[_rel_bucket_table(NEAR)] - rel_cols[REL_BUCKETS - 1][None, :]).T * LOG2E


def _near_bias_t(rel_cols):
    h = rel_cols.shape[1]
    g = jnp.pad(_distance_bias(rel_cols), ((0, 0), (Q_BLOCK, Q_BLOCK)))
    period = 2 * NEAR
    skew = jnp.tile(g, (1, NEAR))[:, :NEAR * (period - 1)].reshape(h, NEAR, period - 1)
    return skew[:, :, NEAR:NEAR + Q_BLOCK]


def _cmp_bias_t(rel_cols):
    g = jnp.pad(_distance_bias(rel_cols), ((0, 0), (NEAR, Q_BLOCK)))
    rows = []
    for j in range(4):
        for k in range(4):
            c0 = -NSA_CMP_STRIDE * j - (NSA_CMP_LEN - 1) + NSA_SLC_BLOCK * (k - 1)
            rows.append(g[:, c0 + NEAR:c0 + NEAR + Q_BLOCK])
    return jnp.stack(rows, axis=1)


def kernel(x, p, w_in, g_mix, g_q_nsa, g_k_nsa, g_q_dsa, g_k_dsa, pe_ck, w_ck1, w_ck2, pe_cv, w_cv1, w_cv2,
           w_branch_a, w_branch_b, w_out, rel_bias, g_ffn, w_router, b_router, w_e1, b_e1, w_e2, b_e2,
           g_ple, w_ple, w_ple_gate):
    B, S, D = x.shape
    assert B == 1 and D == D_MODEL and S % KEY_TILE == 0 and w_in.shape[0] == 1
    G, R = NSA_KV_GROUPS, NSA_GROUP_SIZE
    x2 = x.reshape(S, D)

    w0 = w_in[0]
    w_nat = jnp.concatenate([w0[:, 512:768], w0[:, 768:896], w0[:, 1024:1152], w0[:, 1816:2328], w0[:, 3136:5184],
                             w0[:, 1280:1304], w0[:, 3096:3136], jnp.zeros((D, LANES - 64), F32)], axis=1).astype(BF16)
    w_tr = jnp.concatenate([w0[:, 0:512], w0[:, 1304:1816], w0[:, 2328:2840], w0[:, 2840:3096],
                            w0[:, 896:1024], w0[:, 1152:1280]], axis=1).T.astype(BF16)
    two = lambda g: jnp.tile(g.reshape(1, HEAD_DIM), (1, 2))
    qscale = HEAD_DIM ** -0.5 * LOG2E
    gains = jnp.concatenate([two(g_k_nsa[0]), two(g_k_dsa[0])], axis=0)
    gcol = jnp.stack([g_q_nsa[0] * qscale, g_q_dsa[0] * qscale]).reshape(2, HEAD_DIM, 1)
    (cmpraw, kslc, kwin, dk, gab, small, nqt, dqt, dvt, iqt, vst2, vwt2) = _inproj(
        x2, g_mix[0].reshape(1, D), w_nat, w_tr, gains, gcol)

    NS = S // NSA_SLC_BLOCK
    xc = cmpraw.reshape(2 * G, NS, 4 * NSA_CMP_STRIDE * HEAD_DIM)
    half = NSA_CMP_STRIDE * HEAD_DIM
    w1 = jnp.stack([w_ck1[0], w_cv1[0]]).reshape(2, 2 * half, HEAD_DIM)
    pe = jnp.stack([pe_ck[0], pe_cv[0]]).reshape(2, 2 * half, 1)
    w2 = jnp.stack([w_ck2[0], w_cv2[0]]).astype(BF16)
    kvc = _compress(xc, w1[:, :half].astype(BF16), w1[:, half:].astype(BF16), w1, pe, w2,
                    g_k_nsa[0].reshape(1, HEAD_DIM))

    rel_nsa = rel_bias[:, :NSA_HEADS]
    rel_dsa = rel_bias[:, NSA_HEADS:]
    oct, selt = _cmp(nqt, kvc[:G], kvc[G:].transpose(0, 2, 1), _cmp_bias_t(rel_nsa), S)
    kgroups = lambda a: jnp.pad(a, ((0, 0), (NSA_PADF, TAIL_PAD), (0, 0)))
    vgroups = lambda a: jnp.pad(a.reshape(G, HEAD_DIM, S), ((0, 0), (0, 0), (NSA_PADF, TAIL_PAD)))
    glt = small[:, SMALL_GATE:SMALL_GATE + 3 * NSA_HEADS].reshape(S, G, 3 * R).transpose(1, 2, 0)
    glt = jnp.pad(glt, ((0, 0), (0, 16 - 3 * R), (0, 0)))
    spad = NSA_PADF + S + TAIL_PAD
    key = lax.broadcasted_iota(I32, (spad, LANES), 0) - NSA_PADF
    lane = lax.broadcasted_iota(I32, (spad, LANES), 1)
    onehot = jnp.where((key // NSA_SLC_BLOCK) % LANES == lane, MASK_BIG, 0.0).astype(BF16)
    kaug = jnp.concatenate([kgroups(kslc), jnp.broadcast_to(onehot, (G, spad, LANES)),
                            jnp.zeros((G, spad, KAUG_W - HEAD_DIM - LANES), BF16)], axis=-1)
    o_at = _slcwin(nqt, kaug, vgroups(vst2), kgroups(kwin), vgroups(vwt2), selt,
                   _near_bias_t(rel_nsa), oct, glt, S)

    dkp = jnp.pad(dk, ((DSA_PADF, TAIL_PAD), (0, 0)))
    dvtp = jnp.pad(dvt, ((0, 0), (DSA_PADF, TAIL_PAD)))
    ikt = jnp.pad(small[:, SMALL_IK:SMALL_IK + IDX_DIM].T.astype(BF16), ((0, 0), (0, 2 * KEY_TILE)))
    iwt = small[:, SMALL_IW:SMALL_IW + IDX_HEADS].T
    ltri = jnp.asarray(np.tril(np.ones((KEY_TILE, KEY_TILE), np.float32), -1), BF16)
    o_bt = _dsa(dqt, dkp, dvtp, iqt, ikt, iwt, _near_bias_t(rel_dsa), ltri, S)

    wr = jnp.pad(w_router[0], ((0, 0), (0, LANES - N_EXPERTS)))
    br = jnp.pad(b_router[0].reshape(1, N_EXPERTS), ((0, 0), (0, LANES - N_EXPERTS)), constant_values=-1e30)
    x1, h2, ridx, rw = _merge(x2, o_at, o_bt, gab, w_branch_a[0].astype(BF16), w_branch_b[0].astype(BF16),
                              w_out[0].astype(BF16), g_ffn[0].reshape(1, D), wr, br)

    rank, cnt = _rank(ridx)
    counts = cnt[0, :N_EXPERTS]
    padded = (counts + MOE_ROWS - 1) // MOE_ROWS * MOE_ROWS
    pad_end = jnp.cumsum(padded)
    pad_start = pad_end - padded
    eidx = ridx[:, :TOP_K]
    dest = (jnp.sum(jnp.where(eidx[..., None] == jnp.arange(N_EXPERTS), pad_start, 0), axis=-1)
            + rank[:, :TOP_K]).reshape(-1).astype(I32)
    n_chunks = -(-(S * TOP_K) // MOE_ROWS) + N_EXPERTS
    chunk_start = jnp.arange(n_chunks, dtype=I32) * MOE_ROWS
    chunk_e = jnp.minimum(jnp.sum(chunk_start[:, None] >= pad_end[None, :], axis=-1), N_EXPERTS - 1).astype(I32)
    n_used = (pad_end[-1] // MOE_ROWS).astype(I32).reshape(1)
    zrow = jnp.where(padded > 0, pad_end - MOE_ROWS, -1).astype(I32)
    xe = _dispatch(dest, zrow, n_used, h2, n_chunks * MOE_ROWS)
    yexp = _experts(chunk_e, n_used, xe, w_e1[0], b_e1[0].reshape(N_EXPERTS, 1, 2 * D_FF),
                    w_e2[0], b_e2[0].reshape(N_EXPERTS, 1, D))

    out = _final(dest, x1, rw, p[0].reshape(S, PLE_DIM), g_ple[0].reshape(1, D),
                 w_ple_gate[0].astype(BF16), w_ple[0].astype(BF16), yexp)
    return out.reshape(B, S, D)
```

```python
import functools
import math

import numpy as np
import jax
import jax.numpy as jnp
from jax import lax
from jax.experimental import pallas as pl
from jax.experimental.pallas import tpu as pltpu

F32 = jnp.float32
BF16 = jnp.bfloat16
I32 = jnp.int32

D_MODEL = 1024
PLE_DIM = 256
HEAD_DIM = 64
NSA_HEADS = 8
NSA_KV_GROUPS = 2
NSA_GROUP_SIZE = NSA_HEADS // NSA_KV_GROUPS
NSA_CMP_LEN = 32
NSA_CMP_STRIDE = 16
NSA_SLC_BLOCK = 64
NSA_SLC_TOPK = 16
NSA_LOCAL_BLOCKS = 2
NSA_WINDOW = 512
DSA_HEADS = 8
IDX_HEADS = 8
IDX_DIM = 32
DSA_TOPK_MAX = 256
Q_BLOCK = 128
REL_BUCKETS = 32
REL_MAX_EXACT = 16
REL_MAX_DIST = 128
N_EXPERTS = 32
TOP_K = 4
D_FF = 1024
SWIGLU_LIMIT = 7.0
SWIGLU_ALPHA = 1.702
MOE_ROWS = 512
EPS = 1e-6
LOG2E = math.log2(math.e)

NSA_W = NSA_HEADS * HEAD_DIM
DSA_W = DSA_HEADS * HEAD_DIM

LANES = 128
SUBLANES = 8
KEY_TILE = 512
NEAR = 2 * Q_BLOCK
NSA_PADF = NSA_WINDOW
DSA_PADF = Q_BLOCK
KAUG_W = 2 * LANES
MASK_BIG = 2.0 ** 100
TAIL_PAD = 3 * KEY_TILE
VMEM_LIMIT = 56 * 1024 * 1024
ROW_DMA_UNROLL = 8
CMP_VARIANTS = 8

NAT_CMP, NAT_KSLC, NAT_KWIN, NAT_DK, NAT_GAB, NAT_SMALL = 0, 256, 384, 512, 1024, 3072
TR_NQ, TR_DQ, TR_DV, TR_IQ, TR_VSLC, TR_VWIN = 0, 512, 1024, 1536, 1792, 1920
SMALL_GATE, SMALL_IK, SMALL_IW = 0, 24, 56


def _rel_bucket_table(n):
    d = np.arange(n)
    nf = np.maximum(d, 1).astype(np.float64)
    large = REL_MAX_EXACT + (np.log(nf / REL_MAX_EXACT) / math.log(REL_MAX_DIST / REL_MAX_EXACT)
                             * (REL_BUCKETS - REL_MAX_EXACT)).astype(np.int64)
    large = np.minimum(large, REL_BUCKETS - 1)
    return np.where(d < REL_MAX_EXACT, d, large)


def _cparams(sem, vmem=VMEM_LIMIT):
    return pltpu.CompilerParams(dimension_semantics=sem, vmem_limit_bytes=vmem)


def _dot(a, b):
    return jnp.dot(a, b, preferred_element_type=F32)


def _dot_tn(a, b):
    return lax.dot_general(a, b, (((0,), (0,)), ((), ())), preferred_element_type=F32)


def _inproj_body(x_ref, gmix_ref, w_ref, wt_ref, gains_ref, gcol_ref,
                 cmp_ref, kslc_ref, kwin_ref, dk_ref, gab_ref, small_ref,
                 nqt_ref, dqt_ref, dvt_ref, iqt_ref, vst_ref, vwt_ref):
    x = x_ref[...]
    h = x * lax.rsqrt(jnp.mean(x * x, axis=-1, keepdims=True) + EPS) * gmix_ref[...]
    hb = h.astype(BF16)
    tm = hb.shape[0]
    lo = lax.broadcasted_iota(I32, (1, LANES), 1) < HEAD_DIM

    def mm(off, n):
        return _dot(hb, w_ref[:, off:off + n])

    def headnorm(y, g):
        y2 = y * y
        s_lo = jnp.sum(jnp.where(lo, y2, 0.0), axis=-1, keepdims=True)
        s_hi = jnp.sum(jnp.where(lo, 0.0, y2), axis=-1, keepdims=True)
        r = jnp.where(lo, lax.rsqrt(s_lo * (1.0 / HEAD_DIM) + EPS), lax.rsqrt(s_hi * (1.0 / HEAD_DIM) + EPS))
        return y * r * g

    def headnorm_t(y, g):
        y3 = y.reshape(y.shape[0] // HEAD_DIM, HEAD_DIM, tm)
        r = lax.rsqrt(jnp.mean(y3 * y3, axis=1, keepdims=True) + EPS)
        return (y3 * r * g).reshape(y.shape)

    g_kn, g_kd = gains_ref[0:1, :], gains_ref[1:2, :]
    def put_heads(ref, y):
        for k in range(ref.shape[0]):
            ref[k] = y[:, k * HEAD_DIM:(k + 1) * HEAD_DIM].astype(BF16)

    put_heads(cmp_ref, mm(NAT_CMP, 256))
    put_heads(kslc_ref, headnorm(mm(NAT_KSLC, LANES), g_kn))
    put_heads(kwin_ref, headnorm(mm(NAT_KWIN, LANES), g_kn))
    for c in range(DSA_W // LANES):
        dk_ref[:, c * LANES:(c + 1) * LANES] = headnorm(mm(NAT_DK + c * LANES, LANES), g_kd).astype(BF16)
    gab_ref[...] = jax.nn.sigmoid(mm(NAT_GAB, 2 * D_MODEL))
    small_ref[...] = mm(NAT_SMALL, LANES)

    def mm_t(off, n):
        return lax.dot_general(wt_ref[off:off + n, :], hb, (((1,), (1,)), ((), ())), preferred_element_type=F32)

    nqt_ref[...] = headnorm_t(mm_t(TR_NQ, NSA_W), gcol_ref[0]).astype(BF16)
    dqt_ref[...] = headnorm_t(mm_t(TR_DQ, DSA_W), gcol_ref[1]).astype(BF16)
    dvt_ref[...] = mm_t(TR_DV, DSA_W).astype(BF16)
    iqt_ref[...] = (mm_t(TR_IQ, IDX_HEADS * IDX_DIM) * (IDX_DIM ** -0.5)).astype(BF16)
    vst_ref[...] = mm_t(TR_VSLC, LANES).astype(BF16)
    vwt_ref[...] = mm_t(TR_VWIN, LANES).astype(BF16)


def _inproj(x2, g_mix, w_nat, w_tr, gains, gcol, tm=512):
    S = x2.shape[0]
    row = lambda n: pl.BlockSpec((tm, n), lambda i: (i, 0))
    col = lambda n: pl.BlockSpec((n, tm), lambda i: (0, i))
    full = lambda a: pl.BlockSpec(a.shape, lambda i: (0,) * a.ndim)
    heads = lambda n: pl.BlockSpec((n, tm, HEAD_DIM), lambda i: (0, i, 0))
    grouped = [2 * NSA_KV_GROUPS, NSA_KV_GROUPS, NSA_KV_GROUPS]
    nat = [(DSA_W, BF16), (2 * D_MODEL, F32), (LANES, F32)]
    tr = [NSA_W, DSA_W, DSA_W, IDX_HEADS * IDX_DIM, LANES, LANES]
    return pl.pallas_call(
        _inproj_body,
        grid=(S // tm,),
        in_specs=[row(D_MODEL), full(g_mix), full(w_nat), full(w_tr), full(gains), full(gcol)],
        out_specs=[heads(n) for n in grouped] + [row(n) for n, _ in nat] + [col(n) for n in tr],
        out_shape=[jax.ShapeDtypeStruct((n, S, HEAD_DIM), BF16) for n in grouped]
                  + [jax.ShapeDtypeStruct((S, n), dt) for n, dt in nat]
                  + [jax.ShapeDtypeStruct((n, S), BF16) for n in tr],
        compiler_params=_cparams(("arbitrary",)),
        name="inproj",
    )(x2, g_mix, w_nat, w_tr, gains, gcol)


def _compress_body(xa_ref, xb_ref, w1a_ref, w1b_ref, w1f_ref, pe_ref, w2_ref, gk_ref, o_ref):
    is_k = pl.program_id(0) < NSA_KV_GROUPS
    ns = xa_ref.shape[0]
    pe_term = jnp.sum(pe_ref[...] * w1f_ref[...], axis=0, keepdims=True)
    second = _dot(xb_ref[...], w1b_ref[...])
    second = jnp.where(pl.program_id(1) == 3, pltpu.roll(second, ns - 1, axis=0), second)
    pre = _dot(xa_ref[...], w1a_ref[...]) + second + pe_term
    hdn = pre * jax.nn.sigmoid(pre)
    out = _dot(hdn.astype(BF16), w2_ref[...])
    normed = out * lax.rsqrt(jnp.mean(out * out, axis=-1, keepdims=True) + EPS) * gk_ref[...]
    o_ref[...] = jnp.where(is_k, normed, out).astype(BF16)


def _compress(xc, w1a, w1b, w1f, pe, w2, gk):
    n4, ns, kdim = xc.shape[0], xc.shape[1], xc.shape[2] // 4
    G = NSA_KV_GROUPS
    kv = lambda i: i // G
    return pl.pallas_call(
        _compress_body,
        grid=(n4, 4),
        in_specs=[pl.BlockSpec((None, ns, kdim), lambda i, j: (i, 0, j)),
                  pl.BlockSpec((None, ns, kdim), lambda i, j: (i, 0, (j + 1) % 4)),
                  pl.BlockSpec((None, kdim, HEAD_DIM), lambda i, j: (kv(i), 0, 0)),
                  pl.BlockSpec((None, kdim, HEAD_DIM), lambda i, j: (kv(i), 0, 0)),
                  pl.BlockSpec((None, 2 * kdim, HEAD_DIM), lambda i, j: (kv(i), 0, 0)),
                  pl.BlockSpec((None, 2 * kdim, 1), lambda i, j: (kv(i), 0, 0)),
                  pl.BlockSpec((None, HEAD_DIM, HEAD_DIM), lambda i, j: (kv(i), 0, 0)),
                  pl.BlockSpec((1, HEAD_DIM), lambda i, j: (0, 0))],
        out_specs=pl.BlockSpec((None, ns, HEAD_DIM), lambda i, j: (i, j, 0)),
        out_shape=jax.ShapeDtypeStruct((n4, 4 * ns, HEAD_DIM), BF16),
        compiler_params=_cparams(("arbitrary", "arbitrary")),
        name="compress",
    )(xc, xc, w1a, w1b, w1f, pe, w2, gk)


def _cmp_body(qt_ref, kc_ref, vct_ref, bt_ref, oct_ref, selt_ref, s_ref, imp_ref, *, NS):
    i = pl.program_id(1)

    @pl.when((pl.program_id(0) == 0) & (i == 0))
    def _():
        imp_ref[...] = jnp.zeros(imp_ref.shape, F32)

    imp_prev = imp_ref[...]
    ib = jnp.minimum(i, pl.num_programs(1) - 2)
    t0 = ib * Q_BLOCK
    T = Q_BLOCK
    R = NSA_GROUP_SIZE
    WIN = 2 * SUBLANES

    def step(ne):
        NE = 4 * ne
        row = lax.broadcasted_iota(I32, (NE, 1), 0)
        plane = row // ne
        nblk = row - plane * ne
        cend = NSA_SLC_BLOCK * nblk + NSA_CMP_STRIDE * plane + (NSA_CMP_LEN - 1)
        tq = t0 + lax.broadcasted_iota(I32, (1, T), 1)
        mask = cend <= tq
        kc = jnp.concatenate([kc_ref[j * NS:j * NS + ne, :] for j in range(4)], axis=0)
        vct = jnp.concatenate([vct_ref[:, j * NS:j * NS + ne] for j in range(4)], axis=1)
        wstart = jnp.clip((2 * ib - 2) // SUBLANES * SUBLANES, 0, ne - WIN)
        wrow = lax.broadcasted_iota(I32, (WIN, 1), 0)
        psum = jnp.zeros((NE, T), F32)
        for r in range(R):
            s_ref[r, 0:NE, :] = _dot(kc, qt_ref[r * HEAD_DIM:(r + 1) * HEAD_DIM, :])
        for r in range(R):
            bt = bt_ref[r]
            for j in range(4):
                add = jnp.zeros((WIN, T), F32)
                for k in range(4):
                    n = 2 * ib + 1 - k
                    add = add + jnp.where((wrow == n - wstart) & (n >= 0), bt[4 * j + k:4 * j + k + 1, :], 0.0)
                sl = pl.ds(pl.multiple_of(j * ne + wstart, SUBLANES), WIN)
                s_ref[r, sl, :] = s_ref[r, sl, :] + add
            sc = jnp.where(mask, s_ref[r, 0:NE, :], -jnp.inf)
            m = jnp.maximum(jnp.max(sc, axis=0, keepdims=True), -1e30)
            e = jnp.exp2(sc - m)
            pc = e / jnp.maximum(jnp.sum(e, axis=0, keepdims=True), 1e-30)
            oct_ref[r * HEAD_DIM:(r + 1) * HEAD_DIM, :] = _dot(vct, pc.astype(BF16))
            psum = psum + pc
        p0, p1, p2, p3 = (psum[j * ne:(j + 1) * ne, :] for j in range(4))
        p3s = jnp.where(lax.broadcasted_iota(I32, (ne, 1), 0) == 0, 0.0, pltpu.roll(p3, 1, axis=0))
        imp_ref[0:ne, :] = p0 + p1 + p2 + 0.5 * p3 + 0.5 * p3s
        if ne < NS:
            imp_ref[ne:NS, :] = jnp.zeros((NS - ne, T), F32)

        n_ar = lax.broadcasted_iota(I32, (NS, 1), 0)
        tqs = (i - 1) * Q_BLOCK + lax.broadcasted_iota(I32, (1, T), 1)
        cur = tqs // NSA_SLC_BLOCK
        valid = NSA_SLC_BLOCK * n_ar <= tqs
        forced = valid & ((n_ar == 0) | (n_ar > cur - NSA_LOCAL_BLOCKS))
        score = jnp.where(forced, 1e30, jnp.where(valid, imp_prev, -1.0))
        sel = jnp.zeros((NS, T), jnp.bool_)
        for _ in range(min(NSA_SLC_TOPK, NS)):
            mx = jnp.max(score, axis=0, keepdims=True)
            first = jnp.min(jnp.where(score == mx, n_ar, NS), axis=0, keepdims=True)
            pick = n_ar == first
            sel = sel | pick
            score = jnp.where(pick, -2.0, score)
        selb = jnp.where(sel, 1.0, 0.0).astype(BF16)
        if NS < LANES:
            selb = jnp.concatenate([selb, jnp.zeros((LANES - NS, T), BF16)], axis=0)
        for w in range(selt_ref.shape[0]):
            selt_ref[w] = selb[w * LANES:(w + 1) * LANES, :]

    part = NS // CMP_VARIANTS
    if part % WIN == 0:
        which = (2 * ib + 2 + part - 1) // part - 1
        for v in range(CMP_VARIANTS):
            pl.when(which == v)(functools.partial(step, (v + 1) * part))
    else:
        step(NS)


def _cmp(nqt, kc, vct, bt, S):
    NS = S // NSA_SLC_BLOCK
    NCP = 4 * NS
    NW = max(NS // LANES, 1)
    G, R = NSA_KV_GROUPS, NSA_GROUP_SIZE
    nb = S // Q_BLOCK
    return pl.pallas_call(
        functools.partial(_cmp_body, NS=NS),
        grid=(G, nb + 1),
        in_specs=[pl.BlockSpec((R * HEAD_DIM, Q_BLOCK), lambda g, i: (g, jnp.minimum(i, nb - 1))),
                  pl.BlockSpec((None, NCP, HEAD_DIM), lambda g, i: (g, 0, 0)),
                  pl.BlockSpec((None, HEAD_DIM, NCP), lambda g, i: (g, 0, 0)),
                  pl.BlockSpec((R, 16, Q_BLOCK), lambda g, i: (g, 0, 0))],
        out_specs=[pl.BlockSpec((R * HEAD_DIM, Q_BLOCK), lambda g, i: (g, jnp.minimum(i, nb - 1))),
                   pl.BlockSpec((None, NW, LANES, Q_BLOCK), lambda g, i: (g, 0, 0, jnp.maximum(i - 1, 0)))],
        out_shape=[jax.ShapeDtypeStruct((NSA_W, S), F32),
                   jax.ShapeDtypeStruct((G, NW, LANES, S), BF16)],
        scratch_shapes=[pltpu.VMEM((R, NCP, Q_BLOCK), F32), pltpu.VMEM((NS, Q_BLOCK), F32)],
        compiler_params=_cparams(("arbitrary", "arbitrary")),
        name="cmp",
    )(nqt, kc, vct, bt)


def _flash_init(m_ref, l_ref, acc_ref):
    m_ref[...] = jnp.full(m_ref.shape, -1e30, F32)
    l_ref[...] = jnp.zeros(l_ref.shape, F32)
    acc_ref[...] = jnp.zeros(acc_ref.shape, F32)


def _flash_update(s, vt, m_ref, l_ref, acc_ref):
    m_old = m_ref[...]
    m_new = jnp.maximum(m_old, jnp.max(s, axis=0, keepdims=True))
    alpha = jnp.exp2(m_old - m_new)
    p = jnp.exp2(s - m_new)
    l_ref[...] = alpha * l_ref[...] + jnp.sum(p, axis=0, keepdims=True)
    acc_ref[...] = alpha * acc_ref[...] + _dot(vt, p.astype(BF16))
    m_ref[...] = m_new


def _slcwin_body(qt_ref, ks_ref, vst_ref, kw_ref, vwt_ref, selt_ref, tbt_ref, oct_ref, glt_ref, ot_ref,
                 ms_ref, ls_ref, as_ref, mw_ref, lw_ref, aw_ref, sbuf_ref, qaug_ref):
    i = pl.program_id(1)
    t0 = i * Q_BLOCK
    R = NSA_GROUP_SIZE
    T = Q_BLOCK
    q4t = jnp.concatenate([qt_ref[r * HEAD_DIM:(r + 1) * HEAD_DIM, :] for r in range(R)], axis=1)
    tq = lax.broadcasted_iota(I32, (1, T), 1)
    rep = lambda mk: jnp.concatenate([mk] * R, axis=1)
    tb4 = jnp.concatenate([tbt_ref[r] for r in range(R)], axis=1)

    def expand(w, base, nkeys):
        keyi = lax.broadcasted_iota(I32, (nkeys, LANES), 0)
        blk = lax.broadcasted_iota(I32, (nkeys, LANES), 1)
        e = jnp.where(blk == base + keyi // NSA_SLC_BLOCK, 1.0, 0.0).astype(BF16)
        return _dot(e, selt_ref[w])

    _flash_init(ms_ref, ls_ref, as_ref)
    _flash_init(mw_ref, lw_ref, aw_ref)
    n_win = selt_ref.shape[0]
    zpad = jnp.zeros((KAUG_W - HEAD_DIM - LANES, R * T), BF16)
    for w in range(n_win):
        qaug_ref[w] = jnp.concatenate([q4t, rep(selt_ref[w] - jnp.asarray(1.0, BF16)), zpad], axis=0)
    far_end = jnp.maximum(t0 - Q_BLOCK, 0) // KEY_TILE * KEY_TILE
    nfar = far_end // KEY_TILE
    tiles_per_window = LANES * NSA_SLC_BLOCK // KEY_TILE

    def score_far(kt, slot):
        r0 = pl.multiple_of(NSA_PADF + kt * KEY_TILE, LANES)
        w = jnp.minimum(kt // tiles_per_window, n_win - 1)
        sbuf_ref[slot] = _dot(ks_ref[pl.ds(r0, KEY_TILE), :], qaug_ref[w])

    def update_far(kt, slot):
        r0 = pl.multiple_of(NSA_PADF + kt * KEY_TILE, LANES)
        _flash_update(sbuf_ref[slot], vst_ref[:, pl.ds(r0, KEY_TILE)], ms_ref, ls_ref, as_ref)

    score_far(0, 0)

    def far2(k2, carry):
        kt = 2 * k2
        score_far(kt + 1, 1)
        update_far(kt, 0)
        score_far(kt + 2, 0)
        update_far(kt + 1, 1)
        return carry

    lax.fori_loop(0, nfar // 2, far2, 0)

    @pl.when(nfar % 2 == 1)
    def _():
        update_far(nfar - 1, 0)

    kq = lambda r0, n: _dot(ks_ref[pl.ds(r0, n), 0:HEAD_DIM], q4t)
    keyn = lax.broadcasted_iota(I32, (NEAR, 1), 0)
    near_ok = (keyn - Q_BLOCK <= tq) & (keyn + t0 - Q_BLOCK >= 0)
    rn = pl.multiple_of(t0 + NSA_PADF - Q_BLOCK, LANES)
    WF = NSA_WINDOW - Q_BLOCK
    rw = pl.multiple_of(t0 + NSA_PADF - NSA_WINDOW, LANES)
    keyw = lax.broadcasted_iota(I32, (WF, 1), 0)
    maskw = (keyw > tq) & (keyw + t0 - NSA_WINDOW >= 0)
    blocks = [jnp.maximum(2 * i - (NSA_WINDOW // NSA_SLC_BLOCK) + 2 * j, 0) for j in range(NSA_WINDOW // Q_BLOCK)]
    mexp = [expand(b // LANES, b % LANES, Q_BLOCK) for b in blocks]
    mask_sm = (jnp.concatenate(mexp[:-1], axis=0) > 0.5) & (keyw + t0 - NSA_WINDOW >= far_end)
    mask_sn = (jnp.concatenate([mexp[-1], expand(2 * i // LANES, 2 * i % LANES, Q_BLOCK)], axis=0) > 0.5) & near_ok
    s_sm = jnp.where(rep(mask_sm), kq(rw, WF), -jnp.inf)
    s_sn = jnp.where(rep(mask_sn), kq(rn, NEAR) + tb4, -jnp.inf)
    s_wf = jnp.where(rep(maskw), _dot(kw_ref[pl.ds(rw, WF), :], q4t), -jnp.inf)
    s_wn = jnp.where(rep(near_ok), _dot(kw_ref[pl.ds(rn, NEAR), :], q4t) + tb4, -jnp.inf)
    _flash_update(s_sm, vst_ref[:, pl.ds(rw, WF)], ms_ref, ls_ref, as_ref)
    _flash_update(s_sn, vst_ref[:, pl.ds(rn, NEAR)], ms_ref, ls_ref, as_ref)
    _flash_update(s_wf, vwt_ref[:, pl.ds(rw, WF)], mw_ref, lw_ref, aw_ref)
    _flash_update(s_wn, vwt_ref[:, pl.ds(rn, NEAR)], mw_ref, lw_ref, aw_ref)
    o_s = as_ref[...] / jnp.maximum(ls_ref[...], 1e-30)
    o_w = aw_ref[...] / jnp.maximum(lw_ref[...], 1e-30)

    gates = jax.nn.sigmoid(glt_ref[...])
    for r in range(R):
        hs = slice(r * HEAD_DIM, (r + 1) * HEAD_DIM)
        qs = slice(r * T, (r + 1) * T)
        o = (gates[3 * r:3 * r + 1, :] * oct_ref[hs, :]
             + gates[3 * r + 1:3 * r + 2, :] * o_s[:, qs]
             + gates[3 * r + 2:3 * r + 3, :] * o_w[:, qs])
        ot_ref[hs, :] = o.astype(BF16)


def _slcwin(nqt, ks, vst, kw, vwt, selt, tbt, oct, glt, S):
    G, R = NSA_KV_GROUPS, NSA_GROUP_SIZE
    spad = ks.shape[1]
    NW = selt.shape[1]
    once = pl.Buffered(1)
    kaspec = pl.BlockSpec((None, spad, KAUG_W), lambda g, i: (g, 0, 0), pipeline_mode=once)
    kspec = pl.BlockSpec((None, spad, HEAD_DIM), lambda g, i: (g, 0, 0), pipeline_mode=once)
    vspec = pl.BlockSpec((None, HEAD_DIM, spad), lambda g, i: (g, 0, 0), pipeline_mode=once)
    qspec = pl.BlockSpec((R * HEAD_DIM, Q_BLOCK), lambda g, i: (g, i))
    st = lambda n: pltpu.VMEM((n, R * Q_BLOCK), F32)
    return pl.pallas_call(
        _slcwin_body,
        grid=(G, S // Q_BLOCK),
        in_specs=[qspec, kaspec, vspec, kspec, vspec,
                  pl.BlockSpec((None, NW, LANES, Q_BLOCK), lambda g, i: (g, 0, 0, i)),
                  pl.BlockSpec((R, NEAR, Q_BLOCK), lambda g, i: (g, 0, 0)),
                  qspec,
                  pl.BlockSpec((None, 16, Q_BLOCK), lambda g, i: (g, 0, i))],
        out_specs=qspec,
        out_shape=jax.ShapeDtypeStruct((NSA_W, S), BF16),
        scratch_shapes=[st(1), st(1), st(HEAD_DIM), st(1), st(1), st(HEAD_DIM),
                        pltpu.VMEM((2, KEY_TILE, R * Q_BLOCK), F32),
                        pltpu.VMEM((NW, KAUG_W, R * Q_BLOCK), BF16)],
        compiler_params=_cparams(("arbitrary", "arbitrary")),
        name="slcwin",
    )(nqt, ks, vst, kw, vwt, selt, tbt, oct, glt)


INT_MIN = -2 ** 31
INT_MAX = 2 ** 31 - 1
MAX_PEEL_ROUNDS = 64
UNTESTED_BITS = 16
NEG_INF_KEY = int(np.array(-np.inf, np.float32).view(np.int32)) ^ 0x7FFFFFFF


def _dsa_body(dqt_ref, dk_ref, dvt_ref, iqt_ref, ikt_ref, iwt_ref, tbt_ref, ltri_ref, ot_ref,
              keys_ref, m_ref, l_ref, acc_ref, sbuf_ref, *, S, KEEP):
    i = pl.program_id(0)
    t0 = i * Q_BLOCK
    T = Q_BLOCK
    H = DSA_HEADS
    ntile = (t0 + Q_BLOCK + KEY_TILE - 1) // KEY_TILE
    tq = t0 + lax.broadcasted_iota(I32, (1, T), 1)
    key512 = lax.broadcasted_iota(I32, (KEY_TILE, 1), 0)
    ktile = lambda kt: keys_ref[pl.ds(pl.multiple_of(Q_BLOCK + kt * KEY_TILE, LANES), KEY_TILE), :]

    keys_ref[0:Q_BLOCK, :] = jnp.full((Q_BLOCK, T), NEG_INF_KEY, I32)
    iqt = iqt_ref[...]
    iwt = iwt_ref[...] * (IDX_HEADS ** -0.5)
    qis = [jnp.concatenate([iqt[(2 * j) * IDX_DIM:(2 * j + 1) * IDX_DIM, :],
                            iqt[(2 * j + 1) * IDX_DIM:(2 * j + 2) * IDX_DIM, :]], axis=1)
           for j in range(IDX_HEADS // 2)]
    wrow = [iwt[h:h + 1, :] for h in range(IDX_HEADS)]

    def idx_dots(kt, slot):
        ik = ikt_ref[:, pl.ds(pl.multiple_of(kt * KEY_TILE, KEY_TILE), KEY_TILE)]
        for j in range(IDX_HEADS // 2):
            sbuf_ref[slot, j] = _dot_tn(ik, qis[j])

    def idx_keys(kt, slot):
        c0 = pl.multiple_of(kt * KEY_TILE, KEY_TILE)
        sc = jnp.zeros((KEY_TILE, T), F32)
        for j in range(IDX_HEADS // 2):
            d = jnp.maximum(sbuf_ref[slot, j], 0.0)
            sc = sc + wrow[2 * j] * d[:, 0:T]
            sc = sc + wrow[2 * j + 1] * d[:, T:2 * T]
        sc = jnp.where(sc == 0.0, 0.0, sc)
        sc = jnp.where(c0 + key512 <= tq, sc, -jnp.inf)
        bits = pltpu.bitcast(sc, I32)
        keys_ref[pl.ds(pl.multiple_of(Q_BLOCK + c0, LANES), KEY_TILE), :] = jnp.where(
            bits < 0, bits ^ 0x7FFFFFFF, bits)

    idx_dots(0, 0)

    def score2(k2, carry):
        kt = 2 * k2
        idx_dots(kt + 1, 1)
        idx_keys(kt, 0)
        idx_dots(kt + 2, 0)
        idx_keys(kt + 1, 1)
        return carry

    lax.fori_loop(0, (ntile + 1) // 2, score2, 0)

    def count(pred):
        def body(kt, acc):
            hit = jnp.where(pred(ktile(kt), kt * KEY_TILE + key512), 1, 0)
            return acc + jnp.sum(hit.reshape(KEY_TILE // SUBLANES, SUBLANES, T), axis=0)
        acc = lax.fori_loop(0, ntile, body, jnp.zeros((SUBLANES, T), I32))
        return jnp.sum(acc, axis=0, keepdims=True)

    def min_at_least(lowest):
        def body(kt, acc):
            kk = ktile(kt)
            v = jnp.where(kk >= lowest, kk, INT_MAX)
            return jnp.minimum(acc, jnp.min(v.reshape(KEY_TILE // SUBLANES, SUBLANES, T), axis=0))
        acc = lax.fori_loop(0, ntile, body, jnp.full((SUBLANES, T), INT_MAX, I32))
        return jnp.min(acc, axis=0, keepdims=True)

    def any_open(st, over):
        _, excess, fin, _ = st
        return jnp.max(jnp.where((fin == 0) & (excess > over), 1, 0)) > 0

    def bisect(b, st):
        thr, excess, fin, need = st
        cand = thr + lax.shift_left(jnp.int32(1), 31 - b)
        n = count(lambda kk, ab: kk >= cand)
        ok = (n >= KEEP) & (fin == 0)
        return jnp.where(ok, cand, thr), jnp.where(ok, n - KEEP, excess), fin, need

    def peel(st):
        thr, excess, fin, need = st
        low = min_at_least(thr)
        n_low = count(lambda kk, ab: kk == low)
        active = (fin == 0) & (excess > 0)
        tied = active & (n_low > excess)
        drop = active & (n_low <= excess)
        return (jnp.where(tied, low, jnp.where(drop, low + 1, thr)), jnp.where(drop, excess - n_low, excess),
                jnp.where(tied, 1, fin), jnp.where(tied, n_low - excess, need))

    zero = jnp.zeros((1, T), I32)
    st = (jnp.full((1, T), INT_MIN, I32), KEY_TILE * ntile - KEEP + zero, zero, zero)
    st = peel(bisect(0, st))
    st = lax.fori_loop(1, UNTESTED_BITS, bisect, st)
    _, st = lax.while_loop(lambda c: (c[0] < 32) & any_open(c[1], 1),
                           lambda c: (c[0] + 1, bisect(c[0], c[1])), (jnp.int32(UNTESTED_BITS), st))
    _, st = lax.while_loop(lambda c: (c[0] < MAX_PEEL_ROUNDS) & any_open(c[1], 0),
                           lambda c: (c[0] + 1, peel(c[1])), (jnp.int32(0), st))
    thr, _, fin, need = st
    tie = (fin == 1) & (thr > NEG_INF_KEY)

    def tie_cut():
        needf = need.astype(F32)

        def body(kt, c):
            before, jmax = c
            is_tie = ktile(kt) == thr
            tied = jnp.where(is_tie, 1.0, 0.0)
            rank = before + _dot(ltri_ref[...], tied.astype(BF16))
            admit = is_tie & (rank < needf)
            jm = jnp.max(jnp.where(admit, kt * KEY_TILE + key512, -1), axis=0, keepdims=True)
            return before + jnp.sum(tied, axis=0, keepdims=True), jnp.maximum(jmax, jm)

        _, jmax = lax.fori_loop(0, ntile, body, (jnp.zeros((1, T), F32), jnp.full((1, T), -1, I32)))
        return jnp.where(tie, jmax + 1, S)

    jcut = lax.cond(jnp.max(jnp.where(tie, 1, 0)) > 0, tie_cut, lambda: jnp.full((1, T), S, I32))

    def chosen(kk, ab):
        return (kk > thr) | ((kk == thr) & (ab < jcut))

    _flash_init(m_ref, l_ref, acc_ref)
    lo = lax.broadcasted_iota(I32, (LANES, 1), 0) < HEAD_DIM
    qpairs = []
    for pr in range(H // 2):
        qp = dqt_ref[pr * LANES:(pr + 1) * LANES, :]
        zero = jnp.zeros_like(qp)
        qpairs.append(jnp.concatenate([jnp.where(lo, qp, zero), jnp.where(lo, zero, qp)], axis=1))

    def scores(r0, nkeys, mask, bias):
        mask2 = jnp.concatenate([mask, mask], axis=1)
        out = []
        for pr in range(H // 2):
            s = _dot(dk_ref[pl.ds(r0, nkeys), pr * LANES:(pr + 1) * LANES], qpairs[pr])
            if bias is not None:
                s = s + jnp.concatenate([bias[2 * pr], bias[2 * pr + 1]], axis=1)
            out.append(jnp.where(mask2, s, -jnp.inf))
        return out

    def update(pr, s, r0, nkeys):
        m_old = m_ref[pr]
        m_new = jnp.maximum(m_old, jnp.max(s, axis=0, keepdims=True))
        alpha = jnp.exp2(m_old - m_new)
        p = jnp.exp2(s - m_new)
        l_ref[pr] = alpha * l_ref[pr] + jnp.sum(p, axis=0, keepdims=True)
        m_ref[pr] = m_new
        vt = dvt_ref[pr * LANES:(pr + 1) * LANES, pl.ds(r0, nkeys)]
        acc_ref[pr] = alpha * acc_ref[pr] + _dot(vt, p.astype(BF16))

    far_end = t0 - Q_BLOCK
    nfar = jnp.maximum(far_end + KEY_TILE - 1, 0) // KEY_TILE
    far_row = lambda kt: pl.multiple_of(DSA_PADF + kt * KEY_TILE, LANES)

    def score_far(kt, slot):
        ab = kt * KEY_TILE + key512
        mask = chosen(ktile(jnp.minimum(kt, ntile - 1)), ab) & (ab < far_end)
        for pr, s in enumerate(scores(far_row(kt), KEY_TILE, mask, None)):
            sbuf_ref[slot, pr] = s

    def update_far(kt, slot):
        for pr in range(H // 2):
            update(pr, sbuf_ref[slot, pr], far_row(kt), KEY_TILE)

    score_far(0, 0)

    def far2(k2, carry):
        kt = 2 * k2
        score_far(kt + 1, 1)
        update_far(kt, 0)
        score_far(kt + 2, 0)
        update_far(kt + 1, 1)
        return carry

    lax.fori_loop(0, (nfar + 1) // 2, far2, 0)

    abn = lax.broadcasted_iota(I32, (NEAR, 1), 0) + t0 - Q_BLOCK
    kkn = keys_ref[pl.ds(pl.multiple_of(t0, LANES), NEAR), :]
    rn = pl.multiple_of(t0 + DSA_PADF - Q_BLOCK, LANES)
    for pr, s in enumerate(scores(rn, NEAR, chosen(kkn, abn) & (abn <= tq) & (abn >= 0), tbt_ref)):
        update(pr, s, rn, NEAR)

    for pr in range(H // 2):
        o = acc_ref[pr] / jnp.maximum(l_ref[pr], 1e-30)
        ot_ref[pr * LANES:(pr + 1) * LANES, :] = jnp.where(lo, o[:, 0:T], o[:, T:2 * T]).astype(BF16)


def _dsa(dqt, dkp, dvtp, iqt, ikt, iwt, tbt, ltri, S):
    KEEP = min(DSA_TOPK_MAX, S // 4)
    H = DSA_HEADS
    T = Q_BLOCK
    once = pl.Buffered(1)
    full = lambda a: pl.BlockSpec(a.shape, lambda i: (0,) * a.ndim, pipeline_mode=once)
    return pl.pallas_call(
        functools.partial(_dsa_body, S=S, KEEP=KEEP),
        grid=(S // T,),
        in_specs=[pl.BlockSpec((DSA_W, T), lambda i: (0, i)),
                  full(dkp), full(dvtp),
                  pl.BlockSpec((IDX_HEADS * IDX_DIM, T), lambda i: (0, i)),
                  full(ikt),
                  pl.BlockSpec((IDX_HEADS, T), lambda i: (0, i)),
                  full(tbt), full(ltri)],
        out_specs=pl.BlockSpec((DSA_W, T), lambda i: (0, i)),
        out_shape=jax.ShapeDtypeStruct((DSA_W, S), BF16),
        scratch_shapes=[pltpu.VMEM((Q_BLOCK + S + KEY_TILE, T), I32),
                        pltpu.VMEM((H // 2, 1, 2 * T), F32),
                        pltpu.VMEM((H // 2, 1, 2 * T), F32),
                        pltpu.VMEM((H // 2, LANES, 2 * T), F32),
                        pltpu.VMEM((2, H // 2, KEY_TILE, 2 * T), F32)],
        compiler_params=_cparams(("arbitrary",)),
        name="dsa",
    )(dqt, dkp, dvtp, iqt, ikt, iwt, tbt, ltri)


def _split_bf16(a):
    hi = a.astype(BF16)
    return hi, (a - hi.astype(F32)).astype(BF16)


def _merge_body(x_ref, oa_ref, ob_ref, gab_ref, wa_ref, wb_ref, wo_ref, gffn_ref, wr_ref, br_ref,
                x1_ref, h2_ref, ridx_ref, rw_ref):
    a = _dot_tn(oa_ref[...], wa_ref[...])
    b = _dot_tn(ob_ref[...], wb_ref[...])
    gab = gab_ref[...]
    merged = gab[:, :D_MODEL] * a + gab[:, D_MODEL:] * b
    x1 = x_ref[...] + _dot(merged.astype(BF16), wo_ref[...])
    x1_ref[...] = x1
    h2 = x1 * lax.rsqrt(jnp.mean(x1 * x1, axis=-1, keepdims=True) + EPS) * gffn_ref[...]
    h2_ref[...] = h2
    hh, hl = _split_bf16(h2)
    wh, wl = _split_bf16(wr_ref[...])
    logits = _dot(hh, wh) + _dot(hh, wl) + _dot(hl, wh) + br_ref[...]
    lane = lax.broadcasted_iota(I32, (1, LANES), 1)
    idx_out = jnp.zeros(logits.shape, I32)
    val_out = jnp.full(logits.shape, -jnp.inf, F32)
    for k in range(TOP_K):
        mx = jnp.max(logits, axis=-1, keepdims=True)
        first = jnp.min(jnp.where(logits == mx, lane, LANES), axis=-1, keepdims=True)
        idx_out = jnp.where(lane == k, first, idx_out)
        val_out = jnp.where(lane == k, mx, val_out)
        logits = jnp.where(lane == first, -jnp.inf, logits)
    e = jnp.exp(val_out - jnp.max(val_out, axis=-1, keepdims=True))
    ridx_ref[...] = idx_out
    rw_ref[...] = e / jnp.sum(e, axis=-1, keepdims=True)


def _merge(x2, oa, ob, gab, wa, wb, wo, gffn, wr, br, tm=512):
    S = x2.shape[0]
    row = lambda n: pl.BlockSpec((tm, n), lambda i: (i, 0))
    full = lambda a: pl.BlockSpec(a.shape, lambda i: (0,) * a.ndim)
    return pl.pallas_call(
        _merge_body,
        grid=(S // tm,),
        in_specs=[row(D_MODEL), pl.BlockSpec((NSA_W, tm), lambda i: (0, i)), pl.BlockSpec((DSA_W, tm), lambda i: (0, i)),
                  row(2 * D_MODEL),
                  full(wa), full(wb), full(wo), full(gffn), full(wr), full(br)],
        out_specs=[row(D_MODEL), row(D_MODEL), row(LANES), row(LANES)],
        out_shape=[jax.ShapeDtypeStruct((S, D_MODEL), F32), jax.ShapeDtypeStruct((S, D_MODEL), F32),
                   jax.ShapeDtypeStruct((S, LANES), I32), jax.ShapeDtypeStruct((S, LANES), F32)],
        compiler_params=_cparams(("arbitrary",)),
        name="merge",
    )(x2, oa, ob, gab, wa, wb, wo, gffn, wr, br)


def _rank_body(ridx_ref, rank_ref, cnt_ref, carry_ref):
    @pl.when(pl.program_id(0) == 0)
    def _():
        carry_ref[...] = jnp.zeros(carry_ref.shape, F32)

    ridx = ridx_ref[...]
    tm = ridx.shape[0]
    lane = lax.broadcasted_iota(I32, (1, LANES), 1)
    hits = [lane == ridx[:, k:k + 1] for k in range(TOP_K)]
    member = jnp.zeros((tm, LANES), F32)
    for hk in hits:
        member = member + jnp.where(hk, 1.0, 0.0)
    ri = lax.broadcasted_iota(I32, (tm, tm), 0)
    ci = lax.broadcasted_iota(I32, (tm, tm), 1)
    lower = jnp.where(ci < ri, 1.0, 0.0).astype(BF16)
    before = _dot(lower, member.astype(BF16)) + carry_ref[0:1, :]
    out = jnp.zeros((tm, LANES), I32)
    for k, hk in enumerate(hits):
        rk = jnp.sum(jnp.where(hk, before, 0.0), axis=-1, keepdims=True)
        out = jnp.where(lane == k, rk.astype(I32), out)
    rank_ref[...] = out
    total = carry_ref[0:1, :] + jnp.sum(member, axis=0, keepdims=True)
    carry_ref[...] = jnp.broadcast_to(total, carry_ref.shape)
    cnt_ref[...] = jnp.broadcast_to(total, cnt_ref.shape).astype(I32)


def _rank(ridx, tm=512):
    S = ridx.shape[0]
    return pl.pallas_call(
        _rank_body,
        grid=(S // tm,),
        in_specs=[pl.BlockSpec((tm, LANES), lambda i: (i, 0))],
        out_specs=[pl.BlockSpec((tm, LANES), lambda i: (i, 0)), pl.BlockSpec((8, LANES), lambda i: (0, 0))],
        out_shape=[jax.ShapeDtypeStruct((S, LANES), I32), jax.ShapeDtypeStruct((8, LANES), I32)],
        scratch_shapes=[pltpu.VMEM((8, LANES), F32)],
        compiler_params=_cparams(("arbitrary",)),
        name="rank",
    )(ridx)


def _dispatch_body(dest_ref, zrow_ref, nu_ref, h_ref, xe_ref, zbuf_ref, sem, zsem):
    i = pl.program_id(0)
    tm = h_ref.shape[0]

    @pl.when(i == 0)
    def _():
        zbuf_ref[...] = jnp.zeros(zbuf_ref.shape, F32)
        fill = lambda row: pltpu.make_async_copy(zbuf_ref, xe_ref.at[pl.ds(row, MOE_ROWS)], zsem)
        n_chunks = xe_ref.shape[0] // MOE_ROWS
        for wait in (False, True):
            for e in range(N_EXPERTS):
                @pl.when(zrow_ref[e] >= 0)
                def _():
                    cp = fill(pl.multiple_of(jnp.maximum(zrow_ref[e], 0), MOE_ROWS))
                    cp.wait() if wait else cp.start()

            def tail(j, c):
                cp = fill(pl.multiple_of(j * MOE_ROWS, MOE_ROWS))
                cp.wait() if wait else cp.start()
                return c
            lax.fori_loop(nu_ref[0], n_chunks, tail, 0)

    def start(r, c):
        for k in range(TOP_K):
            d = dest_ref[(i * tm + r) * TOP_K + k]
            pltpu.make_async_copy(h_ref.at[pl.ds(r, 1)], xe_ref.at[pl.ds(d, 1)],
                                  sem).start(priority=k % 2)
        return c

    lax.fori_loop(0, tm, start, 0, unroll=ROW_DMA_UNROLL)
    for _ in range(TOP_K):
        pltpu.make_async_copy(h_ref, xe_ref.at[pl.ds(0, tm)], sem).wait()


def _dispatch(dest, zrow, n_used, h2, n_rows, tm=256):
    S = h2.shape[0]
    return pl.pallas_call(
        _dispatch_body,
        grid_spec=pltpu.PrefetchScalarGridSpec(
            num_scalar_prefetch=3,
            grid=(S // tm,),
            in_specs=[pl.BlockSpec((tm, D_MODEL), lambda i, d, z, n: (i, 0))],
            out_specs=pl.BlockSpec(memory_space=pl.ANY),
            scratch_shapes=[pltpu.VMEM((MOE_ROWS, D_MODEL), F32), pltpu.SemaphoreType.DMA(()),
                            pltpu.SemaphoreType.DMA(())]),
        out_shape=jax.ShapeDtypeStruct((n_rows, D_MODEL), F32),
        compiler_params=_cparams(("arbitrary",)),
        name="dispatch",
    )(dest, zrow, n_used, h2)


def _experts_body(ce_ref, nu_ref, x_ref, w1_ref, b1_ref, w2_ref, b2_ref, y_ref, w1b_ref, w2b_ref):
    c = pl.program_id(0)

    @pl.when((c == 0) | (ce_ref[c] != ce_ref[jnp.maximum(c - 1, 0)]))
    def _():
        w1b_ref[...] = w1_ref[...].astype(BF16)
        w2b_ref[...] = w2_ref[...].astype(BF16)

    @pl.when(c < nu_ref[0])
    def _():
        gu = _dot(x_ref[...].astype(BF16), w1b_ref[...]) + b1_ref[...]
        gate = jnp.minimum(gu[:, :D_FF], SWIGLU_LIMIT)
        lin = jnp.clip(gu[:, D_FF:], -SWIGLU_LIMIT, SWIGLU_LIMIT)
        act = (lin + 1.0) * gate * jax.nn.sigmoid(SWIGLU_ALPHA * gate)
        y_ref[...] = _dot(act.astype(BF16), w2b_ref[...]) + b2_ref[...]

    @pl.when(c >= nu_ref[0])
    def _():
        y_ref[...] = jnp.zeros(y_ref.shape, F32)


def _experts(chunk_e, n_used, xe, w1, b1, w2, b2):
    P = xe.shape[0]
    nch = P // MOE_ROWS
    return pl.pallas_call(
        _experts_body,
        grid_spec=pltpu.PrefetchScalarGridSpec(
            num_scalar_prefetch=2,
            grid=(nch,),
            in_specs=[pl.BlockSpec((MOE_ROWS, D_MODEL), lambda c, ce, nu: (c, 0)),
                      pl.BlockSpec((None, D_MODEL, 2 * D_FF), lambda c, ce, nu: (ce[c], 0, 0)),
                      pl.BlockSpec((None, 1, 2 * D_FF), lambda c, ce, nu: (ce[c], 0, 0)),
                      pl.BlockSpec((None, D_FF, D_MODEL), lambda c, ce, nu: (ce[c], 0, 0)),
                      pl.BlockSpec((None, 1, D_MODEL), lambda c, ce, nu: (ce[c], 0, 0))],
            out_specs=pl.BlockSpec((MOE_ROWS, D_MODEL), lambda c, ce, nu: (c, 0)),
            scratch_shapes=[pltpu.VMEM((D_MODEL, 2 * D_FF), BF16), pltpu.VMEM((D_FF, D_MODEL), BF16)]),
        out_shape=jax.ShapeDtypeStruct((P, D_MODEL), F32),
        compiler_params=_cparams(("arbitrary",)),
        name="experts",
    )(chunk_e, n_used, xe, w1, b1, w2, b2)


def _final_body(dest_ref, x1_ref, rw_ref, p_ref, gple_ref, wg_ref, wp_ref, y_ref, o_ref, rows_ref, sems):
    i = pl.program_id(0)
    tm = x1_ref.shape[0]

    def gather(blk, slot):
        def start(r, c):
            for k in range(TOP_K):
                d = dest_ref[(blk * tm + r) * TOP_K + k]
                pltpu.make_async_copy(y_ref.at[pl.ds(d, 1)], rows_ref.at[slot, k, pl.ds(r, 1)],
                                      sems.at[slot]).start(priority=k % 2)
            return c
        lax.fori_loop(0, tm, start, 0, unroll=ROW_DMA_UNROLL)

    def drain(which):
        for k in range(TOP_K):
            pltpu.make_async_copy(y_ref.at[pl.ds(0, tm)], rows_ref.at[which, k], sems.at[which]).wait()

    @pl.when(i == 0)
    def _():
        gather(0, 0)

    slot = i % 2
    last = pl.num_programs(0) - 1
    drain(slot)
    nxt = jnp.minimum(i + 1, last)
    for r in range(tm):
        for k in range(TOP_K):
            d = dest_ref[(nxt * tm + r) * TOP_K + k]
            pltpu.make_async_copy(y_ref.at[pl.ds(d, 1)], rows_ref.at[1 - slot, k, pl.ds(r, 1)],
                                  sems.at[1 - slot]).start(priority=k % 2)
    rw = rw_ref[...]
    x2 = x1_ref[...]
    for k in range(TOP_K):
        x2 = x2 + rw[:, k:k + 1] * rows_ref[slot, k]
    hn = x2 * lax.rsqrt(jnp.mean(x2 * x2, axis=-1, keepdims=True) + EPS) * gple_ref[...]
    gate = jax.nn.sigmoid(_dot(hn.astype(BF16), wg_ref[...]))
    o_ref[...] = x2 + gate * _dot(p_ref[...].astype(BF16), wp_ref[...])

    @pl.when(i == last)
    def _():
        drain(1 - slot)


def _final(dest, x1, rw, p2, gple, wg, wp, yexp, tm=256):
    S = x1.shape[0]
    row = lambda n: pl.BlockSpec((tm, n), lambda i, d: (i, 0))
    full = lambda a: pl.BlockSpec(a.shape, lambda i, d: (0,) * a.ndim)
    return pl.pallas_call(
        _final_body,
        grid_spec=pltpu.PrefetchScalarGridSpec(
            num_scalar_prefetch=1,
            grid=(S // tm,),
            in_specs=[row(D_MODEL), row(LANES), row(PLE_DIM), full(gple), full(wg), full(wp),
                      pl.BlockSpec(memory_space=pl.ANY)],
            out_specs=row(D_MODEL),
            scratch_shapes=[pltpu.VMEM((2, TOP_K, tm, D_MODEL), F32), pltpu.SemaphoreType.DMA((2,))]),
        out_shape=jax.ShapeDtypeStruct((S, D_MODEL), F32),
        compiler_params=_cparams(("arbitrary",)),
        name="final",
    )(dest, x1, rw, p2, gple, wg, wp, yexp)


def _distance_bias(rel_cols):
    return (rel_cols[_rel_bucket_table(NEAR)] - rel_cols[REL_BUCKETS - 1][None, :]).T * LOG2E


def _near_bias_t(rel_cols):
    h = rel_cols.shape[1]
    g = jnp.pad(_distance_bias(rel_cols), ((0, 0), (Q_BLOCK, Q_BLOCK)))
    period = 2 * NEAR
    skew = jnp.tile(g, (1, NEAR))[:, :NEAR * (period - 1)].reshape(h, NEAR, period - 1)
    return skew[:, :, NEAR:NEAR + Q_BLOCK]


def _cmp_bias_t(rel_cols):
    g = jnp.pad(_distance_bias(rel_cols), ((0, 0), (NEAR, Q_BLOCK)))
    rows = []
    for j in range(4):
        for k in range(4):
            c0 = -NSA_CMP_STRIDE * j - (NSA_CMP_LEN - 1) + NSA_SLC_BLOCK * (k - 1)
            rows.append(g[:, c0 + NEAR:c0 + NEAR + Q_BLOCK])
    return jnp.stack(rows, axis=1)


def kernel(x, p, w_in, g_mix, g_q_nsa, g_k_nsa, g_q_dsa, g_k_dsa, pe_ck, w_ck1, w_ck2, pe_cv, w_cv1, w_cv2,
           w_branch_a, w_branch_b, w_out, rel_bias, g_ffn, w_router, b_router, w_e1, b_e1, w_e2, b_e2,
           g_ple, w_ple, w_ple_gate):
    B, S, D = x.shape
    assert B == 1 and D == D_MODEL and S % KEY_TILE == 0 and w_in.shape[0] == 1
    G, R = NSA_KV_GROUPS, NSA_GROUP_SIZE
    x2 = x.reshape(S, D)

    w0 = w_in[0]
    w_nat = jnp.concatenate([w0[:, 512:768], w0[:, 768:896], w0[:, 1024:1152], w0[:, 1816:2328], w0[:, 3136:5184],
                             w0[:, 1280:1304], w0[:, 3096:3136], jnp.zeros((D, LANES - 64), F32)], axis=1).astype(BF16)
    w_tr = jnp.concatenate([w0[:, 0:512], w0[:, 1304:1816], w0[:, 2328:2840], w0[:, 2840:3096],
                            w0[:, 896:1024], w0[:, 1152:1280]], axis=1).T.astype(BF16)
    two = lambda g: jnp.tile(g.reshape(1, HEAD_DIM), (1, 2))
    qscale = HEAD_DIM ** -0.5 * LOG2E
    gains = jnp.concatenate([two(g_k_nsa[0]), two(g_k_dsa[0])], axis=0)
    gcol = jnp.stack([g_q_nsa[0] * qscale, g_q_dsa[0] * qscale]).reshape(2, HEAD_DIM, 1)
    (cmpraw, kslc, kwin, dk, gab, small, nqt, dqt, dvt, iqt, vst2, vwt2) = _inproj(
        x2, g_mix[0].reshape(1, D), w_nat, w_tr, gains, gcol)

    NS = S // NSA_SLC_BLOCK
    xc = cmpraw.reshape(2 * G, NS, 4 * NSA_CMP_STRIDE * HEAD_DIM)
    half = NSA_CMP_STRIDE * HEAD_DIM
    w1 = jnp.stack([w_ck1[0], w_cv1[0]]).reshape(2, 2 * half, HEAD_DIM)
    pe = jnp.stack([pe_ck[0], pe_cv[0]]).reshape(2, 2 * half, 1)
    w2 = jnp.stack([w_ck2[0], w_cv2[0]]).astype(BF16)
    kvc = _compress(xc, w1[:, :half].astype(BF16), w1[:, half:].astype(BF16), w1, pe, w2,
                    g_k_nsa[0].reshape(1, HEAD_DIM))

    rel_nsa = rel_bias[:, :NSA_HEADS]
    rel_dsa = rel_bias[:, NSA_HEADS:]
    oct, selt = _cmp(nqt, kvc[:G], kvc[G:].transpose(0, 2, 1), _cmp_bias_t(rel_nsa), S)
    kgroups = lambda a: jnp.pad(a, ((0, 0), (NSA_PADF, TAIL_PAD), (0, 0)))
    vgroups = lambda a: jnp.pad(a.reshape(G, HEAD_DIM, S), ((0, 0), (0, 0), (NSA_PADF, TAIL_PAD)))
    glt = small[:, SMALL_GATE:SMALL_GATE + 3 * NSA_HEADS].reshape(S, G, 3 * R).transpose(1, 2, 0)
    glt = jnp.pad(glt, ((0, 0), (0, 16 - 3 * R), (0, 0)))
    blk_lane = (np.arange(S) // NSA_SLC_BLOCK) % LANES
    onehot = jnp.asarray(np.eye(LANES, dtype=np.float32)[blk_lane] * MASK_BIG, BF16)
    kaug = jnp.concatenate([kslc,
                            jnp.broadcast_to(onehot, (G, S, LANES)),
                            jnp.zeros((G, S, KAUG_W - HEAD_DIM - LANES), BF16)], axis=-1)
    kaug = jnp.pad(kaug, ((0, 0), (NSA_PADF, TAIL_PAD), (0, 0)))
    o_at = _slcwin(nqt, kaug, vgroups(vst2), kgroups(kwin), vgroups(vwt2), selt,
                   _near_bias_t(rel_nsa), oct, glt, S)

    dkp = jnp.pad(dk, ((DSA_PADF, TAIL_PAD), (0, 0)))
    dvtp = jnp.pad(dvt, ((0, 0), (DSA_PADF, TAIL_PAD)))
    ikt = jnp.pad(small[:, SMALL_IK:SMALL_IK + IDX_DIM].T.astype(BF16), ((0, 0), (0, 2 * KEY_TILE)))
    iwt = small[:, SMALL_IW:SMALL_IW + IDX_HEADS].T
    ltri = jnp.asarray(np.tril(np.ones((KEY_TILE, KEY_TILE), np.float32), -1), BF16)
    o_bt = _dsa(dqt, dkp, dvtp, iqt, ikt, iwt, _near_bias_t(rel_dsa), ltri, S)

    wr = jnp.pad(w_router[0], ((0, 0), (0, LANES - N_EXPERTS)))
    br = jnp.pad(b_router[0].reshape(1, N_EXPERTS), ((0, 0), (0, LANES - N_EXPERTS)), constant_values=-1e30)
    x1, h2, ridx, rw = _merge(x2, o_at, o_bt, gab, w_branch_a[0].astype(BF16), w_branch_b[0].astype(BF16),
                              w_out[0].astype(BF16), g_ffn[0].reshape(1, D), wr, br)

    rank, cnt = _rank(ridx)
    counts = cnt[0, :N_EXPERTS]
    padded = (counts + MOE_ROWS - 1) // MOE_ROWS * MOE_ROWS
    pad_end = jnp.cumsum(padded)
    pad_start = pad_end - padded
    eidx = ridx[:, :TOP_K]
    dest = (jnp.sum(jnp.where(eidx[..., None] == jnp.arange(N_EXPERTS), pad_start, 0), axis=-1)
            + rank[:, :TOP_K]).reshape(-1).astype(I32)
    n_chunks = -(-(S * TOP_K) // MOE_ROWS) + N_EXPERTS
    chunk_start = jnp.arange(n_chunks, dtype=I32) * MOE_ROWS
    chunk_e = jnp.minimum(jnp.sum(chunk_start[:, None] >= pad_end[None, :], axis=-1), N_EXPERTS - 1).astype(I32)
    n_used = (pad_end[-1] // MOE_ROWS).astype(I32).reshape(1)
    zrow = jnp.where(padded > 0, pad_end - MOE_ROWS, -1).astype(I32)
    xe = _dispatch(dest, zrow, n_used, h2, n_chunks * MOE_ROWS)
    yexp = _experts(chunk_e, n_used, xe, w_e1[0], b_e1[0].reshape(N_EXPERTS, 1, 2 * D_FF),
                    w_e2[0], b_e2[0].reshape(N_EXPERTS, 1, D))

    out = _final(dest, x1, rw, p[0].reshape(S, PLE_DIM), g_ple[0].reshape(1, D),
                 w_ple_gate[0].astype(BF16), w_ple[0].astype(BF16), yexp)
    return out.reshape(B, S, D)
```
